```python
import math
import jax, jax.numpy as jnp
from jax import lax
import numpy as np

D_MODEL = 1024
BATCH = 4
SEQ = 4096
DEPTH = 4

GM_CHUNK = 128
GM_GROUPS = 4
GM_WIDTH = 256
GM_GROUP_DIM = GM_WIDTH // GM_GROUPS
RW_HEADS = 4
RW_HEAD_DIM = 64
RW_WIDTH = RW_HEADS * RW_HEAD_DIM
RW_DECAY_LORA = 32
RW_ICLR_LORA = 32
RW_GATE_LORA = 64
RW_SHIFT_WIDTH = 3 * RW_WIDTH + RW_DECAY_LORA + RW_ICLR_LORA + RW_GATE_LORA
RW_GN_EPS = 64e-5
FOX_HEADS = 8
FOX_HEAD_DIM = 64
FOX_WIDTH = FOX_HEADS * FOX_HEAD_DIM
FOX_BLOCK = 128
ATTN_SCALE = FOX_HEAD_DIM ** -0.5
MASK_VALUE = -1e30
N_BRANCH = 3
MIX_WIDTH = GM_WIDTH + RW_WIDTH + FOX_WIDTH
IN_WIDTH = 2 * GM_WIDTH + RW_SHIFT_WIDTH + 3 * FOX_WIDTH + FOX_HEADS + N_BRANCH * D_MODEL
N_EXPERTS = 32
TOP_K = 4
D_FF = D_MODEL
SWIGLU_LIMIT = 7.0
SWIGLU_ALPHA = 1.702
MOE_BLOCK = 256
EPS = 1e-6

kernel_name = "hybrid_gmlp_rwkv7_fox_moe_adaln"


def _split(z, sizes):
    idx = np.cumsum(sizes)[:-1].tolist()
    return jnp.split(z, idx, axis=-1)


def rms_norm(x):
    xf = x.astype(jnp.float32)
    return (xf * lax.rsqrt(jnp.mean(xf * xf, axis=-1, keepdims=True) + EPS)).astype(x.dtype)


def token_shift(z):
    return jnp.pad(z, ((0, 0), (1, 0), (0, 0)))[:, :-1]


def spatial_gating_mixer(u, v, v_gain, w_s, b_s):
    bsz, t_len, _ = u.shape
    u = jax.nn.gelu(u)
    v = rms_norm(jax.nn.gelu(v)) * v_gain
    n_chunk = t_len // GM_CHUNK
    vc = v.reshape(bsz, n_chunk, GM_CHUNK, GM_GROUPS, GM_GROUP_DIM)
    w_causal = jnp.tril(w_s)
    mixed = jnp.einsum('gts,bnsgc->bntgc', w_causal, vc) + b_s.T[None, None, :, :, None]
    return u * mixed.reshape(bsz, t_len, GM_WIDTH)


def rwkv7_mixer(r, k, v, w_lo, a_lo, g_lo, w0, w2, a0, a2, g2, k_k, k_a, r_k, gn_gain, gn_bias):
    bsz, t_len, _ = r.shape
    f32 = jnp.float32
    heads = lambda z: z.astype(f32).reshape(bsz, t_len, RW_HEADS, RW_HEAD_DIM)
    w_log = -jax.nn.softplus(-(w0 + jnp.tanh(w_lo) @ w2)) - 0.5
    decay = jnp.exp(-jnp.exp(w_log.astype(f32)))
    a = jax.nn.sigmoid(a0 + a_lo @ a2)
    g = jax.nn.sigmoid(g_lo) @ g2
    kk = heads(k * k_k)
    kk = kk / jnp.maximum(jnp.linalg.norm(kk, axis=-1, keepdims=True), 1e-12)
    k = k * (1 + (a - 1) * k_a)
    rh, kh, vh, ah, dh = heads(r), heads(k), heads(v), heads(a), heads(decay)
    a_vec = -kk
    b_vec = kk * ah

    def step(state, inp):
        r_t, w_t, k_t, v_t, a_t, b_t = inp
        sa = jnp.einsum('bhij,bhj->bhi', state, a_t)
        state = (state * w_t[:, :, None, :] + sa[..., None] * b_t[:, :, None, :]
                 + v_t[..., None] * k_t[:, :, None, :])
        return state, jnp.einsum('bhij,bhj->bhi', state, r_t)

    xs = tuple(jnp.moveaxis(z, 1, 0) for z in (rh, dh, kh, vh, a_vec, b_vec))
    state0 = jnp.zeros((bsz, RW_HEADS, RW_HEAD_DIM, RW_HEAD_DIM), f32)
    _, y = lax.scan(step, state0, xs)
    y = jnp.moveaxis(y, 0, 1)
    mu = jnp.mean(y, axis=-1, keepdims=True)
    var = jnp.mean(jnp.square(y - mu), axis=-1, keepdims=True)
    yn = ((y - mu) * lax.rsqrt(var + RW_GN_EPS)).reshape(bsz, t_len, RW_WIDTH) * gn_gain + gn_bias
    bonus = (jnp.sum(rh * kh * r_k, axis=-1, keepdims=True) * vh).reshape(bsz, t_len, RW_WIDTH)
    return ((yn + bonus) * g).astype(r.dtype)


def forgetting_attention(q, k, v, f_logit, f_bias, q_gain, k_gain):
    bsz, t_len, _ = q.shape
    hd = lambda z: z.reshape(bsz, t_len, FOX_HEADS, FOX_HEAD_DIM)
    q = rms_norm(hd(q)) * q_gain
    k = rms_norm(hd(k)) * k_gain
    v = hd(v)
    log_f = jax.nn.log_sigmoid((f_logit + f_bias).astype(jnp.float32))
    cum = jnp.transpose(jnp.cumsum(log_f, axis=1), (0, 2, 1))
    n_blk = t_len // FOX_BLOCK
    q_blocks = q.reshape(bsz, n_blk, FOX_BLOCK, FOX_HEADS, FOX_HEAD_DIM).transpose(1, 0, 2, 3, 4)
    c_blocks = cum.reshape(bsz, FOX_HEADS, n_blk, FOX_BLOCK).transpose(2, 0, 1, 3)
    key_pos = jnp.arange(t_len)

    def attend_block(args):
        qb, cb, bi = args
        s = jnp.einsum('bqhd,bkhd->bhqk', qb, k, preferred_element_type=jnp.float32) * ATTN_SCALE
        s = s + cb[..., :, None] - cum[:, :, None, :]
        q_pos = bi * FOX_BLOCK + jnp.arange(FOX_BLOCK)
        s = jnp.where(key_pos[None, :] <= q_pos[:, None], s, MASK_VALUE)
        p = jax.nn.softmax(s, axis=-1).astype(v.dtype)
        return jnp.einsum('bhqk,bkhd->bqhd', p, v)

    o = lax.map(attend_block, (q_blocks, c_blocks, jnp.arange(n_blk)))
    return o.transpose(1, 0, 2, 3, 4).reshape(bsz, t_len, FOX_WIDTH)


def moe_ffn(h, w_router, b_router, w_gate_up, b_gate_up, w_down, b_down):
    bsz, t_len, d = h.shape
    n_tok = bsz * t_len
    hf = h.reshape(n_tok, d)
    logits = (hf @ w_router + b_router).astype(jnp.float32)
    top_val, top_idx = lax.top_k(logits, TOP_K)
    gate = jax.nn.softmax(top_val, axis=-1)
    n_assign = n_tok * TOP_K
    n_blocks = -(-n_assign // MOE_BLOCK) + N_EXPERTS
    n_rows = n_blocks * MOE_BLOCK
    e_flat = top_idx.reshape(-1).astype(jnp.int32)
    tok_flat = jnp.arange(n_assign, dtype=jnp.int32) // TOP_K
    order = jnp.argsort(e_flat)
    e_sorted = e_flat[order]
    counts = jnp.bincount(e_flat, length=N_EXPERTS).astype(jnp.int32)
    padded = (counts + MOE_BLOCK - 1) // MOE_BLOCK * MOE_BLOCK
    start = jnp.cumsum(counts) - counts
    pad_end = jnp.cumsum(padded)
    pad_start = pad_end - padded
    dest = pad_start[e_sorted] + jnp.arange(n_assign, dtype=jnp.int32) - start[e_sorted]
    row_tok = jnp.zeros((n_rows,), jnp.int32).at[dest].set(tok_flat[order])
    row_w = jnp.zeros((n_rows,), jnp.float32).at[dest].set(gate.reshape(-1)[order])
    block_start = jnp.arange(n_blocks, dtype=jnp.int32) * MOE_BLOCK
    block_expert = jnp.minimum(jnp.searchsorted(pad_end, block_start, side='right'), N_EXPERTS - 1)
    xin = hf[row_tok].reshape(n_blocks, MOE_BLOCK, d)

    def expert_block(args):
        xb, e = args
        gu = xb @ w_gate_up[e] + b_gate_up[e]
        g_, u_ = jnp.split(gu, 2, axis=-1)
        g_ = jnp.minimum(g_, SWIGLU_LIMIT)
        u_ = jnp.clip(u_, -SWIGLU_LIMIT, SWIGLU_LIMIT)
        act = (u_ + 1) * (g_ * jax.nn.sigmoid(SWIGLU_ALPHA * g_))
        return act @ w_down[e] + b_down[e]

    yb = lax.map(expert_block, (xin, block_expert)).reshape(n_rows, d)
    y = jax.ops.segment_sum(yb * row_w[:, None].astype(yb.dtype), row_tok, num_segments=n_tok)
    return y.reshape(bsz, t_len, d)


def hybrid_layer(x, c, w_ada, b_ada, w_in, gm_v_gain, gm_w_s, gm_b_s, rw_mu, rw_w0, rw_w2, rw_a0, rw_a2,
                 rw_g2, rw_k_k, rw_k_a, rw_r_k, rw_gn_gain, rw_gn_bias, fox_f_bias, fox_q_gain, fox_k_gain,
                 w_branch, w_o, w_router, b_router, w_gate_up, b_gate_up, w_down, b_down):
    mod = jax.nn.silu(c) @ w_ada + b_ada
    shift1, scale1, gate1, shift2, scale2, gate2 = jnp.split(mod[:, None, :], 6, axis=-1)
    h = rms_norm(x) * (1 + scale1) + shift1
    z = h @ w_in
    z_gm, z_rw, z_fox, z_gate = _split(z, (2 * GM_WIDTH, RW_SHIFT_WIDTH, 3 * FOX_WIDTH + FOX_HEADS,
                                             N_BRANCH * D_MODEL))
    u, v = _split(z_gm, (GM_WIDTH, GM_WIDTH))
    y_gm = spatial_gating_mixer(u, v, gm_v_gain, gm_w_s, gm_b_s)
    z_rw = z_rw + rw_mu * (token_shift(z_rw) - z_rw)
    r, k, vr, w_lo, a_lo, g_lo = _split(z_rw, (RW_WIDTH, RW_WIDTH, RW_WIDTH, RW_DECAY_LORA, RW_ICLR_LORA,
                                              RW_GATE_LORA))
    y_rw = rwkv7_mixer(r, k, vr, w_lo, a_lo, g_lo, rw_w0, rw_w2, rw_a0, rw_a2, rw_g2, rw_k_k, rw_k_a, rw_r_k,
                       rw_gn_gain, rw_gn_bias)
    q, kf, vf, f_logit = _split(z_fox, (FOX_WIDTH, FOX_WIDTH, FOX_WIDTH, FOX_HEADS))
    y_fox = forgetting_attention(q, kf, vf, f_logit, fox_f_bias, fox_q_gain, fox_k_gain)
    g_gm, g_rw, g_fox = _split(jax.nn.sigmoid(z_gate), (D_MODEL, D_MODEL, D_MODEL))
    p_gm, p_rw, p_fox = jnp.split(w_branch, [GM_WIDTH, GM_WIDTH + RW_WIDTH], axis=0)
    merged = g_gm * (y_gm @ p_gm) + g_rw * (y_rw @ p_rw) + g_fox * (y_fox @ p_fox)
    x = x + gate1 * (merged @ w_o)
    h2 = rms_norm(x) * (1 + scale2) + shift2
    x = x + gate2 * moe_ffn(h2, w_router, b_router, w_gate_up, b_gate_up, w_down, b_down)
    return x


def setup_inputs(seed: int = 0) -> dict:
    key = jax.random.key(seed)
    ks = iter(jax.random.split(key, 48))
    L = DEPTH
    f32 = jnp.float32
    nrm = lambda shape, s: s * jax.random.normal(next(ks), shape, f32)
    uni = lambda shape, lo, hi: jax.random.uniform(next(ks), shape, f32, lo, hi)
    inp = {}
    inp['x'] = nrm((BATCH, SEQ, D_MODEL), 1.0)
    inp['c'] = nrm((BATCH, D_MODEL), 1.0)
    inp['w_ada'] = nrm((L, D_MODEL, 6 * D_MODEL), 0.5 * D_MODEL ** -0.5)
    inp['b_ada'] = nrm((L, 6 * D_MODEL), 0.02)
    inp['w_in'] = nrm((L, D_MODEL, IN_WIDTH), D_MODEL ** -0.5)
    inp['gm_v_gain'] = 1.0 + nrm((L, GM_WIDTH), 0.02)
    inp['gm_w_s'] = nrm((L, GM_GROUPS, GM_CHUNK, GM_CHUNK), GM_CHUNK ** -0.5)
    inp['gm_b_s'] = 1.0 + nrm((L, GM_GROUPS, GM_CHUNK), 0.1)
    inp['rw_mu'] = uni((L, RW_SHIFT_WIDTH), 0.0, 1.0)
    inp['rw_w0'] = uni((L, RW_WIDTH), -6.0, -1.0)
    inp['rw_w2'] = nrm((L, RW_DECAY_LORA, RW_WIDTH), 0.1)
    inp['rw_a0'] = nrm((L, RW_WIDTH), 0.1)
    inp['rw_a2'] = nrm((L, RW_ICLR_LORA, RW_WIDTH), 0.5 * RW_ICLR_LORA ** -0.5)
    inp['rw_g2'] = nrm((L, RW_GATE_LORA, RW_WIDTH), RW_GATE_LORA ** -0.5)
    inp['rw_k_k'] = 0.85 + nrm((L, RW_WIDTH), 0.02)
    inp['rw_k_a'] = 1.0 + nrm((L, RW_WIDTH), 0.02)
    inp['rw_r_k'] = nrm((L, RW_HEADS, RW_HEAD_DIM), 0.1)
    inp['rw_gn_gain'] = 1.0 + nrm((L, RW_WIDTH), 0.02)
    inp['rw_gn_bias'] = nrm((L, RW_WIDTH), 0.02)
    inp['fox_f_bias'] = uni((L, FOX_HEADS), 2.0, 6.0)
    inp['fox_q_gain'] = 1.0 + nrm((L, FOX_HEAD_DIM), 0.02)
    inp['fox_k_gain'] = 1.0 + nrm((L, FOX_HEAD_DIM), 0.02)
    inp['w_branch'] = jnp.concatenate([nrm((L, GM_WIDTH, D_MODEL), GM_WIDTH ** -0.5),
                                       nrm((L, RW_WIDTH, D_MODEL), RW_WIDTH ** -0.5),
                                       nrm((L, FOX_WIDTH, D_MODEL), FOX_WIDTH ** -0.5)], axis=1)
    inp['w_o'] = nrm((L, D_MODEL, D_MODEL), D_MODEL ** -0.5)
    inp['w_router'] = nrm((L, D_MODEL, N_EXPERTS), D_MODEL ** -0.5)
    inp['b_router'] = nrm((L, N_EXPERTS), 0.01)
    inp['w_gate_up'] = nrm((L, N_EXPERTS, D_MODEL, 2 * D_FF), D_MODEL ** -0.5)
    inp['b_gate_up'] = nrm((L, N_EXPERTS, 2 * D_FF), 0.01)
    inp['w_down'] = nrm((L, N_EXPERTS, D_FF, D_MODEL), D_FF ** -0.5)
    inp['b_down'] = nrm((L, N_EXPERTS, D_MODEL), 0.01)
    return inp


def reference(x, c, w_ada, b_ada, w_in, gm_v_gain, gm_w_s, gm_b_s, rw_mu, rw_w0, rw_w2, rw_a0, rw_a2, rw_g2,
              rw_k_k, rw_k_a, rw_r_k, rw_gn_gain, rw_gn_bias, fox_f_bias, fox_q_gain, fox_k_gain, w_branch, w_o,
              w_router, b_router, w_gate_up, b_gate_up, w_down, b_down):
    for l in range(DEPTH):
        x = hybrid_layer(x, c, w_ada[l], b_ada[l], w_in[l], gm_v_gain[l], gm_w_s[l], gm_b_s[l], rw_mu[l],
                         rw_w0[l], rw_w2[l], rw_a0[l], rw_a2[l], rw_g2[l], rw_k_k[l], rw_k_a[l], rw_r_k[l],
                         rw_gn_gain[l], rw_gn_bias[l], fox_f_bias[l], fox_q_gain[l], fox_k_gain[l], w_branch[l],
                         w_o[l], w_router[l], b_router[l], w_gate_up[l], b_gate_up[l], w_down[l], b_down[l])
    return x
```

```python
import functools

import jax
import jax.numpy as jnp
from jax import lax
from jax.experimental import pallas as pl
from jax.experimental.pallas import tpu as pltpu

F32 = jnp.float32
BF16 = jnp.bfloat16
I32 = jnp.int32
HIGHEST = lax.Precision.HIGHEST

D_MODEL = 1024
GM_CHUNK = 128
GM_GROUPS = 4
GM_WIDTH = 256
GM_GROUP_DIM = GM_WIDTH // GM_GROUPS
RW_HEADS = 4
RW_HEAD_DIM = 64
RW_WIDTH = RW_HEADS * RW_HEAD_DIM
RW_DECAY_LORA = 32
RW_ICLR_LORA = 32
RW_GATE_LORA = 64
RW_LORA = RW_DECAY_LORA + RW_ICLR_LORA + RW_GATE_LORA
RW_SHIFT_WIDTH = 3 * RW_WIDTH + RW_LORA
RW_GN_EPS = 64e-5
FOX_HEADS = 8
FOX_HEAD_DIM = 64
FOX_WIDTH = FOX_HEADS * FOX_HEAD_DIM
ATTN_SCALE = FOX_HEAD_DIM ** -0.5
MASK_VALUE = -1e30
N_BRANCH = 3
N_EXPERTS = 32
TOP_K = 4
D_FF = D_MODEL
SWIGLU_LIMIT = 7.0
SWIGLU_ALPHA = 1.702
MOE_BLOCK = 256
EPS = 1e-6

Z_GATE = 0
Z_FOX = N_BRANCH * D_MODEL
Z_GM = Z_FOX + 3 * FOX_WIDTH
Z_RW = Z_GM + 2 * GM_WIDTH
RW_BLOCK = 1024
Z_F = Z_RW + RW_SHIFT_WIDTH
Z_WIDTH = Z_RW + RW_BLOCK
LANES = 128
RW_CHUNK = 64

VMEM_LIMIT = 48 * 1024 * 1024


def _cparams(sem):
    return pltpu.CompilerParams(dimension_semantics=sem, vmem_limit_bytes=VMEM_LIMIT)


def _mm(a, b):
    return jnp.dot(a.astype(BF16), b.astype(BF16), preferred_element_type=F32)


def _mm_nt(a, b):
    return lax.dot_general(a.astype(BF16), b.astype(BF16), (((1,), (1,)), ((), ())), preferred_element_type=F32)


def _mm_tn(a, b):
    return lax.dot_general(a.astype(BF16), b.astype(BF16), (((0,), (0,)), ((), ())), preferred_element_type=F32)


def _split3(x):
    hi = x.astype(BF16)
    r1 = x - hi.astype(F32)
    mid = r1.astype(BF16)
    lo = (r1 - mid.astype(F32)).astype(BF16)
    return hi, mid, lo


def _tri_cumsum(x, n):
    ri = lax.broadcasted_iota(I32, (n, n), 0)
    ci = lax.broadcasted_iota(I32, (n, n), 1)
    ones = jnp.where(ri >= ci, 1.0, 0.0).astype(BF16)
    hi, mid, lo = _split3(x)
    return (jnp.dot(ones, hi, preferred_element_type=F32) + jnp.dot(ones, mid, preferred_element_type=F32)
            + jnp.dot(ones, lo, preferred_element_type=F32))


def _log_sigmoid(x):
    return jnp.minimum(x, 0.0) - jnp.log1p(jnp.exp(-jnp.abs(x)))


def _adaln_kernel(c_ref, w_ref, b_ref, o_ref):
    c = c_ref[...]
    s = c * jax.nn.sigmoid(c)
    o_ref[0] = jnp.dot(s, w_ref[0], preferred_element_type=F32, precision=HIGHEST) + b_ref[0]


def _adaln(c_pad, w_ada, b_ada):
    n_layer, d, w6 = w_ada.shape
    tn = 1536
    return pl.pallas_call(
        _adaln_kernel,
        out_shape=jax.ShapeDtypeStruct((n_layer, c_pad.shape[0], w6), F32),
        grid=(n_layer, w6 // tn),
        in_specs=[pl.BlockSpec(c_pad.shape, lambda l, j: (0, 0)),
                  pl.BlockSpec((1, d, tn), lambda l, j: (l, 0, j)),
                  pl.BlockSpec((1, 1, tn), lambda l, j: (l, 0, j))],
        out_specs=pl.BlockSpec((1, c_pad.shape[0], tn), lambda l, j: (l, 0, j)),
        compiler_params=_cparams(("parallel", "parallel")),
        name="adaln",
    )(c_pad, w_ada, b_ada.reshape(n_layer, 1, w6))


def _inproj_kernel(x_ref, sc_ref, sh_ref, w_ref, o_ref, xn_ref):
    @pl.when(pl.program_id(2) == 0)
    def _():
        x = x_ref[0]
        xn = x * lax.rsqrt(jnp.mean(x * x, axis=-1, keepdims=True) + EPS)
        xn_ref[...] = (xn * (1.0 + sc_ref[0]) + sh_ref[0]).astype(BF16)

    o_ref[0] = jnp.dot(xn_ref[...], w_ref[...], preferred_element_type=F32)


def _inproj(x, scale, shift, w):
    bsz, t_len, d = x.shape
    tm = min(1024, t_len)
    tn = 1024
    return pl.pallas_call(
        _inproj_kernel,
        out_shape=jax.ShapeDtypeStruct((bsz, t_len, Z_WIDTH), F32),
        grid=(bsz, t_len // tm, Z_WIDTH // tn),
        in_specs=[pl.BlockSpec((1, tm, d), lambda b, i, j: (b, i, 0)),
                  pl.BlockSpec((1, 1, d), lambda b, i, j: (b, 0, 0)),
                  pl.BlockSpec((1, 1, d), lambda b, i, j: (b, 0, 0)),
                  pl.BlockSpec((d, tn), lambda b, i, j: (0, j))],
        out_specs=pl.BlockSpec((1, tm, tn), lambda b, i, j: (b, i, j)),
        scratch_shapes=[pltpu.VMEM((tm, d), BF16)],
        compiler_params=_cparams(("parallel", "parallel", "arbitrary")),
        name="inproj",
    )(x, scale, shift, w)


def _gmlp_kernel(z_ref, gain_ref, ws_ref, bst_ref, o_ref):
    tm = z_ref.shape[1]
    z = z_ref[0]
    u = jax.nn.gelu(z[:, :GM_WIDTH])
    v = jax.nn.gelu(z[:, GM_WIDTH:])
    v = v * lax.rsqrt(jnp.mean(v * v, axis=-1, keepdims=True) + EPS) * gain_ref[...]
    vb = v.astype(BF16)
    grp = lax.broadcasted_iota(I32, (GM_CHUNK, GM_WIDTH), 1) // GM_GROUP_DIM
    ri = lax.broadcasted_iota(I32, (GM_CHUNK, GM_CHUNK), 0)
    ci = lax.broadcasted_iota(I32, (GM_CHUNK, GM_CHUNK), 1)
    causal = ri >= ci
    bias = jnp.zeros((GM_CHUNK, GM_WIDTH), F32)
    ws = []
    for g in range(GM_GROUPS):
        ws.append(jnp.where(causal, ws_ref[g], 0.0).astype(BF16))
        bias = jnp.where(grp == g, bst_ref[:, g:g + 1], bias)
    for c in range(tm // GM_CHUNK):
        rows = slice(c * GM_CHUNK, (c + 1) * GM_CHUNK)
        vc = vb[rows]
        mixed = bias
        for g in range(GM_GROUPS):
            m = jnp.dot(ws[g], vc, preferred_element_type=F32)
            mixed = mixed + jnp.where(grp == g, m, 0.0)
        o_ref[0, rows, :] = (u[rows] * mixed).astype(o_ref.dtype)


def _gmlp(z, gain, w_s, b_s):
    bsz, t_len, _ = z.shape
    tm = min(512, t_len)
    return pl.pallas_call(
        _gmlp_kernel,
        out_shape=jax.ShapeDtypeStruct((bsz, t_len, GM_WIDTH), BF16),
        grid=(bsz, t_len // tm),
        in_specs=[pl.BlockSpec((1, tm, 2 * GM_WIDTH), lambda b, i: (b, i, Z_GM // (2 * GM_WIDTH))),
                  pl.BlockSpec((1, GM_WIDTH), lambda b, i: (0, 0)),
                  pl.BlockSpec((GM_GROUPS, GM_CHUNK, GM_CHUNK), lambda b, i: (0, 0, 0)),
                  pl.BlockSpec((GM_CHUNK, GM_GROUPS), lambda b, i: (0, 0))],
        out_specs=pl.BlockSpec((1, tm, GM_WIDTH), lambda b, i: (b, i, 0)),
        compiler_params=_cparams(("parallel", "parallel")),
        name="gmlp",
    )(z, gain.reshape(1, GM_WIDTH), w_s, b_s.T)


def _rwprep_kernel(z_ref, zp_ref, mu_ref, wl_ref, w0_ref, a0_ref, kk_ref, ka_ref,
                   r_o, lw_o, k_o, v_o, a_o, b_o, g_o):
    tm = z_ref.shape[1]
    z = z_ref[0]
    prev = jnp.where(pl.program_id(1) > 0, zp_ref[0, 7:8, :], 0.0)
    rowid = lax.broadcasted_iota(I32, z.shape, 0)
    zs = jnp.where(rowid == 0, prev, pltpu.roll(z, 1, axis=0))
    zz = z + mu_ref[...] * (zs - z)
    r = zz[:, 0:RW_WIDTH]
    k = zz[:, RW_WIDTH:2 * RW_WIDTH]
    v = zz[:, 2 * RW_WIDTH:3 * RW_WIDTH]
    lo = zz[:, 3 * RW_WIDTH:3 * RW_WIDTH + RW_LORA]
    lane = lax.broadcasted_iota(I32, (tm, RW_LORA), 1)
    act = jnp.where(lane < RW_DECAY_LORA, jnp.tanh(lo),
                    jnp.where(lane < RW_DECAY_LORA + RW_ICLR_LORA, lo, jax.nn.sigmoid(lo)))
    proj = jnp.dot(act, wl_ref[...], preferred_element_type=F32, precision=HIGHEST)
    xw = -(w0_ref[...] + proj[:, 0:RW_WIDTH])
    softplus = jnp.maximum(xw, 0.0) + jnp.log1p(jnp.exp(-jnp.abs(xw)))
    lw = -jnp.exp(-softplus - 0.5)
    a = jax.nn.sigmoid(a0_ref[...] + proj[:, RW_WIDTH:2 * RW_WIDTH])
    g = proj[:, 2 * RW_WIDTH:3 * RW_WIDTH]
    kk = k * kk_ref[...]
    k2 = k * (1.0 + (a - 1.0) * ka_ref[...])
    for h in range(RW_HEADS):
        sl = slice(h * RW_HEAD_DIM, (h + 1) * RW_HEAD_DIM)
        kkh = kk[:, sl]
        nrm = jnp.sqrt(jnp.sum(kkh * kkh, axis=-1, keepdims=True))
        kkh = kkh / jnp.maximum(nrm, 1e-12)
        r_o[0, h] = r[:, sl]
        lw_o[0, h] = lw[:, sl]
        k_o[0, h] = k2[:, sl]
        v_o[0, h] = v[:, sl]
        a_o[0, h] = -kkh
        b_o[0, h] = kkh * a[:, sl]
        g_o[0, h] = g[:, sl]


def _rwprep(z, mu_pad, w_lora, w0, a0, k_k, k_a):
    bsz, t_len, _ = z.shape
    tm = min(512, t_len)
    hm = jax.ShapeDtypeStruct((bsz, RW_HEADS, t_len, RW_HEAD_DIM), F32)
    hm_spec = pl.BlockSpec((1, RW_HEADS, tm, RW_HEAD_DIM), lambda b, i: (b, 0, i, 0))
    vec = lambda n: pl.BlockSpec((1, n), lambda b, i: (0, 0))
    rw_blk = Z_RW // RW_BLOCK
    return pl.pallas_call(
        _rwprep_kernel,
        out_shape=(hm,) * 7,
        grid=(bsz, t_len // tm),
        in_specs=[pl.BlockSpec((1, tm, RW_BLOCK), lambda b, i: (b, i, rw_blk)),
                  pl.BlockSpec((1, 8, RW_BLOCK), lambda b, i: (b, jnp.maximum(i * (tm // 8) - 1, 0), rw_blk)),
                  vec(RW_BLOCK),
                  pl.BlockSpec((RW_LORA, 3 * RW_WIDTH), lambda b, i: (0, 0)),
                  vec(RW_WIDTH), vec(RW_WIDTH), vec(RW_WIDTH), vec(RW_WIDTH)],
        out_specs=(hm_spec,) * 7,
        compiler_params=_cparams(("parallel", "parallel")),
        name="rwprep",
    )(z, z, mu_pad, w_lora, w0.reshape(1, -1), a0.reshape(1, -1), k_k.reshape(1, -1), k_a.reshape(1, -1))


def _rwscan_kernel(r_ref, lw_ref, k_ref, v_ref, a_ref, b_ref, g_ref, rk_ref, gg_ref, gb_ref, o_ref, s_ref):
    cl = RW_CHUNK
    tb = r_ref.shape[2]

    @pl.when(pl.program_id(1) == 0)
    def _():
        s_ref[...] = jnp.zeros_like(s_ref)

    ri = lax.broadcasted_iota(I32, (cl, cl), 0)
    ci = lax.broadcasted_iota(I32, (cl, cl), 1)
    lower = ri >= ci
    strict = ri > ci
    eye = jnp.where(ri == ci, 1.0, 0.0)

    def chunk(c, carry):
        t0 = pl.multiple_of(c * cl, cl)
        rows = pl.ds(t0, cl)
        for h in range(RW_HEADS):
            r = r_ref[0, h, rows, :]
            lw = lw_ref[0, h, rows, :]
            k = k_ref[0, h, rows, :]
            v = v_ref[0, h, rows, :]
            a = a_ref[0, h, rows, :]
            b = b_ref[0, h, rows, :]
            cw = _tri_cumsum(lw, cl)
            w_in = jnp.exp(cw)
            w_inv = jnp.exp(-cw)
            rt = r * w_in
            at = a * jnp.exp(cw - lw)
            kt = k * w_inv
            bt = b * w_inv
            w_end = w_in[cl - 1:cl, :]
            a_ab = jnp.where(strict, _mm_nt(at, bt), 0.0)
            a_ak = jnp.where(strict, _mm_nt(at, kt), 0.0)
            m_rb = jnp.where(lower, _mm_nt(rt, bt), 0.0)
            m_rk = jnp.where(lower, _mm_nt(rt, kt), 0.0)
            inv = eye + a_ab
            p = a_ab
            for _ in range(cl.bit_length() - 2):
                p = _mm(p, p)
                inv = inv + _mm(inv, p)
            s = s_ref[h]
            sa = _mm(inv, _mm_nt(at, s) + _mm(a_ak, v))
            y = _mm_nt(rt, s) + _mm(m_rb, sa) + _mm(m_rk, v)
            s_ref[h] = s * w_end + _mm_tn(sa, bt * w_end) + _mm_tn(v, kt * w_end)
            mu = jnp.mean(y, axis=-1, keepdims=True)
            yc = y - mu
            var = jnp.mean(yc * yc, axis=-1, keepdims=True)
            yn = yc * lax.rsqrt(var + RW_GN_EPS) * gg_ref[h] + gb_ref[h]
            bonus = jnp.sum(r * k * rk_ref[h], axis=-1, keepdims=True) * v
            o_ref[0, h, rows, :] = ((yn + bonus) * g_ref[0, h, rows, :]).astype(o_ref.dtype)
        return carry

    lax.fori_loop(0, tb // cl, chunk, 0)


def _rwscan(r, lw, k, v, a, b, g, r_k, gn_gain, gn_bias):
    bsz, _, t_len, _ = r.shape
    tb = min(512, t_len)
    hm_spec = pl.BlockSpec((1, RW_HEADS, tb, RW_HEAD_DIM), lambda bi, i: (bi, 0, i, 0))
    par = pl.BlockSpec((RW_HEADS, 1, RW_HEAD_DIM), lambda bi, i: (0, 0, 0))
    hshape = (RW_HEADS, 1, RW_HEAD_DIM)
    return pl.pallas_call(
        _rwscan_kernel,
        out_shape=jax.ShapeDtypeStruct((bsz, RW_HEADS, t_len, RW_HEAD_DIM), BF16),
        grid=(bsz, t_len // tb),
        in_specs=[hm_spec] * 7 + [par] * 3,
        out_specs=hm_spec,
        scratch_shapes=[pltpu.VMEM((RW_HEADS, RW_HEAD_DIM, RW_HEAD_DIM), F32)],
        compiler_params=_cparams(("parallel", "arbitrary")),
        name="rwscan",
    )(r, lw, k, v, a, b, g, r_k.reshape(hshape), gn_gain.reshape(hshape), gn_bias.reshape(hshape))


def _foxprep_kernel(z_ref, f_ref, fb_ref, qg_ref, kg_ref, q_o, k_o, v_o, carry_ref):
    tm = z_ref.shape[1]

    @pl.when(pl.program_id(1) == 0)
    def _():
        carry_ref[...] = jnp.zeros_like(carry_ref)

    z = z_ref[0]
    log_f = _log_sigmoid(f_ref[0] + fb_ref[...])
    cum = carry_ref[...] + _tri_cumsum(log_f, tm)
    carry_ref[...] = cum[tm - 1:tm, :]
    lane = lax.broadcasted_iota(I32, (tm, FOX_HEAD_DIM), 1)
    qg = qg_ref[...]
    kg = kg_ref[...]
    for h in range(FOX_HEADS):
        f_hi, f_mid, f_lo = (p.astype(F32) for p in _split3(cum[:, h:h + 1]))
        qx = jnp.where(lane == 0, f_hi, jnp.where(lane == 1, f_mid, jnp.where(lane == 2, f_lo,
                       jnp.where(lane < 6, 1.0, 0.0))))
        kx = jnp.where(lane < 3, 1.0, jnp.where(lane == 3, -f_hi, jnp.where(lane == 4, -f_mid,
                       jnp.where(lane == 5, -f_lo, 0.0))))
        qh = z[:, h * FOX_HEAD_DIM:(h + 1) * FOX_HEAD_DIM]
        kh = z[:, FOX_WIDTH + h * FOX_HEAD_DIM:FOX_WIDTH + (h + 1) * FOX_HEAD_DIM]
        qn = qh * lax.rsqrt(jnp.mean(qh * qh, axis=-1, keepdims=True) + EPS) * qg * ATTN_SCALE
        kn = kh * lax.rsqrt(jnp.mean(kh * kh, axis=-1, keepdims=True) + EPS) * kg
        q_o[0, h] = jnp.concatenate([qn, qx], axis=-1).astype(BF16)
        k_o[0, h] = jnp.concatenate([kn, kx], axis=-1).astype(BF16)
        v_o[0, h] = z[:, 2 * FOX_WIDTH + h * FOX_HEAD_DIM:2 * FOX_WIDTH + (h + 1) * FOX_HEAD_DIM].astype(BF16)


def _foxprep(z, f_bias_pad, q_gain, k_gain):
    bsz, t_len, _ = z.shape
    tm = min(512, t_len)
    qk = jax.ShapeDtypeStruct((bsz, FOX_HEADS, t_len, 2 * FOX_HEAD_DIM), BF16)
    vv = jax.ShapeDtypeStruct((bsz, FOX_HEADS, t_len, FOX_HEAD_DIM), BF16)
    qk_spec = pl.BlockSpec((1, FOX_HEADS, tm, 2 * FOX_HEAD_DIM), lambda b, i: (b, 0, i, 0))
    v_spec = pl.BlockSpec((1, FOX_HEADS, tm, FOX_HEAD_DIM), lambda b, i: (b, 0, i, 0))
    return pl.pallas_call(
        _foxprep_kernel,
        out_shape=(qk, qk, vv),
        grid=(bsz, t_len // tm),
        in_specs=[pl.BlockSpec((1, tm, 3 * FOX_WIDTH), lambda b, i: (b, i, Z_FOX // (3 * FOX_WIDTH))),
                  pl.BlockSpec((1, tm, LANES), lambda b, i: (b, i, Z_F // LANES)),
                  pl.BlockSpec((1, LANES), lambda b, i: (0, 0)),
                  pl.BlockSpec((1, FOX_HEAD_DIM), lambda b, i: (0, 0)),
                  pl.BlockSpec((1, FOX_HEAD_DIM), lambda b, i: (0, 0))],
        out_specs=(qk_spec, qk_spec, v_spec),
        scratch_shapes=[pltpu.VMEM((1, LANES), F32)],
        compiler_params=_cparams(("parallel", "arbitrary")),
        name="foxprep",
    )(z, z, f_bias_pad, q_gain.reshape(1, -1), k_gain.reshape(1, -1))


def _fox_kernel(q_ref, k_ref, v_ref, o_ref, m_ref, l_ref, acc_ref):
    i = pl.program_id(1)
    j = pl.program_id(2)
    tq = q_ref.shape[2]
    tk = k_ref.shape[2]

    @pl.when(j == 0)
    def _():
        m_ref[...] = jnp.full_like(m_ref, MASK_VALUE)
        l_ref[...] = jnp.zeros_like(l_ref)
        acc_ref[...] = jnp.zeros_like(acc_ref)

    def update(diagonal):
        if diagonal:
            ri = lax.broadcasted_iota(I32, (tq, tk), 0)
            ci = lax.broadcasted_iota(I32, (tq, tk), 1)
            keep = ri >= ci
        for h in range(FOX_HEADS):
            s = lax.dot_general(q_ref[0, h], k_ref[0, h], (((1,), (1,)), ((), ())), preferred_element_type=F32)
            if diagonal:
                s = jnp.where(keep, s, MASK_VALUE)
            m_prev = m_ref[h]
            m_new = jnp.maximum(m_prev, jnp.max(s, axis=-1, keepdims=True))
            alpha = jnp.exp(m_prev - m_new)
            p = jnp.exp(s - m_new)
            l_ref[h] = alpha * l_ref[h] + jnp.sum(p, axis=-1, keepdims=True)
            acc_ref[h] = alpha * acc_ref[h] + jnp.dot(p.astype(BF16), v_ref[0, h], preferred_element_type=F32)
            m_ref[h] = m_new

    @pl.when(j < i)
    def _():
        update(False)

    @pl.when(j == i)
    def _():
        update(True)
        for h in range(FOX_HEADS):
            o_ref[0, :, h * FOX_HEAD_DIM:(h + 1) * FOX_HEAD_DIM] = (acc_ref[h] / l_ref[h]).astype(o_ref.dtype)


def _fox(q, k, v):
    bsz, _, t_len, _ = q.shape
    tq = min(512, t_len)
    n_blk = t_len // tq
    return pl.pallas_call(
        _fox_kernel,
        out_shape=jax.ShapeDtypeStruct((bsz, t_len, FOX_WIDTH), BF16),
        grid=(bsz, n_blk, n_blk),
        in_specs=[pl.BlockSpec((1, FOX_HEADS, tq, 2 * FOX_HEAD_DIM), lambda b, i, j: (b, 0, i, 0)),
                  pl.BlockSpec((1, FOX_HEADS, tq, 2 * FOX_HEAD_DIM), lambda b, i, j: (b, 0, jnp.minimum(j, i), 0)),
                  pl.BlockSpec((1, FOX_HEADS, tq, FOX_HEAD_DIM), lambda b, i, j: (b, 0, jnp.minimum(j, i), 0))],
        out_specs=pl.BlockSpec((1, tq, FOX_WIDTH), lambda b, i, j: (b, i, 0)),
        scratch_shapes=[pltpu.VMEM((FOX_HEADS, tq, 1), F32), pltpu.VMEM((FOX_HEADS, tq, 1), F32),
                        pltpu.VMEM((FOX_HEADS, tq, FOX_HEAD_DIM), F32)],
        compiler_params=_cparams(("parallel", "parallel", "arbitrary")),
        name="fox",
    )(q, k, v)


def _merge_kernel(zg_ref, ygm_ref, yrw_ref, yfox_ref, x_ref, g1_ref, sc2_ref, sh2_ref, pb_ref, wo_ref, wr_ref, br_ref,
                  x1_o, h2_o, idx_o, gate_o, rank_o, cnt_o, carry_ref):
    tm = x_ref.shape[1]

    @pl.when((pl.program_id(0) == 0) & (pl.program_id(1) == 0))
    def _():
        carry_ref[...] = jnp.zeros_like(carry_ref)

    sg = jax.nn.sigmoid(zg_ref[0])
    p_gm = jnp.dot(ygm_ref[0], pb_ref[0:GM_WIDTH, :], preferred_element_type=F32)
    p_rw = jnp.zeros((tm, D_MODEL), F32)
    for h in range(RW_HEADS):
        lo = GM_WIDTH + h * RW_HEAD_DIM
        p_rw = p_rw + jnp.dot(yrw_ref[0, h], pb_ref[lo:lo + RW_HEAD_DIM, :], preferred_element_type=F32)
    p_fox = jnp.dot(yfox_ref[0], pb_ref[GM_WIDTH + RW_WIDTH:, :], preferred_element_type=F32)
    merged = sg[:, 0:D_MODEL] * p_gm + sg[:, D_MODEL:2 * D_MODEL] * p_rw + sg[:, 2 * D_MODEL:] * p_fox
    x1 = x_ref[0] + g1_ref[0] * jnp.dot(merged.astype(BF16), wo_ref[...], preferred_element_type=F32)
    x1_o[0] = x1
    h2 = x1 * lax.rsqrt(jnp.mean(x1 * x1, axis=-1, keepdims=True) + EPS) * (1.0 + sc2_ref[0]) + sh2_ref[0]
    h2_o[0] = h2.astype(BF16)

    logits = jnp.dot(h2, wr_ref[...], preferred_element_type=F32, precision=HIGHEST) + br_ref[...]
    lane = lax.broadcasted_iota(I32, (tm, N_EXPERTS), 1)
    vals, idxs = [], []
    rest = logits
    for _ in range(TOP_K):
        m = jnp.max(rest, axis=-1, keepdims=True)
        am = jnp.min(jnp.where(rest == m, lane, N_EXPERTS), axis=-1, keepdims=True)
        vals.append(m)
        idxs.append(am)
        rest = jnp.where(lane == am, -jnp.inf, rest)
    exps = [jnp.exp(val - vals[0]) for val in vals]
    denom = exps[0] + exps[1] + exps[2] + exps[3]

    onehot = jnp.zeros((tm, N_EXPERTS), F32)
    for am in idxs:
        onehot = onehot + jnp.where(lane == am, 1.0, 0.0)
    ri = lax.broadcasted_iota(I32, (tm, tm), 0)
    ci = lax.broadcasted_iota(I32, (tm, tm), 1)
    before = jnp.where(ri > ci, 1.0, 0.0).astype(BF16)
    seen = carry_ref[...] + jnp.dot(before, onehot.astype(BF16), preferred_element_type=F32)
    lane_k = lax.broadcasted_iota(I32, (tm, TOP_K), 1)
    idx_out = jnp.zeros((tm, TOP_K), I32)
    gate_out = jnp.zeros((tm, TOP_K), F32)
    rank_out = jnp.zeros((tm, TOP_K), I32)
    for kk in range(TOP_K):
        rank = jnp.sum(jnp.where(lane == idxs[kk], seen, 0.0), axis=-1, keepdims=True).astype(I32)
        idx_out = jnp.where(lane_k == kk, idxs[kk], idx_out)
        gate_out = jnp.where(lane_k == kk, exps[kk] / denom, gate_out)
        rank_out = jnp.where(lane_k == kk, rank, rank_out)
    idx_o[0] = idx_out
    gate_o[0] = gate_out
    rank_o[0] = rank_out
    total = carry_ref[...] + jnp.sum(onehot, axis=0, keepdims=True)
    carry_ref[...] = total
    cnt_o[...] = total.astype(I32)


def _merge(z, y_gm, y_rw, y_fox, x, gate1, scale2, shift2, w_branch, w_o, w_router, b_router):
    bsz, t_len, d = x.shape
    tm = min(512, t_len)
    row = lambda w: pl.BlockSpec((1, tm, w), lambda b, i: (b, i, 0))
    mod = pl.BlockSpec((1, 1, d), lambda b, i: (b, 0, 0))
    full = lambda shape: pl.BlockSpec(shape, lambda b, i: (0,) * len(shape))
    return pl.pallas_call(
        _merge_kernel,
        out_shape=(jax.ShapeDtypeStruct((bsz, t_len, d), F32), jax.ShapeDtypeStruct((bsz, t_len, d), BF16),
                   jax.ShapeDtypeStruct((bsz, t_len, TOP_K), I32), jax.ShapeDtypeStruct((bsz, t_len, TOP_K), F32),
                   jax.ShapeDtypeStruct((bsz, t_len, TOP_K), I32), jax.ShapeDtypeStruct((1, N_EXPERTS), I32)),
        grid=(bsz, t_len // tm),
        in_specs=[row(N_BRANCH * D_MODEL), row(GM_WIDTH),
                  pl.BlockSpec((1, RW_HEADS, tm, RW_HEAD_DIM), lambda b, i: (b, 0, i, 0)),
                  row(FOX_WIDTH), row(d), mod, mod, mod,
                  full(w_branch.shape), full(w_o.shape), full(w_router.shape), full((1, N_EXPERTS))],
        out_specs=(row(d), row(d), row(TOP_K), row(TOP_K), row(TOP_K), full((1, N_EXPERTS))),
        scratch_shapes=[pltpu.VMEM((1, N_EXPERTS), F32)],
        compiler_params=_cparams(("arbitrary", "arbitrary")),
        name="merge_router",
    )(z, y_gm, y_rw, y_fox, x, gate1, scale2, shift2, w_branch, w_o, w_router, b_router.reshape(1, N_EXPERTS))


def _ffn_kernel(be_ref, nv_ref, x_ref, wgu_ref, bgu_ref, wd_ref, bd_ref, rw_ref, o_ref):
    i = pl.program_id(0)

    @pl.when(i < nv_ref[0])
    def _():
        gu = jnp.dot(x_ref[...], wgu_ref[0], preferred_element_type=F32) + bgu_ref[0]
        g_ = jnp.minimum(gu[:, :D_FF], SWIGLU_LIMIT)
        u_ = jnp.clip(gu[:, D_FF:], -SWIGLU_LIMIT, SWIGLU_LIMIT)
        act = (u_ + 1.0) * (g_ * jax.nn.sigmoid(SWIGLU_ALPHA * g_))
        y = jnp.dot(act.astype(BF16), wd_ref[0], preferred_element_type=F32) + bd_ref[0]
        o_ref[...] = y * rw_ref[...]

    @pl.when(i >= nv_ref[0])
    def _():
        o_ref[...] = jnp.zeros_like(o_ref)


def _ffn(block_expert, n_valid, xin, w_gate_up, b_gate_up, w_down, b_down, row_w):
    n_rows, d = xin.shape
    n_blocks = n_rows // MOE_BLOCK
    grid_spec = pltpu.PrefetchScalarGridSpec(
        num_scalar_prefetch=2,
        grid=(n_blocks,),
        in_specs=[pl.BlockSpec((MOE_BLOCK, d), lambda i, be, nv: (i, 0)),
                  pl.BlockSpec((1, d, 2 * D_FF), lambda i, be, nv: (be[i], 0, 0)),
                  pl.BlockSpec((1, 1, 2 * D_FF), lambda i, be, nv: (be[i], 0, 0)),
                  pl.BlockSpec((1, D_FF, d), lambda i, be, nv: (be[i], 0, 0)),
                  pl.BlockSpec((1, 1, d), lambda i, be, nv: (be[i], 0, 0)),
                  pl.BlockSpec((MOE_BLOCK, 1), lambda i, be, nv: (i, 0))],
        out_specs=pl.BlockSpec((MOE_BLOCK, d), lambda i, be, nv: (i, 0)),
    )
    return pl.pallas_call(
        _ffn_kernel,
        out_shape=jax.ShapeDtypeStruct((n_rows, d), F32),
        grid_spec=grid_spec,
        compiler_params=_cparams(("arbitrary",)),
        name="expert_ffn",
    )(block_expert, n_valid, xin, w_gate_up, b_gate_up.reshape(N_EXPERTS, 1, -1), w_down,
      b_down.reshape(N_EXPERTS, 1, -1), row_w)


def _moe(h2, top_idx, gate, rank, counts, w_gate_up, b_gate_up, w_down, b_down):
    bsz, t_len, d = h2.shape
    n_tok = bsz * t_len
    n_assign = n_tok * TOP_K
    n_blocks = -(-n_assign // MOE_BLOCK) + N_EXPERTS
    n_rows = n_blocks * MOE_BLOCK
    counts = counts.reshape(N_EXPERTS)
    padded = (counts + MOE_BLOCK - 1) // MOE_BLOCK * MOE_BLOCK
    pad_end = jnp.cumsum(padded)
    pad_start = pad_end - padded
    e_flat = top_idx.reshape(n_assign)
    dest = pad_start[e_flat] + rank.reshape(n_assign)
    tok_flat = jnp.arange(n_assign, dtype=I32) // TOP_K
    row_tok = jnp.zeros((n_rows,), I32).at[dest].set(tok_flat)
    row_w = jnp.zeros((n_rows,), F32).at[dest].set(gate.reshape(n_assign))
    block_start = jnp.arange(n_blocks, dtype=I32) * MOE_BLOCK
    block_expert = jnp.minimum(jnp.searchsorted(pad_end, block_start, side='right'), N_EXPERTS - 1).astype(I32)
    n_valid = (pad_end[-1:] // MOE_BLOCK).astype(I32)
    xin = jnp.take(h2.reshape(n_tok, d), row_tok, axis=0)
    yb = _ffn(block_expert, n_valid, xin, w_gate_up, b_gate_up, w_down, b_down, row_w.reshape(n_rows, 1))
    y = jnp.take(yb, dest, axis=0).reshape(n_tok, TOP_K, d).sum(axis=1)
    return y.reshape(bsz, t_len, d)


def _permute_w_in(w_in):
    n_layer, d, _ = w_in.shape
    o_gm = 0
    o_rw = o_gm + 2 * GM_WIDTH
    o_fox = o_rw + RW_SHIFT_WIDTH
    o_f = o_fox + 3 * FOX_WIDTH
    o_gate = o_f + FOX_HEADS
    parts = [w_in[:, :, o_gate:o_gate + N_BRANCH * D_MODEL], w_in[:, :, o_fox:o_f], w_in[:, :, o_gm:o_rw],
             w_in[:, :, o_rw:o_fox], w_in[:, :, o_f:o_gate],
             jnp.zeros((n_layer, d, Z_WIDTH - Z_F - FOX_HEADS), w_in.dtype)]
    return jnp.concatenate(parts, axis=-1).astype(BF16)


def _layer(x, mod, w_in_p, gm_v_gain, gm_w_s, gm_b_s, mu_pad, w_lora, rw_w0, rw_a0, rw_k_k, rw_k_a, rw_r_k,
           rw_gn_gain, rw_gn_bias, f_bias_pad, fox_q_gain, fox_k_gain, w_branch, w_o, w_router, b_router,
           w_gate_up, b_gate_up, w_down, b_down):
    shift1, scale1, gate1, shift2, scale2, gate2 = (mod[:, i][:, None, :] for i in range(6))
    z = _inproj(x, scale1, shift1, w_in_p)
    y_gm = _gmlp(z, gm_v_gain, gm_w_s, gm_b_s)
    r, lw, k, v, a, b, g = _rwprep(z, mu_pad, w_lora, rw_w0, rw_a0, rw_k_k, rw_k_a)
    y_rw = _rwscan(r, lw, k, v, a, b, g, rw_r_k, rw_gn_gain, rw_gn_bias)
    q, kf, vf = _foxprep(z, f_bias_pad, fox_q_gain, fox_k_gain)
    y_fox = _fox(q, kf, vf)
    x1, h2, top_idx, gate, rank, counts = _merge(z, y_gm, y_rw, y_fox, x, gate1, scale2, shift2,
                                                 w_branch, w_o, w_router, b_router)
    y = _moe(h2, top_idx, gate, rank, counts, w_gate_up, b_gate_up, w_down, b_down)
    return x1 + gate2 * y


def kernel(x, c, w_ada, b_ada, w_in, gm_v_gain, gm_w_s, gm_b_s, rw_mu, rw_w0, rw_w2, rw_a0, rw_a2, rw_g2, rw_k_k,
           rw_k_a, rw_r_k, rw_gn_gain, rw_gn_bias, fox_f_bias, fox_q_gain, fox_k_gain, w_branch, w_o, w_router,
           b_router, w_gate_up, b_gate_up, w_down, b_down):
    n_layer = w_ada.shape[0]
    bsz = x.shape[0]
    c_pad = jnp.zeros((8, D_MODEL), F32).at[:bsz].set(c)
    mod = _adaln(c_pad, w_ada, b_ada)[:, :bsz].reshape(n_layer, bsz, 6, D_MODEL)
    w_in_p = _permute_w_in(w_in)
    mu_pad = jnp.pad(rw_mu, ((0, 0), (0, RW_BLOCK - RW_SHIFT_WIDTH)))
    w_lora = jnp.zeros((n_layer, RW_LORA, 3 * RW_WIDTH), F32)
    w_lora = w_lora.at[:, 0:RW_DECAY_LORA, 0:RW_WIDTH].set(rw_w2)
    w_lora = w_lora.at[:, RW_DECAY_LORA:RW_DECAY_LORA + RW_ICLR_LORA, RW_WIDTH:2 * RW_WIDTH].set(rw_a2)
    w_lora = w_lora.at[:, RW_DECAY_LORA + RW_ICLR_LORA:, 2 * RW_WIDTH:].set(rw_g2)
    f_bias_pad = jnp.pad(fox_f_bias, ((0, 0), (0, LANES - FOX_HEADS)))
    w_branch_b = w_branch.astype(BF16)
    w_o_b = w_o.astype(BF16)
    w_gu_b = w_gate_up.astype(BF16)
    w_dn_b = w_down.astype(BF16)
    for l in range(n_layer):
        x = _layer(x, mod[l], w_in_p[l], gm_v_gain[l], gm_w_s[l], gm_b_s[l], mu_pad[l:l + 1], w_lora[l], rw_w0[l],
                   rw_a0[l], rw_k_k[l], rw_k_a[l], rw_r_k[l], rw_gn_gain[l], rw_gn_bias[l], f_bias_pad[l:l + 1],
                   fox_q_gain[l], fox_k_gain[l], w_branch_b[l], w_o_b[l], w_router[l], b_router[l],
                   w_gu_b[l], b_gate_up[l], w_dn_b[l], b_down[l])
    return x
```

```python
import functools

import jax
import jax.numpy as jnp
from jax import lax
from jax.experimental import pallas as pl
from jax.experimental.pallas import tpu as pltpu
from jax.experimental.pallas import tpu_sc as plsc

F32 = jnp.float32
BF16 = jnp.bfloat16
I32 = jnp.int32
HIGHEST = lax.Precision.HIGHEST

D_MODEL = 1024
GM_CHUNK = 128
GM_GROUPS = 4
GM_WIDTH = 256
GM_GROUP_DIM = GM_WIDTH // GM_GROUPS
RW_HEADS = 4
RW_HEAD_DIM = 64
RW_WIDTH = RW_HEADS * RW_HEAD_DIM
RW_DECAY_LORA = 32
RW_ICLR_LORA = 32
RW_GATE_LORA = 64
RW_LORA = RW_DECAY_LORA + RW_ICLR_LORA + RW_GATE_LORA
RW_SHIFT_WIDTH = 3 * RW_WIDTH + RW_LORA
RW_GN_EPS = 64e-5
FOX_HEADS = 8
FOX_HEAD_DIM = 64
FOX_WIDTH = FOX_HEADS * FOX_HEAD_DIM
ATTN_SCALE = FOX_HEAD_DIM ** -0.5
MASK_VALUE = -1e30
N_BRANCH = 3
N_EXPERTS = 32
TOP_K = 4
D_FF = D_MODEL
SWIGLU_LIMIT = 7.0
SWIGLU_ALPHA = 1.702
MOE_BLOCK = 256
EPS = 1e-6

Z_GATE = 0
Z_FOX = N_BRANCH * D_MODEL
Z_GM = Z_FOX + 3 * FOX_WIDTH
Z_RW = Z_GM + 2 * GM_WIDTH
RW_BLOCK = 1024
Z_F = Z_RW + RW_SHIFT_WIDTH
Z_WIDTH = Z_RW + RW_BLOCK
LANES = 128
RW_CHUNK = 64

VMEM_LIMIT = 48 * 1024 * 1024
SC_CORES = 2
SC_SUBCORES = 16
SC_WORKERS = SC_CORES * SC_SUBCORES
SC_ROWS = 32


def _cparams(sem):
    return pltpu.CompilerParams(dimension_semantics=sem, vmem_limit_bytes=VMEM_LIMIT)


def _mm(a, b):
    return jnp.dot(a.astype(BF16), b.astype(BF16), preferred_element_type=F32)


def _mm_nt(a, b):
    return lax.dot_general(a.astype(BF16), b.astype(BF16), (((1,), (1,)), ((), ())), preferred_element_type=F32)


def _mm_tn(a, b):
    return lax.dot_general(a.astype(BF16), b.astype(BF16), (((0,), (0,)), ((), ())), preferred_element_type=F32)


def _split3(x):
    hi = x.astype(BF16)
    r1 = x - hi.astype(F32)
    mid = r1.astype(BF16)
    lo = (r1 - mid.astype(F32)).astype(BF16)
    return hi, mid, lo


def _tri_cumsum(x, n):
    ri = lax.broadcasted_iota(I32, (n, n), 0)
    ci = lax.broadcasted_iota(I32, (n, n), 1)
    ones = jnp.where(ri >= ci, 1.0, 0.0).astype(BF16)
    hi, mid, lo = _split3(x)
    return (jnp.dot(ones, hi, preferred_element_type=F32) + jnp.dot(ones, mid, preferred_element_type=F32)
            + jnp.dot(ones, lo, preferred_element_type=F32))


def _log_sigmoid(x):
    return jnp.minimum(x, 0.0) - jnp.log1p(jnp.exp(-jnp.abs(x)))


def _adaln_kernel(c_ref, w_ref, b_ref, o_ref):
    c = c_ref[...]
    s = c * jax.nn.sigmoid(c)
    o_ref[0] = jnp.dot(s, w_ref[0], preferred_element_type=F32, precision=HIGHEST) + b_ref[0]


def _adaln(c_pad, w_ada, b_ada):
    n_layer, d, w6 = w_ada.shape
    tn = 1536
    return pl.pallas_call(
        _adaln_kernel,
        out_shape=jax.ShapeDtypeStruct((n_layer, c_pad.shape[0], w6), F32),
        grid=(n_layer, w6 // tn),
        in_specs=[pl.BlockSpec(c_pad.shape, lambda l, j: (0, 0)),
                  pl.BlockSpec((1, d, tn), lambda l, j: (l, 0, j)),
                  pl.BlockSpec((1, 1, tn), lambda l, j: (l, 0, j))],
        out_specs=pl.BlockSpec((1, c_pad.shape[0], tn), lambda l, j: (l, 0, j)),
        compiler_params=_cparams(("parallel", "parallel")),
        name="adaln",
    )(c_pad, w_ada, b_ada.reshape(n_layer, 1, w6))


def _inproj_kernel(x_ref, sc_ref, sh_ref, w_ref, o_ref, xn_ref):
    @pl.when(pl.program_id(2) == 0)
    def _():
        x = x_ref[0]
        xn = x * lax.rsqrt(jnp.mean(x * x, axis=-1, keepdims=True) + EPS)
        xn_ref[...] = (xn * (1.0 + sc_ref[0]) + sh_ref[0]).astype(BF16)

    o_ref[0] = jnp.dot(xn_ref[...], w_ref[...], preferred_element_type=F32)


def _inproj(x, scale, shift, w):
    bsz, t_len, d = x.shape
    tm = min(1024, t_len)
    tn = 1024
    return pl.pallas_call(
        _inproj_kernel,
        out_shape=jax.ShapeDtypeStruct((bsz, t_len, Z_WIDTH), F32),
        grid=(bsz, t_len // tm, Z_WIDTH // tn),
        in_specs=[pl.BlockSpec((1, tm, d), lambda b, i, j: (b, i, 0)),
                  pl.BlockSpec((1, 1, d), lambda b, i, j: (b, 0, 0)),
                  pl.BlockSpec((1, 1, d), lambda b, i, j: (b, 0, 0)),
                  pl.BlockSpec((d, tn), lambda b, i, j: (0, j))],
        out_specs=pl.BlockSpec((1, tm, tn), lambda b, i, j: (b, i, j)),
        scratch_shapes=[pltpu.VMEM((tm, d), BF16)],
        compiler_params=_cparams(("parallel", "parallel", "arbitrary")),
        name="inproj",
    )(x, scale, shift, w)


def _gmlp_kernel(z_ref, gain_ref, ws_ref, bst_ref, o_ref):
    tm = z_ref.shape[1]
    z = z_ref[0]
    u = jax.nn.gelu(z[:, :GM_WIDTH])
    v = jax.nn.gelu(z[:, GM_WIDTH:])
    v = v * lax.rsqrt(jnp.mean(v * v, axis=-1, keepdims=True) + EPS) * gain_ref[...]
    vb = v.astype(BF16)
    grp = lax.broadcasted_iota(I32, (GM_CHUNK, GM_WIDTH), 1) // GM_GROUP_DIM
    ri = lax.broadcasted_iota(I32, (GM_CHUNK, GM_CHUNK), 0)
    ci = lax.broadcasted_iota(I32, (GM_CHUNK, GM_CHUNK), 1)
    causal = ri >= ci
    bias = jnp.zeros((GM_CHUNK, GM_WIDTH), F32)
    ws = []
    for g in range(GM_GROUPS):
        ws.append(jnp.where(causal, ws_ref[g], 0.0).astype(BF16))
        bias = jnp.where(grp == g, bst_ref[:, g:g + 1], bias)
    for c in range(tm // GM_CHUNK):
        rows = slice(c * GM_CHUNK, (c + 1) * GM_CHUNK)
        vc = vb[rows]
        mixed = bias
        for g in range(GM_GROUPS):
            m = jnp.dot(ws[g], vc, preferred_element_type=F32)
            mixed = mixed + jnp.where(grp == g, m, 0.0)
        o_ref[0, rows, :] = (u[rows] * mixed).astype(o_ref.dtype)


def _gmlp(z, gain, w_s, b_s):
    bsz, t_len, _ = z.shape
    tm = min(512, t_len)
    return pl.pallas_call(
        _gmlp_kernel,
        out_shape=jax.ShapeDtypeStruct((bsz, t_len, GM_WIDTH), BF16),
        grid=(bsz, t_len // tm),
        in_specs=[pl.BlockSpec((1, tm, 2 * GM_WIDTH), lambda b, i: (b, i, Z_GM // (2 * GM_WIDTH))),
                  pl.BlockSpec((1, GM_WIDTH), lambda b, i: (0, 0)),
                  pl.BlockSpec((GM_GROUPS, GM_CHUNK, GM_CHUNK), lambda b, i: (0, 0, 0)),
                  pl.BlockSpec((GM_CHUNK, GM_GROUPS), lambda b, i: (0, 0))],
        out_specs=pl.BlockSpec((1, tm, GM_WIDTH), lambda b, i: (b, i, 0)),
        compiler_params=_cparams(("parallel", "parallel")),
        name="gmlp",
    )(z, gain.reshape(1, GM_WIDTH), w_s, b_s.T)


def _rwprep_kernel(z_ref, zp_ref, mu_ref, wl_ref, w0_ref, a0_ref, kk_ref, ka_ref,
                   r_o, lw_o, k_o, v_o, a_o, b_o, g_o):
    tm = z_ref.shape[1]
    z = z_ref[0]
    prev = jnp.where(pl.program_id(1) > 0, zp_ref[0, 7:8, :], 0.0)
    rowid = lax.broadcasted_iota(I32, z.shape, 0)
    zs = jnp.where(rowid == 0, prev, pltpu.roll(z, 1, axis=0))
    zz = z + mu_ref[...] * (zs - z)
    r = zz[:, 0:RW_WIDTH]
    k = zz[:, RW_WIDTH:2 * RW_WIDTH]
    v = zz[:, 2 * RW_WIDTH:3 * RW_WIDTH]
    lo = zz[:, 3 * RW_WIDTH:3 * RW_WIDTH + RW_LORA]
    lane = lax.broadcasted_iota(I32, (tm, RW_LORA), 1)
    act = jnp.where(lane < RW_DECAY_LORA, jnp.tanh(lo),
                    jnp.where(lane < RW_DECAY_LORA + RW_ICLR_LORA, lo, jax.nn.sigmoid(lo)))
    proj = jnp.dot(act, wl_ref[...], preferred_element_type=F32, precision=HIGHEST)
    xw = -(w0_ref[...] + proj[:, 0:RW_WIDTH])
    softplus = jnp.maximum(xw, 0.0) + jnp.log1p(jnp.exp(-jnp.abs(xw)))
    lw = -jnp.exp(-softplus - 0.5)
    a = jax.nn.sigmoid(a0_ref[...] + proj[:, RW_WIDTH:2 * RW_WIDTH])
    g = proj[:, 2 * RW_WIDTH:3 * RW_WIDTH]
    kk = k * kk_ref[...]
    k2 = k * (1.0 + (a - 1.0) * ka_ref[...])
    for h in range(RW_HEADS):
        sl = slice(h * RW_HEAD_DIM, (h + 1) * RW_HEAD_DIM)
        kkh = kk[:, sl]
        nrm = jnp.sqrt(jnp.sum(kkh * kkh, axis=-1, keepdims=True))
        kkh = kkh / jnp.maximum(nrm, 1e-12)
        r_o[0, h] = r[:, sl]
        lw_o[0, h] = lw[:, sl]
        k_o[0, h] = k2[:, sl]
        v_o[0, h] = v[:, sl]
        a_o[0, h] = -kkh
        b_o[0, h] = kkh * a[:, sl]
        g_o[0, h] = g[:, sl]


def _rwprep(z, mu_pad, w_lora, w0, a0, k_k, k_a):
    bsz, t_len, _ = z.shape
    tm = min(512, t_len)
    hm = jax.ShapeDtypeStruct((bsz, RW_HEADS, t_len, RW_HEAD_DIM), F32)
    hm_spec = pl.BlockSpec((1, RW_HEADS, tm, RW_HEAD_DIM), lambda b, i: (b, 0, i, 0))
    vec = lambda n: pl.BlockSpec((1, n), lambda b, i: (0, 0))
    rw_blk = Z_RW // RW_BLOCK
    return pl.pallas_call(
        _rwprep_kernel,
        out_shape=(hm,) * 7,
        grid=(bsz, t_len // tm),
        in_specs=[pl.BlockSpec((1, tm, RW_BLOCK), lambda b, i: (b, i, rw_blk)),
                  pl.BlockSpec((1, 8, RW_BLOCK), lambda b, i: (b, jnp.maximum(i * (tm // 8) - 1, 0), rw_blk)),
                  vec(RW_BLOCK),
                  pl.BlockSpec((RW_LORA, 3 * RW_WIDTH), lambda b, i: (0, 0)),
                  vec(RW_WIDTH), vec(RW_WIDTH), vec(RW_WIDTH), vec(RW_WIDTH)],
        out_specs=(hm_spec,) * 7,
        compiler_params=_cparams(("parallel", "parallel")),
        name="rwprep",
    )(z, z, mu_pad, w_lora, w0.reshape(1, -1), a0.reshape(1, -1), k_k.reshape(1, -1), k_a.reshape(1, -1))


def _rwscan_kernel(r_ref, lw_ref, k_ref, v_ref, a_ref, b_ref, g_ref, rk_ref, gg_ref, gb_ref, o_ref, s_ref):
    cl = RW_CHUNK
    tb = r_ref.shape[2]

    @pl.when(pl.program_id(1) == 0)
    def _():
        s_ref[...] = jnp.zeros_like(s_ref)

    ri = lax.broadcasted_iota(I32, (cl, cl), 0)
    ci = lax.broadcasted_iota(I32, (cl, cl), 1)
    lower = ri >= ci
    strict = ri > ci
    eye = jnp.where(ri == ci, 1.0, 0.0)

    def chunk(c, carry):
        t0 = pl.multiple_of(c * cl, cl)
        rows = pl.ds(t0, cl)
        for h in range(RW_HEADS):
            r = r_ref[0, h, rows, :]
            lw = lw_ref[0, h, rows, :]
            k = k_ref[0, h, rows, :]
            v = v_ref[0, h, rows, :]
            a = a_ref[0, h, rows, :]
            b = b_ref[0, h, rows, :]
            cw = _tri_cumsum(lw, cl)
            w_in = jnp.exp(cw)
            w_inv = jnp.exp(-cw)
            rt = r * w_in
            at = a * jnp.exp(cw - lw)
            kt = k * w_inv
            bt = b * w_inv
            w_end = w_in[cl - 1:cl, :]
            a_ab = jnp.where(strict, _mm_nt(at, bt), 0.0)
            a_ak = jnp.where(strict, _mm_nt(at, kt), 0.0)
            m_rb = jnp.where(lower, _mm_nt(rt, bt), 0.0)
            m_rk = jnp.where(lower, _mm_nt(rt, kt), 0.0)
            inv = eye + a_ab
            p = a_ab
            for _ in range(cl.bit_length() - 2):
                p = _mm(p, p)
                inv = inv + _mm(inv, p)
            s = s_ref[h]
            sa = _mm(inv, _mm_nt(at, s) + _mm(a_ak, v))
            y = _mm_nt(rt, s) + _mm(m_rb, sa) + _mm(m_rk, v)
            s_ref[h] = s * w_end + _mm_tn(sa, bt * w_end) + _mm_tn(v, kt * w_end)
            mu = jnp.mean(y, axis=-1, keepdims=True)
            yc = y - mu
            var = jnp.mean(yc * yc, axis=-1, keepdims=True)
            yn = yc * lax.rsqrt(var + RW_GN_EPS) * gg_ref[h] + gb_ref[h]
            bonus = jnp.sum(r * k * rk_ref[h], axis=-1, keepdims=True) * v
            o_ref[0, h, rows, :] = ((yn + bonus) * g_ref[0, h, rows, :]).astype(o_ref.dtype)
        return carry

    lax.fori_loop(0, tb // cl, chunk, 0)


def _rwscan(r, lw, k, v, a, b, g, r_k, gn_gain, gn_bias):
    bsz, _, t_len, _ = r.shape
    tb = min(512, t_len)
    hm_spec = pl.BlockSpec((1, RW_HEADS, tb, RW_HEAD_DIM), lambda bi, i: (bi, 0, i, 0))
    par = pl.BlockSpec((RW_HEADS, 1, RW_HEAD_DIM), lambda bi, i: (0, 0, 0))
    hshape = (RW_HEADS, 1, RW_HEAD_DIM)
    return pl.pallas_call(
        _rwscan_kernel,
        out_shape=jax.ShapeDtypeStruct((bsz, RW_HEADS, t_len, RW_HEAD_DIM), BF16),
        grid=(bsz, t_len // tb),
        in_specs=[hm_spec] * 7 + [par] * 3,
        out_specs=hm_spec,
        scratch_shapes=[pltpu.VMEM((RW_HEADS, RW_HEAD_DIM, RW_HEAD_DIM), F32)],
        compiler_params=_cparams(("parallel", "arbitrary")),
        name="rwscan",
    )(r, lw, k, v, a, b, g, r_k.reshape(hshape), gn_gain.reshape(hshape), gn_bias.reshape(hshape))


def _foxprep_kernel(z_ref, f_ref, fb_ref, qg_ref, kg_ref, q_o, k_o, v_o, carry_ref):
    tm = z_ref.shape[1]

    @pl.when(pl.program_id(1) == 0)
    def _():
        carry_ref[...] = jnp.zeros_like(carry_ref)

    z = z_ref[0]
    log_f = _log_sigmoid(f_ref[0] + fb_ref[...])
    cum = carry_ref[...] + _tri_cumsum(log_f, tm)
    carry_ref[...] = cum[tm - 1:tm, :]
    lane = lax.broadcasted_iota(I32, (tm, FOX_HEAD_DIM), 1)
    qg = qg_ref[...]
    kg = kg_ref[...]
    for h in range(FOX_HEADS):
        f_hi, f_mid, f_lo = (p.astype(F32) for p in _split3(cum[:, h:h + 1]))
        qx = jnp.where(lane == 0, f_hi, jnp.where(lane == 1, f_mid, jnp.where(lane == 2, f_lo,
                       jnp.where(lane < 6, 1.0, 0.0))))
        kx = jnp.where(lane < 3, 1.0, jnp.where(lane == 3, -f_hi, jnp.where(lane == 4, -f_mid,
                       jnp.where(lane == 5, -f_lo, 0.0))))
        qh = z[:, h * FOX_HEAD_DIM:(h + 1) * FOX_HEAD_DIM]
        kh = z[:, FOX_WIDTH + h * FOX_HEAD_DIM:FOX_WIDTH + (h + 1) * FOX_HEAD_DIM]
        qn = qh * lax.rsqrt(jnp.mean(qh * qh, axis=-1, keepdims=True) + EPS) * qg * ATTN_SCALE
        kn = kh * lax.rsqrt(jnp.mean(kh * kh, axis=-1, keepdims=True) + EPS) * kg
        q_o[0, h] = jnp.concatenate([qn, qx], axis=-1).astype(BF16)
        k_o[0, h] = jnp.concatenate([kn, kx], axis=-1).astype(BF16)
        v_o[0, h] = z[:, 2 * FOX_WIDTH + h * FOX_HEAD_DIM:2 * FOX_WIDTH + (h + 1) * FOX_HEAD_DIM].astype(BF16)


def _foxprep(z, f_bias_pad, q_gain, k_gain):
    bsz, t_len, _ = z.shape
    tm = min(512, t_len)
    qk = jax.ShapeDtypeStruct((bsz, FOX_HEADS, t_len, 2 * FOX_HEAD_DIM), BF16)
    vv = jax.ShapeDtypeStruct((bsz, FOX_HEADS, t_len, FOX_HEAD_DIM), BF16)
    qk_spec = pl.BlockSpec((1, FOX_HEADS, tm, 2 * FOX_HEAD_DIM), lambda b, i: (b, 0, i, 0))
    v_spec = pl.BlockSpec((1, FOX_HEADS, tm, FOX_HEAD_DIM), lambda b, i: (b, 0, i, 0))
    return pl.pallas_call(
        _foxprep_kernel,
        out_shape=(qk, qk, vv),
        grid=(bsz, t_len // tm),
        in_specs=[pl.BlockSpec((1, tm, 3 * FOX_WIDTH), lambda b, i: (b, i, Z_FOX // (3 * FOX_WIDTH))),
                  pl.BlockSpec((1, tm, LANES), lambda b, i: (b, i, Z_F // LANES)),
                  pl.BlockSpec((1, LANES), lambda b, i: (0, 0)),
                  pl.BlockSpec((1, FOX_HEAD_DIM), lambda b, i: (0, 0)),
                  pl.BlockSpec((1, FOX_HEAD_DIM), lambda b, i: (0, 0))],
        out_specs=(qk_spec, qk_spec, v_spec),
        scratch_shapes=[pltpu.VMEM((1, LANES), F32)],
        compiler_params=_cparams(("parallel", "arbitrary")),
        name="foxprep",
    )(z, z, f_bias_pad, q_gain.reshape(1, -1), k_gain.reshape(1, -1))


def _fox_kernel(q_ref, k_ref, v_ref, o_ref, m_ref, l_ref, acc_ref):
    i = pl.program_id(1)
    j = pl.program_id(2)
    tq = q_ref.shape[2]
    tk = k_ref.shape[2]

    @pl.when(j == 0)
    def _():
        m_ref[...] = jnp.full_like(m_ref, MASK_VALUE)
        l_ref[...] = jnp.zeros_like(l_ref)
        acc_ref[...] = jnp.zeros_like(acc_ref)

    def update(diagonal):
        if diagonal:
            ri = lax.broadcasted_iota(I32, (tq, tk), 0)
            ci = lax.broadcasted_iota(I32, (tq, tk), 1)
            keep = ri >= ci
        for h in range(FOX_HEADS):
            s = lax.dot_general(q_ref[0, h], k_ref[0, h], (((1,), (1,)), ((), ())), preferred_element_type=F32)
            if diagonal:
                s = jnp.where(keep, s, MASK_VALUE)
            m_prev = m_ref[h]
            m_new = jnp.maximum(m_prev, jnp.max(s, axis=-1, keepdims=True))
            alpha = jnp.exp(m_prev - m_new)
            p = jnp.exp(s - m_new)
            l_ref[h] = alpha * l_ref[h] + jnp.sum(p, axis=-1, keepdims=True)
            acc_ref[h] = alpha * acc_ref[h] + jnp.dot(p.astype(BF16), v_ref[0, h], preferred_element_type=F32)
            m_ref[h] = m_new

    @pl.when(j < i)
    def _():
        update(False)

    @pl.when(j == i)
    def _():
        update(True)
        for h in range(FOX_HEADS):
            o_ref[0, :, h * FOX_HEAD_DIM:(h + 1) * FOX_HEAD_DIM] = (acc_ref[h] / l_ref[h]).astype(o_ref.dtype)


def _fox(q, k, v):
    bsz, _, t_len, _ = q.shape
    tq = min(512, t_len)
    n_blk = t_len // tq
    return pl.pallas_call(
        _fox_kernel,
        out_shape=jax.ShapeDtypeStruct((bsz, t_len, FOX_WIDTH), BF16),
        grid=(bsz, n_blk, n_blk),
        in_specs=[pl.BlockSpec((1, FOX_HEADS, tq, 2 * FOX_HEAD_DIM), lambda b, i, j: (b, 0, i, 0)),
                  pl.BlockSpec((1, FOX_HEADS, tq, 2 * FOX_HEAD_DIM), lambda b, i, j: (b, 0, jnp.minimum(j, i), 0)),
                  pl.BlockSpec((1, FOX_HEADS, tq, FOX_HEAD_DIM), lambda b, i, j: (b, 0, jnp.minimum(j, i), 0))],
        out_specs=pl.BlockSpec((1, tq, FOX_WIDTH), lambda b, i, j: (b, i, 0)),
        scratch_shapes=[pltpu.VMEM((FOX_HEADS, tq, 1), F32), pltpu.VMEM((FOX_HEADS, tq, 1), F32),
                        pltpu.VMEM((FOX_HEADS, tq, FOX_HEAD_DIM), F32)],
        compiler_params=_cparams(("parallel", "parallel", "arbitrary")),
        name="fox",
    )(q, k, v)


def _merge_kernel(zg_ref, ygm_ref, yrw_ref, yfox_ref, x_ref, g1_ref, sc2_ref, sh2_ref, pb_ref, wo_ref, wr_ref, br_ref,
                  x1_o, h2_o, idx_o, gate_o, rank_o, cnt_o, carry_ref):
    tm = x_ref.shape[1]

    @pl.when((pl.program_id(0) == 0) & (pl.program_id(1) == 0))
    def _():
        carry_ref[...] = jnp.zeros_like(carry_ref)

    sg = jax.nn.sigmoid(zg_ref[0])
    p_gm = jnp.dot(ygm_ref[0], pb_ref[0:GM_WIDTH, :], preferred_element_type=F32)
    p_rw = jnp.zeros((tm, D_MODEL), F32)
    for h in range(RW_HEADS):
        lo = GM_WIDTH + h * RW_HEAD_DIM
        p_rw = p_rw + jnp.dot(yrw_ref[0, h], pb_ref[lo:lo + RW_HEAD_DIM, :], preferred_element_type=F32)
    p_fox = jnp.dot(yfox_ref[0], pb_ref[GM_WIDTH + RW_WIDTH:, :], preferred_element_type=F32)
    merged = sg[:, 0:D_MODEL] * p_gm + sg[:, D_MODEL:2 * D_MODEL] * p_rw + sg[:, 2 * D_MODEL:] * p_fox
    x1 = x_ref[0] + g1_ref[0] * jnp.dot(merged.astype(BF16), wo_ref[...], preferred_element_type=F32)
    x1_o[0] = x1
    h2 = x1 * lax.rsqrt(jnp.mean(x1 * x1, axis=-1, keepdims=True) + EPS) * (1.0 + sc2_ref[0]) + sh2_ref[0]
    h2_o[0] = h2

    logits = jnp.dot(h2, wr_ref[...], preferred_element_type=F32, precision=HIGHEST) + br_ref[...]
    lane = lax.broadcasted_iota(I32, (tm, N_EXPERTS), 1)
    vals, idxs = [], []
    rest = logits
    for _ in range(TOP_K):
        m = jnp.max(rest, axis=-1, keepdims=True)
        am = jnp.min(jnp.where(rest == m, lane, N_EXPERTS), axis=-1, keepdims=True)
        vals.append(m)
        idxs.append(am)
        rest = jnp.where(lane == am, -jnp.inf, rest)
    exps = [jnp.exp(val - vals[0]) for val in vals]
    denom = exps[0] + exps[1] + exps[2] + exps[3]

    onehot = jnp.zeros((tm, N_EXPERTS), F32)
    for am in idxs:
        onehot = onehot + jnp.where(lane == am, 1.0, 0.0)
    ri = lax.broadcasted_iota(I32, (tm, tm), 0)
    ci = lax.broadcasted_iota(I32, (tm, tm), 1)
    before = jnp.where(ri > ci, 1.0, 0.0).astype(BF16)
    seen = carry_ref[...] + jnp.dot(before, onehot.astype(BF16), preferred_element_type=F32)
    lane_k = lax.broadcasted_iota(I32, (tm, TOP_K), 1)
    idx_out = jnp.zeros((tm, TOP_K), I32)
    gate_out = jnp.zeros((tm, TOP_K), F32)
    rank_out = jnp.zeros((tm, TOP_K), I32)
    for kk in range(TOP_K):
        rank = jnp.sum(jnp.where(lane == idxs[kk], seen, 0.0), axis=-1, keepdims=True).astype(I32)
        idx_out = jnp.where(lane_k == kk, idxs[kk], idx_out)
        gate_out = jnp.where(lane_k == kk, exps[kk] / denom, gate_out)
        rank_out = jnp.where(lane_k == kk, rank, rank_out)
    idx_o[0] = idx_out
    gate_o[0] = gate_out
    rank_o[0] = rank_out
    total = carry_ref[...] + jnp.sum(onehot, axis=0, keepdims=True)
    carry_ref[...] = total
    cnt_o[...] = total.astype(I32)


def _merge(z, y_gm, y_rw, y_fox, x, gate1, scale2, shift2, w_branch, w_o, w_router, b_router):
    bsz, t_len, d = x.shape
    tm = min(512, t_len)
    row = lambda w: pl.BlockSpec((1, tm, w), lambda b, i: (b, i, 0))
    mod = pl.BlockSpec((1, 1, d), lambda b, i: (b, 0, 0))
    full = lambda shape: pl.BlockSpec(shape, lambda b, i: (0,) * len(shape))
    return pl.pallas_call(
        _merge_kernel,
        out_shape=(jax.ShapeDtypeStruct((bsz, t_len, d), F32), jax.ShapeDtypeStruct((bsz, t_len, d), F32),
                   jax.ShapeDtypeStruct((bsz, t_len, TOP_K), I32), jax.ShapeDtypeStruct((bsz, t_len, TOP_K), F32),
                   jax.ShapeDtypeStruct((bsz, t_len, TOP_K), I32), jax.ShapeDtypeStruct((1, N_EXPERTS), I32)),
        grid=(bsz, t_len // tm),
        in_specs=[row(N_BRANCH * D_MODEL), row(GM_WIDTH),
                  pl.BlockSpec((1, RW_HEADS, tm, RW_HEAD_DIM), lambda b, i: (b, 0, i, 0)),
                  row(FOX_WIDTH), row(d), mod, mod, mod,
                  full(w_branch.shape), full(w_o.shape), full(w_router.shape), full((1, N_EXPERTS))],
        out_specs=(row(d), row(d), row(TOP_K), row(TOP_K), row(TOP_K), full((1, N_EXPERTS))),
        scratch_shapes=[pltpu.VMEM((1, N_EXPERTS), F32)],
        compiler_params=_cparams(("arbitrary", "arbitrary")),
        name="merge_router",
    )(z, y_gm, y_rw, y_fox, x, gate1, scale2, shift2, w_branch, w_o, w_router, b_router.reshape(1, N_EXPERTS))


def _sc_mesh():
    return plsc.VectorSubcoreMesh(core_axis_name="c", subcore_axis_name="s",
                                  num_cores=SC_CORES, num_subcores=SC_SUBCORES)


def _sc_worker():
    return lax.axis_index("s") * SC_CORES + lax.axis_index("c")


def _sc_scatter_rows(src, idx3, n_out):
    _, d = src.shape
    n_copy, n_grp, _ = idx3.shape
    grp_per_w = n_grp // SC_WORKERS

    def body(src_hbm, idx_hbm, out_hbm, idx_v, rows_v):
        g0 = _sc_worker() * grp_per_w
        for q in range(n_copy):
            pltpu.sync_copy(idx_hbm.at[q, pl.ds(g0, grp_per_w)], idx_v.at[pl.ds(q * grp_per_w, grp_per_w)])

        @pl.loop(0, grp_per_w)
        def _(j):
            pltpu.sync_copy(src_hbm.at[pl.ds((g0 + j) * SC_ROWS, SC_ROWS)], rows_v)
            for q in range(n_copy):
                pltpu.sync_copy(rows_v, out_hbm.at[idx_v.at[q * grp_per_w + j]])

    return pl.kernel(
        body, out_type=jax.ShapeDtypeStruct((n_out, d), src.dtype), mesh=_sc_mesh(),
        scratch_types=[pltpu.VMEM((n_copy * grp_per_w, SC_ROWS), I32), pltpu.VMEM((SC_ROWS, d), src.dtype)],
        name="sc_dispatch",
    )(src, idx3)


def _sc_gather_rows(table, idx2):
    _, d = table.shape
    n_grp, _ = idx2.shape
    grp_per_w = n_grp // SC_WORKERS

    def body(table_hbm, idx_hbm, out_hbm, idx_v, rows_v):
        g0 = _sc_worker() * grp_per_w
        pltpu.sync_copy(idx_hbm.at[pl.ds(g0, grp_per_w)], idx_v)

        @pl.loop(0, grp_per_w)
        def _(j):
            pltpu.sync_copy(table_hbm.at[idx_v.at[j]], rows_v)
            pltpu.sync_copy(rows_v, out_hbm.at[pl.ds((g0 + j) * SC_ROWS, SC_ROWS)])

    return pl.kernel(
        body, out_type=jax.ShapeDtypeStruct((n_grp * SC_ROWS, d), table.dtype), mesh=_sc_mesh(),
        scratch_types=[pltpu.VMEM((grp_per_w, SC_ROWS), I32), pltpu.VMEM((SC_ROWS, d), table.dtype)],
        name="sc_combine_gather",
    )(table, idx2)


def _ffn_kernel(be_ref, nv_ref, x_ref, wgu_ref, bgu_ref, wd_ref, bd_ref, o_ref):
    n_valid = nv_ref[pl.program_id(0)]

    @pl.when(n_valid > 0)
    def _():
        rowid = lax.broadcasted_iota(I32, x_ref.shape, 0)
        x = jnp.where(rowid < n_valid, x_ref[...], 0.0).astype(BF16)
        gu = jnp.dot(x, wgu_ref[0], preferred_element_type=F32) + bgu_ref[0]
        g_ = jnp.minimum(gu[:, :D_FF], SWIGLU_LIMIT)
        u_ = jnp.clip(gu[:, D_FF:], -SWIGLU_LIMIT, SWIGLU_LIMIT)
        act = (u_ + 1.0) * (g_ * jax.nn.sigmoid(SWIGLU_ALPHA * g_))
        o_ref[...] = jnp.dot(act.astype(BF16), wd_ref[0], preferred_element_type=F32) + bd_ref[0]

    @pl.when(n_valid <= 0)
    def _():
        o_ref[...] = jnp.zeros_like(o_ref)


def _ffn(block_expert, block_valid, xin, w_gate_up, b_gate_up, w_down, b_down):
    n_rows, d = xin.shape
    n_blocks = n_rows // MOE_BLOCK
    grid_spec = pltpu.PrefetchScalarGridSpec(
        num_scalar_prefetch=2,
        grid=(n_blocks,),
        in_specs=[pl.BlockSpec((MOE_BLOCK, d), lambda i, be, nv: (i, 0)),
                  pl.BlockSpec((1, d, 2 * D_FF), lambda i, be, nv: (be[i], 0, 0)),
                  pl.BlockSpec((1, 1, 2 * D_FF), lambda i, be, nv: (be[i], 0, 0)),
                  pl.BlockSpec((1, D_FF, d), lambda i, be, nv: (be[i], 0, 0)),
                  pl.BlockSpec((1, 1, d), lambda i, be, nv: (be[i], 0, 0))],
        out_specs=pl.BlockSpec((MOE_BLOCK, d), lambda i, be, nv: (i, 0)),
    )
    return pl.pallas_call(
        _ffn_kernel,
        out_shape=jax.ShapeDtypeStruct((n_rows, d), F32),
        grid_spec=grid_spec,
        compiler_params=_cparams(("arbitrary",)),
        name="expert_ffn",
    )(block_expert, block_valid, xin, w_gate_up, b_gate_up.reshape(N_EXPERTS, 1, -1), w_down,
      b_down.reshape(N_EXPERTS, 1, -1))


def _combine_kernel(x1_ref, g2_ref, gate_ref, yg_ref, o_ref):
    gate = gate_ref[0]
    y = gate[:, 0:1] * yg_ref[0, 0]
    for q in range(1, TOP_K):
        y = y + gate[:, q:q + 1] * yg_ref[q, 0]
    o_ref[0] = x1_ref[0] + g2_ref[0] * y


def _combine(x1, gate2, gate, yg):
    bsz, t_len, d = x1.shape
    tm = min(512, t_len)
    return pl.pallas_call(
        _combine_kernel,
        out_shape=jax.ShapeDtypeStruct((bsz, t_len, d), F32),
        grid=(bsz, t_len // tm),
        in_specs=[pl.BlockSpec((1, tm, d), lambda b, i: (b, i, 0)),
                  pl.BlockSpec((1, 1, d), lambda b, i: (b, 0, 0)),
                  pl.BlockSpec((1, tm, TOP_K), lambda b, i: (b, i, 0)),
                  pl.BlockSpec((TOP_K, 1, tm, d), lambda b, i: (0, b, i, 0))],
        out_specs=pl.BlockSpec((1, tm, d), lambda b, i: (b, i, 0)),
        compiler_params=_cparams(("parallel", "parallel")),
        name="moe_combine",
    )(x1, gate2, gate, yg)


def _moe(x1, gate2, h2, top_idx, gate, rank, counts, w_gate_up, b_gate_up, w_down, b_down):
    bsz, t_len, d = h2.shape
    n_tok = bsz * t_len
    n_assign = n_tok * TOP_K
    n_blocks = -(-n_assign // MOE_BLOCK) + N_EXPERTS
    counts = counts.reshape(N_EXPERTS)
    blocks_e = (counts + MOE_BLOCK - 1) // MOE_BLOCK
    blk_end = jnp.cumsum(blocks_e)
    blk_start = blk_end - blocks_e
    experts = jnp.arange(N_EXPERTS, dtype=I32)
    onehot = top_idx.reshape(n_tok, TOP_K, 1) == experts
    dest = jnp.sum(jnp.where(onehot, blk_start * MOE_BLOCK, 0), axis=-1) + rank.reshape(n_tok, TOP_K)
    dest_t = dest.T.astype(I32)
    blk = jnp.arange(n_blocks, dtype=I32)
    block_expert = jnp.minimum(jnp.sum(blk_end[None, :] <= blk[:, None], axis=1), N_EXPERTS - 1).astype(I32)
    be_hot = block_expert[:, None] == experts
    cnt_b = jnp.sum(jnp.where(be_hot, counts, 0), axis=1)
    start_b = jnp.sum(jnp.where(be_hot, blk_start, 0), axis=1)
    block_valid = jnp.clip(cnt_b - (blk - start_b) * MOE_BLOCK, 0, MOE_BLOCK).astype(I32)
    xin = _sc_scatter_rows(h2.reshape(n_tok, d), dest_t.reshape(TOP_K, n_tok // SC_ROWS, SC_ROWS),
                           n_blocks * MOE_BLOCK)
    yb = _ffn(block_expert, block_valid, xin, w_gate_up, b_gate_up, w_down, b_down)
    yg = _sc_gather_rows(yb, dest_t.reshape(n_assign // SC_ROWS, SC_ROWS))
    return _combine(x1, gate2, gate, yg.reshape(TOP_K, bsz, t_len, d))


def _permute_w_in(w_in):
    n_layer, d, _ = w_in.shape
    o_gm = 0
    o_rw = o_gm + 2 * GM_WIDTH
    o_fox = o_rw + RW_SHIFT_WIDTH
    o_f = o_fox + 3 * FOX_WIDTH
    o_gate = o_f + FOX_HEADS
    parts = [w_in[:, :, o_gate:o_gate + N_BRANCH * D_MODEL], w_in[:, :, o_fox:o_f], w_in[:, :, o_gm:o_rw],
             w_in[:, :, o_rw:o_fox], w_in[:, :, o_f:o_gate],
             jnp.zeros((n_layer, d, Z_WIDTH - Z_F - FOX_HEADS), w_in.dtype)]
    return jnp.concatenate(parts, axis=-1).astype(BF16)


def _layer(x, mod, w_in_p, gm_v_gain, gm_w_s, gm_b_s, mu_pad, w_lora, rw_w0, rw_a0, rw_k_k, rw_k_a, rw_r_k,
           rw_gn_gain, rw_gn_bias, f_bias_pad, fox_q_gain, fox_k_gain, w_branch, w_o, w_router, b_router,
           w_gate_up, b_gate_up, w_down, b_down):
    shift1, scale1, gate1, shift2, scale2, gate2 = (mod[:, i][:, None, :] for i in range(6))
    z = _inproj(x, scale1, shift1, w_in_p)
    y_gm = _gmlp(z, gm_v_gain, gm_w_s, gm_b_s)
    r, lw, k, v, a, b, g = _rwprep(z, mu_pad, w_lora, rw_w0, rw_a0, rw_k_k, rw_k_a)
    y_rw = _rwscan(r, lw, k, v, a, b, g, rw_r_k, rw_gn_gain, rw_gn_bias)
    q, kf, vf = _foxprep(z, f_bias_pad, fox_q_gain, fox_k_gain)
    y_fox = _fox(q, kf, vf)
    x1, h2, top_idx, gate, rank, counts = _merge(z, y_gm, y_rw, y_fox, x, gate1, scale2, shift2,
                                                 w_branch, w_o, w_router, b_router)
    return _moe(x1, gate2, h2, top_idx, gate, rank, counts, w_gate_up, b_gate_up, w_down, b_down)


def kernel(x, c, w_ada, b_ada, w_in, gm_v_gain, gm_w_s, gm_b_s, rw_mu, rw_w0, rw_w2, rw_a0, rw_a2, rw_g2, rw_k_k,
           rw_k_a, rw_r_k, rw_gn_gain, rw_gn_bias, fox_f_bias, fox_q_gain, fox_k_gain, w_branch, w_o, w_router,
           b_router, w_gate_up, b_gate_up, w_down, b_down):
    n_layer = w_ada.shape[0]
    bsz = x.shape[0]
    c_pad = jnp.zeros((8, D_MODEL), F32).at[:bsz].set(c)
    mod = _adaln(c_pad, w_ada, b_ada)[:, :bsz].reshape(n_layer, bsz, 6, D_MODEL)
    w_in_p = _permute_w_in(w_in)
    mu_pad = jnp.pad(rw_mu, ((0, 0), (0, RW_BLOCK - RW_SHIFT_WIDTH)))
    w_lora = jnp.zeros((n_layer, RW_LORA, 3 * RW_WIDTH), F32)
    w_lora = w_lora.at[:, 0:RW_DECAY_LORA, 0:RW_WIDTH].set(rw_w2)
    w_lora = w_lora.at[:, RW_DECAY_LORA:RW_DECAY_LORA + RW_ICLR_LORA, RW_WIDTH:2 * RW_WIDTH].set(rw_a2)
    w_lora = w_lora.at[:, RW_DECAY_LORA + RW_ICLR_LORA:, 2 * RW_WIDTH:].set(rw_g2)
    f_bias_pad = jnp.pad(fox_f_bias, ((0, 0), (0, LANES - FOX_HEADS)))
    w_branch_b = w_branch.astype(BF16)
    w_o_b = w_o.astype(BF16)
    w_gu_b = w_gate_up.astype(BF16)
    w_dn_b = w_down.astype(BF16)
    for l in range(n_layer):
        x = _layer(x, mod[l], w_in_p[l], gm_v_gain[l], gm_w_s[l], gm_b_s[l], mu_pad[l:l + 1], w_lora[l], rw_w0[l],
                   rw_a0[l], rw_k_k[l], rw_k_a[l], rw_r_k[l], rw_gn_gain[l], rw_gn_bias[l], f_bias_pad[l:l + 1],
                   fox_q_gain[l], fox_k_gain[l], w_branch_b[l], w_o_b[l], w_router[l], b_router[l],
                   w_gu_b[l], b_gate_up[l], w_dn_b[l], b_down[l])
    return x
```

```python
import functools

import jax
import jax.numpy as jnp
from jax import lax
from jax.experimental import pallas as pl
from jax.experimental.pallas import tpu as pltpu
from jax.experimental.pallas import tpu_sc as plsc

F32 = jnp.float32
BF16 = jnp.bfloat16
I32 = jnp.int32
HIGHEST = lax.Precision.HIGHEST

D_MODEL = 1024
GM_CHUNK = 128
GM_GROUPS = 4
GM_WIDTH = 256
GM_GROUP_DIM = GM_WIDTH // GM_GROUPS
RW_HEADS = 4
RW_HEAD_DIM = 64
RW_WIDTH = RW_HEADS * RW_HEAD_DIM
RW_DECAY_LORA = 32
RW_ICLR_LORA = 32
RW_GATE_LORA = 64
RW_LORA = RW_DECAY_LORA + RW_ICLR_LORA + RW_GATE_LORA
RW_SHIFT_WIDTH = 3 * RW_WIDTH + RW_LORA
RW_GN_EPS = 64e-5
FOX_HEADS = 8
FOX_HEAD_DIM = 64
FOX_WIDTH = FOX_HEADS * FOX_HEAD_DIM
ATTN_SCALE = FOX_HEAD_DIM ** -0.5
MASK_VALUE = -1e30
LOG2E = 1.4426950408889634
N_BRANCH = 3
N_EXPERTS = 32
TOP_K = 4
D_FF = D_MODEL
SWIGLU_LIMIT = 7.0
SWIGLU_ALPHA = 1.702
MOE_BLOCK = 256
EPS = 1e-6

Z_GATE = 0
Z_FOX = N_BRANCH * D_MODEL
Z_GM = Z_FOX + 3 * FOX_WIDTH
Z_RW = Z_GM + 2 * GM_WIDTH
RW_BLOCK = 1024
Z_F = Z_RW + RW_SHIFT_WIDTH
Z_WIDTH = Z_RW + RW_BLOCK
LANES = 128
RW_CHUNK = 64
RW_PREP_UNROLL = 2

VMEM_LIMIT = 48 * 1024 * 1024
SC_CORES = 2
SC_SUBCORES = 16
SC_WORKERS = SC_CORES * SC_SUBCORES
SC_ROWS = 32


def _cparams(sem):
    return pltpu.CompilerParams(dimension_semantics=sem, vmem_limit_bytes=VMEM_LIMIT)


def _mm(a, b):
    return jnp.dot(a.astype(BF16), b.astype(BF16), preferred_element_type=F32)


def _mm_nt(a, b):
    return lax.dot_general(a.astype(BF16), b.astype(BF16), (((1,), (1,)), ((), ())), preferred_element_type=F32)


def _mm_tn(a, b):
    return lax.dot_general(a.astype(BF16), b.astype(BF16), (((0,), (0,)), ((), ())), preferred_element_type=F32)


def _split3(x):
    hi = x.astype(BF16)
    r1 = x - hi.astype(F32)
    mid = r1.astype(BF16)
    lo = (r1 - mid.astype(F32)).astype(BF16)
    return hi, mid, lo


def _tri_cumsum(x, n):
    ri = lax.broadcasted_iota(I32, (n, n), 0)
    ci = lax.broadcasted_iota(I32, (n, n), 1)
    ones = jnp.where(ri >= ci, 1.0, 0.0).astype(BF16)
    hi, mid, lo = _split3(x)
    return (jnp.dot(ones, hi, preferred_element_type=F32) + jnp.dot(ones, mid, preferred_element_type=F32)
            + jnp.dot(ones, lo, preferred_element_type=F32))


def _log_sigmoid(x):
    return jnp.minimum(x, 0.0) - jnp.log1p(jnp.exp(-jnp.abs(x)))


def _adaln_kernel(c_ref, w_ref, b_ref, o_ref):
    c = c_ref[...]
    s = c * jax.nn.sigmoid(c)
    o_ref[0] = jnp.dot(s, w_ref[0], preferred_element_type=F32, precision=HIGHEST) + b_ref[0]


def _adaln(c_pad, w_ada, b_ada):
    n_layer, d, w6 = w_ada.shape
    tn = 1536
    return pl.pallas_call(
        _adaln_kernel,
        out_shape=jax.ShapeDtypeStruct((n_layer, c_pad.shape[0], w6), F32),
        grid=(n_layer, w6 // tn),
        in_specs=[pl.BlockSpec(c_pad.shape, lambda l, j: (0, 0)),
                  pl.BlockSpec((1, d, tn), lambda l, j: (l, 0, j)),
                  pl.BlockSpec((1, 1, tn), lambda l, j: (l, 0, j))],
        out_specs=pl.BlockSpec((1, c_pad.shape[0], tn), lambda l, j: (l, 0, j)),
        compiler_params=_cparams(("parallel", "parallel")),
        name="adaln",
    )(c_pad, w_ada, b_ada.reshape(n_layer, 1, w6))


def _inproj_kernel(x_ref, sc_ref, sh_ref, w_ref, o_ref, xn_ref):
    @pl.when(pl.program_id(2) == 0)
    def _():
        x = x_ref[0]
        xn = x * lax.rsqrt(jnp.mean(x * x, axis=-1, keepdims=True) + EPS)
        xn_ref[...] = (xn * (1.0 + sc_ref[0]) + sh_ref[0]).astype(BF16)

    o_ref[0] = jnp.dot(xn_ref[...], w_ref[...], preferred_element_type=F32)


def _inproj(x, scale, shift, w):
    bsz, t_len, d = x.shape
    tm = min(1024, t_len)
    tn = 1024
    return pl.pallas_call(
        _inproj_kernel,
        out_shape=jax.ShapeDtypeStruct((bsz, t_len, Z_WIDTH), F32),
        grid=(bsz, t_len // tm, Z_WIDTH // tn),
        in_specs=[pl.BlockSpec((1, tm, d), lambda b, i, j: (b, i, 0)),
                  pl.BlockSpec((1, 1, d), lambda b, i, j: (b, 0, 0)),
                  pl.BlockSpec((1, 1, d), lambda b, i, j: (b, 0, 0)),
                  pl.BlockSpec((d, tn), lambda b, i, j: (0, j))],
        out_specs=pl.BlockSpec((1, tm, tn), lambda b, i, j: (b, i, j)),
        scratch_shapes=[pltpu.VMEM((tm, d), BF16)],
        compiler_params=_cparams(("parallel", "parallel", "arbitrary")),
        name="inproj",
    )(x, scale, shift, w)


def _gmlp_kernel(z_ref, gain_ref, ws_ref, bst_ref, o_ref):
    tm = z_ref.shape[1]
    z = z_ref[0]
    u = jax.nn.gelu(z[:, :GM_WIDTH])
    v = jax.nn.gelu(z[:, GM_WIDTH:])
    v = v * lax.rsqrt(jnp.mean(v * v, axis=-1, keepdims=True) + EPS) * gain_ref[...]
    vb = v.astype(BF16)
    grp = lax.broadcasted_iota(I32, (GM_CHUNK, GM_WIDTH), 1) // GM_GROUP_DIM
    ri = lax.broadcasted_iota(I32, (GM_CHUNK, GM_CHUNK), 0)
    ci = lax.broadcasted_iota(I32, (GM_CHUNK, GM_CHUNK), 1)
    causal = ri >= ci
    bias = jnp.zeros((GM_CHUNK, GM_WIDTH), F32)
    ws = []
    for g in range(GM_GROUPS):
        ws.append(jnp.where(causal, ws_ref[g], 0.0).astype(BF16))
        bias = jnp.where(grp == g, bst_ref[:, g:g + 1], bias)
    for c in range(tm // GM_CHUNK):
        rows = slice(c * GM_CHUNK, (c + 1) * GM_CHUNK)
        vc = vb[rows]
        mixed = bias
        for g in range(GM_GROUPS):
            m = jnp.dot(ws[g], vc, preferred_element_type=F32)
            mixed = mixed + jnp.where(grp == g, m, 0.0)
        o_ref[0, rows, :] = (u[rows] * mixed).astype(o_ref.dtype)


def _gmlp(z, gain, w_s, b_s):
    bsz, t_len, _ = z.shape
    tm = min(512, t_len)
    return pl.pallas_call(
        _gmlp_kernel,
        out_shape=jax.ShapeDtypeStruct((bsz, t_len, GM_WIDTH), BF16),
        grid=(bsz, t_len // tm),
        in_specs=[pl.BlockSpec((1, tm, 2 * GM_WIDTH), lambda b, i: (b, i, Z_GM // (2 * GM_WIDTH))),
                  pl.BlockSpec((1, GM_WIDTH), lambda b, i: (0, 0)),
                  pl.BlockSpec((GM_GROUPS, GM_CHUNK, GM_CHUNK), lambda b, i: (0, 0, 0)),
                  pl.BlockSpec((GM_CHUNK, GM_GROUPS), lambda b, i: (0, 0))],
        out_specs=pl.BlockSpec((1, tm, GM_WIDTH), lambda b, i: (b, i, 0)),
        compiler_params=_cparams(("parallel", "parallel")),
        name="gmlp",
    )(z, gain.reshape(1, GM_WIDTH), w_s, b_s.T)


def _rwprep_kernel(z_ref, zp_ref, mu_ref, wl_ref, w0_ref, a0_ref, kk_ref, ka_ref,
                   r_o, lw_o, k_o, v_o, a_o, b_o, g_o):
    tm = z_ref.shape[1]
    z = z_ref[0]
    prev = jnp.where(pl.program_id(1) > 0, zp_ref[0, 7:8, :], 0.0)
    rowid = lax.broadcasted_iota(I32, z.shape, 0)
    zs = jnp.where(rowid == 0, prev, pltpu.roll(z, 1, axis=0))
    zz = z + mu_ref[...] * (zs - z)
    r = zz[:, 0:RW_WIDTH]
    k = zz[:, RW_WIDTH:2 * RW_WIDTH]
    v = zz[:, 2 * RW_WIDTH:3 * RW_WIDTH]
    lo = zz[:, 3 * RW_WIDTH:3 * RW_WIDTH + RW_LORA]
    lane = lax.broadcasted_iota(I32, (tm, RW_LORA), 1)
    act = jnp.where(lane < RW_DECAY_LORA, jnp.tanh(lo),
                    jnp.where(lane < RW_DECAY_LORA + RW_ICLR_LORA, lo, jax.nn.sigmoid(lo)))
    proj = jnp.dot(act, wl_ref[...], preferred_element_type=F32, precision=HIGHEST)
    xw = -(w0_ref[...] + proj[:, 0:RW_WIDTH])
    softplus = jnp.maximum(xw, 0.0) + jnp.log1p(jnp.exp(-jnp.abs(xw)))
    lw = -jnp.exp(-softplus - 0.5)
    a = jax.nn.sigmoid(a0_ref[...] + proj[:, RW_WIDTH:2 * RW_WIDTH])
    g = proj[:, 2 * RW_WIDTH:3 * RW_WIDTH]
    kk = k * kk_ref[...]
    k2 = k * (1.0 + (a - 1.0) * ka_ref[...])
    for h in range(RW_HEADS):
        sl = slice(h * RW_HEAD_DIM, (h + 1) * RW_HEAD_DIM)
        kkh = kk[:, sl]
        nrm = jnp.sqrt(jnp.sum(kkh * kkh, axis=-1, keepdims=True))
        kkh = kkh / jnp.maximum(nrm, 1e-12)
        r_o[0, h] = r[:, sl]
        lw_o[0, h] = lw[:, sl]
        k_o[0, h] = k2[:, sl]
        v_o[0, h] = v[:, sl]
        a_o[0, h] = -kkh
        b_o[0, h] = kkh * a[:, sl]
        g_o[0, h] = g[:, sl]


def _rwprep(z, mu_pad, w_lora, w0, a0, k_k, k_a):
    bsz, t_len, _ = z.shape
    tm = min(512, t_len)
    hm = jax.ShapeDtypeStruct((bsz, RW_HEADS, t_len, RW_HEAD_DIM), F32)
    hm_spec = pl.BlockSpec((1, RW_HEADS, tm, RW_HEAD_DIM), lambda b, i: (b, 0, i, 0))
    vec = lambda n: pl.BlockSpec((1, n), lambda b, i: (0, 0))
    rw_blk = Z_RW // RW_BLOCK
    return pl.pallas_call(
        _rwprep_kernel,
        out_shape=(hm,) * 7,
        grid=(bsz, t_len // tm),
        in_specs=[pl.BlockSpec((1, tm, RW_BLOCK), lambda b, i: (b, i, rw_blk)),
                  pl.BlockSpec((1, 8, RW_BLOCK), lambda b, i: (b, jnp.maximum(i * (tm // 8) - 1, 0), rw_blk)),
                  vec(RW_BLOCK),
                  pl.BlockSpec((RW_LORA, 3 * RW_WIDTH), lambda b, i: (0, 0)),
                  vec(RW_WIDTH), vec(RW_WIDTH), vec(RW_WIDTH), vec(RW_WIDTH)],
        out_specs=(hm_spec,) * 7,
        compiler_params=_cparams(("parallel", "parallel")),
        name="rwprep",
    )(z, z, mu_pad, w_lora, w0.reshape(1, -1), a0.reshape(1, -1), k_k.reshape(1, -1), k_a.reshape(1, -1))


def _rwscan_kernel(r_ref, lw_ref, k_ref, v_ref, a_ref, b_ref, g_ref, rk_ref, gg_ref, gb_ref, o_ref,
                   s_ref, rp_ref, y_ref, gm_ref, h0_ref, we_ref):
    cl = RW_CHUNK
    tb = r_ref.shape[2]
    n_chunk = tb // cl

    @pl.when(pl.program_id(1) == 0)
    def _():
        s_ref[...] = jnp.zeros_like(s_ref)

    n = RW_HEADS * cl
    ri = lax.broadcasted_iota(I32, (n, n), 0)
    ci = lax.broadcasted_iota(I32, (n, n), 1)
    same_head = (ri // cl) == (ci // cl)
    lower = same_head & (ri >= ci)
    strict = same_head & (ri > ci)
    eye = jnp.where(ri == ci, 1.0, 0.0)
    ones_lower = jnp.where(lower, 1.0, 0.0).astype(BF16)

    def prepare(c):
        rows = pl.ds(pl.multiple_of(c * cl, cl), cl)
        stack = lambda ref: ref[0, :, rows, :].reshape(n, RW_HEAD_DIM)
        r, lw, k, v, a, b = (stack(ref) for ref in (r_ref, lw_ref, k_ref, v_ref, a_ref, b_ref))
        hi, mid, lo = _split3(lw)
        cw = (jnp.dot(ones_lower, hi, preferred_element_type=F32) + jnp.dot(ones_lower, mid, preferred_element_type=F32)
              + jnp.dot(ones_lower, lo, preferred_element_type=F32))
        w_in = jnp.exp(cw)
        w_inv = jnp.exp(-cw)
        rt = r * w_in
        at = a * jnp.exp(cw - lw)
        kt = k * w_inv
        bt = b * w_inv
        w_end = w_in.reshape(RW_HEADS, cl, RW_HEAD_DIM)[:, cl - 1:cl, :]
        w_end_rows = jnp.broadcast_to(w_end, (RW_HEADS, cl, RW_HEAD_DIM)).reshape(n, RW_HEAD_DIM)
        a_ab = jnp.where(strict, _mm_nt(at, bt), 0.0)
        a_ak = jnp.where(strict, _mm_nt(at, kt), 0.0)
        m_rb = jnp.where(lower, _mm_nt(rt, bt), 0.0)
        m_rk = jnp.where(lower, _mm_nt(rt, kt), 0.0)
        inv = eye + a_ab
        p = a_ab
        for _ in range(cl.bit_length() - 2):
            p = _mm(p, p)
            inv = inv + _mm(inv, p)
        ap = _mm(inv, at)
        z0 = _mm(inv, _mm(a_ak, v))
        bend = bt * w_end_rows
        kend = kt * w_end_rows
        rp = (rt + _mm(m_rb, ap)).astype(BF16)
        y0 = _mm(m_rb, z0) + _mm(m_rk, v)
        for h in range(RW_HEADS):
            hs = slice(h * cl, (h + 1) * cl)
            rp_ref[h, rows, :] = rp[hs]
            y_ref[h, rows, :] = y0[hs]
            gm_ref[h, rows, :] = _mm_tn(ap[hs], bend[hs]).astype(BF16)
            h0_ref[h, rows, :] = _mm_tn(z0[hs], bend[hs]) + _mm_tn(v[hs], kend[hs])
            we_ref[h, c] = w_end[h]

    def prepare_step(i, carry):
        for u in range(RW_PREP_UNROLL):
            prepare(i * RW_PREP_UNROLL + u)
        return carry

    lax.fori_loop(0, n_chunk // RW_PREP_UNROLL, prepare_step, 0)

    def advance(c, carry):
        rows = pl.ds(pl.multiple_of(c * cl, cl), cl)
        for h in range(RW_HEADS):
            s = s_ref[h]
            sb = s.astype(BF16)
            y_ref[h, rows, :] = y_ref[h, rows, :] + lax.dot_general(
                rp_ref[h, rows, :], sb, (((1,), (1,)), ((), ())), preferred_element_type=F32)
            s_ref[h] = (s * we_ref[h, c] + jnp.dot(sb, gm_ref[h, rows, :], preferred_element_type=F32)
                        + h0_ref[h, rows, :])
        return carry

    lax.fori_loop(0, n_chunk, advance, 0)

    for h in range(RW_HEADS):
        y = y_ref[h]
        mu = jnp.mean(y, axis=-1, keepdims=True)
        yc = y - mu
        var = jnp.mean(yc * yc, axis=-1, keepdims=True)
        yn = yc * lax.rsqrt(var + RW_GN_EPS) * gg_ref[h] + gb_ref[h]
        v = v_ref[0, h]
        bonus = jnp.sum(r_ref[0, h] * k_ref[0, h] * rk_ref[h], axis=-1, keepdims=True) * v
        o_ref[0, h] = ((yn + bonus) * g_ref[0, h]).astype(o_ref.dtype)


def _rwscan(r, lw, k, v, a, b, g, r_k, gn_gain, gn_bias):
    bsz, _, t_len, _ = r.shape
    tb = min(512, t_len)
    hm_spec = pl.BlockSpec((1, RW_HEADS, tb, RW_HEAD_DIM), lambda bi, i: (bi, 0, i, 0))
    par = pl.BlockSpec((RW_HEADS, 1, RW_HEAD_DIM), lambda bi, i: (0, 0, 0))
    hshape = (RW_HEADS, 1, RW_HEAD_DIM)
    return pl.pallas_call(
        _rwscan_kernel,
        out_shape=jax.ShapeDtypeStruct((bsz, RW_HEADS, t_len, RW_HEAD_DIM), BF16),
        grid=(bsz, t_len // tb),
        in_specs=[hm_spec] * 7 + [par] * 3,
        out_specs=hm_spec,
        scratch_shapes=[pltpu.VMEM((RW_HEADS, RW_HEAD_DIM, RW_HEAD_DIM), F32),
                        pltpu.VMEM((RW_HEADS, tb, RW_HEAD_DIM), BF16), pltpu.VMEM((RW_HEADS, tb, RW_HEAD_DIM), F32),
                        pltpu.VMEM((RW_HEADS, tb, RW_HEAD_DIM), BF16), pltpu.VMEM((RW_HEADS, tb, RW_HEAD_DIM), F32),
                        pltpu.VMEM((RW_HEADS, tb // RW_CHUNK, 1, RW_HEAD_DIM), F32)],
        compiler_params=_cparams(("parallel", "arbitrary")),
        name="rwscan",
    )(r, lw, k, v, a, b, g, r_k.reshape(hshape), gn_gain.reshape(hshape), gn_bias.reshape(hshape))


def _foxprep_kernel(z_ref, f_ref, fb_ref, qg_ref, kg_ref, q_o, k_o, v_o, carry_ref):
    tm = z_ref.shape[1]

    @pl.when(pl.program_id(1) == 0)
    def _():
        carry_ref[...] = jnp.zeros_like(carry_ref)

    z = z_ref[0]
    log_f = _log_sigmoid(f_ref[0] + fb_ref[...])
    cum = carry_ref[...] + _tri_cumsum(log_f, tm)
    carry_ref[...] = cum[tm - 1:tm, :]
    lane = lax.broadcasted_iota(I32, (tm, FOX_HEAD_DIM), 1)
    qg = qg_ref[...]
    kg = kg_ref[...]
    for h in range(FOX_HEADS):
        f_hi, f_mid, f_lo = (p.astype(F32) for p in _split3(cum[:, h:h + 1] * LOG2E))
        qx = jnp.where(lane == 0, f_hi, jnp.where(lane == 1, f_mid, jnp.where(lane == 2, f_lo,
                       jnp.where(lane < 6, 1.0, 0.0))))
        kx = jnp.where(lane < 3, 1.0, jnp.where(lane == 3, -f_hi, jnp.where(lane == 4, -f_mid,
                       jnp.where(lane == 5, -f_lo, 0.0))))
        qh = z[:, h * FOX_HEAD_DIM:(h + 1) * FOX_HEAD_DIM]
        kh = z[:, FOX_WIDTH + h * FOX_HEAD_DIM:FOX_WIDTH + (h + 1) * FOX_HEAD_DIM]
        qn = qh * lax.rsqrt(jnp.mean(qh * qh, axis=-1, keepdims=True) + EPS) * qg * (ATTN_SCALE * LOG2E)
        kn = kh * lax.rsqrt(jnp.mean(kh * kh, axis=-1, keepdims=True) + EPS) * kg
        q_o[0, h] = jnp.concatenate([qn, qx], axis=-1).astype(BF16)
        k_o[0, h] = jnp.concatenate([kn, kx], axis=-1).astype(BF16)
        v_o[0, h] = z[:, 2 * FOX_WIDTH + h * FOX_HEAD_DIM:2 * FOX_WIDTH + (h + 1) * FOX_HEAD_DIM].astype(BF16)


def _foxprep(z, f_bias_pad, q_gain, k_gain):
    bsz, t_len, _ = z.shape
    tm = min(512, t_len)
    qk = jax.ShapeDtypeStruct((bsz, FOX_HEADS, t_len, 2 * FOX_HEAD_DIM), BF16)
    vv = jax.ShapeDtypeStruct((bsz, FOX_HEADS, t_len, FOX_HEAD_DIM), BF16)
    qk_spec = pl.BlockSpec((1, FOX_HEADS, tm, 2 * FOX_HEAD_DIM), lambda b, i: (b, 0, i, 0))
    v_spec = pl.BlockSpec((1, FOX_HEADS, tm, FOX_HEAD_DIM), lambda b, i: (b, 0, i, 0))
    return pl.pallas_call(
        _foxprep_kernel,
        out_shape=(qk, qk, vv),
        grid=(bsz, t_len // tm),
        in_specs=[pl.BlockSpec((1, tm, 3 * FOX_WIDTH), lambda b, i: (b, i, Z_FOX // (3 * FOX_WIDTH))),
                  pl.BlockSpec((1, tm, LANES), lambda b, i: (b, i, Z_F // LANES)),
                  pl.BlockSpec((1, LANES), lambda b, i: (0, 0)),
                  pl.BlockSpec((1, FOX_HEAD_DIM), lambda b, i: (0, 0)),
                  pl.BlockSpec((1, FOX_HEAD_DIM), lambda b, i: (0, 0))],
        out_specs=(qk_spec, qk_spec, v_spec),
        scratch_shapes=[pltpu.VMEM((1, LANES), F32)],
        compiler_params=_cparams(("parallel", "arbitrary")),
        name="foxprep",
    )(z, z, f_bias_pad, q_gain.reshape(1, -1), k_gain.reshape(1, -1))


def _fox_kernel(q_ref, k_ref, v_ref, o_ref, m_ref, l_ref, acc_ref):
    i = pl.program_id(1)
    j = pl.program_id(2)
    tq = q_ref.shape[2]
    tk = k_ref.shape[2]

    @pl.when(j == 0)
    def _():
        m_ref[...] = jnp.full_like(m_ref, MASK_VALUE)
        l_ref[...] = jnp.zeros_like(l_ref)
        acc_ref[...] = jnp.zeros_like(acc_ref)

    n_col = tk // LANES

    def update(diagonal):
        if diagonal:
            ri = lax.broadcasted_iota(I32, (tq, tk), 0)
            ci = lax.broadcasted_iota(I32, (tq, tk), 1)
            keep = ri >= ci
        for h in range(FOX_HEADS):
            s = lax.dot_general(q_ref[0, h], k_ref[0, h], (((1,), (1,)), ((), ())), preferred_element_type=F32)
            if diagonal:
                s = jnp.where(keep, s, MASK_VALUE)
            m_prev = m_ref[h]
            m_new = jnp.maximum(m_prev, jnp.max(s, axis=-1, keepdims=True))
            alpha = jnp.exp2(m_prev - m_new)
            cols = [jnp.exp2(s[:, c * LANES:(c + 1) * LANES] - m_new) for c in range(n_col)]
            l_new = alpha * l_ref[h]
            for pc in cols:
                l_new = l_new + pc
            l_ref[h] = l_new
            p = jnp.concatenate(cols, axis=-1).astype(BF16)
            acc_ref[h] = (alpha[:, :FOX_HEAD_DIM] * acc_ref[h]
                          + jnp.dot(p, v_ref[0, h], preferred_element_type=F32))
            m_ref[h] = m_new

    @pl.when(j < i)
    def _():
        update(False)

    @pl.when(j == i)
    def _():
        update(True)
        for h in range(FOX_HEADS):
            denom = jnp.sum(l_ref[h], axis=-1, keepdims=True)
            o_ref[0, :, h * FOX_HEAD_DIM:(h + 1) * FOX_HEAD_DIM] = (acc_ref[h] / denom).astype(o_ref.dtype)


def _fox(q, k, v):
    bsz, _, t_len, _ = q.shape
    tq = min(512, t_len)
    n_blk = t_len // tq
    return pl.pallas_call(
        _fox_kernel,
        out_shape=jax.ShapeDtypeStruct((bsz, t_len, FOX_WIDTH), BF16),
        grid=(bsz, n_blk, n_blk),
        in_specs=[pl.BlockSpec((1, FOX_HEADS, tq, 2 * FOX_HEAD_DIM), lambda b, i, j: (b, 0, i, 0)),
                  pl.BlockSpec((1, FOX_HEADS, tq, 2 * FOX_HEAD_DIM), lambda b, i, j: (b, 0, jnp.minimum(j, i), 0)),
                  pl.BlockSpec((1, FOX_HEADS, tq, FOX_HEAD_DIM), lambda b, i, j: (b, 0, jnp.minimum(j, i), 0))],
        out_specs=pl.BlockSpec((1, tq, FOX_WIDTH), lambda b, i, j: (b, i, 0)),
        scratch_shapes=[pltpu.VMEM((FOX_HEADS, tq, LANES), F32), pltpu.VMEM((FOX_HEADS, tq, LANES), F32),
                        pltpu.VMEM((FOX_HEADS, tq, FOX_HEAD_DIM), F32)],
        compiler_params=_cparams(("parallel", "parallel", "arbitrary")),
        name="fox",
    )(q, k, v)


def _merge_kernel(zg_ref, ygm_ref, yrw_ref, yfox_ref, x_ref, g1_ref, sc2_ref, sh2_ref, pb_ref, wo_ref, wr_ref, br_ref,
                  x1_o, h2_o, idx_o, gate_o, rank_o, cnt_o, carry_ref):
    tm = x_ref.shape[1]

    @pl.when((pl.program_id(0) == 0) & (pl.program_id(1) == 0))
    def _():
        carry_ref[...] = jnp.zeros_like(carry_ref)

    sg = jax.nn.sigmoid(zg_ref[0])
    p_gm = jnp.dot(ygm_ref[0], pb_ref[0:GM_WIDTH, :], preferred_element_type=F32)
    p_rw = jnp.zeros((tm, D_MODEL), F32)
    for h in range(RW_HEADS):
        lo = GM_WIDTH + h * RW_HEAD_DIM
        p_rw = p_rw + jnp.dot(yrw_ref[0, h], pb_ref[lo:lo + RW_HEAD_DIM, :], preferred_element_type=F32)
    p_fox = jnp.dot(yfox_ref[0], pb_ref[GM_WIDTH + RW_WIDTH:, :], preferred_element_type=F32)
    merged = sg[:, 0:D_MODEL] * p_gm + sg[:, D_MODEL:2 * D_MODEL] * p_rw + sg[:, 2 * D_MODEL:] * p_fox
    x1 = x_ref[0] + g1_ref[0] * jnp.dot(merged.astype(BF16), wo_ref[...], preferred_element_type=F32)
    x1_o[0] = x1
    h2 = x1 * lax.rsqrt(jnp.mean(x1 * x1, axis=-1, keepdims=True) + EPS) * (1.0 + sc2_ref[0]) + sh2_ref[0]
    h2_o[0] = h2

    logits = jnp.dot(h2, wr_ref[...], preferred_element_type=F32, precision=HIGHEST) + br_ref[...]
    lane = lax.broadcasted_iota(I32, (tm, N_EXPERTS), 1)
    vals, idxs = [], []
    rest = logits
    for _ in range(TOP_K):
        m = jnp.max(rest, axis=-1, keepdims=True)
        am = jnp.min(jnp.where(rest == m, lane, N_EXPERTS), axis=-1, keepdims=True)
        vals.append(m)
        idxs.append(am)
        rest = jnp.where(lane == am, -jnp.inf, rest)
    exps = [jnp.exp(val - vals[0]) for val in vals]
    denom = exps[0] + exps[1] + exps[2] + exps[3]

    onehot = jnp.zeros((tm, N_EXPERTS), F32)
    for am in idxs:
        onehot = onehot + jnp.where(lane == am, 1.0, 0.0)
    ri = lax.broadcasted_iota(I32, (tm, tm), 0)
    ci = lax.broadcasted_iota(I32, (tm, tm), 1)
    before = jnp.where(ri > ci, 1.0, 0.0).astype(BF16)
    seen = carry_ref[...] + jnp.dot(before, onehot.astype(BF16), preferred_element_type=F32)
    lane_k = lax.broadcasted_iota(I32, (tm, TOP_K), 1)
    idx_out = jnp.zeros((tm, TOP_K), I32)
    gate_out = jnp.zeros((tm, TOP_K), F32)
    rank_out = jnp.zeros((tm, TOP_K), I32)
    for kk in range(TOP_K):
        rank = jnp.sum(jnp.where(lane == idxs[kk], seen, 0.0), axis=-1, keepdims=True).astype(I32)
        idx_out = jnp.where(lane_k == kk, idxs[kk], idx_out)
        gate_out = jnp.where(lane_k == kk, exps[kk] / denom, gate_out)
        rank_out = jnp.where(lane_k == kk, rank, rank_out)
    idx_o[0] = idx_out
    gate_o[0] = gate_out
    rank_o[0] = rank_out
    total = carry_ref[...] + jnp.sum(onehot, axis=0, keepdims=True)
    carry_ref[...] = total
    cnt_o[...] = total.astype(I32)


def _merge(z, y_gm, y_rw, y_fox, x, gate1, scale2, shift2, w_branch, w_o, w_router, b_router):
    bsz, t_len, d = x.shape
    tm = min(512, t_len)
    row = lambda w: pl.BlockSpec((1, tm, w), lambda b, i: (b, i, 0))
    mod = pl.BlockSpec((1, 1, d), lambda b, i: (b, 0, 0))
    full = lambda shape: pl.BlockSpec(shape, lambda b, i: (0,) * len(shape))
    return pl.pallas_call(
        _merge_kernel,
        out_shape=(jax.ShapeDtypeStruct((bsz, t_len, d), F32), jax.ShapeDtypeStruct((bsz, t_len, d), F32),
                   jax.ShapeDtypeStruct((bsz, t_len, TOP_K), I32), jax.ShapeDtypeStruct((bsz, t_len, TOP_K), F32),
                   jax.ShapeDtypeStruct((bsz, t_len, TOP_K), I32), jax.ShapeDtypeStruct((1, N_EXPERTS), I32)),
        grid=(bsz, t_len // tm),
        in_specs=[row(N_BRANCH * D_MODEL), row(GM_WIDTH),
                  pl.BlockSpec((1, RW_HEADS, tm, RW_HEAD_DIM), lambda b, i: (b, 0, i, 0)),
                  row(FOX_WIDTH), row(d), mod, mod, mod,
                  full(w_branch.shape), full(w_o.shape), full(w_router.shape), full((1, N_EXPERTS))],
        out_specs=(row(d), row(d), row(TOP_K), row(TOP_K), row(TOP_K), full((1, N_EXPERTS))),
        scratch_shapes=[pltpu.VMEM((1, N_EXPERTS), F32)],
        compiler_params=_cparams(("arbitrary", "arbitrary")),
        name="merge_router",
    )(z, y_gm, y_rw, y_fox, x, gate1, scale2, shift2, w_branch, w_o, w_router, b_router.reshape(1, N_EXPERTS))


def _sc_mesh():
    return plsc.VectorSubcoreMesh(core_axis_name="c", subcore_axis_name="s",
                                  num_cores=SC_CORES, num_subcores=SC_SUBCORES)


def _sc_worker():
    return lax.axis_index("s") * SC_CORES + lax.axis_index("c")


def _sc_scatter_rows(src, idx3, n_out):
    _, d = src.shape
    n_copy, n_grp, _ = idx3.shape
    grp_per_w = n_grp // SC_WORKERS

    def body(src_hbm, idx_hbm, out_hbm, idx_v, rows_v):
        g0 = _sc_worker() * grp_per_w
        for q in range(n_copy):
            pltpu.sync_copy(idx_hbm.at[q, pl.ds(g0, grp_per_w)], idx_v.at[pl.ds(q * grp_per_w, grp_per_w)])

        @pl.loop(0, grp_per_w)
        def _(j):
            pltpu.sync_copy(src_hbm.at[pl.ds((g0 + j) * SC_ROWS, SC_ROWS)], rows_v)
            for q in range(n_copy):
                pltpu.sync_copy(rows_v, out_hbm.at[idx_v.at[q * grp_per_w + j]])

    return pl.kernel(
        body, out_type=jax.ShapeDtypeStruct((n_out, d), src.dtype), mesh=_sc_mesh(),
        scratch_types=[pltpu.VMEM((n_copy * grp_per_w, SC_ROWS), I32), pltpu.VMEM((SC_ROWS, d), src.dtype)],
        name="sc_dispatch",
    )(src, idx3)


def _sc_gather_rows(table, idx2):
    _, d = table.shape
    n_grp, _ = idx2.shape
    grp_per_w = n_grp // SC_WORKERS

    def body(table_hbm, idx_hbm, out_hbm, idx_v, rows_v):
        g0 = _sc_worker() * grp_per_w
        pltpu.sync_copy(idx_hbm.at[pl.ds(g0, grp_per_w)], idx_v)

        @pl.loop(0, grp_per_w)
        def _(j):
            pltpu.sync_copy(table_hbm.at[idx_v.at[j]], rows_v)
            pltpu.sync_copy(rows_v, out_hbm.at[pl.ds((g0 + j) * SC_ROWS, SC_ROWS)])

    return pl.kernel(
        body, out_type=jax.ShapeDtypeStruct((n_grp * SC_ROWS, d), table.dtype), mesh=_sc_mesh(),
        scratch_types=[pltpu.VMEM((grp_per_w, SC_ROWS), I32), pltpu.VMEM((SC_ROWS, d), table.dtype)],
        name="sc_combine_gather",
    )(table, idx2)


def _ffn_kernel(be_ref, nv_ref, x_ref, wgu_ref, bgu_ref, wd_ref, bd_ref, o_ref):
    n_valid = nv_ref[pl.program_id(0)]

    @pl.when(n_valid > 0)
    def _():
        rowid = lax.broadcasted_iota(I32, x_ref.shape, 0)
        x = jnp.where(rowid < n_valid, x_ref[...], 0.0).astype(BF16)
        gu = jnp.dot(x, wgu_ref[0], preferred_element_type=F32) + bgu_ref[0]
        g_ = jnp.minimum(gu[:, :D_FF], SWIGLU_LIMIT)
        u_ = jnp.clip(gu[:, D_FF:], -SWIGLU_LIMIT, SWIGLU_LIMIT)
        act = (u_ + 1.0) * (g_ * jax.nn.sigmoid(SWIGLU_ALPHA * g_))
        o_ref[...] = jnp.dot(act.astype(BF16), wd_ref[0], preferred_element_type=F32) + bd_ref[0]

    @pl.when(n_valid <= 0)
    def _():
        o_ref[...] = jnp.zeros_like(o_ref)


def _ffn(block_expert, block_valid, xin, w_gate_up, b_gate_up, w_down, b_down):
    n_rows, d = xin.shape
    n_blocks = n_rows // MOE_BLOCK
    grid_spec = pltpu.PrefetchScalarGridSpec(
        num_scalar_prefetch=2,
        grid=(n_blocks,),
        in_specs=[pl.BlockSpec((MOE_BLOCK, d), lambda i, be, nv: (i, 0)),
                  pl.BlockSpec((1, d, 2 * D_FF), lambda i, be, nv: (be[i], 0, 0)),
                  pl.BlockSpec((1, 1, 2 * D_FF), lambda i, be, nv: (be[i], 0, 0)),
                  pl.BlockSpec((1, D_FF, d), lambda i, be, nv: (be[i], 0, 0)),
                  pl.BlockSpec((1, 1, d), lambda i, be, nv: (be[i], 0, 0))],
        out_specs=pl.BlockSpec((MOE_BLOCK, d), lambda i, be, nv: (i, 0)),
    )
    return pl.pallas_call(
        _ffn_kernel,
        out_shape=jax.ShapeDtypeStruct((n_rows, d), F32),
        grid_spec=grid_spec,
        compiler_params=_cparams(("arbitrary",)),
        name="expert_ffn",
    )(block_expert, block_valid, xin, w_gate_up, b_gate_up.reshape(N_EXPERTS, 1, -1), w_down,
      b_down.reshape(N_EXPERTS, 1, -1))


def _combine_kernel(x1_ref, g2_ref, gate_ref, yg_ref, o_ref):
    gate = gate_ref[0]
    y = gate[:, 0:1] * yg_ref[0, 0]
    for q in range(1, TOP_K):
        y = y + gate[:, q:q + 1] * yg_ref[q, 0]
    o_ref[0] = x1_ref[0] + g2_ref[0] * y


def _combine(x1, gate2, gate, yg):
    bsz, t_len, d = x1.shape
    tm = min(512, t_len)
    return pl.pallas_call(
        _combine_kernel,
        out_shape=jax.ShapeDtypeStruct((bsz, t_len, d), F32),
        grid=(bsz, t_len // tm),
        in_specs=[pl.BlockSpec((1, tm, d), lambda b, i: (b, i, 0)),
                  pl.BlockSpec((1, 1, d), lambda b, i: (b, 0, 0)),
                  pl.BlockSpec((1, tm, TOP_K), lambda b, i: (b, i, 0)),
                  pl.BlockSpec((TOP_K, 1, tm, d), lambda b, i: (0, b, i, 0))],
        out_specs=pl.BlockSpec((1, tm, d), lambda b, i: (b, i, 0)),
        compiler_params=_cparams(("parallel", "parallel")),
        name="moe_combine",
    )(x1, gate2, gate, yg)


def _moe(x1, gate2, h2, top_idx, gate, rank, counts, w_gate_up, b_gate_up, w_down, b_down):
    bsz, t_len, d = h2.shape
    n_tok = bsz * t_len
    n_assign = n_tok * TOP_K
    n_blocks = -(-n_assign // MOE_BLOCK) + N_EXPERTS
    counts = counts.reshape(N_EXPERTS)
    blocks_e = (counts + MOE_BLOCK - 1) // MOE_BLOCK
    blk_end = jnp.cumsum(blocks_e)
    blk_start = blk_end - blocks_e
    experts = jnp.arange(N_EXPERTS, dtype=I32)
    onehot = top_idx.reshape(n_tok, TOP_K, 1) == experts
    dest = jnp.sum(jnp.where(onehot, blk_start * MOE_BLOCK, 0), axis=-1) + rank.reshape(n_tok, TOP_K)
    dest_t = dest.T.astype(I32)
    blk = jnp.arange(n_blocks, dtype=I32)
    block_expert = jnp.minimum(jnp.sum(blk_end[None, :] <= blk[:, None], axis=1), N_EXPERTS - 1).astype(I32)
    be_hot = block_expert[:, None] == experts
    cnt_b = jnp.sum(jnp.where(be_hot, counts, 0), axis=1)
    start_b = jnp.sum(jnp.where(be_hot, blk_start, 0), axis=1)
    block_valid = jnp.clip(cnt_b - (blk - start_b) * MOE_BLOCK, 0, MOE_BLOCK).astype(I32)
    xin = _sc_scatter_rows(h2.reshape(n_tok, d), dest_t.reshape(TOP_K, n_tok // SC_ROWS, SC_ROWS),
                           n_blocks * MOE_BLOCK)
    yb = _ffn(block_expert, block_valid, xin, w_gate_up, b_gate_up, w_down, b_down)
    yg = _sc_gather_rows(yb, dest_t.reshape(n_assign // SC_ROWS, SC_ROWS))
    return _combine(x1, gate2, gate, yg.reshape(TOP_K, bsz, t_len, d))


def _permute_w_in(w_in):
    n_layer, d, _ = w_in.shape
    o_gm = 0
    o_rw = o_gm + 2 * GM_WIDTH
    o_fox = o_rw + RW_SHIFT_WIDTH
    o_f = o_fox + 3 * FOX_WIDTH
    o_gate = o_f + FOX_HEADS
    parts = [w_in[:, :, o_gate:o_gate + N_BRANCH * D_MODEL], w_in[:, :, o_fox:o_f], w_in[:, :, o_gm:o_rw],
             w_in[:, :, o_rw:o_fox], w_in[:, :, o_f:o_gate],
             jnp.zeros((n_layer, d, Z_WIDTH - Z_F - FOX_HEADS), w_in.dtype)]
    return jnp.concatenate(parts, axis=-1).astype(BF16)


def _layer(x, mod, w_in_p, gm_v_gain, gm_w_s, gm_b_s, mu_pad, w_lora, rw_w0, rw_a0, rw_k_k, rw_k_a, rw_r_k,
           rw_gn_gain, rw_gn_bias, f_bias_pad, fox_q_gain, fox_k_gain, w_branch, w_o, w_router, b_router,
           w_gate_up, b_gate_up, w_down, b_down):
    shift1, scale1, gate1, shift2, scale2, gate2 = (mod[:, i][:, None, :] for i in range(6))
    z = _inproj(x, scale1, shift1, w_in_p)
    y_gm = _gmlp(z, gm_v_gain, gm_w_s, gm_b_s)
    r, lw, k, v, a, b, g = _rwprep(z, mu_pad, w_lora, rw_w0, rw_a0, rw_k_k, rw_k_a)
    y_rw = _rwscan(r, lw, k, v, a, b, g, rw_r_k, rw_gn_gain, rw_gn_bias)
    q, kf, vf = _foxprep(z, f_bias_pad, fox_q_gain, fox_k_gain)
    y_fox = _fox(q, kf, vf)
    x1, h2, top_idx, gate, rank, counts = _merge(z, y_gm, y_rw, y_fox, x, gate1, scale2, shift2,
                                                 w_branch, w_o, w_router, b_router)
    return _moe(x1, gate2, h2, top_idx, gate, rank, counts, w_gate_up, b_gate_up, w_down, b_down)


def kernel(x, c, w_ada, b_ada, w_in, gm_v_gain, gm_w_s, gm_b_s, rw_mu, rw_w0, rw_w2, rw_a0, rw_a2, rw_g2, rw_k_k,
           rw_k_a, rw_r_k, rw_gn_gain, rw_gn_bias, fox_f_bias, fox_q_gain, fox_k_gain, w_branch, w_o, w_router,
           b_router, w_gate_up, b_gate_up, w_down, b_down):
    n_layer = w_ada.shape[0]
    bsz = x.shape[0]
    c_pad = jnp.zeros((8, D_MODEL), F32).at[:bsz].set(c)
    mod = _adaln(c_pad, w_ada, b_ada)[:, :bsz].reshape(n_layer, bsz, 6, D_MODEL)
    w_in_p = _permute_w_in(w_in)
    mu_pad = jnp.pad(rw_mu, ((0, 0), (0, RW_BLOCK - RW_SHIFT_WIDTH)))
    w_lora = jnp.zeros((n_layer, RW_LORA, 3 * RW_WIDTH), F32)
    w_lora = w_lora.at[:, 0:RW_DECAY_LORA, 0:RW_WIDTH].set(rw_w2)
    w_lora = w_lora.at[:, RW_DECAY_LORA:RW_DECAY_LORA + RW_ICLR_LORA, RW_WIDTH:2 * RW_WIDTH].set(rw_a2)
    w_lora = w_lora.at[:, RW_DECAY_LORA + RW_ICLR_LORA:, 2 * RW_WIDTH:].set(rw_g2)
    f_bias_pad = jnp.pad(fox_f_bias, ((0, 0), (0, LANES - FOX_HEADS)))
    w_branch_b = w_branch.astype(BF16)
    w_o_b = w_o.astype(BF16)
    w_gu_b = w_gate_up.astype(BF16)
    w_dn_b = w_down.astype(BF16)
    for l in range(n_layer):
        x = _layer(x, mod[l], w_in_p[l], gm_v_gain[l], gm_w_s[l], gm_b_s[l], mu_pad[l:l + 1], w_lora[l], rw_w0[l],
                   rw_a0[l], rw_k_k[l], rw_k_a[l], rw_r_k[l], rw_gn_gain[l], rw_gn_bias[l], f_bias_pad[l:l + 1],
                   fox_q_gain[l], fox_k_gain[l], w_branch_b[l], w_o_b[l], w_router[l], b_router[l],
                   w_gu_b[l], b_gate_up[l], w_dn_b[l], b_down[l])
    return x
```

```python
import functools

import jax
import jax.numpy as jnp
from jax import lax
from jax.experimental import pallas as pl
from jax.experimental.pallas import tpu as pltpu
from jax.experimental.pallas import tpu_sc as plsc

F32 = jnp.float32
BF16 = jnp.bfloat16
I32 = jnp.int32
HIGHEST = lax.Precision.HIGHEST

D_MODEL = 1024
GM_CHUNK = 128
GM_GROUPS = 4
GM_WIDTH = 256
GM_GROUP_DIM = GM_WIDTH // GM_GROUPS
RW_HEADS = 4
RW_HEAD_DIM = 64
RW_WIDTH = RW_HEADS * RW_HEAD_DIM
RW_DECAY_LORA = 32
RW_ICLR_LORA = 32
RW_GATE_LORA = 64
RW_LORA = RW_DECAY_LORA + RW_ICLR_LORA + RW_GATE_LORA
RW_SHIFT_WIDTH = 3 * RW_WIDTH + RW_LORA
RW_GN_EPS = 64e-5
FOX_HEADS = 8
FOX_HEAD_DIM = 64
FOX_WIDTH = FOX_HEADS * FOX_HEAD_DIM
ATTN_SCALE = FOX_HEAD_DIM ** -0.5
MASK_VALUE = -1e30
LOG2E = 1.4426950408889634
N_BRANCH = 3
N_EXPERTS = 32
TOP_K = 4
D_FF = D_MODEL
SWIGLU_LIMIT = 7.0
SWIGLU_ALPHA = 1.702
MOE_BLOCK = 256
EPS = 1e-6

Z_GATE = 0
Z_FOX = N_BRANCH * D_MODEL
Z_GM = Z_FOX + 3 * FOX_WIDTH
Z_RW = Z_GM + 2 * GM_WIDTH
RW_BLOCK = 1024
Z_F = Z_RW + RW_SHIFT_WIDTH
Z_WIDTH = Z_RW + RW_BLOCK
LANES = 128
RW_CHUNK = 64
RW_PREP_UNROLL = 4

VMEM_LIMIT = 48 * 1024 * 1024
FFN_VMEM_LIMIT = 56 * 1024 * 1024
SC_CORES = 2
SC_SUBCORES = 16
SC_WORKERS = SC_CORES * SC_SUBCORES
SC_ROWS = 32


def _cparams(sem):
    return pltpu.CompilerParams(dimension_semantics=sem, vmem_limit_bytes=VMEM_LIMIT)


def _mm(a, b):
    return jnp.dot(a.astype(BF16), b.astype(BF16), preferred_element_type=F32)


def _mm_nt(a, b):
    return lax.dot_general(a.astype(BF16), b.astype(BF16), (((1,), (1,)), ((), ())), preferred_element_type=F32)


def _mm_tn(a, b):
    return lax.dot_general(a.astype(BF16), b.astype(BF16), (((0,), (0,)), ((), ())), preferred_element_type=F32)


def _split3(x):
    hi = x.astype(BF16)
    r1 = x - hi.astype(F32)
    mid = r1.astype(BF16)
    lo = (r1 - mid.astype(F32)).astype(BF16)
    return hi, mid, lo


def _tri_cumsum(x, n):
    ri = lax.broadcasted_iota(I32, (n, n), 0)
    ci = lax.broadcasted_iota(I32, (n, n), 1)
    ones = jnp.where(ri >= ci, 1.0, 0.0).astype(BF16)
    hi, mid, lo = _split3(x)
    return (jnp.dot(ones, hi, preferred_element_type=F32) + jnp.dot(ones, mid, preferred_element_type=F32)
            + jnp.dot(ones, lo, preferred_element_type=F32))


def _log_sigmoid(x):
    return jnp.minimum(x, 0.0) - jnp.log1p(jnp.exp(-jnp.abs(x)))


def _adaln_kernel(c_ref, w_ref, b_ref, o_ref):
    c = c_ref[...]
    s = c * jax.nn.sigmoid(c)
    o_ref[0] = jnp.dot(s, w_ref[0], preferred_element_type=F32, precision=HIGHEST) + b_ref[0]


def _adaln(c_pad, w_ada, b_ada):
    n_layer, d, w6 = w_ada.shape
    tn = 1536
    return pl.pallas_call(
        _adaln_kernel,
        out_shape=jax.ShapeDtypeStruct((n_layer, c_pad.shape[0], w6), F32),
        grid=(n_layer, w6 // tn),
        in_specs=[pl.BlockSpec(c_pad.shape, lambda l, j: (0, 0)),
                  pl.BlockSpec((1, d, tn), lambda l, j: (l, 0, j)),
                  pl.BlockSpec((1, 1, tn), lambda l, j: (l, 0, j))],
        out_specs=pl.BlockSpec((1, c_pad.shape[0], tn), lambda l, j: (l, 0, j)),
        compiler_params=_cparams(("parallel", "parallel")),
        name="adaln",
    )(c_pad, w_ada, b_ada.reshape(n_layer, 1, w6))


def _inproj_kernel(x_ref, sc_ref, sh_ref, w_ref, o_ref, xn_ref):
    @pl.when(pl.program_id(2) == 0)
    def _():
        x = x_ref[0]
        xn = x * lax.rsqrt(jnp.mean(x * x, axis=-1, keepdims=True) + EPS)
        xn_ref[...] = (xn * (1.0 + sc_ref[0]) + sh_ref[0]).astype(BF16)

    o_ref[0] = jnp.dot(xn_ref[...], w_ref[...], preferred_element_type=F32)


def _inproj(x, scale, shift, w):
    bsz, t_len, d = x.shape
    tm = min(1024, t_len)
    tn = 1024
    return pl.pallas_call(
        _inproj_kernel,
        out_shape=jax.ShapeDtypeStruct((bsz, t_len, Z_WIDTH), F32),
        grid=(bsz, t_len // tm, Z_WIDTH // tn),
        in_specs=[pl.BlockSpec((1, tm, d), lambda b, i, j: (b, i, 0)),
                  pl.BlockSpec((1, 1, d), lambda b, i, j: (b, 0, 0)),
                  pl.BlockSpec((1, 1, d), lambda b, i, j: (b, 0, 0)),
                  pl.BlockSpec((d, tn), lambda b, i, j: (0, j))],
        out_specs=pl.BlockSpec((1, tm, tn), lambda b, i, j: (b, i, j)),
        scratch_shapes=[pltpu.VMEM((tm, d), BF16)],
        compiler_params=_cparams(("parallel", "parallel", "arbitrary")),
        name="inproj",
    )(x, scale, shift, w)


def _gmlp_kernel(z_ref, gain_ref, ws_ref, bst_ref, o_ref):
    tm = z_ref.shape[1]
    z = z_ref[0]
    u = jax.nn.gelu(z[:, :GM_WIDTH])
    v = jax.nn.gelu(z[:, GM_WIDTH:])
    v = v * lax.rsqrt(jnp.mean(v * v, axis=-1, keepdims=True) + EPS) * gain_ref[...]
    vb = v.astype(BF16)
    grp = lax.broadcasted_iota(I32, (GM_CHUNK, GM_WIDTH), 1) // GM_GROUP_DIM
    ri = lax.broadcasted_iota(I32, (GM_CHUNK, GM_CHUNK), 0)
    ci = lax.broadcasted_iota(I32, (GM_CHUNK, GM_CHUNK), 1)
    causal = ri >= ci
    bias = jnp.zeros((GM_CHUNK, GM_WIDTH), F32)
    ws = []
    for g in range(GM_GROUPS):
        ws.append(jnp.where(causal, ws_ref[g], 0.0).astype(BF16))
        bias = jnp.where(grp == g, bst_ref[:, g:g + 1], bias)
    for c in range(tm // GM_CHUNK):
        rows = slice(c * GM_CHUNK, (c + 1) * GM_CHUNK)
        vc = vb[rows]
        mixed = bias
        for g in range(GM_GROUPS):
            m = jnp.dot(ws[g], vc, preferred_element_type=F32)
            mixed = mixed + jnp.where(grp == g, m, 0.0)
        o_ref[0, rows, :] = (u[rows] * mixed).astype(o_ref.dtype)


def _gmlp(z, gain, w_s, b_s):
    bsz, t_len, _ = z.shape
    tm = min(512, t_len)
    return pl.pallas_call(
        _gmlp_kernel,
        out_shape=jax.ShapeDtypeStruct((bsz, t_len, GM_WIDTH), BF16),
        grid=(bsz, t_len // tm),
        in_specs=[pl.BlockSpec((1, tm, 2 * GM_WIDTH), lambda b, i: (b, i, Z_GM // (2 * GM_WIDTH))),
                  pl.BlockSpec((1, GM_WIDTH), lambda b, i: (0, 0)),
                  pl.BlockSpec((GM_GROUPS, GM_CHUNK, GM_CHUNK), lambda b, i: (0, 0, 0)),
                  pl.BlockSpec((GM_CHUNK, GM_GROUPS), lambda b, i: (0, 0))],
        out_specs=pl.BlockSpec((1, tm, GM_WIDTH), lambda b, i: (b, i, 0)),
        compiler_params=_cparams(("parallel", "parallel")),
        name="gmlp",
    )(z, gain.reshape(1, GM_WIDTH), w_s, b_s.T)


def _rwprep_kernel(z_ref, zp_ref, mu_ref, wl_ref, w0_ref, a0_ref, kk_ref, ka_ref,
                   r_o, lw_o, k_o, v_o, a_o, b_o, g_o):
    tm = z_ref.shape[1]
    z = z_ref[0]
    prev = jnp.where(pl.program_id(1) > 0, zp_ref[0, 7:8, :], 0.0)
    rowid = lax.broadcasted_iota(I32, z.shape, 0)
    zs = jnp.where(rowid == 0, prev, pltpu.roll(z, 1, axis=0))
    zz = z + mu_ref[...] * (zs - z)
    r = zz[:, 0:RW_WIDTH]
    k = zz[:, RW_WIDTH:2 * RW_WIDTH]
    v = zz[:, 2 * RW_WIDTH:3 * RW_WIDTH]
    lo = zz[:, 3 * RW_WIDTH:3 * RW_WIDTH + RW_LORA]
    lane = lax.broadcasted_iota(I32, (tm, RW_LORA), 1)
    act = jnp.where(lane < RW_DECAY_LORA, jnp.tanh(lo),
                    jnp.where(lane < RW_DECAY_LORA + RW_ICLR_LORA, lo, jax.nn.sigmoid(lo)))
    proj = jnp.dot(act, wl_ref[...], preferred_element_type=F32, precision=HIGHEST)
    xw = -(w0_ref[...] + proj[:, 0:RW_WIDTH])
    softplus = jnp.maximum(xw, 0.0) + jnp.log1p(jnp.exp(-jnp.abs(xw)))
    lw = -jnp.exp(-softplus - 0.5)
    a = jax.nn.sigmoid(a0_ref[...] + proj[:, RW_WIDTH:2 * RW_WIDTH])
    g = proj[:, 2 * RW_WIDTH:3 * RW_WIDTH]
    kk = k * kk_ref[...]
    k2 = k * (1.0 + (a - 1.0) * ka_ref[...])
    for h in range(RW_HEADS):
        sl = slice(h * RW_HEAD_DIM, (h + 1) * RW_HEAD_DIM)
        kkh = kk[:, sl]
        nrm = jnp.sqrt(jnp.sum(kkh * kkh, axis=-1, keepdims=True))
        kkh = kkh / jnp.maximum(nrm, 1e-12)
        r_o[0, h] = r[:, sl]
        lw_o[0, h] = lw[:, sl]
        k_o[0, h] = k2[:, sl]
        v_o[0, h] = v[:, sl]
        a_o[0, h] = -kkh
        b_o[0, h] = kkh * a[:, sl]
        g_o[0, h] = g[:, sl]


def _rwprep(z, mu_pad, w_lora, w0, a0, k_k, k_a):
    bsz, t_len, _ = z.shape
    tm = min(512, t_len)
    hm = jax.ShapeDtypeStruct((bsz, RW_HEADS, t_len, RW_HEAD_DIM), F32)
    hm_spec = pl.BlockSpec((1, RW_HEADS, tm, RW_HEAD_DIM), lambda b, i: (b, 0, i, 0))
    vec = lambda n: pl.BlockSpec((1, n), lambda b, i: (0, 0))
    rw_blk = Z_RW // RW_BLOCK
    return pl.pallas_call(
        _rwprep_kernel,
        out_shape=(hm,) * 7,
        grid=(bsz, t_len // tm),
        in_specs=[pl.BlockSpec((1, tm, RW_BLOCK), lambda b, i: (b, i, rw_blk)),
                  pl.BlockSpec((1, 8, RW_BLOCK), lambda b, i: (b, jnp.maximum(i * (tm // 8) - 1, 0), rw_blk)),
                  vec(RW_BLOCK),
                  pl.BlockSpec((RW_LORA, 3 * RW_WIDTH), lambda b, i: (0, 0)),
                  vec(RW_WIDTH), vec(RW_WIDTH), vec(RW_WIDTH), vec(RW_WIDTH)],
        out_specs=(hm_spec,) * 7,
        compiler_params=_cparams(("parallel", "parallel")),
        name="rwprep",
    )(z, z, mu_pad, w_lora, w0.reshape(1, -1), a0.reshape(1, -1), k_k.reshape(1, -1), k_a.reshape(1, -1))


def _rwscan_kernel(r_ref, lw_ref, k_ref, v_ref, a_ref, b_ref, g_ref, rk_ref, gg_ref, gb_ref, o_ref,
                   s_ref, rp_ref, y_ref, gm_ref, h0_ref, we_ref):
    cl = RW_CHUNK
    tb = r_ref.shape[2]
    n_chunk = tb // cl

    @pl.when(pl.program_id(1) == 0)
    def _():
        s_ref[...] = jnp.zeros_like(s_ref)

    n = RW_HEADS * cl
    ri = lax.broadcasted_iota(I32, (n, n), 0)
    ci = lax.broadcasted_iota(I32, (n, n), 1)
    same_head = (ri // cl) == (ci // cl)
    lower = same_head & (ri >= ci)
    strict = same_head & (ri > ci)
    eye = jnp.where(ri == ci, 1.0, 0.0)
    ones_lower = jnp.where(lower, 1.0, 0.0).astype(BF16)

    def prepare(chunks):
        grp = range(len(chunks))
        each = lambda fn: [fn(u) for u in grp]
        rows = [pl.ds(pl.multiple_of(c * cl, cl), cl) for c in chunks]
        stack = lambda ref: each(lambda u: ref[0, :, rows[u], :].reshape(n, RW_HEAD_DIM))
        r, lw, k, v, a, b = (stack(ref) for ref in (r_ref, lw_ref, k_ref, v_ref, a_ref, b_ref))
        parts = each(lambda u: _split3(lw[u]))
        tri = lambda x: jnp.dot(ones_lower, x, preferred_element_type=F32)
        cw = each(lambda u: tri(parts[u][0]) + tri(parts[u][1]) + tri(parts[u][2]))
        w_in = each(lambda u: jnp.exp(cw[u]))
        w_inv = each(lambda u: jnp.exp(-cw[u]))
        rt = each(lambda u: r[u] * w_in[u])
        at = each(lambda u: a[u] * jnp.exp(cw[u] - lw[u]))
        kt = each(lambda u: k[u] * w_inv[u])
        bt = each(lambda u: b[u] * w_inv[u])
        w_end = each(lambda u: w_in[u].reshape(RW_HEADS, cl, RW_HEAD_DIM)[:, cl - 1:cl, :])
        w_end_rows = each(lambda u: jnp.broadcast_to(w_end[u], (RW_HEADS, cl, RW_HEAD_DIM)).reshape(n, RW_HEAD_DIM))
        a_ab = each(lambda u: jnp.where(strict, _mm_nt(at[u], bt[u]), 0.0))
        a_ak = each(lambda u: jnp.where(strict, _mm_nt(at[u], kt[u]), 0.0))
        m_rb = each(lambda u: jnp.where(lower, _mm_nt(rt[u], bt[u]), 0.0))
        m_rk = each(lambda u: jnp.where(lower, _mm_nt(rt[u], kt[u]), 0.0))
        inv = each(lambda u: eye + a_ab[u])
        p = a_ab
        for _ in range(cl.bit_length() - 2):
            p = [_mm(p[u], p[u]) for u in grp]
            inv = [inv[u] + _mm(inv[u], p[u]) for u in grp]
        akv = each(lambda u: _mm(a_ak[u], v[u]))
        ap = each(lambda u: _mm(inv[u], at[u]))
        z0 = each(lambda u: _mm(inv[u], akv[u]))
        bend = each(lambda u: bt[u] * w_end_rows[u])
        kend = each(lambda u: kt[u] * w_end_rows[u])
        rp = each(lambda u: (rt[u] + _mm(m_rb[u], ap[u])).astype(BF16))
        y0 = each(lambda u: _mm(m_rb[u], z0[u]) + _mm(m_rk[u], v[u]))
        for u in grp:
            for h in range(RW_HEADS):
                hs = slice(h * cl, (h + 1) * cl)
                rp_ref[h, rows[u], :] = rp[u][hs]
                y_ref[h, rows[u], :] = y0[u][hs]
                gm_ref[h, rows[u], :] = _mm_tn(ap[u][hs], bend[u][hs]).astype(BF16)
                h0_ref[h, rows[u], :] = _mm_tn(z0[u][hs], bend[u][hs]) + _mm_tn(v[u][hs], kend[u][hs])
                we_ref[h, chunks[u]] = w_end[u][h]

    def prepare_step(i, carry):
        prepare([i * RW_PREP_UNROLL + u for u in range(RW_PREP_UNROLL)])
        return carry

    lax.fori_loop(0, n_chunk // RW_PREP_UNROLL, prepare_step, 0)

    def advance(c, carry):
        rows = pl.ds(pl.multiple_of(c * cl, cl), cl)
        for h in range(RW_HEADS):
            s = s_ref[h]
            sb = s.astype(BF16)
            y_ref[h, rows, :] = y_ref[h, rows, :] + lax.dot_general(
                rp_ref[h, rows, :], sb, (((1,), (1,)), ((), ())), preferred_element_type=F32)
            s_ref[h] = (s * we_ref[h, c] + jnp.dot(sb, gm_ref[h, rows, :], preferred_element_type=F32)
                        + h0_ref[h, rows, :])
        return carry

    lax.fori_loop(0, n_chunk, advance, 0)

    for h in range(RW_HEADS):
        y = y_ref[h]
        mu = jnp.mean(y, axis=-1, keepdims=True)
        yc = y - mu
        var = jnp.mean(yc * yc, axis=-1, keepdims=True)
        yn = yc * lax.rsqrt(var + RW_GN_EPS) * gg_ref[h] + gb_ref[h]
        v = v_ref[0, h]
        bonus = jnp.sum(r_ref[0, h] * k_ref[0, h] * rk_ref[h], axis=-1, keepdims=True) * v
        o_ref[0, h] = ((yn + bonus) * g_ref[0, h]).astype(o_ref.dtype)


def _rwscan(r, lw, k, v, a, b, g, r_k, gn_gain, gn_bias):
    bsz, _, t_len, _ = r.shape
    tb = min(512, t_len)
    hm_spec = pl.BlockSpec((1, RW_HEADS, tb, RW_HEAD_DIM), lambda bi, i: (bi, 0, i, 0))
    par = pl.BlockSpec((RW_HEADS, 1, RW_HEAD_DIM), lambda bi, i: (0, 0, 0))
    hshape = (RW_HEADS, 1, RW_HEAD_DIM)
    return pl.pallas_call(
        _rwscan_kernel,
        out_shape=jax.ShapeDtypeStruct((bsz, RW_HEADS, t_len, RW_HEAD_DIM), BF16),
        grid=(bsz, t_len // tb),
        in_specs=[hm_spec] * 7 + [par] * 3,
        out_specs=hm_spec,
        scratch_shapes=[pltpu.VMEM((RW_HEADS, RW_HEAD_DIM, RW_HEAD_DIM), F32),
                        pltpu.VMEM((RW_HEADS, tb, RW_HEAD_DIM), BF16), pltpu.VMEM((RW_HEADS, tb, RW_HEAD_DIM), F32),
                        pltpu.VMEM((RW_HEADS, tb, RW_HEAD_DIM), BF16), pltpu.VMEM((RW_HEADS, tb, RW_HEAD_DIM), F32),
                        pltpu.VMEM((RW_HEADS, tb // RW_CHUNK, 1, RW_HEAD_DIM), F32)],
        compiler_params=_cparams(("parallel", "arbitrary")),
        name="rwscan",
    )(r, lw, k, v, a, b, g, r_k.reshape(hshape), gn_gain.reshape(hshape), gn_bias.reshape(hshape))


def _foxprep_kernel(z_ref, f_ref, fb_ref, qg_ref, kg_ref, q_o, k_o, v_o, carry_ref):
    tm = z_ref.shape[1]

    @pl.when(pl.program_id(1) == 0)
    def _():
        carry_ref[...] = jnp.zeros_like(carry_ref)

    z = z_ref[0]
    log_f = _log_sigmoid(f_ref[0] + fb_ref[...])
    cum = carry_ref[...] + _tri_cumsum(log_f, tm)
    carry_ref[...] = cum[tm - 1:tm, :]
    lane = lax.broadcasted_iota(I32, (tm, FOX_HEAD_DIM), 1)
    qg = qg_ref[...]
    kg = kg_ref[...]
    for h in range(FOX_HEADS):
        f_hi, f_mid, f_lo = (p.astype(F32) for p in _split3(cum[:, h:h + 1] * LOG2E))
        qx = jnp.where(lane == 0, f_hi, jnp.where(lane == 1, f_mid, jnp.where(lane == 2, f_lo,
                       jnp.where(lane < 6, 1.0, 0.0))))
        kx = jnp.where(lane < 3, 1.0, jnp.where(lane == 3, -f_hi, jnp.where(lane == 4, -f_mid,
                       jnp.where(lane == 5, -f_lo, 0.0))))
        qh = z[:, h * FOX_HEAD_DIM:(h + 1) * FOX_HEAD_DIM]
        kh = z[:, FOX_WIDTH + h * FOX_HEAD_DIM:FOX_WIDTH + (h + 1) * FOX_HEAD_DIM]
        qn = qh * lax.rsqrt(jnp.mean(qh * qh, axis=-1, keepdims=True) + EPS) * qg * (ATTN_SCALE * LOG2E)
        kn = kh * lax.rsqrt(jnp.mean(kh * kh, axis=-1, keepdims=True) + EPS) * kg
        q_o[0, h] = jnp.concatenate([qn, qx], axis=-1).astype(BF16)
        k_o[0, h] = jnp.concatenate([kn, kx], axis=-1).astype(BF16)
        v_o[0, h] = z[:, 2 * FOX_WIDTH + h * FOX_HEAD_DIM:2 * FOX_WIDTH + (h + 1) * FOX_HEAD_DIM].astype(BF16)


def _foxprep(z, f_bias_pad, q_gain, k_gain):
    bsz, t_len, _ = z.shape
    tm = min(512, t_len)
    qk = jax.ShapeDtypeStruct((bsz, FOX_HEADS, t_len, 2 * FOX_HEAD_DIM), BF16)
    vv = jax.ShapeDtypeStruct((bsz, FOX_HEADS, t_len, FOX_HEAD_DIM), BF16)
    qk_spec = pl.BlockSpec((1, FOX_HEADS, tm, 2 * FOX_HEAD_DIM), lambda b, i: (b, 0, i, 0))
    v_spec = pl.BlockSpec((1, FOX_HEADS, tm, FOX_HEAD_DIM), lambda b, i: (b, 0, i, 0))
    return pl.pallas_call(
        _foxprep_kernel,
        out_shape=(qk, qk, vv),
        grid=(bsz, t_len // tm),
        in_specs=[pl.BlockSpec((1, tm, 3 * FOX_WIDTH), lambda b, i: (b, i, Z_FOX // (3 * FOX_WIDTH))),
                  pl.BlockSpec((1, tm, LANES), lambda b, i: (b, i, Z_F // LANES)),
                  pl.BlockSpec((1, LANES), lambda b, i: (0, 0)),
                  pl.BlockSpec((1, FOX_HEAD_DIM), lambda b, i: (0, 0)),
                  pl.BlockSpec((1, FOX_HEAD_DIM), lambda b, i: (0, 0))],
        out_specs=(qk_spec, qk_spec, v_spec),
        scratch_shapes=[pltpu.VMEM((1, LANES), F32)],
        compiler_params=_cparams(("parallel", "arbitrary")),
        name="foxprep",
    )(z, z, f_bias_pad, q_gain.reshape(1, -1), k_gain.reshape(1, -1))


def _fox_kernel(q_ref, k_ref, v_ref, o_ref, m_ref, l_ref, acc_ref):
    i = pl.program_id(1)
    j = pl.program_id(2)
    tq = q_ref.shape[2]
    tk = k_ref.shape[2]

    @pl.when(j == 0)
    def _():
        m_ref[...] = jnp.full_like(m_ref, MASK_VALUE)
        l_ref[...] = jnp.zeros_like(l_ref)
        acc_ref[...] = jnp.zeros_like(acc_ref)

    n_col = tk // LANES

    def update(diagonal):
        if diagonal:
            ri = lax.broadcasted_iota(I32, (tq, tk), 0)
            ci = lax.broadcasted_iota(I32, (tq, tk), 1)
            keep = ri >= ci
        for h in range(FOX_HEADS):
            s = lax.dot_general(q_ref[0, h], k_ref[0, h], (((1,), (1,)), ((), ())), preferred_element_type=F32)
            if diagonal:
                s = jnp.where(keep, s, MASK_VALUE)
            m_prev = m_ref[h]
            m_new = jnp.maximum(m_prev, jnp.max(s, axis=-1, keepdims=True))
            alpha = jnp.exp2(m_prev - m_new)
            cols = [jnp.exp2(s[:, c * LANES:(c + 1) * LANES] - m_new) for c in range(n_col)]
            l_new = alpha * l_ref[h]
            for pc in cols:
                l_new = l_new + pc
            l_ref[h] = l_new
            p = jnp.concatenate(cols, axis=-1).astype(BF16)
            acc_ref[h] = (alpha[:, :FOX_HEAD_DIM] * acc_ref[h]
                          + jnp.dot(p, v_ref[0, h], preferred_element_type=F32))
            m_ref[h] = m_new

    @pl.when(j < i)
    def _():
        update(False)

    @pl.when(j == i)
    def _():
        update(True)
        for h in range(FOX_HEADS):
            denom = jnp.sum(l_ref[h], axis=-1, keepdims=True)
            o_ref[0, :, h * FOX_HEAD_DIM:(h + 1) * FOX_HEAD_DIM] = (acc_ref[h] / denom).astype(o_ref.dtype)


def _fox(q, k, v):
    bsz, _, t_len, _ = q.shape
    tq = min(512, t_len)
    n_blk = t_len // tq
    return pl.pallas_call(
        _fox_kernel,
        out_shape=jax.ShapeDtypeStruct((bsz, t_len, FOX_WIDTH), BF16),
        grid=(bsz, n_blk, n_blk),
        in_specs=[pl.BlockSpec((1, FOX_HEADS, tq, 2 * FOX_HEAD_DIM), lambda b, i, j: (b, 0, i, 0)),
                  pl.BlockSpec((1, FOX_HEADS, tq, 2 * FOX_HEAD_DIM), lambda b, i, j: (b, 0, jnp.minimum(j, i), 0)),
                  pl.BlockSpec((1, FOX_HEADS, tq, FOX_HEAD_DIM), lambda b, i, j: (b, 0, jnp.minimum(j, i), 0))],
        out_specs=pl.BlockSpec((1, tq, FOX_WIDTH), lambda b, i, j: (b, i, 0)),
        scratch_shapes=[pltpu.VMEM((FOX_HEADS, tq, LANES), F32), pltpu.VMEM((FOX_HEADS, tq, LANES), F32),
                        pltpu.VMEM((FOX_HEADS, tq, FOX_HEAD_DIM), F32)],
        compiler_params=_cparams(("parallel", "parallel", "arbitrary")),
        name="fox",
    )(q, k, v)


def _merge_kernel(zg_ref, ygm_ref, yrw_ref, yfox_ref, x_ref, g1_ref, sc2_ref, sh2_ref, pb_ref, wo_ref, wr_ref, br_ref,
                  x1_o, h2_o, idx_o, gate_o, rank_o, cnt_o, carry_ref):
    tm = x_ref.shape[1]

    @pl.when((pl.program_id(0) == 0) & (pl.program_id(1) == 0))
    def _():
        carry_ref[...] = jnp.zeros_like(carry_ref)

    sg = jax.nn.sigmoid(zg_ref[0])
    p_gm = jnp.dot(ygm_ref[0], pb_ref[0:GM_WIDTH, :], preferred_element_type=F32)
    y_rw = jnp.concatenate([yrw_ref[0, h] for h in range(RW_HEADS)], axis=-1)
    p_rw = jnp.dot(y_rw, pb_ref[GM_WIDTH:GM_WIDTH + RW_WIDTH, :], preferred_element_type=F32)
    p_fox = jnp.dot(yfox_ref[0], pb_ref[GM_WIDTH + RW_WIDTH:, :], preferred_element_type=F32)
    merged = sg[:, 0:D_MODEL] * p_gm + sg[:, D_MODEL:2 * D_MODEL] * p_rw + sg[:, 2 * D_MODEL:] * p_fox
    x1 = x_ref[0] + g1_ref[0] * jnp.dot(merged.astype(BF16), wo_ref[...], preferred_element_type=F32)
    x1_o[0] = x1
    h2 = x1 * lax.rsqrt(jnp.mean(x1 * x1, axis=-1, keepdims=True) + EPS) * (1.0 + sc2_ref[0]) + sh2_ref[0]
    h2_o[0] = h2

    h_hi, h_lo, _ = _split3(h2)
    w_hi, w_lo, _ = _split3(wr_ref[...])
    logits = (jnp.dot(h_hi, w_hi, preferred_element_type=F32) + jnp.dot(h_hi, w_lo, preferred_element_type=F32)
              + jnp.dot(h_lo, w_hi, preferred_element_type=F32)) + br_ref[...]
    lane = lax.broadcasted_iota(I32, (tm, N_EXPERTS), 1)
    vals, idxs = [], []
    rest = logits
    for _ in range(TOP_K):
        m = jnp.max(rest, axis=-1, keepdims=True)
        am = jnp.min(jnp.where(rest == m, lane, N_EXPERTS), axis=-1, keepdims=True)
        vals.append(m)
        idxs.append(am)
        rest = jnp.where(lane == am, -jnp.inf, rest)
    exps = [jnp.exp(val - vals[0]) for val in vals]
    denom = exps[0] + exps[1] + exps[2] + exps[3]

    onehot = jnp.zeros((tm, N_EXPERTS), F32)
    for am in idxs:
        onehot = onehot + jnp.where(lane == am, 1.0, 0.0)
    ri = lax.broadcasted_iota(I32, (tm, tm), 0)
    ci = lax.broadcasted_iota(I32, (tm, tm), 1)
    before = jnp.where(ri > ci, 1.0, 0.0).astype(BF16)
    seen = carry_ref[...] + jnp.dot(before, onehot.astype(BF16), preferred_element_type=F32)
    lane_k = lax.broadcasted_iota(I32, (tm, TOP_K), 1)
    idx_out = jnp.zeros((tm, TOP_K), I32)
    gate_out = jnp.zeros((tm, TOP_K), F32)
    rank_out = jnp.zeros((tm, TOP_K), I32)
    for kk in range(TOP_K):
        rank = jnp.sum(jnp.where(lane == idxs[kk], seen, 0.0), axis=-1, keepdims=True).astype(I32)
        idx_out = jnp.where(lane_k == kk, idxs[kk], idx_out)
        gate_out = jnp.where(lane_k == kk, exps[kk] / denom, gate_out)
        rank_out = jnp.where(lane_k == kk, rank, rank_out)
    idx_o[0] = idx_out
    gate_o[0] = gate_out
    rank_o[0] = rank_out
    total = carry_ref[...] + jnp.sum(onehot, axis=0, keepdims=True)
    carry_ref[...] = total
    cnt_o[...] = total.astype(I32)


def _merge(z, y_gm, y_rw, y_fox, x, gate1, scale2, shift2, w_branch, w_o, w_router, b_router):
    bsz, t_len, d = x.shape
    tm = min(512, t_len)
    row = lambda w: pl.BlockSpec((1, tm, w), lambda b, i: (b, i, 0))
    mod = pl.BlockSpec((1, 1, d), lambda b, i: (b, 0, 0))
    full = lambda shape: pl.BlockSpec(shape, lambda b, i: (0,) * len(shape))
    return pl.pallas_call(
        _merge_kernel,
        out_shape=(jax.ShapeDtypeStruct((bsz, t_len, d), F32), jax.ShapeDtypeStruct((bsz, t_len, d), F32),
                   jax.ShapeDtypeStruct((bsz, t_len, TOP_K), I32), jax.ShapeDtypeStruct((bsz, t_len, TOP_K), F32),
                   jax.ShapeDtypeStruct((bsz, t_len, TOP_K), I32), jax.ShapeDtypeStruct((1, N_EXPERTS), I32)),
        grid=(bsz, t_len // tm),
        in_specs=[row(N_BRANCH * D_MODEL), row(GM_WIDTH),
                  pl.BlockSpec((1, RW_HEADS, tm, RW_HEAD_DIM), lambda b, i: (b, 0, i, 0)),
                  row(FOX_WIDTH), row(d), mod, mod, mod,
                  full(w_branch.shape), full(w_o.shape), full(w_router.shape), full((1, N_EXPERTS))],
        out_specs=(row(d), row(d), row(TOP_K), row(TOP_K), row(TOP_K), full((1, N_EXPERTS))),
        scratch_shapes=[pltpu.VMEM((1, N_EXPERTS), F32)],
        compiler_params=_cparams(("arbitrary", "arbitrary")),
        name="merge_router",
    )(z, y_gm, y_rw, y_fox, x, gate1, scale2, shift2, w_branch, w_o, w_router, b_router.reshape(1, N_EXPERTS))


def _sc_mesh():
    return plsc.VectorSubcoreMesh(core_axis_name="c", subcore_axis_name="s",
                                  num_cores=SC_CORES, num_subcores=SC_SUBCORES)


def _sc_worker():
    return lax.axis_index("s") * SC_CORES + lax.axis_index("c")


def _sc_scatter_rows(src, idx3, n_out):
    _, d = src.shape
    n_copy, n_grp, _ = idx3.shape
    grp_per_w = n_grp // SC_WORKERS

    def body(src_hbm, idx_hbm, out_hbm, idx_v, rows_v):
        g0 = _sc_worker() * grp_per_w
        for q in range(n_copy):
            pltpu.sync_copy(idx_hbm.at[q, pl.ds(g0, grp_per_w)], idx_v.at[pl.ds(q * grp_per_w, grp_per_w)])

        @pl.loop(0, grp_per_w)
        def _(j):
            pltpu.sync_copy(src_hbm.at[pl.ds((g0 + j) * SC_ROWS, SC_ROWS)], rows_v)
            for q in range(n_copy):
                pltpu.sync_copy(rows_v, out_hbm.at[idx_v.at[q * grp_per_w + j]])

    return pl.kernel(
        body, out_type=jax.ShapeDtypeStruct((n_out, d), src.dtype), mesh=_sc_mesh(),
        scratch_types=[pltpu.VMEM((n_copy * grp_per_w, SC_ROWS), I32), pltpu.VMEM((SC_ROWS, d), src.dtype)],
        name="sc_dispatch",
    )(src, idx3)


def _sc_gather_rows(table, idx2):
    _, d = table.shape
    n_grp, _ = idx2.shape
    grp_per_w = n_grp // SC_WORKERS

    def body(table_hbm, idx_hbm, out_hbm, idx_v, rows_v):
        g0 = _sc_worker() * grp_per_w
        pltpu.sync_copy(idx_hbm.at[pl.ds(g0, grp_per_w)], idx_v)

        @pl.loop(0, grp_per_w)
        def _(j):
            pltpu.sync_copy(table_hbm.at[idx_v.at[j]], rows_v)
            pltpu.sync_copy(rows_v, out_hbm.at[pl.ds((g0 + j) * SC_ROWS, SC_ROWS)])

    return pl.kernel(
        body, out_type=jax.ShapeDtypeStruct((n_grp * SC_ROWS, d), table.dtype), mesh=_sc_mesh(),
        scratch_types=[pltpu.VMEM((grp_per_w, SC_ROWS), I32), pltpu.VMEM((SC_ROWS, d), table.dtype)],
        name="sc_combine_gather",
    )(table, idx2)


def _ffn_kernel(be_ref, nv_ref, x_ref, wgu_ref, bgu_ref, wd_ref, bd_ref, o_ref, wgu_b, wd_b):
    i = pl.program_id(0)
    n_valid = nv_ref[i]

    @pl.when((i == 0) | (be_ref[i] != be_ref[jnp.maximum(i - 1, 0)]))
    def _():
        wgu_b[...] = wgu_ref[0].astype(BF16)
        wd_b[...] = wd_ref[0].astype(BF16)

    @pl.when(n_valid > 0)
    def _():
        rowid = lax.broadcasted_iota(I32, x_ref.shape, 0)
        x = jnp.where(rowid < n_valid, x_ref[...], 0.0).astype(BF16)
        gu = jnp.dot(x, wgu_b[...], preferred_element_type=F32) + bgu_ref[0]
        g_ = jnp.minimum(gu[:, :D_FF], SWIGLU_LIMIT)
        u_ = jnp.clip(gu[:, D_FF:], -SWIGLU_LIMIT, SWIGLU_LIMIT)
        act = (u_ + 1.0) * (g_ * jax.nn.sigmoid(SWIGLU_ALPHA * g_))
        o_ref[...] = jnp.dot(act.astype(BF16), wd_b[...], preferred_element_type=F32) + bd_ref[0]

    @pl.when(n_valid <= 0)
    def _():
        o_ref[...] = jnp.zeros_like(o_ref)


def _ffn(block_expert, block_valid, xin, w_gate_up, b_gate_up, w_down, b_down):
    n_rows, d = xin.shape
    n_blocks = n_rows // MOE_BLOCK
    grid_spec = pltpu.PrefetchScalarGridSpec(
        num_scalar_prefetch=2,
        grid=(n_blocks,),
        in_specs=[pl.BlockSpec((MOE_BLOCK, d), lambda i, be, nv: (i, 0)),
                  pl.BlockSpec((1, d, 2 * D_FF), lambda i, be, nv: (be[i], 0, 0)),
                  pl.BlockSpec((1, 1, 2 * D_FF), lambda i, be, nv: (be[i], 0, 0)),
                  pl.BlockSpec((1, D_FF, d), lambda i, be, nv: (be[i], 0, 0)),
                  pl.BlockSpec((1, 1, d), lambda i, be, nv: (be[i], 0, 0))],
        out_specs=pl.BlockSpec((MOE_BLOCK, d), lambda i, be, nv: (i, 0)),
        scratch_shapes=[pltpu.VMEM((d, 2 * D_FF), BF16), pltpu.VMEM((D_FF, d), BF16)],
    )
    return pl.pallas_call(
        _ffn_kernel,
        out_shape=jax.ShapeDtypeStruct((n_rows, d), F32),
        grid_spec=grid_spec,
        compiler_params=pltpu.CompilerParams(dimension_semantics=("arbitrary",), vmem_limit_bytes=FFN_VMEM_LIMIT),
        name="expert_ffn",
    )(block_expert, block_valid, xin, w_gate_up, b_gate_up.reshape(N_EXPERTS, 1, -1), w_down,
      b_down.reshape(N_EXPERTS, 1, -1))


def _combine_kernel(x1_ref, g2_ref, gate_ref, yg_ref, o_ref):
    gate = gate_ref[0]
    y = gate[:, 0:1] * yg_ref[0, 0]
    for q in range(1, TOP_K):
        y = y + gate[:, q:q + 1] * yg_ref[q, 0]
    o_ref[0] = x1_ref[0] + g2_ref[0] * y


def _combine(x1, gate2, gate, yg):
    bsz, t_len, d = x1.shape
    tm = min(512, t_len)
    return pl.pallas_call(
        _combine_kernel,
        out_shape=jax.ShapeDtypeStruct((bsz, t_len, d), F32),
        grid=(bsz, t_len // tm),
        in_specs=[pl.BlockSpec((1, tm, d), lambda b, i: (b, i, 0)),
                  pl.BlockSpec((1, 1, d), lambda b, i: (b, 0, 0)),
                  pl.BlockSpec((1, tm, TOP_K), lambda b, i: (b, i, 0)),
                  pl.BlockSpec((TOP_K, 1, tm, d), lambda b, i: (0, b, i, 0))],
        out_specs=pl.BlockSpec((1, tm, d), lambda b, i: (b, i, 0)),
        compiler_params=_cparams(("parallel", "parallel")),
        name="moe_combine",
    )(x1, gate2, gate, yg)


def _moe(x1, gate2, h2, top_idx, gate, rank, counts, w_gate_up, b_gate_up, w_down, b_down):
    bsz, t_len, d = h2.shape
    n_tok = bsz * t_len
    n_assign = n_tok * TOP_K
    n_blocks = -(-n_assign // MOE_BLOCK) + N_EXPERTS
    counts = counts.reshape(N_EXPERTS)
    blocks_e = (counts + MOE_BLOCK - 1) // MOE_BLOCK
    blk_end = jnp.cumsum(blocks_e)
    blk_start = blk_end - blocks_e
    experts = jnp.arange(N_EXPERTS, dtype=I32)
    onehot = top_idx.reshape(n_tok, TOP_K, 1) == experts
    dest = jnp.sum(jnp.where(onehot, blk_start * MOE_BLOCK, 0), axis=-1) + rank.reshape(n_tok, TOP_K)
    dest_t = dest.T.astype(I32)
    blk = jnp.arange(n_blocks, dtype=I32)
    block_expert = jnp.minimum(jnp.sum(blk_end[None, :] <= blk[:, None], axis=1), N_EXPERTS - 1).astype(I32)
    be_hot = block_expert[:, None] == experts
    cnt_b = jnp.sum(jnp.where(be_hot, counts, 0), axis=1)
    start_b = jnp.sum(jnp.where(be_hot, blk_start, 0), axis=1)
    block_valid = jnp.clip(cnt_b - (blk - start_b) * MOE_BLOCK, 0, MOE_BLOCK).astype(I32)
    xin = _sc_scatter_rows(h2.reshape(n_tok, d), dest_t.reshape(TOP_K, n_tok // SC_ROWS, SC_ROWS),
                           n_blocks * MOE_BLOCK)
    yb = _ffn(block_expert, block_valid, xin, w_gate_up, b_gate_up, w_down, b_down)
    yg = _sc_gather_rows(yb, dest_t.reshape(n_assign // SC_ROWS, SC_ROWS))
    return _combine(x1, gate2, gate, yg.reshape(TOP_K, bsz, t_len, d))


def _permute_kernel(w_ref, o_ref):
    o_gm = 0
    o_rw = o_gm + 2 * GM_WIDTH
    o_fox = o_rw + RW_SHIFT_WIDTH
    o_f = o_fox + 3 * FOX_WIDTH
    o_gate = o_f + FOX_HEADS
    w = w_ref[0]
    o_ref[0, :, Z_GATE:Z_FOX] = w[:, o_gate:o_gate + N_BRANCH * D_MODEL].astype(BF16)
    o_ref[0, :, Z_FOX:Z_GM] = w[:, o_fox:o_f].astype(BF16)
    o_ref[0, :, Z_GM:Z_RW] = w[:, o_gm:o_rw].astype(BF16)
    o_ref[0, :, Z_RW:Z_F] = w[:, o_rw:o_fox].astype(BF16)
    tail = jnp.concatenate([w[:, o_f:o_gate], jnp.zeros((w.shape[0], Z_WIDTH - Z_F - FOX_HEADS), F32)], axis=-1)
    o_ref[0, :, Z_F:Z_WIDTH] = tail.astype(BF16)


def _permute_w_in(w_in):
    n_layer, d, w_cols = w_in.shape
    tr = 256
    return pl.pallas_call(
        _permute_kernel,
        out_shape=jax.ShapeDtypeStruct((n_layer, d, Z_WIDTH), BF16),
        grid=(n_layer, d // tr),
        in_specs=[pl.BlockSpec((1, tr, w_cols), lambda l, i: (l, i, 0))],
        out_specs=pl.BlockSpec((1, tr, Z_WIDTH), lambda l, i: (l, i, 0)),
        compiler_params=_cparams(("parallel", "parallel")),
        name="permute_w_in",
    )(w_in)


def _layer(x, mod, w_in_p, gm_v_gain, gm_w_s, gm_b_s, mu_pad, w_lora, rw_w0, rw_a0, rw_k_k, rw_k_a, rw_r_k,
           rw_gn_gain, rw_gn_bias, f_bias_pad, fox_q_gain, fox_k_gain, w_branch, w_o, w_router, b_router,
           w_gate_up, b_gate_up, w_down, b_down):
    shift1, scale1, gate1, shift2, scale2, gate2 = (mod[:, i][:, None, :] for i in range(6))
    z = _inproj(x, scale1, shift1, w_in_p)
    y_gm = _gmlp(z, gm_v_gain, gm_w_s, gm_b_s)
    r, lw, k, v, a, b, g = _rwprep(z, mu_pad, w_lora, rw_w0, rw_a0, rw_k_k, rw_k_a)
    y_rw = _rwscan(r, lw, k, v, a, b, g, rw_r_k, rw_gn_gain, rw_gn_bias)
    q, kf, vf = _foxprep(z, f_bias_pad, fox_q_gain, fox_k_gain)
    y_fox = _fox(q, kf, vf)
    x1, h2, top_idx, gate, rank, counts = _merge(z, y_gm, y_rw, y_fox, x, gate1, scale2, shift2,
                                                 w_branch, w_o, w_router, b_router)
    return _moe(x1, gate2, h2, top_idx, gate, rank, counts, w_gate_up, b_gate_up, w_down, b_down)


def kernel(x, c, w_ada, b_ada, w_in, gm_v_gain, gm_w_s, gm_b_s, rw_mu, rw_w0, rw_w2, rw_a0, rw_a2, rw_g2, rw_k_k,
           rw_k_a, rw_r_k, rw_gn_gain, rw_gn_bias, fox_f_bias, fox_q_gain, fox_k_gain, w_branch, w_o, w_router,
           b_router, w_gate_up, b_gate_up, w_down, b_down):
    n_layer = w_ada.shape[0]
    bsz = x.shape[0]
    c_pad = jnp.zeros((8, D_MODEL), F32).at[:bsz].set(c)
    mod = _adaln(c_pad, w_ada, b_ada)[:, :bsz].reshape(n_layer, bsz, 6, D_MODEL)
    w_in_p = _permute_w_in(w_in)
    mu_pad = jnp.pad(rw_mu, ((0, 0), (0, RW_BLOCK - RW_SHIFT_WIDTH)))
    w_lora = jnp.zeros((n_layer, RW_LORA, 3 * RW_WIDTH), F32)
    w_lora = w_lora.at[:, 0:RW_DECAY_LORA, 0:RW_WIDTH].set(rw_w2)
    w_lora = w_lora.at[:, RW_DECAY_LORA:RW_DECAY_LORA + RW_ICLR_LORA, RW_WIDTH:2 * RW_WIDTH].set(rw_a2)
    w_lora = w_lora.at[:, RW_DECAY_LORA + RW_ICLR_LORA:, 2 * RW_WIDTH:].set(rw_g2)
    f_bias_pad = jnp.pad(fox_f_bias, ((0, 0), (0, LANES - FOX_HEADS)))
    w_branch_b = w_branch.astype(BF16)
    w_o_b = w_o.astype(BF16)
    for l in range(n_layer):
        x = _layer(x, mod[l], w_in_p[l], gm_v_gain[l], gm_w_s[l], gm_b_s[l], mu_pad[l:l + 1], w_lora[l], rw_w0[l],
                   rw_a0[l], rw_k_k[l], rw_k_a[l], rw_r_k[l], rw_gn_gain[l], rw_gn_bias[l], f_bias_pad[l:l + 1],
                   fox_q_gain[l], fox_k_gain[l], w_branch_b[l], w_o_b[l], w_router[l], b_router[l],
                   w_gate_up[l], b_gate_up[l], w_down[l], b_down[l])
    return x
```

```python
import functools

import jax
import jax.numpy as jnp
from jax import lax
from jax.experimental import pallas as pl
from jax.experimental.pallas import tpu as pltpu
from jax.experimental.pallas import tpu_sc as plsc

F32 = jnp.float32
BF16 = jnp.bfloat16
I32 = jnp.int32
HIGHEST = lax.Precision.HIGHEST

D_MODEL = 1024
GM_CHUNK = 128
GM_GROUPS = 4
GM_WIDTH = 256
GM_GROUP_DIM = GM_WIDTH // GM_GROUPS
RW_HEADS = 4
RW_HEAD_DIM = 64
RW_WIDTH = RW_HEADS * RW_HEAD_DIM
RW_DECAY_LORA = 32
RW_ICLR_LORA = 32
RW_GATE_LORA = 64
RW_LORA = RW_DECAY_LORA + RW_ICLR_LORA + RW_GATE_LORA
RW_SHIFT_WIDTH = 3 * RW_WIDTH + RW_LORA
RW_GN_EPS = 64e-5
FOX_HEADS = 8
FOX_HEAD_DIM = 64
FOX_WIDTH = FOX_HEADS * FOX_HEAD_DIM
ATTN_SCALE = FOX_HEAD_DIM ** -0.5
MASK_VALUE = -1e30
LOG2E = 1.4426950408889634
N_BRANCH = 3
MIX_WIDTH = GM_WIDTH + RW_WIDTH + FOX_WIDTH
N_EXPERTS = 32
TOP_K = 4
D_FF = D_MODEL
SWIGLU_LIMIT = 7.0
SWIGLU_ALPHA = 1.702
MOE_BLOCK = 256
EPS = 1e-6

Z_GATE = 0
Z_FOX = N_BRANCH * D_MODEL
Z_GM = Z_FOX + 3 * FOX_WIDTH
Z_RW = Z_GM + 2 * GM_WIDTH
RW_BLOCK = 1024
Z_F = Z_RW + RW_SHIFT_WIDTH
Z_WIDTH = Z_RW + RW_BLOCK
LANES = 128
RW_CHUNK = 64
RW_PREP_UNROLL = 4

VMEM_LIMIT = 48 * 1024 * 1024
FFN_VMEM_LIMIT = 56 * 1024 * 1024
SC_CORES = 2
SC_SUBCORES = 16
SC_WORKERS = SC_CORES * SC_SUBCORES
SC_ROWS = 64


def _cparams(sem):
    return pltpu.CompilerParams(dimension_semantics=sem, vmem_limit_bytes=VMEM_LIMIT)


def _mm(a, b):
    return jnp.dot(a.astype(BF16), b.astype(BF16), preferred_element_type=F32)


def _mm_nt(a, b):
    return lax.dot_general(a.astype(BF16), b.astype(BF16), (((1,), (1,)), ((), ())), preferred_element_type=F32)


def _mm_tn(a, b):
    return lax.dot_general(a.astype(BF16), b.astype(BF16), (((0,), (0,)), ((), ())), preferred_element_type=F32)


def _split3(x):
    hi = x.astype(BF16)
    r1 = x - hi.astype(F32)
    mid = r1.astype(BF16)
    lo = (r1 - mid.astype(F32)).astype(BF16)
    return hi, mid, lo


def _tri_cumsum(x, n):
    ri = lax.broadcasted_iota(I32, (n, n), 0)
    ci = lax.broadcasted_iota(I32, (n, n), 1)
    ones = jnp.where(ri >= ci, 1.0, 0.0).astype(BF16)
    hi, mid, lo = _split3(x)
    return (jnp.dot(ones, hi, preferred_element_type=F32) + jnp.dot(ones, mid, preferred_element_type=F32)
            + jnp.dot(ones, lo, preferred_element_type=F32))


def _pack_halves(x):
    w = x.shape[1] // 2
    hi = pltpu.bitcast(x[:, :w].astype(BF16).astype(F32), jnp.uint32)
    lo = pltpu.bitcast(x[:, w:].astype(BF16).astype(F32), jnp.uint32)
    return pltpu.bitcast(hi | (lo >> 16), I32)


def _unpack_halves(p):
    u = pltpu.bitcast(p, jnp.uint32)
    return pltpu.bitcast(u & jnp.uint32(0xFFFF0000), F32), pltpu.bitcast(u << 16, F32)


def _log_sigmoid(x):
    return jnp.minimum(x, 0.0) - jnp.log1p(jnp.exp(-jnp.abs(x)))


def _adaln_kernel(c_ref, w_ref, b_ref, o_ref):
    c = c_ref[...]
    s = c * jax.nn.sigmoid(c)
    o_ref[0] = jnp.dot(s, w_ref[0], preferred_element_type=F32, precision=HIGHEST) + b_ref[0]


def _adaln(c_pad, w_ada, b_ada):
    n_layer, d, w6 = w_ada.shape
    tn = 1536
    return pl.pallas_call(
        _adaln_kernel,
        out_shape=jax.ShapeDtypeStruct((n_layer, c_pad.shape[0], w6), F32),
        grid=(n_layer, w6 // tn),
        in_specs=[pl.BlockSpec(c_pad.shape, lambda l, j: (0, 0)),
                  pl.BlockSpec((1, d, tn), lambda l, j: (l, 0, j)),
                  pl.BlockSpec((1, 1, tn), lambda l, j: (l, 0, j))],
        out_specs=pl.BlockSpec((1, c_pad.shape[0], tn), lambda l, j: (l, 0, j)),
        compiler_params=_cparams(("parallel", "parallel")),
        name="adaln",
    )(c_pad, w_ada, b_ada.reshape(n_layer, 1, w6))


def _inproj_kernel(x_ref, sc_ref, sh_ref, w_ref, o_ref, xn_ref):
    @pl.when(pl.program_id(2) == 0)
    def _():
        x = x_ref[0]
        xn = x * lax.rsqrt(jnp.mean(x * x, axis=-1, keepdims=True) + EPS)
        xn_ref[...] = (xn * (1.0 + sc_ref[0]) + sh_ref[0]).astype(BF16)

    o_ref[0] = jnp.dot(xn_ref[...], w_ref[...], preferred_element_type=F32)


def _inproj(x, scale, shift, w, layer):
    bsz, t_len, d = x.shape
    tm = min(1024, t_len)
    tn = 1024
    return pl.pallas_call(
        _inproj_kernel,
        out_shape=jax.ShapeDtypeStruct((bsz, t_len, Z_WIDTH), F32),
        grid=(bsz, t_len // tm, Z_WIDTH // tn),
        in_specs=[pl.BlockSpec((1, tm, d), lambda b, i, j: (b, i, 0)),
                  pl.BlockSpec((1, 1, d), lambda b, i, j: (b, 0, 0)),
                  pl.BlockSpec((1, 1, d), lambda b, i, j: (b, 0, 0)),
                  pl.BlockSpec((d, tn), lambda b, i, j: (layer, j))],
        out_specs=pl.BlockSpec((1, tm, tn), lambda b, i, j: (b, i, j)),
        scratch_shapes=[pltpu.VMEM((tm, d), BF16)],
        compiler_params=_cparams(("parallel", "parallel", "arbitrary")),
        name="inproj",
    )(x, scale, shift, w)


def _gmlp_kernel(z_ref, gain_ref, ws_ref, bst_ref, o_ref):
    tm = z_ref.shape[1]
    z = z_ref[0]
    u = jax.nn.gelu(z[:, :GM_WIDTH])
    v = jax.nn.gelu(z[:, GM_WIDTH:])
    v = v * lax.rsqrt(jnp.mean(v * v, axis=-1, keepdims=True) + EPS) * gain_ref[...]
    vb = v.astype(BF16)
    grp = lax.broadcasted_iota(I32, (GM_CHUNK, GM_WIDTH), 1) // GM_GROUP_DIM
    ri = lax.broadcasted_iota(I32, (GM_CHUNK, GM_CHUNK), 0)
    ci = lax.broadcasted_iota(I32, (GM_CHUNK, GM_CHUNK), 1)
    causal = ri >= ci
    bias = jnp.zeros((GM_CHUNK, GM_WIDTH), F32)
    ws = []
    for g in range(GM_GROUPS):
        ws.append(jnp.where(causal, ws_ref[g], 0.0).astype(BF16))
        bias = jnp.where(grp == g, bst_ref[:, g:g + 1], bias)
    for c in range(tm // GM_CHUNK):
        rows = slice(c * GM_CHUNK, (c + 1) * GM_CHUNK)
        vc = vb[rows]
        mixed = bias
        for g in range(GM_GROUPS):
            m = jnp.dot(ws[g], vc, preferred_element_type=F32)
            mixed = mixed + jnp.where(grp == g, m, 0.0)
        o_ref[0, rows, :] = (u[rows] * mixed).astype(o_ref.dtype)


def _gmlp(z, gain, w_s, b_s):
    bsz, t_len, _ = z.shape
    tm = min(512, t_len)
    return pl.pallas_call(
        _gmlp_kernel,
        out_shape=jax.ShapeDtypeStruct((bsz, t_len, GM_WIDTH), BF16),
        grid=(bsz, t_len // tm),
        in_specs=[pl.BlockSpec((1, tm, 2 * GM_WIDTH), lambda b, i: (b, i, Z_GM // (2 * GM_WIDTH))),
                  pl.BlockSpec((1, GM_WIDTH), lambda b, i: (0, 0)),
                  pl.BlockSpec((GM_GROUPS, GM_CHUNK, GM_CHUNK), lambda b, i: (0, 0, 0)),
                  pl.BlockSpec((GM_CHUNK, GM_GROUPS), lambda b, i: (0, 0))],
        out_specs=pl.BlockSpec((1, tm, GM_WIDTH), lambda b, i: (b, i, 0)),
        compiler_params=_cparams(("parallel", "parallel")),
        name="gmlp",
    )(z, gain.reshape(1, GM_WIDTH), w_s, b_s.T)


def _rwprep_kernel(z_ref, zp_ref, mu_ref, wl_ref, w0_ref, a0_ref, kk_ref, ka_ref,
                   r_o, lw_o, k_o, v_o, a_o, b_o, g_o):
    tm = z_ref.shape[1]
    z = z_ref[0]
    prev = jnp.where(pl.program_id(1) > 0, zp_ref[0, 7:8, :], 0.0)
    rowid = lax.broadcasted_iota(I32, z.shape, 0)
    zs = jnp.where(rowid == 0, prev, pltpu.roll(z, 1, axis=0))
    zz = z + mu_ref[...] * (zs - z)
    r = zz[:, 0:RW_WIDTH]
    k = zz[:, RW_WIDTH:2 * RW_WIDTH]
    v = zz[:, 2 * RW_WIDTH:3 * RW_WIDTH]
    lo = zz[:, 3 * RW_WIDTH:3 * RW_WIDTH + RW_LORA]
    lane = lax.broadcasted_iota(I32, (tm, RW_LORA), 1)
    act = jnp.where(lane < RW_DECAY_LORA, jnp.tanh(lo),
                    jnp.where(lane < RW_DECAY_LORA + RW_ICLR_LORA, lo, jax.nn.sigmoid(lo)))
    proj = jnp.dot(act, wl_ref[...], preferred_element_type=F32, precision=HIGHEST)
    xw = -(w0_ref[...] + proj[:, 0:RW_WIDTH])
    softplus = jnp.maximum(xw, 0.0) + jnp.log1p(jnp.exp(-jnp.abs(xw)))
    lw = -jnp.exp(-softplus - 0.5)
    a = jax.nn.sigmoid(a0_ref[...] + proj[:, RW_WIDTH:2 * RW_WIDTH])
    g = proj[:, 2 * RW_WIDTH:3 * RW_WIDTH]
    kk = k * kk_ref[...]
    k2 = k * (1.0 + (a - 1.0) * ka_ref[...])
    for h in range(RW_HEADS):
        sl = slice(h * RW_HEAD_DIM, (h + 1) * RW_HEAD_DIM)
        kkh = kk[:, sl]
        nrm = jnp.sqrt(jnp.sum(kkh * kkh, axis=-1, keepdims=True))
        kkh = kkh / jnp.maximum(nrm, 1e-12)
        r_o[0, h] = r[:, sl]
        lw_o[0, h] = lw[:, sl]
        k_o[0, h] = k2[:, sl]
        v_o[0, h] = v[:, sl]
        a_o[0, h] = -kkh
        b_o[0, h] = kkh * a[:, sl]
        g_o[0, h] = g[:, sl]


def _rwprep(z, mu_pad, w_lora, w0, a0, k_k, k_a):
    bsz, t_len, _ = z.shape
    tm = min(512, t_len)
    hm = jax.ShapeDtypeStruct((bsz, RW_HEADS, t_len, RW_HEAD_DIM), F32)
    hm_spec = pl.BlockSpec((1, RW_HEADS, tm, RW_HEAD_DIM), lambda b, i: (b, 0, i, 0))
    vec = lambda n: pl.BlockSpec((1, n), lambda b, i: (0, 0))
    rw_blk = Z_RW // RW_BLOCK
    return pl.pallas_call(
        _rwprep_kernel,
        out_shape=(hm,) * 7,
        grid=(bsz, t_len // tm),
        in_specs=[pl.BlockSpec((1, tm, RW_BLOCK), lambda b, i: (b, i, rw_blk)),
                  pl.BlockSpec((1, 8, RW_BLOCK), lambda b, i: (b, jnp.maximum(i * (tm // 8) - 1, 0), rw_blk)),
                  vec(RW_BLOCK),
                  pl.BlockSpec((RW_LORA, 3 * RW_WIDTH), lambda b, i: (0, 0)),
                  vec(RW_WIDTH), vec(RW_WIDTH), vec(RW_WIDTH), vec(RW_WIDTH)],
        out_specs=(hm_spec,) * 7,
        compiler_params=_cparams(("parallel", "parallel")),
        name="rwprep",
    )(z, z, mu_pad, w_lora, w0.reshape(1, -1), a0.reshape(1, -1), k_k.reshape(1, -1), k_a.reshape(1, -1))


def _rwscan_kernel(r_ref, lw_ref, k_ref, v_ref, a_ref, b_ref, g_ref, rk_ref, gg_ref, gb_ref, o_ref,
                   s_ref, rp_ref, y_ref, gm_ref, h0_ref, we_ref):
    cl = RW_CHUNK
    tb = r_ref.shape[2]
    n_chunk = tb // cl

    @pl.when(pl.program_id(1) == 0)
    def _():
        s_ref[...] = jnp.zeros_like(s_ref)

    n = RW_HEADS * cl
    ri = lax.broadcasted_iota(I32, (n, n), 0)
    ci = lax.broadcasted_iota(I32, (n, n), 1)
    same_head = (ri // cl) == (ci // cl)
    lower = same_head & (ri >= ci)
    strict = same_head & (ri > ci)
    eye = jnp.where(ri == ci, 1.0, 0.0)
    ones_lower = jnp.where(lower, 1.0, 0.0).astype(BF16)

    def prepare(chunks):
        grp = range(len(chunks))
        each = lambda fn: [fn(u) for u in grp]
        rows = [pl.ds(pl.multiple_of(c * cl, cl), cl) for c in chunks]
        stack = lambda ref: each(lambda u: ref[0, :, rows[u], :].reshape(n, RW_HEAD_DIM))
        r, lw, k, v, a, b = (stack(ref) for ref in (r_ref, lw_ref, k_ref, v_ref, a_ref, b_ref))
        parts = each(lambda u: _split3(lw[u]))
        tri = lambda x: jnp.dot(ones_lower, x, preferred_element_type=F32)
        cw = each(lambda u: tri(parts[u][0]) + tri(parts[u][1]) + tri(parts[u][2]))
        w_in = each(lambda u: jnp.exp(cw[u]))
        w_inv = each(lambda u: jnp.exp(-cw[u]))
        rt = each(lambda u: r[u] * w_in[u])
        at = each(lambda u: a[u] * jnp.exp(cw[u] - lw[u]))
        kt = each(lambda u: k[u] * w_inv[u])
        bt = each(lambda u: b[u] * w_inv[u])
        w_end = each(lambda u: w_in[u].reshape(RW_HEADS, cl, RW_HEAD_DIM)[:, cl - 1:cl, :])
        w_end_rows = each(lambda u: jnp.broadcast_to(w_end[u], (RW_HEADS, cl, RW_HEAD_DIM)).reshape(n, RW_HEAD_DIM))
        a_ab = each(lambda u: jnp.where(strict, _mm_nt(at[u], bt[u]), 0.0))
        a_ak = each(lambda u: jnp.where(strict, _mm_nt(at[u], kt[u]), 0.0))
        m_rb = each(lambda u: jnp.where(lower, _mm_nt(rt[u], bt[u]), 0.0))
        m_rk = each(lambda u: jnp.where(lower, _mm_nt(rt[u], kt[u]), 0.0))
        inv = each(lambda u: eye + a_ab[u])
        p = a_ab
        for _ in range(cl.bit_length() - 2):
            p = [_mm(p[u], p[u]) for u in grp]
            inv = [inv[u] + _mm(inv[u], p[u]) for u in grp]
        akv = each(lambda u: _mm(a_ak[u], v[u]))
        ap = each(lambda u: _mm(inv[u], at[u]))
        z0 = each(lambda u: _mm(inv[u], akv[u]))
        bend = each(lambda u: bt[u] * w_end_rows[u])
        kend = each(lambda u: kt[u] * w_end_rows[u])
        rp = each(lambda u: (rt[u] + _mm(m_rb[u], ap[u])).astype(BF16))
        y0 = each(lambda u: _mm(m_rb[u], z0[u]) + _mm(m_rk[u], v[u]))
        for u in grp:
            for h in range(RW_HEADS):
                hs = slice(h * cl, (h + 1) * cl)
                rp_ref[h, rows[u], :] = rp[u][hs]
                y_ref[h, rows[u], :] = y0[u][hs]
                gm_ref[h, rows[u], :] = _mm_tn(ap[u][hs], bend[u][hs]).astype(BF16)
                h0_ref[h, rows[u], :] = _mm_tn(z0[u][hs], bend[u][hs]) + _mm_tn(v[u][hs], kend[u][hs])
                we_ref[h, chunks[u]] = w_end[u][h]

    def prepare_step(i, carry):
        prepare([i * RW_PREP_UNROLL + u for u in range(RW_PREP_UNROLL)])
        return carry

    lax.fori_loop(0, n_chunk // RW_PREP_UNROLL, prepare_step, 0)

    def advance(c, carry):
        rows = pl.ds(pl.multiple_of(c * cl, cl), cl)
        for h in range(RW_HEADS):
            s = s_ref[h]
            sb = s.astype(BF16)
            y_ref[h, rows, :] = y_ref[h, rows, :] + lax.dot_general(
                rp_ref[h, rows, :], sb, (((1,), (1,)), ((), ())), preferred_element_type=F32)
            s_ref[h] = (s * we_ref[h, c] + jnp.dot(sb, gm_ref[h, rows, :], preferred_element_type=F32)
                        + h0_ref[h, rows, :])
        return carry

    lax.fori_loop(0, n_chunk, advance, 0)

    for h in range(RW_HEADS):
        y = y_ref[h]
        mu = jnp.mean(y, axis=-1, keepdims=True)
        yc = y - mu
        var = jnp.mean(yc * yc, axis=-1, keepdims=True)
        yn = yc * lax.rsqrt(var + RW_GN_EPS) * gg_ref[h] + gb_ref[h]
        v = v_ref[0, h]
        bonus = jnp.sum(r_ref[0, h] * k_ref[0, h] * rk_ref[h], axis=-1, keepdims=True) * v
        o_ref[0, h] = ((yn + bonus) * g_ref[0, h]).astype(o_ref.dtype)


def _rwscan(r, lw, k, v, a, b, g, r_k, gn_gain, gn_bias):
    bsz, _, t_len, _ = r.shape
    tb = min(512, t_len)
    hm_spec = pl.BlockSpec((1, RW_HEADS, tb, RW_HEAD_DIM), lambda bi, i: (bi, 0, i, 0))
    par = pl.BlockSpec((RW_HEADS, 1, RW_HEAD_DIM), lambda bi, i: (0, 0, 0))
    hshape = (RW_HEADS, 1, RW_HEAD_DIM)
    return pl.pallas_call(
        _rwscan_kernel,
        out_shape=jax.ShapeDtypeStruct((bsz, RW_HEADS, t_len, RW_HEAD_DIM), BF16),
        grid=(bsz, t_len // tb),
        in_specs=[hm_spec] * 7 + [par] * 3,
        out_specs=hm_spec,
        scratch_shapes=[pltpu.VMEM((RW_HEADS, RW_HEAD_DIM, RW_HEAD_DIM), F32),
                        pltpu.VMEM((RW_HEADS, tb, RW_HEAD_DIM), BF16), pltpu.VMEM((RW_HEADS, tb, RW_HEAD_DIM), F32),
                        pltpu.VMEM((RW_HEADS, tb, RW_HEAD_DIM), BF16), pltpu.VMEM((RW_HEADS, tb, RW_HEAD_DIM), F32),
                        pltpu.VMEM((RW_HEADS, tb // RW_CHUNK, 1, RW_HEAD_DIM), F32)],
        compiler_params=_cparams(("parallel", "arbitrary")),
        name="rwscan",
    )(r, lw, k, v, a, b, g, r_k.reshape(hshape), gn_gain.reshape(hshape), gn_bias.reshape(hshape))


def _foxprep_kernel(z_ref, f_ref, fb_ref, qg_ref, kg_ref, q_o, k_o, v_o, carry_ref):
    tm = z_ref.shape[1]

    @pl.when(pl.program_id(1) == 0)
    def _():
        carry_ref[...] = jnp.zeros_like(carry_ref)

    z = z_ref[0]
    log_f = _log_sigmoid(f_ref[0] + fb_ref[...])
    cum = carry_ref[...] + _tri_cumsum(log_f, tm)
    carry_ref[...] = cum[tm - 1:tm, :]
    lane = lax.broadcasted_iota(I32, (tm, FOX_HEAD_DIM), 1)
    qg = qg_ref[...]
    kg = kg_ref[...]
    for h in range(FOX_HEADS):
        f_hi, f_mid, f_lo = (p.astype(F32) for p in _split3(cum[:, h:h + 1] * LOG2E))
        qx = jnp.where(lane == 0, f_hi, jnp.where(lane == 1, f_mid, jnp.where(lane == 2, f_lo,
                       jnp.where(lane < 6, 1.0, 0.0))))
        kx = jnp.where(lane < 3, 1.0, jnp.where(lane == 3, -f_hi, jnp.where(lane == 4, -f_mid,
                       jnp.where(lane == 5, -f_lo, 0.0))))
        qh = z[:, h * FOX_HEAD_DIM:(h + 1) * FOX_HEAD_DIM]
        kh = z[:, FOX_WIDTH + h * FOX_HEAD_DIM:FOX_WIDTH + (h + 1) * FOX_HEAD_DIM]
        qn = qh * lax.rsqrt(jnp.mean(qh * qh, axis=-1, keepdims=True) + EPS) * qg * (ATTN_SCALE * LOG2E)
        kn = kh * lax.rsqrt(jnp.mean(kh * kh, axis=-1, keepdims=True) + EPS) * kg
        q_o[0, h] = jnp.concatenate([qn, qx], axis=-1).astype(BF16)
        k_o[0, h] = jnp.concatenate([kn, kx], axis=-1).astype(BF16)
        v_o[0, h] = z[:, 2 * FOX_WIDTH + h * FOX_HEAD_DIM:2 * FOX_WIDTH + (h + 1) * FOX_HEAD_DIM].astype(BF16)


def _foxprep(z, f_bias_pad, q_gain, k_gain):
    bsz, t_len, _ = z.shape
    tm = min(512, t_len)
    qk = jax.ShapeDtypeStruct((bsz, FOX_HEADS, t_len, 2 * FOX_HEAD_DIM), BF16)
    vv = jax.ShapeDtypeStruct((bsz, FOX_HEADS, t_len, FOX_HEAD_DIM), BF16)
    qk_spec = pl.BlockSpec((1, FOX_HEADS, tm, 2 * FOX_HEAD_DIM), lambda b, i: (b, 0, i, 0))
    v_spec = pl.BlockSpec((1, FOX_HEADS, tm, FOX_HEAD_DIM), lambda b, i: (b, 0, i, 0))
    return pl.pallas_call(
        _foxprep_kernel,
        out_shape=(qk, qk, vv),
        grid=(bsz, t_len // tm),
        in_specs=[pl.BlockSpec((1, tm, 3 * FOX_WIDTH), lambda b, i: (b, i, Z_FOX // (3 * FOX_WIDTH))),
                  pl.BlockSpec((1, tm, LANES), lambda b, i: (b, i, Z_F // LANES)),
                  pl.BlockSpec((1, LANES), lambda b, i: (0, 0)),
                  pl.BlockSpec((1, FOX_HEAD_DIM), lambda b, i: (0, 0)),
                  pl.BlockSpec((1, FOX_HEAD_DIM), lambda b, i: (0, 0))],
        out_specs=(qk_spec, qk_spec, v_spec),
        scratch_shapes=[pltpu.VMEM((1, LANES), F32)],
        compiler_params=_cparams(("parallel", "arbitrary")),
        name="foxprep",
    )(z, z, f_bias_pad, q_gain.reshape(1, -1), k_gain.reshape(1, -1))


def _fox_kernel(q_ref, k_ref, v_ref, o_ref, m_ref, l_ref, acc_ref):
    i = pl.program_id(1)
    j = pl.program_id(2)
    tq = q_ref.shape[2]
    tk = k_ref.shape[2]

    @pl.when(j == 0)
    def _():
        m_ref[...] = jnp.full_like(m_ref, MASK_VALUE)
        l_ref[...] = jnp.zeros_like(l_ref)
        acc_ref[...] = jnp.zeros_like(acc_ref)

    n_col = tk // LANES

    def update(diagonal):
        if diagonal:
            ri = lax.broadcasted_iota(I32, (tq, tk), 0)
            ci = lax.broadcasted_iota(I32, (tq, tk), 1)
            keep = ri >= ci
        for h in range(FOX_HEADS):
            s = lax.dot_general(q_ref[0, h], k_ref[0, h], (((1,), (1,)), ((), ())), preferred_element_type=F32)
            if diagonal:
                s = jnp.where(keep, s, MASK_VALUE)
            m_prev = m_ref[h]
            m_new = jnp.maximum(m_prev, jnp.max(s, axis=-1, keepdims=True))
            alpha = jnp.exp2(m_prev - m_new)
            cols = [jnp.exp2(s[:, c * LANES:(c + 1) * LANES] - m_new) for c in range(n_col)]
            l_new = alpha * l_ref[h]
            for pc in cols:
                l_new = l_new + pc
            l_ref[h] = l_new
            p = jnp.concatenate(cols, axis=-1).astype(BF16)
            acc_ref[h] = (alpha[:, :FOX_HEAD_DIM] * acc_ref[h]
                          + jnp.dot(p, v_ref[0, h], preferred_element_type=F32))
            m_ref[h] = m_new

    @pl.when(j < i)
    def _():
        update(False)

    @pl.when(j == i)
    def _():
        update(True)
        for h in range(FOX_HEADS):
            denom = jnp.sum(l_ref[h], axis=-1, keepdims=True)
            o_ref[0, :, h * FOX_HEAD_DIM:(h + 1) * FOX_HEAD_DIM] = (acc_ref[h] / denom).astype(o_ref.dtype)


def _fox(q, k, v):
    bsz, _, t_len, _ = q.shape
    tq = min(512, t_len)
    n_blk = t_len // tq
    return pl.pallas_call(
        _fox_kernel,
        out_shape=jax.ShapeDtypeStruct((bsz, t_len, FOX_WIDTH), BF16),
        grid=(bsz, n_blk, n_blk),
        in_specs=[pl.BlockSpec((1, FOX_HEADS, tq, 2 * FOX_HEAD_DIM), lambda b, i, j: (b, 0, i, 0)),
                  pl.BlockSpec((1, FOX_HEADS, tq, 2 * FOX_HEAD_DIM), lambda b, i, j: (b, 0, jnp.minimum(j, i), 0)),
                  pl.BlockSpec((1, FOX_HEADS, tq, FOX_HEAD_DIM), lambda b, i, j: (b, 0, jnp.minimum(j, i), 0))],
        out_specs=pl.BlockSpec((1, tq, FOX_WIDTH), lambda b, i, j: (b, i, 0)),
        scratch_shapes=[pltpu.VMEM((FOX_HEADS, tq, LANES), F32), pltpu.VMEM((FOX_HEADS, tq, LANES), F32),
                        pltpu.VMEM((FOX_HEADS, tq, FOX_HEAD_DIM), F32)],
        compiler_params=_cparams(("parallel", "parallel", "arbitrary")),
        name="fox",
    )(q, k, v)


def _merge_kernel(zg_ref, ygm_ref, yrw_ref, yfox_ref, x_ref, g1_ref, sc2_ref, sh2_ref, pb_ref, wo_ref, wr_ref, br_ref,
                  x1_o, h2_o, idx_o, gate_o, rank_o, cnt_o, carry_ref):
    tm = x_ref.shape[1]

    @pl.when((pl.program_id(0) == 0) & (pl.program_id(1) == 0))
    def _():
        carry_ref[...] = jnp.zeros_like(carry_ref)

    sg = jax.nn.sigmoid(zg_ref[0])
    p_gm = jnp.dot(ygm_ref[0], pb_ref[0:GM_WIDTH, :], preferred_element_type=F32)
    y_rw = jnp.concatenate([yrw_ref[0, h] for h in range(RW_HEADS)], axis=-1)
    p_rw = jnp.dot(y_rw, pb_ref[GM_WIDTH:GM_WIDTH + RW_WIDTH, :], preferred_element_type=F32)
    p_fox = jnp.dot(yfox_ref[0], pb_ref[GM_WIDTH + RW_WIDTH:, :], preferred_element_type=F32)
    merged = sg[:, 0:D_MODEL] * p_gm + sg[:, D_MODEL:2 * D_MODEL] * p_rw + sg[:, 2 * D_MODEL:] * p_fox
    x1 = x_ref[0] + g1_ref[0] * jnp.dot(merged.astype(BF16), wo_ref[...], preferred_element_type=F32)
    x1_o[0] = x1
    h2 = x1 * lax.rsqrt(jnp.mean(x1 * x1, axis=-1, keepdims=True) + EPS) * (1.0 + sc2_ref[0]) + sh2_ref[0]
    h2_o[0] = _pack_halves(h2)

    h_hi, h_lo, _ = _split3(h2)
    w_hi, w_lo, _ = _split3(wr_ref[...])
    logits = (jnp.dot(h_hi, w_hi, preferred_element_type=F32) + jnp.dot(h_hi, w_lo, preferred_element_type=F32)
              + jnp.dot(h_lo, w_hi, preferred_element_type=F32)) + br_ref[...]
    lane = lax.broadcasted_iota(I32, (tm, N_EXPERTS), 1)
    vals, idxs = [], []
    rest = logits
    for _ in range(TOP_K):
        m = jnp.max(rest, axis=-1, keepdims=True)
        am = jnp.min(jnp.where(rest == m, lane, N_EXPERTS), axis=-1, keepdims=True)
        vals.append(m)
        idxs.append(am)
        rest = jnp.where(lane == am, -jnp.inf, rest)
    exps = [jnp.exp(val - vals[0]) for val in vals]
    denom = exps[0] + exps[1] + exps[2] + exps[3]

    onehot = jnp.zeros((tm, N_EXPERTS), F32)
    for am in idxs:
        onehot = onehot + jnp.where(lane == am, 1.0, 0.0)
    ri = lax.broadcasted_iota(I32, (tm, tm), 0)
    ci = lax.broadcasted_iota(I32, (tm, tm), 1)
    before = jnp.where(ri > ci, 1.0, 0.0).astype(BF16)
    seen = carry_ref[...] + jnp.dot(before, onehot.astype(BF16), preferred_element_type=F32)
    lane_k = lax.broadcasted_iota(I32, (tm, TOP_K), 1)
    idx_out = jnp.zeros((tm, TOP_K), I32)
    gate_out = jnp.zeros((tm, TOP_K), F32)
    rank_out = jnp.zeros((tm, TOP_K), I32)
    for kk in range(TOP_K):
        rank = jnp.sum(jnp.where(lane == idxs[kk], seen, 0.0), axis=-1, keepdims=True).astype(I32)
        idx_out = jnp.where(lane_k == kk, idxs[kk], idx_out)
        gate_out = jnp.where(lane_k == kk, exps[kk] / denom, gate_out)
        rank_out = jnp.where(lane_k == kk, rank, rank_out)
    idx_o[0] = idx_out
    gate_o[0] = gate_out
    rank_o[0] = rank_out
    total = carry_ref[...] + jnp.sum(onehot, axis=0, keepdims=True)
    carry_ref[...] = total
    cnt_o[...] = total.astype(I32)


def _merge(z, y_gm, y_rw, y_fox, x, gate1, scale2, shift2, w_branch, w_o, w_router, b_router, layer):
    bsz, t_len, d = x.shape
    tm = min(512, t_len)
    row = lambda w: pl.BlockSpec((1, tm, w), lambda b, i: (b, i, 0))
    mod = pl.BlockSpec((1, 1, d), lambda b, i: (b, 0, 0))
    full = lambda shape: pl.BlockSpec(shape, lambda b, i: (0,) * len(shape))
    return pl.pallas_call(
        _merge_kernel,
        out_shape=(jax.ShapeDtypeStruct((bsz, t_len, d), F32), jax.ShapeDtypeStruct((bsz, t_len, d // 2), I32),
                   jax.ShapeDtypeStruct((bsz, t_len, TOP_K), I32), jax.ShapeDtypeStruct((bsz, t_len, TOP_K), F32),
                   jax.ShapeDtypeStruct((bsz, t_len, TOP_K), I32), jax.ShapeDtypeStruct((1, N_EXPERTS), I32)),
        grid=(bsz, t_len // tm),
        in_specs=[row(N_BRANCH * D_MODEL), row(GM_WIDTH),
                  pl.BlockSpec((1, RW_HEADS, tm, RW_HEAD_DIM), lambda b, i: (b, 0, i, 0)),
                  row(FOX_WIDTH), row(d), mod, mod, mod,
                  pl.BlockSpec((MIX_WIDTH, d), lambda b, i: (layer, 0)), pl.BlockSpec((d, d), lambda b, i: (layer, 0)),
                  full(w_router.shape), full((1, N_EXPERTS))],
        out_specs=(row(d), row(d // 2), row(TOP_K), row(TOP_K), row(TOP_K), full((1, N_EXPERTS))),
        scratch_shapes=[pltpu.VMEM((1, N_EXPERTS), F32)],
        compiler_params=_cparams(("arbitrary", "arbitrary")),
        name="merge_router",
    )(z, y_gm, y_rw, y_fox, x, gate1, scale2, shift2, w_branch, w_o, w_router, b_router.reshape(1, N_EXPERTS))


def _sc_mesh():
    return plsc.VectorSubcoreMesh(core_axis_name="c", subcore_axis_name="s",
                                  num_cores=SC_CORES, num_subcores=SC_SUBCORES)


def _sc_worker():
    return lax.axis_index("s") * SC_CORES + lax.axis_index("c")


def _sc_scatter_rows(src, idx3, n_out):
    _, d = src.shape
    n_copy, n_grp, _ = idx3.shape
    grp_per_w = n_grp // SC_WORKERS

    def body(src_hbm, idx_hbm, out_hbm, idx_v, rows_v):
        g0 = _sc_worker() * grp_per_w
        for q in range(n_copy):
            pltpu.sync_copy(idx_hbm.at[q, pl.ds(g0, grp_per_w)], idx_v.at[pl.ds(q * grp_per_w, grp_per_w)])

        @pl.loop(0, grp_per_w)
        def _(j):
            pltpu.sync_copy(src_hbm.at[pl.ds((g0 + j) * SC_ROWS, SC_ROWS)], rows_v)
            for q in range(n_copy):
                pltpu.sync_copy(rows_v, out_hbm.at[idx_v.at[q * grp_per_w + j]])

    return pl.kernel(
        body, out_type=jax.ShapeDtypeStruct((n_out, d), src.dtype), mesh=_sc_mesh(),
        scratch_types=[pltpu.VMEM((n_copy * grp_per_w, SC_ROWS), I32), pltpu.VMEM((SC_ROWS, d), src.dtype)],
        name="sc_dispatch",
    )(src, idx3)


def _sc_gather_rows(table, idx2):
    _, d = table.shape
    n_grp, _ = idx2.shape
    grp_per_w = n_grp // SC_WORKERS

    def body(table_hbm, idx_hbm, out_hbm, idx_v, rows_v):
        g0 = _sc_worker() * grp_per_w
        pltpu.sync_copy(idx_hbm.at[pl.ds(g0, grp_per_w)], idx_v)

        @pl.loop(0, grp_per_w)
        def _(j):
            pltpu.sync_copy(table_hbm.at[idx_v.at[j]], rows_v)
            pltpu.sync_copy(rows_v, out_hbm.at[pl.ds((g0 + j) * SC_ROWS, SC_ROWS)])

    return pl.kernel(
        body, out_type=jax.ShapeDtypeStruct((n_grp * SC_ROWS, d), table.dtype), mesh=_sc_mesh(),
        scratch_types=[pltpu.VMEM((grp_per_w, SC_ROWS), I32), pltpu.VMEM((SC_ROWS, d), table.dtype)],
        name="sc_combine_gather",
    )(table, idx2)


def _ffn_kernel(be_ref, nv_ref, x_ref, wgu_ref, bgu_ref, wd_ref, bd_ref, o_ref, wgu_b, wd_b):
    i = pl.program_id(0)
    n_valid = nv_ref[i]

    @pl.when((i == 0) | (be_ref[i] != be_ref[jnp.maximum(i - 1, 0)]))
    def _():
        wgu_b[...] = wgu_ref[0].astype(BF16)
        wd_b[...] = wd_ref[0].astype(BF16)

    @pl.when(n_valid > 0)
    def _():
        rowid = lax.broadcasted_iota(I32, x_ref.shape, 0)
        xp = jnp.where(rowid < n_valid, x_ref[...], 0)
        x = jnp.concatenate(_unpack_halves(xp), axis=-1).astype(BF16)
        gu = jnp.dot(x, wgu_b[...], preferred_element_type=F32) + bgu_ref[0]
        g_ = jnp.minimum(gu[:, :D_FF], SWIGLU_LIMIT)
        u_ = jnp.clip(gu[:, D_FF:], -SWIGLU_LIMIT, SWIGLU_LIMIT)
        act = (u_ + 1.0) * (g_ * jax.nn.sigmoid(SWIGLU_ALPHA * g_))
        o_ref[...] = _pack_halves(jnp.dot(act.astype(BF16), wd_b[...], preferred_element_type=F32) + bd_ref[0])

    @pl.when(n_valid <= 0)
    def _():
        o_ref[...] = jnp.zeros_like(o_ref)


def _ffn(block_expert, block_valid, xin, w_gate_up, b_gate_up, w_down, b_down):
    n_rows, dp = xin.shape
    d = 2 * dp
    n_blocks = n_rows // MOE_BLOCK
    grid_spec = pltpu.PrefetchScalarGridSpec(
        num_scalar_prefetch=2,
        grid=(n_blocks,),
        in_specs=[pl.BlockSpec((MOE_BLOCK, dp), lambda i, be, nv: (i, 0)),
                  pl.BlockSpec((1, d, 2 * D_FF), lambda i, be, nv: (be[i], 0, 0)),
                  pl.BlockSpec((1, 1, 2 * D_FF), lambda i, be, nv: (be[i], 0, 0)),
                  pl.BlockSpec((1, D_FF, d), lambda i, be, nv: (be[i], 0, 0)),
                  pl.BlockSpec((1, 1, d), lambda i, be, nv: (be[i], 0, 0))],
        out_specs=pl.BlockSpec((MOE_BLOCK, dp), lambda i, be, nv: (i, 0)),
        scratch_shapes=[pltpu.VMEM((d, 2 * D_FF), BF16), pltpu.VMEM((D_FF, d), BF16)],
    )
    return pl.pallas_call(
        _ffn_kernel,
        out_shape=jax.ShapeDtypeStruct((n_rows, dp), I32),
        grid_spec=grid_spec,
        compiler_params=pltpu.CompilerParams(dimension_semantics=("arbitrary",), vmem_limit_bytes=FFN_VMEM_LIMIT),
        name="expert_ffn",
    )(block_expert, block_valid, xin, w_gate_up, b_gate_up, w_down, b_down)


def _combine_kernel(x1_ref, g2_ref, gate_ref, yg_ref, o_ref):
    gate = gate_ref[0]
    y_lo = y_hi = None
    for q in range(TOP_K):
        lo, hi = _unpack_halves(yg_ref[q, 0])
        wq = gate[:, q:q + 1]
        y_lo = wq * lo if y_lo is None else y_lo + wq * lo
        y_hi = wq * hi if y_hi is None else y_hi + wq * hi
    o_ref[0] = x1_ref[0] + g2_ref[0] * jnp.concatenate([y_lo, y_hi], axis=-1)


def _combine(x1, gate2, gate, yg):
    bsz, t_len, d = x1.shape
    tm = min(512, t_len)
    return pl.pallas_call(
        _combine_kernel,
        out_shape=jax.ShapeDtypeStruct((bsz, t_len, d), F32),
        grid=(bsz, t_len // tm),
        in_specs=[pl.BlockSpec((1, tm, d), lambda b, i: (b, i, 0)),
                  pl.BlockSpec((1, 1, d), lambda b, i: (b, 0, 0)),
                  pl.BlockSpec((1, tm, TOP_K), lambda b, i: (b, i, 0)),
                  pl.BlockSpec((TOP_K, 1, tm, d // 2), lambda b, i: (0, b, i, 0))],
        out_specs=pl.BlockSpec((1, tm, d), lambda b, i: (b, i, 0)),
        compiler_params=_cparams(("parallel", "parallel")),
        name="moe_combine",
    )(x1, gate2, gate, yg)


def _moe(x1, gate2, h2, top_idx, gate, rank, counts, w_gate_up, b_gate_up, w_down, b_down, layer):
    bsz, t_len, d = h2.shape
    n_tok = bsz * t_len
    n_assign = n_tok * TOP_K
    n_blocks = -(-n_assign // MOE_BLOCK) + N_EXPERTS
    counts = counts.reshape(N_EXPERTS)
    blocks_e = (counts + MOE_BLOCK - 1) // MOE_BLOCK
    blk_end = jnp.cumsum(blocks_e)
    blk_start = blk_end - blocks_e
    experts = jnp.arange(N_EXPERTS, dtype=I32)
    onehot = top_idx.reshape(n_tok, TOP_K, 1) == experts
    dest = jnp.sum(jnp.where(onehot, blk_start * MOE_BLOCK, 0), axis=-1) + rank.reshape(n_tok, TOP_K)
    dest_t = dest.T.astype(I32)
    blk = jnp.arange(n_blocks, dtype=I32)
    block_expert = jnp.minimum(jnp.sum(blk_end[None, :] <= blk[:, None], axis=1), N_EXPERTS - 1).astype(I32)
    be_hot = block_expert[:, None] == experts
    cnt_b = jnp.sum(jnp.where(be_hot, counts, 0), axis=1)
    start_b = jnp.sum(jnp.where(be_hot, blk_start, 0), axis=1)
    block_valid = jnp.clip(cnt_b - (blk - start_b) * MOE_BLOCK, 0, MOE_BLOCK).astype(I32)
    xin = _sc_scatter_rows(h2.reshape(n_tok, d), dest_t.reshape(TOP_K, n_tok // SC_ROWS, SC_ROWS),
                           n_blocks * MOE_BLOCK)
    yb = _ffn(block_expert + layer * N_EXPERTS, block_valid, xin, w_gate_up, b_gate_up, w_down, b_down)
    yg = _sc_gather_rows(yb, dest_t.reshape(n_assign // SC_ROWS, SC_ROWS))
    return _combine(x1, gate2, gate, yg.reshape(TOP_K, bsz, t_len, d))


def _permute_kernel(w_ref, o_ref):
    o_gm = 0
    o_rw = o_gm + 2 * GM_WIDTH
    o_fox = o_rw + RW_SHIFT_WIDTH
    o_f = o_fox + 3 * FOX_WIDTH
    o_gate = o_f + FOX_HEADS
    w = w_ref[0]
    o_ref[0, :, Z_GATE:Z_FOX] = w[:, o_gate:o_gate + N_BRANCH * D_MODEL].astype(BF16)
    o_ref[0, :, Z_FOX:Z_GM] = w[:, o_fox:o_f].astype(BF16)
    o_ref[0, :, Z_GM:Z_RW] = w[:, o_gm:o_rw].astype(BF16)
    o_ref[0, :, Z_RW:Z_F] = w[:, o_rw:o_fox].astype(BF16)
    tail = jnp.concatenate([w[:, o_f:o_gate], jnp.zeros((w.shape[0], Z_WIDTH - Z_F - FOX_HEADS), F32)], axis=-1)
    o_ref[0, :, Z_F:Z_WIDTH] = tail.astype(BF16)


def _permute_w_in(w_in):
    n_layer, d, w_cols = w_in.shape
    tr = 256
    return pl.pallas_call(
        _permute_kernel,
        out_shape=jax.ShapeDtypeStruct((n_layer, d, Z_WIDTH), BF16),
        grid=(n_layer, d // tr),
        in_specs=[pl.BlockSpec((1, tr, w_cols), lambda l, i: (l, i, 0))],
        out_specs=pl.BlockSpec((1, tr, Z_WIDTH), lambda l, i: (l, i, 0)),
        compiler_params=_cparams(("parallel", "parallel")),
        name="permute_w_in",
    )(w_in)


def _layer(x, mod, w_in_p, gm_v_gain, gm_w_s, gm_b_s, mu_pad, w_lora, rw_w0, rw_a0, rw_k_k, rw_k_a, rw_r_k,
           rw_gn_gain, rw_gn_bias, f_bias_pad, fox_q_gain, fox_k_gain, w_branch, w_o, w_router, b_router,
           w_gate_up, b_gate_up, w_down, b_down, layer):
    shift1, scale1, gate1, shift2, scale2, gate2 = (mod[:, i][:, None, :] for i in range(6))
    z = _inproj(x, scale1, shift1, w_in_p, layer)
    y_gm = _gmlp(z, gm_v_gain, gm_w_s, gm_b_s)
    r, lw, k, v, a, b, g = _rwprep(z, mu_pad, w_lora, rw_w0, rw_a0, rw_k_k, rw_k_a)
    y_rw = _rwscan(r, lw, k, v, a, b, g, rw_r_k, rw_gn_gain, rw_gn_bias)
    q, kf, vf = _foxprep(z, f_bias_pad, fox_q_gain, fox_k_gain)
    y_fox = _fox(q, kf, vf)
    x1, h2, top_idx, gate, rank, counts = _merge(z, y_gm, y_rw, y_fox, x, gate1, scale2, shift2,
                                                 w_branch, w_o, w_router, b_router, layer)
    return _moe(x1, gate2, h2, top_idx, gate, rank, counts, w_gate_up, b_gate_up, w_down, b_down, layer)


def kernel(x, c, w_ada, b_ada, w_in, gm_v_gain, gm_w_s, gm_b_s, rw_mu, rw_w0, rw_w2, rw_a0, rw_a2, rw_g2, rw_k_k,
           rw_k_a, rw_r_k, rw_gn_gain, rw_gn_bias, fox_f_bias, fox_q_gain, fox_k_gain, w_branch, w_o, w_router,
           b_router, w_gate_up, b_gate_up, w_down, b_down):
    n_layer = w_ada.shape[0]
    bsz = x.shape[0]
    c_pad = jnp.zeros((8, D_MODEL), F32).at[:bsz].set(c)
    mod = _adaln(c_pad, w_ada, b_ada)[:, :bsz].reshape(n_layer, bsz, 6, D_MODEL)
    w_in_p = _permute_w_in(w_in)
    mu_pad = jnp.pad(rw_mu, ((0, 0), (0, RW_BLOCK - RW_SHIFT_WIDTH)))
    w_lora = jnp.zeros((n_layer, RW_LORA, 3 * RW_WIDTH), F32)
    w_lora = w_lora.at[:, 0:RW_DECAY_LORA, 0:RW_WIDTH].set(rw_w2)
    w_lora = w_lora.at[:, RW_DECAY_LORA:RW_DECAY_LORA + RW_ICLR_LORA, RW_WIDTH:2 * RW_WIDTH].set(rw_a2)
    w_lora = w_lora.at[:, RW_DECAY_LORA + RW_ICLR_LORA:, 2 * RW_WIDTH:].set(rw_g2)
    f_bias_pad = jnp.pad(fox_f_bias, ((0, 0), (0, LANES - FOX_HEADS)))
    w_in_p = w_in_p.reshape(n_layer * D_MODEL, Z_WIDTH)
    w_branch_b = w_branch.astype(BF16).reshape(n_layer * MIX_WIDTH, D_MODEL)
    w_o_b = w_o.astype(BF16).reshape(n_layer * D_MODEL, D_MODEL)
    w_gu = w_gate_up.reshape(n_layer * N_EXPERTS, D_MODEL, 2 * D_FF)
    b_gu = b_gate_up.reshape(n_layer * N_EXPERTS, 1, 2 * D_FF)
    w_dn = w_down.reshape(n_layer * N_EXPERTS, D_FF, D_MODEL)
    b_dn = b_down.reshape(n_layer * N_EXPERTS, 1, D_MODEL)
    for l in range(n_layer):
        x = _layer(x, mod[l], w_in_p, gm_v_gain[l], gm_w_s[l], gm_b_s[l], mu_pad[l:l + 1], w_lora[l], rw_w0[l],
                   rw_a0[l], rw_k_k[l], rw_k_a[l], rw_r_k[l], rw_gn_gain[l], rw_gn_bias[l], f_bias_pad[l:l + 1],
                   fox_q_gain[l], fox_k_gain[l], w_branch_b, w_o_b, w_router[l], b_router[l],
                   w_gu, b_gu, w_dn, b_dn, l)
    return x
```

```python
import functools

import jax
import jax.numpy as jnp
import numpy as np
from jax import lax
from jax.experimental import pallas as pl
from jax.experimental.pallas import tpu as pltpu
from jax.experimental.pallas import tpu_sc as plsc

F32 = jnp.float32
BF16 = jnp.bfloat16
I32 = jnp.int32
HIGHEST = lax.Precision.HIGHEST

D_MODEL = 1024
GM_CHUNK = 128
GM_GROUPS = 4
GM_WIDTH = 256
GM_GROUP_DIM = GM_WIDTH // GM_GROUPS
RW_HEADS = 4
RW_HEAD_DIM = 64
RW_WIDTH = RW_HEADS * RW_HEAD_DIM
RW_DECAY_LORA = 32
RW_ICLR_LORA = 32
RW_GATE_LORA = 64
RW_LORA = RW_DECAY_LORA + RW_ICLR_LORA + RW_GATE_LORA
RW_SHIFT_WIDTH = 3 * RW_WIDTH + RW_LORA
RW_GN_EPS = 64e-5
FOX_HEADS = 8
FOX_HEAD_DIM = 64
FOX_WIDTH = FOX_HEADS * FOX_HEAD_DIM
ATTN_SCALE = FOX_HEAD_DIM ** -0.5
MASK_VALUE = -1e30
LOG2E = 1.4426950408889634
N_BRANCH = 3
MIX_WIDTH = GM_WIDTH + RW_WIDTH + FOX_WIDTH
N_EXPERTS = 32
TOP_K = 4
D_FF = D_MODEL
SWIGLU_LIMIT = 7.0
SWIGLU_ALPHA = 1.702
MOE_BLOCK = 256
EPS = 1e-6

Z_GATE = 0
Z_FOX = N_BRANCH * D_MODEL
Z_GM = Z_FOX + 3 * FOX_WIDTH
Z_RW = Z_GM + 2 * GM_WIDTH
RW_BLOCK = 1024
Z_F = Z_RW + RW_SHIFT_WIDTH
Z_WIDTH = Z_RW + RW_BLOCK
LANES = 128
RW_CHUNK = 64
RW_PREP_UNROLL = 4

VMEM_LIMIT = 48 * 1024 * 1024
FFN_VMEM_LIMIT = 56 * 1024 * 1024
SC_CORES = 2
SC_SUBCORES = 16
SC_WORKERS = SC_CORES * SC_SUBCORES
SC_ROWS = 64


def _cparams(sem):
    return pltpu.CompilerParams(dimension_semantics=sem, vmem_limit_bytes=VMEM_LIMIT)


def _mm(a, b):
    return jnp.dot(a.astype(BF16), b.astype(BF16), preferred_element_type=F32)


def _mm_nt(a, b):
    return lax.dot_general(a.astype(BF16), b.astype(BF16), (((1,), (1,)), ((), ())), preferred_element_type=F32)


def _mm_tn(a, b):
    return lax.dot_general(a.astype(BF16), b.astype(BF16), (((0,), (0,)), ((), ())), preferred_element_type=F32)


def _split3(x):
    hi = x.astype(BF16)
    r1 = x - hi.astype(F32)
    mid = r1.astype(BF16)
    lo = (r1 - mid.astype(F32)).astype(BF16)
    return hi, mid, lo


def _tri_cumsum(x, n):
    ri = lax.broadcasted_iota(I32, (n, n), 0)
    ci = lax.broadcasted_iota(I32, (n, n), 1)
    ones = jnp.where(ri >= ci, 1.0, 0.0).astype(BF16)
    hi, mid, lo = _split3(x)
    return (jnp.dot(ones, hi, preferred_element_type=F32) + jnp.dot(ones, mid, preferred_element_type=F32)
            + jnp.dot(ones, lo, preferred_element_type=F32))


def _pack_halves(x):
    w = x.shape[1] // 2
    hi = pltpu.bitcast(x[:, :w].astype(BF16).astype(F32), jnp.uint32)
    lo = pltpu.bitcast(x[:, w:].astype(BF16).astype(F32), jnp.uint32)
    return pltpu.bitcast(hi | (lo >> 16), I32)


def _unpack_halves(p):
    u = pltpu.bitcast(p, jnp.uint32)
    return pltpu.bitcast(u & jnp.uint32(0xFFFF0000), F32), pltpu.bitcast(u << 16, F32)


def _log_sigmoid(x):
    return jnp.minimum(x, 0.0) - jnp.log1p(jnp.exp(-jnp.abs(x)))


def _adaln_kernel(c_ref, w_ref, b_ref, o_ref):
    c = c_ref[...]
    s = c * jax.nn.sigmoid(c)
    o_ref[0] = jnp.dot(s, w_ref[0], preferred_element_type=F32, precision=HIGHEST) + b_ref[0]


def _adaln(c_pad, w_ada, b_ada):
    n_layer, d, w6 = w_ada.shape
    tn = 1536
    return pl.pallas_call(
        _adaln_kernel,
        out_shape=jax.ShapeDtypeStruct((n_layer, c_pad.shape[0], w6), F32),
        grid=(n_layer, w6 // tn),
        in_specs=[pl.BlockSpec(c_pad.shape, lambda l, j: (0, 0)),
                  pl.BlockSpec((1, d, tn), lambda l, j: (l, 0, j)),
                  pl.BlockSpec((1, 1, tn), lambda l, j: (l, 0, j))],
        out_specs=pl.BlockSpec((1, c_pad.shape[0], tn), lambda l, j: (l, 0, j)),
        compiler_params=_cparams(("parallel", "parallel")),
        name="adaln",
    )(c_pad, w_ada, b_ada.reshape(n_layer, 1, w6))


def _inproj_kernel(x_ref, sc_ref, sh_ref, w_ref, o_ref, xn_ref):
    @pl.when(pl.program_id(2) == 0)
    def _():
        x = x_ref[0]
        xn = x * lax.rsqrt(jnp.mean(x * x, axis=-1, keepdims=True) + EPS)
        xn_ref[...] = (xn * (1.0 + sc_ref[0]) + sh_ref[0]).astype(BF16)

    o_ref[0] = jnp.dot(xn_ref[...], w_ref[...], preferred_element_type=F32)


def _inproj(x, scale, shift, w, layer):
    bsz, t_len, d = x.shape
    tm = min(1024, t_len)
    tn = 1024
    return pl.pallas_call(
        _inproj_kernel,
        out_shape=jax.ShapeDtypeStruct((bsz, t_len, Z_WIDTH), F32),
        grid=(bsz, t_len // tm, Z_WIDTH // tn),
        in_specs=[pl.BlockSpec((1, tm, d), lambda b, i, j: (b, i, 0)),
                  pl.BlockSpec((1, 1, d), lambda b, i, j: (b, 0, 0)),
                  pl.BlockSpec((1, 1, d), lambda b, i, j: (b, 0, 0)),
                  pl.BlockSpec((d, tn), lambda b, i, j: (layer, j))],
        out_specs=pl.BlockSpec((1, tm, tn), lambda b, i, j: (b, i, j)),
        scratch_shapes=[pltpu.VMEM((tm, d), BF16)],
        compiler_params=_cparams(("parallel", "parallel", "arbitrary")),
        name="inproj",
    )(x, scale, shift, w)


def _gmlp_kernel(z_ref, gain_ref, ws_ref, bst_ref, o_ref):
    tm = z_ref.shape[1]
    z = z_ref[0]
    u = jax.nn.gelu(z[:, :GM_WIDTH])
    v = jax.nn.gelu(z[:, GM_WIDTH:])
    v = v * lax.rsqrt(jnp.mean(v * v, axis=-1, keepdims=True) + EPS) * gain_ref[...]
    vb = v.astype(BF16)
    grp = lax.broadcasted_iota(I32, (GM_CHUNK, GM_WIDTH), 1) // GM_GROUP_DIM
    ri = lax.broadcasted_iota(I32, (GM_CHUNK, GM_CHUNK), 0)
    ci = lax.broadcasted_iota(I32, (GM_CHUNK, GM_CHUNK), 1)
    causal = ri >= ci
    bias = jnp.zeros((GM_CHUNK, GM_WIDTH), F32)
    ws = []
    for g in range(GM_GROUPS):
        ws.append(jnp.where(causal, ws_ref[g], 0.0).astype(BF16))
        bias = jnp.where(grp == g, bst_ref[:, g:g + 1], bias)
    for c in range(tm // GM_CHUNK):
        rows = slice(c * GM_CHUNK, (c + 1) * GM_CHUNK)
        vc = vb[rows]
        mixed = bias
        for g in range(GM_GROUPS):
            m = jnp.dot(ws[g], vc, preferred_element_type=F32)
            mixed = mixed + jnp.where(grp == g, m, 0.0)
        o_ref[0, rows, :] = (u[rows] * mixed).astype(o_ref.dtype)


def _gmlp(z, gain, w_s, b_s):
    bsz, t_len, _ = z.shape
    tm = min(512, t_len)
    return pl.pallas_call(
        _gmlp_kernel,
        out_shape=jax.ShapeDtypeStruct((bsz, t_len, GM_WIDTH), BF16),
        grid=(bsz, t_len // tm),
        in_specs=[pl.BlockSpec((1, tm, 2 * GM_WIDTH), lambda b, i: (b, i, Z_GM // (2 * GM_WIDTH))),
                  pl.BlockSpec((1, GM_WIDTH), lambda b, i: (0, 0)),
                  pl.BlockSpec((GM_GROUPS, GM_CHUNK, GM_CHUNK), lambda b, i: (0, 0, 0)),
                  pl.BlockSpec((GM_CHUNK, GM_GROUPS), lambda b, i: (0, 0))],
        out_specs=pl.BlockSpec((1, tm, GM_WIDTH), lambda b, i: (b, i, 0)),
        compiler_params=_cparams(("parallel", "parallel")),
        name="gmlp",
    )(z, gain.reshape(1, GM_WIDTH), w_s, b_s.T)


def _rwprep_kernel(z_ref, zp_ref, mu_ref, wl_ref, w0_ref, a0_ref, kk_ref, ka_ref,
                   r_o, lw_o, k_o, v_o, a_o, b_o, g_o):
    tm = z_ref.shape[1]
    z = z_ref[0]
    prev = jnp.where(pl.program_id(1) > 0, zp_ref[0, 7:8, :], 0.0)
    rowid = lax.broadcasted_iota(I32, z.shape, 0)
    zs = jnp.where(rowid == 0, prev, pltpu.roll(z, 1, axis=0))
    zz = z + mu_ref[...] * (zs - z)
    r = zz[:, 0:RW_WIDTH]
    k = zz[:, RW_WIDTH:2 * RW_WIDTH]
    v = zz[:, 2 * RW_WIDTH:3 * RW_WIDTH]
    lo = zz[:, 3 * RW_WIDTH:3 * RW_WIDTH + RW_LORA]
    lane = lax.broadcasted_iota(I32, (tm, RW_LORA), 1)
    act = jnp.where(lane < RW_DECAY_LORA, jnp.tanh(lo),
                    jnp.where(lane < RW_DECAY_LORA + RW_ICLR_LORA, lo, jax.nn.sigmoid(lo)))
    proj = jnp.dot(act, wl_ref[...], preferred_element_type=F32, precision=HIGHEST)
    xw = -(w0_ref[...] + proj[:, 0:RW_WIDTH])
    softplus = jnp.maximum(xw, 0.0) + jnp.log1p(jnp.exp(-jnp.abs(xw)))
    lw = -jnp.exp(-softplus - 0.5)
    a = jax.nn.sigmoid(a0_ref[...] + proj[:, RW_WIDTH:2 * RW_WIDTH])
    g = proj[:, 2 * RW_WIDTH:3 * RW_WIDTH]
    kk = k * kk_ref[...]
    k2 = k * (1.0 + (a - 1.0) * ka_ref[...])
    for h in range(RW_HEADS):
        sl = slice(h * RW_HEAD_DIM, (h + 1) * RW_HEAD_DIM)
        kkh = kk[:, sl]
        nrm = jnp.sqrt(jnp.sum(kkh * kkh, axis=-1, keepdims=True))
        kkh = kkh / jnp.maximum(nrm, 1e-12)
        r_o[0, h] = r[:, sl]
        lw_o[0, h] = lw[:, sl]
        k_o[0, h] = k2[:, sl]
        v_o[0, h] = v[:, sl]
        a_o[0, h] = -kkh
        b_o[0, h] = kkh * a[:, sl]
        g_o[0, h] = g[:, sl]


def _rwprep(z, mu_pad, w_lora, w0, a0, k_k, k_a):
    bsz, t_len, _ = z.shape
    tm = min(512, t_len)
    hm = jax.ShapeDtypeStruct((bsz, RW_HEADS, t_len, RW_HEAD_DIM), F32)
    hm_spec = pl.BlockSpec((1, RW_HEADS, tm, RW_HEAD_DIM), lambda b, i: (b, 0, i, 0))
    vec = lambda n: pl.BlockSpec((1, n), lambda b, i: (0, 0))
    rw_blk = Z_RW // RW_BLOCK
    return pl.pallas_call(
        _rwprep_kernel,
        out_shape=(hm,) * 7,
        grid=(bsz, t_len // tm),
        in_specs=[pl.BlockSpec((1, tm, RW_BLOCK), lambda b, i: (b, i, rw_blk)),
                  pl.BlockSpec((1, 8, RW_BLOCK), lambda b, i: (b, jnp.maximum(i * (tm // 8) - 1, 0), rw_blk)),
                  vec(RW_BLOCK),
                  pl.BlockSpec((RW_LORA, 3 * RW_WIDTH), lambda b, i: (0, 0)),
                  vec(RW_WIDTH), vec(RW_WIDTH), vec(RW_WIDTH), vec(RW_WIDTH)],
        out_specs=(hm_spec,) * 7,
        compiler_params=_cparams(("parallel", "parallel")),
        name="rwprep",
    )(z, z, mu_pad, w_lora, w0.reshape(1, -1), a0.reshape(1, -1), k_k.reshape(1, -1), k_a.reshape(1, -1))


def _rwscan_kernel(r_ref, lw_ref, k_ref, v_ref, a_ref, b_ref, g_ref, rk_ref, gg_ref, gb_ref, o_ref,
                   s_ref, rp_ref, y_ref, gm_ref, h0_ref, we_ref):
    cl = RW_CHUNK
    tb = r_ref.shape[2]
    n_chunk = tb // cl

    @pl.when(pl.program_id(1) == 0)
    def _():
        s_ref[...] = jnp.zeros_like(s_ref)

    n = RW_HEADS * cl
    ri = lax.broadcasted_iota(I32, (n, n), 0)
    ci = lax.broadcasted_iota(I32, (n, n), 1)
    same_head = (ri // cl) == (ci // cl)
    lower = same_head & (ri >= ci)
    strict = same_head & (ri > ci)
    eye = jnp.where(ri == ci, 1.0, 0.0)
    ones_lower = jnp.where(lower, 1.0, 0.0).astype(BF16)

    def prepare(chunks):
        grp = range(len(chunks))
        each = lambda fn: [fn(u) for u in grp]
        rows = [pl.ds(pl.multiple_of(c * cl, cl), cl) for c in chunks]
        stack = lambda ref: each(lambda u: ref[0, :, rows[u], :].reshape(n, RW_HEAD_DIM))
        r, lw, k, v, a, b = (stack(ref) for ref in (r_ref, lw_ref, k_ref, v_ref, a_ref, b_ref))
        parts = each(lambda u: _split3(lw[u]))
        tri = lambda x: jnp.dot(ones_lower, x, preferred_element_type=F32)
        cw = each(lambda u: tri(parts[u][0]) + tri(parts[u][1]) + tri(parts[u][2]))
        w_in = each(lambda u: jnp.exp(cw[u]))
        w_inv = each(lambda u: jnp.exp(-cw[u]))
        rt = each(lambda u: r[u] * w_in[u])
        at = each(lambda u: a[u] * jnp.exp(cw[u] - lw[u]))
        kt = each(lambda u: k[u] * w_inv[u])
        bt = each(lambda u: b[u] * w_inv[u])
        w_end = each(lambda u: w_in[u].reshape(RW_HEADS, cl, RW_HEAD_DIM)[:, cl - 1:cl, :])
        w_end_rows = each(lambda u: jnp.broadcast_to(w_end[u], (RW_HEADS, cl, RW_HEAD_DIM)).reshape(n, RW_HEAD_DIM))
        a_ab = each(lambda u: jnp.where(strict, _mm_nt(at[u], bt[u]), 0.0))
        a_ak = each(lambda u: jnp.where(strict, _mm_nt(at[u], kt[u]), 0.0))
        m_rb = each(lambda u: jnp.where(lower, _mm_nt(rt[u], bt[u]), 0.0))
        m_rk = each(lambda u: jnp.where(lower, _mm_nt(rt[u], kt[u]), 0.0))
        inv = each(lambda u: eye + a_ab[u])
        p = a_ab
        for _ in range(cl.bit_length() - 2):
            p = [_mm(p[u], p[u]) for u in grp]
            inv = [inv[u] + _mm(inv[u], p[u]) for u in grp]
        akv = each(lambda u: _mm(a_ak[u], v[u]))
        ap = each(lambda u: _mm(inv[u], at[u]))
        z0 = each(lambda u: _mm(inv[u], akv[u]))
        bend = each(lambda u: bt[u] * w_end_rows[u])
        kend = each(lambda u: kt[u] * w_end_rows[u])
        rp = each(lambda u: (rt[u] + _mm(m_rb[u], ap[u])).astype(BF16))
        y0 = each(lambda u: _mm(m_rb[u], z0[u]) + _mm(m_rk[u], v[u]))
        for u in grp:
            for h in range(RW_HEADS):
                hs = slice(h * cl, (h + 1) * cl)
                rp_ref[h, rows[u], :] = rp[u][hs]
                y_ref[h, rows[u], :] = y0[u][hs]
                gm_ref[h, rows[u], :] = _mm_tn(ap[u][hs], bend[u][hs]).astype(BF16)
                h0_ref[h, rows[u], :] = _mm_tn(z0[u][hs], bend[u][hs]) + _mm_tn(v[u][hs], kend[u][hs])
                we_ref[h, chunks[u]] = w_end[u][h]

    def prepare_step(i, carry):
        prepare([i * RW_PREP_UNROLL + u for u in range(RW_PREP_UNROLL)])
        return carry

    lax.fori_loop(0, n_chunk // RW_PREP_UNROLL, prepare_step, 0)

    def advance(c, carry):
        rows = pl.ds(pl.multiple_of(c * cl, cl), cl)
        for h in range(RW_HEADS):
            s = s_ref[h]
            sb = s.astype(BF16)
            y_ref[h, rows, :] = y_ref[h, rows, :] + lax.dot_general(
                rp_ref[h, rows, :], sb, (((1,), (1,)), ((), ())), preferred_element_type=F32)
            s_ref[h] = (s * we_ref[h, c] + jnp.dot(sb, gm_ref[h, rows, :], preferred_element_type=F32)
                        + h0_ref[h, rows, :])
        return carry

    lax.fori_loop(0, n_chunk, advance, 0)

    for h in range(RW_HEADS):
        y = y_ref[h]
        mu = jnp.mean(y, axis=-1, keepdims=True)
        yc = y - mu
        var = jnp.mean(yc * yc, axis=-1, keepdims=True)
        yn = yc * lax.rsqrt(var + RW_GN_EPS) * gg_ref[h] + gb_ref[h]
        v = v_ref[0, h]
        bonus = jnp.sum(r_ref[0, h] * k_ref[0, h] * rk_ref[h], axis=-1, keepdims=True) * v
        o_ref[0, h] = ((yn + bonus) * g_ref[0, h]).astype(o_ref.dtype)


def _rwscan(r, lw, k, v, a, b, g, r_k, gn_gain, gn_bias):
    bsz, _, t_len, _ = r.shape
    tb = min(512, t_len)
    hm_spec = pl.BlockSpec((1, RW_HEADS, tb, RW_HEAD_DIM), lambda bi, i: (bi, 0, i, 0))
    par = pl.BlockSpec((RW_HEADS, 1, RW_HEAD_DIM), lambda bi, i: (0, 0, 0))
    hshape = (RW_HEADS, 1, RW_HEAD_DIM)
    return pl.pallas_call(
        _rwscan_kernel,
        out_shape=jax.ShapeDtypeStruct((bsz, RW_HEADS, t_len, RW_HEAD_DIM), BF16),
        grid=(bsz, t_len // tb),
        in_specs=[hm_spec] * 7 + [par] * 3,
        out_specs=hm_spec,
        scratch_shapes=[pltpu.VMEM((RW_HEADS, RW_HEAD_DIM, RW_HEAD_DIM), F32),
                        pltpu.VMEM((RW_HEADS, tb, RW_HEAD_DIM), BF16), pltpu.VMEM((RW_HEADS, tb, RW_HEAD_DIM), F32),
                        pltpu.VMEM((RW_HEADS, tb, RW_HEAD_DIM), BF16), pltpu.VMEM((RW_HEADS, tb, RW_HEAD_DIM), F32),
                        pltpu.VMEM((RW_HEADS, tb // RW_CHUNK, 1, RW_HEAD_DIM), F32)],
        compiler_params=_cparams(("parallel", "arbitrary")),
        name="rwscan",
    )(r, lw, k, v, a, b, g, r_k.reshape(hshape), gn_gain.reshape(hshape), gn_bias.reshape(hshape))


FOX_PAIRS = FOX_HEADS // 2
FOX_EXTRA = 3


def _fox_bias_selector():
    sel = np.zeros((LANES, 2 * FOX_HEADS * LANES), np.float32)
    for h in range(FOX_HEADS):
        base = FOX_HEAD_DIM if h % 2 == 0 else 0
        for p in range(FOX_EXTRA):
            sel[p * FOX_HEADS + h, h * LANES + base + p] = 1.0
            sel[p * FOX_HEADS + h, (FOX_HEADS + h) * LANES + base + FOX_EXTRA + p] = -1.0
    return sel


def _foxprep_kernel(z_ref, f_ref, fb_ref, qg_ref, kg_ref, sel_ref, q_o, k_o, vt_o, carry_ref):
    tm = z_ref.shape[1]

    @pl.when(pl.program_id(1) == 0)
    def _():
        carry_ref[...] = jnp.zeros_like(carry_ref)

    log_f = _log_sigmoid(f_ref[0] + fb_ref[...])
    cum = carry_ref[...] + _tri_cumsum(log_f, tm)
    carry_ref[...] = cum[tm - 1:tm, :]
    lane = lax.broadcasted_iota(I32, (tm, LANES), 1)
    hi, mid, lo = (p.astype(F32) for p in _split3(cum * LOG2E))
    packed = jnp.where(lane < FOX_HEADS, hi,
                       jnp.where(lane < 2 * FOX_HEADS, pltpu.roll(mid, FOX_HEADS, axis=1),
                                 pltpu.roll(lo, 2 * FOX_HEADS, axis=1)))
    packed = jnp.where(lane < FOX_EXTRA * FOX_HEADS, packed, 0.0).astype(BF16)
    extra = jnp.dot(packed, sel_ref[...], preferred_element_type=F32)

    left = lane < FOX_HEAD_DIM
    in_half = lane % FOX_HEAD_DIM
    ones_q = jnp.where((in_half >= FOX_EXTRA) & (in_half < 2 * FOX_EXTRA), 1.0, 0.0)
    ones_k = jnp.where(in_half < FOX_EXTRA, 1.0, 0.0)

    def normed(block, gain):
        sq = block * block
        s_left = jnp.sum(jnp.where(left, sq, 0.0), axis=-1, keepdims=True)
        s_right = jnp.sum(jnp.where(left, 0.0, sq), axis=-1, keepdims=True)
        ms = jnp.where(left, s_left, s_right) * (1.0 / FOX_HEAD_DIM)
        return block * lax.rsqrt(ms + EPS) * gain

    for j in range(FOX_PAIRS):
        qn = normed(z_ref[0, :, j * LANES:(j + 1) * LANES], qg_ref[...] * (ATTN_SCALE * LOG2E))
        kn = normed(z_ref[0, :, FOX_WIDTH + j * LANES:FOX_WIDTH + (j + 1) * LANES], kg_ref[...])
        for par in range(2):
            h = 2 * j + par
            own = left if par == 0 else jnp.logical_not(left)
            q_o[0, h] = jnp.where(own, qn, extra[:, h * LANES:(h + 1) * LANES] + ones_q).astype(BF16)
            k_o[0, h] = jnp.where(own, kn, extra[:, (FOX_HEADS + h) * LANES:(FOX_HEADS + h + 1) * LANES]
                                  + ones_k).astype(BF16)
    ri = lax.broadcasted_iota(I32, (FOX_WIDTH, FOX_WIDTH), 0)
    ci = lax.broadcasted_iota(I32, (FOX_WIDTH, FOX_WIDTH), 1)
    eye = jnp.where(ri == ci, 1.0, 0.0).astype(BF16)
    v = z_ref[0, :, 2 * FOX_WIDTH:3 * FOX_WIDTH].astype(BF16)
    vt_o[0] = lax.dot_general(eye, v, (((1,), (1,)), ((), ())), preferred_element_type=F32).astype(BF16)


def _foxprep(z, f_bias_pad, q_gain, k_gain):
    bsz, t_len, _ = z.shape
    tm = min(512, t_len)
    qk = jax.ShapeDtypeStruct((bsz, FOX_HEADS, t_len, LANES), BF16)
    qk_spec = pl.BlockSpec((1, FOX_HEADS, tm, LANES), lambda b, i: (b, 0, i, 0))
    sel = jnp.asarray(_fox_bias_selector(), BF16)
    return pl.pallas_call(
        _foxprep_kernel,
        out_shape=(qk, qk, jax.ShapeDtypeStruct((bsz, FOX_WIDTH, t_len), BF16)),
        grid=(bsz, t_len // tm),
        in_specs=[pl.BlockSpec((1, tm, 3 * FOX_WIDTH), lambda b, i: (b, i, Z_FOX // (3 * FOX_WIDTH))),
                  pl.BlockSpec((1, tm, LANES), lambda b, i: (b, i, Z_F // LANES)),
                  pl.BlockSpec((1, LANES), lambda b, i: (0, 0)),
                  pl.BlockSpec((1, LANES), lambda b, i: (0, 0)),
                  pl.BlockSpec((1, LANES), lambda b, i: (0, 0)),
                  pl.BlockSpec(sel.shape, lambda b, i: (0, 0))],
        out_specs=(qk_spec, qk_spec, pl.BlockSpec((1, FOX_WIDTH, tm), lambda b, i: (b, 0, i))),
        scratch_shapes=[pltpu.VMEM((1, LANES), F32)],
        compiler_params=_cparams(("parallel", "arbitrary")),
        name="foxprep",
    )(z, z, f_bias_pad, jnp.tile(q_gain.reshape(1, -1), (1, 2)), jnp.tile(k_gain.reshape(1, -1), (1, 2)), sel)


def _fox_kernel(q_ref, k_ref, vt_ref, o_ref, m_ref, l_ref, acc_ref):
    i = pl.program_id(1)
    j = pl.program_id(2)
    tq = q_ref.shape[2]
    tk = k_ref.shape[2]
    sub = 8

    @pl.when(j == 0)
    def _():
        m_ref[...] = jnp.full_like(m_ref, MASK_VALUE)
        l_ref[...] = jnp.zeros_like(l_ref)
        acc_ref[...] = jnp.zeros_like(acc_ref)

    def scores(h):
        return lax.dot_general(k_ref[0, h], q_ref[0, h], (((1,), (1,)), ((), ())), preferred_element_type=F32)

    def update(diagonal):
        if diagonal:
            key = lax.broadcasted_iota(I32, (tk, tq), 0)
            qry = lax.broadcasted_iota(I32, (tk, tq), 1)
            keep = key <= qry
        s_next = scores(0)
        for h in range(FOX_HEADS):
            s = s_next
            if h + 1 < FOX_HEADS:
                s_next = scores(h + 1)
            if diagonal:
                s = jnp.where(keep, s, MASK_VALUE)
            s3 = s.reshape(tk // sub, sub, tq)
            m_prev = m_ref[h]
            m_cur = jnp.max(jnp.max(s3, axis=0), axis=0, keepdims=True)
            m_new = jnp.maximum(m_prev, m_cur)
            alpha = jnp.exp2(m_prev - m_new)
            p3 = jnp.exp2(s3 - m_new[None])
            l_ref[h] = alpha * l_ref[h] + jnp.sum(p3, axis=0)
            pv = jnp.dot(vt_ref[0, h * FOX_HEAD_DIM:(h + 1) * FOX_HEAD_DIM, :], p3.reshape(tk, tq).astype(BF16),
                         preferred_element_type=F32)
            acc = acc_ref[h].reshape(FOX_HEAD_DIM // sub, sub, tq) * alpha[None]
            acc_ref[h] = acc.reshape(FOX_HEAD_DIM, tq) + pv
            m_ref[h] = m_new

    @pl.when(j < i)
    def _():
        update(False)

    @pl.when(j == i)
    def _():
        update(True)
        outs = []
        for h in range(FOX_HEADS):
            denom = jnp.sum(l_ref[h], axis=0, keepdims=True)
            outs.append((acc_ref[h] / denom).astype(BF16))
        out_t = jnp.concatenate(outs, axis=0)
        ri = lax.broadcasted_iota(I32, (tq, tq), 0)
        ci = lax.broadcasted_iota(I32, (tq, tq), 1)
        eye = jnp.where(ri == ci, 1.0, 0.0).astype(BF16)
        o_ref[0] = lax.dot_general(eye, out_t, (((1,), (1,)), ((), ())),
                                   preferred_element_type=F32).astype(o_ref.dtype)


def _fox(q, k, vt):
    bsz, _, t_len, _ = q.shape
    tq = min(512, t_len)
    n_blk = t_len // tq
    return pl.pallas_call(
        _fox_kernel,
        out_shape=jax.ShapeDtypeStruct((bsz, t_len, FOX_WIDTH), BF16),
        grid=(bsz, n_blk, n_blk),
        in_specs=[pl.BlockSpec((1, FOX_HEADS, tq, LANES), lambda b, i, j: (b, 0, i, 0)),
                  pl.BlockSpec((1, FOX_HEADS, tq, LANES), lambda b, i, j: (b, 0, jnp.minimum(j, i), 0)),
                  pl.BlockSpec((1, FOX_WIDTH, tq), lambda b, i, j: (b, 0, jnp.minimum(j, i)))],
        out_specs=pl.BlockSpec((1, tq, FOX_WIDTH), lambda b, i, j: (b, i, 0)),
        scratch_shapes=[pltpu.VMEM((FOX_HEADS, 8, tq), F32), pltpu.VMEM((FOX_HEADS, 8, tq), F32),
                        pltpu.VMEM((FOX_HEADS, FOX_HEAD_DIM, tq), F32)],
        compiler_params=_cparams(("parallel", "parallel", "arbitrary")),
        name="fox",
    )(q, k, vt)


def _merge_kernel(zg_ref, ygm_ref, yrw_ref, yfox_ref, x_ref, g1_ref, sc2_ref, sh2_ref, pb_ref, wo_ref, wr_ref, br_ref,
                  x1_o, h2_o, idx_o, gate_o, rank_o, cnt_o, carry_ref):
    tm = x_ref.shape[1]

    @pl.when((pl.program_id(0) == 0) & (pl.program_id(1) == 0))
    def _():
        carry_ref[...] = jnp.zeros_like(carry_ref)

    sg = jax.nn.sigmoid(zg_ref[0])
    p_gm = jnp.dot(ygm_ref[0], pb_ref[0:GM_WIDTH, :], preferred_element_type=F32)
    y_rw = jnp.concatenate([yrw_ref[0, h] for h in range(RW_HEADS)], axis=-1)
    p_rw = jnp.dot(y_rw, pb_ref[GM_WIDTH:GM_WIDTH + RW_WIDTH, :], preferred_element_type=F32)
    p_fox = jnp.dot(yfox_ref[0], pb_ref[GM_WIDTH + RW_WIDTH:, :], preferred_element_type=F32)
    merged = sg[:, 0:D_MODEL] * p_gm + sg[:, D_MODEL:2 * D_MODEL] * p_rw + sg[:, 2 * D_MODEL:] * p_fox
    x1 = x_ref[0] + g1_ref[0] * jnp.dot(merged.astype(BF16), wo_ref[...], preferred_element_type=F32)
    x1_o[0] = x1
    h2 = x1 * lax.rsqrt(jnp.mean(x1 * x1, axis=-1, keepdims=True) + EPS) * (1.0 + sc2_ref[0]) + sh2_ref[0]
    h2_o[0] = _pack_halves(h2)

    h_hi, h_lo, _ = _split3(h2)
    w_hi, w_lo, _ = _split3(wr_ref[...])
    logits = (jnp.dot(h_hi, w_hi, preferred_element_type=F32) + jnp.dot(h_hi, w_lo, preferred_element_type=F32)
              + jnp.dot(h_lo, w_hi, preferred_element_type=F32)) + br_ref[...]
    lane = lax.broadcasted_iota(I32, (tm, N_EXPERTS), 1)
    vals, idxs = [], []
    rest = logits
    for _ in range(TOP_K):
        m = jnp.max(rest, axis=-1, keepdims=True)
        am = jnp.min(jnp.where(rest == m, lane, N_EXPERTS), axis=-1, keepdims=True)
        vals.append(m)
        idxs.append(am)
        rest = jnp.where(lane == am, -jnp.inf, rest)
    exps = [jnp.exp(val - vals[0]) for val in vals]
    denom = exps[0] + exps[1] + exps[2] + exps[3]

    onehot = jnp.zeros((tm, N_EXPERTS), F32)
    for am in idxs:
        onehot = onehot + jnp.where(lane == am, 1.0, 0.0)
    ri = lax.broadcasted_iota(I32, (tm, tm), 0)
    ci = lax.broadcasted_iota(I32, (tm, tm), 1)
    before = jnp.where(ri > ci, 1.0, 0.0).astype(BF16)
    seen = carry_ref[...] + jnp.dot(before, onehot.astype(BF16), preferred_element_type=F32)
    lane_k = lax.broadcasted_iota(I32, (tm, TOP_K), 1)
    idx_out = jnp.zeros((tm, TOP_K), I32)
    gate_out = jnp.zeros((tm, TOP_K), F32)
    rank_out = jnp.zeros((tm, TOP_K), I32)
    for kk in range(TOP_K):
        rank = jnp.sum(jnp.where(lane == idxs[kk], seen, 0.0), axis=-1, keepdims=True).astype(I32)
        idx_out = jnp.where(lane_k == kk, idxs[kk], idx_out)
        gate_out = jnp.where(lane_k == kk, exps[kk] / denom, gate_out)
        rank_out = jnp.where(lane_k == kk, rank, rank_out)
    idx_o[0] = idx_out
    gate_o[0] = gate_out
    rank_o[0] = rank_out
    total = carry_ref[...] + jnp.sum(onehot, axis=0, keepdims=True)
    carry_ref[...] = total
    cnt_o[...] = total.astype(I32)


def _merge(z, y_gm, y_rw, y_fox, x, gate1, scale2, shift2, w_branch, w_o, w_router, b_router, layer):
    bsz, t_len, d = x.shape
    tm = min(512, t_len)
    row = lambda w: pl.BlockSpec((1, tm, w), lambda b, i: (b, i, 0))
    mod = pl.BlockSpec((1, 1, d), lambda b, i: (b, 0, 0))
    full = lambda shape: pl.BlockSpec(shape, lambda b, i: (0,) * len(shape))
    return pl.pallas_call(
        _merge_kernel,
        out_shape=(jax.ShapeDtypeStruct((bsz, t_len, d), F32), jax.ShapeDtypeStruct((bsz, t_len, d // 2), I32),
                   jax.ShapeDtypeStruct((bsz, t_len, TOP_K), I32), jax.ShapeDtypeStruct((bsz, t_len, TOP_K), F32),
                   jax.ShapeDtypeStruct((bsz, t_len, TOP_K), I32), jax.ShapeDtypeStruct((1, N_EXPERTS), I32)),
        grid=(bsz, t_len // tm),
        in_specs=[row(N_BRANCH * D_MODEL), row(GM_WIDTH),
                  pl.BlockSpec((1, RW_HEADS, tm, RW_HEAD_DIM), lambda b, i: (b, 0, i, 0)),
                  row(FOX_WIDTH), row(d), mod, mod, mod,
                  pl.BlockSpec((MIX_WIDTH, d), lambda b, i: (layer, 0)), pl.BlockSpec((d, d), lambda b, i: (layer, 0)),
                  full(w_router.shape), full((1, N_EXPERTS))],
        out_specs=(row(d), row(d // 2), row(TOP_K), row(TOP_K), row(TOP_K), full((1, N_EXPERTS))),
        scratch_shapes=[pltpu.VMEM((1, N_EXPERTS), F32)],
        compiler_params=_cparams(("arbitrary", "arbitrary")),
        name="merge_router",
    )(z, y_gm, y_rw, y_fox, x, gate1, scale2, shift2, w_branch, w_o, w_router, b_router.reshape(1, N_EXPERTS))


def _sc_mesh():
    return plsc.VectorSubcoreMesh(core_axis_name="c", subcore_axis_name="s",
                                  num_cores=SC_CORES, num_subcores=SC_SUBCORES)


def _sc_worker():
    return lax.axis_index("s") * SC_CORES + lax.axis_index("c")


def _sc_scatter_rows(src, idx3, n_out):
    _, d = src.shape
    n_copy, n_grp, _ = idx3.shape
    grp_per_w = n_grp // SC_WORKERS

    def body(src_hbm, idx_hbm, out_hbm, idx_v, rows_v):
        g0 = _sc_worker() * grp_per_w
        for q in range(n_copy):
            pltpu.sync_copy(idx_hbm.at[q, pl.ds(g0, grp_per_w)], idx_v.at[pl.ds(q * grp_per_w, grp_per_w)])

        @pl.loop(0, grp_per_w)
        def _(j):
            pltpu.sync_copy(src_hbm.at[pl.ds((g0 + j) * SC_ROWS, SC_ROWS)], rows_v)
            for q in range(n_copy):
                pltpu.sync_copy(rows_v, out_hbm.at[idx_v.at[q * grp_per_w + j]])

    return pl.kernel(
        body, out_type=jax.ShapeDtypeStruct((n_out, d), src.dtype), mesh=_sc_mesh(),
        scratch_types=[pltpu.VMEM((n_copy * grp_per_w, SC_ROWS), I32), pltpu.VMEM((SC_ROWS, d), src.dtype)],
        name="sc_dispatch",
    )(src, idx3)


def _sc_gather_rows(table, idx2):
    _, d = table.shape
    n_grp, _ = idx2.shape
    grp_per_w = n_grp // SC_WORKERS

    def body(table_hbm, idx_hbm, out_hbm, idx_v, rows_v):
        g0 = _sc_worker() * grp_per_w
        pltpu.sync_copy(idx_hbm.at[pl.ds(g0, grp_per_w)], idx_v)

        @pl.loop(0, grp_per_w)
        def _(j):
            pltpu.sync_copy(table_hbm.at[idx_v.at[j]], rows_v)
            pltpu.sync_copy(rows_v, out_hbm.at[pl.ds((g0 + j) * SC_ROWS, SC_ROWS)])

    return pl.kernel(
        body, out_type=jax.ShapeDtypeStruct((n_grp * SC_ROWS, d), table.dtype), mesh=_sc_mesh(),
        scratch_types=[pltpu.VMEM((grp_per_w, SC_ROWS), I32), pltpu.VMEM((SC_ROWS, d), table.dtype)],
        name="sc_combine_gather",
    )(table, idx2)


def _ffn_kernel(be_ref, nv_ref, x_ref, wgu_ref, bgu_ref, wd_ref, bd_ref, o_ref, wgu_b, wd_b):
    i = pl.program_id(0)
    n_valid = nv_ref[i]

    @pl.when((i == 0) | (be_ref[i] != be_ref[jnp.maximum(i - 1, 0)]))
    def _():
        wgu_b[...] = wgu_ref[0].astype(BF16)
        wd_b[...] = wd_ref[0].astype(BF16)

    @pl.when(n_valid > 0)
    def _():
        rowid = lax.broadcasted_iota(I32, x_ref.shape, 0)
        xp = jnp.where(rowid < n_valid, x_ref[...], 0)
        x = jnp.concatenate(_unpack_halves(xp), axis=-1).astype(BF16)
        gu = jnp.dot(x, wgu_b[...], preferred_element_type=F32) + bgu_ref[0]
        g_ = jnp.minimum(gu[:, :D_FF], SWIGLU_LIMIT)
        u_ = jnp.clip(gu[:, D_FF:], -SWIGLU_LIMIT, SWIGLU_LIMIT)
        act = (u_ + 1.0) * (g_ * jax.nn.sigmoid(SWIGLU_ALPHA * g_))
        o_ref[...] = _pack_halves(jnp.dot(act.astype(BF16), wd_b[...], preferred_element_type=F32) + bd_ref[0])

    @pl.when(n_valid <= 0)
    def _():
        o_ref[...] = jnp.zeros_like(o_ref)


def _ffn(block_expert, block_valid, xin, w_gate_up, b_gate_up, w_down, b_down):
    n_rows, dp = xin.shape
    d = 2 * dp
    n_blocks = n_rows // MOE_BLOCK
    grid_spec = pltpu.PrefetchScalarGridSpec(
        num_scalar_prefetch=2,
        grid=(n_blocks,),
        in_specs=[pl.BlockSpec((MOE_BLOCK, dp), lambda i, be, nv: (i, 0)),
                  pl.BlockSpec((1, d, 2 * D_FF), lambda i, be, nv: (be[i], 0, 0)),
                  pl.BlockSpec((1, 1, 2 * D_FF), lambda i, be, nv: (be[i], 0, 0)),
                  pl.BlockSpec((1, D_FF, d), lambda i, be, nv: (be[i], 0, 0)),
                  pl.BlockSpec((1, 1, d), lambda i, be, nv: (be[i], 0, 0))],
        out_specs=pl.BlockSpec((MOE_BLOCK, dp), lambda i, be, nv: (i, 0)),
        scratch_shapes=[pltpu.VMEM((d, 2 * D_FF), BF16), pltpu.VMEM((D_FF, d), BF16)],
    )
    return pl.pallas_call(
        _ffn_kernel,
        out_shape=jax.ShapeDtypeStruct((n_rows, dp), I32),
        grid_spec=grid_spec,
        compiler_params=pltpu.CompilerParams(dimension_semantics=("arbitrary",), vmem_limit_bytes=FFN_VMEM_LIMIT),
        name="expert_ffn",
    )(block_expert, block_valid, xin, w_gate_up, b_gate_up, w_down, b_down)


def _combine_kernel(x1_ref, g2_ref, gate_ref, yg_ref, o_ref):
    gate = gate_ref[0]
    y_lo = y_hi = None
    for q in range(TOP_K):
        lo, hi = _unpack_halves(yg_ref[q, 0])
        wq = gate[:, q:q + 1]
        y_lo = wq * lo if y_lo is None else y_lo + wq * lo
        y_hi = wq * hi if y_hi is None else y_hi + wq * hi
    o_ref[0] = x1_ref[0] + g2_ref[0] * jnp.concatenate([y_lo, y_hi], axis=-1)


def _combine(x1, gate2, gate, yg):
    bsz, t_len, d = x1.shape
    tm = min(512, t_len)
    return pl.pallas_call(
        _combine_kernel,
        out_shape=jax.ShapeDtypeStruct((bsz, t_len, d), F32),
        grid=(bsz, t_len // tm),
        in_specs=[pl.BlockSpec((1, tm, d), lambda b, i: (b, i, 0)),
                  pl.BlockSpec((1, 1, d), lambda b, i: (b, 0, 0)),
                  pl.BlockSpec((1, tm, TOP_K), lambda b, i: (b, i, 0)),
                  pl.BlockSpec((TOP_K, 1, tm, d // 2), lambda b, i: (0, b, i, 0))],
        out_specs=pl.BlockSpec((1, tm, d), lambda b, i: (b, i, 0)),
        compiler_params=_cparams(("parallel", "parallel")),
        name="moe_combine",
    )(x1, gate2, gate, yg)


def _moe(x1, gate2, h2, top_idx, gate, rank, counts, w_gate_up, b_gate_up, w_down, b_down, layer):
    bsz, t_len, d = h2.shape
    n_tok = bsz * t_len
    n_assign = n_tok * TOP_K
    n_blocks = -(-n_assign // MOE_BLOCK) + N_EXPERTS
    counts = counts.reshape(N_EXPERTS)
    blocks_e = (counts + MOE_BLOCK - 1) // MOE_BLOCK
    blk_end = jnp.cumsum(blocks_e)
    blk_start = blk_end - blocks_e
    experts = jnp.arange(N_EXPERTS, dtype=I32)
    onehot = top_idx.reshape(n_tok, TOP_K, 1) == experts
    dest = jnp.sum(jnp.where(onehot, blk_start * MOE_BLOCK, 0), axis=-1) + rank.reshape(n_tok, TOP_K)
    dest_t = dest.T.astype(I32)
    blk = jnp.arange(n_blocks, dtype=I32)
    block_expert = jnp.minimum(jnp.sum(blk_end[None, :] <= blk[:, None], axis=1), N_EXPERTS - 1).astype(I32)
    be_hot = block_expert[:, None] == experts
    cnt_b = jnp.sum(jnp.where(be_hot, counts, 0), axis=1)
    start_b = jnp.sum(jnp.where(be_hot, blk_start, 0), axis=1)
    block_valid = jnp.clip(cnt_b - (blk - start_b) * MOE_BLOCK, 0, MOE_BLOCK).astype(I32)
    xin = _sc_scatter_rows(h2.reshape(n_tok, d), dest_t.reshape(TOP_K, n_tok // SC_ROWS, SC_ROWS),
                           n_blocks * MOE_BLOCK)
    yb = _ffn(block_expert + layer * N_EXPERTS, block_valid, xin, w_gate_up, b_gate_up, w_down, b_down)
    yg = _sc_gather_rows(yb, dest_t.reshape(n_assign // SC_ROWS, SC_ROWS))
    return _combine(x1, gate2, gate, yg.reshape(TOP_K, bsz, t_len, d))


def _permute_kernel(w_ref, o_ref):
    o_gm = 0
    o_rw = o_gm + 2 * GM_WIDTH
    o_fox = o_rw + RW_SHIFT_WIDTH
    o_f = o_fox + 3 * FOX_WIDTH
    o_gate = o_f + FOX_HEADS
    w = w_ref[0]
    o_ref[0, :, Z_GATE:Z_FOX] = w[:, o_gate:o_gate + N_BRANCH * D_MODEL].astype(BF16)
    o_ref[0, :, Z_FOX:Z_GM] = w[:, o_fox:o_f].astype(BF16)
    o_ref[0, :, Z_GM:Z_RW] = w[:, o_gm:o_rw].astype(BF16)
    o_ref[0, :, Z_RW:Z_F] = w[:, o_rw:o_fox].astype(BF16)
    tail = jnp.concatenate([w[:, o_f:o_gate], jnp.zeros((w.shape[0], Z_WIDTH - Z_F - FOX_HEADS), F32)], axis=-1)
    o_ref[0, :, Z_F:Z_WIDTH] = tail.astype(BF16)


def _permute_w_in(w_in):
    n_layer, d, w_cols = w_in.shape
    tr = 256
    return pl.pallas_call(
        _permute_kernel,
        out_shape=jax.ShapeDtypeStruct((n_layer, d, Z_WIDTH), BF16),
        grid=(n_layer, d // tr),
        in_specs=[pl.BlockSpec((1, tr, w_cols), lambda l, i: (l, i, 0))],
        out_specs=pl.BlockSpec((1, tr, Z_WIDTH), lambda l, i: (l, i, 0)),
        compiler_params=_cparams(("parallel", "parallel")),
        name="permute_w_in",
    )(w_in)


def _layer(x, mod, w_in_p, gm_v_gain, gm_w_s, gm_b_s, mu_pad, w_lora, rw_w0, rw_a0, rw_k_k, rw_k_a, rw_r_k,
           rw_gn_gain, rw_gn_bias, f_bias_pad, fox_q_gain, fox_k_gain, w_branch, w_o, w_router, b_router,
           w_gate_up, b_gate_up, w_down, b_down, layer):
    shift1, scale1, gate1, shift2, scale2, gate2 = (mod[:, i][:, None, :] for i in range(6))
    z = _inproj(x, scale1, shift1, w_in_p, layer)
    y_gm = _gmlp(z, gm_v_gain, gm_w_s, gm_b_s)
    r, lw, k, v, a, b, g = _rwprep(z, mu_pad, w_lora, rw_w0, rw_a0, rw_k_k, rw_k_a)
    y_rw = _rwscan(r, lw, k, v, a, b, g, rw_r_k, rw_gn_gain, rw_gn_bias)
    q, kf, vf = _foxprep(z, f_bias_pad, fox_q_gain, fox_k_gain)
    y_fox = _fox(q, kf, vf)
    x1, h2, top_idx, gate, rank, counts = _merge(z, y_gm, y_rw, y_fox, x, gate1, scale2, shift2,
                                                 w_branch, w_o, w_router, b_router, layer)
    return _moe(x1, gate2, h2, top_idx, gate, rank, counts, w_gate_up, b_gate_up, w_down, b_down, layer)


def kernel(x, c, w_ada, b_ada, w_in, gm_v_gain, gm_w_s, gm_b_s, rw_mu, rw_w0, rw_w2, rw_a0, rw_a2, rw_g2, rw_k_k,
           rw_k_a, rw_r_k, rw_gn_gain, rw_gn_bias, fox_f_bias, fox_q_gain, fox_k_gain, w_branch, w_o, w_router,
           b_router, w_gate_up, b_gate_up, w_down, b_down):
    n_layer = w_ada.shape[0]
    bsz = x.shape[0]
    c_pad = jnp.zeros((8, D_MODEL), F32).at[:bsz].set(c)
    mod = _adaln(c_pad, w_ada, b_ada)[:, :bsz].reshape(n_layer, bsz, 6, D_MODEL)
    w_in_p = _permute_w_in(w_in)
    mu_pad = jnp.pad(rw_mu, ((0, 0), (0, RW_BLOCK - RW_SHIFT_WIDTH)))
    w_lora = jnp.zeros((n_layer, RW_LORA, 3 * RW_WIDTH), F32)
    w_lora = w_lora.at[:, 0:RW_DECAY_LORA, 0:RW_WIDTH].set(rw_w2)
    w_lora = w_lora.at[:, RW_DECAY_LORA:RW_DECAY_LORA + RW_ICLR_LORA, RW_WIDTH:2 * RW_WIDTH].set(rw_a2)
    w_lora = w_lora.at[:, RW_DECAY_LORA + RW_ICLR_LORA:, 2 * RW_WIDTH:].set(rw_g2)
    f_bias_pad = jnp.pad(fox_f_bias, ((0, 0), (0, LANES - FOX_HEADS)))
    w_in_p = w_in_p.reshape(n_layer * D_MODEL, Z_WIDTH)
    w_branch_b = w_branch.astype(BF16).reshape(n_layer * MIX_WIDTH, D_MODEL)
    w_o_b = w_o.astype(BF16).reshape(n_layer * D_MODEL, D_MODEL)
    w_gu = w_gate_up.reshape(n_layer * N_EXPERTS, D_MODEL, 2 * D_FF)
    b_gu = b_gate_up.reshape(n_layer * N_EXPERTS, 1, 2 * D_FF)
    w_dn = w_down.reshape(n_layer * N_EXPERTS, D_FF, D_MODEL)
    b_dn = b_down.reshape(n_layer * N_EXPERTS, 1, D_MODEL)
    for l in range(n_layer):
        x = _layer(x, mod[l], w_in_p, gm_v_gain[l], gm_w_s[l], gm_b_s[l], mu_pad[l:l + 1], w_lora[l], rw_w0[l],
                   rw_a0[l], rw_k_k[l], rw_k_a[l], rw_r_k[l], rw_gn_gain[l], rw_gn_bias[l], f_bias_pad[l:l + 1],
                   fox_q_gain[l], fox_k_gain[l], w_branch_b, w_o_b, w_router[l], b_router[l],
                   w_gu, b_gu, w_dn, b_dn, l)
    return x
```

```python
import functools

import jax
import jax.numpy as jnp
import numpy as np
from jax import lax
from jax.experimental import pallas as pl
from jax.experimental.pallas import tpu as pltpu
from jax.experimental.pallas import tpu_sc as plsc

F32 = jnp.float32
BF16 = jnp.bfloat16
I32 = jnp.int32
HIGHEST = lax.Precision.HIGHEST

D_MODEL = 1024
GM_CHUNK = 128
GM_GROUPS = 4
GM_WIDTH = 256
GM_GROUP_DIM = GM_WIDTH // GM_GROUPS
RW_HEADS = 4
RW_HEAD_DIM = 64
RW_WIDTH = RW_HEADS * RW_HEAD_DIM
RW_DECAY_LORA = 32
RW_ICLR_LORA = 32
RW_GATE_LORA = 64
RW_LORA = RW_DECAY_LORA + RW_ICLR_LORA + RW_GATE_LORA
RW_SHIFT_WIDTH = 3 * RW_WIDTH + RW_LORA
RW_GN_EPS = 64e-5
FOX_HEADS = 8
FOX_HEAD_DIM = 64
FOX_WIDTH = FOX_HEADS * FOX_HEAD_DIM
ATTN_SCALE = FOX_HEAD_DIM ** -0.5
MASK_VALUE = -1e30
LOG2E = 1.4426950408889634
N_BRANCH = 3
MIX_WIDTH = GM_WIDTH + RW_WIDTH + FOX_WIDTH
N_EXPERTS = 32
TOP_K = 4
D_FF = D_MODEL
SWIGLU_LIMIT = 7.0
SWIGLU_ALPHA = 1.702
MOE_BLOCK = 256
EPS = 1e-6

Z_GATE = 0
Z_FOX = N_BRANCH * D_MODEL
Z_GM = Z_FOX + 3 * FOX_WIDTH
Z_RW = Z_GM + 2 * GM_WIDTH
RW_BLOCK = 1024
Z_F = Z_RW + RW_SHIFT_WIDTH
Z_WIDTH = Z_RW + RW_BLOCK
LANES = 128
RW_CHUNK = 64
RW_PREP_UNROLL = 4

VMEM_LIMIT = 48 * 1024 * 1024
FFN_VMEM_LIMIT = 56 * 1024 * 1024
SC_CORES = 2
SC_SUBCORES = 16
SC_WORKERS = SC_CORES * SC_SUBCORES
SC_ROWS = 64


def _cparams(sem):
    return pltpu.CompilerParams(dimension_semantics=sem, vmem_limit_bytes=VMEM_LIMIT)


def _mm(a, b):
    return jnp.dot(a.astype(BF16), b.astype(BF16), preferred_element_type=F32)


def _mm_nt(a, b):
    return lax.dot_general(a.astype(BF16), b.astype(BF16), (((1,), (1,)), ((), ())), preferred_element_type=F32)


def _mm_tn(a, b):
    return lax.dot_general(a.astype(BF16), b.astype(BF16), (((0,), (0,)), ((), ())), preferred_element_type=F32)


def _split3(x):
    hi = x.astype(BF16)
    r1 = x - hi.astype(F32)
    mid = r1.astype(BF16)
    lo = (r1 - mid.astype(F32)).astype(BF16)
    return hi, mid, lo


def _tri_cumsum(x, n):
    ri = lax.broadcasted_iota(I32, (n, n), 0)
    ci = lax.broadcasted_iota(I32, (n, n), 1)
    ones = jnp.where(ri >= ci, 1.0, 0.0).astype(BF16)
    hi, mid, lo = _split3(x)
    return (jnp.dot(ones, hi, preferred_element_type=F32) + jnp.dot(ones, mid, preferred_element_type=F32)
            + jnp.dot(ones, lo, preferred_element_type=F32))


def _pack_halves(x):
    w = x.shape[1] // 2
    hi = pltpu.bitcast(x[:, :w].astype(BF16).astype(F32), jnp.uint32)
    lo = pltpu.bitcast(x[:, w:].astype(BF16).astype(F32), jnp.uint32)
    return pltpu.bitcast(hi | (lo >> 16), I32)


def _unpack_halves(p):
    u = pltpu.bitcast(p, jnp.uint32)
    return pltpu.bitcast(u & jnp.uint32(0xFFFF0000), F32), pltpu.bitcast(u << 16, F32)


def _log_sigmoid(x):
    return jnp.minimum(x, 0.0) - jnp.log1p(jnp.exp(-jnp.abs(x)))


def _adaln_kernel(c_ref, w_ref, b_ref, o_ref):
    c = c_ref[...]
    s = c * jax.nn.sigmoid(c)
    o_ref[0] = jnp.dot(s, w_ref[0], preferred_element_type=F32, precision=HIGHEST) + b_ref[0]


def _adaln(c_pad, w_ada, b_ada):
    n_layer, d, w6 = w_ada.shape
    tn = 1536
    return pl.pallas_call(
        _adaln_kernel,
        out_shape=jax.ShapeDtypeStruct((n_layer, c_pad.shape[0], w6), F32),
        grid=(n_layer, w6 // tn),
        in_specs=[pl.BlockSpec(c_pad.shape, lambda l, j: (0, 0)),
                  pl.BlockSpec((1, d, tn), lambda l, j: (l, 0, j)),
                  pl.BlockSpec((1, 1, tn), lambda l, j: (l, 0, j))],
        out_specs=pl.BlockSpec((1, c_pad.shape[0], tn), lambda l, j: (l, 0, j)),
        compiler_params=_cparams(("parallel", "parallel")),
        name="adaln",
    )(c_pad, w_ada, b_ada.reshape(n_layer, 1, w6))


def _inproj_kernel(x_ref, sc_ref, sh_ref, w_ref, zm_ref, zr_ref, xn_ref):
    j = pl.program_id(2)

    @pl.when(j == 0)
    def _():
        x = x_ref[0]
        xn = x * lax.rsqrt(jnp.mean(x * x, axis=-1, keepdims=True) + EPS)
        xn_ref[...] = (xn * (1.0 + sc_ref[0]) + sh_ref[0]).astype(BF16)

    acc = jnp.dot(xn_ref[...], w_ref[...], preferred_element_type=F32)

    @pl.when(j < Z_RW // RW_BLOCK)
    def _():
        zm_ref[0] = acc.astype(BF16)

    @pl.when(j == Z_RW // RW_BLOCK)
    def _():
        zr_ref[0] = acc


def _inproj(x, scale, shift, w, layer):
    bsz, t_len, d = x.shape
    tm = min(1024, t_len)
    tn = RW_BLOCK
    n_main = Z_RW // tn
    return pl.pallas_call(
        _inproj_kernel,
        out_shape=(jax.ShapeDtypeStruct((bsz, t_len, Z_RW), BF16), jax.ShapeDtypeStruct((bsz, t_len, RW_BLOCK), F32)),
        grid=(bsz, t_len // tm, Z_WIDTH // tn),
        in_specs=[pl.BlockSpec((1, tm, d), lambda b, i, j: (b, i, 0)),
                  pl.BlockSpec((1, 1, d), lambda b, i, j: (b, 0, 0)),
                  pl.BlockSpec((1, 1, d), lambda b, i, j: (b, 0, 0)),
                  pl.BlockSpec((d, tn), lambda b, i, j: (layer, j))],
        out_specs=(pl.BlockSpec((1, tm, tn), lambda b, i, j: (b, i, jnp.minimum(j, n_main - 1))),
                   pl.BlockSpec((1, tm, tn), lambda b, i, j: (b, i, 0))),
        scratch_shapes=[pltpu.VMEM((tm, d), BF16)],
        compiler_params=_cparams(("parallel", "parallel", "arbitrary")),
        name="inproj",
    )(x, scale, shift, w)


def _gmlp_kernel(z_ref, gain_ref, ws_ref, bst_ref, o_ref):
    tm = z_ref.shape[1]
    z = z_ref[0].astype(F32)
    u = jax.nn.gelu(z[:, :GM_WIDTH])
    v = jax.nn.gelu(z[:, GM_WIDTH:])
    v = v * lax.rsqrt(jnp.mean(v * v, axis=-1, keepdims=True) + EPS) * gain_ref[...]
    vb = v.astype(BF16)
    grp = lax.broadcasted_iota(I32, (GM_CHUNK, GM_WIDTH), 1) // GM_GROUP_DIM
    ri = lax.broadcasted_iota(I32, (GM_CHUNK, GM_CHUNK), 0)
    ci = lax.broadcasted_iota(I32, (GM_CHUNK, GM_CHUNK), 1)
    causal = ri >= ci
    bias = jnp.zeros((GM_CHUNK, GM_WIDTH), F32)
    ws = []
    for g in range(GM_GROUPS):
        ws.append(jnp.where(causal, ws_ref[g], 0.0).astype(BF16))
        bias = jnp.where(grp == g, bst_ref[:, g:g + 1], bias)
    for c in range(tm // GM_CHUNK):
        rows = slice(c * GM_CHUNK, (c + 1) * GM_CHUNK)
        vc = vb[rows]
        mixed = bias
        for g in range(GM_GROUPS):
            m = jnp.dot(ws[g], vc, preferred_element_type=F32)
            mixed = mixed + jnp.where(grp == g, m, 0.0)
        o_ref[0, rows, :] = (u[rows] * mixed).astype(o_ref.dtype)


def _gmlp(z, gain, w_s, b_s):
    bsz, t_len, _ = z.shape
    tm = min(512, t_len)
    return pl.pallas_call(
        _gmlp_kernel,
        out_shape=jax.ShapeDtypeStruct((bsz, t_len, GM_WIDTH), BF16),
        grid=(bsz, t_len // tm),
        in_specs=[pl.BlockSpec((1, tm, 2 * GM_WIDTH), lambda b, i: (b, i, Z_GM // (2 * GM_WIDTH))),
                  pl.BlockSpec((1, GM_WIDTH), lambda b, i: (0, 0)),
                  pl.BlockSpec((GM_GROUPS, GM_CHUNK, GM_CHUNK), lambda b, i: (0, 0, 0)),
                  pl.BlockSpec((GM_CHUNK, GM_GROUPS), lambda b, i: (0, 0))],
        out_specs=pl.BlockSpec((1, tm, GM_WIDTH), lambda b, i: (b, i, 0)),
        compiler_params=_cparams(("parallel", "parallel")),
        name="gmlp",
    )(z, gain.reshape(1, GM_WIDTH), w_s, b_s.T)


def _rwprep_kernel(z_ref, zp_ref, mu_ref, wl_ref, w0_ref, a0_ref, kk_ref, ka_ref,
                   r_o, lw_o, k_o, v_o, a_o, b_o, g_o):
    tm = z_ref.shape[1]
    z = z_ref[0]
    prev = jnp.where(pl.program_id(1) > 0, zp_ref[0, 7:8, :], 0.0)
    rowid = lax.broadcasted_iota(I32, z.shape, 0)
    zs = jnp.where(rowid == 0, prev, pltpu.roll(z, 1, axis=0))
    zz = z + mu_ref[...] * (zs - z)
    r = zz[:, 0:RW_WIDTH]
    k = zz[:, RW_WIDTH:2 * RW_WIDTH]
    v = zz[:, 2 * RW_WIDTH:3 * RW_WIDTH]
    lo = zz[:, 3 * RW_WIDTH:3 * RW_WIDTH + RW_LORA]
    lane = lax.broadcasted_iota(I32, (tm, RW_LORA), 1)
    act = jnp.where(lane < RW_DECAY_LORA, jnp.tanh(lo),
                    jnp.where(lane < RW_DECAY_LORA + RW_ICLR_LORA, lo, jax.nn.sigmoid(lo)))
    proj = jnp.dot(act, wl_ref[...], preferred_element_type=F32, precision=HIGHEST)
    xw = -(w0_ref[...] + proj[:, 0:RW_WIDTH])
    softplus = jnp.maximum(xw, 0.0) + jnp.log1p(jnp.exp(-jnp.abs(xw)))
    lw = -jnp.exp(-softplus - 0.5)
    a = jax.nn.sigmoid(a0_ref[...] + proj[:, RW_WIDTH:2 * RW_WIDTH])
    g = proj[:, 2 * RW_WIDTH:3 * RW_WIDTH]
    kk = k * kk_ref[...]
    k2 = k * (1.0 + (a - 1.0) * ka_ref[...])
    for h in range(RW_HEADS):
        sl = slice(h * RW_HEAD_DIM, (h + 1) * RW_HEAD_DIM)
        kkh = kk[:, sl]
        nrm = jnp.sqrt(jnp.sum(kkh * kkh, axis=-1, keepdims=True))
        kkh = kkh / jnp.maximum(nrm, 1e-12)
        r_o[0, h] = r[:, sl]
        lw_o[0, h] = lw[:, sl]
        k_o[0, h] = k2[:, sl]
        v_o[0, h] = v[:, sl]
        a_o[0, h] = -kkh
        b_o[0, h] = kkh * a[:, sl]
        g_o[0, h] = g[:, sl]


def _rwprep(z, mu_pad, w_lora, w0, a0, k_k, k_a):
    bsz, t_len, _ = z.shape
    tm = min(512, t_len)
    hm = jax.ShapeDtypeStruct((bsz, RW_HEADS, t_len, RW_HEAD_DIM), F32)
    hm_spec = pl.BlockSpec((1, RW_HEADS, tm, RW_HEAD_DIM), lambda b, i: (b, 0, i, 0))
    vec = lambda n: pl.BlockSpec((1, n), lambda b, i: (0, 0))
    rw_blk = 0
    return pl.pallas_call(
        _rwprep_kernel,
        out_shape=(hm,) * 7,
        grid=(bsz, t_len // tm),
        in_specs=[pl.BlockSpec((1, tm, RW_BLOCK), lambda b, i: (b, i, rw_blk)),
                  pl.BlockSpec((1, 8, RW_BLOCK), lambda b, i: (b, jnp.maximum(i * (tm // 8) - 1, 0), rw_blk)),
                  vec(RW_BLOCK),
                  pl.BlockSpec((RW_LORA, 3 * RW_WIDTH), lambda b, i: (0, 0)),
                  vec(RW_WIDTH), vec(RW_WIDTH), vec(RW_WIDTH), vec(RW_WIDTH)],
        out_specs=(hm_spec,) * 7,
        compiler_params=_cparams(("parallel", "parallel")),
        name="rwprep",
    )(z, z, mu_pad, w_lora, w0.reshape(1, -1), a0.reshape(1, -1), k_k.reshape(1, -1), k_a.reshape(1, -1))


def _rwscan_kernel(r_ref, lw_ref, k_ref, v_ref, a_ref, b_ref, g_ref, rk_ref, gg_ref, gb_ref, o_ref,
                   s_ref, rp_ref, y_ref, gm_ref, h0_ref, we_ref):
    cl = RW_CHUNK
    tb = r_ref.shape[2]
    n_chunk = tb // cl

    @pl.when(pl.program_id(1) == 0)
    def _():
        s_ref[...] = jnp.zeros_like(s_ref)

    n = RW_HEADS * cl
    ri = lax.broadcasted_iota(I32, (n, n), 0)
    ci = lax.broadcasted_iota(I32, (n, n), 1)
    same_head = (ri // cl) == (ci // cl)
    lower = same_head & (ri >= ci)
    strict = same_head & (ri > ci)
    eye = jnp.where(ri == ci, 1.0, 0.0)
    ones_lower = jnp.where(lower, 1.0, 0.0).astype(BF16)

    def prepare(chunks):
        grp = range(len(chunks))
        each = lambda fn: [fn(u) for u in grp]
        rows = [pl.ds(pl.multiple_of(c * cl, cl), cl) for c in chunks]
        stack = lambda ref: each(lambda u: ref[0, :, rows[u], :].reshape(n, RW_HEAD_DIM))
        r, lw, k, v, a, b = (stack(ref) for ref in (r_ref, lw_ref, k_ref, v_ref, a_ref, b_ref))
        hd = RW_HEAD_DIM
        parts = each(lambda u: jnp.concatenate(_split3(lw[u]), axis=-1))
        sums = each(lambda u: jnp.dot(ones_lower, parts[u], preferred_element_type=F32))
        cw = each(lambda u: sums[u][:, :hd] + sums[u][:, hd:2 * hd] + sums[u][:, 2 * hd:])
        w_in = each(lambda u: jnp.exp(cw[u]))
        w_inv = each(lambda u: jnp.exp(-cw[u]))
        rt = each(lambda u: r[u] * w_in[u])
        at = each(lambda u: a[u] * jnp.exp(cw[u] - lw[u]))
        kt = each(lambda u: k[u] * w_inv[u])
        bt = each(lambda u: b[u] * w_inv[u])
        w_end = each(lambda u: w_in[u].reshape(RW_HEADS, cl, RW_HEAD_DIM)[:, cl - 1:cl, :])
        w_end_rows = each(lambda u: jnp.broadcast_to(w_end[u], (RW_HEADS, cl, RW_HEAD_DIM)).reshape(n, RW_HEAD_DIM))
        a_ab = each(lambda u: jnp.where(strict, _mm_nt(at[u], bt[u]), 0.0))
        a_ak = each(lambda u: jnp.where(strict, _mm_nt(at[u], kt[u]), 0.0))
        m_rb = each(lambda u: jnp.where(lower, _mm_nt(rt[u], bt[u]), 0.0))
        m_rk = each(lambda u: jnp.where(lower, _mm_nt(rt[u], kt[u]), 0.0))
        inv = each(lambda u: eye + a_ab[u])
        p = a_ab
        for _ in range(cl.bit_length() - 2):
            p = [_mm(p[u], p[u]) for u in grp]
            inv = [inv[u] + _mm(inv[u], p[u]) for u in grp]
        akv = each(lambda u: _mm(a_ak[u], v[u]))
        apz = each(lambda u: _mm(inv[u], jnp.concatenate([at[u], akv[u]], axis=-1)).astype(BF16))
        mix = each(lambda u: jnp.dot(m_rb[u].astype(BF16), apz[u], preferred_element_type=F32))
        bend = each(lambda u: bt[u] * w_end_rows[u])
        kend = each(lambda u: kt[u] * w_end_rows[u])
        rp = each(lambda u: (rt[u] + mix[u][:, :hd]).astype(BF16))
        y0 = each(lambda u: mix[u][:, hd:] + _mm(m_rk[u], v[u]))
        for u in grp:
            for h in range(RW_HEADS):
                hs = slice(h * cl, (h + 1) * cl)
                both = _mm_tn(apz[u][hs], bend[u][hs])
                rp_ref[h, rows[u], :] = rp[u][hs]
                y_ref[h, rows[u], :] = y0[u][hs]
                gm_ref[h, rows[u], :] = both[:hd].astype(BF16)
                h0_ref[h, rows[u], :] = both[hd:] + _mm_tn(v[u][hs], kend[u][hs])
                we_ref[h, chunks[u]] = w_end[u][h]

    def prepare_step(i, carry):
        prepare([i * RW_PREP_UNROLL + u for u in range(RW_PREP_UNROLL)])
        return carry

    lax.fori_loop(0, n_chunk // RW_PREP_UNROLL, prepare_step, 0)

    def advance(c, carry):
        rows = pl.ds(pl.multiple_of(c * cl, cl), cl)
        for h in range(RW_HEADS):
            s = s_ref[h]
            sb = s.astype(BF16)
            y_ref[h, rows, :] = y_ref[h, rows, :] + lax.dot_general(
                rp_ref[h, rows, :], sb, (((1,), (1,)), ((), ())), preferred_element_type=F32)
            s_ref[h] = (s * we_ref[h, c] + jnp.dot(sb, gm_ref[h, rows, :], preferred_element_type=F32)
                        + h0_ref[h, rows, :])
        return carry

    lax.fori_loop(0, n_chunk, advance, 0)

    for h in range(RW_HEADS):
        y = y_ref[h]
        mu = jnp.mean(y, axis=-1, keepdims=True)
        yc = y - mu
        var = jnp.mean(yc * yc, axis=-1, keepdims=True)
        yn = yc * lax.rsqrt(var + RW_GN_EPS) * gg_ref[h] + gb_ref[h]
        v = v_ref[0, h]
        bonus = jnp.sum(r_ref[0, h] * k_ref[0, h] * rk_ref[h], axis=-1, keepdims=True) * v
        o_ref[0, h] = ((yn + bonus) * g_ref[0, h]).astype(o_ref.dtype)


def _rwscan(r, lw, k, v, a, b, g, r_k, gn_gain, gn_bias):
    bsz, _, t_len, _ = r.shape
    tb = min(512, t_len)
    hm_spec = pl.BlockSpec((1, RW_HEADS, tb, RW_HEAD_DIM), lambda bi, i: (bi, 0, i, 0))
    par = pl.BlockSpec((RW_HEADS, 1, RW_HEAD_DIM), lambda bi, i: (0, 0, 0))
    hshape = (RW_HEADS, 1, RW_HEAD_DIM)
    return pl.pallas_call(
        _rwscan_kernel,
        out_shape=jax.ShapeDtypeStruct((bsz, RW_HEADS, t_len, RW_HEAD_DIM), BF16),
        grid=(bsz, t_len // tb),
        in_specs=[hm_spec] * 7 + [par] * 3,
        out_specs=hm_spec,
        scratch_shapes=[pltpu.VMEM((RW_HEADS, RW_HEAD_DIM, RW_HEAD_DIM), F32),
                        pltpu.VMEM((RW_HEADS, tb, RW_HEAD_DIM), BF16), pltpu.VMEM((RW_HEADS, tb, RW_HEAD_DIM), F32),
                        pltpu.VMEM((RW_HEADS, tb, RW_HEAD_DIM), BF16), pltpu.VMEM((RW_HEADS, tb, RW_HEAD_DIM), F32),
                        pltpu.VMEM((RW_HEADS, tb // RW_CHUNK, 1, RW_HEAD_DIM), F32)],
        compiler_params=_cparams(("parallel", "arbitrary")),
        name="rwscan",
    )(r, lw, k, v, a, b, g, r_k.reshape(hshape), gn_gain.reshape(hshape), gn_bias.reshape(hshape))


FOX_PAIRS = FOX_HEADS // 2
FOX_EXTRA = 3


def _fox_bias_selector():
    sel = np.zeros((LANES, 2 * FOX_HEADS * LANES), np.float32)
    for h in range(FOX_HEADS):
        base = FOX_HEAD_DIM if h % 2 == 0 else 0
        for p in range(FOX_EXTRA):
            sel[p * FOX_HEADS + h, h * LANES + base + p] = 1.0
            sel[p * FOX_HEADS + h, (FOX_HEADS + h) * LANES + base + FOX_EXTRA + p] = -1.0
    return sel


def _foxprep_kernel(z_ref, f_ref, fb_ref, qg_ref, kg_ref, sel_ref, q_o, k_o, vt_o, carry_ref):
    tm = z_ref.shape[1]

    @pl.when(pl.program_id(1) == 0)
    def _():
        carry_ref[...] = jnp.zeros_like(carry_ref)

    log_f = _log_sigmoid(f_ref[0] + fb_ref[...])
    cum = carry_ref[...] + _tri_cumsum(log_f, tm)
    carry_ref[...] = cum[tm - 1:tm, :]
    lane = lax.broadcasted_iota(I32, (tm, LANES), 1)
    hi, mid, lo = (p.astype(F32) for p in _split3(cum * LOG2E))
    packed = jnp.where(lane < FOX_HEADS, hi,
                       jnp.where(lane < 2 * FOX_HEADS, pltpu.roll(mid, FOX_HEADS, axis=1),
                                 pltpu.roll(lo, 2 * FOX_HEADS, axis=1)))
    packed = jnp.where(lane < FOX_EXTRA * FOX_HEADS, packed, 0.0).astype(BF16)
    extra = jnp.dot(packed, sel_ref[...], preferred_element_type=F32)

    left = lane < FOX_HEAD_DIM
    in_half = lane % FOX_HEAD_DIM
    ones_q = jnp.where((in_half >= FOX_EXTRA) & (in_half < 2 * FOX_EXTRA), 1.0, 0.0)
    ones_k = jnp.where(in_half < FOX_EXTRA, 1.0, 0.0)

    def normed(block, gain):
        sq = block * block
        s_left = jnp.sum(jnp.where(left, sq, 0.0), axis=-1, keepdims=True)
        s_right = jnp.sum(jnp.where(left, 0.0, sq), axis=-1, keepdims=True)
        ms = jnp.where(left, s_left, s_right) * (1.0 / FOX_HEAD_DIM)
        return block * lax.rsqrt(ms + EPS) * gain

    for j in range(FOX_PAIRS):
        qn = normed(z_ref[0, :, j * LANES:(j + 1) * LANES].astype(F32), qg_ref[...] * (ATTN_SCALE * LOG2E))
        kn = normed(z_ref[0, :, FOX_WIDTH + j * LANES:FOX_WIDTH + (j + 1) * LANES].astype(F32), kg_ref[...])
        for par in range(2):
            h = 2 * j + par
            own = left if par == 0 else jnp.logical_not(left)
            q_o[0, h] = jnp.where(own, qn, extra[:, h * LANES:(h + 1) * LANES] + ones_q).astype(BF16)
            k_o[0, h] = jnp.where(own, kn, extra[:, (FOX_HEADS + h) * LANES:(FOX_HEADS + h + 1) * LANES]
                                  + ones_k).astype(BF16)
    ri = lax.broadcasted_iota(I32, (FOX_WIDTH, FOX_WIDTH), 0)
    ci = lax.broadcasted_iota(I32, (FOX_WIDTH, FOX_WIDTH), 1)
    eye = jnp.where(ri == ci, 1.0, 0.0).astype(BF16)
    v = z_ref[0, :, 2 * FOX_WIDTH:3 * FOX_WIDTH].astype(BF16)
    vt_o[0] = lax.dot_general(eye, v, (((1,), (1,)), ((), ())), preferred_element_type=F32).astype(BF16)


def _foxprep(z, z_rw, f_bias_pad, q_gain, k_gain):
    bsz, t_len, _ = z.shape
    tm = min(512, t_len)
    qk = jax.ShapeDtypeStruct((bsz, FOX_HEADS, t_len, LANES), BF16)
    qk_spec = pl.BlockSpec((1, FOX_HEADS, tm, LANES), lambda b, i: (b, 0, i, 0))
    sel = jnp.asarray(_fox_bias_selector(), BF16)
    return pl.pallas_call(
        _foxprep_kernel,
        out_shape=(qk, qk, jax.ShapeDtypeStruct((bsz, FOX_WIDTH, t_len), BF16)),
        grid=(bsz, t_len // tm),
        in_specs=[pl.BlockSpec((1, tm, 3 * FOX_WIDTH), lambda b, i: (b, i, Z_FOX // (3 * FOX_WIDTH))),
                  pl.BlockSpec((1, tm, LANES), lambda b, i: (b, i, (Z_F - Z_RW) // LANES)),
                  pl.BlockSpec((1, LANES), lambda b, i: (0, 0)),
                  pl.BlockSpec((1, LANES), lambda b, i: (0, 0)),
                  pl.BlockSpec((1, LANES), lambda b, i: (0, 0)),
                  pl.BlockSpec(sel.shape, lambda b, i: (0, 0))],
        out_specs=(qk_spec, qk_spec, pl.BlockSpec((1, FOX_WIDTH, tm), lambda b, i: (b, 0, i))),
        scratch_shapes=[pltpu.VMEM((1, LANES), F32)],
        compiler_params=_cparams(("parallel", "arbitrary")),
        name="foxprep",
    )(z, z_rw, f_bias_pad, jnp.tile(q_gain.reshape(1, -1), (1, 2)), jnp.tile(k_gain.reshape(1, -1), (1, 2)), sel)


def _fox_kernel(qi_ref, kj_ref, q_ref, k_ref, vt_ref, o_ref, m_ref, l_ref, acc_ref):
    i = qi_ref[pl.program_id(1)]
    j = kj_ref[pl.program_id(1)]
    tq = q_ref.shape[2]
    tk = k_ref.shape[2]
    sub = 8

    @pl.when(j == 0)
    def _():
        m_ref[...] = jnp.full_like(m_ref, MASK_VALUE)
        l_ref[...] = jnp.zeros_like(l_ref)
        acc_ref[...] = jnp.zeros_like(acc_ref)

    def scores(h):
        return lax.dot_general(k_ref[0, h], q_ref[0, h], (((1,), (1,)), ((), ())), preferred_element_type=F32)

    def update(diagonal):
        if diagonal:
            key = lax.broadcasted_iota(I32, (tk, tq), 0)
            qry = lax.broadcasted_iota(I32, (tk, tq), 1)
            keep = key <= qry
        s_next = scores(0)
        for h in range(FOX_HEADS):
            s = s_next
            if h + 1 < FOX_HEADS:
                s_next = scores(h + 1)
            if diagonal:
                s = jnp.where(keep, s, MASK_VALUE)
            s3 = s.reshape(tk // sub, sub, tq)
            m_prev = m_ref[h]
            m_cur = jnp.max(jnp.max(s3, axis=0), axis=0, keepdims=True)
            m_new = jnp.maximum(m_prev, m_cur)
            alpha = jnp.exp2(m_prev - m_new)
            p3 = jnp.exp2(s3 - m_new[None])
            l_ref[h] = alpha * l_ref[h] + jnp.sum(p3, axis=0)
            pv = jnp.dot(vt_ref[0, h * FOX_HEAD_DIM:(h + 1) * FOX_HEAD_DIM, :], p3.reshape(tk, tq).astype(BF16),
                         preferred_element_type=F32)
            acc = acc_ref[h].reshape(FOX_HEAD_DIM // sub, sub, tq) * alpha[None]
            acc_ref[h] = acc.reshape(FOX_HEAD_DIM, tq) + pv
            m_ref[h] = m_new

    @pl.when(j < i)
    def _():
        update(False)

    @pl.when(j == i)
    def _():
        update(True)
        outs = []
        for h in range(FOX_HEADS):
            denom = jnp.sum(l_ref[h], axis=0, keepdims=True)
            outs.append((acc_ref[h] / denom).astype(BF16))
        out_t = jnp.concatenate(outs, axis=0)
        ri = lax.broadcasted_iota(I32, (tq, tq), 0)
        ci = lax.broadcasted_iota(I32, (tq, tq), 1)
        eye = jnp.where(ri == ci, 1.0, 0.0).astype(BF16)
        o_ref[0] = lax.dot_general(eye, out_t, (((1,), (1,)), ((), ())),
                                   preferred_element_type=F32).astype(o_ref.dtype)


def _fox(q, k, vt):
    bsz, _, t_len, _ = q.shape
    tq = min(512, t_len)
    n_blk = t_len // tq
    pairs = [(i, j) for i in range(n_blk) for j in range(i + 1)]
    qi = jnp.asarray([p[0] for p in pairs], I32)
    kj = jnp.asarray([p[1] for p in pairs], I32)
    grid_spec = pltpu.PrefetchScalarGridSpec(
        num_scalar_prefetch=2,
        grid=(bsz, len(pairs)),
        in_specs=[pl.BlockSpec((1, FOX_HEADS, tq, LANES), lambda b, s, qi, kj: (b, 0, qi[s], 0)),
                  pl.BlockSpec((1, FOX_HEADS, tq, LANES), lambda b, s, qi, kj: (b, 0, kj[s], 0)),
                  pl.BlockSpec((1, FOX_WIDTH, tq), lambda b, s, qi, kj: (b, 0, kj[s]))],
        out_specs=pl.BlockSpec((1, tq, FOX_WIDTH), lambda b, s, qi, kj: (b, qi[s], 0)),
        scratch_shapes=[pltpu.VMEM((FOX_HEADS, 8, tq), F32), pltpu.VMEM((FOX_HEADS, 8, tq), F32),
                        pltpu.VMEM((FOX_HEADS, FOX_HEAD_DIM, tq), F32)],
    )
    return pl.pallas_call(
        _fox_kernel,
        out_shape=jax.ShapeDtypeStruct((bsz, t_len, FOX_WIDTH), BF16),
        grid_spec=grid_spec,
        compiler_params=_cparams(("parallel", "arbitrary")),
        name="fox",
    )(qi, kj, q, k, vt)


def _merge_kernel(zg_ref, ygm_ref, yrw_ref, yfox_ref, x_ref, g1_ref, sc2_ref, sh2_ref, pb_ref, wo_ref, wr_ref, br_ref,
                  x1_o, h2_o, idx_o, gate_o, rank_o, cnt_o, carry_ref):
    tm = x_ref.shape[1]

    @pl.when((pl.program_id(0) == 0) & (pl.program_id(1) == 0))
    def _():
        carry_ref[...] = jnp.zeros_like(carry_ref)

    sg = jax.nn.sigmoid(zg_ref[0].astype(F32))
    p_gm = jnp.dot(ygm_ref[0], pb_ref[0:GM_WIDTH, :], preferred_element_type=F32)
    y_rw = jnp.concatenate([yrw_ref[0, h] for h in range(RW_HEADS)], axis=-1)
    p_rw = jnp.dot(y_rw, pb_ref[GM_WIDTH:GM_WIDTH + RW_WIDTH, :], preferred_element_type=F32)
    p_fox = jnp.dot(yfox_ref[0], pb_ref[GM_WIDTH + RW_WIDTH:, :], preferred_element_type=F32)
    merged = sg[:, 0:D_MODEL] * p_gm + sg[:, D_MODEL:2 * D_MODEL] * p_rw + sg[:, 2 * D_MODEL:] * p_fox
    x1 = x_ref[0] + g1_ref[0] * jnp.dot(merged.astype(BF16), wo_ref[...], preferred_element_type=F32)
    x1_o[0] = x1
    h2 = x1 * lax.rsqrt(jnp.mean(x1 * x1, axis=-1, keepdims=True) + EPS) * (1.0 + sc2_ref[0]) + sh2_ref[0]
    h2_o[0] = _pack_halves(h2)

    h_hi, h_lo, _ = _split3(h2)
    w_hi, w_lo, _ = _split3(wr_ref[...])
    logits = (jnp.dot(h_hi, w_hi, preferred_element_type=F32) + jnp.dot(h_hi, w_lo, preferred_element_type=F32)
              + jnp.dot(h_lo, w_hi, preferred_element_type=F32)) + br_ref[...]
    lane = lax.broadcasted_iota(I32, (tm, N_EXPERTS), 1)
    vals, idxs = [], []
    rest = logits
    for _ in range(TOP_K):
        m = jnp.max(rest, axis=-1, keepdims=True)
        am = jnp.min(jnp.where(rest == m, lane, N_EXPERTS), axis=-1, keepdims=True)
        vals.append(m)
        idxs.append(am)
        rest = jnp.where(lane == am, -jnp.inf, rest)
    exps = [jnp.exp(val - vals[0]) for val in vals]
    denom = exps[0] + exps[1] + exps[2] + exps[3]

    onehot = jnp.zeros((tm, N_EXPERTS), F32)
    for am in idxs:
        onehot = onehot + jnp.where(lane == am, 1.0, 0.0)
    ri = lax.broadcasted_iota(I32, (tm, tm), 0)
    ci = lax.broadcasted_iota(I32, (tm, tm), 1)
    before = jnp.where(ri > ci, 1.0, 0.0).astype(BF16)
    seen = carry_ref[...] + jnp.dot(before, onehot.astype(BF16), preferred_element_type=F32)
    lane_k = lax.broadcasted_iota(I32, (tm, TOP_K), 1)
    idx_out = jnp.zeros((tm, TOP_K), I32)
    gate_out = jnp.zeros((tm, TOP_K), F32)
    rank_out = jnp.zeros((tm, TOP_K), I32)
    for kk in range(TOP_K):
        rank = jnp.sum(jnp.where(lane == idxs[kk], seen, 0.0), axis=-1, keepdims=True).astype(I32)
        idx_out = jnp.where(lane_k == kk, idxs[kk], idx_out)
        gate_out = jnp.where(lane_k == kk, exps[kk] / denom, gate_out)
        rank_out = jnp.where(lane_k == kk, rank, rank_out)
    idx_o[0] = idx_out
    gate_o[0] = gate_out
    rank_o[0] = rank_out
    total = carry_ref[...] + jnp.sum(onehot, axis=0, keepdims=True)
    carry_ref[...] = total
    cnt_o[...] = total.astype(I32)


def _merge(z, y_gm, y_rw, y_fox, x, gate1, scale2, shift2, w_branch, w_o, w_router, b_router, layer):
    bsz, t_len, d = x.shape
    tm = min(512, t_len)
    row = lambda w: pl.BlockSpec((1, tm, w), lambda b, i: (b, i, 0))
    mod = pl.BlockSpec((1, 1, d), lambda b, i: (b, 0, 0))
    full = lambda shape: pl.BlockSpec(shape, lambda b, i: (0,) * len(shape))
    return pl.pallas_call(
        _merge_kernel,
        out_shape=(jax.ShapeDtypeStruct((bsz, t_len, d), F32), jax.ShapeDtypeStruct((bsz, t_len, d // 2), I32),
                   jax.ShapeDtypeStruct((bsz, t_len, TOP_K), I32), jax.ShapeDtypeStruct((bsz, t_len, TOP_K), F32),
                   jax.ShapeDtypeStruct((bsz, t_len, TOP_K), I32), jax.ShapeDtypeStruct((1, N_EXPERTS), I32)),
        grid=(bsz, t_len // tm),
        in_specs=[row(N_BRANCH * D_MODEL), row(GM_WIDTH),
                  pl.BlockSpec((1, RW_HEADS, tm, RW_HEAD_DIM), lambda b, i: (b, 0, i, 0)),
                  row(FOX_WIDTH), row(d), mod, mod, mod,
                  pl.BlockSpec((MIX_WIDTH, d), lambda b, i: (layer, 0)), pl.BlockSpec((d, d), lambda b, i: (layer, 0)),
                  full(w_router.shape), full((1, N_EXPERTS))],
        out_specs=(row(d), row(d // 2), row(TOP_K), row(TOP_K), row(TOP_K), full((1, N_EXPERTS))),
        scratch_shapes=[pltpu.VMEM((1, N_EXPERTS), F32)],
        compiler_params=_cparams(("arbitrary", "arbitrary")),
        name="merge_router",
    )(z, y_gm, y_rw, y_fox, x, gate1, scale2, shift2, w_branch, w_o, w_router, b_router.reshape(1, N_EXPERTS))


def _sc_mesh():
    return plsc.VectorSubcoreMesh(core_axis_name="c", subcore_axis_name="s",
                                  num_cores=SC_CORES, num_subcores=SC_SUBCORES)


def _sc_worker():
    return lax.axis_index("s") * SC_CORES + lax.axis_index("c")


def _sc_scatter_rows(src, idx3, n_out):
    _, d = src.shape
    n_copy, n_grp, _ = idx3.shape
    grp_per_w = n_grp // SC_WORKERS

    def body(src_hbm, idx_hbm, out_hbm, idx_v, rows_v):
        g0 = _sc_worker() * grp_per_w
        for q in range(n_copy):
            pltpu.sync_copy(idx_hbm.at[q, pl.ds(g0, grp_per_w)], idx_v.at[pl.ds(q * grp_per_w, grp_per_w)])

        @pl.loop(0, grp_per_w)
        def _(j):
            pltpu.sync_copy(src_hbm.at[pl.ds((g0 + j) * SC_ROWS, SC_ROWS)], rows_v)
            for q in range(n_copy):
                pltpu.sync_copy(rows_v, out_hbm.at[idx_v.at[q * grp_per_w + j]])

    return pl.kernel(
        body, out_type=jax.ShapeDtypeStruct((n_out, d), src.dtype), mesh=_sc_mesh(),
        scratch_types=[pltpu.VMEM((n_copy * grp_per_w, SC_ROWS), I32), pltpu.VMEM((SC_ROWS, d), src.dtype)],
        name="sc_dispatch",
    )(src, idx3)


def _sc_gather_rows(table, idx2):
    _, d = table.shape
    n_grp, _ = idx2.shape
    grp_per_w = n_grp // SC_WORKERS

    def body(table_hbm, idx_hbm, out_hbm, idx_v, rows_v):
        g0 = _sc_worker() * grp_per_w
        pltpu.sync_copy(idx_hbm.at[pl.ds(g0, grp_per_w)], idx_v)

        @pl.loop(0, grp_per_w)
        def _(j):
            pltpu.sync_copy(table_hbm.at[idx_v.at[j]], rows_v)
            pltpu.sync_copy(rows_v, out_hbm.at[pl.ds((g0 + j) * SC_ROWS, SC_ROWS)])

    return pl.kernel(
        body, out_type=jax.ShapeDtypeStruct((n_grp * SC_ROWS, d), table.dtype), mesh=_sc_mesh(),
        scratch_types=[pltpu.VMEM((grp_per_w, SC_ROWS), I32), pltpu.VMEM((SC_ROWS, d), table.dtype)],
        name="sc_combine_gather",
    )(table, idx2)


def _ffn_kernel(be_ref, nv_ref, x_ref, wgu_ref, bgu_ref, wd_ref, bd_ref, o_ref, wgu_b, wd_b):
    i = pl.program_id(0)
    n_valid = nv_ref[i]

    @pl.when((i == 0) | (be_ref[i] != be_ref[jnp.maximum(i - 1, 0)]))
    def _():
        wgu_b[...] = wgu_ref[0].astype(BF16)
        wd_b[...] = wd_ref[0].astype(BF16)

    @pl.when(n_valid > 0)
    def _():
        rowid = lax.broadcasted_iota(I32, x_ref.shape, 0)
        xp = jnp.where(rowid < n_valid, x_ref[...], 0)
        x = jnp.concatenate(_unpack_halves(xp), axis=-1).astype(BF16)
        gu = jnp.dot(x, wgu_b[...], preferred_element_type=F32) + bgu_ref[0]
        g_ = jnp.minimum(gu[:, :D_FF], SWIGLU_LIMIT)
        u_ = jnp.clip(gu[:, D_FF:], -SWIGLU_LIMIT, SWIGLU_LIMIT)
        act = (u_ + 1.0) * (g_ * jax.nn.sigmoid(SWIGLU_ALPHA * g_))
        o_ref[...] = _pack_halves(jnp.dot(act.astype(BF16), wd_b[...], preferred_element_type=F32) + bd_ref[0])

    @pl.when(n_valid <= 0)
    def _():
        o_ref[...] = jnp.zeros_like(o_ref)


def _ffn(block_expert, block_valid, xin, w_gate_up, b_gate_up, w_down, b_down):
    n_rows, dp = xin.shape
    d = 2 * dp
    n_blocks = n_rows // MOE_BLOCK
    grid_spec = pltpu.PrefetchScalarGridSpec(
        num_scalar_prefetch=2,
        grid=(n_blocks,),
        in_specs=[pl.BlockSpec((MOE_BLOCK, dp), lambda i, be, nv: (i, 0)),
                  pl.BlockSpec((1, d, 2 * D_FF), lambda i, be, nv: (be[i], 0, 0)),
                  pl.BlockSpec((1, 1, 2 * D_FF), lambda i, be, nv: (be[i], 0, 0)),
                  pl.BlockSpec((1, D_FF, d), lambda i, be, nv: (be[i], 0, 0)),
                  pl.BlockSpec((1, 1, d), lambda i, be, nv: (be[i], 0, 0))],
        out_specs=pl.BlockSpec((MOE_BLOCK, dp), lambda i, be, nv: (i, 0)),
        scratch_shapes=[pltpu.VMEM((d, 2 * D_FF), BF16), pltpu.VMEM((D_FF, d), BF16)],
    )
    return pl.pallas_call(
        _ffn_kernel,
        out_shape=jax.ShapeDtypeStruct((n_rows, dp), I32),
        grid_spec=grid_spec,
        compiler_params=pltpu.CompilerParams(dimension_semantics=("arbitrary",), vmem_limit_bytes=FFN_VMEM_LIMIT),
        name="expert_ffn",
    )(block_expert, block_valid, xin, w_gate_up, b_gate_up, w_down, b_down)


def _combine_kernel(x1_ref, g2_ref, gate_ref, yg_ref, o_ref):
    gate = gate_ref[0]
    y_lo = y_hi = None
    for q in range(TOP_K):
        lo, hi = _unpack_halves(yg_ref[q, 0])
        wq = gate[:, q:q + 1]
        y_lo = wq * lo if y_lo is None else y_lo + wq * lo
        y_hi = wq * hi if y_hi is None else y_hi + wq * hi
    o_ref[0] = x1_ref[0] + g2_ref[0] * jnp.concatenate([y_lo, y_hi], axis=-1)


def _combine(x1, gate2, gate, yg):
    bsz, t_len, d = x1.shape
    tm = min(512, t_len)
    return pl.pallas_call(
        _combine_kernel,
        out_shape=jax.ShapeDtypeStruct((bsz, t_len, d), F32),
        grid=(bsz, t_len // tm),
        in_specs=[pl.BlockSpec((1, tm, d), lambda b, i: (b, i, 0)),
                  pl.BlockSpec((1, 1, d), lambda b, i: (b, 0, 0)),
                  pl.BlockSpec((1, tm, TOP_K), lambda b, i: (b, i, 0)),
                  pl.BlockSpec((TOP_K, 1, tm, d // 2), lambda b, i: (0, b, i, 0))],
        out_specs=pl.BlockSpec((1, tm, d), lambda b, i: (b, i, 0)),
        compiler_params=_cparams(("parallel", "parallel")),
        name="moe_combine",
    )(x1, gate2, gate, yg)


def _moe(x1, gate2, h2, top_idx, gate, rank, counts, w_gate_up, b_gate_up, w_down, b_down, layer):
    bsz, t_len, d = h2.shape
    n_tok = bsz * t_len
    n_assign = n_tok * TOP_K
    n_blocks = -(-n_assign // MOE_BLOCK) + N_EXPERTS
    counts = counts.reshape(N_EXPERTS)
    blocks_e = (counts + MOE_BLOCK - 1) // MOE_BLOCK
    blk_end = jnp.cumsum(blocks_e)
    blk_start = blk_end - blocks_e
    experts = jnp.arange(N_EXPERTS, dtype=I32)
    onehot = top_idx.reshape(n_tok, TOP_K, 1) == experts
    dest = jnp.sum(jnp.where(onehot, blk_start * MOE_BLOCK, 0), axis=-1) + rank.reshape(n_tok, TOP_K)
    dest_t = dest.T.astype(I32)
    blk = jnp.arange(n_blocks, dtype=I32)
    block_expert = jnp.minimum(jnp.sum(blk_end[None, :] <= blk[:, None], axis=1), N_EXPERTS - 1).astype(I32)
    be_hot = block_expert[:, None] == experts
    cnt_b = jnp.sum(jnp.where(be_hot, counts, 0), axis=1)
    start_b = jnp.sum(jnp.where(be_hot, blk_start, 0), axis=1)
    block_valid = jnp.clip(cnt_b - (blk - start_b) * MOE_BLOCK, 0, MOE_BLOCK).astype(I32)
    xin = _sc_scatter_rows(h2.reshape(n_tok, d), dest_t.reshape(TOP_K, n_tok // SC_ROWS, SC_ROWS),
                           n_blocks * MOE_BLOCK)
    yb = _ffn(block_expert + layer * N_EXPERTS, block_valid, xin, w_gate_up, b_gate_up, w_down, b_down)
    yg = _sc_gather_rows(yb, dest_t.reshape(n_assign // SC_ROWS, SC_ROWS))
    return _combine(x1, gate2, gate, yg.reshape(TOP_K, bsz, t_len, d))


def _permute_kernel(w_ref, o_ref):
    o_gm = 0
    o_rw = o_gm + 2 * GM_WIDTH
    o_fox = o_rw + RW_SHIFT_WIDTH
    o_f = o_fox + 3 * FOX_WIDTH
    o_gate = o_f + FOX_HEADS
    w = w_ref[0]
    o_ref[0, :, Z_GATE:Z_FOX] = w[:, o_gate:o_gate + N_BRANCH * D_MODEL].astype(BF16)
    o_ref[0, :, Z_FOX:Z_GM] = w[:, o_fox:o_f].astype(BF16)
    o_ref[0, :, Z_GM:Z_RW] = w[:, o_gm:o_rw].astype(BF16)
    o_ref[0, :, Z_RW:Z_F] = w[:, o_rw:o_fox].astype(BF16)
    tail = jnp.concatenate([w[:, o_f:o_gate], jnp.zeros((w.shape[0], Z_WIDTH - Z_F - FOX_HEADS), F32)], axis=-1)
    o_ref[0, :, Z_F:Z_WIDTH] = tail.astype(BF16)


def _permute_w_in(w_in):
    n_layer, d, w_cols = w_in.shape
    tr = 256
    return pl.pallas_call(
        _permute_kernel,
        out_shape=jax.ShapeDtypeStruct((n_layer, d, Z_WIDTH), BF16),
        grid=(n_layer, d // tr),
        in_specs=[pl.BlockSpec((1, tr, w_cols), lambda l, i: (l, i, 0))],
        out_specs=pl.BlockSpec((1, tr, Z_WIDTH), lambda l, i: (l, i, 0)),
        compiler_params=_cparams(("parallel", "parallel")),
        name="permute_w_in",
    )(w_in)


def _layer(x, mod, w_in_p, gm_v_gain, gm_w_s, gm_b_s, mu_pad, w_lora, rw_w0, rw_a0, rw_k_k, rw_k_a, rw_r_k,
           rw_gn_gain, rw_gn_bias, f_bias_pad, fox_q_gain, fox_k_gain, w_branch, w_o, w_router, b_router,
           w_gate_up, b_gate_up, w_down, b_down, layer):
    shift1, scale1, gate1, shift2, scale2, gate2 = (mod[:, i][:, None, :] for i in range(6))
    z, z_rw = _inproj(x, scale1, shift1, w_in_p, layer)
    y_gm = _gmlp(z, gm_v_gain, gm_w_s, gm_b_s)
    r, lw, k, v, a, b, g = _rwprep(z_rw, mu_pad, w_lora, rw_w0, rw_a0, rw_k_k, rw_k_a)
    y_rw = _rwscan(r, lw, k, v, a, b, g, rw_r_k, rw_gn_gain, rw_gn_bias)
    q, kf, vf = _foxprep(z, z_rw, f_bias_pad, fox_q_gain, fox_k_gain)
    y_fox = _fox(q, kf, vf)
    x1, h2, top_idx, gate, rank, counts = _merge(z, y_gm, y_rw, y_fox, x, gate1, scale2, shift2,
                                                 w_branch, w_o, w_router, b_router, layer)
    return _moe(x1, gate2, h2, top_idx, gate, rank, counts, w_gate_up, b_gate_up, w_down, b_down, layer)


def kernel(x, c, w_ada, b_ada, w_in, gm_v_gain, gm_w_s, gm_b_s, rw_mu, rw_w0, rw_w2, rw_a0, rw_a2, rw_g2, rw_k_k,
           rw_k_a, rw_r_k, rw_gn_gain, rw_gn_bias, fox_f_bias, fox_q_gain, fox_k_gain, w_branch, w_o, w_router,
           b_router, w_gate_up, b_gate_up, w_down, b_down):
    n_layer = w_ada.shape[0]
    bsz = x.shape[0]
    c_pad = jnp.zeros((8, D_MODEL), F32).at[:bsz].set(c)
    mod = _adaln(c_pad, w_ada, b_ada)[:, :bsz].reshape(n_layer, bsz, 6, D_MODEL)
    w_in_p = _permute_w_in(w_in)
    mu_pad = jnp.pad(rw_mu, ((0, 0), (0, RW_BLOCK - RW_SHIFT_WIDTH)))
    w_lora = jnp.zeros((n_layer, RW_LORA, 3 * RW_WIDTH), F32)
    w_lora = w_lora.at[:, 0:RW_DECAY_LORA, 0:RW_WIDTH].set(rw_w2)
    w_lora = w_lora.at[:, RW_DECAY_LORA:RW_DECAY_LORA + RW_ICLR_LORA, RW_WIDTH:2 * RW_WIDTH].set(rw_a2)
    w_lora = w_lora.at[:, RW_DECAY_LORA + RW_ICLR_LORA:, 2 * RW_WIDTH:].set(rw_g2)
    f_bias_pad = jnp.pad(fox_f_bias, ((0, 0), (0, LANES - FOX_HEADS)))
    w_in_p = w_in_p.reshape(n_layer * D_MODEL, Z_WIDTH)
    w_branch_b = w_branch.astype(BF16).reshape(n_layer * MIX_WIDTH, D_MODEL)
    w_o_b = w_o.astype(BF16).reshape(n_layer * D_MODEL, D_MODEL)
    w_gu = w_gate_up.reshape(n_layer * N_EXPERTS, D_MODEL, 2 * D_FF)
    b_gu = b_gate_up.reshape(n_layer * N_EXPERTS, 1, 2 * D_FF)
    w_dn = w_down.reshape(n_layer * N_EXPERTS, D_FF, D_MODEL)
    b_dn = b_down.reshape(n_layer * N_EXPERTS, 1, D_MODEL)
    for l in range(n_layer):
        x = _layer(x, mod[l], w_in_p, gm_v_gain[l], gm_w_s[l], gm_b_s[l], mu_pad[l:l + 1], w_lora[l], rw_w0[l],
                   rw_a0[l], rw_k_k[l], rw_k_a[l], rw_r_k[l], rw_gn_gain[l], rw_gn_bias[l], f_bias_pad[l:l + 1],
                   fox_q_gain[l], fox_k_gain[l], w_branch_b, w_o_b, w_router[l], b_router[l],
                   w_gu, b_gu, w_dn, b_dn, l)
    return x
```

```python
import functools

import jax
import jax.numpy as jnp
import numpy as np
from jax import lax
from jax.experimental import pallas as pl
from jax.experimental.pallas import tpu as pltpu
from jax.experimental.pallas import tpu_sc as plsc

F32 = jnp.float32
BF16 = jnp.bfloat16
I32 = jnp.int32
HIGHEST = lax.Precision.HIGHEST

D_MODEL = 1024
GM_CHUNK = 128
GM_GROUPS = 4
GM_WIDTH = 256
GM_GROUP_DIM = GM_WIDTH // GM_GROUPS
RW_HEADS = 4
RW_HEAD_DIM = 64
RW_WIDTH = RW_HEADS * RW_HEAD_DIM
RW_DECAY_LORA = 32
RW_ICLR_LORA = 32
RW_GATE_LORA = 64
RW_LORA = RW_DECAY_LORA + RW_ICLR_LORA + RW_GATE_LORA
RW_SHIFT_WIDTH = 3 * RW_WIDTH + RW_LORA
RW_GN_EPS = 64e-5
FOX_HEADS = 8
FOX_HEAD_DIM = 64
FOX_WIDTH = FOX_HEADS * FOX_HEAD_DIM
ATTN_SCALE = FOX_HEAD_DIM ** -0.5
MASK_VALUE = -1e30
LOG2E = 1.4426950408889634
N_BRANCH = 3
MIX_WIDTH = GM_WIDTH + RW_WIDTH + FOX_WIDTH
N_EXPERTS = 32
TOP_K = 4
D_FF = D_MODEL
SWIGLU_LIMIT = 7.0
SWIGLU_ALPHA = 1.702
MOE_BLOCK = 256
EPS = 1e-6

Z_GATE = 0
Z_FOX = N_BRANCH * D_MODEL
Z_GM = Z_FOX + 3 * FOX_WIDTH
Z_RW = Z_GM + 2 * GM_WIDTH
RW_BLOCK = 1024
Z_F = Z_RW + RW_SHIFT_WIDTH
Z_WIDTH = Z_RW + RW_BLOCK
LANES = 128
RW_CHUNK = 64
RW_PREP_UNROLL = 4

VMEM_LIMIT = 48 * 1024 * 1024
FFN_VMEM_LIMIT = 56 * 1024 * 1024
FFN_STEP_BLOCKS = 4
SC_CORES = 2
SC_SUBCORES = 16
SC_WORKERS = SC_CORES * SC_SUBCORES
SC_ROWS = 64


def _cparams(sem):
    return pltpu.CompilerParams(dimension_semantics=sem, vmem_limit_bytes=VMEM_LIMIT)


def _mm(a, b):
    return jnp.dot(a.astype(BF16), b.astype(BF16), preferred_element_type=F32)


def _mm_nt(a, b):
    return lax.dot_general(a.astype(BF16), b.astype(BF16), (((1,), (1,)), ((), ())), preferred_element_type=F32)


def _mm_tn(a, b):
    return lax.dot_general(a.astype(BF16), b.astype(BF16), (((0,), (0,)), ((), ())), preferred_element_type=F32)


def _split3(x):
    hi = x.astype(BF16)
    r1 = x - hi.astype(F32)
    mid = r1.astype(BF16)
    lo = (r1 - mid.astype(F32)).astype(BF16)
    return hi, mid, lo


def _tri_cumsum(x, n):
    ri = lax.broadcasted_iota(I32, (n, n), 0)
    ci = lax.broadcasted_iota(I32, (n, n), 1)
    ones = jnp.where(ri >= ci, 1.0, 0.0).astype(BF16)
    hi, mid, lo = _split3(x)
    return (jnp.dot(ones, hi, preferred_element_type=F32) + jnp.dot(ones, mid, preferred_element_type=F32)
            + jnp.dot(ones, lo, preferred_element_type=F32))


def _pack_halves(x):
    w = x.shape[1] // 2
    hi = pltpu.bitcast(x[:, :w].astype(BF16).astype(F32), jnp.uint32)
    lo = pltpu.bitcast(x[:, w:].astype(BF16).astype(F32), jnp.uint32)
    return pltpu.bitcast(hi | (lo >> 16), I32)


def _unpack_halves(p):
    u = pltpu.bitcast(p, jnp.uint32)
    return pltpu.bitcast(u & jnp.uint32(0xFFFF0000), F32), pltpu.bitcast(u << 16, F32)


def _log_sigmoid(x):
    return jnp.minimum(x, 0.0) - jnp.log1p(jnp.exp(-jnp.abs(x)))


def _adaln_kernel(c_ref, w_ref, b_ref, o_ref):
    c = c_ref[...]
    s = c * jax.nn.sigmoid(c)
    o_ref[0] = jnp.dot(s, w_ref[0], preferred_element_type=F32, precision=HIGHEST) + b_ref[0]


def _adaln(c_pad, w_ada, b_ada):
    n_layer, d, w6 = w_ada.shape
    tn = 1536
    return pl.pallas_call(
        _adaln_kernel,
        out_shape=jax.ShapeDtypeStruct((n_layer, c_pad.shape[0], w6), F32),
        grid=(n_layer, w6 // tn),
        in_specs=[pl.BlockSpec(c_pad.shape, lambda l, j: (0, 0)),
                  pl.BlockSpec((1, d, tn), lambda l, j: (l, 0, j)),
                  pl.BlockSpec((1, 1, tn), lambda l, j: (l, 0, j))],
        out_specs=pl.BlockSpec((1, c_pad.shape[0], tn), lambda l, j: (l, 0, j)),
        compiler_params=_cparams(("parallel", "parallel")),
        name="adaln",
    )(c_pad, w_ada, b_ada.reshape(n_layer, 1, w6))


def _inproj_kernel(x_ref, sc_ref, sh_ref, w_ref, zm_ref, zr_ref, xn_ref):
    j = pl.program_id(2)

    @pl.when(j == 0)
    def _():
        x = x_ref[0]
        xn = x * lax.rsqrt(jnp.mean(x * x, axis=-1, keepdims=True) + EPS)
        xn_ref[...] = (xn * (1.0 + sc_ref[0]) + sh_ref[0]).astype(BF16)

    acc = jnp.dot(xn_ref[...], w_ref[...], preferred_element_type=F32)

    @pl.when(j < Z_RW // RW_BLOCK)
    def _():
        zm_ref[0] = acc.astype(BF16)

    @pl.when(j == Z_RW // RW_BLOCK)
    def _():
        zr_ref[0] = acc


def _inproj(x, scale, shift, w, layer):
    bsz, t_len, d = x.shape
    tm = min(1024, t_len)
    tn = RW_BLOCK
    n_main = Z_RW // tn
    return pl.pallas_call(
        _inproj_kernel,
        out_shape=(jax.ShapeDtypeStruct((bsz, t_len, Z_RW), BF16), jax.ShapeDtypeStruct((bsz, t_len, RW_BLOCK), F32)),
        grid=(bsz, t_len // tm, Z_WIDTH // tn),
        in_specs=[pl.BlockSpec((1, tm, d), lambda b, i, j: (b, i, 0)),
                  pl.BlockSpec((1, 1, d), lambda b, i, j: (b, 0, 0)),
                  pl.BlockSpec((1, 1, d), lambda b, i, j: (b, 0, 0)),
                  pl.BlockSpec((d, tn), lambda b, i, j: (layer, j))],
        out_specs=(pl.BlockSpec((1, tm, tn), lambda b, i, j: (b, i, jnp.minimum(j, n_main - 1))),
                   pl.BlockSpec((1, tm, tn), lambda b, i, j: (b, i, 0))),
        scratch_shapes=[pltpu.VMEM((tm, d), BF16)],
        compiler_params=_cparams(("parallel", "parallel", "arbitrary")),
        name="inproj",
    )(x, scale, shift, w)


def _gmlp_kernel(z_ref, gain_ref, ws_ref, bst_ref, o_ref):
    tm = z_ref.shape[1]
    z = z_ref[0].astype(F32)
    u = jax.nn.gelu(z[:, :GM_WIDTH])
    v = jax.nn.gelu(z[:, GM_WIDTH:])
    v = v * lax.rsqrt(jnp.mean(v * v, axis=-1, keepdims=True) + EPS) * gain_ref[...]
    vb = v.astype(BF16)
    grp = lax.broadcasted_iota(I32, (GM_CHUNK, GM_WIDTH), 1) // GM_GROUP_DIM
    ri = lax.broadcasted_iota(I32, (GM_CHUNK, GM_CHUNK), 0)
    ci = lax.broadcasted_iota(I32, (GM_CHUNK, GM_CHUNK), 1)
    causal = ri >= ci
    bias = jnp.zeros((GM_CHUNK, GM_WIDTH), F32)
    ws = []
    for g in range(GM_GROUPS):
        ws.append(jnp.where(causal, ws_ref[g], 0.0).astype(BF16))
        bias = jnp.where(grp == g, bst_ref[:, g:g + 1], bias)
    for c in range(tm // GM_CHUNK):
        rows = slice(c * GM_CHUNK, (c + 1) * GM_CHUNK)
        vc = vb[rows]
        mixed = bias
        for g in range(GM_GROUPS):
            m = jnp.dot(ws[g], vc, preferred_element_type=F32)
            mixed = mixed + jnp.where(grp == g, m, 0.0)
        o_ref[0, rows, :] = (u[rows] * mixed).astype(o_ref.dtype)


def _gmlp(z, gain, w_s, b_s):
    bsz, t_len, _ = z.shape
    tm = min(512, t_len)
    return pl.pallas_call(
        _gmlp_kernel,
        out_shape=jax.ShapeDtypeStruct((bsz, t_len, GM_WIDTH), BF16),
        grid=(bsz, t_len // tm),
        in_specs=[pl.BlockSpec((1, tm, 2 * GM_WIDTH), lambda b, i: (b, i, Z_GM // (2 * GM_WIDTH))),
                  pl.BlockSpec((1, GM_WIDTH), lambda b, i: (0, 0)),
                  pl.BlockSpec((GM_GROUPS, GM_CHUNK, GM_CHUNK), lambda b, i: (0, 0, 0)),
                  pl.BlockSpec((GM_CHUNK, GM_GROUPS), lambda b, i: (0, 0))],
        out_specs=pl.BlockSpec((1, tm, GM_WIDTH), lambda b, i: (b, i, 0)),
        compiler_params=_cparams(("parallel", "parallel")),
        name="gmlp",
    )(z, gain.reshape(1, GM_WIDTH), w_s, b_s.T)


def _rwprep_kernel(z_ref, zp_ref, mu_ref, wl_ref, w0_ref, a0_ref, kk_ref, ka_ref,
                   r_o, lw_o, k_o, v_o, a_o, b_o, g_o):
    tm = z_ref.shape[1]
    z = z_ref[0]
    prev = jnp.where(pl.program_id(1) > 0, zp_ref[0, 7:8, :], 0.0)
    rowid = lax.broadcasted_iota(I32, z.shape, 0)
    zs = jnp.where(rowid == 0, prev, pltpu.roll(z, 1, axis=0))
    zz = z + mu_ref[...] * (zs - z)
    r = zz[:, 0:RW_WIDTH]
    k = zz[:, RW_WIDTH:2 * RW_WIDTH]
    v = zz[:, 2 * RW_WIDTH:3 * RW_WIDTH]
    lo = zz[:, 3 * RW_WIDTH:3 * RW_WIDTH + RW_LORA]
    lane = lax.broadcasted_iota(I32, (tm, RW_LORA), 1)
    act = jnp.where(lane < RW_DECAY_LORA, jnp.tanh(lo),
                    jnp.where(lane < RW_DECAY_LORA + RW_ICLR_LORA, lo, jax.nn.sigmoid(lo)))
    proj = jnp.dot(act, wl_ref[...], preferred_element_type=F32, precision=HIGHEST)
    xw = -(w0_ref[...] + proj[:, 0:RW_WIDTH])
    softplus = jnp.maximum(xw, 0.0) + jnp.log1p(jnp.exp(-jnp.abs(xw)))
    lw = -jnp.exp(-softplus - 0.5)
    a = jax.nn.sigmoid(a0_ref[...] + proj[:, RW_WIDTH:2 * RW_WIDTH])
    g = proj[:, 2 * RW_WIDTH:3 * RW_WIDTH]
    kk = k * kk_ref[...]
    k2 = k * (1.0 + (a - 1.0) * ka_ref[...])
    for h in range(RW_HEADS):
        sl = slice(h * RW_HEAD_DIM, (h + 1) * RW_HEAD_DIM)
        kkh = kk[:, sl]
        nrm = jnp.sqrt(jnp.sum(kkh * kkh, axis=-1, keepdims=True))
        kkh = kkh / jnp.maximum(nrm, 1e-12)
        r_o[0, h] = r[:, sl]
        lw_o[0, h] = lw[:, sl]
        k_o[0, h] = k2[:, sl]
        v_o[0, h] = v[:, sl]
        a_o[0, h] = -kkh
        b_o[0, h] = kkh * a[:, sl]
        g_o[0, h] = g[:, sl]


def _rwprep(z, mu_pad, w_lora, w0, a0, k_k, k_a):
    bsz, t_len, _ = z.shape
    tm = min(512, t_len)
    hm = jax.ShapeDtypeStruct((bsz, RW_HEADS, t_len, RW_HEAD_DIM), F32)
    hm_spec = pl.BlockSpec((1, RW_HEADS, tm, RW_HEAD_DIM), lambda b, i: (b, 0, i, 0))
    vec = lambda n: pl.BlockSpec((1, n), lambda b, i: (0, 0))
    rw_blk = 0
    return pl.pallas_call(
        _rwprep_kernel,
        out_shape=(hm,) * 7,
        grid=(bsz, t_len // tm),
        in_specs=[pl.BlockSpec((1, tm, RW_BLOCK), lambda b, i: (b, i, rw_blk)),
                  pl.BlockSpec((1, 8, RW_BLOCK), lambda b, i: (b, jnp.maximum(i * (tm // 8) - 1, 0), rw_blk)),
                  vec(RW_BLOCK),
                  pl.BlockSpec((RW_LORA, 3 * RW_WIDTH), lambda b, i: (0, 0)),
                  vec(RW_WIDTH), vec(RW_WIDTH), vec(RW_WIDTH), vec(RW_WIDTH)],
        out_specs=(hm_spec,) * 7,
        compiler_params=_cparams(("parallel", "parallel")),
        name="rwprep",
    )(z, z, mu_pad, w_lora, w0.reshape(1, -1), a0.reshape(1, -1), k_k.reshape(1, -1), k_a.reshape(1, -1))


def _rwscan_kernel(r_ref, lw_ref, k_ref, v_ref, a_ref, b_ref, g_ref, rk_ref, gg_ref, gb_ref, o_ref,
                   s_ref, rp_ref, y_ref, gm_ref, h0_ref, we_ref):
    cl = RW_CHUNK
    tb = r_ref.shape[2]
    n_chunk = tb // cl

    @pl.when(pl.program_id(1) == 0)
    def _():
        s_ref[...] = jnp.zeros_like(s_ref)

    n = RW_HEADS * cl
    ri = lax.broadcasted_iota(I32, (n, n), 0)
    ci = lax.broadcasted_iota(I32, (n, n), 1)
    same_head = (ri // cl) == (ci // cl)
    lower = same_head & (ri >= ci)
    strict = same_head & (ri > ci)
    eye = jnp.where(ri == ci, 1.0, 0.0)
    ones_lower = jnp.where(lower, 1.0, 0.0).astype(BF16)

    def prepare(chunks):
        grp = range(len(chunks))
        each = lambda fn: [fn(u) for u in grp]
        rows = [pl.ds(pl.multiple_of(c * cl, cl), cl) for c in chunks]
        stack = lambda ref: each(lambda u: ref[0, :, rows[u], :].reshape(n, RW_HEAD_DIM))
        r, lw, k, v, a, b = (stack(ref) for ref in (r_ref, lw_ref, k_ref, v_ref, a_ref, b_ref))
        hd = RW_HEAD_DIM
        parts = each(lambda u: jnp.concatenate(_split3(lw[u]), axis=-1))
        sums = each(lambda u: jnp.dot(ones_lower, parts[u], preferred_element_type=F32))
        cw = each(lambda u: sums[u][:, :hd] + sums[u][:, hd:2 * hd] + sums[u][:, 2 * hd:])
        w_in = each(lambda u: jnp.exp(cw[u]))
        w_inv = each(lambda u: jnp.exp(-cw[u]))
        rt = each(lambda u: r[u] * w_in[u])
        at = each(lambda u: a[u] * jnp.exp(cw[u] - lw[u]))
        kt = each(lambda u: k[u] * w_inv[u])
        bt = each(lambda u: b[u] * w_inv[u])
        w_end = each(lambda u: w_in[u].reshape(RW_HEADS, cl, RW_HEAD_DIM)[:, cl - 1:cl, :])
        w_end_rows = each(lambda u: jnp.broadcast_to(w_end[u], (RW_HEADS, cl, RW_HEAD_DIM)).reshape(n, RW_HEAD_DIM))
        a_ab = each(lambda u: jnp.where(strict, _mm_nt(at[u], bt[u]), 0.0))
        a_ak = each(lambda u: jnp.where(strict, _mm_nt(at[u], kt[u]), 0.0))
        m_rb = each(lambda u: jnp.where(lower, _mm_nt(rt[u], bt[u]), 0.0))
        m_rk = each(lambda u: jnp.where(lower, _mm_nt(rt[u], kt[u]), 0.0))
        inv = each(lambda u: eye + a_ab[u])
        p = a_ab
        for _ in range(cl.bit_length() - 2):
            p = [_mm(p[u], p[u]) for u in grp]
            inv = [inv[u] + _mm(inv[u], p[u]) for u in grp]
        akv = each(lambda u: _mm(a_ak[u], v[u]))
        apz = each(lambda u: _mm(inv[u], jnp.concatenate([at[u], akv[u]], axis=-1)).astype(BF16))
        mix = each(lambda u: jnp.dot(m_rb[u].astype(BF16), apz[u], preferred_element_type=F32))
        bend = each(lambda u: bt[u] * w_end_rows[u])
        kend = each(lambda u: kt[u] * w_end_rows[u])
        rp = each(lambda u: (rt[u] + mix[u][:, :hd]).astype(BF16))
        y0 = each(lambda u: mix[u][:, hd:] + _mm(m_rk[u], v[u]))
        for u in grp:
            for h in range(RW_HEADS):
                hs = slice(h * cl, (h + 1) * cl)
                both = _mm_tn(apz[u][hs], bend[u][hs])
                rp_ref[h, rows[u], :] = rp[u][hs]
                y_ref[h, rows[u], :] = y0[u][hs]
                gm_ref[h, rows[u], :] = both[:hd].astype(BF16)
                h0_ref[h, rows[u], :] = both[hd:] + _mm_tn(v[u][hs], kend[u][hs])
                we_ref[h, chunks[u]] = w_end[u][h]

    def prepare_step(i, carry):
        prepare([i * RW_PREP_UNROLL + u for u in range(RW_PREP_UNROLL)])
        return carry

    lax.fori_loop(0, n_chunk // RW_PREP_UNROLL, prepare_step, 0)

    def advance(c, carry):
        rows = pl.ds(pl.multiple_of(c * cl, cl), cl)
        for h in range(RW_HEADS):
            s = s_ref[h]
            sb = s.astype(BF16)
            y_ref[h, rows, :] = y_ref[h, rows, :] + lax.dot_general(
                rp_ref[h, rows, :], sb, (((1,), (1,)), ((), ())), preferred_element_type=F32)
            s_ref[h] = (s * we_ref[h, c] + jnp.dot(sb, gm_ref[h, rows, :], preferred_element_type=F32)
                        + h0_ref[h, rows, :])
        return carry

    lax.fori_loop(0, n_chunk, advance, 0)

    for h in range(RW_HEADS):
        y = y_ref[h]
        mu = jnp.mean(y, axis=-1, keepdims=True)
        yc = y - mu
        var = jnp.mean(yc * yc, axis=-1, keepdims=True)
        yn = yc * lax.rsqrt(var + RW_GN_EPS) * gg_ref[h] + gb_ref[h]
        v = v_ref[0, h]
        bonus = jnp.sum(r_ref[0, h] * k_ref[0, h] * rk_ref[h], axis=-1, keepdims=True) * v
        o_ref[0, h] = ((yn + bonus) * g_ref[0, h]).astype(o_ref.dtype)


def _rwscan(r, lw, k, v, a, b, g, r_k, gn_gain, gn_bias):
    bsz, _, t_len, _ = r.shape
    tb = min(512, t_len)
    hm_spec = pl.BlockSpec((1, RW_HEADS, tb, RW_HEAD_DIM), lambda bi, i: (bi, 0, i, 0))
    par = pl.BlockSpec((RW_HEADS, 1, RW_HEAD_DIM), lambda bi, i: (0, 0, 0))
    hshape = (RW_HEADS, 1, RW_HEAD_DIM)
    return pl.pallas_call(
        _rwscan_kernel,
        out_shape=jax.ShapeDtypeStruct((bsz, RW_HEADS, t_len, RW_HEAD_DIM), BF16),
        grid=(bsz, t_len // tb),
        in_specs=[hm_spec] * 7 + [par] * 3,
        out_specs=hm_spec,
        scratch_shapes=[pltpu.VMEM((RW_HEADS, RW_HEAD_DIM, RW_HEAD_DIM), F32),
                        pltpu.VMEM((RW_HEADS, tb, RW_HEAD_DIM), BF16), pltpu.VMEM((RW_HEADS, tb, RW_HEAD_DIM), F32),
                        pltpu.VMEM((RW_HEADS, tb, RW_HEAD_DIM), BF16), pltpu.VMEM((RW_HEADS, tb, RW_HEAD_DIM), F32),
                        pltpu.VMEM((RW_HEADS, tb // RW_CHUNK, 1, RW_HEAD_DIM), F32)],
        compiler_params=_cparams(("parallel", "arbitrary")),
        name="rwscan",
    )(r, lw, k, v, a, b, g, r_k.reshape(hshape), gn_gain.reshape(hshape), gn_bias.reshape(hshape))


FOX_PAIRS = FOX_HEADS // 2
FOX_EXTRA = 3


def _fox_bias_selector():
    sel = np.zeros((LANES, 2 * FOX_HEADS * LANES), np.float32)
    for h in range(FOX_HEADS):
        base = FOX_HEAD_DIM if h % 2 == 0 else 0
        for p in range(FOX_EXTRA):
            sel[p * FOX_HEADS + h, h * LANES + base + p] = 1.0
            sel[p * FOX_HEADS + h, (FOX_HEADS + h) * LANES + base + FOX_EXTRA + p] = -1.0
    return sel


def _foxprep_kernel(z_ref, f_ref, fb_ref, qg_ref, kg_ref, sel_ref, q_o, k_o, vt_o, carry_ref):
    tm = z_ref.shape[1]

    @pl.when(pl.program_id(1) == 0)
    def _():
        carry_ref[...] = jnp.zeros_like(carry_ref)

    log_f = _log_sigmoid(f_ref[0] + fb_ref[...])
    cum = carry_ref[...] + _tri_cumsum(log_f, tm)
    carry_ref[...] = cum[tm - 1:tm, :]
    lane = lax.broadcasted_iota(I32, (tm, LANES), 1)
    hi, mid, lo = (p.astype(F32) for p in _split3(cum * LOG2E))
    packed = jnp.where(lane < FOX_HEADS, hi,
                       jnp.where(lane < 2 * FOX_HEADS, pltpu.roll(mid, FOX_HEADS, axis=1),
                                 pltpu.roll(lo, 2 * FOX_HEADS, axis=1)))
    packed = jnp.where(lane < FOX_EXTRA * FOX_HEADS, packed, 0.0).astype(BF16)
    extra = jnp.dot(packed, sel_ref[...], preferred_element_type=F32)

    left = lane < FOX_HEAD_DIM
    in_half = lane % FOX_HEAD_DIM
    ones_q = jnp.where((in_half >= FOX_EXTRA) & (in_half < 2 * FOX_EXTRA), 1.0, 0.0)
    ones_k = jnp.where(in_half < FOX_EXTRA, 1.0, 0.0)

    def normed(block, gain):
        sq = block * block
        s_left = jnp.sum(jnp.where(left, sq, 0.0), axis=-1, keepdims=True)
        s_right = jnp.sum(jnp.where(left, 0.0, sq), axis=-1, keepdims=True)
        ms = jnp.where(left, s_left, s_right) * (1.0 / FOX_HEAD_DIM)
        return block * lax.rsqrt(ms + EPS) * gain

    for j in range(FOX_PAIRS):
        qn = normed(z_ref[0, :, j * LANES:(j + 1) * LANES].astype(F32), qg_ref[...] * (ATTN_SCALE * LOG2E))
        kn = normed(z_ref[0, :, FOX_WIDTH + j * LANES:FOX_WIDTH + (j + 1) * LANES].astype(F32), kg_ref[...])
        for par in range(2):
            h = 2 * j + par
            own = left if par == 0 else jnp.logical_not(left)
            q_o[0, h] = jnp.where(own, qn, extra[:, h * LANES:(h + 1) * LANES] + ones_q).astype(BF16)
            k_o[0, h] = jnp.where(own, kn, extra[:, (FOX_HEADS + h) * LANES:(FOX_HEADS + h + 1) * LANES]
                                  + ones_k).astype(BF16)
    ri = lax.broadcasted_iota(I32, (FOX_WIDTH, FOX_WIDTH), 0)
    ci = lax.broadcasted_iota(I32, (FOX_WIDTH, FOX_WIDTH), 1)
    eye = jnp.where(ri == ci, 1.0, 0.0).astype(BF16)
    v = z_ref[0, :, 2 * FOX_WIDTH:3 * FOX_WIDTH].astype(BF16)
    vt_o[0] = lax.dot_general(eye, v, (((1,), (1,)), ((), ())), preferred_element_type=F32).astype(BF16)


def _foxprep(z, z_rw, f_bias_pad, q_gain, k_gain):
    bsz, t_len, _ = z.shape
    tm = min(512, t_len)
    qk = jax.ShapeDtypeStruct((bsz, FOX_HEADS, t_len, LANES), BF16)
    qk_spec = pl.BlockSpec((1, FOX_HEADS, tm, LANES), lambda b, i: (b, 0, i, 0))
    sel = jnp.asarray(_fox_bias_selector(), BF16)
    return pl.pallas_call(
        _foxprep_kernel,
        out_shape=(qk, qk, jax.ShapeDtypeStruct((bsz, FOX_WIDTH, t_len), BF16)),
        grid=(bsz, t_len // tm),
        in_specs=[pl.BlockSpec((1, tm, 3 * FOX_WIDTH), lambda b, i: (b, i, Z_FOX // (3 * FOX_WIDTH))),
                  pl.BlockSpec((1, tm, LANES), lambda b, i: (b, i, (Z_F - Z_RW) // LANES)),
                  pl.BlockSpec((1, LANES), lambda b, i: (0, 0)),
                  pl.BlockSpec((1, LANES), lambda b, i: (0, 0)),
                  pl.BlockSpec((1, LANES), lambda b, i: (0, 0)),
                  pl.BlockSpec(sel.shape, lambda b, i: (0, 0))],
        out_specs=(qk_spec, qk_spec, pl.BlockSpec((1, FOX_WIDTH, tm), lambda b, i: (b, 0, i))),
        scratch_shapes=[pltpu.VMEM((1, LANES), F32)],
        compiler_params=_cparams(("parallel", "arbitrary")),
        name="foxprep",
    )(z, z_rw, f_bias_pad, jnp.tile(q_gain.reshape(1, -1), (1, 2)), jnp.tile(k_gain.reshape(1, -1), (1, 2)), sel)


def _fox_kernel(qi_ref, kj_ref, q_ref, k_ref, vt_ref, o_ref, m_ref, l_ref, acc_ref):
    i = qi_ref[pl.program_id(1)]
    j = kj_ref[pl.program_id(1)]
    tq = q_ref.shape[2]
    tk = k_ref.shape[2]
    sub = 8

    @pl.when(j == 0)
    def _():
        m_ref[...] = jnp.full_like(m_ref, MASK_VALUE)
        l_ref[...] = jnp.zeros_like(l_ref)
        acc_ref[...] = jnp.zeros_like(acc_ref)

    def scores(h):
        return lax.dot_general(k_ref[0, h], q_ref[0, h], (((1,), (1,)), ((), ())), preferred_element_type=F32)

    def update(diagonal):
        if diagonal:
            key = lax.broadcasted_iota(I32, (tk, tq), 0)
            qry = lax.broadcasted_iota(I32, (tk, tq), 1)
            keep = key <= qry
        s_next = scores(0)
        for h in range(FOX_HEADS):
            s = s_next
            if h + 1 < FOX_HEADS:
                s_next = scores(h + 1)
            if diagonal:
                s = jnp.where(keep, s, MASK_VALUE)
            s3 = s.reshape(tk // sub, sub, tq)
            m_prev = m_ref[h]
            m_cur = jnp.max(jnp.max(s3, axis=0), axis=0, keepdims=True)
            m_new = jnp.maximum(m_prev, m_cur)
            alpha = jnp.exp2(m_prev - m_new)
            p3 = jnp.exp2(s3 - m_new[None])
            l_ref[h] = alpha * l_ref[h] + jnp.sum(p3, axis=0)
            pv = jnp.dot(vt_ref[0, h * FOX_HEAD_DIM:(h + 1) * FOX_HEAD_DIM, :], p3.reshape(tk, tq).astype(BF16),
                         preferred_element_type=F32)
            acc = acc_ref[h].reshape(FOX_HEAD_DIM // sub, sub, tq) * alpha[None]
            acc_ref[h] = acc.reshape(FOX_HEAD_DIM, tq) + pv
            m_ref[h] = m_new

    @pl.when(j < i)
    def _():
        update(False)

    @pl.when(j == i)
    def _():
        update(True)
        outs = []
        for h in range(FOX_HEADS):
            denom = jnp.sum(l_ref[h], axis=0, keepdims=True)
            outs.append((acc_ref[h] / denom).astype(BF16))
        out_t = jnp.concatenate(outs, axis=0)
        ri = lax.broadcasted_iota(I32, (tq, tq), 0)
        ci = lax.broadcasted_iota(I32, (tq, tq), 1)
        eye = jnp.where(ri == ci, 1.0, 0.0).astype(BF16)
        o_ref[0] = lax.dot_general(eye, out_t, (((1,), (1,)), ((), ())),
                                   preferred_element_type=F32).astype(o_ref.dtype)


def _fox(q, k, vt):
    bsz, _, t_len, _ = q.shape
    tq = min(512, t_len)
    n_blk = t_len // tq
    pairs = [(i, j) for i in range(n_blk) for j in range(i + 1)]
    qi = jnp.asarray([p[0] for p in pairs], I32)
    kj = jnp.asarray([p[1] for p in pairs], I32)
    grid_spec = pltpu.PrefetchScalarGridSpec(
        num_scalar_prefetch=2,
        grid=(bsz, len(pairs)),
        in_specs=[pl.BlockSpec((1, FOX_HEADS, tq, LANES), lambda b, s, qi, kj: (b, 0, qi[s], 0)),
                  pl.BlockSpec((1, FOX_HEADS, tq, LANES), lambda b, s, qi, kj: (b, 0, kj[s], 0)),
                  pl.BlockSpec((1, FOX_WIDTH, tq), lambda b, s, qi, kj: (b, 0, kj[s]))],
        out_specs=pl.BlockSpec((1, tq, FOX_WIDTH), lambda b, s, qi, kj: (b, qi[s], 0)),
        scratch_shapes=[pltpu.VMEM((FOX_HEADS, 8, tq), F32), pltpu.VMEM((FOX_HEADS, 8, tq), F32),
                        pltpu.VMEM((FOX_HEADS, FOX_HEAD_DIM, tq), F32)],
    )
    return pl.pallas_call(
        _fox_kernel,
        out_shape=jax.ShapeDtypeStruct((bsz, t_len, FOX_WIDTH), BF16),
        grid_spec=grid_spec,
        compiler_params=_cparams(("parallel", "arbitrary")),
        name="fox",
    )(qi, kj, q, k, vt)


def _merge_kernel(zg_ref, ygm_ref, yrw_ref, yfox_ref, x_ref, g1_ref, sc2_ref, sh2_ref, pb_ref, wo_ref, wr_ref, br_ref,
                  x1_o, h2_o, idx_o, gate_o, rank_o, cnt_o, carry_ref):
    tm = x_ref.shape[1]

    @pl.when((pl.program_id(0) == 0) & (pl.program_id(1) == 0))
    def _():
        carry_ref[...] = jnp.zeros_like(carry_ref)

    sg = 0.5 * jnp.tanh(0.5 * zg_ref[0].astype(F32)) + 0.5
    p_gm = jnp.dot(ygm_ref[0], pb_ref[0:GM_WIDTH, :], preferred_element_type=F32)
    y_rw = jnp.concatenate([yrw_ref[0, h] for h in range(RW_HEADS)], axis=-1)
    p_rw = jnp.dot(y_rw, pb_ref[GM_WIDTH:GM_WIDTH + RW_WIDTH, :], preferred_element_type=F32)
    p_fox = jnp.dot(yfox_ref[0], pb_ref[GM_WIDTH + RW_WIDTH:, :], preferred_element_type=F32)
    merged = sg[:, 0:D_MODEL] * p_gm + sg[:, D_MODEL:2 * D_MODEL] * p_rw + sg[:, 2 * D_MODEL:] * p_fox
    x1 = x_ref[0] + g1_ref[0] * jnp.dot(merged.astype(BF16), wo_ref[...], preferred_element_type=F32)
    x1_o[0] = x1
    h2 = x1 * lax.rsqrt(jnp.mean(x1 * x1, axis=-1, keepdims=True) + EPS) * (1.0 + sc2_ref[0]) + sh2_ref[0]
    h2_o[0] = _pack_halves(h2)

    h_hi, h_lo, _ = _split3(h2)
    w_hi, w_lo, _ = _split3(wr_ref[...])
    logits = (jnp.dot(h_hi, w_hi, preferred_element_type=F32) + jnp.dot(h_hi, w_lo, preferred_element_type=F32)
              + jnp.dot(h_lo, w_hi, preferred_element_type=F32)) + br_ref[...]
    lane = lax.broadcasted_iota(I32, (tm, N_EXPERTS), 1)
    vals, idxs = [], []
    rest = logits
    for _ in range(TOP_K):
        m = jnp.max(rest, axis=-1, keepdims=True)
        am = jnp.min(jnp.where(rest == m, lane, N_EXPERTS), axis=-1, keepdims=True)
        vals.append(m)
        idxs.append(am)
        rest = jnp.where(lane == am, -jnp.inf, rest)
    exps = [jnp.exp(val - vals[0]) for val in vals]
    denom = exps[0] + exps[1] + exps[2] + exps[3]

    onehot = jnp.zeros((tm, N_EXPERTS), F32)
    for am in idxs:
        onehot = onehot + jnp.where(lane == am, 1.0, 0.0)
    ri = lax.broadcasted_iota(I32, (tm, tm), 0)
    ci = lax.broadcasted_iota(I32, (tm, tm), 1)
    before = jnp.where(ri > ci, 1.0, 0.0).astype(BF16)
    seen = carry_ref[...] + jnp.dot(before, onehot.astype(BF16), preferred_element_type=F32)
    lane_k = lax.broadcasted_iota(I32, (tm, TOP_K), 1)
    idx_out = jnp.zeros((tm, TOP_K), I32)
    gate_out = jnp.zeros((tm, TOP_K), F32)
    rank_out = jnp.zeros((tm, TOP_K), I32)
    for kk in range(TOP_K):
        rank = jnp.sum(jnp.where(lane == idxs[kk], seen, 0.0), axis=-1, keepdims=True).astype(I32)
        idx_out = jnp.where(lane_k == kk, idxs[kk], idx_out)
        gate_out = jnp.where(lane_k == kk, exps[kk] / denom, gate_out)
        rank_out = jnp.where(lane_k == kk, rank, rank_out)
    idx_o[0] = idx_out
    gate_o[0] = gate_out
    rank_o[0] = rank_out
    total = carry_ref[...] + jnp.sum(onehot, axis=0, keepdims=True)
    carry_ref[...] = total
    cnt_o[...] = total.astype(I32)


def _merge(z, y_gm, y_rw, y_fox, x, gate1, scale2, shift2, w_branch, w_o, w_router, b_router, layer):
    bsz, t_len, d = x.shape
    tm = min(512, t_len)
    row = lambda w: pl.BlockSpec((1, tm, w), lambda b, i: (b, i, 0))
    mod = pl.BlockSpec((1, 1, d), lambda b, i: (b, 0, 0))
    full = lambda shape: pl.BlockSpec(shape, lambda b, i: (0,) * len(shape))
    return pl.pallas_call(
        _merge_kernel,
        out_shape=(jax.ShapeDtypeStruct((bsz, t_len, d), F32), jax.ShapeDtypeStruct((bsz, t_len, d // 2), I32),
                   jax.ShapeDtypeStruct((bsz, t_len, TOP_K), I32), jax.ShapeDtypeStruct((bsz, t_len, TOP_K), F32),
                   jax.ShapeDtypeStruct((bsz, t_len, TOP_K), I32), jax.ShapeDtypeStruct((1, N_EXPERTS), I32)),
        grid=(bsz, t_len // tm),
        in_specs=[row(N_BRANCH * D_MODEL), row(GM_WIDTH),
                  pl.BlockSpec((1, RW_HEADS, tm, RW_HEAD_DIM), lambda b, i: (b, 0, i, 0)),
                  row(FOX_WIDTH), row(d), mod, mod, mod,
                  pl.BlockSpec((MIX_WIDTH, d), lambda b, i: (layer, 0)), pl.BlockSpec((d, d), lambda b, i: (layer, 0)),
                  full(w_router.shape), full((1, N_EXPERTS))],
        out_specs=(row(d), row(d // 2), row(TOP_K), row(TOP_K), row(TOP_K), full((1, N_EXPERTS))),
        scratch_shapes=[pltpu.VMEM((1, N_EXPERTS), F32)],
        compiler_params=_cparams(("arbitrary", "arbitrary")),
        name="merge_router",
    )(z, y_gm, y_rw, y_fox, x, gate1, scale2, shift2, w_branch, w_o, w_router, b_router.reshape(1, N_EXPERTS))


def _sc_mesh():
    return plsc.VectorSubcoreMesh(core_axis_name="c", subcore_axis_name="s",
                                  num_cores=SC_CORES, num_subcores=SC_SUBCORES)


def _sc_worker():
    return lax.axis_index("s") * SC_CORES + lax.axis_index("c")


def _sc_scatter_rows(src, idx3, n_out):
    _, d = src.shape
    n_copy, n_grp, _ = idx3.shape
    grp_per_w = n_grp // SC_WORKERS

    def body(src_hbm, idx_hbm, out_hbm, idx_v, rows_v):
        g0 = _sc_worker() * grp_per_w
        for q in range(n_copy):
            pltpu.sync_copy(idx_hbm.at[q, pl.ds(g0, grp_per_w)], idx_v.at[pl.ds(q * grp_per_w, grp_per_w)])

        @pl.loop(0, grp_per_w)
        def _(j):
            pltpu.sync_copy(src_hbm.at[pl.ds((g0 + j) * SC_ROWS, SC_ROWS)], rows_v)
            for q in range(n_copy):
                pltpu.sync_copy(rows_v, out_hbm.at[idx_v.at[q * grp_per_w + j]])

    return pl.kernel(
        body, out_type=jax.ShapeDtypeStruct((n_out, d), src.dtype), mesh=_sc_mesh(),
        scratch_types=[pltpu.VMEM((n_copy * grp_per_w, SC_ROWS), I32), pltpu.VMEM((SC_ROWS, d), src.dtype)],
        name="sc_dispatch",
    )(src, idx3)


def _sc_gather_rows(table, idx2):
    _, d = table.shape
    n_grp, _ = idx2.shape
    grp_per_w = n_grp // SC_WORKERS

    def body(table_hbm, idx_hbm, out_hbm, idx_v, rows_v):
        g0 = _sc_worker() * grp_per_w
        pltpu.sync_copy(idx_hbm.at[pl.ds(g0, grp_per_w)], idx_v)

        @pl.loop(0, grp_per_w)
        def _(j):
            pltpu.sync_copy(table_hbm.at[idx_v.at[j]], rows_v)
            pltpu.sync_copy(rows_v, out_hbm.at[pl.ds((g0 + j) * SC_ROWS, SC_ROWS)])

    return pl.kernel(
        body, out_type=jax.ShapeDtypeStruct((n_grp * SC_ROWS, d), table.dtype), mesh=_sc_mesh(),
        scratch_types=[pltpu.VMEM((grp_per_w, SC_ROWS), I32), pltpu.VMEM((SC_ROWS, d), table.dtype)],
        name="sc_combine_gather",
    )(table, idx2)


def _ffn_weight_copies(expert, wgu_hbm, wd_hbm, stage_gu, stage_d, sem):
    return (pltpu.make_async_copy(wgu_hbm.at[expert], stage_gu, sem.at[0]),
            pltpu.make_async_copy(wd_hbm.at[expert], stage_d, sem.at[1]))


def _ffn_kernel(be_ref, first_ref, nxt_ref, nv_ref, x_ref, wgu_hbm, wd_hbm, bgu_ref, bd_ref, o_ref,
                stage_gu, stage_d, wgu_b, wd_b, sem):
    step = pl.program_id(0)
    copies = functools.partial(_ffn_weight_copies, wgu_hbm=wgu_hbm, wd_hbm=wd_hbm, stage_gu=stage_gu,
                               stage_d=stage_d, sem=sem)

    @pl.when(step == 0)
    def _():
        for cp in copies(be_ref[0]):
            cp.start()

    for b in range(FFN_STEP_BLOCKS):
        idx = step * FFN_STEP_BLOCKS + b
        expert = be_ref[idx]
        n_valid = nv_ref[idx]
        rows = slice(b * MOE_BLOCK, (b + 1) * MOE_BLOCK)

        @pl.when(first_ref[idx] == 1)
        def _():
            for cp in copies(expert):
                cp.wait()
            wgu_b[...] = stage_gu[...].astype(BF16)
            wd_b[...] = stage_d[...].astype(BF16)

            @pl.when(nxt_ref[idx] >= 0)
            def _():
                for cp in copies(nxt_ref[idx]):
                    cp.start()

        @pl.when(n_valid > 0)
        def _():
            rowid = lax.broadcasted_iota(I32, (MOE_BLOCK, x_ref.shape[1]), 0)
            xp = jnp.where(rowid < n_valid, x_ref[rows, :], 0)
            x = jnp.concatenate(_unpack_halves(xp), axis=-1).astype(BF16)
            gu = jnp.dot(x, wgu_b[...], preferred_element_type=F32) + bgu_ref[expert]
            g_ = jnp.minimum(gu[:, :D_FF], SWIGLU_LIMIT)
            u_ = jnp.clip(gu[:, D_FF:], -SWIGLU_LIMIT, SWIGLU_LIMIT)
            act = (u_ + 1.0) * (g_ * jax.nn.sigmoid(SWIGLU_ALPHA * g_))
            y = jnp.dot(act.astype(BF16), wd_b[...], preferred_element_type=F32) + bd_ref[expert]
            o_ref[rows, :] = _pack_halves(y)

        @pl.when(n_valid <= 0)
        def _():
            o_ref[rows, :] = jnp.zeros((MOE_BLOCK, o_ref.shape[1]), o_ref.dtype)


def _ffn(block_expert, block_first, block_next, block_valid, xin, w_gate_up, b_gate_up, w_down, b_down):
    n_rows, dp = xin.shape
    d = 2 * dp
    step_rows = FFN_STEP_BLOCKS * MOE_BLOCK
    resident = lambda arr: pl.BlockSpec(arr.shape, lambda i, *_: (0,) * arr.ndim)
    grid_spec = pltpu.PrefetchScalarGridSpec(
        num_scalar_prefetch=4,
        grid=(n_rows // step_rows,),
        in_specs=[pl.BlockSpec((step_rows, dp), lambda i, *_: (i, 0)),
                  pl.BlockSpec(memory_space=pl.ANY), pl.BlockSpec(memory_space=pl.ANY),
                  resident(b_gate_up), resident(b_down)],
        out_specs=pl.BlockSpec((step_rows, dp), lambda i, *_: (i, 0)),
        scratch_shapes=[pltpu.VMEM((d, 2 * D_FF), F32), pltpu.VMEM((D_FF, d), F32),
                        pltpu.VMEM((d, 2 * D_FF), BF16), pltpu.VMEM((D_FF, d), BF16),
                        pltpu.SemaphoreType.DMA((2,))],
    )
    return pl.pallas_call(
        _ffn_kernel,
        out_shape=jax.ShapeDtypeStruct((n_rows, dp), I32),
        grid_spec=grid_spec,
        compiler_params=pltpu.CompilerParams(dimension_semantics=("arbitrary",), vmem_limit_bytes=FFN_VMEM_LIMIT),
        name="expert_ffn",
    )(block_expert, block_first, block_next, block_valid, xin, w_gate_up, w_down, b_gate_up, b_down)


def _combine_kernel(x1_ref, g2_ref, gate_ref, yg_ref, o_ref):
    gate = gate_ref[0]
    y_lo = y_hi = None
    for q in range(TOP_K):
        lo, hi = _unpack_halves(yg_ref[q, 0])
        wq = gate[:, q:q + 1]
        y_lo = wq * lo if y_lo is None else y_lo + wq * lo
        y_hi = wq * hi if y_hi is None else y_hi + wq * hi
    o_ref[0] = x1_ref[0] + g2_ref[0] * jnp.concatenate([y_lo, y_hi], axis=-1)


def _combine(x1, gate2, gate, yg):
    bsz, t_len, d = x1.shape
    tm = min(512, t_len)
    return pl.pallas_call(
        _combine_kernel,
        out_shape=jax.ShapeDtypeStruct((bsz, t_len, d), F32),
        grid=(bsz, t_len // tm),
        in_specs=[pl.BlockSpec((1, tm, d), lambda b, i: (b, i, 0)),
                  pl.BlockSpec((1, 1, d), lambda b, i: (b, 0, 0)),
                  pl.BlockSpec((1, tm, TOP_K), lambda b, i: (b, i, 0)),
                  pl.BlockSpec((TOP_K, 1, tm, d // 2), lambda b, i: (0, b, i, 0))],
        out_specs=pl.BlockSpec((1, tm, d), lambda b, i: (b, i, 0)),
        compiler_params=_cparams(("parallel", "parallel")),
        name="moe_combine",
    )(x1, gate2, gate, yg)


def _moe(x1, gate2, h2, top_idx, gate, rank, counts, w_gate_up, b_gate_up, w_down, b_down, layer):
    bsz, t_len, d = h2.shape
    n_tok = bsz * t_len
    n_assign = n_tok * TOP_K
    n_blocks = -(-n_assign // MOE_BLOCK) + N_EXPERTS
    counts = counts.reshape(N_EXPERTS)
    blocks_e = (counts + MOE_BLOCK - 1) // MOE_BLOCK
    blk_end = jnp.cumsum(blocks_e)
    blk_start = blk_end - blocks_e
    experts = jnp.arange(N_EXPERTS, dtype=I32)
    onehot = top_idx.reshape(n_tok, TOP_K, 1) == experts
    dest = jnp.sum(jnp.where(onehot, blk_start * MOE_BLOCK, 0), axis=-1) + rank.reshape(n_tok, TOP_K)
    dest_t = dest.T.astype(I32)
    blk = jnp.arange(n_blocks, dtype=I32)
    block_expert = jnp.minimum(jnp.sum(blk_end[None, :] <= blk[:, None], axis=1), N_EXPERTS - 1).astype(I32)
    be_hot = block_expert[:, None] == experts
    cnt_b = jnp.sum(jnp.where(be_hot, counts, 0), axis=1)
    start_b = jnp.sum(jnp.where(be_hot, blk_start, 0), axis=1)
    block_valid = jnp.clip(cnt_b - (blk - start_b) * MOE_BLOCK, 0, MOE_BLOCK).astype(I32)
    xin = _sc_scatter_rows(h2.reshape(n_tok, d), dest_t.reshape(TOP_K, n_tok // SC_ROWS, SC_ROWS),
                           n_blocks * MOE_BLOCK)
    block_first = jnp.concatenate([jnp.ones((1,), I32), (block_expert[1:] != block_expert[:-1]).astype(I32)])
    run_start = jnp.where(block_first == 1, blk, n_blocks)
    later_start = lax.cummin(jnp.concatenate([run_start[1:], jnp.full((1,), n_blocks, I32)]), reverse=True)
    next_expert = jnp.concatenate([block_expert, jnp.full((1,), -1 - layer * N_EXPERTS, I32)])[later_start]
    yb = _ffn(block_expert + layer * N_EXPERTS, block_first, next_expert + layer * N_EXPERTS, block_valid, xin,
              w_gate_up, b_gate_up, w_down, b_down)
    yg = _sc_gather_rows(yb, dest_t.reshape(n_assign // SC_ROWS, SC_ROWS))
    return _combine(x1, gate2, gate, yg.reshape(TOP_K, bsz, t_len, d))


def _permute_kernel(w_ref, o_ref):
    o_gm = 0
    o_rw = o_gm + 2 * GM_WIDTH
    o_fox = o_rw + RW_SHIFT_WIDTH
    o_f = o_fox + 3 * FOX_WIDTH
    o_gate = o_f + FOX_HEADS
    w = w_ref[0]
    o_ref[0, :, Z_GATE:Z_FOX] = w[:, o_gate:o_gate + N_BRANCH * D_MODEL].astype(BF16)
    o_ref[0, :, Z_FOX:Z_GM] = w[:, o_fox:o_f].astype(BF16)
    o_ref[0, :, Z_GM:Z_RW] = w[:, o_gm:o_rw].astype(BF16)
    o_ref[0, :, Z_RW:Z_F] = w[:, o_rw:o_fox].astype(BF16)
    tail = jnp.concatenate([w[:, o_f:o_gate], jnp.zeros((w.shape[0], Z_WIDTH - Z_F - FOX_HEADS), F32)], axis=-1)
    o_ref[0, :, Z_F:Z_WIDTH] = tail.astype(BF16)


def _permute_w_in(w_in):
    n_layer, d, w_cols = w_in.shape
    tr = 256
    return pl.pallas_call(
        _permute_kernel,
        out_shape=jax.ShapeDtypeStruct((n_layer, d, Z_WIDTH), BF16),
        grid=(n_layer, d // tr),
        in_specs=[pl.BlockSpec((1, tr, w_cols), lambda l, i: (l, i, 0))],
        out_specs=pl.BlockSpec((1, tr, Z_WIDTH), lambda l, i: (l, i, 0)),
        compiler_params=_cparams(("parallel", "parallel")),
        name="permute_w_in",
    )(w_in)


def _layer(x, mod, w_in_p, gm_v_gain, gm_w_s, gm_b_s, mu_pad, w_lora, rw_w0, rw_a0, rw_k_k, rw_k_a, rw_r_k,
           rw_gn_gain, rw_gn_bias, f_bias_pad, fox_q_gain, fox_k_gain, w_branch, w_o, w_router, b_router,
           w_gate_up, b_gate_up, w_down, b_down, layer):
    shift1, scale1, gate1, shift2, scale2, gate2 = (mod[:, i][:, None, :] for i in range(6))
    z, z_rw = _inproj(x, scale1, shift1, w_in_p, layer)
    y_gm = _gmlp(z, gm_v_gain, gm_w_s, gm_b_s)
    r, lw, k, v, a, b, g = _rwprep(z_rw, mu_pad, w_lora, rw_w0, rw_a0, rw_k_k, rw_k_a)
    y_rw = _rwscan(r, lw, k, v, a, b, g, rw_r_k, rw_gn_gain, rw_gn_bias)
    q, kf, vf = _foxprep(z, z_rw, f_bias_pad, fox_q_gain, fox_k_gain)
    y_fox = _fox(q, kf, vf)
    x1, h2, top_idx, gate, rank, counts = _merge(z, y_gm, y_rw, y_fox, x, gate1, scale2, shift2,
                                                 w_branch, w_o, w_router, b_router, layer)
    return _moe(x1, gate2, h2, top_idx, gate, rank, counts, w_gate_up, b_gate_up, w_down, b_down, layer)


def kernel(x, c, w_ada, b_ada, w_in, gm_v_gain, gm_w_s, gm_b_s, rw_mu, rw_w0, rw_w2, rw_a0, rw_a2, rw_g2, rw_k_k,
           rw_k_a, rw_r_k, rw_gn_gain, rw_gn_bias, fox_f_bias, fox_q_gain, fox_k_gain, w_branch, w_o, w_router,
           b_router, w_gate_up, b_gate_up, w_down, b_down):
    n_layer = w_ada.shape[0]
    bsz = x.shape[0]
    c_pad = jnp.zeros((8, D_MODEL), F32).at[:bsz].set(c)
    mod = _adaln(c_pad, w_ada, b_ada)[:, :bsz].reshape(n_layer, bsz, 6, D_MODEL)
    w_in_p = _permute_w_in(w_in)
    mu_pad = jnp.pad(rw_mu, ((0, 0), (0, RW_BLOCK - RW_SHIFT_WIDTH)))
    w_lora = jnp.zeros((n_layer, RW_LORA, 3 * RW_WIDTH), F32)
    w_lora = w_lora.at[:, 0:RW_DECAY_LORA, 0:RW_WIDTH].set(rw_w2)
    w_lora = w_lora.at[:, RW_DECAY_LORA:RW_DECAY_LORA + RW_ICLR_LORA, RW_WIDTH:2 * RW_WIDTH].set(rw_a2)
    w_lora = w_lora.at[:, RW_DECAY_LORA + RW_ICLR_LORA:, 2 * RW_WIDTH:].set(rw_g2)
    f_bias_pad = jnp.pad(fox_f_bias, ((0, 0), (0, LANES - FOX_HEADS)))
    w_in_p = w_in_p.reshape(n_layer * D_MODEL, Z_WIDTH)
    w_branch_b = w_branch.astype(BF16).reshape(n_layer * MIX_WIDTH, D_MODEL)
    w_o_b = w_o.astype(BF16).reshape(n_layer * D_MODEL, D_MODEL)
    w_gu = w_gate_up.reshape(n_layer * N_EXPERTS, D_MODEL, 2 * D_FF)
    b_gu = b_gate_up.reshape(n_layer * N_EXPERTS, 1, 2 * D_FF)
    w_dn = w_down.reshape(n_layer * N_EXPERTS, D_FF, D_MODEL)
    b_dn = b_down.reshape(n_layer * N_EXPERTS, 1, D_MODEL)
    for l in range(n_layer):
        x = _layer(x, mod[l], w_in_p, gm_v_gain[l], gm_w_s[l], gm_b_s[l], mu_pad[l:l + 1], w_lora[l], rw_w0[l],
                   rw_a0[l], rw_k_k[l], rw_k_a[l], rw_r_k[l], rw_gn_gain[l], rw_gn_bias[l], f_bias_pad[l:l + 1],
                   fox_q_gain[l], fox_k_gain[l], w_branch_b, w_o_b, w_router[l], b_router[l],
                   w_gu, b_gu, w_dn, b_dn, l)
    return x
```

```python
import functools

import jax
import jax.numpy as jnp
import numpy as np
from jax import lax
from jax.experimental import pallas as pl
from jax.experimental.pallas import tpu as pltpu
from jax.experimental.pallas import tpu_sc as plsc

F32 = jnp.float32
BF16 = jnp.bfloat16
I32 = jnp.int32
HIGHEST = lax.Precision.HIGHEST

D_MODEL = 1024
GM_CHUNK = 128
GM_GROUPS = 4
GM_WIDTH = 256
GM_GROUP_DIM = GM_WIDTH // GM_GROUPS
RW_HEADS = 4
RW_HEAD_DIM = 64
RW_WIDTH = RW_HEADS * RW_HEAD_DIM
RW_DECAY_LORA = 32
RW_ICLR_LORA = 32
RW_GATE_LORA = 64
RW_LORA = RW_DECAY_LORA + RW_ICLR_LORA + RW_GATE_LORA
RW_SHIFT_WIDTH = 3 * RW_WIDTH + RW_LORA
RW_GN_EPS = 64e-5
FOX_HEADS = 8
FOX_HEAD_DIM = 64
FOX_WIDTH = FOX_HEADS * FOX_HEAD_DIM
ATTN_SCALE = FOX_HEAD_DIM ** -0.5
MASK_VALUE = -1e30
LOG2E = 1.4426950408889634
N_BRANCH = 3
MIX_WIDTH = GM_WIDTH + RW_WIDTH + FOX_WIDTH
N_EXPERTS = 32
TOP_K = 4
D_FF = D_MODEL
SWIGLU_LIMIT = 7.0
SWIGLU_ALPHA = 1.702
MOE_BLOCK = 256
EPS = 1e-6

Z_GATE = 0
Z_FOX = N_BRANCH * D_MODEL
Z_GM = Z_FOX + 3 * FOX_WIDTH
Z_RW = Z_GM + 2 * GM_WIDTH
RW_BLOCK = 1024
Z_F = Z_RW + RW_SHIFT_WIDTH
Z_WIDTH = Z_RW + RW_BLOCK
LANES = 128
RW_CHUNK = 64
RW_PREP_UNROLL = 4

VMEM_LIMIT = 48 * 1024 * 1024
FFN_VMEM_LIMIT = 56 * 1024 * 1024
N_STREAMS = 2
FFN_STEP_BLOCKS = 4
SC_CORES = 2
SC_SUBCORES = 16
SC_WORKERS = SC_CORES * SC_SUBCORES
SC_ROWS = 64


def _cparams(sem):
    return pltpu.CompilerParams(dimension_semantics=sem, vmem_limit_bytes=VMEM_LIMIT)


def _mm(a, b):
    return jnp.dot(a.astype(BF16), b.astype(BF16), preferred_element_type=F32)


def _mm_nt(a, b):
    return lax.dot_general(a.astype(BF16), b.astype(BF16), (((1,), (1,)), ((), ())), preferred_element_type=F32)


def _mm_tn(a, b):
    return lax.dot_general(a.astype(BF16), b.astype(BF16), (((0,), (0,)), ((), ())), preferred_element_type=F32)


def _split3(x):
    hi = x.astype(BF16)
    r1 = x - hi.astype(F32)
    mid = r1.astype(BF16)
    lo = (r1 - mid.astype(F32)).astype(BF16)
    return hi, mid, lo


def _tri_cumsum(x, n):
    ri = lax.broadcasted_iota(I32, (n, n), 0)
    ci = lax.broadcasted_iota(I32, (n, n), 1)
    ones = jnp.where(ri >= ci, 1.0, 0.0).astype(BF16)
    hi, mid, lo = _split3(x)
    return (jnp.dot(ones, hi, preferred_element_type=F32) + jnp.dot(ones, mid, preferred_element_type=F32)
            + jnp.dot(ones, lo, preferred_element_type=F32))


def _pack_halves(x):
    w = x.shape[1] // 2
    hi = pltpu.bitcast(x[:, :w].astype(BF16).astype(F32), jnp.uint32)
    lo = pltpu.bitcast(x[:, w:].astype(BF16).astype(F32), jnp.uint32)
    return pltpu.bitcast(hi | (lo >> 16), I32)


def _unpack_halves(p):
    u = pltpu.bitcast(p, jnp.uint32)
    return pltpu.bitcast(u & jnp.uint32(0xFFFF0000), F32), pltpu.bitcast(u << 16, F32)


def _log_sigmoid(x):
    return jnp.minimum(x, 0.0) - jnp.log1p(jnp.exp(-jnp.abs(x)))


def _adaln_kernel(c_ref, w_ref, b_ref, o_ref):
    c = c_ref[...]
    s = c * jax.nn.sigmoid(c)
    o_ref[0] = jnp.dot(s, w_ref[0], preferred_element_type=F32, precision=HIGHEST) + b_ref[0]


def _adaln(c_pad, w_ada, b_ada):
    n_layer, d, w6 = w_ada.shape
    tn = 1536
    return pl.pallas_call(
        _adaln_kernel,
        out_shape=jax.ShapeDtypeStruct((n_layer, c_pad.shape[0], w6), F32),
        grid=(n_layer, w6 // tn),
        in_specs=[pl.BlockSpec(c_pad.shape, lambda l, j: (0, 0)),
                  pl.BlockSpec((1, d, tn), lambda l, j: (l, 0, j)),
                  pl.BlockSpec((1, 1, tn), lambda l, j: (l, 0, j))],
        out_specs=pl.BlockSpec((1, c_pad.shape[0], tn), lambda l, j: (l, 0, j)),
        compiler_params=_cparams(("parallel", "parallel")),
        name="adaln",
    )(c_pad, w_ada, b_ada.reshape(n_layer, 1, w6))


def _inproj_kernel(x_ref, sc_ref, sh_ref, w_ref, zm_ref, zr_ref, xn_ref):
    j = pl.program_id(2)

    @pl.when(j == 0)
    def _():
        x = x_ref[0]
        xn = x * lax.rsqrt(jnp.mean(x * x, axis=-1, keepdims=True) + EPS)
        xn_ref[...] = (xn * (1.0 + sc_ref[0]) + sh_ref[0]).astype(BF16)

    acc = jnp.dot(xn_ref[...], w_ref[...], preferred_element_type=F32)

    @pl.when(j < Z_RW // RW_BLOCK)
    def _():
        zm_ref[0] = acc.astype(BF16)

    @pl.when(j == Z_RW // RW_BLOCK)
    def _():
        zr_ref[0] = acc


def _inproj(x, scale, shift, w, layer):
    bsz, t_len, d = x.shape
    tm = min(1024, t_len)
    tn = RW_BLOCK
    n_main = Z_RW // tn
    return pl.pallas_call(
        _inproj_kernel,
        out_shape=(jax.ShapeDtypeStruct((bsz, t_len, Z_RW), BF16), jax.ShapeDtypeStruct((bsz, t_len, RW_BLOCK), F32)),
        grid=(bsz, t_len // tm, Z_WIDTH // tn),
        in_specs=[pl.BlockSpec((1, tm, d), lambda b, i, j: (b, i, 0)),
                  pl.BlockSpec((1, 1, d), lambda b, i, j: (b, 0, 0)),
                  pl.BlockSpec((1, 1, d), lambda b, i, j: (b, 0, 0)),
                  pl.BlockSpec((d, tn), lambda b, i, j: (layer, j))],
        out_specs=(pl.BlockSpec((1, tm, tn), lambda b, i, j: (b, i, jnp.minimum(j, n_main - 1))),
                   pl.BlockSpec((1, tm, tn), lambda b, i, j: (b, i, 0))),
        scratch_shapes=[pltpu.VMEM((tm, d), BF16)],
        compiler_params=_cparams(("parallel", "parallel", "arbitrary")),
        name="inproj",
    )(x, scale, shift, w)


def _gmlp_kernel(z_ref, gain_ref, ws_ref, bst_ref, o_ref):
    tm = z_ref.shape[1]
    z = z_ref[0].astype(F32)
    u = jax.nn.gelu(z[:, :GM_WIDTH])
    v = jax.nn.gelu(z[:, GM_WIDTH:])
    v = v * lax.rsqrt(jnp.mean(v * v, axis=-1, keepdims=True) + EPS) * gain_ref[...]
    vb = v.astype(BF16)
    grp = lax.broadcasted_iota(I32, (GM_CHUNK, GM_WIDTH), 1) // GM_GROUP_DIM
    ri = lax.broadcasted_iota(I32, (GM_CHUNK, GM_CHUNK), 0)
    ci = lax.broadcasted_iota(I32, (GM_CHUNK, GM_CHUNK), 1)
    causal = ri >= ci
    bias = jnp.zeros((GM_CHUNK, GM_WIDTH), F32)
    ws = []
    for g in range(GM_GROUPS):
        ws.append(jnp.where(causal, ws_ref[g], 0.0).astype(BF16))
        bias = jnp.where(grp == g, bst_ref[:, g:g + 1], bias)
    for c in range(tm // GM_CHUNK):
        rows = slice(c * GM_CHUNK, (c + 1) * GM_CHUNK)
        vc = vb[rows]
        mixed = bias
        for g in range(GM_GROUPS):
            m = jnp.dot(ws[g], vc, preferred_element_type=F32)
            mixed = mixed + jnp.where(grp == g, m, 0.0)
        o_ref[0, rows, :] = (u[rows] * mixed).astype(o_ref.dtype)


def _gmlp(z, gain, w_s, b_s):
    bsz, t_len, _ = z.shape
    tm = min(512, t_len)
    return pl.pallas_call(
        _gmlp_kernel,
        out_shape=jax.ShapeDtypeStruct((bsz, t_len, GM_WIDTH), BF16),
        grid=(bsz, t_len // tm),
        in_specs=[pl.BlockSpec((1, tm, 2 * GM_WIDTH), lambda b, i: (b, i, Z_GM // (2 * GM_WIDTH))),
                  pl.BlockSpec((1, GM_WIDTH), lambda b, i: (0, 0)),
                  pl.BlockSpec((GM_GROUPS, GM_CHUNK, GM_CHUNK), lambda b, i: (0, 0, 0)),
                  pl.BlockSpec((GM_CHUNK, GM_GROUPS), lambda b, i: (0, 0))],
        out_specs=pl.BlockSpec((1, tm, GM_WIDTH), lambda b, i: (b, i, 0)),
        compiler_params=_cparams(("parallel", "parallel")),
        name="gmlp",
    )(z, gain.reshape(1, GM_WIDTH), w_s, b_s.T)


def _rwprep_kernel(z_ref, zp_ref, mu_ref, wl_ref, w0_ref, a0_ref, kk_ref, ka_ref,
                   r_o, lw_o, k_o, v_o, a_o, b_o, g_o):
    tm = z_ref.shape[1]
    z = z_ref[0]
    prev = jnp.where(pl.program_id(1) > 0, zp_ref[0, 7:8, :], 0.0)
    rowid = lax.broadcasted_iota(I32, z.shape, 0)
    zs = jnp.where(rowid == 0, prev, pltpu.roll(z, 1, axis=0))
    zz = z + mu_ref[...] * (zs - z)
    r = zz[:, 0:RW_WIDTH]
    k = zz[:, RW_WIDTH:2 * RW_WIDTH]
    v = zz[:, 2 * RW_WIDTH:3 * RW_WIDTH]
    lo = zz[:, 3 * RW_WIDTH:3 * RW_WIDTH + RW_LORA]
    lane = lax.broadcasted_iota(I32, (tm, RW_LORA), 1)
    act = jnp.where(lane < RW_DECAY_LORA, jnp.tanh(lo),
                    jnp.where(lane < RW_DECAY_LORA + RW_ICLR_LORA, lo, jax.nn.sigmoid(lo)))
    proj = jnp.dot(act, wl_ref[...], preferred_element_type=F32, precision=HIGHEST)
    xw = -(w0_ref[...] + proj[:, 0:RW_WIDTH])
    softplus = jnp.maximum(xw, 0.0) + jnp.log1p(jnp.exp(-jnp.abs(xw)))
    lw = -jnp.exp(-softplus - 0.5)
    a = jax.nn.sigmoid(a0_ref[...] + proj[:, RW_WIDTH:2 * RW_WIDTH])
    g = proj[:, 2 * RW_WIDTH:3 * RW_WIDTH]
    kk = k * kk_ref[...]
    k2 = k * (1.0 + (a - 1.0) * ka_ref[...])
    for h in range(RW_HEADS):
        sl = slice(h * RW_HEAD_DIM, (h + 1) * RW_HEAD_DIM)
        kkh = kk[:, sl]
        nrm = jnp.sqrt(jnp.sum(kkh * kkh, axis=-1, keepdims=True))
        kkh = kkh / jnp.maximum(nrm, 1e-12)
        r_o[0, h] = r[:, sl]
        lw_o[0, h] = lw[:, sl]
        k_o[0, h] = k2[:, sl]
        v_o[0, h] = v[:, sl]
        a_o[0, h] = -kkh
        b_o[0, h] = kkh * a[:, sl]
        g_o[0, h] = g[:, sl]


def _rwprep(z, mu_pad, w_lora, w0, a0, k_k, k_a):
    bsz, t_len, _ = z.shape
    tm = min(512, t_len)
    hm = jax.ShapeDtypeStruct((bsz, RW_HEADS, t_len, RW_HEAD_DIM), F32)
    hm_spec = pl.BlockSpec((1, RW_HEADS, tm, RW_HEAD_DIM), lambda b, i: (b, 0, i, 0))
    vec = lambda n: pl.BlockSpec((1, n), lambda b, i: (0, 0))
    rw_blk = 0
    return pl.pallas_call(
        _rwprep_kernel,
        out_shape=(hm,) * 7,
        grid=(bsz, t_len // tm),
        in_specs=[pl.BlockSpec((1, tm, RW_BLOCK), lambda b, i: (b, i, rw_blk)),
                  pl.BlockSpec((1, 8, RW_BLOCK), lambda b, i: (b, jnp.maximum(i * (tm // 8) - 1, 0), rw_blk)),
                  vec(RW_BLOCK),
                  pl.BlockSpec((RW_LORA, 3 * RW_WIDTH), lambda b, i: (0, 0)),
                  vec(RW_WIDTH), vec(RW_WIDTH), vec(RW_WIDTH), vec(RW_WIDTH)],
        out_specs=(hm_spec,) * 7,
        compiler_params=_cparams(("parallel", "parallel")),
        name="rwprep",
    )(z, z, mu_pad, w_lora, w0.reshape(1, -1), a0.reshape(1, -1), k_k.reshape(1, -1), k_a.reshape(1, -1))


def _rwscan_kernel(r_ref, lw_ref, k_ref, v_ref, a_ref, b_ref, g_ref, rk_ref, gg_ref, gb_ref, o_ref,
                   s_ref, rp_ref, y_ref, gm_ref, h0_ref, we_ref):
    cl = RW_CHUNK
    tb = r_ref.shape[2]
    n_chunk = tb // cl

    @pl.when(pl.program_id(1) == 0)
    def _():
        s_ref[...] = jnp.zeros_like(s_ref)

    n = RW_HEADS * cl
    ri = lax.broadcasted_iota(I32, (n, n), 0)
    ci = lax.broadcasted_iota(I32, (n, n), 1)
    same_head = (ri // cl) == (ci // cl)
    lower = same_head & (ri >= ci)
    strict = same_head & (ri > ci)
    eye = jnp.where(ri == ci, 1.0, 0.0)
    ones_lower = jnp.where(lower, 1.0, 0.0).astype(BF16)

    def prepare(chunks):
        grp = range(len(chunks))
        each = lambda fn: [fn(u) for u in grp]
        rows = [pl.ds(pl.multiple_of(c * cl, cl), cl) for c in chunks]
        stack = lambda ref: each(lambda u: ref[0, :, rows[u], :].reshape(n, RW_HEAD_DIM))
        r, lw, k, v, a, b = (stack(ref) for ref in (r_ref, lw_ref, k_ref, v_ref, a_ref, b_ref))
        hd = RW_HEAD_DIM
        parts = each(lambda u: jnp.concatenate(_split3(lw[u]), axis=-1))
        sums = each(lambda u: jnp.dot(ones_lower, parts[u], preferred_element_type=F32))
        cw = each(lambda u: sums[u][:, :hd] + sums[u][:, hd:2 * hd] + sums[u][:, 2 * hd:])
        w_in = each(lambda u: jnp.exp(cw[u]))
        w_inv = each(lambda u: jnp.exp(-cw[u]))
        rt = each(lambda u: r[u] * w_in[u])
        at = each(lambda u: a[u] * jnp.exp(cw[u] - lw[u]))
        kt = each(lambda u: k[u] * w_inv[u])
        bt = each(lambda u: b[u] * w_inv[u])
        w_end = each(lambda u: w_in[u].reshape(RW_HEADS, cl, RW_HEAD_DIM)[:, cl - 1:cl, :])
        w_end_rows = each(lambda u: jnp.broadcast_to(w_end[u], (RW_HEADS, cl, RW_HEAD_DIM)).reshape(n, RW_HEAD_DIM))
        a_ab = each(lambda u: jnp.where(strict, _mm_nt(at[u], bt[u]), 0.0))
        a_ak = each(lambda u: jnp.where(strict, _mm_nt(at[u], kt[u]), 0.0))
        m_rb = each(lambda u: jnp.where(lower, _mm_nt(rt[u], bt[u]), 0.0))
        m_rk = each(lambda u: jnp.where(lower, _mm_nt(rt[u], kt[u]), 0.0))
        inv = each(lambda u: eye + a_ab[u])
        p = a_ab
        for _ in range(cl.bit_length() - 2):
            p = [_mm(p[u], p[u]) for u in grp]
            inv = [inv[u] + _mm(inv[u], p[u]) for u in grp]
        akv = each(lambda u: _mm(a_ak[u], v[u]))
        apz = each(lambda u: _mm(inv[u], jnp.concatenate([at[u], akv[u]], axis=-1)).astype(BF16))
        mix = each(lambda u: jnp.dot(m_rb[u].astype(BF16), apz[u], preferred_element_type=F32))
        bend = each(lambda u: bt[u] * w_end_rows[u])
        kend = each(lambda u: kt[u] * w_end_rows[u])
        rp = each(lambda u: (rt[u] + mix[u][:, :hd]).astype(BF16))
        y0 = each(lambda u: mix[u][:, hd:] + _mm(m_rk[u], v[u]))
        for u in grp:
            for h in range(RW_HEADS):
                hs = slice(h * cl, (h + 1) * cl)
                both = _mm_tn(apz[u][hs], bend[u][hs])
                rp_ref[h, rows[u], :] = rp[u][hs]
                y_ref[h, rows[u], :] = y0[u][hs]
                gm_ref[h, rows[u], :] = both[:hd].astype(BF16)
                h0_ref[h, rows[u], :] = both[hd:] + _mm_tn(v[u][hs], kend[u][hs])
                we_ref[h, chunks[u]] = w_end[u][h]

    def prepare_step(i, carry):
        prepare([i * RW_PREP_UNROLL + u for u in range(RW_PREP_UNROLL)])
        return carry

    lax.fori_loop(0, n_chunk // RW_PREP_UNROLL, prepare_step, 0)

    def advance(c, carry):
        rows = pl.ds(pl.multiple_of(c * cl, cl), cl)
        for h in range(RW_HEADS):
            s = s_ref[h]
            sb = s.astype(BF16)
            y_ref[h, rows, :] = y_ref[h, rows, :] + lax.dot_general(
                rp_ref[h, rows, :], sb, (((1,), (1,)), ((), ())), preferred_element_type=F32)
            s_ref[h] = (s * we_ref[h, c] + jnp.dot(sb, gm_ref[h, rows, :], preferred_element_type=F32)
                        + h0_ref[h, rows, :])
        return carry

    lax.fori_loop(0, n_chunk, advance, 0)

    for h in range(RW_HEADS):
        y = y_ref[h]
        mu = jnp.mean(y, axis=-1, keepdims=True)
        yc = y - mu
        var = jnp.mean(yc * yc, axis=-1, keepdims=True)
        yn = yc * lax.rsqrt(var + RW_GN_EPS) * gg_ref[h] + gb_ref[h]
        v = v_ref[0, h]
        bonus = jnp.sum(r_ref[0, h] * k_ref[0, h] * rk_ref[h], axis=-1, keepdims=True) * v
        o_ref[0, h] = ((yn + bonus) * g_ref[0, h]).astype(o_ref.dtype)


def _rwscan(r, lw, k, v, a, b, g, r_k, gn_gain, gn_bias):
    bsz, _, t_len, _ = r.shape
    tb = min(512, t_len)
    hm_spec = pl.BlockSpec((1, RW_HEADS, tb, RW_HEAD_DIM), lambda bi, i: (bi, 0, i, 0))
    par = pl.BlockSpec((RW_HEADS, 1, RW_HEAD_DIM), lambda bi, i: (0, 0, 0))
    hshape = (RW_HEADS, 1, RW_HEAD_DIM)
    return pl.pallas_call(
        _rwscan_kernel,
        out_shape=jax.ShapeDtypeStruct((bsz, RW_HEADS, t_len, RW_HEAD_DIM), BF16),
        grid=(bsz, t_len // tb),
        in_specs=[hm_spec] * 7 + [par] * 3,
        out_specs=hm_spec,
        scratch_shapes=[pltpu.VMEM((RW_HEADS, RW_HEAD_DIM, RW_HEAD_DIM), F32),
                        pltpu.VMEM((RW_HEADS, tb, RW_HEAD_DIM), BF16), pltpu.VMEM((RW_HEADS, tb, RW_HEAD_DIM), F32),
                        pltpu.VMEM((RW_HEADS, tb, RW_HEAD_DIM), BF16), pltpu.VMEM((RW_HEADS, tb, RW_HEAD_DIM), F32),
                        pltpu.VMEM((RW_HEADS, tb // RW_CHUNK, 1, RW_HEAD_DIM), F32)],
        compiler_params=_cparams(("parallel", "arbitrary")),
        name="rwscan",
    )(r, lw, k, v, a, b, g, r_k.reshape(hshape), gn_gain.reshape(hshape), gn_bias.reshape(hshape))


FOX_PAIRS = FOX_HEADS // 2
FOX_EXTRA = 3
FOX_ACC_ROWS = FOX_HEAD_DIM + 16
FOX_LOOKAHEAD = 2


def _fox_bias_selector():
    sel = np.zeros((LANES, 2 * FOX_HEADS * LANES), np.float32)
    for h in range(FOX_HEADS):
        base = FOX_HEAD_DIM if h % 2 == 0 else 0
        for p in range(FOX_EXTRA):
            sel[p * FOX_HEADS + h, h * LANES + base + p] = 1.0
            sel[p * FOX_HEADS + h, (FOX_HEADS + h) * LANES + base + FOX_EXTRA + p] = -1.0
    return sel


def _foxprep_kernel(z_ref, f_ref, fb_ref, qg_ref, kg_ref, sel_ref, q_o, k_o, vt_o, carry_ref):
    tm = z_ref.shape[1]

    @pl.when(pl.program_id(1) == 0)
    def _():
        carry_ref[...] = jnp.zeros_like(carry_ref)

    log_f = _log_sigmoid(f_ref[0] + fb_ref[...])
    cum = carry_ref[...] + _tri_cumsum(log_f, tm)
    carry_ref[...] = cum[tm - 1:tm, :]
    lane = lax.broadcasted_iota(I32, (tm, LANES), 1)
    hi, mid, lo = (p.astype(F32) for p in _split3(cum * LOG2E))
    packed = jnp.where(lane < FOX_HEADS, hi,
                       jnp.where(lane < 2 * FOX_HEADS, pltpu.roll(mid, FOX_HEADS, axis=1),
                                 pltpu.roll(lo, 2 * FOX_HEADS, axis=1)))
    packed = jnp.where(lane < FOX_EXTRA * FOX_HEADS, packed, 0.0).astype(BF16)
    extra = jnp.dot(packed, sel_ref[...], preferred_element_type=F32)

    left = lane < FOX_HEAD_DIM
    in_half = lane % FOX_HEAD_DIM
    ones_q = jnp.where((in_half >= FOX_EXTRA) & (in_half < 2 * FOX_EXTRA), 1.0, 0.0)
    ones_k = jnp.where(in_half < FOX_EXTRA, 1.0, 0.0)

    def normed(block, gain):
        sq = block * block
        s_left = jnp.sum(jnp.where(left, sq, 0.0), axis=-1, keepdims=True)
        s_right = jnp.sum(jnp.where(left, 0.0, sq), axis=-1, keepdims=True)
        ms = jnp.where(left, s_left, s_right) * (1.0 / FOX_HEAD_DIM)
        return block * lax.rsqrt(ms + EPS) * gain

    for j in range(FOX_PAIRS):
        qn = normed(z_ref[0, :, j * LANES:(j + 1) * LANES].astype(F32), qg_ref[...] * (ATTN_SCALE * LOG2E))
        kn = normed(z_ref[0, :, FOX_WIDTH + j * LANES:FOX_WIDTH + (j + 1) * LANES].astype(F32), kg_ref[...])
        for par in range(2):
            h = 2 * j + par
            own = left if par == 0 else jnp.logical_not(left)
            q_o[0, h] = jnp.where(own, qn, extra[:, h * LANES:(h + 1) * LANES] + ones_q).astype(BF16)
            k_o[0, h] = jnp.where(own, kn, extra[:, (FOX_HEADS + h) * LANES:(FOX_HEADS + h + 1) * LANES]
                                  + ones_k).astype(BF16)
    ri = lax.broadcasted_iota(I32, (FOX_WIDTH, FOX_WIDTH), 0)
    ci = lax.broadcasted_iota(I32, (FOX_WIDTH, FOX_WIDTH), 1)
    eye = jnp.where(ri == ci, 1.0, 0.0).astype(BF16)
    v = z_ref[0, :, 2 * FOX_WIDTH:3 * FOX_WIDTH].astype(BF16)
    vt_o[0] = lax.dot_general(eye, v, (((1,), (1,)), ((), ())), preferred_element_type=F32).astype(BF16)


def _foxprep(z, z_rw, f_bias_pad, q_gain, k_gain):
    bsz, t_len, _ = z.shape
    tm = min(512, t_len)
    qk = jax.ShapeDtypeStruct((bsz, FOX_HEADS, t_len, LANES), BF16)
    qk_spec = pl.BlockSpec((1, FOX_HEADS, tm, LANES), lambda b, i: (b, 0, i, 0))
    sel = jnp.asarray(_fox_bias_selector(), BF16)
    return pl.pallas_call(
        _foxprep_kernel,
        out_shape=(qk, qk, jax.ShapeDtypeStruct((bsz, FOX_WIDTH, t_len), BF16)),
        grid=(bsz, t_len // tm),
        in_specs=[pl.BlockSpec((1, tm, 3 * FOX_WIDTH), lambda b, i: (b, i, Z_FOX // (3 * FOX_WIDTH))),
                  pl.BlockSpec((1, tm, LANES), lambda b, i: (b, i, (Z_F - Z_RW) // LANES)),
                  pl.BlockSpec((1, LANES), lambda b, i: (0, 0)),
                  pl.BlockSpec((1, LANES), lambda b, i: (0, 0)),
                  pl.BlockSpec((1, LANES), lambda b, i: (0, 0)),
                  pl.BlockSpec(sel.shape, lambda b, i: (0, 0))],
        out_specs=(qk_spec, qk_spec, pl.BlockSpec((1, FOX_WIDTH, tm), lambda b, i: (b, 0, i))),
        scratch_shapes=[pltpu.VMEM((1, LANES), F32)],
        compiler_params=_cparams(("parallel", "arbitrary")),
        name="foxprep",
    )(z, z_rw, f_bias_pad, jnp.tile(q_gain.reshape(1, -1), (1, 2)), jnp.tile(k_gain.reshape(1, -1), (1, 2)), sel)


def _fox_kernel(qi_ref, kj_ref, q_ref, k_ref, vt_ref, o_ref, m_ref, acc_ref):
    i = qi_ref[pl.program_id(1)]
    j = kj_ref[pl.program_id(1)]
    tq = q_ref.shape[2]
    tk = k_ref.shape[2]
    sub = 8

    @pl.when(j == 0)
    def _():
        m_ref[...] = jnp.full_like(m_ref, MASK_VALUE)
        acc_ref[...] = jnp.zeros_like(acc_ref)

    ones_rows = jnp.ones((FOX_ACC_ROWS - FOX_HEAD_DIM, tk), BF16)

    def scores(h):
        return lax.dot_general(k_ref[0, h], q_ref[0, h], (((1,), (1,)), ((), ())), preferred_element_type=F32)

    def update(diagonal):
        if diagonal:
            key = lax.broadcasted_iota(I32, (tk, tq), 0)
            qry = lax.broadcasted_iota(I32, (tk, tq), 1)
            keep = key <= qry
        ahead = [scores(h) for h in range(FOX_LOOKAHEAD)]
        for h in range(FOX_HEADS):
            s = ahead.pop(0)
            if h + FOX_LOOKAHEAD < FOX_HEADS:
                ahead.append(scores(h + FOX_LOOKAHEAD))
            if diagonal:
                s = jnp.where(keep, s, MASK_VALUE)
            s3 = s.reshape(tk // sub, sub, tq)
            m_prev = m_ref[h]
            m_cur = jnp.max(jnp.max(s3, axis=0), axis=0, keepdims=True)
            m_new = jnp.maximum(m_prev, m_cur)
            alpha = jnp.exp2(m_prev - m_new)
            p = jnp.exp2(s3 - m_new[None]).astype(BF16).reshape(tk, tq)
            lhs = jnp.concatenate([vt_ref[0, h * FOX_HEAD_DIM:(h + 1) * FOX_HEAD_DIM, :], ones_rows], axis=0)
            pv = jnp.dot(lhs, p, preferred_element_type=F32)
            acc = acc_ref[h].reshape(FOX_ACC_ROWS // sub, sub, tq) * alpha[None]
            acc_ref[h] = acc.reshape(FOX_ACC_ROWS, tq) + pv
            m_ref[h] = m_new

    @pl.when(j < i)
    def _():
        update(False)

    @pl.when(j == i)
    def _():
        update(True)
        outs = []
        for h in range(FOX_HEADS):
            acc = acc_ref[h]
            outs.append((acc[:FOX_HEAD_DIM] / acc[FOX_HEAD_DIM:FOX_HEAD_DIM + 1]).astype(BF16))
        out_t = jnp.concatenate(outs, axis=0)
        ri = lax.broadcasted_iota(I32, (tq, tq), 0)
        ci = lax.broadcasted_iota(I32, (tq, tq), 1)
        eye = jnp.where(ri == ci, 1.0, 0.0).astype(BF16)
        o_ref[0] = lax.dot_general(eye, out_t, (((1,), (1,)), ((), ())),
                                   preferred_element_type=F32).astype(o_ref.dtype)


def _fox(q, k, vt):
    bsz, _, t_len, _ = q.shape
    tq = min(512, t_len)
    n_blk = t_len // tq
    pairs = [(i, j) for i in range(n_blk) for j in range(i + 1)]
    qi = jnp.asarray([p[0] for p in pairs], I32)
    kj = jnp.asarray([p[1] for p in pairs], I32)
    grid_spec = pltpu.PrefetchScalarGridSpec(
        num_scalar_prefetch=2,
        grid=(bsz, len(pairs)),
        in_specs=[pl.BlockSpec((1, FOX_HEADS, tq, LANES), lambda b, s, qi, kj: (b, 0, qi[s], 0)),
                  pl.BlockSpec((1, FOX_HEADS, tq, LANES), lambda b, s, qi, kj: (b, 0, kj[s], 0)),
                  pl.BlockSpec((1, FOX_WIDTH, tq), lambda b, s, qi, kj: (b, 0, kj[s]))],
        out_specs=pl.BlockSpec((1, tq, FOX_WIDTH), lambda b, s, qi, kj: (b, qi[s], 0)),
        scratch_shapes=[pltpu.VMEM((FOX_HEADS, 8, tq), F32), pltpu.VMEM((FOX_HEADS, FOX_ACC_ROWS, tq), F32)],
    )
    return pl.pallas_call(
        _fox_kernel,
        out_shape=jax.ShapeDtypeStruct((bsz, t_len, FOX_WIDTH), BF16),
        grid_spec=grid_spec,
        compiler_params=_cparams(("parallel", "arbitrary")),
        name="fox",
    )(qi, kj, q, k, vt)


def _merge_kernel(zg_ref, ygm_ref, yrw_ref, yfox_ref, x_ref, g1_ref, sc2_ref, sh2_ref, pb_ref, wo_ref, wr_ref, br_ref,
                  x1_o, h2_o, idx_o, gate_o, rank_o, cnt_o, carry_ref):
    tm = x_ref.shape[1]

    @pl.when((pl.program_id(0) == 0) & (pl.program_id(1) == 0))
    def _():
        carry_ref[...] = jnp.zeros_like(carry_ref)

    sg = 0.5 * jnp.tanh(0.5 * zg_ref[0].astype(F32)) + 0.5
    p_gm = jnp.dot(ygm_ref[0], pb_ref[0:GM_WIDTH, :], preferred_element_type=F32)
    y_rw = jnp.concatenate([yrw_ref[0, h] for h in range(RW_HEADS)], axis=-1)
    p_rw = jnp.dot(y_rw, pb_ref[GM_WIDTH:GM_WIDTH + RW_WIDTH, :], preferred_element_type=F32)
    p_fox = jnp.dot(yfox_ref[0], pb_ref[GM_WIDTH + RW_WIDTH:, :], preferred_element_type=F32)
    merged = sg[:, 0:D_MODEL] * p_gm + sg[:, D_MODEL:2 * D_MODEL] * p_rw + sg[:, 2 * D_MODEL:] * p_fox
    x1 = x_ref[0] + g1_ref[0] * jnp.dot(merged.astype(BF16), wo_ref[...], preferred_element_type=F32)
    x1_o[0] = x1
    h2 = x1 * lax.rsqrt(jnp.mean(x1 * x1, axis=-1, keepdims=True) + EPS) * (1.0 + sc2_ref[0]) + sh2_ref[0]
    h2_o[0] = _pack_halves(h2)

    h_hi, h_lo, _ = _split3(h2)
    w_hi, w_lo, _ = _split3(wr_ref[...])
    logits = (jnp.dot(h_hi, w_hi, preferred_element_type=F32) + jnp.dot(h_hi, w_lo, preferred_element_type=F32)
              + jnp.dot(h_lo, w_hi, preferred_element_type=F32)) + br_ref[...]
    lane = lax.broadcasted_iota(I32, (tm, N_EXPERTS), 1)
    vals, idxs = [], []
    rest = logits
    for _ in range(TOP_K):
        m = jnp.max(rest, axis=-1, keepdims=True)
        am = jnp.min(jnp.where(rest == m, lane, N_EXPERTS), axis=-1, keepdims=True)
        vals.append(m)
        idxs.append(am)
        rest = jnp.where(lane == am, -jnp.inf, rest)
    exps = [jnp.exp(val - vals[0]) for val in vals]
    denom = exps[0] + exps[1] + exps[2] + exps[3]

    onehot = jnp.zeros((tm, N_EXPERTS), F32)
    for am in idxs:
        onehot = onehot + jnp.where(lane == am, 1.0, 0.0)
    ri = lax.broadcasted_iota(I32, (tm, tm), 0)
    ci = lax.broadcasted_iota(I32, (tm, tm), 1)
    before = jnp.where(ri > ci, 1.0, 0.0).astype(BF16)
    seen = carry_ref[...] + jnp.dot(before, onehot.astype(BF16), preferred_element_type=F32)
    lane_k = lax.broadcasted_iota(I32, (tm, TOP_K), 1)
    idx_out = jnp.zeros((tm, TOP_K), I32)
    gate_out = jnp.zeros((tm, TOP_K), F32)
    rank_out = jnp.zeros((tm, TOP_K), I32)
    for kk in range(TOP_K):
        rank = jnp.sum(jnp.where(lane == idxs[kk], seen, 0.0), axis=-1, keepdims=True).astype(I32)
        idx_out = jnp.where(lane_k == kk, idxs[kk], idx_out)
        gate_out = jnp.where(lane_k == kk, exps[kk] / denom, gate_out)
        rank_out = jnp.where(lane_k == kk, rank, rank_out)
    idx_o[0] = idx_out
    gate_o[0] = gate_out
    rank_o[0] = rank_out
    total = carry_ref[...] + jnp.sum(onehot, axis=0, keepdims=True)
    carry_ref[...] = total
    cnt_o[...] = total.astype(I32)


def _merge(z, y_gm, y_rw, y_fox, x, gate1, scale2, shift2, w_branch, w_o, w_router, b_router, layer):
    bsz, t_len, d = x.shape
    tm = min(512, t_len)
    row = lambda w: pl.BlockSpec((1, tm, w), lambda b, i: (b, i, 0))
    mod = pl.BlockSpec((1, 1, d), lambda b, i: (b, 0, 0))
    full = lambda shape: pl.BlockSpec(shape, lambda b, i: (0,) * len(shape))
    return pl.pallas_call(
        _merge_kernel,
        out_shape=(jax.ShapeDtypeStruct((bsz, t_len, d), F32), jax.ShapeDtypeStruct((bsz, t_len, d // 2), I32),
                   jax.ShapeDtypeStruct((bsz, t_len, TOP_K), I32), jax.ShapeDtypeStruct((bsz, t_len, TOP_K), F32),
                   jax.ShapeDtypeStruct((bsz, t_len, TOP_K), I32), jax.ShapeDtypeStruct((1, N_EXPERTS), I32)),
        grid=(bsz, t_len // tm),
        in_specs=[row(N_BRANCH * D_MODEL), row(GM_WIDTH),
                  pl.BlockSpec((1, RW_HEADS, tm, RW_HEAD_DIM), lambda b, i: (b, 0, i, 0)),
                  row(FOX_WIDTH), row(d), mod, mod, mod,
                  pl.BlockSpec((MIX_WIDTH, d), lambda b, i: (layer, 0)), pl.BlockSpec((d, d), lambda b, i: (layer, 0)),
                  full(w_router.shape), full((1, N_EXPERTS))],
        out_specs=(row(d), row(d // 2), row(TOP_K), row(TOP_K), row(TOP_K), full((1, N_EXPERTS))),
        scratch_shapes=[pltpu.VMEM((1, N_EXPERTS), F32)],
        compiler_params=_cparams(("arbitrary", "arbitrary")),
        name="merge_router",
    )(z, y_gm, y_rw, y_fox, x, gate1, scale2, shift2, w_branch, w_o, w_router, b_router.reshape(1, N_EXPERTS))


def _sc_mesh():
    return plsc.VectorSubcoreMesh(core_axis_name="c", subcore_axis_name="s",
                                  num_cores=SC_CORES, num_subcores=SC_SUBCORES)


def _sc_worker():
    return lax.axis_index("s") * SC_CORES + lax.axis_index("c")


def _sc_scatter_rows(src, idx3, n_out):
    _, d = src.shape
    n_copy, n_grp, _ = idx3.shape
    grp_per_w = n_grp // SC_WORKERS

    def body(src_hbm, idx_hbm, out_hbm, idx_v, rows_v):
        g0 = _sc_worker() * grp_per_w
        for q in range(n_copy):
            pltpu.sync_copy(idx_hbm.at[q, pl.ds(g0, grp_per_w)], idx_v.at[pl.ds(q * grp_per_w, grp_per_w)])

        @pl.loop(0, grp_per_w)
        def _(j):
            pltpu.sync_copy(src_hbm.at[pl.ds((g0 + j) * SC_ROWS, SC_ROWS)], rows_v)
            for q in range(n_copy):
                pltpu.sync_copy(rows_v, out_hbm.at[idx_v.at[q * grp_per_w + j]])

    return pl.kernel(
        body, out_type=jax.ShapeDtypeStruct((n_out, d), src.dtype), mesh=_sc_mesh(),
        scratch_types=[pltpu.VMEM((n_copy * grp_per_w, SC_ROWS), I32), pltpu.VMEM((SC_ROWS, d), src.dtype)],
        name="sc_dispatch",
    )(src, idx3)


def _sc_gather_rows(table, idx2):
    _, d = table.shape
    n_grp, _ = idx2.shape
    grp_per_w = n_grp // SC_WORKERS

    def body(table_hbm, idx_hbm, out_hbm, idx_v, rows_v):
        g0 = _sc_worker() * grp_per_w
        pltpu.sync_copy(idx_hbm.at[pl.ds(g0, grp_per_w)], idx_v)

        @pl.loop(0, grp_per_w)
        def _(j):
            pltpu.sync_copy(table_hbm.at[idx_v.at[j]], rows_v)
            pltpu.sync_copy(rows_v, out_hbm.at[pl.ds((g0 + j) * SC_ROWS, SC_ROWS)])

    return pl.kernel(
        body, out_type=jax.ShapeDtypeStruct((n_grp * SC_ROWS, d), table.dtype), mesh=_sc_mesh(),
        scratch_types=[pltpu.VMEM((grp_per_w, SC_ROWS), I32), pltpu.VMEM((SC_ROWS, d), table.dtype)],
        name="sc_combine_gather",
    )(table, idx2)


def _ffn_weight_copies(expert, wgu_hbm, wd_hbm, stage_gu, stage_d, sem):
    return (pltpu.make_async_copy(wgu_hbm.at[expert], stage_gu, sem.at[0]),
            pltpu.make_async_copy(wd_hbm.at[expert], stage_d, sem.at[1]))


def _ffn_kernel(be_ref, first_ref, nxt_ref, nv_ref, x_ref, wgu_hbm, wd_hbm, bgu_ref, bd_ref, o_ref,
                stage_gu, stage_d, wgu_b, wd_b, sem):
    step = pl.program_id(0)
    copies = functools.partial(_ffn_weight_copies, wgu_hbm=wgu_hbm, wd_hbm=wd_hbm, stage_gu=stage_gu,
                               stage_d=stage_d, sem=sem)

    @pl.when(step == 0)
    def _():
        for cp in copies(be_ref[0]):
            cp.start()

    for b in range(FFN_STEP_BLOCKS):
        idx = step * FFN_STEP_BLOCKS + b
        expert = be_ref[idx]
        n_valid = nv_ref[idx]
        rows = slice(b * MOE_BLOCK, (b + 1) * MOE_BLOCK)

        @pl.when(first_ref[idx] == 1)
        def _():
            for cp in copies(expert):
                cp.wait()
            wgu_b[...] = stage_gu[...].astype(BF16)
            wd_b[...] = stage_d[...].astype(BF16)

            @pl.when(nxt_ref[idx] >= 0)
            def _():
                for cp in copies(nxt_ref[idx]):
                    cp.start()

        @pl.when(n_valid > 0)
        def _():
            rowid = lax.broadcasted_iota(I32, (MOE_BLOCK, x_ref.shape[1]), 0)
            xp = jnp.where(rowid < n_valid, x_ref[rows, :], 0)
            x = jnp.concatenate(_unpack_halves(xp), axis=-1).astype(BF16)
            gu = jnp.dot(x, wgu_b[...], preferred_element_type=F32) + bgu_ref[expert]
            g_ = jnp.minimum(gu[:, :D_FF], SWIGLU_LIMIT)
            u_ = jnp.clip(gu[:, D_FF:], -SWIGLU_LIMIT, SWIGLU_LIMIT)
            act = (u_ + 1.0) * (g_ * jax.nn.sigmoid(SWIGLU_ALPHA * g_))
            y = jnp.dot(act.astype(BF16), wd_b[...], preferred_element_type=F32) + bd_ref[expert]
            o_ref[rows, :] = _pack_halves(y)

        @pl.when(n_valid <= 0)
        def _():
            o_ref[rows, :] = jnp.zeros((MOE_BLOCK, o_ref.shape[1]), o_ref.dtype)


def _ffn(block_expert, block_first, block_next, block_valid, xin, w_gate_up, b_gate_up, w_down, b_down):
    n_rows, dp = xin.shape
    d = 2 * dp
    step_rows = FFN_STEP_BLOCKS * MOE_BLOCK
    resident = lambda arr: pl.BlockSpec(arr.shape, lambda i, *_: (0,) * arr.ndim)
    grid_spec = pltpu.PrefetchScalarGridSpec(
        num_scalar_prefetch=4,
        grid=(n_rows // step_rows,),
        in_specs=[pl.BlockSpec((step_rows, dp), lambda i, *_: (i, 0)),
                  pl.BlockSpec(memory_space=pl.ANY), pl.BlockSpec(memory_space=pl.ANY),
                  resident(b_gate_up), resident(b_down)],
        out_specs=pl.BlockSpec((step_rows, dp), lambda i, *_: (i, 0)),
        scratch_shapes=[pltpu.VMEM((d, 2 * D_FF), F32), pltpu.VMEM((D_FF, d), F32),
                        pltpu.VMEM((d, 2 * D_FF), BF16), pltpu.VMEM((D_FF, d), BF16),
                        pltpu.SemaphoreType.DMA((2,))],
    )
    return pl.pallas_call(
        _ffn_kernel,
        out_shape=jax.ShapeDtypeStruct((n_rows, dp), I32),
        grid_spec=grid_spec,
        compiler_params=pltpu.CompilerParams(dimension_semantics=("arbitrary",), vmem_limit_bytes=FFN_VMEM_LIMIT),
        name="expert_ffn",
    )(block_expert, block_first, block_next, block_valid, xin, w_gate_up, w_down, b_gate_up, b_down)


def _combine_kernel(x1_ref, g2_ref, gate_ref, yg_ref, o_ref):
    gate = gate_ref[0]
    y_lo = y_hi = None
    for q in range(TOP_K):
        lo, hi = _unpack_halves(yg_ref[q, 0])
        wq = gate[:, q:q + 1]
        y_lo = wq * lo if y_lo is None else y_lo + wq * lo
        y_hi = wq * hi if y_hi is None else y_hi + wq * hi
    o_ref[0] = x1_ref[0] + g2_ref[0] * jnp.concatenate([y_lo, y_hi], axis=-1)


def _combine(x1, gate2, gate, yg):
    bsz, t_len, d = x1.shape
    tm = min(512, t_len)
    return pl.pallas_call(
        _combine_kernel,
        out_shape=jax.ShapeDtypeStruct((bsz, t_len, d), F32),
        grid=(bsz, t_len // tm),
        in_specs=[pl.BlockSpec((1, tm, d), lambda b, i: (b, i, 0)),
                  pl.BlockSpec((1, 1, d), lambda b, i: (b, 0, 0)),
                  pl.BlockSpec((1, tm, TOP_K), lambda b, i: (b, i, 0)),
                  pl.BlockSpec((TOP_K, 1, tm, d // 2), lambda b, i: (0, b, i, 0))],
        out_specs=pl.BlockSpec((1, tm, d), lambda b, i: (b, i, 0)),
        compiler_params=_cparams(("parallel", "parallel")),
        name="moe_combine",
    )(x1, gate2, gate, yg)


def _moe(x1, gate2, h2, top_idx, gate, rank, counts, w_gate_up, b_gate_up, w_down, b_down, layer):
    bsz, t_len, d = h2.shape
    n_tok = bsz * t_len
    n_assign = n_tok * TOP_K
    n_blocks = -(-n_assign // MOE_BLOCK) + N_EXPERTS
    counts = counts.reshape(N_EXPERTS)
    blocks_e = (counts + MOE_BLOCK - 1) // MOE_BLOCK
    blk_end = jnp.cumsum(blocks_e)
    blk_start = blk_end - blocks_e
    experts = jnp.arange(N_EXPERTS, dtype=I32)
    onehot = top_idx.reshape(n_tok, TOP_K, 1) == experts
    dest = jnp.sum(jnp.where(onehot, blk_start * MOE_BLOCK, 0), axis=-1) + rank.reshape(n_tok, TOP_K)
    dest_t = dest.T.astype(I32)
    blk = jnp.arange(n_blocks, dtype=I32)
    block_expert = jnp.minimum(jnp.sum(blk_end[None, :] <= blk[:, None], axis=1), N_EXPERTS - 1).astype(I32)
    be_hot = block_expert[:, None] == experts
    cnt_b = jnp.sum(jnp.where(be_hot, counts, 0), axis=1)
    start_b = jnp.sum(jnp.where(be_hot, blk_start, 0), axis=1)
    block_valid = jnp.clip(cnt_b - (blk - start_b) * MOE_BLOCK, 0, MOE_BLOCK).astype(I32)
    xin = _sc_scatter_rows(h2.reshape(n_tok, d), dest_t.reshape(TOP_K, n_tok // SC_ROWS, SC_ROWS),
                           n_blocks * MOE_BLOCK)
    block_first = jnp.concatenate([jnp.ones((1,), I32), (block_expert[1:] != block_expert[:-1]).astype(I32)])
    run_start = jnp.where(block_first == 1, blk, n_blocks)
    later_start = lax.cummin(jnp.concatenate([run_start[1:], jnp.full((1,), n_blocks, I32)]), reverse=True)
    next_expert = jnp.concatenate([block_expert, jnp.full((1,), -1 - layer * N_EXPERTS, I32)])[later_start]
    yb = _ffn(block_expert + layer * N_EXPERTS, block_first, next_expert + layer * N_EXPERTS, block_valid, xin,
              w_gate_up, b_gate_up, w_down, b_down)
    yg = _sc_gather_rows(yb, dest_t.reshape(n_assign // SC_ROWS, SC_ROWS))
    return _combine(x1, gate2, gate, yg.reshape(TOP_K, bsz, t_len, d))


def _permute_kernel(w_ref, o_ref):
    o_gm = 0
    o_rw = o_gm + 2 * GM_WIDTH
    o_fox = o_rw + RW_SHIFT_WIDTH
    o_f = o_fox + 3 * FOX_WIDTH
    o_gate = o_f + FOX_HEADS
    w = w_ref[0]
    o_ref[0, :, Z_GATE:Z_FOX] = w[:, o_gate:o_gate + N_BRANCH * D_MODEL].astype(BF16)
    o_ref[0, :, Z_FOX:Z_GM] = w[:, o_fox:o_f].astype(BF16)
    o_ref[0, :, Z_GM:Z_RW] = w[:, o_gm:o_rw].astype(BF16)
    o_ref[0, :, Z_RW:Z_F] = w[:, o_rw:o_fox].astype(BF16)
    tail = jnp.concatenate([w[:, o_f:o_gate], jnp.zeros((w.shape[0], Z_WIDTH - Z_F - FOX_HEADS), F32)], axis=-1)
    o_ref[0, :, Z_F:Z_WIDTH] = tail.astype(BF16)


def _permute_w_in(w_in):
    n_layer, d, w_cols = w_in.shape
    tr = 256
    return pl.pallas_call(
        _permute_kernel,
        out_shape=jax.ShapeDtypeStruct((n_layer, d, Z_WIDTH), BF16),
        grid=(n_layer, d // tr),
        in_specs=[pl.BlockSpec((1, tr, w_cols), lambda l, i: (l, i, 0))],
        out_specs=pl.BlockSpec((1, tr, Z_WIDTH), lambda l, i: (l, i, 0)),
        compiler_params=_cparams(("parallel", "parallel")),
        name="permute_w_in",
    )(w_in)


def _layer(x, mod, w_in_p, gm_v_gain, gm_w_s, gm_b_s, mu_pad, w_lora, rw_w0, rw_a0, rw_k_k, rw_k_a, rw_r_k,
           rw_gn_gain, rw_gn_bias, f_bias_pad, fox_q_gain, fox_k_gain, w_branch, w_o, w_router, b_router,
           w_gate_up, b_gate_up, w_down, b_down, layer):
    shift1, scale1, gate1, shift2, scale2, gate2 = (mod[:, i][:, None, :] for i in range(6))
    z, z_rw = _inproj(x, scale1, shift1, w_in_p, layer)
    y_gm = _gmlp(z, gm_v_gain, gm_w_s, gm_b_s)
    r, lw, k, v, a, b, g = _rwprep(z_rw, mu_pad, w_lora, rw_w0, rw_a0, rw_k_k, rw_k_a)
    y_rw = _rwscan(r, lw, k, v, a, b, g, rw_r_k, rw_gn_gain, rw_gn_bias)
    q, kf, vf = _foxprep(z, z_rw, f_bias_pad, fox_q_gain, fox_k_gain)
    y_fox = _fox(q, kf, vf)
    x1, h2, top_idx, gate, rank, counts = _merge(z, y_gm, y_rw, y_fox, x, gate1, scale2, shift2,
                                                 w_branch, w_o, w_router, b_router, layer)
    return _moe(x1, gate2, h2, top_idx, gate, rank, counts, w_gate_up, b_gate_up, w_down, b_down, layer)


def kernel(x, c, w_ada, b_ada, w_in, gm_v_gain, gm_w_s, gm_b_s, rw_mu, rw_w0, rw_w2, rw_a0, rw_a2, rw_g2, rw_k_k,
           rw_k_a, rw_r_k, rw_gn_gain, rw_gn_bias, fox_f_bias, fox_q_gain, fox_k_gain, w_branch, w_o, w_router,
           b_router, w_gate_up, b_gate_up, w_down, b_down):
    n_layer = w_ada.shape[0]
    bsz = x.shape[0]
    c_pad = jnp.zeros((8, D_MODEL), F32).at[:bsz].set(c)
    mod = _adaln(c_pad, w_ada, b_ada)[:, :bsz].reshape(n_layer, bsz, 6, D_MODEL)
    w_in_p = _permute_w_in(w_in)
    mu_pad = jnp.pad(rw_mu, ((0, 0), (0, RW_BLOCK - RW_SHIFT_WIDTH)))
    w_lora = jnp.zeros((n_layer, RW_LORA, 3 * RW_WIDTH), F32)
    w_lora = w_lora.at[:, 0:RW_DECAY_LORA, 0:RW_WIDTH].set(rw_w2)
    w_lora = w_lora.at[:, RW_DECAY_LORA:RW_DECAY_LORA + RW_ICLR_LORA, RW_WIDTH:2 * RW_WIDTH].set(rw_a2)
    w_lora = w_lora.at[:, RW_DECAY_LORA + RW_ICLR_LORA:, 2 * RW_WIDTH:].set(rw_g2)
    f_bias_pad = jnp.pad(fox_f_bias, ((0, 0), (0, LANES - FOX_HEADS)))
    w_in_p = w_in_p.reshape(n_layer * D_MODEL, Z_WIDTH)
    w_branch_b = w_branch.astype(BF16).reshape(n_layer * MIX_WIDTH, D_MODEL)
    w_o_b = w_o.astype(BF16).reshape(n_layer * D_MODEL, D_MODEL)
    w_gu = w_gate_up.reshape(n_layer * N_EXPERTS, D_MODEL, 2 * D_FF)
    b_gu = b_gate_up.reshape(n_layer * N_EXPERTS, 1, 2 * D_FF)
    w_dn = w_down.reshape(n_layer * N_EXPERTS, D_FF, D_MODEL)
    b_dn = b_down.reshape(n_layer * N_EXPERTS, 1, D_MODEL)
    n_stream = N_STREAMS if bsz % N_STREAMS == 0 else 1
    per = bsz // n_stream
    streams = [x[s * per:(s + 1) * per] for s in range(n_stream)]
    for l in range(n_layer):
        streams = [_layer(xs, mod[l, s * per:(s + 1) * per], w_in_p, gm_v_gain[l], gm_w_s[l], gm_b_s[l],
                          mu_pad[l:l + 1], w_lora[l], rw_w0[l], rw_a0[l], rw_k_k[l], rw_k_a[l], rw_r_k[l],
                          rw_gn_gain[l], rw_gn_bias[l], f_bias_pad[l:l + 1], fox_q_gain[l], fox_k_gain[l],
                          w_branch_b, w_o_b, w_router[l], b_router[l], w_gu, b_gu, w_dn, b_dn, l)
                   for s, xs in enumerate(streams)]
    return jnp.concatenate(streams, axis=0)
```

```python
import functools

import jax
import jax.numpy as jnp
import numpy as np
from jax import lax
from jax.experimental import pallas as pl
from jax.experimental.pallas import tpu as pltpu
from jax.experimental.pallas import tpu_sc as plsc

F32 = jnp.float32
BF16 = jnp.bfloat16
I32 = jnp.int32
HIGHEST = lax.Precision.HIGHEST

D_MODEL = 1024
GM_CHUNK = 128
GM_GROUPS = 4
GM_WIDTH = 256
GM_GROUP_DIM = GM_WIDTH // GM_GROUPS
RW_HEADS = 4
RW_HEAD_DIM = 64
RW_WIDTH = RW_HEADS * RW_HEAD_DIM
RW_DECAY_LORA = 32
RW_ICLR_LORA = 32
RW_GATE_LORA = 64
RW_LORA = RW_DECAY_LORA + RW_ICLR_LORA + RW_GATE_LORA
RW_SHIFT_WIDTH = 3 * RW_WIDTH + RW_LORA
RW_GN_EPS = 64e-5
FOX_HEADS = 8
FOX_HEAD_DIM = 64
FOX_WIDTH = FOX_HEADS * FOX_HEAD_DIM
ATTN_SCALE = FOX_HEAD_DIM ** -0.5
MASK_VALUE = -1e30
LOG2E = 1.4426950408889634
N_BRANCH = 3
MIX_WIDTH = GM_WIDTH + RW_WIDTH + FOX_WIDTH
N_EXPERTS = 32
TOP_K = 4
D_FF = D_MODEL
SWIGLU_LIMIT = 7.0
SWIGLU_ALPHA = 1.702
MOE_BLOCK = 256
EPS = 1e-6

Z_GATE = 0
Z_FOX = N_BRANCH * D_MODEL
Z_GM = Z_FOX + 3 * FOX_WIDTH
Z_RW = Z_GM + 2 * GM_WIDTH
RW_BLOCK = 1024
Z_F = Z_RW + RW_SHIFT_WIDTH
Z_WIDTH = Z_RW + RW_BLOCK
LANES = 128
RW_CHUNK = 64
RW_PREP_UNROLL = 4

VMEM_LIMIT = 48 * 1024 * 1024
FFN_VMEM_LIMIT = 56 * 1024 * 1024
FFN_STEP_BLOCKS = 4
SC_CORES = 2
SC_SUBCORES = 16
SC_WORKERS = SC_CORES * SC_SUBCORES
SC_ROWS = 64


def _cparams(sem):
    return pltpu.CompilerParams(dimension_semantics=sem, vmem_limit_bytes=VMEM_LIMIT)


def _mm(a, b):
    return jnp.dot(a.astype(BF16), b.astype(BF16), preferred_element_type=F32)


def _mm_nt(a, b):
    return lax.dot_general(a.astype(BF16), b.astype(BF16), (((1,), (1,)), ((), ())), preferred_element_type=F32)


def _mm_tn(a, b):
    return lax.dot_general(a.astype(BF16), b.astype(BF16), (((0,), (0,)), ((), ())), preferred_element_type=F32)


def _split3(x):
    hi = x.astype(BF16)
    r1 = x - hi.astype(F32)
    mid = r1.astype(BF16)
    lo = (r1 - mid.astype(F32)).astype(BF16)
    return hi, mid, lo


def _tri_cumsum(x, n):
    ri = lax.broadcasted_iota(I32, (n, n), 0)
    ci = lax.broadcasted_iota(I32, (n, n), 1)
    ones = jnp.where(ri >= ci, 1.0, 0.0).astype(BF16)
    hi, mid, lo = _split3(x)
    return (jnp.dot(ones, hi, preferred_element_type=F32) + jnp.dot(ones, mid, preferred_element_type=F32)
            + jnp.dot(ones, lo, preferred_element_type=F32))


def _pack_halves(x):
    w = x.shape[1] // 2
    hi = pltpu.bitcast(x[:, :w].astype(BF16).astype(F32), jnp.uint32)
    lo = pltpu.bitcast(x[:, w:].astype(BF16).astype(F32), jnp.uint32)
    return pltpu.bitcast(hi | (lo >> 16), I32)


def _unpack_halves(p):
    u = pltpu.bitcast(p, jnp.uint32)
    return pltpu.bitcast(u & jnp.uint32(0xFFFF0000), F32), pltpu.bitcast(u << 16, F32)


def _log_sigmoid(x):
    return jnp.minimum(x, 0.0) - jnp.log1p(jnp.exp(-jnp.abs(x)))


def _adaln_kernel(c_ref, w_ref, b_ref, o_ref):
    c = c_ref[...]
    s = c * jax.nn.sigmoid(c)
    o_ref[0] = jnp.dot(s, w_ref[0], preferred_element_type=F32, precision=HIGHEST) + b_ref[0]


def _adaln(c_pad, w_ada, b_ada):
    n_layer, d, w6 = w_ada.shape
    tn = 1536
    return pl.pallas_call(
        _adaln_kernel,
        out_shape=jax.ShapeDtypeStruct((n_layer, c_pad.shape[0], w6), F32),
        grid=(n_layer, w6 // tn),
        in_specs=[pl.BlockSpec(c_pad.shape, lambda l, j: (0, 0)),
                  pl.BlockSpec((1, d, tn), lambda l, j: (l, 0, j)),
                  pl.BlockSpec((1, 1, tn), lambda l, j: (l, 0, j))],
        out_specs=pl.BlockSpec((1, c_pad.shape[0], tn), lambda l, j: (l, 0, j)),
        compiler_params=_cparams(("parallel", "parallel")),
        name="adaln",
    )(c_pad, w_ada, b_ada.reshape(n_layer, 1, w6))


def _inproj_kernel(x_ref, sc_ref, sh_ref, w_ref, zm_ref, zr_ref, xn_ref):
    j = pl.program_id(2)

    @pl.when(j == 0)
    def _():
        x = x_ref[0]
        xn = x * lax.rsqrt(jnp.mean(x * x, axis=-1, keepdims=True) + EPS)
        xn_ref[...] = (xn * (1.0 + sc_ref[0]) + sh_ref[0]).astype(BF16)

    acc = jnp.dot(xn_ref[...], w_ref[...], preferred_element_type=F32)

    @pl.when(j < Z_RW // RW_BLOCK)
    def _():
        zm_ref[0] = acc.astype(BF16)

    @pl.when(j == Z_RW // RW_BLOCK)
    def _():
        zr_ref[0] = acc


def _inproj(x, scale, shift, w, layer):
    bsz, t_len, d = x.shape
    tm = min(1024, t_len)
    tn = RW_BLOCK
    n_main = Z_RW // tn
    return pl.pallas_call(
        _inproj_kernel,
        out_shape=(jax.ShapeDtypeStruct((bsz, t_len, Z_RW), BF16), jax.ShapeDtypeStruct((bsz, t_len, RW_BLOCK), F32)),
        grid=(bsz, t_len // tm, Z_WIDTH // tn),
        in_specs=[pl.BlockSpec((1, tm, d), lambda b, i, j: (b, i, 0)),
                  pl.BlockSpec((1, 1, d), lambda b, i, j: (b, 0, 0)),
                  pl.BlockSpec((1, 1, d), lambda b, i, j: (b, 0, 0)),
                  pl.BlockSpec((d, tn), lambda b, i, j: (layer, j))],
        out_specs=(pl.BlockSpec((1, tm, tn), lambda b, i, j: (b, i, jnp.minimum(j, n_main - 1))),
                   pl.BlockSpec((1, tm, tn), lambda b, i, j: (b, i, 0))),
        scratch_shapes=[pltpu.VMEM((tm, d), BF16)],
        compiler_params=_cparams(("parallel", "parallel", "arbitrary")),
        name="inproj",
    )(x, scale, shift, w)


def _gmlp_kernel(z_ref, gain_ref, ws_ref, bst_ref, o_ref):
    tm = z_ref.shape[1]
    z = z_ref[0].astype(F32)
    u = jax.nn.gelu(z[:, :GM_WIDTH])
    v = jax.nn.gelu(z[:, GM_WIDTH:])
    v = v * lax.rsqrt(jnp.mean(v * v, axis=-1, keepdims=True) + EPS) * gain_ref[...]
    vb = v.astype(BF16)
    grp = lax.broadcasted_iota(I32, (GM_CHUNK, GM_WIDTH), 1) // GM_GROUP_DIM
    ri = lax.broadcasted_iota(I32, (GM_CHUNK, GM_CHUNK), 0)
    ci = lax.broadcasted_iota(I32, (GM_CHUNK, GM_CHUNK), 1)
    causal = ri >= ci
    bias = jnp.zeros((GM_CHUNK, GM_WIDTH), F32)
    ws = []
    for g in range(GM_GROUPS):
        ws.append(jnp.where(causal, ws_ref[g], 0.0).astype(BF16))
        bias = jnp.where(grp == g, bst_ref[:, g:g + 1], bias)
    for c in range(tm // GM_CHUNK):
        rows = slice(c * GM_CHUNK, (c + 1) * GM_CHUNK)
        vc = vb[rows]
        mixed = bias
        for g in range(GM_GROUPS):
            m = jnp.dot(ws[g], vc, preferred_element_type=F32)
            mixed = mixed + jnp.where(grp == g, m, 0.0)
        o_ref[0, rows, :] = (u[rows] * mixed).astype(o_ref.dtype)


def _gmlp(z, gain, w_s, b_s):
    bsz, t_len, _ = z.shape
    tm = min(512, t_len)
    return pl.pallas_call(
        _gmlp_kernel,
        out_shape=jax.ShapeDtypeStruct((bsz, t_len, GM_WIDTH), BF16),
        grid=(bsz, t_len // tm),
        in_specs=[pl.BlockSpec((1, tm, 2 * GM_WIDTH), lambda b, i: (b, i, Z_GM // (2 * GM_WIDTH))),
                  pl.BlockSpec((1, GM_WIDTH), lambda b, i: (0, 0)),
                  pl.BlockSpec((GM_GROUPS, GM_CHUNK, GM_CHUNK), lambda b, i: (0, 0, 0)),
                  pl.BlockSpec((GM_CHUNK, GM_GROUPS), lambda b, i: (0, 0))],
        out_specs=pl.BlockSpec((1, tm, GM_WIDTH), lambda b, i: (b, i, 0)),
        compiler_params=_cparams(("parallel", "parallel")),
        name="gmlp",
    )(z, gain.reshape(1, GM_WIDTH), w_s, b_s.T)


def _rwprep_kernel(z_ref, zp_ref, mu_ref, wl_ref, w0_ref, a0_ref, kk_ref, ka_ref,
                   r_o, lw_o, k_o, v_o, a_o, b_o, g_o):
    tm = z_ref.shape[1]
    z = z_ref[0]
    prev = jnp.where(pl.program_id(1) > 0, zp_ref[0, 7:8, :], 0.0)
    rowid = lax.broadcasted_iota(I32, z.shape, 0)
    zs = jnp.where(rowid == 0, prev, pltpu.roll(z, 1, axis=0))
    zz = z + mu_ref[...] * (zs - z)
    r = zz[:, 0:RW_WIDTH]
    k = zz[:, RW_WIDTH:2 * RW_WIDTH]
    v = zz[:, 2 * RW_WIDTH:3 * RW_WIDTH]
    lo = zz[:, 3 * RW_WIDTH:3 * RW_WIDTH + RW_LORA]
    lane = lax.broadcasted_iota(I32, (tm, RW_LORA), 1)
    act = jnp.where(lane < RW_DECAY_LORA, jnp.tanh(lo),
                    jnp.where(lane < RW_DECAY_LORA + RW_ICLR_LORA, lo, jax.nn.sigmoid(lo)))
    proj = jnp.dot(act, wl_ref[...], preferred_element_type=F32, precision=HIGHEST)
    xw = -(w0_ref[...] + proj[:, 0:RW_WIDTH])
    softplus = jnp.maximum(xw, 0.0) + jnp.log1p(jnp.exp(-jnp.abs(xw)))
    lw = -jnp.exp(-softplus - 0.5)
    a = jax.nn.sigmoid(a0_ref[...] + proj[:, RW_WIDTH:2 * RW_WIDTH])
    g = proj[:, 2 * RW_WIDTH:3 * RW_WIDTH]
    kk = k * kk_ref[...]
    k2 = k * (1.0 + (a - 1.0) * ka_ref[...])
    for h in range(RW_HEADS):
        sl = slice(h * RW_HEAD_DIM, (h + 1) * RW_HEAD_DIM)
        kkh = kk[:, sl]
        nrm = jnp.sqrt(jnp.sum(kkh * kkh, axis=-1, keepdims=True))
        kkh = kkh / jnp.maximum(nrm, 1e-12)
        r_o[0, h] = r[:, sl]
        lw_o[0, h] = lw[:, sl]
        k_o[0, h] = k2[:, sl]
        v_o[0, h] = v[:, sl]
        a_o[0, h] = -kkh
        b_o[0, h] = kkh * a[:, sl]
        g_o[0, h] = g[:, sl]


def _rwprep(z, mu_pad, w_lora, w0, a0, k_k, k_a):
    bsz, t_len, _ = z.shape
    tm = min(512, t_len)
    hm = jax.ShapeDtypeStruct((bsz, RW_HEADS, t_len, RW_HEAD_DIM), F32)
    hm_spec = pl.BlockSpec((1, RW_HEADS, tm, RW_HEAD_DIM), lambda b, i: (b, 0, i, 0))
    vec = lambda n: pl.BlockSpec((1, n), lambda b, i: (0, 0))
    rw_blk = 0
    return pl.pallas_call(
        _rwprep_kernel,
        out_shape=(hm,) * 7,
        grid=(bsz, t_len // tm),
        in_specs=[pl.BlockSpec((1, tm, RW_BLOCK), lambda b, i: (b, i, rw_blk)),
                  pl.BlockSpec((1, 8, RW_BLOCK), lambda b, i: (b, jnp.maximum(i * (tm // 8) - 1, 0), rw_blk)),
                  vec(RW_BLOCK),
                  pl.BlockSpec((RW_LORA, 3 * RW_WIDTH), lambda b, i: (0, 0)),
                  vec(RW_WIDTH), vec(RW_WIDTH), vec(RW_WIDTH), vec(RW_WIDTH)],
        out_specs=(hm_spec,) * 7,
        compiler_params=_cparams(("parallel", "parallel")),
        name="rwprep",
    )(z, z, mu_pad, w_lora, w0.reshape(1, -1), a0.reshape(1, -1), k_k.reshape(1, -1), k_a.reshape(1, -1))


def _rwscan_kernel(r_ref, lw_ref, k_ref, v_ref, a_ref, b_ref, g_ref, rk_ref, gg_ref, gb_ref, o_ref,
                   s_ref, rp_ref, y_ref, gm_ref, h0_ref, we_ref):
    cl = RW_CHUNK
    tb = r_ref.shape[2]
    n_chunk = tb // cl

    @pl.when(pl.program_id(1) == 0)
    def _():
        s_ref[...] = jnp.zeros_like(s_ref)

    n = RW_HEADS * cl
    ri = lax.broadcasted_iota(I32, (n, n), 0)
    ci = lax.broadcasted_iota(I32, (n, n), 1)
    same_head = (ri // cl) == (ci // cl)
    lower = same_head & (ri >= ci)
    strict = same_head & (ri > ci)
    eye = jnp.where(ri == ci, 1.0, 0.0)
    ones_lower = jnp.where(lower, 1.0, 0.0).astype(BF16)

    def prepare(chunks):
        grp = range(len(chunks))
        each = lambda fn: [fn(u) for u in grp]
        rows = [pl.ds(pl.multiple_of(c * cl, cl), cl) for c in chunks]
        stack = lambda ref: each(lambda u: ref[0, :, rows[u], :].reshape(n, RW_HEAD_DIM))
        r, lw, k, v, a, b = (stack(ref) for ref in (r_ref, lw_ref, k_ref, v_ref, a_ref, b_ref))
        hd = RW_HEAD_DIM
        parts = each(lambda u: jnp.concatenate(_split3(lw[u]), axis=-1))
        sums = each(lambda u: jnp.dot(ones_lower, parts[u], preferred_element_type=F32))
        cw = each(lambda u: sums[u][:, :hd] + sums[u][:, hd:2 * hd] + sums[u][:, 2 * hd:])
        w_in = each(lambda u: jnp.exp(cw[u]))
        w_inv = each(lambda u: jnp.exp(-cw[u]))
        rt = each(lambda u: r[u] * w_in[u])
        at = each(lambda u: a[u] * jnp.exp(cw[u] - lw[u]))
        kt = each(lambda u: k[u] * w_inv[u])
        bt = each(lambda u: b[u] * w_inv[u])
        w_end = each(lambda u: w_in[u].reshape(RW_HEADS, cl, RW_HEAD_DIM)[:, cl - 1:cl, :])
        w_end_rows = each(lambda u: jnp.broadcast_to(w_end[u], (RW_HEADS, cl, RW_HEAD_DIM)).reshape(n, RW_HEAD_DIM))
        a_ab = each(lambda u: jnp.where(strict, _mm_nt(at[u], bt[u]), 0.0))
        a_ak = each(lambda u: jnp.where(strict, _mm_nt(at[u], kt[u]), 0.0))
        m_rb = each(lambda u: jnp.where(lower, _mm_nt(rt[u], bt[u]), 0.0))
        m_rk = each(lambda u: jnp.where(lower, _mm_nt(rt[u], kt[u]), 0.0))
        inv = each(lambda u: eye + a_ab[u])
        p = a_ab
        for _ in range(cl.bit_length() - 2):
            p = [_mm(p[u], p[u]) for u in grp]
            inv = [inv[u] + _mm(inv[u], p[u]) for u in grp]
        akv = each(lambda u: _mm(a_ak[u], v[u]))
        apz = each(lambda u: _mm(inv[u], jnp.concatenate([at[u], akv[u]], axis=-1)).astype(BF16))
        mix = each(lambda u: jnp.dot(m_rb[u].astype(BF16), apz[u], preferred_element_type=F32))
        bend = each(lambda u: bt[u] * w_end_rows[u])
        kend = each(lambda u: kt[u] * w_end_rows[u])
        rp = each(lambda u: (rt[u] + mix[u][:, :hd]).astype(BF16))
        y0 = each(lambda u: mix[u][:, hd:] + _mm(m_rk[u], v[u]))
        for u in grp:
            for h in range(RW_HEADS):
                hs = slice(h * cl, (h + 1) * cl)
                both = _mm_tn(apz[u][hs], bend[u][hs])
                rp_ref[h, rows[u], :] = rp[u][hs]
                y_ref[h, rows[u], :] = y0[u][hs]
                gm_ref[h, rows[u], :] = both[:hd].astype(BF16)
                h0_ref[h, rows[u], :] = both[hd:] + _mm_tn(v[u][hs], kend[u][hs])
                we_ref[h, chunks[u]] = w_end[u][h]

    def prepare_step(i, carry):
        prepare([i * RW_PREP_UNROLL + u for u in range(RW_PREP_UNROLL)])
        return carry

    lax.fori_loop(0, n_chunk // RW_PREP_UNROLL, prepare_step, 0)

    def advance(c, carry):
        rows = pl.ds(pl.multiple_of(c * cl, cl), cl)
        for h in range(RW_HEADS):
            s = s_ref[h]
            sb = s.astype(BF16)
            y_ref[h, rows, :] = y_ref[h, rows, :] + lax.dot_general(
                rp_ref[h, rows, :], sb, (((1,), (1,)), ((), ())), preferred_element_type=F32)
            s_ref[h] = (s * we_ref[h, c] + jnp.dot(sb, gm_ref[h, rows, :], preferred_element_type=F32)
                        + h0_ref[h, rows, :])
        return carry

    lax.fori_loop(0, n_chunk, advance, 0)

    for h in range(RW_HEADS):
        y = y_ref[h]
        mu = jnp.mean(y, axis=-1, keepdims=True)
        yc = y - mu
        var = jnp.mean(yc * yc, axis=-1, keepdims=True)
        yn = yc * lax.rsqrt(var + RW_GN_EPS) * gg_ref[h] + gb_ref[h]
        v = v_ref[0, h]
        bonus = jnp.sum(r_ref[0, h] * k_ref[0, h] * rk_ref[h], axis=-1, keepdims=True) * v
        o_ref[0, h] = ((yn + bonus) * g_ref[0, h]).astype(o_ref.dtype)


def _rwscan(r, lw, k, v, a, b, g, r_k, gn_gain, gn_bias):
    bsz, _, t_len, _ = r.shape
    tb = min(512, t_len)
    hm_spec = pl.BlockSpec((1, RW_HEADS, tb, RW_HEAD_DIM), lambda bi, i: (bi, 0, i, 0))
    par = pl.BlockSpec((RW_HEADS, 1, RW_HEAD_DIM), lambda bi, i: (0, 0, 0))
    hshape = (RW_HEADS, 1, RW_HEAD_DIM)
    return pl.pallas_call(
        _rwscan_kernel,
        out_shape=jax.ShapeDtypeStruct((bsz, RW_HEADS, t_len, RW_HEAD_DIM), BF16),
        grid=(bsz, t_len // tb),
        in_specs=[hm_spec] * 7 + [par] * 3,
        out_specs=hm_spec,
        scratch_shapes=[pltpu.VMEM((RW_HEADS, RW_HEAD_DIM, RW_HEAD_DIM), F32),
                        pltpu.VMEM((RW_HEADS, tb, RW_HEAD_DIM), BF16), pltpu.VMEM((RW_HEADS, tb, RW_HEAD_DIM), F32),
                        pltpu.VMEM((RW_HEADS, tb, RW_HEAD_DIM), BF16), pltpu.VMEM((RW_HEADS, tb, RW_HEAD_DIM), F32),
                        pltpu.VMEM((RW_HEADS, tb // RW_CHUNK, 1, RW_HEAD_DIM), F32)],
        compiler_params=_cparams(("parallel", "arbitrary")),
        name="rwscan",
    )(r, lw, k, v, a, b, g, r_k.reshape(hshape), gn_gain.reshape(hshape), gn_bias.reshape(hshape))


FOX_PAIRS = FOX_HEADS // 2
FOX_EXTRA = 3
FOX_ACC_ROWS = FOX_HEAD_DIM + 16
FOX_LOOKAHEAD = 2


def _fox_bias_selector():
    sel = np.zeros((LANES, 2 * FOX_HEADS * LANES), np.float32)
    for h in range(FOX_HEADS):
        base = FOX_HEAD_DIM if h % 2 == 0 else 0
        for p in range(FOX_EXTRA):
            sel[p * FOX_HEADS + h, h * LANES + base + p] = 1.0
            sel[p * FOX_HEADS + h, (FOX_HEADS + h) * LANES + base + FOX_EXTRA + p] = -1.0
    return sel


def _foxprep_kernel(z_ref, f_ref, fb_ref, qg_ref, kg_ref, sel_ref, q_o, k_o, vt_o, carry_ref):
    tm = z_ref.shape[1]

    @pl.when(pl.program_id(1) == 0)
    def _():
        carry_ref[...] = jnp.zeros_like(carry_ref)

    log_f = _log_sigmoid(f_ref[0] + fb_ref[...])
    cum = carry_ref[...] + _tri_cumsum(log_f, tm)
    carry_ref[...] = cum[tm - 1:tm, :]
    lane = lax.broadcasted_iota(I32, (tm, LANES), 1)
    hi, mid, lo = (p.astype(F32) for p in _split3(cum * LOG2E))
    packed = jnp.where(lane < FOX_HEADS, hi,
                       jnp.where(lane < 2 * FOX_HEADS, pltpu.roll(mid, FOX_HEADS, axis=1),
                                 pltpu.roll(lo, 2 * FOX_HEADS, axis=1)))
    packed = jnp.where(lane < FOX_EXTRA * FOX_HEADS, packed, 0.0).astype(BF16)
    extra = jnp.dot(packed, sel_ref[...], preferred_element_type=F32)

    left = lane < FOX_HEAD_DIM
    in_half = lane % FOX_HEAD_DIM
    ones_q = jnp.where((in_half >= FOX_EXTRA) & (in_half < 2 * FOX_EXTRA), 1.0, 0.0)
    ones_k = jnp.where(in_half < FOX_EXTRA, 1.0, 0.0)

    def normed(block, gain):
        sq = block * block
        s_left = jnp.sum(jnp.where(left, sq, 0.0), axis=-1, keepdims=True)
        s_right = jnp.sum(jnp.where(left, 0.0, sq), axis=-1, keepdims=True)
        ms = jnp.where(left, s_left, s_right) * (1.0 / FOX_HEAD_DIM)
        return block * lax.rsqrt(ms + EPS) * gain

    for j in range(FOX_PAIRS):
        qn = normed(z_ref[0, :, j * LANES:(j + 1) * LANES].astype(F32), qg_ref[...] * (ATTN_SCALE * LOG2E))
        kn = normed(z_ref[0, :, FOX_WIDTH + j * LANES:FOX_WIDTH + (j + 1) * LANES].astype(F32), kg_ref[...])
        for par in range(2):
            h = 2 * j + par
            own = left if par == 0 else jnp.logical_not(left)
            q_o[0, h] = jnp.where(own, qn, extra[:, h * LANES:(h + 1) * LANES] + ones_q).astype(BF16)
            k_o[0, h] = jnp.where(own, kn, extra[:, (FOX_HEADS + h) * LANES:(FOX_HEADS + h + 1) * LANES]
                                  + ones_k).astype(BF16)
    ri = lax.broadcasted_iota(I32, (FOX_WIDTH, FOX_WIDTH), 0)
    ci = lax.broadcasted_iota(I32, (FOX_WIDTH, FOX_WIDTH), 1)
    eye = jnp.where(ri == ci, 1.0, 0.0).astype(BF16)
    v = z_ref[0, :, 2 * FOX_WIDTH:3 * FOX_WIDTH].astype(BF16)
    vt_o[0] = lax.dot_general(eye, v, (((1,), (1,)), ((), ())), preferred_element_type=F32).astype(BF16)


def _foxprep(z, z_rw, f_bias_pad, q_gain, k_gain):
    bsz, t_len, _ = z.shape
    tm = min(512, t_len)
    qk = jax.ShapeDtypeStruct((bsz, FOX_HEADS, t_len, LANES), BF16)
    qk_spec = pl.BlockSpec((1, FOX_HEADS, tm, LANES), lambda b, i: (b, 0, i, 0))
    sel = jnp.asarray(_fox_bias_selector(), BF16)
    return pl.pallas_call(
        _foxprep_kernel,
        out_shape=(qk, qk, jax.ShapeDtypeStruct((bsz, FOX_WIDTH, t_len), BF16)),
        grid=(bsz, t_len // tm),
        in_specs=[pl.BlockSpec((1, tm, 3 * FOX_WIDTH), lambda b, i: (b, i, Z_FOX // (3 * FOX_WIDTH))),
                  pl.BlockSpec((1, tm, LANES), lambda b, i: (b, i, (Z_F - Z_RW) // LANES)),
                  pl.BlockSpec((1, LANES), lambda b, i: (0, 0)),
                  pl.BlockSpec((1, LANES), lambda b, i: (0, 0)),
                  pl.BlockSpec((1, LANES), lambda b, i: (0, 0)),
                  pl.BlockSpec(sel.shape, lambda b, i: (0, 0))],
        out_specs=(qk_spec, qk_spec, pl.BlockSpec((1, FOX_WIDTH, tm), lambda b, i: (b, 0, i))),
        scratch_shapes=[pltpu.VMEM((1, LANES), F32)],
        compiler_params=_cparams(("parallel", "arbitrary")),
        name="foxprep",
    )(z, z_rw, f_bias_pad, jnp.tile(q_gain.reshape(1, -1), (1, 2)), jnp.tile(k_gain.reshape(1, -1), (1, 2)), sel)


def _fox_kernel(qi_ref, kj_ref, q_ref, k_ref, vt_ref, o_ref, m_ref, acc_ref):
    i = qi_ref[pl.program_id(1)]
    j = kj_ref[pl.program_id(1)]
    tq = q_ref.shape[2]
    tk = k_ref.shape[2]
    sub = 8

    @pl.when(j == 0)
    def _():
        m_ref[...] = jnp.full_like(m_ref, MASK_VALUE)
        acc_ref[...] = jnp.zeros_like(acc_ref)

    ones_rows = jnp.ones((FOX_ACC_ROWS - FOX_HEAD_DIM, tk), BF16)

    def scores(h):
        return lax.dot_general(k_ref[0, h], q_ref[0, h], (((1,), (1,)), ((), ())), preferred_element_type=F32)

    def update(diagonal):
        if diagonal:
            key = lax.broadcasted_iota(I32, (tk, tq), 0)
            qry = lax.broadcasted_iota(I32, (tk, tq), 1)
            keep = key <= qry
        ahead = [scores(h) for h in range(FOX_LOOKAHEAD)]
        for h in range(FOX_HEADS):
            s = ahead.pop(0)
            if h + FOX_LOOKAHEAD < FOX_HEADS:
                ahead.append(scores(h + FOX_LOOKAHEAD))
            if diagonal:
                s = jnp.where(keep, s, MASK_VALUE)
            s3 = s.reshape(tk // sub, sub, tq)
            m_prev = m_ref[h]
            m_cur = jnp.max(jnp.max(s3, axis=0), axis=0, keepdims=True)
            m_new = jnp.maximum(m_prev, m_cur)
            alpha = jnp.exp2(m_prev - m_new)
            p = jnp.exp2(s3 - m_new[None]).astype(BF16).reshape(tk, tq)
            lhs = jnp.concatenate([vt_ref[0, h * FOX_HEAD_DIM:(h + 1) * FOX_HEAD_DIM, :], ones_rows], axis=0)
            pv = jnp.dot(lhs, p, preferred_element_type=F32)
            acc = acc_ref[h].reshape(FOX_ACC_ROWS // sub, sub, tq) * alpha[None]
            acc_ref[h] = acc.reshape(FOX_ACC_ROWS, tq) + pv
            m_ref[h] = m_new

    @pl.when(j < i)
    def _():
        update(False)

    @pl.when(j == i)
    def _():
        update(True)
        outs = []
        for h in range(FOX_HEADS):
            acc = acc_ref[h]
            outs.append((acc[:FOX_HEAD_DIM] / acc[FOX_HEAD_DIM:FOX_HEAD_DIM + 1]).astype(BF16))
        out_t = jnp.concatenate(outs, axis=0)
        ri = lax.broadcasted_iota(I32, (tq, tq), 0)
        ci = lax.broadcasted_iota(I32, (tq, tq), 1)
        eye = jnp.where(ri == ci, 1.0, 0.0).astype(BF16)
        o_ref[0] = lax.dot_general(eye, out_t, (((1,), (1,)), ((), ())),
                                   preferred_element_type=F32).astype(o_ref.dtype)


def _fox(q, k, vt):
    bsz, _, t_len, _ = q.shape
    tq = min(512, t_len)
    n_blk = t_len // tq
    pairs = [(i, j) for i in range(n_blk) for j in range(i + 1)]
    qi = jnp.asarray([p[0] for p in pairs], I32)
    kj = jnp.asarray([p[1] for p in pairs], I32)
    grid_spec = pltpu.PrefetchScalarGridSpec(
        num_scalar_prefetch=2,
        grid=(bsz, len(pairs)),
        in_specs=[pl.BlockSpec((1, FOX_HEADS, tq, LANES), lambda b, s, qi, kj: (b, 0, qi[s], 0)),
                  pl.BlockSpec((1, FOX_HEADS, tq, LANES), lambda b, s, qi, kj: (b, 0, kj[s], 0)),
                  pl.BlockSpec((1, FOX_WIDTH, tq), lambda b, s, qi, kj: (b, 0, kj[s]))],
        out_specs=pl.BlockSpec((1, tq, FOX_WIDTH), lambda b, s, qi, kj: (b, qi[s], 0)),
        scratch_shapes=[pltpu.VMEM((FOX_HEADS, 8, tq), F32), pltpu.VMEM((FOX_HEADS, FOX_ACC_ROWS, tq), F32)],
    )
    return pl.pallas_call(
        _fox_kernel,
        out_shape=jax.ShapeDtypeStruct((bsz, t_len, FOX_WIDTH), BF16),
        grid_spec=grid_spec,
        compiler_params=_cparams(("parallel", "arbitrary")),
        name="fox",
    )(qi, kj, q, k, vt)


def _merge_kernel(zg_ref, ygm_ref, yrw_ref, yfox_ref, x_ref, g1_ref, sc2_ref, sh2_ref, pb_ref, wo_ref, wr_ref, br_ref,
                  x1_o, h2_o, idx_o, gate_o, rank_o, cnt_o, carry_ref):
    tm = x_ref.shape[1]

    @pl.when((pl.program_id(0) == 0) & (pl.program_id(1) == 0))
    def _():
        carry_ref[...] = jnp.zeros_like(carry_ref)

    sg = 0.5 * jnp.tanh(0.5 * zg_ref[0].astype(F32)) + 0.5
    p_gm = jnp.dot(ygm_ref[0], pb_ref[0:GM_WIDTH, :], preferred_element_type=F32)
    y_rw = jnp.concatenate([yrw_ref[0, h] for h in range(RW_HEADS)], axis=-1)
    p_rw = jnp.dot(y_rw, pb_ref[GM_WIDTH:GM_WIDTH + RW_WIDTH, :], preferred_element_type=F32)
    p_fox = jnp.dot(yfox_ref[0], pb_ref[GM_WIDTH + RW_WIDTH:, :], preferred_element_type=F32)
    merged = sg[:, 0:D_MODEL] * p_gm + sg[:, D_MODEL:2 * D_MODEL] * p_rw + sg[:, 2 * D_MODEL:] * p_fox
    x1 = x_ref[0] + g1_ref[0] * jnp.dot(merged.astype(BF16), wo_ref[...], preferred_element_type=F32)
    x1_o[0] = x1
    h2 = x1 * lax.rsqrt(jnp.mean(x1 * x1, axis=-1, keepdims=True) + EPS) * (1.0 + sc2_ref[0]) + sh2_ref[0]
    h2_o[0] = _pack_halves(h2)

    h_hi, h_lo, _ = _split3(h2)
    w_hi, w_lo, _ = _split3(wr_ref[...])
    logits = (jnp.dot(h_hi, w_hi, preferred_element_type=F32) + jnp.dot(h_hi, w_lo, preferred_element_type=F32)
              + jnp.dot(h_lo, w_hi, preferred_element_type=F32)) + br_ref[...]
    lane = lax.broadcasted_iota(I32, (tm, N_EXPERTS), 1)
    vals, idxs = [], []
    rest = logits
    for _ in range(TOP_K):
        m = jnp.max(rest, axis=-1, keepdims=True)
        am = jnp.min(jnp.where(rest == m, lane, N_EXPERTS), axis=-1, keepdims=True)
        vals.append(m)
        idxs.append(am)
        rest = jnp.where(lane == am, -jnp.inf, rest)
    exps = [jnp.exp(val - vals[0]) for val in vals]
    denom = exps[0] + exps[1] + exps[2] + exps[3]

    onehot = jnp.zeros((tm, N_EXPERTS), F32)
    for am in idxs:
        onehot = onehot + jnp.where(lane == am, 1.0, 0.0)
    ri = lax.broadcasted_iota(I32, (tm, tm), 0)
    ci = lax.broadcasted_iota(I32, (tm, tm), 1)
    before = jnp.where(ri > ci, 1.0, 0.0).astype(BF16)
    seen = carry_ref[...] + jnp.dot(before, onehot.astype(BF16), preferred_element_type=F32)
    lane_k = lax.broadcasted_iota(I32, (tm, TOP_K), 1)
    idx_out = jnp.zeros((tm, TOP_K), I32)
    gate_out = jnp.zeros((tm, TOP_K), F32)
    rank_out = jnp.zeros((tm, TOP_K), I32)
    for kk in range(TOP_K):
        rank = jnp.sum(jnp.where(lane == idxs[kk], seen, 0.0), axis=-1, keepdims=True).astype(I32)
        idx_out = jnp.where(lane_k == kk, idxs[kk], idx_out)
        gate_out = jnp.where(lane_k == kk, exps[kk] / denom, gate_out)
        rank_out = jnp.where(lane_k == kk, rank, rank_out)
    idx_o[0] = idx_out
    gate_o[0] = gate_out
    rank_o[0] = rank_out
    total = carry_ref[...] + jnp.sum(onehot, axis=0, keepdims=True)
    carry_ref[...] = total
    cnt_o[...] = total.astype(I32)


def _merge(z, y_gm, y_rw, y_fox, x, gate1, scale2, shift2, w_branch, w_o, w_router, b_router, layer):
    bsz, t_len, d = x.shape
    tm = min(512, t_len)
    row = lambda w: pl.BlockSpec((1, tm, w), lambda b, i: (b, i, 0))
    mod = pl.BlockSpec((1, 1, d), lambda b, i: (b, 0, 0))
    full = lambda shape: pl.BlockSpec(shape, lambda b, i: (0,) * len(shape))
    return pl.pallas_call(
        _merge_kernel,
        out_shape=(jax.ShapeDtypeStruct((bsz, t_len, d), F32), jax.ShapeDtypeStruct((bsz, t_len, d // 2), I32),
                   jax.ShapeDtypeStruct((bsz, t_len, TOP_K), I32), jax.ShapeDtypeStruct((bsz, t_len, TOP_K), F32),
                   jax.ShapeDtypeStruct((bsz, t_len, TOP_K), I32), jax.ShapeDtypeStruct((1, N_EXPERTS), I32)),
        grid=(bsz, t_len // tm),
        in_specs=[row(N_BRANCH * D_MODEL), row(GM_WIDTH),
                  pl.BlockSpec((1, RW_HEADS, tm, RW_HEAD_DIM), lambda b, i: (b, 0, i, 0)),
                  row(FOX_WIDTH), row(d), mod, mod, mod,
                  pl.BlockSpec((MIX_WIDTH, d), lambda b, i: (layer, 0)), pl.BlockSpec((d, d), lambda b, i: (layer, 0)),
                  full(w_router.shape), full((1, N_EXPERTS))],
        out_specs=(row(d), row(d // 2), row(TOP_K), row(TOP_K), row(TOP_K), full((1, N_EXPERTS))),
        scratch_shapes=[pltpu.VMEM((1, N_EXPERTS), F32)],
        compiler_params=_cparams(("arbitrary", "arbitrary")),
        name="merge_router",
    )(z, y_gm, y_rw, y_fox, x, gate1, scale2, shift2, w_branch, w_o, w_router, b_router.reshape(1, N_EXPERTS))


def _sc_mesh():
    return plsc.VectorSubcoreMesh(core_axis_name="c", subcore_axis_name="s",
                                  num_cores=SC_CORES, num_subcores=SC_SUBCORES)


def _sc_worker():
    return lax.axis_index("s") * SC_CORES + lax.axis_index("c")


def _sc_scatter_rows(src, idx3, n_out):
    _, d = src.shape
    n_copy, n_grp, _ = idx3.shape
    grp_per_w = n_grp // SC_WORKERS
    assert grp_per_w % 2 == 0

    def body(src_hbm, idx_hbm, out_hbm, idx_v, rows_a, rows_b, sem):
        g0 = _sc_worker() * grp_per_w
        for q in range(n_copy):
            pltpu.sync_copy(idx_hbm.at[q, pl.ds(g0, grp_per_w)], idx_v.at[pl.ds(q * grp_per_w, grp_per_w)])

        @pl.loop(0, grp_per_w, step=2)
        def _(j):
            read_a = pltpu.async_copy(src_hbm.at[pl.ds((g0 + j) * SC_ROWS, SC_ROWS)], rows_a, sem.at[0])
            read_b = pltpu.async_copy(src_hbm.at[pl.ds((g0 + j + 1) * SC_ROWS, SC_ROWS)], rows_b, sem.at[1])
            read_a.wait()
            put_a = [pltpu.async_copy(rows_a, out_hbm.at[idx_v.at[q * grp_per_w + j]], sem.at[2])
                     for q in range(n_copy)]
            read_b.wait()
            put_b = [pltpu.async_copy(rows_b, out_hbm.at[idx_v.at[q * grp_per_w + j + 1]], sem.at[3])
                     for q in range(n_copy)]
            for cp in put_a + put_b:
                cp.wait()

    return pl.kernel(
        body, out_type=jax.ShapeDtypeStruct((n_out, d), src.dtype), mesh=_sc_mesh(),
        scratch_types=[pltpu.VMEM((n_copy * grp_per_w, SC_ROWS), I32), pltpu.VMEM((SC_ROWS, d), src.dtype),
                       pltpu.VMEM((SC_ROWS, d), src.dtype), pltpu.SemaphoreType.DMA((4,))],
        name="sc_dispatch",
    )(src, idx3)


def _sc_gather_rows(table, idx2):
    _, d = table.shape
    n_grp, _ = idx2.shape
    grp_per_w = n_grp // SC_WORKERS
    assert grp_per_w % 2 == 0

    def body(table_hbm, idx_hbm, out_hbm, idx_v, rows_a, rows_b, sem):
        g0 = _sc_worker() * grp_per_w
        pltpu.sync_copy(idx_hbm.at[pl.ds(g0, grp_per_w)], idx_v)

        @pl.loop(0, grp_per_w, step=2)
        def _(j):
            get_a = pltpu.async_copy(table_hbm.at[idx_v.at[j]], rows_a, sem.at[0])
            get_b = pltpu.async_copy(table_hbm.at[idx_v.at[j + 1]], rows_b, sem.at[1])
            get_a.wait()
            put_a = pltpu.async_copy(rows_a, out_hbm.at[pl.ds((g0 + j) * SC_ROWS, SC_ROWS)], sem.at[2])
            get_b.wait()
            put_b = pltpu.async_copy(rows_b, out_hbm.at[pl.ds((g0 + j + 1) * SC_ROWS, SC_ROWS)], sem.at[3])
            put_a.wait()
            put_b.wait()

    return pl.kernel(
        body, out_type=jax.ShapeDtypeStruct((n_grp * SC_ROWS, d), table.dtype), mesh=_sc_mesh(),
        scratch_types=[pltpu.VMEM((grp_per_w, SC_ROWS), I32), pltpu.VMEM((SC_ROWS, d), table.dtype),
                       pltpu.VMEM((SC_ROWS, d), table.dtype), pltpu.SemaphoreType.DMA((4,))],
        name="sc_combine_gather",
    )(table, idx2)


def _ffn_weight_copies(expert, wgu_hbm, wd_hbm, stage_gu, stage_d, sem):
    return (pltpu.make_async_copy(wgu_hbm.at[expert], stage_gu, sem.at[0]),
            pltpu.make_async_copy(wd_hbm.at[expert], stage_d, sem.at[1]))


def _ffn_kernel(be_ref, first_ref, nxt_ref, nv_ref, x_ref, wgu_hbm, wd_hbm, bgu_ref, bd_ref, o_ref,
                stage_gu, stage_d, wgu_b, wd_b, sem):
    step = pl.program_id(0)
    copies = functools.partial(_ffn_weight_copies, wgu_hbm=wgu_hbm, wd_hbm=wd_hbm, stage_gu=stage_gu,
                               stage_d=stage_d, sem=sem)

    @pl.when(step == 0)
    def _():
        for cp in copies(be_ref[0]):
            cp.start()

    for b in range(FFN_STEP_BLOCKS):
        idx = step * FFN_STEP_BLOCKS + b
        expert = be_ref[idx]
        n_valid = nv_ref[idx]
        rows = slice(b * MOE_BLOCK, (b + 1) * MOE_BLOCK)

        @pl.when(first_ref[idx] == 1)
        def _():
            for cp in copies(expert):
                cp.wait()
            wgu_b[...] = stage_gu[...].astype(BF16)
            wd_b[...] = stage_d[...].astype(BF16)

            @pl.when(nxt_ref[idx] >= 0)
            def _():
                for cp in copies(nxt_ref[idx]):
                    cp.start()

        @pl.when(n_valid > 0)
        def _():
            rowid = lax.broadcasted_iota(I32, (MOE_BLOCK, x_ref.shape[1]), 0)
            xp = jnp.where(rowid < n_valid, x_ref[rows, :], 0)
            x = jnp.concatenate(_unpack_halves(xp), axis=-1).astype(BF16)
            gu = jnp.dot(x, wgu_b[...], preferred_element_type=F32) + bgu_ref[expert]
            g_ = jnp.minimum(gu[:, :D_FF], SWIGLU_LIMIT)
            u_ = jnp.clip(gu[:, D_FF:], -SWIGLU_LIMIT, SWIGLU_LIMIT)
            act = (u_ + 1.0) * (g_ * jax.nn.sigmoid(SWIGLU_ALPHA * g_))
            y = jnp.dot(act.astype(BF16), wd_b[...], preferred_element_type=F32) + bd_ref[expert]
            o_ref[rows, :] = _pack_halves(y)

        @pl.when(n_valid <= 0)
        def _():
            o_ref[rows, :] = jnp.zeros((MOE_BLOCK, o_ref.shape[1]), o_ref.dtype)


def _ffn(block_expert, block_first, block_next, block_valid, xin, w_gate_up, b_gate_up, w_down, b_down):
    n_rows, dp = xin.shape
    d = 2 * dp
    step_rows = FFN_STEP_BLOCKS * MOE_BLOCK
    resident = lambda arr: pl.BlockSpec(arr.shape, lambda i, *_: (0,) * arr.ndim)
    grid_spec = pltpu.PrefetchScalarGridSpec(
        num_scalar_prefetch=4,
        grid=(n_rows // step_rows,),
        in_specs=[pl.BlockSpec((step_rows, dp), lambda i, *_: (i, 0)),
                  pl.BlockSpec(memory_space=pl.ANY), pl.BlockSpec(memory_space=pl.ANY),
                  resident(b_gate_up), resident(b_down)],
        out_specs=pl.BlockSpec((step_rows, dp), lambda i, *_: (i, 0)),
        scratch_shapes=[pltpu.VMEM((d, 2 * D_FF), F32), pltpu.VMEM((D_FF, d), F32),
                        pltpu.VMEM((d, 2 * D_FF), BF16), pltpu.VMEM((D_FF, d), BF16),
                        pltpu.SemaphoreType.DMA((2,))],
    )
    return pl.pallas_call(
        _ffn_kernel,
        out_shape=jax.ShapeDtypeStruct((n_rows, dp), I32),
        grid_spec=grid_spec,
        compiler_params=pltpu.CompilerParams(dimension_semantics=("arbitrary",), vmem_limit_bytes=FFN_VMEM_LIMIT),
        name="expert_ffn",
    )(block_expert, block_first, block_next, block_valid, xin, w_gate_up, w_down, b_gate_up, b_down)


def _combine_kernel(x1_ref, g2_ref, gate_ref, yg_ref, o_ref):
    gate = gate_ref[0]
    y_lo = y_hi = None
    for q in range(TOP_K):
        lo, hi = _unpack_halves(yg_ref[q, 0])
        wq = gate[:, q:q + 1]
        y_lo = wq * lo if y_lo is None else y_lo + wq * lo
        y_hi = wq * hi if y_hi is None else y_hi + wq * hi
    o_ref[0] = x1_ref[0] + g2_ref[0] * jnp.concatenate([y_lo, y_hi], axis=-1)


def _combine(x1, gate2, gate, yg):
    bsz, t_len, d = x1.shape
    tm = min(512, t_len)
    return pl.pallas_call(
        _combine_kernel,
        out_shape=jax.ShapeDtypeStruct((bsz, t_len, d), F32),
        grid=(bsz, t_len // tm),
        in_specs=[pl.BlockSpec((1, tm, d), lambda b, i: (b, i, 0)),
                  pl.BlockSpec((1, 1, d), lambda b, i: (b, 0, 0)),
                  pl.BlockSpec((1, tm, TOP_K), lambda b, i: (b, i, 0)),
                  pl.BlockSpec((TOP_K, 1, tm, d // 2), lambda b, i: (0, b, i, 0))],
        out_specs=pl.BlockSpec((1, tm, d), lambda b, i: (b, i, 0)),
        compiler_params=_cparams(("parallel", "parallel")),
        name="moe_combine",
    )(x1, gate2, gate, yg)


def _moe(x1, gate2, h2, top_idx, gate, rank, counts, w_gate_up, b_gate_up, w_down, b_down, layer):
    bsz, t_len, d = h2.shape
    n_tok = bsz * t_len
    n_assign = n_tok * TOP_K
    n_blocks = -(-n_assign // MOE_BLOCK) + N_EXPERTS
    counts = counts.reshape(N_EXPERTS)
    blocks_e = (counts + MOE_BLOCK - 1) // MOE_BLOCK
    blk_end = jnp.cumsum(blocks_e)
    blk_start = blk_end - blocks_e
    experts = jnp.arange(N_EXPERTS, dtype=I32)
    onehot = top_idx.reshape(n_tok, TOP_K, 1) == experts
    dest = jnp.sum(jnp.where(onehot, blk_start * MOE_BLOCK, 0), axis=-1) + rank.reshape(n_tok, TOP_K)
    dest_t = dest.T.astype(I32)
    blk = jnp.arange(n_blocks, dtype=I32)
    block_expert = jnp.minimum(jnp.sum(blk_end[None, :] <= blk[:, None], axis=1), N_EXPERTS - 1).astype(I32)
    be_hot = block_expert[:, None] == experts
    cnt_b = jnp.sum(jnp.where(be_hot, counts, 0), axis=1)
    start_b = jnp.sum(jnp.where(be_hot, blk_start, 0), axis=1)
    block_valid = jnp.clip(cnt_b - (blk - start_b) * MOE_BLOCK, 0, MOE_BLOCK).astype(I32)
    xin = _sc_scatter_rows(h2.reshape(n_tok, d), dest_t.reshape(TOP_K, n_tok // SC_ROWS, SC_ROWS),
                           n_blocks * MOE_BLOCK)
    block_first = jnp.concatenate([jnp.ones((1,), I32), (block_expert[1:] != block_expert[:-1]).astype(I32)])
    run_start = jnp.where(block_first == 1, blk, n_blocks)
    later_start = lax.cummin(jnp.concatenate([run_start[1:], jnp.full((1,), n_blocks, I32)]), reverse=True)
    next_expert = jnp.concatenate([block_expert, jnp.full((1,), -1 - layer * N_EXPERTS, I32)])[later_start]
    yb = _ffn(block_expert + layer * N_EXPERTS, block_first, next_expert + layer * N_EXPERTS, block_valid, xin,
              w_gate_up, b_gate_up, w_down, b_down)
    yg = _sc_gather_rows(yb, dest_t.reshape(n_assign // SC_ROWS, SC_ROWS))
    return _combine(x1, gate2, gate, yg.reshape(TOP_K, bsz, t_len, d))


def _permute_kernel(w_ref, o_ref):
    o_gm = 0
    o_rw = o_gm + 2 * GM_WIDTH
    o_fox = o_rw + RW_SHIFT_WIDTH
    o_f = o_fox + 3 * FOX_WIDTH
    o_gate = o_f + FOX_HEADS
    w = w_ref[0]
    o_ref[0, :, Z_GATE:Z_FOX] = w[:, o_gate:o_gate + N_BRANCH * D_MODEL].astype(BF16)
    o_ref[0, :, Z_FOX:Z_GM] = w[:, o_fox:o_f].astype(BF16)
    o_ref[0, :, Z_GM:Z_RW] = w[:, o_gm:o_rw].astype(BF16)
    o_ref[0, :, Z_RW:Z_F] = w[:, o_rw:o_fox].astype(BF16)
    tail = jnp.concatenate([w[:, o_f:o_gate], jnp.zeros((w.shape[0], Z_WIDTH - Z_F - FOX_HEADS), F32)], axis=-1)
    o_ref[0, :, Z_F:Z_WIDTH] = tail.astype(BF16)


def _permute_w_in(w_in):
    n_layer, d, w_cols = w_in.shape
    tr = 256
    return pl.pallas_call(
        _permute_kernel,
        out_shape=jax.ShapeDtypeStruct((n_layer, d, Z_WIDTH), BF16),
        grid=(n_layer, d // tr),
        in_specs=[pl.BlockSpec((1, tr, w_cols), lambda l, i: (l, i, 0))],
        out_specs=pl.BlockSpec((1, tr, Z_WIDTH), lambda l, i: (l, i, 0)),
        compiler_params=_cparams(("parallel", "parallel")),
        name="permute_w_in",
    )(w_in)


def _layer(x, mod, w_in_p, gm_v_gain, gm_w_s, gm_b_s, mu_pad, w_lora, rw_w0, rw_a0, rw_k_k, rw_k_a, rw_r_k,
           rw_gn_gain, rw_gn_bias, f_bias_pad, fox_q_gain, fox_k_gain, w_branch, w_o, w_router, b_router,
           w_gate_up, b_gate_up, w_down, b_down, layer):
    shift1, scale1, gate1, shift2, scale2, gate2 = (mod[:, i][:, None, :] for i in range(6))
    z, z_rw = _inproj(x, scale1, shift1, w_in_p, layer)
    y_gm = _gmlp(z, gm_v_gain, gm_w_s, gm_b_s)
    r, lw, k, v, a, b, g = _rwprep(z_rw, mu_pad, w_lora, rw_w0, rw_a0, rw_k_k, rw_k_a)
    y_rw = _rwscan(r, lw, k, v, a, b, g, rw_r_k, rw_gn_gain, rw_gn_bias)
    q, kf, vf = _foxprep(z, z_rw, f_bias_pad, fox_q_gain, fox_k_gain)
    y_fox = _fox(q, kf, vf)
    x1, h2, top_idx, gate, rank, counts = _merge(z, y_gm, y_rw, y_fox, x, gate1, scale2, shift2,
                                                 w_branch, w_o, w_router, b_router, layer)
    return _moe(x1, gate2, h2, top_idx, gate, rank, counts, w_gate_up, b_gate_up, w_down, b_down, layer)


def kernel(x, c, w_ada, b_ada, w_in, gm_v_gain, gm_w_s, gm_b_s, rw_mu, rw_w0, rw_w2, rw_a0, rw_a2, rw_g2, rw_k_k,
           rw_k_a, rw_r_k, rw_gn_gain, rw_gn_bias, fox_f_bias, fox_q_gain, fox_k_gain, w_branch, w_o, w_router,
           b_router, w_gate_up, b_gate_up, w_down, b_down):
    n_layer = w_ada.shape[0]
    bsz = x.shape[0]
    c_pad = jnp.zeros((8, D_MODEL), F32).at[:bsz].set(c)
    mod = _adaln(c_pad, w_ada, b_ada)[:, :bsz].reshape(n_layer, bsz, 6, D_MODEL)
    w_in_p = _permute_w_in(w_in)
    mu_pad = jnp.pad(rw_mu, ((0, 0), (0, RW_BLOCK - RW_SHIFT_WIDTH)))
    w_lora = jnp.zeros((n_layer, RW_LORA, 3 * RW_WIDTH), F32)
    w_lora = w_lora.at[:, 0:RW_DECAY_LORA, 0:RW_WIDTH].set(rw_w2)
    w_lora = w_lora.at[:, RW_DECAY_LORA:RW_DECAY_LORA + RW_ICLR_LORA, RW_WIDTH:2 * RW_WIDTH].set(rw_a2)
    w_lora = w_lora.at[:, RW_DECAY_LORA + RW_ICLR_LORA:, 2 * RW_WIDTH:].set(rw_g2)
    f_bias_pad = jnp.pad(fox_f_bias, ((0, 0), (0, LANES - FOX_HEADS)))
    w_in_p = w_in_p.reshape(n_layer * D_MODEL, Z_WIDTH)
    w_branch_b = w_branch.astype(BF16).reshape(n_layer * MIX_WIDTH, D_MODEL)
    w_o_b = w_o.astype(BF16).reshape(n_layer * D_MODEL, D_MODEL)
    w_gu = w_gate_up.reshape(n_layer * N_EXPERTS, D_MODEL, 2 * D_FF)
    b_gu = b_gate_up.reshape(n_layer * N_EXPERTS, 1, 2 * D_FF)
    w_dn = w_down.reshape(n_layer * N_EXPERTS, D_FF, D_MODEL)
    b_dn = b_down.reshape(n_layer * N_EXPERTS, 1, D_MODEL)
    for l in range(n_layer):
        x = _layer(x, mod[l], w_in_p, gm_v_gain[l], gm_w_s[l], gm_b_s[l], mu_pad[l:l + 1], w_lora[l], rw_w0[l],
                   rw_a0[l], rw_k_k[l], rw_k_a[l], rw_r_k[l], rw_gn_gain[l], rw_gn_bias[l], f_bias_pad[l:l + 1],
                   fox_q_gain[l], fox_k_gain[l], w_branch_b, w_o_b, w_router[l], b_router[l],
                   w_gu, b_gu, w_dn, b_dn, l)
    return x
```

```python
import functools

import jax
import jax.numpy as jnp
import numpy as np
from jax import lax
from jax.experimental import pallas as pl
from jax.experimental.pallas import tpu as pltpu
from jax.experimental.pallas import tpu_sc as plsc

F32 = jnp.float32
BF16 = jnp.bfloat16
I32 = jnp.int32
HIGHEST = lax.Precision.HIGHEST

D_MODEL = 1024
GM_CHUNK = 128
GM_GROUPS = 4
GM_WIDTH = 256
GM_GROUP_DIM = GM_WIDTH // GM_GROUPS
RW_HEADS = 4
RW_HEAD_DIM = 64
RW_WIDTH = RW_HEADS * RW_HEAD_DIM
RW_DECAY_LORA = 32
RW_ICLR_LORA = 32
RW_GATE_LORA = 64
RW_LORA = RW_DECAY_LORA + RW_ICLR_LORA + RW_GATE_LORA
RW_SHIFT_WIDTH = 3 * RW_WIDTH + RW_LORA
RW_GN_EPS = 64e-5
FOX_HEADS = 8
FOX_HEAD_DIM = 64
FOX_WIDTH = FOX_HEADS * FOX_HEAD_DIM
ATTN_SCALE = FOX_HEAD_DIM ** -0.5
MASK_VALUE = -1e30
LOG2E = 1.4426950408889634
N_BRANCH = 3
MIX_WIDTH = GM_WIDTH + RW_WIDTH + FOX_WIDTH
N_EXPERTS = 32
TOP_K = 4
D_FF = D_MODEL
SWIGLU_LIMIT = 7.0
SWIGLU_ALPHA = 1.702
MOE_BLOCK = 256
EPS = 1e-6

Z_GATE = 0
Z_FOX = N_BRANCH * D_MODEL
Z_GM = Z_FOX + 3 * FOX_WIDTH
Z_RW = Z_GM + 2 * GM_WIDTH
RW_BLOCK = 1024
Z_F = Z_RW + RW_SHIFT_WIDTH
Z_WIDTH = Z_RW + RW_BLOCK
LANES = 128
RW_CHUNK = 64
RW_PREP_UNROLL = 4

VMEM_LIMIT = 48 * 1024 * 1024
FFN_VMEM_LIMIT = 56 * 1024 * 1024
FFN_STEP_BLOCKS = 8
SC_CORES = 2
SC_SUBCORES = 16
SC_WORKERS = SC_CORES * SC_SUBCORES
SC_ROWS = 64


def _cparams(sem):
    return pltpu.CompilerParams(dimension_semantics=sem, vmem_limit_bytes=VMEM_LIMIT)


def _mm(a, b):
    return jnp.dot(a.astype(BF16), b.astype(BF16), preferred_element_type=F32)


def _mm_nt(a, b):
    return lax.dot_general(a.astype(BF16), b.astype(BF16), (((1,), (1,)), ((), ())), preferred_element_type=F32)


def _mm_tn(a, b):
    return lax.dot_general(a.astype(BF16), b.astype(BF16), (((0,), (0,)), ((), ())), preferred_element_type=F32)


def _split3(x):
    hi = x.astype(BF16)
    r1 = x - hi.astype(F32)
    mid = r1.astype(BF16)
    lo = (r1 - mid.astype(F32)).astype(BF16)
    return hi, mid, lo


def _tri_cumsum(x, n):
    ri = lax.broadcasted_iota(I32, (n, n), 0)
    ci = lax.broadcasted_iota(I32, (n, n), 1)
    ones = jnp.where(ri >= ci, 1.0, 0.0).astype(BF16)
    hi, mid, lo = _split3(x)
    return (jnp.dot(ones, hi, preferred_element_type=F32) + jnp.dot(ones, mid, preferred_element_type=F32)
            + jnp.dot(ones, lo, preferred_element_type=F32))


def _pack_halves(x):
    w = x.shape[1] // 2
    hi = pltpu.bitcast(x[:, :w].astype(BF16).astype(F32), jnp.uint32)
    lo = pltpu.bitcast(x[:, w:].astype(BF16).astype(F32), jnp.uint32)
    return pltpu.bitcast(hi | (lo >> 16), I32)


def _unpack_halves(p):
    u = pltpu.bitcast(p, jnp.uint32)
    return pltpu.bitcast(u & jnp.uint32(0xFFFF0000), F32), pltpu.bitcast(u << 16, F32)


def _log_sigmoid(x):
    return jnp.minimum(x, 0.0) - jnp.log1p(jnp.exp(-jnp.abs(x)))


def _adaln_kernel(c_ref, w_ref, b_ref, o_ref):
    c = c_ref[...]
    s = c * jax.nn.sigmoid(c)
    o_ref[0] = jnp.dot(s, w_ref[0], preferred_element_type=F32, precision=HIGHEST) + b_ref[0]


def _adaln(c_pad, w_ada, b_ada):
    n_layer, d, w6 = w_ada.shape
    tn = 1536
    return pl.pallas_call(
        _adaln_kernel,
        out_shape=jax.ShapeDtypeStruct((n_layer, c_pad.shape[0], w6), F32),
        grid=(n_layer, w6 // tn),
        in_specs=[pl.BlockSpec(c_pad.shape, lambda l, j: (0, 0)),
                  pl.BlockSpec((1, d, tn), lambda l, j: (l, 0, j)),
                  pl.BlockSpec((1, 1, tn), lambda l, j: (l, 0, j))],
        out_specs=pl.BlockSpec((1, c_pad.shape[0], tn), lambda l, j: (l, 0, j)),
        compiler_params=_cparams(("parallel", "parallel")),
        name="adaln",
    )(c_pad, w_ada, b_ada.reshape(n_layer, 1, w6))


def _inproj_kernel(x_ref, sc_ref, sh_ref, w_ref, zm_ref, zr_ref, xn_ref):
    j = pl.program_id(2)

    @pl.when(j == 0)
    def _():
        x = x_ref[0]
        xn = x * lax.rsqrt(jnp.mean(x * x, axis=-1, keepdims=True) + EPS)
        xn_ref[...] = (xn * (1.0 + sc_ref[0]) + sh_ref[0]).astype(BF16)

    acc = jnp.dot(xn_ref[...], w_ref[...], preferred_element_type=F32)

    @pl.when(j < Z_RW // RW_BLOCK)
    def _():
        zm_ref[0] = acc.astype(BF16)

    @pl.when(j == Z_RW // RW_BLOCK)
    def _():
        zr_ref[0] = acc


def _inproj(x, scale, shift, w, layer):
    bsz, t_len, d = x.shape
    tm = min(1024, t_len)
    tn = RW_BLOCK
    n_main = Z_RW // tn
    return pl.pallas_call(
        _inproj_kernel,
        out_shape=(jax.ShapeDtypeStruct((bsz, t_len, Z_RW), BF16), jax.ShapeDtypeStruct((bsz, t_len, RW_BLOCK), F32)),
        grid=(bsz, t_len // tm, Z_WIDTH // tn),
        in_specs=[pl.BlockSpec((1, tm, d), lambda b, i, j: (b, i, 0)),
                  pl.BlockSpec((1, 1, d), lambda b, i, j: (b, 0, 0)),
                  pl.BlockSpec((1, 1, d), lambda b, i, j: (b, 0, 0)),
                  pl.BlockSpec((d, tn), lambda b, i, j: (layer, j))],
        out_specs=(pl.BlockSpec((1, tm, tn), lambda b, i, j: (b, i, jnp.minimum(j, n_main - 1))),
                   pl.BlockSpec((1, tm, tn), lambda b, i, j: (b, i, 0))),
        scratch_shapes=[pltpu.VMEM((tm, d), BF16)],
        compiler_params=_cparams(("parallel", "parallel", "arbitrary")),
        name="inproj",
    )(x, scale, shift, w)


def _gmlp_kernel(z_ref, gain_ref, ws_ref, bst_ref, o_ref):
    tm = z_ref.shape[1]
    z = z_ref[0].astype(F32)
    u = jax.nn.gelu(z[:, :GM_WIDTH])
    v = jax.nn.gelu(z[:, GM_WIDTH:])
    v = v * lax.rsqrt(jnp.mean(v * v, axis=-1, keepdims=True) + EPS) * gain_ref[...]
    vb = v.astype(BF16)
    grp = lax.broadcasted_iota(I32, (GM_CHUNK, GM_WIDTH), 1) // GM_GROUP_DIM
    ri = lax.broadcasted_iota(I32, (GM_CHUNK, GM_CHUNK), 0)
    ci = lax.broadcasted_iota(I32, (GM_CHUNK, GM_CHUNK), 1)
    causal = ri >= ci
    bias = jnp.zeros((GM_CHUNK, GM_WIDTH), F32)
    ws = []
    for g in range(GM_GROUPS):
        ws.append(jnp.where(causal, ws_ref[g], 0.0).astype(BF16))
        bias = jnp.where(grp == g, bst_ref[:, g:g + 1], bias)
    for c in range(tm // GM_CHUNK):
        rows = slice(c * GM_CHUNK, (c + 1) * GM_CHUNK)
        vc = vb[rows]
        mixed = bias
        for g in range(GM_GROUPS):
            m = jnp.dot(ws[g], vc, preferred_element_type=F32)
            mixed = mixed + jnp.where(grp == g, m, 0.0)
        o_ref[0, rows, :] = (u[rows] * mixed).astype(o_ref.dtype)


def _gmlp(z, gain, w_s, b_s):
    bsz, t_len, _ = z.shape
    tm = min(512, t_len)
    return pl.pallas_call(
        _gmlp_kernel,
        out_shape=jax.ShapeDtypeStruct((bsz, t_len, GM_WIDTH), BF16),
        grid=(bsz, t_len // tm),
        in_specs=[pl.BlockSpec((1, tm, 2 * GM_WIDTH), lambda b, i: (b, i, Z_GM // (2 * GM_WIDTH))),
                  pl.BlockSpec((1, GM_WIDTH), lambda b, i: (0, 0)),
                  pl.BlockSpec((GM_GROUPS, GM_CHUNK, GM_CHUNK), lambda b, i: (0, 0, 0)),
                  pl.BlockSpec((GM_CHUNK, GM_GROUPS), lambda b, i: (0, 0))],
        out_specs=pl.BlockSpec((1, tm, GM_WIDTH), lambda b, i: (b, i, 0)),
        compiler_params=_cparams(("parallel", "parallel")),
        name="gmlp",
    )(z, gain.reshape(1, GM_WIDTH), w_s, b_s.T)


def _rwprep_kernel(z_ref, zp_ref, mu_ref, wl_ref, w0_ref, a0_ref, kk_ref, ka_ref,
                   r_o, lw_o, k_o, v_o, a_o, b_o, g_o):
    tm = z_ref.shape[1]
    z = z_ref[0]
    prev = jnp.where(pl.program_id(1) > 0, zp_ref[0, 7:8, :], 0.0)
    rowid = lax.broadcasted_iota(I32, z.shape, 0)
    zs = jnp.where(rowid == 0, prev, pltpu.roll(z, 1, axis=0))
    zz = z + mu_ref[...] * (zs - z)
    r = zz[:, 0:RW_WIDTH]
    k = zz[:, RW_WIDTH:2 * RW_WIDTH]
    v = zz[:, 2 * RW_WIDTH:3 * RW_WIDTH]
    lo = zz[:, 3 * RW_WIDTH:3 * RW_WIDTH + RW_LORA]
    lane = lax.broadcasted_iota(I32, (tm, RW_LORA), 1)
    act = jnp.where(lane < RW_DECAY_LORA, jnp.tanh(lo),
                    jnp.where(lane < RW_DECAY_LORA + RW_ICLR_LORA, lo, jax.nn.sigmoid(lo)))
    proj = jnp.dot(act, wl_ref[...], preferred_element_type=F32, precision=HIGHEST)
    xw = -(w0_ref[...] + proj[:, 0:RW_WIDTH])
    softplus = jnp.maximum(xw, 0.0) + jnp.log1p(jnp.exp(-jnp.abs(xw)))
    lw = -jnp.exp(-softplus - 0.5)
    a = jax.nn.sigmoid(a0_ref[...] + proj[:, RW_WIDTH:2 * RW_WIDTH])
    g = proj[:, 2 * RW_WIDTH:3 * RW_WIDTH]
    kk = k * kk_ref[...]
    k2 = k * (1.0 + (a - 1.0) * ka_ref[...])
    for h in range(RW_HEADS):
        sl = slice(h * RW_HEAD_DIM, (h + 1) * RW_HEAD_DIM)
        kkh = kk[:, sl]
        nrm = jnp.sqrt(jnp.sum(kkh * kkh, axis=-1, keepdims=True))
        kkh = kkh / jnp.maximum(nrm, 1e-12)
        r_o[0, h] = r[:, sl]
        lw_o[0, h] = lw[:, sl]
        k_o[0, h] = k2[:, sl]
        v_o[0, h] = v[:, sl]
        a_o[0, h] = -kkh
        b_o[0, h] = kkh * a[:, sl]
        g_o[0, h] = g[:, sl]


def _rwprep(z, mu_pad, w_lora, w0, a0, k_k, k_a):
    bsz, t_len, _ = z.shape
    tm = min(512, t_len)
    hm = jax.ShapeDtypeStruct((bsz, RW_HEADS, t_len, RW_HEAD_DIM), F32)
    hm_spec = pl.BlockSpec((1, RW_HEADS, tm, RW_HEAD_DIM), lambda b, i: (b, 0, i, 0))
    vec = lambda n: pl.BlockSpec((1, n), lambda b, i: (0, 0))
    rw_blk = 0
    return pl.pallas_call(
        _rwprep_kernel,
        out_shape=(hm,) * 7,
        grid=(bsz, t_len // tm),
        in_specs=[pl.BlockSpec((1, tm, RW_BLOCK), lambda b, i: (b, i, rw_blk)),
                  pl.BlockSpec((1, 8, RW_BLOCK), lambda b, i: (b, jnp.maximum(i * (tm // 8) - 1, 0), rw_blk)),
                  vec(RW_BLOCK),
                  pl.BlockSpec((RW_LORA, 3 * RW_WIDTH), lambda b, i: (0, 0)),
                  vec(RW_WIDTH), vec(RW_WIDTH), vec(RW_WIDTH), vec(RW_WIDTH)],
        out_specs=(hm_spec,) * 7,
        compiler_params=_cparams(("parallel", "parallel")),
        name="rwprep",
    )(z, z, mu_pad, w_lora, w0.reshape(1, -1), a0.reshape(1, -1), k_k.reshape(1, -1), k_a.reshape(1, -1))


def _rwscan_kernel(r_ref, lw_ref, k_ref, v_ref, a_ref, b_ref, g_ref, rk_ref, gg_ref, gb_ref, o_ref,
                   s_ref, rp_ref, y_ref, gm_ref, h0_ref, we_ref):
    cl = RW_CHUNK
    tb = r_ref.shape[2]
    n_chunk = tb // cl

    @pl.when(pl.program_id(1) == 0)
    def _():
        s_ref[...] = jnp.zeros_like(s_ref)

    n = RW_HEADS * cl
    ri = lax.broadcasted_iota(I32, (n, n), 0)
    ci = lax.broadcasted_iota(I32, (n, n), 1)
    same_head = (ri // cl) == (ci // cl)
    lower = same_head & (ri >= ci)
    strict = same_head & (ri > ci)
    eye = jnp.where(ri == ci, 1.0, 0.0)
    ones_lower = jnp.where(lower, 1.0, 0.0).astype(BF16)

    def prepare(chunks):
        grp = range(len(chunks))
        each = lambda fn: [fn(u) for u in grp]
        rows = [pl.ds(pl.multiple_of(c * cl, cl), cl) for c in chunks]
        stack = lambda ref: each(lambda u: ref[0, :, rows[u], :].reshape(n, RW_HEAD_DIM))
        r, lw, k, v, a, b = (stack(ref) for ref in (r_ref, lw_ref, k_ref, v_ref, a_ref, b_ref))
        hd = RW_HEAD_DIM
        parts = each(lambda u: jnp.concatenate(_split3(lw[u]), axis=-1))
        sums = each(lambda u: jnp.dot(ones_lower, parts[u], preferred_element_type=F32))
        cw = each(lambda u: sums[u][:, :hd] + sums[u][:, hd:2 * hd] + sums[u][:, 2 * hd:])
        w_in = each(lambda u: jnp.exp(cw[u]))
        w_inv = each(lambda u: jnp.exp(-cw[u]))
        rt = each(lambda u: r[u] * w_in[u])
        at = each(lambda u: a[u] * jnp.exp(cw[u] - lw[u]))
        kt = each(lambda u: k[u] * w_inv[u])
        bt = each(lambda u: b[u] * w_inv[u])
        w_end = each(lambda u: w_in[u].reshape(RW_HEADS, cl, RW_HEAD_DIM)[:, cl - 1:cl, :])
        w_end_rows = each(lambda u: jnp.broadcast_to(w_end[u], (RW_HEADS, cl, RW_HEAD_DIM)).reshape(n, RW_HEAD_DIM))
        a_ab = each(lambda u: jnp.where(strict, _mm_nt(at[u], bt[u]), 0.0))
        a_ak = each(lambda u: jnp.where(strict, _mm_nt(at[u], kt[u]), 0.0))
        m_rb = each(lambda u: jnp.where(lower, _mm_nt(rt[u], bt[u]), 0.0))
        m_rk = each(lambda u: jnp.where(lower, _mm_nt(rt[u], kt[u]), 0.0))
        inv = each(lambda u: eye + a_ab[u])
        p = a_ab
        for _ in range(cl.bit_length() - 2):
            p = [_mm(p[u], p[u]) for u in grp]
            inv = [inv[u] + _mm(inv[u], p[u]) for u in grp]
        akv = each(lambda u: _mm(a_ak[u], v[u]))
        apz = each(lambda u: _mm(inv[u], jnp.concatenate([at[u], akv[u]], axis=-1)).astype(BF16))
        mix = each(lambda u: jnp.dot(m_rb[u].astype(BF16), apz[u], preferred_element_type=F32))
        bend = each(lambda u: bt[u] * w_end_rows[u])
        kend = each(lambda u: kt[u] * w_end_rows[u])
        rp = each(lambda u: (rt[u] + mix[u][:, :hd]).astype(BF16))
        y0 = each(lambda u: mix[u][:, hd:] + _mm(m_rk[u], v[u]))
        for u in grp:
            for h in range(RW_HEADS):
                hs = slice(h * cl, (h + 1) * cl)
                both = _mm_tn(apz[u][hs], bend[u][hs])
                rp_ref[h, rows[u], :] = rp[u][hs]
                y_ref[h, rows[u], :] = y0[u][hs]
                gm_ref[h, rows[u], :] = both[:hd].astype(BF16)
                h0_ref[h, rows[u], :] = both[hd:] + _mm_tn(v[u][hs], kend[u][hs])
                we_ref[h, chunks[u]] = w_end[u][h]

    def prepare_step(i, carry):
        prepare([i * RW_PREP_UNROLL + u for u in range(RW_PREP_UNROLL)])
        return carry

    lax.fori_loop(0, n_chunk // RW_PREP_UNROLL, prepare_step, 0)

    def advance(c, carry):
        rows = pl.ds(pl.multiple_of(c * cl, cl), cl)
        for h in range(RW_HEADS):
            s = s_ref[h]
            sb = s.astype(BF16)
            y_ref[h, rows, :] = y_ref[h, rows, :] + lax.dot_general(
                rp_ref[h, rows, :], sb, (((1,), (1,)), ((), ())), preferred_element_type=F32)
            s_ref[h] = (s * we_ref[h, c] + jnp.dot(sb, gm_ref[h, rows, :], preferred_element_type=F32)
                        + h0_ref[h, rows, :])
        return carry

    lax.fori_loop(0, n_chunk, advance, 0)

    for h in range(RW_HEADS):
        y = y_ref[h]
        mu = jnp.mean(y, axis=-1, keepdims=True)
        yc = y - mu
        var = jnp.mean(yc * yc, axis=-1, keepdims=True)
        yn = yc * lax.rsqrt(var + RW_GN_EPS) * gg_ref[h] + gb_ref[h]
        v = v_ref[0, h]
        bonus = jnp.sum(r_ref[0, h] * k_ref[0, h] * rk_ref[h], axis=-1, keepdims=True) * v
        o_ref[0, h] = ((yn + bonus) * g_ref[0, h]).astype(o_ref.dtype)


def _rwscan(r, lw, k, v, a, b, g, r_k, gn_gain, gn_bias):
    bsz, _, t_len, _ = r.shape
    tb = min(512, t_len)
    hm_spec = pl.BlockSpec((1, RW_HEADS, tb, RW_HEAD_DIM), lambda bi, i: (bi, 0, i, 0))
    par = pl.BlockSpec((RW_HEADS, 1, RW_HEAD_DIM), lambda bi, i: (0, 0, 0))
    hshape = (RW_HEADS, 1, RW_HEAD_DIM)
    return pl.pallas_call(
        _rwscan_kernel,
        out_shape=jax.ShapeDtypeStruct((bsz, RW_HEADS, t_len, RW_HEAD_DIM), BF16),
        grid=(bsz, t_len // tb),
        in_specs=[hm_spec] * 7 + [par] * 3,
        out_specs=hm_spec,
        scratch_shapes=[pltpu.VMEM((RW_HEADS, RW_HEAD_DIM, RW_HEAD_DIM), F32),
                        pltpu.VMEM((RW_HEADS, tb, RW_HEAD_DIM), BF16), pltpu.VMEM((RW_HEADS, tb, RW_HEAD_DIM), F32),
                        pltpu.VMEM((RW_HEADS, tb, RW_HEAD_DIM), BF16), pltpu.VMEM((RW_HEADS, tb, RW_HEAD_DIM), F32),
                        pltpu.VMEM((RW_HEADS, tb // RW_CHUNK, 1, RW_HEAD_DIM), F32)],
        compiler_params=_cparams(("parallel", "arbitrary")),
        name="rwscan",
    )(r, lw, k, v, a, b, g, r_k.reshape(hshape), gn_gain.reshape(hshape), gn_bias.reshape(hshape))


FOX_PAIRS = FOX_HEADS // 2
FOX_EXTRA = 3
FOX_ACC_ROWS = FOX_HEAD_DIM + 16
FOX_LOOKAHEAD = 2


def _fox_bias_selector():
    sel = np.zeros((LANES, 2 * FOX_HEADS * LANES), np.float32)
    for h in range(FOX_HEADS):
        base = FOX_HEAD_DIM if h % 2 == 0 else 0
        for p in range(FOX_EXTRA):
            sel[p * FOX_HEADS + h, h * LANES + base + p] = 1.0
            sel[p * FOX_HEADS + h, (FOX_HEADS + h) * LANES + base + FOX_EXTRA + p] = -1.0
    return sel


def _foxprep_kernel(z_ref, f_ref, fb_ref, qg_ref, kg_ref, sel_ref, q_o, k_o, vt_o, carry_ref):
    tm = z_ref.shape[1]

    @pl.when(pl.program_id(1) == 0)
    def _():
        carry_ref[...] = jnp.zeros_like(carry_ref)

    log_f = _log_sigmoid(f_ref[0] + fb_ref[...])
    cum = carry_ref[...] + _tri_cumsum(log_f, tm)
    carry_ref[...] = cum[tm - 1:tm, :]
    lane = lax.broadcasted_iota(I32, (tm, LANES), 1)
    hi, mid, lo = (p.astype(F32) for p in _split3(cum * LOG2E))
    packed = jnp.where(lane < FOX_HEADS, hi,
                       jnp.where(lane < 2 * FOX_HEADS, pltpu.roll(mid, FOX_HEADS, axis=1),
                                 pltpu.roll(lo, 2 * FOX_HEADS, axis=1)))
    packed = jnp.where(lane < FOX_EXTRA * FOX_HEADS, packed, 0.0).astype(BF16)
    extra = jnp.dot(packed, sel_ref[...], preferred_element_type=F32)

    left = lane < FOX_HEAD_DIM
    in_half = lane % FOX_HEAD_DIM
    ones_q = jnp.where((in_half >= FOX_EXTRA) & (in_half < 2 * FOX_EXTRA), 1.0, 0.0)
    ones_k = jnp.where(in_half < FOX_EXTRA, 1.0, 0.0)

    def normed(block, gain):
        sq = block * block
        s_left = jnp.sum(jnp.where(left, sq, 0.0), axis=-1, keepdims=True)
        s_right = jnp.sum(jnp.where(left, 0.0, sq), axis=-1, keepdims=True)
        ms = jnp.where(left, s_left, s_right) * (1.0 / FOX_HEAD_DIM)
        return block * lax.rsqrt(ms + EPS) * gain

    for j in range(FOX_PAIRS):
        qn = normed(z_ref[0, :, j * LANES:(j + 1) * LANES].astype(F32), qg_ref[...] * (ATTN_SCALE * LOG2E))
        kn = normed(z_ref[0, :, FOX_WIDTH + j * LANES:FOX_WIDTH + (j + 1) * LANES].astype(F32), kg_ref[...])
        for par in range(2):
            h = 2 * j + par
            own = left if par == 0 else jnp.logical_not(left)
            q_o[0, h] = jnp.where(own, qn, extra[:, h * LANES:(h + 1) * LANES] + ones_q).astype(BF16)
            k_o[0, h] = jnp.where(own, kn, extra[:, (FOX_HEADS + h) * LANES:(FOX_HEADS + h + 1) * LANES]
                                  + ones_k).astype(BF16)
    ri = lax.broadcasted_iota(I32, (FOX_WIDTH, FOX_WIDTH), 0)
    ci = lax.broadcasted_iota(I32, (FOX_WIDTH, FOX_WIDTH), 1)
    eye = jnp.where(ri == ci, 1.0, 0.0).astype(BF16)
    v = z_ref[0, :, 2 * FOX_WIDTH:3 * FOX_WIDTH].astype(BF16)
    vt_o[0] = lax.dot_general(eye, v, (((1,), (1,)), ((), ())), preferred_element_type=F32).astype(BF16)


def _foxprep(z, z_rw, f_bias_pad, q_gain, k_gain):
    bsz, t_len, _ = z.shape
    tm = min(512, t_len)
    qk = jax.ShapeDtypeStruct((bsz, FOX_HEADS, t_len, LANES), BF16)
    qk_spec = pl.BlockSpec((1, FOX_HEADS, tm, LANES), lambda b, i: (b, 0, i, 0))
    sel = jnp.asarray(_fox_bias_selector(), BF16)
    return pl.pallas_call(
        _foxprep_kernel,
        out_shape=(qk, qk, jax.ShapeDtypeStruct((bsz, FOX_WIDTH, t_len), BF16)),
        grid=(bsz, t_len // tm),
        in_specs=[pl.BlockSpec((1, tm, 3 * FOX_WIDTH), lambda b, i: (b, i, Z_FOX // (3 * FOX_WIDTH))),
                  pl.BlockSpec((1, tm, LANES), lambda b, i: (b, i, (Z_F - Z_RW) // LANES)),
                  pl.BlockSpec((1, LANES), lambda b, i: (0, 0)),
                  pl.BlockSpec((1, LANES), lambda b, i: (0, 0)),
                  pl.BlockSpec((1, LANES), lambda b, i: (0, 0)),
                  pl.BlockSpec(sel.shape, lambda b, i: (0, 0))],
        out_specs=(qk_spec, qk_spec, pl.BlockSpec((1, FOX_WIDTH, tm), lambda b, i: (b, 0, i))),
        scratch_shapes=[pltpu.VMEM((1, LANES), F32)],
        compiler_params=_cparams(("parallel", "arbitrary")),
        name="foxprep",
    )(z, z_rw, f_bias_pad, jnp.tile(q_gain.reshape(1, -1), (1, 2)), jnp.tile(k_gain.reshape(1, -1), (1, 2)), sel)


def _fox_kernel(qi_ref, kj_ref, q_ref, k_ref, vt_ref, o_ref, m_ref, acc_ref):
    i = qi_ref[pl.program_id(1)]
    j = kj_ref[pl.program_id(1)]
    tq = q_ref.shape[2]
    tk = k_ref.shape[2]
    sub = 8

    @pl.when(j == 0)
    def _():
        m_ref[...] = jnp.full_like(m_ref, MASK_VALUE)
        acc_ref[...] = jnp.zeros_like(acc_ref)

    ones_rows = jnp.ones((FOX_ACC_ROWS - FOX_HEAD_DIM, tk), BF16)

    def scores(h):
        return lax.dot_general(k_ref[0, h], q_ref[0, h], (((1,), (1,)), ((), ())), preferred_element_type=F32)

    def update(diagonal):
        if diagonal:
            key = lax.broadcasted_iota(I32, (tk, tq), 0)
            qry = lax.broadcasted_iota(I32, (tk, tq), 1)
            keep = key <= qry
        ahead = [scores(h) for h in range(FOX_LOOKAHEAD)]
        for h in range(FOX_HEADS):
            s = ahead.pop(0)
            if h + FOX_LOOKAHEAD < FOX_HEADS:
                ahead.append(scores(h + FOX_LOOKAHEAD))
            if diagonal:
                s = jnp.where(keep, s, MASK_VALUE)
            s3 = s.reshape(tk // sub, sub, tq)
            m_prev = m_ref[h]
            m_cur = jnp.max(jnp.max(s3, axis=0), axis=0, keepdims=True)
            m_new = jnp.maximum(m_prev, m_cur)
            alpha = jnp.exp2(m_prev - m_new)
            p = jnp.exp2(s3 - m_new[None]).astype(BF16).reshape(tk, tq)
            lhs = jnp.concatenate([vt_ref[0, h * FOX_HEAD_DIM:(h + 1) * FOX_HEAD_DIM, :], ones_rows], axis=0)
            pv = jnp.dot(lhs, p, preferred_element_type=F32)
            acc = acc_ref[h].reshape(FOX_ACC_ROWS // sub, sub, tq) * alpha[None]
            acc_ref[h] = acc.reshape(FOX_ACC_ROWS, tq) + pv
            m_ref[h] = m_new

    @pl.when(j < i)
    def _():
        update(False)

    @pl.when(j == i)
    def _():
        update(True)
        outs = []
        for h in range(FOX_HEADS):
            acc = acc_ref[h]
            outs.append((acc[:FOX_HEAD_DIM] / acc[FOX_HEAD_DIM:FOX_HEAD_DIM + 1]).astype(BF16))
        out_t = jnp.concatenate(outs, axis=0)
        ri = lax.broadcasted_iota(I32, (tq, tq), 0)
        ci = lax.broadcasted_iota(I32, (tq, tq), 1)
        eye = jnp.where(ri == ci, 1.0, 0.0).astype(BF16)
        o_ref[0] = lax.dot_general(eye, out_t, (((1,), (1,)), ((), ())),
                                   preferred_element_type=F32).astype(o_ref.dtype)


def _fox(q, k, vt):
    bsz, _, t_len, _ = q.shape
    tq = min(512, t_len)
    n_blk = t_len // tq
    pairs = [(i, j) for i in range(n_blk) for j in range(i + 1)]
    qi = jnp.asarray([p[0] for p in pairs], I32)
    kj = jnp.asarray([p[1] for p in pairs], I32)
    grid_spec = pltpu.PrefetchScalarGridSpec(
        num_scalar_prefetch=2,
        grid=(bsz, len(pairs)),
        in_specs=[pl.BlockSpec((1, FOX_HEADS, tq, LANES), lambda b, s, qi, kj: (b, 0, qi[s], 0)),
                  pl.BlockSpec((1, FOX_HEADS, tq, LANES), lambda b, s, qi, kj: (b, 0, kj[s], 0)),
                  pl.BlockSpec((1, FOX_WIDTH, tq), lambda b, s, qi, kj: (b, 0, kj[s]))],
        out_specs=pl.BlockSpec((1, tq, FOX_WIDTH), lambda b, s, qi, kj: (b, qi[s], 0)),
        scratch_shapes=[pltpu.VMEM((FOX_HEADS, 8, tq), F32), pltpu.VMEM((FOX_HEADS, FOX_ACC_ROWS, tq), F32)],
    )
    return pl.pallas_call(
        _fox_kernel,
        out_shape=jax.ShapeDtypeStruct((bsz, t_len, FOX_WIDTH), BF16),
        grid_spec=grid_spec,
        compiler_params=_cparams(("parallel", "arbitrary")),
        name="fox",
    )(qi, kj, q, k, vt)


def _merge_kernel(zg_ref, ygm_ref, yrw_ref, yfox_ref, x_ref, g1_ref, sc2_ref, sh2_ref, pb_ref, wo_ref, wr_ref, br_ref,
                  x1_o, h2_o, idx_o, gate_o, rank_o, cnt_o, carry_ref):
    tm = x_ref.shape[1]

    @pl.when((pl.program_id(0) == 0) & (pl.program_id(1) == 0))
    def _():
        carry_ref[...] = jnp.zeros_like(carry_ref)

    sg = 0.5 * jnp.tanh(0.5 * zg_ref[0].astype(F32)) + 0.5
    p_gm = jnp.dot(ygm_ref[0], pb_ref[0:GM_WIDTH, :], preferred_element_type=F32)
    y_rw = jnp.concatenate([yrw_ref[0, h] for h in range(RW_HEADS)], axis=-1)
    p_rw = jnp.dot(y_rw, pb_ref[GM_WIDTH:GM_WIDTH + RW_WIDTH, :], preferred_element_type=F32)
    p_fox = jnp.dot(yfox_ref[0], pb_ref[GM_WIDTH + RW_WIDTH:, :], preferred_element_type=F32)
    merged = sg[:, 0:D_MODEL] * p_gm + sg[:, D_MODEL:2 * D_MODEL] * p_rw + sg[:, 2 * D_MODEL:] * p_fox
    x1 = x_ref[0] + g1_ref[0] * jnp.dot(merged.astype(BF16), wo_ref[...], preferred_element_type=F32)
    x1_o[0] = x1
    h2 = x1 * lax.rsqrt(jnp.mean(x1 * x1, axis=-1, keepdims=True) + EPS) * (1.0 + sc2_ref[0]) + sh2_ref[0]
    h2_o[0] = _pack_halves(h2)

    h_hi, h_lo, _ = _split3(h2)
    w_hi, w_lo, _ = _split3(wr_ref[...])
    logits = (jnp.dot(h_hi, w_hi, preferred_element_type=F32) + jnp.dot(h_hi, w_lo, preferred_element_type=F32)
              + jnp.dot(h_lo, w_hi, preferred_element_type=F32)) + br_ref[...]
    lane = lax.broadcasted_iota(I32, (tm, N_EXPERTS), 1)
    vals, idxs = [], []
    rest = logits
    for _ in range(TOP_K):
        m = jnp.max(rest, axis=-1, keepdims=True)
        am = jnp.min(jnp.where(rest == m, lane, N_EXPERTS), axis=-1, keepdims=True)
        vals.append(m)
        idxs.append(am)
        rest = jnp.where(lane == am, -jnp.inf, rest)
    exps = [jnp.exp(val - vals[0]) for val in vals]
    denom = exps[0] + exps[1] + exps[2] + exps[3]

    onehot = jnp.zeros((tm, N_EXPERTS), F32)
    for am in idxs:
        onehot = onehot + jnp.where(lane == am, 1.0, 0.0)
    ri = lax.broadcasted_iota(I32, (tm, tm), 0)
    ci = lax.broadcasted_iota(I32, (tm, tm), 1)
    before = jnp.where(ri > ci, 1.0, 0.0).astype(BF16)
    seen = carry_ref[...] + jnp.dot(before, onehot.astype(BF16), preferred_element_type=F32)
    lane_k = lax.broadcasted_iota(I32, (tm, TOP_K), 1)
    idx_out = jnp.zeros((tm, TOP_K), I32)
    gate_out = jnp.zeros((tm, TOP_K), F32)
    rank_out = jnp.zeros((tm, TOP_K), I32)
    for kk in range(TOP_K):
        rank = jnp.sum(jnp.where(lane == idxs[kk], seen, 0.0), axis=-1, keepdims=True).astype(I32)
        idx_out = jnp.where(lane_k == kk, idxs[kk], idx_out)
        gate_out = jnp.where(lane_k == kk, exps[kk] / denom, gate_out)
        rank_out = jnp.where(lane_k == kk, rank, rank_out)
    idx_o[0] = idx_out
    gate_o[0] = gate_out
    rank_o[0] = rank_out
    total = carry_ref[...] + jnp.sum(onehot, axis=0, keepdims=True)
    carry_ref[...] = total
    cnt_o[...] = total.astype(I32)


def _merge(z, y_gm, y_rw, y_fox, x, gate1, scale2, shift2, w_branch, w_o, w_router, b_router, layer):
    bsz, t_len, d = x.shape
    tm = min(512, t_len)
    row = lambda w: pl.BlockSpec((1, tm, w), lambda b, i: (b, i, 0))
    mod = pl.BlockSpec((1, 1, d), lambda b, i: (b, 0, 0))
    full = lambda shape: pl.BlockSpec(shape, lambda b, i: (0,) * len(shape))
    return pl.pallas_call(
        _merge_kernel,
        out_shape=(jax.ShapeDtypeStruct((bsz, t_len, d), F32), jax.ShapeDtypeStruct((bsz, t_len, d // 2), I32),
                   jax.ShapeDtypeStruct((bsz, t_len, TOP_K), I32), jax.ShapeDtypeStruct((bsz, t_len, TOP_K), F32),
                   jax.ShapeDtypeStruct((bsz, t_len, TOP_K), I32), jax.ShapeDtypeStruct((1, N_EXPERTS), I32)),
        grid=(bsz, t_len // tm),
        in_specs=[row(N_BRANCH * D_MODEL), row(GM_WIDTH),
                  pl.BlockSpec((1, RW_HEADS, tm, RW_HEAD_DIM), lambda b, i: (b, 0, i, 0)),
                  row(FOX_WIDTH), row(d), mod, mod, mod,
                  pl.BlockSpec((MIX_WIDTH, d), lambda b, i: (layer, 0)), pl.BlockSpec((d, d), lambda b, i: (layer, 0)),
                  full(w_router.shape), full((1, N_EXPERTS))],
        out_specs=(row(d), row(d // 2), row(TOP_K), row(TOP_K), row(TOP_K), full((1, N_EXPERTS))),
        scratch_shapes=[pltpu.VMEM((1, N_EXPERTS), F32)],
        compiler_params=_cparams(("arbitrary", "arbitrary")),
        name="merge_router",
    )(z, y_gm, y_rw, y_fox, x, gate1, scale2, shift2, w_branch, w_o, w_router, b_router.reshape(1, N_EXPERTS))


def _sc_mesh():
    return plsc.VectorSubcoreMesh(core_axis_name="c", subcore_axis_name="s",
                                  num_cores=SC_CORES, num_subcores=SC_SUBCORES)


def _sc_worker():
    return lax.axis_index("s") * SC_CORES + lax.axis_index("c")


def _sc_scatter_rows(src, idx3, n_out):
    _, d = src.shape
    n_copy, n_grp, _ = idx3.shape
    grp_per_w = n_grp // SC_WORKERS
    assert grp_per_w % 2 == 0

    def body(src_hbm, idx_hbm, out_hbm, idx_v, rows_a, rows_b, sem):
        g0 = _sc_worker() * grp_per_w
        for q in range(n_copy):
            pltpu.sync_copy(idx_hbm.at[q, pl.ds(g0, grp_per_w)], idx_v.at[pl.ds(q * grp_per_w, grp_per_w)])

        @pl.loop(0, grp_per_w, step=2)
        def _(j):
            read_a = pltpu.async_copy(src_hbm.at[pl.ds((g0 + j) * SC_ROWS, SC_ROWS)], rows_a, sem.at[0])
            read_b = pltpu.async_copy(src_hbm.at[pl.ds((g0 + j + 1) * SC_ROWS, SC_ROWS)], rows_b, sem.at[1])
            read_a.wait()
            put_a = [pltpu.async_copy(rows_a, out_hbm.at[idx_v.at[q * grp_per_w + j]], sem.at[2])
                     for q in range(n_copy)]
            read_b.wait()
            put_b = [pltpu.async_copy(rows_b, out_hbm.at[idx_v.at[q * grp_per_w + j + 1]], sem.at[3])
                     for q in range(n_copy)]
            for cp in put_a + put_b:
                cp.wait()

    return pl.kernel(
        body, out_type=jax.ShapeDtypeStruct((n_out, d), src.dtype), mesh=_sc_mesh(),
        scratch_types=[pltpu.VMEM((n_copy * grp_per_w, SC_ROWS), I32), pltpu.VMEM((SC_ROWS, d), src.dtype),
                       pltpu.VMEM((SC_ROWS, d), src.dtype), pltpu.SemaphoreType.DMA((4,))],
        name="sc_dispatch",
    )(src, idx3)


def _sc_gather_rows(table, idx2):
    _, d = table.shape
    n_grp, _ = idx2.shape
    grp_per_w = n_grp // SC_WORKERS
    assert grp_per_w % 2 == 0

    def body(table_hbm, idx_hbm, out_hbm, idx_v, rows_a, rows_b, sem):
        g0 = _sc_worker() * grp_per_w
        pltpu.sync_copy(idx_hbm.at[pl.ds(g0, grp_per_w)], idx_v)

        @pl.loop(0, grp_per_w, step=2)
        def _(j):
            get_a = pltpu.async_copy(table_hbm.at[idx_v.at[j]], rows_a, sem.at[0])
            get_b = pltpu.async_copy(table_hbm.at[idx_v.at[j + 1]], rows_b, sem.at[1])
            get_a.wait()
            put_a = pltpu.async_copy(rows_a, out_hbm.at[pl.ds((g0 + j) * SC_ROWS, SC_ROWS)], sem.at[2])
            get_b.wait()
            put_b = pltpu.async_copy(rows_b, out_hbm.at[pl.ds((g0 + j + 1) * SC_ROWS, SC_ROWS)], sem.at[3])
            put_a.wait()
            put_b.wait()

    return pl.kernel(
        body, out_type=jax.ShapeDtypeStruct((n_grp * SC_ROWS, d), table.dtype), mesh=_sc_mesh(),
        scratch_types=[pltpu.VMEM((grp_per_w, SC_ROWS), I32), pltpu.VMEM((SC_ROWS, d), table.dtype),
                       pltpu.VMEM((SC_ROWS, d), table.dtype), pltpu.SemaphoreType.DMA((4,))],
        name="sc_combine_gather",
    )(table, idx2)


def _ffn_weight_copies(expert, wgu_hbm, wd_hbm, stage_gu, stage_d, sem):
    return (pltpu.make_async_copy(wgu_hbm.at[expert], stage_gu, sem.at[0]),
            pltpu.make_async_copy(wd_hbm.at[expert], stage_d, sem.at[1]))


def _ffn_kernel(be_ref, first_ref, nxt_ref, nv_ref, x_ref, wgu_hbm, wd_hbm, bgu_ref, bd_ref, o_ref,
                stage_gu, stage_d, wgu_b, wd_b, sem):
    step = pl.program_id(0)
    copies = functools.partial(_ffn_weight_copies, wgu_hbm=wgu_hbm, wd_hbm=wd_hbm, stage_gu=stage_gu,
                               stage_d=stage_d, sem=sem)

    @pl.when(step == 0)
    def _():
        for cp in copies(be_ref[0]):
            cp.start()

    def switch_weights(idx):
        @pl.when(first_ref[idx] == 1)
        def _():
            for cp in copies(be_ref[idx]):
                cp.wait()
            wgu_b[...] = stage_gu[...].astype(BF16)
            wd_b[...] = stage_d[...].astype(BF16)

            @pl.when(nxt_ref[idx] >= 0)
            def _():
                for cp in copies(nxt_ref[idx]):
                    cp.start()

    def compute(row0, n_rows, n_valid, expert):
        rows = slice(row0, row0 + n_rows)
        rowid = lax.broadcasted_iota(I32, (n_rows, x_ref.shape[1]), 0)
        xp = jnp.where(rowid < n_valid, x_ref[rows, :], 0)
        x = jnp.concatenate(_unpack_halves(xp), axis=-1).astype(BF16)
        gu = jnp.dot(x, wgu_b[...], preferred_element_type=F32) + bgu_ref[expert]
        g_ = jnp.minimum(gu[:, :D_FF], SWIGLU_LIMIT)
        u_ = jnp.clip(gu[:, D_FF:], -SWIGLU_LIMIT, SWIGLU_LIMIT)
        act = (u_ + 1.0) * (g_ * jax.nn.sigmoid(SWIGLU_ALPHA * g_))
        y = jnp.dot(act.astype(BF16), wd_b[...], preferred_element_type=F32) + bd_ref[expert]
        o_ref[rows, :] = _pack_halves(y)

    def single(idx, row0):
        switch_weights(idx)

        @pl.when(nv_ref[idx] > 0)
        def _():
            compute(row0, MOE_BLOCK, nv_ref[idx], be_ref[idx])

        @pl.when(nv_ref[idx] <= 0)
        def _():
            o_ref[row0:row0 + MOE_BLOCK, :] = jnp.zeros((MOE_BLOCK, o_ref.shape[1]), o_ref.dtype)

    for pair in range(FFN_STEP_BLOCKS // 2):
        idx = step * FFN_STEP_BLOCKS + 2 * pair
        row0 = 2 * pair * MOE_BLOCK
        together = (first_ref[idx + 1] == 0) & (nv_ref[idx + 1] > 0)

        @pl.when(together)
        def _():
            switch_weights(idx)
            compute(row0, 2 * MOE_BLOCK, MOE_BLOCK + nv_ref[idx + 1], be_ref[idx])

        @pl.when(jnp.logical_not(together))
        def _():
            single(idx, row0)
            single(idx + 1, row0 + MOE_BLOCK)


def _ffn(block_expert, block_first, block_next, block_valid, xin, w_gate_up, b_gate_up, w_down, b_down):
    n_rows, dp = xin.shape
    d = 2 * dp
    step_rows = FFN_STEP_BLOCKS * MOE_BLOCK
    resident = lambda arr: pl.BlockSpec(arr.shape, lambda i, *_: (0,) * arr.ndim)
    grid_spec = pltpu.PrefetchScalarGridSpec(
        num_scalar_prefetch=4,
        grid=(n_rows // step_rows,),
        in_specs=[pl.BlockSpec((step_rows, dp), lambda i, *_: (i, 0)),
                  pl.BlockSpec(memory_space=pl.ANY), pl.BlockSpec(memory_space=pl.ANY),
                  resident(b_gate_up), resident(b_down)],
        out_specs=pl.BlockSpec((step_rows, dp), lambda i, *_: (i, 0)),
        scratch_shapes=[pltpu.VMEM((d, 2 * D_FF), F32), pltpu.VMEM((D_FF, d), F32),
                        pltpu.VMEM((d, 2 * D_FF), BF16), pltpu.VMEM((D_FF, d), BF16),
                        pltpu.SemaphoreType.DMA((2,))],
    )
    return pl.pallas_call(
        _ffn_kernel,
        out_shape=jax.ShapeDtypeStruct((n_rows, dp), I32),
        grid_spec=grid_spec,
        compiler_params=pltpu.CompilerParams(dimension_semantics=("arbitrary",), vmem_limit_bytes=FFN_VMEM_LIMIT),
        name="expert_ffn",
    )(block_expert, block_first, block_next, block_valid, xin, w_gate_up, w_down, b_gate_up, b_down)


def _combine_kernel(x1_ref, g2_ref, gate_ref, yg_ref, o_ref):
    gate = gate_ref[0]
    y_lo = y_hi = None
    for q in range(TOP_K):
        lo, hi = _unpack_halves(yg_ref[q, 0])
        wq = gate[:, q:q + 1]
        y_lo = wq * lo if y_lo is None else y_lo + wq * lo
        y_hi = wq * hi if y_hi is None else y_hi + wq * hi
    o_ref[0] = x1_ref[0] + g2_ref[0] * jnp.concatenate([y_lo, y_hi], axis=-1)


def _combine(x1, gate2, gate, yg):
    bsz, t_len, d = x1.shape
    tm = min(512, t_len)
    return pl.pallas_call(
        _combine_kernel,
        out_shape=jax.ShapeDtypeStruct((bsz, t_len, d), F32),
        grid=(bsz, t_len // tm),
        in_specs=[pl.BlockSpec((1, tm, d), lambda b, i: (b, i, 0)),
                  pl.BlockSpec((1, 1, d), lambda b, i: (b, 0, 0)),
                  pl.BlockSpec((1, tm, TOP_K), lambda b, i: (b, i, 0)),
                  pl.BlockSpec((TOP_K, 1, tm, d // 2), lambda b, i: (0, b, i, 0))],
        out_specs=pl.BlockSpec((1, tm, d), lambda b, i: (b, i, 0)),
        compiler_params=_cparams(("parallel", "parallel")),
        name="moe_combine",
    )(x1, gate2, gate, yg)


def _moe(x1, gate2, h2, top_idx, gate, rank, counts, w_gate_up, b_gate_up, w_down, b_down, layer):
    bsz, t_len, d = h2.shape
    n_tok = bsz * t_len
    n_assign = n_tok * TOP_K
    n_blocks = -(-n_assign // MOE_BLOCK) + N_EXPERTS
    counts = counts.reshape(N_EXPERTS)
    blocks_e = (counts + MOE_BLOCK - 1) // MOE_BLOCK
    blk_end = jnp.cumsum(blocks_e)
    blk_start = blk_end - blocks_e
    experts = jnp.arange(N_EXPERTS, dtype=I32)
    onehot = top_idx.reshape(n_tok, TOP_K, 1) == experts
    dest = jnp.sum(jnp.where(onehot, blk_start * MOE_BLOCK, 0), axis=-1) + rank.reshape(n_tok, TOP_K)
    dest_t = dest.T.astype(I32)
    blk = jnp.arange(n_blocks, dtype=I32)
    block_expert = jnp.minimum(jnp.sum(blk_end[None, :] <= blk[:, None], axis=1), N_EXPERTS - 1).astype(I32)
    be_hot = block_expert[:, None] == experts
    cnt_b = jnp.sum(jnp.where(be_hot, counts, 0), axis=1)
    start_b = jnp.sum(jnp.where(be_hot, blk_start, 0), axis=1)
    block_valid = jnp.clip(cnt_b - (blk - start_b) * MOE_BLOCK, 0, MOE_BLOCK).astype(I32)
    xin = _sc_scatter_rows(h2.reshape(n_tok, d), dest_t.reshape(TOP_K, n_tok // SC_ROWS, SC_ROWS),
                           n_blocks * MOE_BLOCK)
    block_first = jnp.concatenate([jnp.ones((1,), I32), (block_expert[1:] != block_expert[:-1]).astype(I32)])
    run_start = jnp.where(block_first == 1, blk, n_blocks)
    later_start = lax.cummin(jnp.concatenate([run_start[1:], jnp.full((1,), n_blocks, I32)]), reverse=True)
    next_expert = jnp.concatenate([block_expert, jnp.full((1,), -1 - layer * N_EXPERTS, I32)])[later_start]
    yb = _ffn(block_expert + layer * N_EXPERTS, block_first, next_expert + layer * N_EXPERTS, block_valid, xin,
              w_gate_up, b_gate_up, w_down, b_down)
    yg = _sc_gather_rows(yb, dest_t.reshape(n_assign // SC_ROWS, SC_ROWS))
    return _combine(x1, gate2, gate, yg.reshape(TOP_K, bsz, t_len, d))


def _permute_kernel(w_ref, o_ref):
    o_gm = 0
    o_rw = o_gm + 2 * GM_WIDTH
    o_fox = o_rw + RW_SHIFT_WIDTH
    o_f = o_fox + 3 * FOX_WIDTH
    o_gate = o_f + FOX_HEADS
    w = w_ref[0]
    o_ref[0, :, Z_GATE:Z_FOX] = w[:, o_gate:o_gate + N_BRANCH * D_MODEL].astype(BF16)
    o_ref[0, :, Z_FOX:Z_GM] = w[:, o_fox:o_f].astype(BF16)
    o_ref[0, :, Z_GM:Z_RW] = w[:, o_gm:o_rw].astype(BF16)
    o_ref[0, :, Z_RW:Z_F] = w[:, o_rw:o_fox].astype(BF16)
    tail = jnp.concatenate([w[:, o_f:o_gate], jnp.zeros((w.shape[0], Z_WIDTH - Z_F - FOX_HEADS), F32)], axis=-1)
    o_ref[0, :, Z_F:Z_WIDTH] = tail.astype(BF16)


def _permute_w_in(w_in):
    n_layer, d, w_cols = w_in.shape
    tr = 256
    return pl.pallas_call(
        _permute_kernel,
        out_shape=jax.ShapeDtypeStruct((n_layer, d, Z_WIDTH), BF16),
        grid=(n_layer, d // tr),
        in_specs=[pl.BlockSpec((1, tr, w_cols), lambda l, i: (l, i, 0))],
        out_specs=pl.BlockSpec((1, tr, Z_WIDTH), lambda l, i: (l, i, 0)),
        compiler_params=_cparams(("parallel", "parallel")),
        name="permute_w_in",
    )(w_in)


def _layer(x, mod, w_in_p, gm_v_gain, gm_w_s, gm_b_s, mu_pad, w_lora, rw_w0, rw_a0, rw_k_k, rw_k_a, rw_r_k,
           rw_gn_gain, rw_gn_bias, f_bias_pad, fox_q_gain, fox_k_gain, w_branch, w_o, w_router, b_router,
           w_gate_up, b_gate_up, w_down, b_down, layer):
    shift1, scale1, gate1, shift2, scale2, gate2 = (mod[:, i][:, None, :] for i in range(6))
    z, z_rw = _inproj(x, scale1, shift1, w_in_p, layer)
    y_gm = _gmlp(z, gm_v_gain, gm_w_s, gm_b_s)
    r, lw, k, v, a, b, g = _rwprep(z_rw, mu_pad, w_lora, rw_w0, rw_a0, rw_k_k, rw_k_a)
    y_rw = _rwscan(r, lw, k, v, a, b, g, rw_r_k, rw_gn_gain, rw_gn_bias)
    q, kf, vf = _foxprep(z, z_rw, f_bias_pad, fox_q_gain, fox_k_gain)
    y_fox = _fox(q, kf, vf)
    x1, h2, top_idx, gate, rank, counts = _merge(z, y_gm, y_rw, y_fox, x, gate1, scale2, shift2,
                                                 w_branch, w_o, w_router, b_router, layer)
    return _moe(x1, gate2, h2, top_idx, gate, rank, counts, w_gate_up, b_gate_up, w_down, b_down, layer)


def kernel(x, c, w_ada, b_ada, w_in, gm_v_gain, gm_w_s, gm_b_s, rw_mu, rw_w0, rw_w2, rw_a0, rw_a2, rw_g2, rw_k_k,
           rw_k_a, rw_r_k, rw_gn_gain, rw_gn_bias, fox_f_bias, fox_q_gain, fox_k_gain, w_branch, w_o, w_router,
           b_router, w_gate_up, b_gate_up, w_down, b_down):
    n_layer = w_ada.shape[0]
    bsz = x.shape[0]
    c_pad = jnp.zeros((8, D_MODEL), F32).at[:bsz].set(c)
    mod = _adaln(c_pad, w_ada, b_ada)[:, :bsz].reshape(n_layer, bsz, 6, D_MODEL)
    w_in_p = _permute_w_in(w_in)
    mu_pad = jnp.pad(rw_mu, ((0, 0), (0, RW_BLOCK - RW_SHIFT_WIDTH)))
    w_lora = jnp.zeros((n_layer, RW_LORA, 3 * RW_WIDTH), F32)
    w_lora = w_lora.at[:, 0:RW_DECAY_LORA, 0:RW_WIDTH].set(rw_w2)
    w_lora = w_lora.at[:, RW_DECAY_LORA:RW_DECAY_LORA + RW_ICLR_LORA, RW_WIDTH:2 * RW_WIDTH].set(rw_a2)
    w_lora = w_lora.at[:, RW_DECAY_LORA + RW_ICLR_LORA:, 2 * RW_WIDTH:].set(rw_g2)
    f_bias_pad = jnp.pad(fox_f_bias, ((0, 0), (0, LANES - FOX_HEADS)))
    w_in_p = w_in_p.reshape(n_layer * D_MODEL, Z_WIDTH)
    w_branch_b = w_branch.astype(BF16).reshape(n_layer * MIX_WIDTH, D_MODEL)
    w_o_b = w_o.astype(BF16).reshape(n_layer * D_MODEL, D_MODEL)
    w_gu = w_gate_up.reshape(n_layer * N_EXPERTS, D_MODEL, 2 * D_FF)
    b_gu = b_gate_up.reshape(n_layer * N_EXPERTS, 1, 2 * D_FF)
    w_dn = w_down.reshape(n_layer * N_EXPERTS, D_FF, D_MODEL)
    b_dn = b_down.reshape(n_layer * N_EXPERTS, 1, D_MODEL)
    for l in range(n_layer):
        x = _layer(x, mod[l], w_in_p, gm_v_gain[l], gm_w_s[l], gm_b_s[l], mu_pad[l:l + 1], w_lora[l], rw_w0[l],
                   rw_a0[l], rw_k_k[l], rw_k_a[l], rw_r_k[l], rw_gn_gain[l], rw_gn_bias[l], f_bias_pad[l:l + 1],
                   fox_q_gain[l], fox_k_gain[l], w_branch_b, w_o_b, w_router[l], b_router[l],
                   w_gu, b_gu, w_dn, b_dn, l)
    return x
```

```python
import functools

import jax
import jax.numpy as jnp
import numpy as np
from jax import lax
from jax.experimental import pallas as pl
from jax.experimental.pallas import tpu as pltpu
from jax.experimental.pallas import tpu_sc as plsc

F32 = jnp.float32
BF16 = jnp.bfloat16
I32 = jnp.int32
HIGHEST = lax.Precision.HIGHEST

D_MODEL = 1024
GM_CHUNK = 128
GM_GROUPS = 4
GM_WIDTH = 256
GM_GROUP_DIM = GM_WIDTH // GM_GROUPS
RW_HEADS = 4
RW_HEAD_DIM = 64
RW_WIDTH = RW_HEADS * RW_HEAD_DIM
RW_DECAY_LORA = 32
RW_ICLR_LORA = 32
RW_GATE_LORA = 64
RW_LORA = RW_DECAY_LORA + RW_ICLR_LORA + RW_GATE_LORA
RW_SHIFT_WIDTH = 3 * RW_WIDTH + RW_LORA
RW_GN_EPS = 64e-5
FOX_HEADS = 8
FOX_HEAD_DIM = 64
FOX_WIDTH = FOX_HEADS * FOX_HEAD_DIM
ATTN_SCALE = FOX_HEAD_DIM ** -0.5
MASK_VALUE = -1e30
LOG2E = 1.4426950408889634
N_BRANCH = 3
MIX_WIDTH = GM_WIDTH + RW_WIDTH + FOX_WIDTH
N_EXPERTS = 32
TOP_K = 4
D_FF = D_MODEL
SWIGLU_LIMIT = 7.0
SWIGLU_ALPHA = 1.702
MOE_BLOCK = 256
EPS = 1e-6

Z_GATE = 0
Z_FOX = N_BRANCH * D_MODEL
Z_GM = Z_FOX + 3 * FOX_WIDTH
Z_RW = Z_GM + 2 * GM_WIDTH
RW_BLOCK = 1024
Z_F = Z_RW + RW_SHIFT_WIDTH
Z_WIDTH = Z_RW + RW_BLOCK
LANES = 128
RW_CHUNK = 64
RW_PREP_UNROLL = 4

VMEM_LIMIT = 48 * 1024 * 1024
FFN_VMEM_LIMIT = 56 * 1024 * 1024
FFN_STEP_BLOCKS = 4
SC_CORES = 2
SC_SUBCORES = 16
SC_WORKERS = SC_CORES * SC_SUBCORES
SC_ROWS = 64


def _cparams(sem):
    return pltpu.CompilerParams(dimension_semantics=sem, vmem_limit_bytes=VMEM_LIMIT)


def _mm(a, b):
    return jnp.dot(a.astype(BF16), b.astype(BF16), preferred_element_type=F32)


def _mm_nt(a, b):
    return lax.dot_general(a.astype(BF16), b.astype(BF16), (((1,), (1,)), ((), ())), preferred_element_type=F32)


def _mm_tn(a, b):
    return lax.dot_general(a.astype(BF16), b.astype(BF16), (((0,), (0,)), ((), ())), preferred_element_type=F32)


def _split3(x):
    hi = x.astype(BF16)
    r1 = x - hi.astype(F32)
    mid = r1.astype(BF16)
    lo = (r1 - mid.astype(F32)).astype(BF16)
    return hi, mid, lo


def _tri_cumsum(x, n):
    ri = lax.broadcasted_iota(I32, (n, n), 0)
    ci = lax.broadcasted_iota(I32, (n, n), 1)
    ones = jnp.where(ri >= ci, 1.0, 0.0).astype(BF16)
    hi, mid, lo = _split3(x)
    return (jnp.dot(ones, hi, preferred_element_type=F32) + jnp.dot(ones, mid, preferred_element_type=F32)
            + jnp.dot(ones, lo, preferred_element_type=F32))


def _pack_halves(x):
    w = x.shape[1] // 2
    hi = pltpu.bitcast(x[:, :w].astype(BF16).astype(F32), jnp.uint32)
    lo = pltpu.bitcast(x[:, w:].astype(BF16).astype(F32), jnp.uint32)
    return pltpu.bitcast(hi | (lo >> 16), I32)


def _unpack_halves(p):
    u = pltpu.bitcast(p, jnp.uint32)
    return pltpu.bitcast(u & jnp.uint32(0xFFFF0000), F32), pltpu.bitcast(u << 16, F32)


def _log_sigmoid(x):
    return jnp.minimum(x, 0.0) - jnp.log1p(jnp.exp(-jnp.abs(x)))


def _adaln_kernel(c_ref, w_ref, b_ref, o_ref):
    c = c_ref[...]
    s = c * jax.nn.sigmoid(c)
    o_ref[0] = jnp.dot(s, w_ref[0], preferred_element_type=F32, precision=HIGHEST) + b_ref[0]


def _adaln(c_pad, w_ada, b_ada):
    n_layer, d, w6 = w_ada.shape
    tn = 1536
    return pl.pallas_call(
        _adaln_kernel,
        out_shape=jax.ShapeDtypeStruct((n_layer, c_pad.shape[0], w6), F32),
        grid=(n_layer, w6 // tn),
        in_specs=[pl.BlockSpec(c_pad.shape, lambda l, j: (0, 0)),
                  pl.BlockSpec((1, d, tn), lambda l, j: (l, 0, j)),
                  pl.BlockSpec((1, 1, tn), lambda l, j: (l, 0, j))],
        out_specs=pl.BlockSpec((1, c_pad.shape[0], tn), lambda l, j: (l, 0, j)),
        compiler_params=_cparams(("parallel", "parallel")),
        name="adaln",
    )(c_pad, w_ada, b_ada.reshape(n_layer, 1, w6))


def _inproj_kernel(x_ref, sc_ref, sh_ref, w_ref, zm_ref, zr_ref, xn_ref):
    j = pl.program_id(2)

    @pl.when(j == 0)
    def _():
        x = x_ref[0]
        xn = x * lax.rsqrt(jnp.mean(x * x, axis=-1, keepdims=True) + EPS)
        xn_ref[...] = (xn * (1.0 + sc_ref[0]) + sh_ref[0]).astype(BF16)

    acc = jnp.dot(xn_ref[...], w_ref[...], preferred_element_type=F32)

    @pl.when(j < Z_RW // RW_BLOCK)
    def _():
        zm_ref[0] = acc.astype(BF16)

    @pl.when(j == Z_RW // RW_BLOCK)
    def _():
        zr_ref[0] = acc


def _inproj(x, scale, shift, w, layer):
    bsz, t_len, d = x.shape
    tm = min(1024, t_len)
    tn = RW_BLOCK
    n_main = Z_RW // tn
    return pl.pallas_call(
        _inproj_kernel,
        out_shape=(jax.ShapeDtypeStruct((bsz, t_len, Z_RW), BF16), jax.ShapeDtypeStruct((bsz, t_len, RW_BLOCK), F32)),
        grid=(bsz, t_len // tm, Z_WIDTH // tn),
        in_specs=[pl.BlockSpec((1, tm, d), lambda b, i, j: (b, i, 0)),
                  pl.BlockSpec((1, 1, d), lambda b, i, j: (b, 0, 0)),
                  pl.BlockSpec((1, 1, d), lambda b, i, j: (b, 0, 0)),
                  pl.BlockSpec((d, tn), lambda b, i, j: (layer, j))],
        out_specs=(pl.BlockSpec((1, tm, tn), lambda b, i, j: (b, i, jnp.minimum(j, n_main - 1))),
                   pl.BlockSpec((1, tm, tn), lambda b, i, j: (b, i, 0))),
        scratch_shapes=[pltpu.VMEM((tm, d), BF16)],
        compiler_params=_cparams(("parallel", "parallel", "arbitrary")),
        name="inproj",
    )(x, scale, shift, w)


def _gmlp_kernel(z_ref, gain_ref, ws_ref, bst_ref, o_ref):
    tm = z_ref.shape[1]
    z = z_ref[0].astype(F32)
    u = jax.nn.gelu(z[:, :GM_WIDTH])
    v = jax.nn.gelu(z[:, GM_WIDTH:])
    v = v * lax.rsqrt(jnp.mean(v * v, axis=-1, keepdims=True) + EPS) * gain_ref[...]
    vb = v.astype(BF16)
    grp = lax.broadcasted_iota(I32, (GM_CHUNK, GM_WIDTH), 1) // GM_GROUP_DIM
    ri = lax.broadcasted_iota(I32, (GM_CHUNK, GM_CHUNK), 0)
    ci = lax.broadcasted_iota(I32, (GM_CHUNK, GM_CHUNK), 1)
    causal = ri >= ci
    bias = jnp.zeros((GM_CHUNK, GM_WIDTH), F32)
    ws = []
    for g in range(GM_GROUPS):
        ws.append(jnp.where(causal, ws_ref[g], 0.0).astype(BF16))
        bias = jnp.where(grp == g, bst_ref[:, g:g + 1], bias)
    for c in range(tm // GM_CHUNK):
        rows = slice(c * GM_CHUNK, (c + 1) * GM_CHUNK)
        vc = vb[rows]
        mixed = bias
        for g in range(GM_GROUPS):
            m = jnp.dot(ws[g], vc, preferred_element_type=F32)
            mixed = mixed + jnp.where(grp == g, m, 0.0)
        o_ref[0, rows, :] = (u[rows] * mixed).astype(o_ref.dtype)


def _gmlp(z, gain, w_s, b_s):
    bsz, t_len, _ = z.shape
    tm = min(512, t_len)
    return pl.pallas_call(
        _gmlp_kernel,
        out_shape=jax.ShapeDtypeStruct((bsz, t_len, GM_WIDTH), BF16),
        grid=(bsz, t_len // tm),
        in_specs=[pl.BlockSpec((1, tm, 2 * GM_WIDTH), lambda b, i: (b, i, Z_GM // (2 * GM_WIDTH))),
                  pl.BlockSpec((1, GM_WIDTH), lambda b, i: (0, 0)),
                  pl.BlockSpec((GM_GROUPS, GM_CHUNK, GM_CHUNK), lambda b, i: (0, 0, 0)),
                  pl.BlockSpec((GM_CHUNK, GM_GROUPS), lambda b, i: (0, 0))],
        out_specs=pl.BlockSpec((1, tm, GM_WIDTH), lambda b, i: (b, i, 0)),
        compiler_params=_cparams(("parallel", "parallel")),
        name="gmlp",
    )(z, gain.reshape(1, GM_WIDTH), w_s, b_s.T)


def _rwprep_kernel(z_ref, zp_ref, mu_ref, wl_ref, w0_ref, a0_ref, kk_ref, ka_ref,
                   r_o, lw_o, k_o, v_o, a_o, b_o, g_o):
    tm = z_ref.shape[1]
    z = z_ref[0]
    prev = jnp.where(pl.program_id(1) > 0, zp_ref[0, 7:8, :], 0.0)
    rowid = lax.broadcasted_iota(I32, z.shape, 0)
    zs = jnp.where(rowid == 0, prev, pltpu.roll(z, 1, axis=0))
    zz = z + mu_ref[...] * (zs - z)
    r = zz[:, 0:RW_WIDTH]
    k = zz[:, RW_WIDTH:2 * RW_WIDTH]
    v = zz[:, 2 * RW_WIDTH:3 * RW_WIDTH]
    lo = zz[:, 3 * RW_WIDTH:3 * RW_WIDTH + RW_LORA]
    lane = lax.broadcasted_iota(I32, (tm, RW_LORA), 1)
    act = jnp.where(lane < RW_DECAY_LORA, jnp.tanh(lo),
                    jnp.where(lane < RW_DECAY_LORA + RW_ICLR_LORA, lo, jax.nn.sigmoid(lo)))
    a_hi, a_lo, _ = _split3(act)
    w_hi, w_lo, _ = _split3(wl_ref[...])
    proj = (jnp.dot(a_hi, w_hi, preferred_element_type=F32) + jnp.dot(a_hi, w_lo, preferred_element_type=F32)
            + jnp.dot(a_lo, w_hi, preferred_element_type=F32))
    xw = -(w0_ref[...] + proj[:, 0:RW_WIDTH])
    softplus = jnp.maximum(xw, 0.0) + jnp.log1p(jnp.exp(-jnp.abs(xw)))
    lw = -jnp.exp(-softplus - 0.5)
    a = jax.nn.sigmoid(a0_ref[...] + proj[:, RW_WIDTH:2 * RW_WIDTH])
    g = proj[:, 2 * RW_WIDTH:3 * RW_WIDTH]
    kk = k * kk_ref[...]
    k2 = k * (1.0 + (a - 1.0) * ka_ref[...])
    for h in range(RW_HEADS):
        sl = slice(h * RW_HEAD_DIM, (h + 1) * RW_HEAD_DIM)
        kkh = kk[:, sl]
        nrm = jnp.sqrt(jnp.sum(kkh * kkh, axis=-1, keepdims=True))
        kkh = kkh / jnp.maximum(nrm, 1e-12)
        r_o[0, h] = r[:, sl]
        lw_o[0, h] = lw[:, sl]
        k_o[0, h] = k2[:, sl]
        v_o[0, h] = v[:, sl]
        a_o[0, h] = -kkh
        b_o[0, h] = kkh * a[:, sl]
        g_o[0, h] = g[:, sl]


def _rwprep(z, mu_pad, w_lora, w0, a0, k_k, k_a):
    bsz, t_len, _ = z.shape
    tm = min(512, t_len)
    hm = jax.ShapeDtypeStruct((bsz, RW_HEADS, t_len, RW_HEAD_DIM), F32)
    hm_spec = pl.BlockSpec((1, RW_HEADS, tm, RW_HEAD_DIM), lambda b, i: (b, 0, i, 0))
    vec = lambda n: pl.BlockSpec((1, n), lambda b, i: (0, 0))
    rw_blk = 0
    return pl.pallas_call(
        _rwprep_kernel,
        out_shape=(hm,) * 7,
        grid=(bsz, t_len // tm),
        in_specs=[pl.BlockSpec((1, tm, RW_BLOCK), lambda b, i: (b, i, rw_blk)),
                  pl.BlockSpec((1, 8, RW_BLOCK), lambda b, i: (b, jnp.maximum(i * (tm // 8) - 1, 0), rw_blk)),
                  vec(RW_BLOCK),
                  pl.BlockSpec((RW_LORA, 3 * RW_WIDTH), lambda b, i: (0, 0)),
                  vec(RW_WIDTH), vec(RW_WIDTH), vec(RW_WIDTH), vec(RW_WIDTH)],
        out_specs=(hm_spec,) * 7,
        compiler_params=_cparams(("parallel", "parallel")),
        name="rwprep",
    )(z, z, mu_pad, w_lora, w0.reshape(1, -1), a0.reshape(1, -1), k_k.reshape(1, -1), k_a.reshape(1, -1))


def _rwscan_kernel(r_ref, lw_ref, k_ref, v_ref, a_ref, b_ref, g_ref, rk_ref, gg_ref, gb_ref, o_ref,
                   s_ref, rp_ref, y_ref, gm_ref, h0_ref, we_ref):
    cl = RW_CHUNK
    tb = r_ref.shape[2]
    n_chunk = tb // cl

    @pl.when(pl.program_id(1) == 0)
    def _():
        s_ref[...] = jnp.zeros_like(s_ref)

    n = RW_HEADS * cl
    ri = lax.broadcasted_iota(I32, (n, n), 0)
    ci = lax.broadcasted_iota(I32, (n, n), 1)
    same_head = (ri // cl) == (ci // cl)
    lower = same_head & (ri >= ci)
    strict = same_head & (ri > ci)
    eye = jnp.where(ri == ci, 1.0, 0.0)
    ones_lower = jnp.where(lower, 1.0, 0.0).astype(BF16)

    def prepare(chunks):
        grp = range(len(chunks))
        each = lambda fn: [fn(u) for u in grp]
        rows = [pl.ds(pl.multiple_of(c * cl, cl), cl) for c in chunks]
        stack = lambda ref: each(lambda u: ref[0, :, rows[u], :].reshape(n, RW_HEAD_DIM))
        r, lw, k, v, a, b = (stack(ref) for ref in (r_ref, lw_ref, k_ref, v_ref, a_ref, b_ref))
        hd = RW_HEAD_DIM
        parts = each(lambda u: jnp.concatenate(_split3(lw[u]), axis=-1))
        sums = each(lambda u: jnp.dot(ones_lower, parts[u], preferred_element_type=F32))
        cw = each(lambda u: sums[u][:, :hd] + sums[u][:, hd:2 * hd] + sums[u][:, 2 * hd:])
        w_in = each(lambda u: jnp.exp(cw[u]))
        w_inv = each(lambda u: jnp.exp(-cw[u]))
        rt = each(lambda u: r[u] * w_in[u])
        at = each(lambda u: a[u] * jnp.exp(cw[u] - lw[u]))
        kt = each(lambda u: k[u] * w_inv[u])
        bt = each(lambda u: b[u] * w_inv[u])
        w_end = each(lambda u: w_in[u].reshape(RW_HEADS, cl, RW_HEAD_DIM)[:, cl - 1:cl, :])
        w_end_rows = each(lambda u: jnp.broadcast_to(w_end[u], (RW_HEADS, cl, RW_HEAD_DIM)).reshape(n, RW_HEAD_DIM))
        a_ab = each(lambda u: jnp.where(strict, _mm_nt(at[u], bt[u]), 0.0))
        a_ak = each(lambda u: jnp.where(strict, _mm_nt(at[u], kt[u]), 0.0))
        m_rb = each(lambda u: jnp.where(lower, _mm_nt(rt[u], bt[u]), 0.0))
        m_rk = each(lambda u: jnp.where(lower, _mm_nt(rt[u], kt[u]), 0.0))
        inv = each(lambda u: eye + a_ab[u])
        p = a_ab
        for _ in range(cl.bit_length() - 2):
            p = [_mm(p[u], p[u]) for u in grp]
            inv = [inv[u] + _mm(inv[u], p[u]) for u in grp]
        akv = each(lambda u: _mm(a_ak[u], v[u]))
        apz = each(lambda u: _mm(inv[u], jnp.concatenate([at[u], akv[u]], axis=-1)).astype(BF16))
        mix = each(lambda u: jnp.dot(m_rb[u].astype(BF16), apz[u], preferred_element_type=F32))
        bend = each(lambda u: bt[u] * w_end_rows[u])
        kend = each(lambda u: kt[u] * w_end_rows[u])
        rp = each(lambda u: (rt[u] + mix[u][:, :hd]).astype(BF16))
        y0 = each(lambda u: mix[u][:, hd:] + _mm(m_rk[u], v[u]))
        for u in grp:
            for h in range(RW_HEADS):
                hs = slice(h * cl, (h + 1) * cl)
                both = _mm_tn(apz[u][hs], bend[u][hs])
                rp_ref[h, rows[u], :] = rp[u][hs]
                y_ref[h, rows[u], :] = y0[u][hs]
                gm_ref[h, rows[u], :] = both[:hd].astype(BF16)
                h0_ref[h, rows[u], :] = both[hd:] + _mm_tn(v[u][hs], kend[u][hs])
                we_ref[h, chunks[u]] = w_end[u][h]

    def prepare_step(i, carry):
        prepare([i * RW_PREP_UNROLL + u for u in range(RW_PREP_UNROLL)])
        return carry

    lax.fori_loop(0, n_chunk // RW_PREP_UNROLL, prepare_step, 0)

    def advance(c, carry):
        rows = pl.ds(pl.multiple_of(c * cl, cl), cl)
        for h in range(RW_HEADS):
            s = s_ref[h]
            sb = s.astype(BF16)
            y_ref[h, rows, :] = y_ref[h, rows, :] + lax.dot_general(
                rp_ref[h, rows, :], sb, (((1,), (1,)), ((), ())), preferred_element_type=F32)
            s_ref[h] = (s * we_ref[h, c] + jnp.dot(sb, gm_ref[h, rows, :], preferred_element_type=F32)
                        + h0_ref[h, rows, :])
        return carry

    lax.fori_loop(0, n_chunk, advance, 0)

    for h in range(RW_HEADS):
        y = y_ref[h]
        mu = jnp.mean(y, axis=-1, keepdims=True)
        yc = y - mu
        var = jnp.mean(yc * yc, axis=-1, keepdims=True)
        yn = yc * lax.rsqrt(var + RW_GN_EPS) * gg_ref[h] + gb_ref[h]
        v = v_ref[0, h]
        bonus = jnp.sum(r_ref[0, h] * k_ref[0, h] * rk_ref[h], axis=-1, keepdims=True) * v
        o_ref[0, h] = ((yn + bonus) * g_ref[0, h]).astype(o_ref.dtype)


def _rwscan(r, lw, k, v, a, b, g, r_k, gn_gain, gn_bias):
    bsz, _, t_len, _ = r.shape
    tb = min(512, t_len)
    hm_spec = pl.BlockSpec((1, RW_HEADS, tb, RW_HEAD_DIM), lambda bi, i: (bi, 0, i, 0))
    par = pl.BlockSpec((RW_HEADS, 1, RW_HEAD_DIM), lambda bi, i: (0, 0, 0))
    hshape = (RW_HEADS, 1, RW_HEAD_DIM)
    return pl.pallas_call(
        _rwscan_kernel,
        out_shape=jax.ShapeDtypeStruct((bsz, RW_HEADS, t_len, RW_HEAD_DIM), BF16),
        grid=(bsz, t_len // tb),
        in_specs=[hm_spec] * 7 + [par] * 3,
        out_specs=hm_spec,
        scratch_shapes=[pltpu.VMEM((RW_HEADS, RW_HEAD_DIM, RW_HEAD_DIM), F32),
                        pltpu.VMEM((RW_HEADS, tb, RW_HEAD_DIM), BF16), pltpu.VMEM((RW_HEADS, tb, RW_HEAD_DIM), F32),
                        pltpu.VMEM((RW_HEADS, tb, RW_HEAD_DIM), BF16), pltpu.VMEM((RW_HEADS, tb, RW_HEAD_DIM), F32),
                        pltpu.VMEM((RW_HEADS, tb // RW_CHUNK, 1, RW_HEAD_DIM), F32)],
        compiler_params=_cparams(("parallel", "arbitrary")),
        name="rwscan",
    )(r, lw, k, v, a, b, g, r_k.reshape(hshape), gn_gain.reshape(hshape), gn_bias.reshape(hshape))


FOX_PAIRS = FOX_HEADS // 2
FOX_EXTRA = 3
FOX_ACC_ROWS = FOX_HEAD_DIM + 16
FOX_LOOKAHEAD = 2


def _fox_bias_selector():
    sel = np.zeros((LANES, 2 * FOX_HEADS * LANES), np.float32)
    for h in range(FOX_HEADS):
        base = FOX_HEAD_DIM if h % 2 == 0 else 0
        for p in range(FOX_EXTRA):
            sel[p * FOX_HEADS + h, h * LANES + base + p] = 1.0
            sel[p * FOX_HEADS + h, (FOX_HEADS + h) * LANES + base + FOX_EXTRA + p] = -1.0
    return sel


def _foxprep_kernel(z_ref, f_ref, fb_ref, qg_ref, kg_ref, sel_ref, q_o, k_o, vt_o, carry_ref):
    tm = z_ref.shape[1]

    @pl.when(pl.program_id(1) == 0)
    def _():
        carry_ref[...] = jnp.zeros_like(carry_ref)

    log_f = _log_sigmoid(f_ref[0] + fb_ref[...])
    cum = carry_ref[...] + _tri_cumsum(log_f, tm)
    carry_ref[...] = cum[tm - 1:tm, :]
    lane = lax.broadcasted_iota(I32, (tm, LANES), 1)
    hi, mid, lo = (p.astype(F32) for p in _split3(cum * LOG2E))
    packed = jnp.where(lane < FOX_HEADS, hi,
                       jnp.where(lane < 2 * FOX_HEADS, pltpu.roll(mid, FOX_HEADS, axis=1),
                                 pltpu.roll(lo, 2 * FOX_HEADS, axis=1)))
    packed = jnp.where(lane < FOX_EXTRA * FOX_HEADS, packed, 0.0).astype(BF16)
    extra = jnp.dot(packed, sel_ref[...], preferred_element_type=F32)

    left = lane < FOX_HEAD_DIM
    in_half = lane % FOX_HEAD_DIM
    ones_q = jnp.where((in_half >= FOX_EXTRA) & (in_half < 2 * FOX_EXTRA), 1.0, 0.0)
    ones_k = jnp.where(in_half < FOX_EXTRA, 1.0, 0.0)

    def normed(block, gain):
        sq = block * block
        s_left = jnp.sum(jnp.where(left, sq, 0.0), axis=-1, keepdims=True)
        s_right = jnp.sum(jnp.where(left, 0.0, sq), axis=-1, keepdims=True)
        ms = jnp.where(left, s_left, s_right) * (1.0 / FOX_HEAD_DIM)
        return block * lax.rsqrt(ms + EPS) * gain

    for j in range(FOX_PAIRS):
        qn = normed(z_ref[0, :, j * LANES:(j + 1) * LANES].astype(F32), qg_ref[...] * (ATTN_SCALE * LOG2E))
        kn = normed(z_ref[0, :, FOX_WIDTH + j * LANES:FOX_WIDTH + (j + 1) * LANES].astype(F32), kg_ref[...])
        for par in range(2):
            h = 2 * j + par
            own = left if par == 0 else jnp.logical_not(left)
            q_o[0, h] = jnp.where(own, qn, extra[:, h * LANES:(h + 1) * LANES] + ones_q).astype(BF16)
            k_o[0, h] = jnp.where(own, kn, extra[:, (FOX_HEADS + h) * LANES:(FOX_HEADS + h + 1) * LANES]
                                  + ones_k).astype(BF16)
    ri = lax.broadcasted_iota(I32, (FOX_WIDTH, FOX_WIDTH), 0)
    ci = lax.broadcasted_iota(I32, (FOX_WIDTH, FOX_WIDTH), 1)
    eye = jnp.where(ri == ci, 1.0, 0.0).astype(BF16)
    v = z_ref[0, :, 2 * FOX_WIDTH:3 * FOX_WIDTH].astype(BF16)
    vt_o[0] = lax.dot_general(eye, v, (((1,), (1,)), ((), ())), preferred_element_type=F32).astype(BF16)


def _foxprep(z, z_rw, f_bias_pad, q_gain, k_gain):
    bsz, t_len, _ = z.shape
    tm = min(512, t_len)
    qk = jax.ShapeDtypeStruct((bsz, FOX_HEADS, t_len, LANES), BF16)
    qk_spec = pl.BlockSpec((1, FOX_HEADS, tm, LANES), lambda b, i: (b, 0, i, 0))
    sel = jnp.asarray(_fox_bias_selector(), BF16)
    return pl.pallas_call(
        _foxprep_kernel,
        out_shape=(qk, qk, jax.ShapeDtypeStruct((bsz, FOX_WIDTH, t_len), BF16)),
        grid=(bsz, t_len // tm),
        in_specs=[pl.BlockSpec((1, tm, 3 * FOX_WIDTH), lambda b, i: (b, i, Z_FOX // (3 * FOX_WIDTH))),
                  pl.BlockSpec((1, tm, LANES), lambda b, i: (b, i, (Z_F - Z_RW) // LANES)),
                  pl.BlockSpec((1, LANES), lambda b, i: (0, 0)),
                  pl.BlockSpec((1, LANES), lambda b, i: (0, 0)),
                  pl.BlockSpec((1, LANES), lambda b, i: (0, 0)),
                  pl.BlockSpec(sel.shape, lambda b, i: (0, 0))],
        out_specs=(qk_spec, qk_spec, pl.BlockSpec((1, FOX_WIDTH, tm), lambda b, i: (b, 0, i))),
        scratch_shapes=[pltpu.VMEM((1, LANES), F32)],
        compiler_params=_cparams(("parallel", "arbitrary")),
        name="foxprep",
    )(z, z_rw, f_bias_pad, jnp.tile(q_gain.reshape(1, -1), (1, 2)), jnp.tile(k_gain.reshape(1, -1), (1, 2)), sel)


def _fox_kernel(qi_ref, kj_ref, q_ref, k_ref, vt_ref, o_ref, m_ref, acc_ref):
    i = qi_ref[pl.program_id(1)]
    j = kj_ref[pl.program_id(1)]
    tq = q_ref.shape[2]
    tk = k_ref.shape[2]
    sub = 8

    @pl.when(j == 0)
    def _():
        m_ref[...] = jnp.full_like(m_ref, MASK_VALUE)
        acc_ref[...] = jnp.zeros_like(acc_ref)

    ones_rows = jnp.ones((FOX_ACC_ROWS - FOX_HEAD_DIM, tk), BF16)

    def scores(h):
        return lax.dot_general(k_ref[0, h], q_ref[0, h], (((1,), (1,)), ((), ())), preferred_element_type=F32)

    def update(diagonal):
        if diagonal:
            key = lax.broadcasted_iota(I32, (tk, tq), 0)
            qry = lax.broadcasted_iota(I32, (tk, tq), 1)
            keep = key <= qry
        def accumulate(h, p, alpha):
            lhs = jnp.concatenate([vt_ref[0, h * FOX_HEAD_DIM:(h + 1) * FOX_HEAD_DIM, :], ones_rows], axis=0)
            pv = jnp.dot(lhs, p, preferred_element_type=F32)
            acc = acc_ref[h].reshape(FOX_ACC_ROWS // sub, sub, tq) * alpha[None]
            acc_ref[h] = acc.reshape(FOX_ACC_ROWS, tq) + pv

        ahead = [scores(h) for h in range(FOX_LOOKAHEAD)]
        pending = None
        for h in range(FOX_HEADS):
            s = ahead.pop(0)
            if h + FOX_LOOKAHEAD < FOX_HEADS:
                ahead.append(scores(h + FOX_LOOKAHEAD))
            if diagonal:
                s = jnp.where(keep, s, MASK_VALUE)
            s3 = s.reshape(tk // sub, sub, tq)
            m_prev = m_ref[h]
            m_cur = jnp.max(jnp.max(s3, axis=0), axis=0, keepdims=True)
            m_new = jnp.maximum(m_prev, m_cur)
            alpha = jnp.exp2(m_prev - m_new)
            p = jnp.exp2(s3 - m_new[None]).astype(BF16).reshape(tk, tq)
            m_ref[h] = m_new
            if pending is not None:
                accumulate(*pending)
            pending = (h, p, alpha)
        accumulate(*pending)

    @pl.when(j < i)
    def _():
        update(False)

    @pl.when(j == i)
    def _():
        update(True)
        outs = []
        for h in range(FOX_HEADS):
            acc = acc_ref[h]
            outs.append((acc[:FOX_HEAD_DIM] / acc[FOX_HEAD_DIM:FOX_HEAD_DIM + 1]).astype(BF16))
        out_t = jnp.concatenate(outs, axis=0)
        ri = lax.broadcasted_iota(I32, (tq, tq), 0)
        ci = lax.broadcasted_iota(I32, (tq, tq), 1)
        eye = jnp.where(ri == ci, 1.0, 0.0).astype(BF16)
        o_ref[0] = lax.dot_general(eye, out_t, (((1,), (1,)), ((), ())),
                                   preferred_element_type=F32).astype(o_ref.dtype)


def _fox(q, k, vt):
    bsz, _, t_len, _ = q.shape
    tq = min(512, t_len)
    n_blk = t_len // tq
    pairs = [(i, j) for i in range(n_blk) for j in range(i + 1)]
    qi = jnp.asarray([p[0] for p in pairs], I32)
    kj = jnp.asarray([p[1] for p in pairs], I32)
    grid_spec = pltpu.PrefetchScalarGridSpec(
        num_scalar_prefetch=2,
        grid=(bsz, len(pairs)),
        in_specs=[pl.BlockSpec((1, FOX_HEADS, tq, LANES), lambda b, s, qi, kj: (b, 0, qi[s], 0)),
                  pl.BlockSpec((1, FOX_HEADS, tq, LANES), lambda b, s, qi, kj: (b, 0, kj[s], 0)),
                  pl.BlockSpec((1, FOX_WIDTH, tq), lambda b, s, qi, kj: (b, 0, kj[s]))],
        out_specs=pl.BlockSpec((1, tq, FOX_WIDTH), lambda b, s, qi, kj: (b, qi[s], 0)),
        scratch_shapes=[pltpu.VMEM((FOX_HEADS, 8, tq), F32), pltpu.VMEM((FOX_HEADS, FOX_ACC_ROWS, tq), F32)],
    )
    return pl.pallas_call(
        _fox_kernel,
        out_shape=jax.ShapeDtypeStruct((bsz, t_len, FOX_WIDTH), BF16),
        grid_spec=grid_spec,
        compiler_params=_cparams(("parallel", "arbitrary")),
        name="fox",
    )(qi, kj, q, k, vt)


def _merge_kernel(zg_ref, ygm_ref, yrw_ref, yfox_ref, x_ref, g1_ref, sc2_ref, sh2_ref, pb_ref, wo_ref, wr_ref, br_ref,
                  x1_o, h2_o, idx_o, gate_o, rank_o, cnt_o, carry_ref):
    tm = x_ref.shape[1]

    @pl.when((pl.program_id(0) == 0) & (pl.program_id(1) == 0))
    def _():
        carry_ref[...] = jnp.zeros_like(carry_ref)

    sg = 0.5 * jnp.tanh(0.5 * zg_ref[0].astype(F32)) + 0.5
    p_gm = jnp.dot(ygm_ref[0], pb_ref[0:GM_WIDTH, :], preferred_element_type=F32)
    y_rw = jnp.concatenate([yrw_ref[0, h] for h in range(RW_HEADS)], axis=-1)
    p_rw = jnp.dot(y_rw, pb_ref[GM_WIDTH:GM_WIDTH + RW_WIDTH, :], preferred_element_type=F32)
    p_fox = jnp.dot(yfox_ref[0], pb_ref[GM_WIDTH + RW_WIDTH:, :], preferred_element_type=F32)
    merged = sg[:, 0:D_MODEL] * p_gm + sg[:, D_MODEL:2 * D_MODEL] * p_rw + sg[:, 2 * D_MODEL:] * p_fox
    x1 = x_ref[0] + g1_ref[0] * jnp.dot(merged.astype(BF16), wo_ref[...], preferred_element_type=F32)
    x1_o[0] = x1
    h2 = x1 * lax.rsqrt(jnp.mean(x1 * x1, axis=-1, keepdims=True) + EPS) * (1.0 + sc2_ref[0]) + sh2_ref[0]
    h2_o[0] = _pack_halves(h2)

    h_hi, h_lo, _ = _split3(h2)
    w_hi, w_lo, _ = _split3(wr_ref[...])
    logits = (jnp.dot(h_hi, w_hi, preferred_element_type=F32) + jnp.dot(h_hi, w_lo, preferred_element_type=F32)
              + jnp.dot(h_lo, w_hi, preferred_element_type=F32)) + br_ref[...]
    lane = lax.broadcasted_iota(I32, (tm, N_EXPERTS), 1)
    vals, idxs = [], []
    rest = logits
    for _ in range(TOP_K):
        m = jnp.max(rest, axis=-1, keepdims=True)
        am = jnp.min(jnp.where(rest == m, lane, N_EXPERTS), axis=-1, keepdims=True)
        vals.append(m)
        idxs.append(am)
        rest = jnp.where(lane == am, -jnp.inf, rest)
    exps = [jnp.exp(val - vals[0]) for val in vals]
    denom = exps[0] + exps[1] + exps[2] + exps[3]

    onehot = jnp.zeros((tm, N_EXPERTS), F32)
    for am in idxs:
        onehot = onehot + jnp.where(lane == am, 1.0, 0.0)
    ri = lax.broadcasted_iota(I32, (tm, tm), 0)
    ci = lax.broadcasted_iota(I32, (tm, tm), 1)
    before = jnp.where(ri > ci, 1.0, 0.0).astype(BF16)
    seen = carry_ref[...] + jnp.dot(before, onehot.astype(BF16), preferred_element_type=F32)
    lane_k = lax.broadcasted_iota(I32, (tm, TOP_K), 1)
    idx_out = jnp.zeros((tm, TOP_K), I32)
    gate_out = jnp.zeros((tm, TOP_K), F32)
    rank_out = jnp.zeros((tm, TOP_K), I32)
    for kk in range(TOP_K):
        rank = jnp.sum(jnp.where(lane == idxs[kk], seen, 0.0), axis=-1, keepdims=True).astype(I32)
        idx_out = jnp.where(lane_k == kk, idxs[kk], idx_out)
        gate_out = jnp.where(lane_k == kk, exps[kk] / denom, gate_out)
        rank_out = jnp.where(lane_k == kk, rank, rank_out)
    idx_o[0] = idx_out
    gate_o[0] = gate_out
    rank_o[0] = rank_out
    total = carry_ref[...] + jnp.sum(onehot, axis=0, keepdims=True)
    carry_ref[...] = total
    cnt_o[...] = total.astype(I32)


def _merge(z, y_gm, y_rw, y_fox, x, gate1, scale2, shift2, w_branch, w_o, w_router, b_router, layer):
    bsz, t_len, d = x.shape
    tm = min(512, t_len)
    row = lambda w: pl.BlockSpec((1, tm, w), lambda b, i: (b, i, 0))
    mod = pl.BlockSpec((1, 1, d), lambda b, i: (b, 0, 0))
    full = lambda shape: pl.BlockSpec(shape, lambda b, i: (0,) * len(shape))
    return pl.pallas_call(
        _merge_kernel,
        out_shape=(jax.ShapeDtypeStruct((bsz, t_len, d), F32), jax.ShapeDtypeStruct((bsz, t_len, d // 2), I32),
                   jax.ShapeDtypeStruct((bsz, t_len, TOP_K), I32), jax.ShapeDtypeStruct((bsz, t_len, TOP_K), F32),
                   jax.ShapeDtypeStruct((bsz, t_len, TOP_K), I32), jax.ShapeDtypeStruct((1, N_EXPERTS), I32)),
        grid=(bsz, t_len // tm),
        in_specs=[row(N_BRANCH * D_MODEL), row(GM_WIDTH),
                  pl.BlockSpec((1, RW_HEADS, tm, RW_HEAD_DIM), lambda b, i: (b, 0, i, 0)),
                  row(FOX_WIDTH), row(d), mod, mod, mod,
                  pl.BlockSpec((MIX_WIDTH, d), lambda b, i: (layer, 0)), pl.BlockSpec((d, d), lambda b, i: (layer, 0)),
                  full(w_router.shape), full((1, N_EXPERTS))],
        out_specs=(row(d), row(d // 2), row(TOP_K), row(TOP_K), row(TOP_K), full((1, N_EXPERTS))),
        scratch_shapes=[pltpu.VMEM((1, N_EXPERTS), F32)],
        compiler_params=_cparams(("arbitrary", "arbitrary")),
        name="merge_router",
    )(z, y_gm, y_rw, y_fox, x, gate1, scale2, shift2, w_branch, w_o, w_router, b_router.reshape(1, N_EXPERTS))


def _sc_mesh():
    return plsc.VectorSubcoreMesh(core_axis_name="c", subcore_axis_name="s",
                                  num_cores=SC_CORES, num_subcores=SC_SUBCORES)


def _sc_worker():
    return lax.axis_index("s") * SC_CORES + lax.axis_index("c")


def _sc_scatter_rows(src, idx3, n_out):
    _, d = src.shape
    n_copy, n_grp, _ = idx3.shape
    grp_per_w = n_grp // SC_WORKERS
    assert grp_per_w % 2 == 0

    def body(src_hbm, idx_hbm, out_hbm, idx_v, rows_a, rows_b, sem):
        g0 = _sc_worker() * grp_per_w
        for q in range(n_copy):
            pltpu.sync_copy(idx_hbm.at[q, pl.ds(g0, grp_per_w)], idx_v.at[pl.ds(q * grp_per_w, grp_per_w)])

        @pl.loop(0, grp_per_w, step=2)
        def _(j):
            read_a = pltpu.async_copy(src_hbm.at[pl.ds((g0 + j) * SC_ROWS, SC_ROWS)], rows_a, sem.at[0])
            read_b = pltpu.async_copy(src_hbm.at[pl.ds((g0 + j + 1) * SC_ROWS, SC_ROWS)], rows_b, sem.at[1])
            read_a.wait()
            put_a = [pltpu.async_copy(rows_a, out_hbm.at[idx_v.at[q * grp_per_w + j]], sem.at[2])
                     for q in range(n_copy)]
            read_b.wait()
            put_b = [pltpu.async_copy(rows_b, out_hbm.at[idx_v.at[q * grp_per_w + j + 1]], sem.at[3])
                     for q in range(n_copy)]
            for cp in put_a + put_b:
                cp.wait()

    return pl.kernel(
        body, out_type=jax.ShapeDtypeStruct((n_out, d), src.dtype), mesh=_sc_mesh(),
        scratch_types=[pltpu.VMEM((n_copy * grp_per_w, SC_ROWS), I32), pltpu.VMEM((SC_ROWS, d), src.dtype),
                       pltpu.VMEM((SC_ROWS, d), src.dtype), pltpu.SemaphoreType.DMA((4,))],
        name="sc_dispatch",
    )(src, idx3)


def _sc_gather_rows(table, idx2):
    _, d = table.shape
    n_grp, _ = idx2.shape
    grp_per_w = n_grp // SC_WORKERS
    assert grp_per_w % 2 == 0

    def body(table_hbm, idx_hbm, out_hbm, idx_v, rows_a, rows_b, sem):
        g0 = _sc_worker() * grp_per_w
        pltpu.sync_copy(idx_hbm.at[pl.ds(g0, grp_per_w)], idx_v)

        @pl.loop(0, grp_per_w, step=2)
        def _(j):
            get_a = pltpu.async_copy(table_hbm.at[idx_v.at[j]], rows_a, sem.at[0])
            get_b = pltpu.async_copy(table_hbm.at[idx_v.at[j + 1]], rows_b, sem.at[1])
            get_a.wait()
            put_a = pltpu.async_copy(rows_a, out_hbm.at[pl.ds((g0 + j) * SC_ROWS, SC_ROWS)], sem.at[2])
            get_b.wait()
            put_b = pltpu.async_copy(rows_b, out_hbm.at[pl.ds((g0 + j + 1) * SC_ROWS, SC_ROWS)], sem.at[3])
            put_a.wait()
            put_b.wait()

    return pl.kernel(
        body, out_type=jax.ShapeDtypeStruct((n_grp * SC_ROWS, d), table.dtype), mesh=_sc_mesh(),
        scratch_types=[pltpu.VMEM((grp_per_w, SC_ROWS), I32), pltpu.VMEM((SC_ROWS, d), table.dtype),
                       pltpu.VMEM((SC_ROWS, d), table.dtype), pltpu.SemaphoreType.DMA((4,))],
        name="sc_combine_gather",
    )(table, idx2)


def _ffn_weight_copies(expert, wgu_hbm, wd_hbm, stage_gu, stage_d, sem):
    return (pltpu.make_async_copy(wgu_hbm.at[expert], stage_gu, sem.at[0]),
            pltpu.make_async_copy(wd_hbm.at[expert], stage_d, sem.at[1]))


def _ffn_kernel(be_ref, first_ref, nxt_ref, nv_ref, x_ref, wgu_hbm, wd_hbm, bgu_ref, bd_ref, o_ref,
                stage_gu, stage_d, wgu_b, wd_b, sem):
    step = pl.program_id(0)
    copies = functools.partial(_ffn_weight_copies, wgu_hbm=wgu_hbm, wd_hbm=wd_hbm, stage_gu=stage_gu,
                               stage_d=stage_d, sem=sem)

    @pl.when(step == 0)
    def _():
        for cp in copies(be_ref[0]):
            cp.start()

    def switch_weights(idx):
        @pl.when(first_ref[idx] == 1)
        def _():
            for cp in copies(be_ref[idx]):
                cp.wait()
            wgu_b[...] = stage_gu[...].astype(BF16)
            wd_b[...] = stage_d[...].astype(BF16)

            @pl.when(nxt_ref[idx] >= 0)
            def _():
                for cp in copies(nxt_ref[idx]):
                    cp.start()

    def compute(row0, n_rows, n_valid, expert):
        rows = slice(row0, row0 + n_rows)
        rowid = lax.broadcasted_iota(I32, (n_rows, x_ref.shape[1]), 0)
        xp = jnp.where(rowid < n_valid, x_ref[rows, :], 0)
        x = jnp.concatenate(_unpack_halves(xp), axis=-1).astype(BF16)
        gu = jnp.dot(x, wgu_b[...], preferred_element_type=F32) + bgu_ref[expert]
        g_ = jnp.minimum(gu[:, :D_FF], SWIGLU_LIMIT)
        u_ = jnp.clip(gu[:, D_FF:], -SWIGLU_LIMIT, SWIGLU_LIMIT)
        act = (u_ + 1.0) * (g_ * jax.nn.sigmoid(SWIGLU_ALPHA * g_))
        y = jnp.dot(act.astype(BF16), wd_b[...], preferred_element_type=F32) + bd_ref[expert]
        o_ref[rows, :] = _pack_halves(y)

    def single(idx, row0):
        switch_weights(idx)

        @pl.when(nv_ref[idx] > 0)
        def _():
            compute(row0, MOE_BLOCK, nv_ref[idx], be_ref[idx])

        @pl.when(nv_ref[idx] <= 0)
        def _():
            o_ref[row0:row0 + MOE_BLOCK, :] = jnp.zeros((MOE_BLOCK, o_ref.shape[1]), o_ref.dtype)

    for b in range(FFN_STEP_BLOCKS):
        single(step * FFN_STEP_BLOCKS + b, b * MOE_BLOCK)


def _ffn(block_expert, block_first, block_next, block_valid, xin, w_gate_up, b_gate_up, w_down, b_down):
    n_rows, dp = xin.shape
    d = 2 * dp
    step_rows = FFN_STEP_BLOCKS * MOE_BLOCK
    resident = lambda arr: pl.BlockSpec(arr.shape, lambda i, *_: (0,) * arr.ndim)
    grid_spec = pltpu.PrefetchScalarGridSpec(
        num_scalar_prefetch=4,
        grid=(n_rows // step_rows,),
        in_specs=[pl.BlockSpec((step_rows, dp), lambda i, *_: (i, 0)),
                  pl.BlockSpec(memory_space=pl.ANY), pl.BlockSpec(memory_space=pl.ANY),
                  resident(b_gate_up), resident(b_down)],
        out_specs=pl.BlockSpec((step_rows, dp), lambda i, *_: (i, 0)),
        scratch_shapes=[pltpu.VMEM((d, 2 * D_FF), F32), pltpu.VMEM((D_FF, d), F32),
                        pltpu.VMEM((d, 2 * D_FF), BF16), pltpu.VMEM((D_FF, d), BF16),
                        pltpu.SemaphoreType.DMA((2,))],
    )
    return pl.pallas_call(
        _ffn_kernel,
        out_shape=jax.ShapeDtypeStruct((n_rows, dp), I32),
        grid_spec=grid_spec,
        compiler_params=pltpu.CompilerParams(dimension_semantics=("arbitrary",), vmem_limit_bytes=FFN_VMEM_LIMIT),
        name="expert_ffn",
    )(block_expert, block_first, block_next, block_valid, xin, w_gate_up, w_down, b_gate_up, b_down)


def _combine_kernel(x1_ref, g2_ref, gate_ref, yg_ref, o_ref):
    gate = gate_ref[0]
    y_lo = y_hi = None
    for q in range(TOP_K):
        lo, hi = _unpack_halves(yg_ref[q, 0])
        wq = gate[:, q:q + 1]
        y_lo = wq * lo if y_lo is None else y_lo + wq * lo
        y_hi = wq * hi if y_hi is None else y_hi + wq * hi
    o_ref[0] = x1_ref[0] + g2_ref[0] * jnp.concatenate([y_lo, y_hi], axis=-1)


def _combine(x1, gate2, gate, yg):
    bsz, t_len, d = x1.shape
    tm = min(512, t_len)
    return pl.pallas_call(
        _combine_kernel,
        out_shape=jax.ShapeDtypeStruct((bsz, t_len, d), F32),
        grid=(bsz, t_len // tm),
        in_specs=[pl.BlockSpec((1, tm, d), lambda b, i: (b, i, 0)),
                  pl.BlockSpec((1, 1, d), lambda b, i: (b, 0, 0)),
                  pl.BlockSpec((1, tm, TOP_K), lambda b, i: (b, i, 0)),
                  pl.BlockSpec((TOP_K, 1, tm, d // 2), lambda b, i: (0, b, i, 0))],
        out_specs=pl.BlockSpec((1, tm, d), lambda b, i: (b, i, 0)),
        compiler_params=_cparams(("parallel", "parallel")),
        name="moe_combine",
    )(x1, gate2, gate, yg)


def _moe(x1, gate2, h2, top_idx, gate, rank, counts, w_gate_up, b_gate_up, w_down, b_down, layer):
    bsz, t_len, d = h2.shape
    n_tok = bsz * t_len
    n_assign = n_tok * TOP_K
    n_blocks = -(-n_assign // MOE_BLOCK) + N_EXPERTS
    counts = counts.reshape(N_EXPERTS)
    blocks_e = (counts + MOE_BLOCK - 1) // MOE_BLOCK
    blk_end = jnp.cumsum(blocks_e)
    blk_start = blk_end - blocks_e
    experts = jnp.arange(N_EXPERTS, dtype=I32)
    onehot = top_idx.reshape(n_tok, TOP_K, 1) == experts
    dest = jnp.sum(jnp.where(onehot, blk_start * MOE_BLOCK, 0), axis=-1) + rank.reshape(n_tok, TOP_K)
    dest_t = dest.T.astype(I32)
    blk = jnp.arange(n_blocks, dtype=I32)
    block_expert = jnp.minimum(jnp.sum(blk_end[None, :] <= blk[:, None], axis=1), N_EXPERTS - 1).astype(I32)
    be_hot = block_expert[:, None] == experts
    cnt_b = jnp.sum(jnp.where(be_hot, counts, 0), axis=1)
    start_b = jnp.sum(jnp.where(be_hot, blk_start, 0), axis=1)
    block_valid = jnp.clip(cnt_b - (blk - start_b) * MOE_BLOCK, 0, MOE_BLOCK).astype(I32)
    xin = _sc_scatter_rows(h2.reshape(n_tok, d), dest_t.reshape(TOP_K, n_tok // SC_ROWS, SC_ROWS),
                           n_blocks * MOE_BLOCK)
    block_first = jnp.concatenate([jnp.ones((1,), I32), (block_expert[1:] != block_expert[:-1]).astype(I32)])
    run_start = jnp.where(block_first == 1, blk, n_blocks)
    later_start = lax.cummin(jnp.concatenate([run_start[1:], jnp.full((1,), n_blocks, I32)]), reverse=True)
    next_expert = jnp.concatenate([block_expert, jnp.full((1,), -1 - layer * N_EXPERTS, I32)])[later_start]
    yb = _ffn(block_expert + layer * N_EXPERTS, block_first, next_expert + layer * N_EXPERTS, block_valid, xin,
              w_gate_up, b_gate_up, w_down, b_down)
    yg = _sc_gather_rows(yb, dest_t.reshape(n_assign // SC_ROWS, SC_ROWS))
    return _combine(x1, gate2, gate, yg.reshape(TOP_K, bsz, t_len, d))


def _permute_kernel(w_ref, o_ref):
    o_gm = 0
    o_rw = o_gm + 2 * GM_WIDTH
    o_fox = o_rw + RW_SHIFT_WIDTH
    o_f = o_fox + 3 * FOX_WIDTH
    o_gate = o_f + FOX_HEADS
    w = w_ref[0]
    o_ref[0, :, Z_GATE:Z_FOX] = w[:, o_gate:o_gate + N_BRANCH * D_MODEL].astype(BF16)
    o_ref[0, :, Z_FOX:Z_GM] = w[:, o_fox:o_f].astype(BF16)
    o_ref[0, :, Z_GM:Z_RW] = w[:, o_gm:o_rw].astype(BF16)
    o_ref[0, :, Z_RW:Z_F] = w[:, o_rw:o_fox].astype(BF16)
    tail = jnp.concatenate([w[:, o_f:o_gate], jnp.zeros((w.shape[0], Z_WIDTH - Z_F - FOX_HEADS), F32)], axis=-1)
    o_ref[0, :, Z_F:Z_WIDTH] = tail.astype(BF16)


def _permute_w_in(w_in):
    n_layer, d, w_cols = w_in.shape
    tr = 256
    return pl.pallas_call(
        _permute_kernel,
        out_shape=jax.ShapeDtypeStruct((n_layer, d, Z_WIDTH), BF16),
        grid=(n_layer, d // tr),
        in_specs=[pl.BlockSpec((1, tr, w_cols), lambda l, i: (l, i, 0))],
        out_specs=pl.BlockSpec((1, tr, Z_WIDTH), lambda l, i: (l, i, 0)),
        compiler_params=_cparams(("parallel", "parallel")),
        name="permute_w_in",
    )(w_in)


def _layer(x, mod, w_in_p, gm_v_gain, gm_w_s, gm_b_s, mu_pad, w_lora, rw_w0, rw_a0, rw_k_k, rw_k_a, rw_r_k,
           rw_gn_gain, rw_gn_bias, f_bias_pad, fox_q_gain, fox_k_gain, w_branch, w_o, w_router, b_router,
           w_gate_up, b_gate_up, w_down, b_down, layer):
    shift1, scale1, gate1, shift2, scale2, gate2 = (mod[:, i][:, None, :] for i in range(6))
    z, z_rw = _inproj(x, scale1, shift1, w_in_p, layer)
    y_gm = _gmlp(z, gm_v_gain, gm_w_s, gm_b_s)
    r, lw, k, v, a, b, g = _rwprep(z_rw, mu_pad, w_lora, rw_w0, rw_a0, rw_k_k, rw_k_a)
    y_rw = _rwscan(r, lw, k, v, a, b, g, rw_r_k, rw_gn_gain, rw_gn_bias)
    q, kf, vf = _foxprep(z, z_rw, f_bias_pad, fox_q_gain, fox_k_gain)
    y_fox = _fox(q, kf, vf)
    x1, h2, top_idx, gate, rank, counts = _merge(z, y_gm, y_rw, y_fox, x, gate1, scale2, shift2,
                                                 w_branch, w_o, w_router, b_router, layer)
    return _moe(x1, gate2, h2, top_idx, gate, rank, counts, w_gate_up, b_gate_up, w_down, b_down, layer)


def kernel(x, c, w_ada, b_ada, w_in, gm_v_gain, gm_w_s, gm_b_s, rw_mu, rw_w0, rw_w2, rw_a0, rw_a2, rw_g2, rw_k_k,
           rw_k_a, rw_r_k, rw_gn_gain, rw_gn_bias, fox_f_bias, fox_q_gain, fox_k_gain, w_branch, w_o, w_router,
           b_router, w_gate_up, b_gate_up, w_down, b_down):
    n_layer = w_ada.shape[0]
    bsz = x.shape[0]
    c_pad = jnp.zeros((8, D_MODEL), F32).at[:bsz].set(c)
    mod = _adaln(c_pad, w_ada, b_ada)[:, :bsz].reshape(n_layer, bsz, 6, D_MODEL)
    w_in_p = _permute_w_in(w_in)
    mu_pad = jnp.pad(rw_mu, ((0, 0), (0, RW_BLOCK - RW_SHIFT_WIDTH)))
    w_lora = jnp.zeros((n_layer, RW_LORA, 3 * RW_WIDTH), F32)
    w_lora = w_lora.at[:, 0:RW_DECAY_LORA, 0:RW_WIDTH].set(rw_w2)
    w_lora = w_lora.at[:, RW_DECAY_LORA:RW_DECAY_LORA + RW_ICLR_LORA, RW_WIDTH:2 * RW_WIDTH].set(rw_a2)
    w_lora = w_lora.at[:, RW_DECAY_LORA + RW_ICLR_LORA:, 2 * RW_WIDTH:].set(rw_g2)
    f_bias_pad = jnp.pad(fox_f_bias, ((0, 0), (0, LANES - FOX_HEADS)))
    w_in_p = w_in_p.reshape(n_layer * D_MODEL, Z_WIDTH)
    w_branch_b = w_branch.astype(BF16).reshape(n_layer * MIX_WIDTH, D_MODEL)
    w_o_b = w_o.astype(BF16).reshape(n_layer * D_MODEL, D_MODEL)
    w_gu = w_gate_up.reshape(n_layer * N_EXPERTS, D_MODEL, 2 * D_FF)
    b_gu = b_gate_up.reshape(n_layer * N_EXPERTS, 1, 2 * D_FF)
    w_dn = w_down.reshape(n_layer * N_EXPERTS, D_FF, D_MODEL)
    b_dn = b_down.reshape(n_layer * N_EXPERTS, 1, D_MODEL)
    for l in range(n_layer):
        x = _layer(x, mod[l], w_in_p, gm_v_gain[l], gm_w_s[l], gm_b_s[l], mu_pad[l:l + 1], w_lora[l], rw_w0[l],
                   rw_a0[l], rw_k_k[l], rw_k_a[l], rw_r_k[l], rw_gn_gain[l], rw_gn_bias[l], f_bias_pad[l:l + 1],
                   fox_q_gain[l], fox_k_gain[l], w_branch_b, w_o_b, w_router[l], b_router[l],
                   w_gu, b_gu, w_dn, b_dn, l)
    return x
```

```python
import functools

import jax
import jax.numpy as jnp
import numpy as np
from jax import lax
from jax.experimental import pallas as pl
from jax.experimental.pallas import tpu as pltpu
from jax.experimental.pallas import tpu_sc as plsc

F32 = jnp.float32
BF16 = jnp.bfloat16
I32 = jnp.int32
HIGHEST = lax.Precision.HIGHEST

D_MODEL = 1024
GM_CHUNK = 128
GM_GROUPS = 4
GM_WIDTH = 256
GM_GROUP_DIM = GM_WIDTH // GM_GROUPS
RW_HEADS = 4
RW_HEAD_DIM = 64
RW_WIDTH = RW_HEADS * RW_HEAD_DIM
RW_DECAY_LORA = 32
RW_ICLR_LORA = 32
RW_GATE_LORA = 64
RW_LORA = RW_DECAY_LORA + RW_ICLR_LORA + RW_GATE_LORA
RW_SHIFT_WIDTH = 3 * RW_WIDTH + RW_LORA
RW_GN_EPS = 64e-5
FOX_HEADS = 8
FOX_HEAD_DIM = 64
FOX_WIDTH = FOX_HEADS * FOX_HEAD_DIM
ATTN_SCALE = FOX_HEAD_DIM ** -0.5
MASK_VALUE = -1e30
LOG2E = 1.4426950408889634
N_BRANCH = 3
MIX_WIDTH = GM_WIDTH + RW_WIDTH + FOX_WIDTH
N_EXPERTS = 32
TOP_K = 4
D_FF = D_MODEL
SWIGLU_LIMIT = 7.0
SWIGLU_ALPHA = 1.702
MOE_BLOCK = 256
EPS = 1e-6

Z_GATE = 0
Z_FOX = N_BRANCH * D_MODEL
Z_GM = Z_FOX + 3 * FOX_WIDTH
Z_RW = Z_GM + 2 * GM_WIDTH
RW_BLOCK = 1024
Z_F = Z_RW + RW_SHIFT_WIDTH
Z_WIDTH = Z_RW + RW_BLOCK
LANES = 128
RW_CHUNK = 64
RW_PREP_UNROLL = 4

VMEM_LIMIT = 48 * 1024 * 1024
FFN_VMEM_LIMIT = 56 * 1024 * 1024
FFN_STEP_BLOCKS = 4
SC_CORES = 2
SC_SUBCORES = 16
SC_WORKERS = SC_CORES * SC_SUBCORES
SC_ROWS = 64


def _cparams(sem):
    return pltpu.CompilerParams(dimension_semantics=sem, vmem_limit_bytes=VMEM_LIMIT)


def _mm(a, b):
    return jnp.dot(a.astype(BF16), b.astype(BF16), preferred_element_type=F32)


def _mm_nt(a, b):
    return lax.dot_general(a.astype(BF16), b.astype(BF16), (((1,), (1,)), ((), ())), preferred_element_type=F32)


def _mm_tn(a, b):
    return lax.dot_general(a.astype(BF16), b.astype(BF16), (((0,), (0,)), ((), ())), preferred_element_type=F32)


def _split3(x):
    hi = x.astype(BF16)
    r1 = x - hi.astype(F32)
    mid = r1.astype(BF16)
    lo = (r1 - mid.astype(F32)).astype(BF16)
    return hi, mid, lo


def _tri_cumsum(x, n):
    ri = lax.broadcasted_iota(I32, (n, n), 0)
    ci = lax.broadcasted_iota(I32, (n, n), 1)
    ones = jnp.where(ri >= ci, 1.0, 0.0).astype(BF16)
    hi, mid, lo = _split3(x)
    return (jnp.dot(ones, hi, preferred_element_type=F32) + jnp.dot(ones, mid, preferred_element_type=F32)
            + jnp.dot(ones, lo, preferred_element_type=F32))


def _pack_halves(x):
    w = x.shape[1] // 2
    hi = pltpu.bitcast(x[:, :w].astype(BF16).astype(F32), jnp.uint32)
    lo = pltpu.bitcast(x[:, w:].astype(BF16).astype(F32), jnp.uint32)
    return pltpu.bitcast(hi | (lo >> 16), I32)


def _unpack_halves(p):
    u = pltpu.bitcast(p, jnp.uint32)
    return pltpu.bitcast(u & jnp.uint32(0xFFFF0000), F32), pltpu.bitcast(u << 16, F32)


def _log_sigmoid(x):
    return jnp.minimum(x, 0.0) - jnp.log1p(jnp.exp(-jnp.abs(x)))


def _adaln_kernel(c_ref, w_ref, b_ref, o_ref):
    c = c_ref[...]
    s = c * jax.nn.sigmoid(c)
    o_ref[0] = jnp.dot(s, w_ref[0], preferred_element_type=F32, precision=HIGHEST) + b_ref[0]


def _adaln(c_pad, w_ada, b_ada):
    n_layer, d, w6 = w_ada.shape
    tn = 1536
    return pl.pallas_call(
        _adaln_kernel,
        out_shape=jax.ShapeDtypeStruct((n_layer, c_pad.shape[0], w6), F32),
        grid=(n_layer, w6 // tn),
        in_specs=[pl.BlockSpec(c_pad.shape, lambda l, j: (0, 0)),
                  pl.BlockSpec((1, d, tn), lambda l, j: (l, 0, j)),
                  pl.BlockSpec((1, 1, tn), lambda l, j: (l, 0, j))],
        out_specs=pl.BlockSpec((1, c_pad.shape[0], tn), lambda l, j: (l, 0, j)),
        compiler_params=_cparams(("parallel", "parallel")),
        name="adaln",
    )(c_pad, w_ada, b_ada.reshape(n_layer, 1, w6))


def _inproj_kernel(x_ref, sc_ref, sh_ref, w_ref, zm_ref, zr_ref, xn_ref):
    j = pl.program_id(2)

    @pl.when(j == 0)
    def _():
        x = x_ref[0]
        xn = x * lax.rsqrt(jnp.mean(x * x, axis=-1, keepdims=True) + EPS)
        xn_ref[...] = (xn * (1.0 + sc_ref[0]) + sh_ref[0]).astype(BF16)

    acc = jnp.dot(xn_ref[...], w_ref[...], preferred_element_type=F32)

    @pl.when(j < Z_RW // RW_BLOCK)
    def _():
        zm_ref[0] = acc.astype(BF16)

    @pl.when(j == Z_RW // RW_BLOCK)
    def _():
        zr_ref[0] = acc


def _inproj(x, scale, shift, w, layer):
    bsz, t_len, d = x.shape
    tm = min(1024, t_len)
    tn = RW_BLOCK
    n_main = Z_RW // tn
    return pl.pallas_call(
        _inproj_kernel,
        out_shape=(jax.ShapeDtypeStruct((bsz, t_len, Z_RW), BF16), jax.ShapeDtypeStruct((bsz, t_len, RW_BLOCK), F32)),
        grid=(bsz, t_len // tm, Z_WIDTH // tn),
        in_specs=[pl.BlockSpec((1, tm, d), lambda b, i, j: (b, i, 0)),
                  pl.BlockSpec((1, 1, d), lambda b, i, j: (b, 0, 0)),
                  pl.BlockSpec((1, 1, d), lambda b, i, j: (b, 0, 0)),
                  pl.BlockSpec((d, tn), lambda b, i, j: (layer, j))],
        out_specs=(pl.BlockSpec((1, tm, tn), lambda b, i, j: (b, i, jnp.minimum(j, n_main - 1))),
                   pl.BlockSpec((1, tm, tn), lambda b, i, j: (b, i, 0))),
        scratch_shapes=[pltpu.VMEM((tm, d), BF16)],
        compiler_params=_cparams(("parallel", "parallel", "arbitrary")),
        name="inproj",
    )(x, scale, shift, w)


def _gmlp_kernel(z_ref, gain_ref, ws_ref, bst_ref, o_ref):
    tm = z_ref.shape[1]
    z = z_ref[0].astype(F32)
    u = jax.nn.gelu(z[:, :GM_WIDTH])
    v = jax.nn.gelu(z[:, GM_WIDTH:])
    v = v * lax.rsqrt(jnp.mean(v * v, axis=-1, keepdims=True) + EPS) * gain_ref[...]
    vb = v.astype(BF16)
    grp = lax.broadcasted_iota(I32, (GM_CHUNK, GM_WIDTH), 1) // GM_GROUP_DIM
    ri = lax.broadcasted_iota(I32, (GM_CHUNK, GM_CHUNK), 0)
    ci = lax.broadcasted_iota(I32, (GM_CHUNK, GM_CHUNK), 1)
    causal = ri >= ci
    bias = jnp.zeros((GM_CHUNK, GM_WIDTH), F32)
    ws = []
    for g in range(GM_GROUPS):
        ws.append(jnp.where(causal, ws_ref[g], 0.0).astype(BF16))
        bias = jnp.where(grp == g, bst_ref[:, g:g + 1], bias)
    for c in range(tm // GM_CHUNK):
        rows = slice(c * GM_CHUNK, (c + 1) * GM_CHUNK)
        vc = vb[rows]
        mixed = bias
        for g in range(GM_GROUPS):
            m = jnp.dot(ws[g], vc, preferred_element_type=F32)
            mixed = mixed + jnp.where(grp == g, m, 0.0)
        o_ref[0, rows, :] = (u[rows] * mixed).astype(o_ref.dtype)


def _gmlp(z, gain, w_s, b_s):
    bsz, t_len, _ = z.shape
    tm = min(512, t_len)
    return pl.pallas_call(
        _gmlp_kernel,
        out_shape=jax.ShapeDtypeStruct((bsz, t_len, GM_WIDTH), BF16),
        grid=(bsz, t_len // tm),
        in_specs=[pl.BlockSpec((1, tm, 2 * GM_WIDTH), lambda b, i: (b, i, Z_GM // (2 * GM_WIDTH))),
                  pl.BlockSpec((1, GM_WIDTH), lambda b, i: (0, 0)),
                  pl.BlockSpec((GM_GROUPS, GM_CHUNK, GM_CHUNK), lambda b, i: (0, 0, 0)),
                  pl.BlockSpec((GM_CHUNK, GM_GROUPS), lambda b, i: (0, 0))],
        out_specs=pl.BlockSpec((1, tm, GM_WIDTH), lambda b, i: (b, i, 0)),
        compiler_params=_cparams(("parallel", "parallel")),
        name="gmlp",
    )(z, gain.reshape(1, GM_WIDTH), w_s, b_s.T)


def _rwprep_kernel(z_ref, zp_ref, mu_ref, wl_ref, w0_ref, a0_ref, kk_ref, ka_ref,
                   r_o, lw_o, k_o, v_o, a_o, b_o, g_o):
    tm = z_ref.shape[1]
    z = z_ref[0]
    prev = jnp.where(pl.program_id(1) > 0, zp_ref[0, 7:8, :], 0.0)
    rowid = lax.broadcasted_iota(I32, z.shape, 0)
    zs = jnp.where(rowid == 0, prev, pltpu.roll(z, 1, axis=0))
    zz = z + mu_ref[...] * (zs - z)
    r = zz[:, 0:RW_WIDTH]
    k = zz[:, RW_WIDTH:2 * RW_WIDTH]
    v = zz[:, 2 * RW_WIDTH:3 * RW_WIDTH]
    lo = zz[:, 3 * RW_WIDTH:3 * RW_WIDTH + RW_LORA]
    lane = lax.broadcasted_iota(I32, (tm, RW_LORA), 1)
    act = jnp.where(lane < RW_DECAY_LORA, jnp.tanh(lo),
                    jnp.where(lane < RW_DECAY_LORA + RW_ICLR_LORA, lo, jax.nn.sigmoid(lo)))
    a_hi, a_lo, _ = _split3(act)
    w_hi, w_lo, _ = _split3(wl_ref[...])
    proj = (jnp.dot(a_hi, w_hi, preferred_element_type=F32) + jnp.dot(a_hi, w_lo, preferred_element_type=F32)
            + jnp.dot(a_lo, w_hi, preferred_element_type=F32))
    xw = -(w0_ref[...] + proj[:, 0:RW_WIDTH])
    softplus = jnp.maximum(xw, 0.0) + jnp.log1p(jnp.exp(-jnp.abs(xw)))
    lw = -jnp.exp(-softplus - 0.5)
    a = jax.nn.sigmoid(a0_ref[...] + proj[:, RW_WIDTH:2 * RW_WIDTH])
    g = proj[:, 2 * RW_WIDTH:3 * RW_WIDTH]
    kk = k * kk_ref[...]
    k2 = k * (1.0 + (a - 1.0) * ka_ref[...])
    for h in range(RW_HEADS):
        sl = slice(h * RW_HEAD_DIM, (h + 1) * RW_HEAD_DIM)
        kkh = kk[:, sl]
        nrm = jnp.sqrt(jnp.sum(kkh * kkh, axis=-1, keepdims=True))
        kkh = kkh / jnp.maximum(nrm, 1e-12)
        r_o[0, h] = r[:, sl]
        lw_o[0, h] = lw[:, sl]
        k_o[0, h] = k2[:, sl]
        v_o[0, h] = v[:, sl]
        a_o[0, h] = -kkh
        b_o[0, h] = kkh * a[:, sl]
        g_o[0, h] = g[:, sl]


def _rwprep(z, mu_pad, w_lora, w0, a0, k_k, k_a):
    bsz, t_len, _ = z.shape
    tm = min(512, t_len)
    hm = jax.ShapeDtypeStruct((bsz, RW_HEADS, t_len, RW_HEAD_DIM), F32)
    hm_spec = pl.BlockSpec((1, RW_HEADS, tm, RW_HEAD_DIM), lambda b, i: (b, 0, i, 0))
    vec = lambda n: pl.BlockSpec((1, n), lambda b, i: (0, 0))
    rw_blk = 0
    return pl.pallas_call(
        _rwprep_kernel,
        out_shape=(hm,) * 7,
        grid=(bsz, t_len // tm),
        in_specs=[pl.BlockSpec((1, tm, RW_BLOCK), lambda b, i: (b, i, rw_blk)),
                  pl.BlockSpec((1, 8, RW_BLOCK), lambda b, i: (b, jnp.maximum(i * (tm // 8) - 1, 0), rw_blk)),
                  vec(RW_BLOCK),
                  pl.BlockSpec((RW_LORA, 3 * RW_WIDTH), lambda b, i: (0, 0)),
                  vec(RW_WIDTH), vec(RW_WIDTH), vec(RW_WIDTH), vec(RW_WIDTH)],
        out_specs=(hm_spec,) * 7,
        compiler_params=_cparams(("parallel", "parallel")),
        name="rwprep",
    )(z, z, mu_pad, w_lora, w0.reshape(1, -1), a0.reshape(1, -1), k_k.reshape(1, -1), k_a.reshape(1, -1))


def _rwscan_kernel(r_ref, lw_ref, k_ref, v_ref, a_ref, b_ref, g_ref, rk_ref, gg_ref, gb_ref, o_ref,
                   s_ref, rp_ref, y_ref, gm_ref, h0_ref, we_ref):
    cl = RW_CHUNK
    tb = r_ref.shape[2]
    n_chunk = tb // cl

    @pl.when(pl.program_id(1) == 0)
    def _():
        s_ref[...] = jnp.zeros_like(s_ref)

    n = RW_HEADS * cl
    ri = lax.broadcasted_iota(I32, (n, n), 0)
    ci = lax.broadcasted_iota(I32, (n, n), 1)
    same_head = (ri // cl) == (ci // cl)
    lower = same_head & (ri >= ci)
    strict = same_head & (ri > ci)
    eye = jnp.where(ri == ci, 1.0, 0.0)
    ones_lower = jnp.where(lower, 1.0, 0.0).astype(BF16)

    def prepare(chunks):
        grp = range(len(chunks))
        each = lambda fn: [fn(u) for u in grp]
        rows = [pl.ds(pl.multiple_of(c * cl, cl), cl) for c in chunks]
        stack = lambda ref: each(lambda u: ref[0, :, rows[u], :].reshape(n, RW_HEAD_DIM))
        r, lw, k, v, a, b = (stack(ref) for ref in (r_ref, lw_ref, k_ref, v_ref, a_ref, b_ref))
        hd = RW_HEAD_DIM
        parts = each(lambda u: jnp.concatenate(_split3(lw[u]), axis=-1))
        sums = each(lambda u: jnp.dot(ones_lower, parts[u], preferred_element_type=F32))
        cw = each(lambda u: sums[u][:, :hd] + sums[u][:, hd:2 * hd] + sums[u][:, 2 * hd:])
        w_in = each(lambda u: jnp.exp(cw[u]))
        w_inv = each(lambda u: jnp.exp(-cw[u]))
        rt = each(lambda u: r[u] * w_in[u])
        at = each(lambda u: a[u] * jnp.exp(cw[u] - lw[u]))
        kt = each(lambda u: k[u] * w_inv[u])
        bt = each(lambda u: b[u] * w_inv[u])
        w_end = each(lambda u: w_in[u].reshape(RW_HEADS, cl, RW_HEAD_DIM)[:, cl - 1:cl, :])
        w_end_rows = each(lambda u: jnp.broadcast_to(w_end[u], (RW_HEADS, cl, RW_HEAD_DIM)).reshape(n, RW_HEAD_DIM))
        a_ab = each(lambda u: jnp.where(strict, _mm_nt(at[u], bt[u]), 0.0))
        a_ak = each(lambda u: jnp.where(strict, _mm_nt(at[u], kt[u]), 0.0))
        m_rb = each(lambda u: jnp.where(lower, _mm_nt(rt[u], bt[u]), 0.0))
        m_rk = each(lambda u: jnp.where(lower, _mm_nt(rt[u], kt[u]), 0.0))
        inv = each(lambda u: eye + a_ab[u])
        p = a_ab
        for _ in range(cl.bit_length() - 2):
            p = [_mm(p[u], p[u]) for u in grp]
            inv = [inv[u] + _mm(inv[u], p[u]) for u in grp]
        akv = each(lambda u: _mm(a_ak[u], v[u]))
        apz = each(lambda u: _mm(inv[u], jnp.concatenate([at[u], akv[u]], axis=-1)).astype(BF16))
        mix = each(lambda u: jnp.dot(m_rb[u].astype(BF16), apz[u], preferred_element_type=F32))
        bend = each(lambda u: bt[u] * w_end_rows[u])
        kend = each(lambda u: kt[u] * w_end_rows[u])
        rp = each(lambda u: (rt[u] + mix[u][:, :hd]).astype(BF16))
        y0 = each(lambda u: mix[u][:, hd:] + _mm(m_rk[u], v[u]))
        for u in grp:
            for h in range(RW_HEADS):
                hs = slice(h * cl, (h + 1) * cl)
                both = _mm_tn(apz[u][hs], bend[u][hs])
                rp_ref[h, rows[u], :] = rp[u][hs]
                y_ref[h, rows[u], :] = y0[u][hs]
                gm_ref[h, rows[u], :] = both[:hd].astype(BF16)
                h0_ref[h, rows[u], :] = both[hd:] + _mm_tn(v[u][hs], kend[u][hs])
                we_ref[h, chunks[u]] = w_end[u][h]

    def prepare_step(i, carry):
        prepare([i * RW_PREP_UNROLL + u for u in range(RW_PREP_UNROLL)])
        return carry

    lax.fori_loop(0, n_chunk // RW_PREP_UNROLL, prepare_step, 0)

    def advance(c, carry):
        rows = pl.ds(pl.multiple_of(c * cl, cl), cl)
        for h in range(RW_HEADS):
            s = s_ref[h]
            sb = s.astype(BF16)
            y_ref[h, rows, :] = y_ref[h, rows, :] + lax.dot_general(
                rp_ref[h, rows, :], sb, (((1,), (1,)), ((), ())), preferred_element_type=F32)
            s_ref[h] = (s * we_ref[h, c] + jnp.dot(sb, gm_ref[h, rows, :], preferred_element_type=F32)
                        + h0_ref[h, rows, :])
        return carry

    lax.fori_loop(0, n_chunk, advance, 0)

    for h in range(RW_HEADS):
        y = y_ref[h]
        mu = jnp.mean(y, axis=-1, keepdims=True)
        yc = y - mu
        var = jnp.mean(yc * yc, axis=-1, keepdims=True)
        yn = yc * lax.rsqrt(var + RW_GN_EPS) * gg_ref[h] + gb_ref[h]
        v = v_ref[0, h]
        bonus = jnp.sum(r_ref[0, h] * k_ref[0, h] * rk_ref[h], axis=-1, keepdims=True) * v
        o_ref[0, h] = ((yn + bonus) * g_ref[0, h]).astype(o_ref.dtype)


def _rwscan(r, lw, k, v, a, b, g, r_k, gn_gain, gn_bias):
    bsz, _, t_len, _ = r.shape
    tb = min(512, t_len)
    hm_spec = pl.BlockSpec((1, RW_HEADS, tb, RW_HEAD_DIM), lambda bi, i: (bi, 0, i, 0))
    par = pl.BlockSpec((RW_HEADS, 1, RW_HEAD_DIM), lambda bi, i: (0, 0, 0))
    hshape = (RW_HEADS, 1, RW_HEAD_DIM)
    return pl.pallas_call(
        _rwscan_kernel,
        out_shape=jax.ShapeDtypeStruct((bsz, RW_HEADS, t_len, RW_HEAD_DIM), BF16),
        grid=(bsz, t_len // tb),
        in_specs=[hm_spec] * 7 + [par] * 3,
        out_specs=hm_spec,
        scratch_shapes=[pltpu.VMEM((RW_HEADS, RW_HEAD_DIM, RW_HEAD_DIM), F32),
                        pltpu.VMEM((RW_HEADS, tb, RW_HEAD_DIM), BF16), pltpu.VMEM((RW_HEADS, tb, RW_HEAD_DIM), F32),
                        pltpu.VMEM((RW_HEADS, tb, RW_HEAD_DIM), BF16), pltpu.VMEM((RW_HEADS, tb, RW_HEAD_DIM), F32),
                        pltpu.VMEM((RW_HEADS, tb // RW_CHUNK, 1, RW_HEAD_DIM), F32)],
        compiler_params=_cparams(("parallel", "arbitrary")),
        name="rwscan",
    )(r, lw, k, v, a, b, g, r_k.reshape(hshape), gn_gain.reshape(hshape), gn_bias.reshape(hshape))


FOX_PAIRS = FOX_HEADS // 2
FOX_EXTRA = 3
FOX_ACC_ROWS = FOX_HEAD_DIM + 16
FOX_LOOKAHEAD = 2


def _fox_bias_selector():
    sel = np.zeros((LANES, 2 * FOX_HEADS * LANES), np.float32)
    for h in range(FOX_HEADS):
        base = FOX_HEAD_DIM if h % 2 == 0 else 0
        for p in range(FOX_EXTRA):
            sel[p * FOX_HEADS + h, h * LANES + base + p] = 1.0
            sel[p * FOX_HEADS + h, (FOX_HEADS + h) * LANES + base + FOX_EXTRA + p] = -1.0
    return sel


def _foxprep_kernel(z_ref, f_ref, fb_ref, qg_ref, kg_ref, sel_ref, q_o, k_o, vt_o, carry_ref):
    tm = z_ref.shape[1]

    @pl.when(pl.program_id(1) == 0)
    def _():
        carry_ref[...] = jnp.zeros_like(carry_ref)

    log_f = _log_sigmoid(f_ref[0] + fb_ref[...])
    cum = carry_ref[...] + _tri_cumsum(log_f, tm)
    carry_ref[...] = cum[tm - 1:tm, :]
    lane = lax.broadcasted_iota(I32, (tm, LANES), 1)
    hi, mid, lo = (p.astype(F32) for p in _split3(cum * LOG2E))
    packed = jnp.where(lane < FOX_HEADS, hi,
                       jnp.where(lane < 2 * FOX_HEADS, pltpu.roll(mid, FOX_HEADS, axis=1),
                                 pltpu.roll(lo, 2 * FOX_HEADS, axis=1)))
    packed = jnp.where(lane < FOX_EXTRA * FOX_HEADS, packed, 0.0).astype(BF16)
    extra = jnp.dot(packed, sel_ref[...], preferred_element_type=F32)

    left = lane < FOX_HEAD_DIM
    in_half = lane % FOX_HEAD_DIM
    ones_q = jnp.where((in_half >= FOX_EXTRA) & (in_half < 2 * FOX_EXTRA), 1.0, 0.0)
    ones_k = jnp.where(in_half < FOX_EXTRA, 1.0, 0.0)

    def normed(block, gain):
        sq = block * block
        s_left = jnp.sum(jnp.where(left, sq, 0.0), axis=-1, keepdims=True)
        s_right = jnp.sum(jnp.where(left, 0.0, sq), axis=-1, keepdims=True)
        ms = jnp.where(left, s_left, s_right) * (1.0 / FOX_HEAD_DIM)
        return block * lax.rsqrt(ms + EPS) * gain

    for j in range(FOX_PAIRS):
        qn = normed(z_ref[0, :, j * LANES:(j + 1) * LANES].astype(F32), qg_ref[...] * (ATTN_SCALE * LOG2E))
        kn = normed(z_ref[0, :, FOX_WIDTH + j * LANES:FOX_WIDTH + (j + 1) * LANES].astype(F32), kg_ref[...])
        for par in range(2):
            h = 2 * j + par
            own = left if par == 0 else jnp.logical_not(left)
            q_o[0, h] = jnp.where(own, qn, extra[:, h * LANES:(h + 1) * LANES] + ones_q).astype(BF16)
            k_o[0, h] = jnp.where(own, kn, extra[:, (FOX_HEADS + h) * LANES:(FOX_HEADS + h + 1) * LANES]
                                  + ones_k).astype(BF16)
    ri = lax.broadcasted_iota(I32, (FOX_WIDTH, FOX_WIDTH), 0)
    ci = lax.broadcasted_iota(I32, (FOX_WIDTH, FOX_WIDTH), 1)
    eye = jnp.where(ri == ci, 1.0, 0.0).astype(BF16)
    v = z_ref[0, :, 2 * FOX_WIDTH:3 * FOX_WIDTH].astype(BF16)
    vt_o[0] = lax.dot_general(eye, v, (((1,), (1,)), ((), ())), preferred_element_type=F32).astype(BF16)


def _foxprep(z, z_rw, f_bias_pad, q_gain, k_gain):
    bsz, t_len, _ = z.shape
    tm = min(512, t_len)
    qk = jax.ShapeDtypeStruct((bsz, FOX_HEADS, t_len, LANES), BF16)
    qk_spec = pl.BlockSpec((1, FOX_HEADS, tm, LANES), lambda b, i: (b, 0, i, 0))
    sel = jnp.asarray(_fox_bias_selector(), BF16)
    return pl.pallas_call(
        _foxprep_kernel,
        out_shape=(qk, qk, jax.ShapeDtypeStruct((bsz, FOX_WIDTH, t_len), BF16)),
        grid=(bsz, t_len // tm),
        in_specs=[pl.BlockSpec((1, tm, 3 * FOX_WIDTH), lambda b, i: (b, i, Z_FOX // (3 * FOX_WIDTH))),
                  pl.BlockSpec((1, tm, LANES), lambda b, i: (b, i, (Z_F - Z_RW) // LANES)),
                  pl.BlockSpec((1, LANES), lambda b, i: (0, 0)),
                  pl.BlockSpec((1, LANES), lambda b, i: (0, 0)),
                  pl.BlockSpec((1, LANES), lambda b, i: (0, 0)),
                  pl.BlockSpec(sel.shape, lambda b, i: (0, 0))],
        out_specs=(qk_spec, qk_spec, pl.BlockSpec((1, FOX_WIDTH, tm), lambda b, i: (b, 0, i))),
        scratch_shapes=[pltpu.VMEM((1, LANES), F32)],
        compiler_params=_cparams(("parallel", "arbitrary")),
        name="foxprep",
    )(z, z_rw, f_bias_pad, jnp.tile(q_gain.reshape(1, -1), (1, 2)), jnp.tile(k_gain.reshape(1, -1), (1, 2)), sel)


def _fox_kernel(qi_ref, kj_ref, q_ref, k_ref, vt_ref, o_ref, m_ref, acc_ref):
    i = qi_ref[pl.program_id(1)]
    j = kj_ref[pl.program_id(1)]
    tq = q_ref.shape[2]
    tk = k_ref.shape[2]
    sub = 8

    @pl.when(j == 0)
    def _():
        m_ref[...] = jnp.full_like(m_ref, MASK_VALUE)
        acc_ref[...] = jnp.zeros_like(acc_ref)

    ones_rows = jnp.ones((FOX_ACC_ROWS - FOX_HEAD_DIM, tk), BF16)

    def scores(h):
        return lax.dot_general(k_ref[0, h], q_ref[0, h], (((1,), (1,)), ((), ())), preferred_element_type=F32)

    def update(diagonal):
        if diagonal:
            key = lax.broadcasted_iota(I32, (tk, tq), 0)
            qry = lax.broadcasted_iota(I32, (tk, tq), 1)
            keep = key <= qry
        ahead = [scores(h) for h in range(FOX_LOOKAHEAD)]
        for h in range(FOX_HEADS):
            s = ahead.pop(0)
            if h + FOX_LOOKAHEAD < FOX_HEADS:
                ahead.append(scores(h + FOX_LOOKAHEAD))
            if diagonal:
                s = jnp.where(keep, s, MASK_VALUE)
            s3 = s.reshape(tk // sub, sub, tq)
            m_prev = m_ref[h]
            m_cur = jnp.max(jnp.max(s3, axis=0), axis=0, keepdims=True)
            m_new = jnp.maximum(m_prev, m_cur)
            alpha = jnp.exp2(m_prev - m_new)
            p = jnp.exp2(s3 - m_new[None]).astype(BF16).reshape(tk, tq)
            lhs = jnp.concatenate([vt_ref[0, h * FOX_HEAD_DIM:(h + 1) * FOX_HEAD_DIM, :], ones_rows], axis=0)
            pv = jnp.dot(lhs, p, preferred_element_type=F32)
            acc = acc_ref[h].reshape(FOX_ACC_ROWS // sub, sub, tq) * alpha[None]
            acc_ref[h] = acc.reshape(FOX_ACC_ROWS, tq) + pv
            m_ref[h] = m_new

    @pl.when(j < i)
    def _():
        update(False)

    @pl.when(j == i)
    def _():
        update(True)
        outs = []
        for h in range(FOX_HEADS):
            acc = acc_ref[h]
            outs.append((acc[:FOX_HEAD_DIM] / acc[FOX_HEAD_DIM:FOX_HEAD_DIM + 1]).astype(BF16))
        out_t = jnp.concatenate(outs, axis=0)
        ri = lax.broadcasted_iota(I32, (tq, tq), 0)
        ci = lax.broadcasted_iota(I32, (tq, tq), 1)
        eye = jnp.where(ri == ci, 1.0, 0.0).astype(BF16)
        o_ref[0] = lax.dot_general(eye, out_t, (((1,), (1,)), ((), ())),
                                   preferred_element_type=F32).astype(o_ref.dtype)


def _fox(q, k, vt):
    bsz, _, t_len, _ = q.shape
    tq = min(512, t_len)
    n_blk = t_len // tq
    pairs = [(i, j) for i in range(n_blk) for j in range(i + 1)]
    qi = jnp.asarray([p[0] for p in pairs], I32)
    kj = jnp.asarray([p[1] for p in pairs], I32)
    grid_spec = pltpu.PrefetchScalarGridSpec(
        num_scalar_prefetch=2,
        grid=(bsz, len(pairs)),
        in_specs=[pl.BlockSpec((1, FOX_HEADS, tq, LANES), lambda b, s, qi, kj: (b, 0, qi[s], 0)),
                  pl.BlockSpec((1, FOX_HEADS, tq, LANES), lambda b, s, qi, kj: (b, 0, kj[s], 0)),
                  pl.BlockSpec((1, FOX_WIDTH, tq), lambda b, s, qi, kj: (b, 0, kj[s]))],
        out_specs=pl.BlockSpec((1, tq, FOX_WIDTH), lambda b, s, qi, kj: (b, qi[s], 0)),
        scratch_shapes=[pltpu.VMEM((FOX_HEADS, 8, tq), F32), pltpu.VMEM((FOX_HEADS, FOX_ACC_ROWS, tq), F32)],
    )
    return pl.pallas_call(
        _fox_kernel,
        out_shape=jax.ShapeDtypeStruct((bsz, t_len, FOX_WIDTH), BF16),
        grid_spec=grid_spec,
        compiler_params=_cparams(("parallel", "arbitrary")),
        name="fox",
    )(qi, kj, q, k, vt)


def _merge_kernel(zg_ref, ygm_ref, yrw_ref, yfox_ref, x_ref, g1_ref, sc2_ref, sh2_ref, pb_ref, wo_ref, wr_ref, br_ref,
                  x1_o, h2_o, idx_o, gate_o, rank_o, cnt_o, carry_ref):
    tm = x_ref.shape[1]

    @pl.when((pl.program_id(0) == 0) & (pl.program_id(1) == 0))
    def _():
        carry_ref[...] = jnp.zeros_like(carry_ref)

    sg = 0.5 * jnp.tanh(0.5 * zg_ref[0].astype(F32)) + 0.5
    p_gm = jnp.dot(ygm_ref[0], pb_ref[0:GM_WIDTH, :], preferred_element_type=F32)
    y_rw = jnp.concatenate([yrw_ref[0, h] for h in range(RW_HEADS)], axis=-1)
    p_rw = jnp.dot(y_rw, pb_ref[GM_WIDTH:GM_WIDTH + RW_WIDTH, :], preferred_element_type=F32)
    p_fox = jnp.dot(yfox_ref[0], pb_ref[GM_WIDTH + RW_WIDTH:, :], preferred_element_type=F32)
    merged = sg[:, 0:D_MODEL] * p_gm + sg[:, D_MODEL:2 * D_MODEL] * p_rw + sg[:, 2 * D_MODEL:] * p_fox
    x1 = x_ref[0] + g1_ref[0] * jnp.dot(merged.astype(BF16), wo_ref[...], preferred_element_type=F32)
    x1_o[0] = x1
    h2 = x1 * lax.rsqrt(jnp.mean(x1 * x1, axis=-1, keepdims=True) + EPS) * (1.0 + sc2_ref[0]) + sh2_ref[0]
    h2_o[0] = _pack_halves(h2)

    h_hi, h_lo, _ = _split3(h2)
    w_hi, w_lo, _ = _split3(wr_ref[...])
    logits = (jnp.dot(h_hi, w_hi, preferred_element_type=F32) + jnp.dot(h_hi, w_lo, preferred_element_type=F32)
              + jnp.dot(h_lo, w_hi, preferred_element_type=F32)) + br_ref[...]
    lane = lax.broadcasted_iota(I32, (tm, N_EXPERTS), 1)
    vals, idxs = [], []
    rest = logits
    for _ in range(TOP_K):
        m = jnp.max(rest, axis=-1, keepdims=True)
        am = jnp.min(jnp.where(rest == m, lane, N_EXPERTS), axis=-1, keepdims=True)
        vals.append(m)
        idxs.append(am)
        rest = jnp.where(lane == am, -jnp.inf, rest)
    exps = [jnp.exp(val - vals[0]) for val in vals]
    denom = exps[0] + exps[1] + exps[2] + exps[3]

    onehot = jnp.zeros((tm, N_EXPERTS), F32)
    for am in idxs:
        onehot = onehot + jnp.where(lane == am, 1.0, 0.0)
    ri = lax.broadcasted_iota(I32, (tm, tm), 0)
    ci = lax.broadcasted_iota(I32, (tm, tm), 1)
    before = jnp.where(ri > ci, 1.0, 0.0).astype(BF16)
    seen = carry_ref[...] + jnp.dot(before, onehot.astype(BF16), preferred_element_type=F32)
    lane_k = lax.broadcasted_iota(I32, (tm, TOP_K), 1)
    idx_out = jnp.zeros((tm, TOP_K), I32)
    gate_out = jnp.zeros((tm, TOP_K), F32)
    rank_out = jnp.zeros((tm, TOP_K), I32)
    for kk in range(TOP_K):
        rank = jnp.sum(jnp.where(lane == idxs[kk], seen, 0.0), axis=-1, keepdims=True).astype(I32)
        idx_out = jnp.where(lane_k == kk, idxs[kk], idx_out)
        gate_out = jnp.where(lane_k == kk, exps[kk] / denom, gate_out)
        rank_out = jnp.where(lane_k == kk, rank, rank_out)
    idx_o[0] = idx_out
    gate_o[0] = gate_out
    rank_o[0] = rank_out
    total = carry_ref[...] + jnp.sum(onehot, axis=0, keepdims=True)
    carry_ref[...] = total
    cnt_o[...] = total.astype(I32)


def _merge(z, y_gm, y_rw, y_fox, x, gate1, scale2, shift2, w_branch, w_o, w_router, b_router, layer):
    bsz, t_len, d = x.shape
    tm = min(512, t_len)
    row = lambda w: pl.BlockSpec((1, tm, w), lambda b, i: (b, i, 0))
    mod = pl.BlockSpec((1, 1, d), lambda b, i: (b, 0, 0))
    full = lambda shape: pl.BlockSpec(shape, lambda b, i: (0,) * len(shape))
    return pl.pallas_call(
        _merge_kernel,
        out_shape=(jax.ShapeDtypeStruct((bsz, t_len, d), F32), jax.ShapeDtypeStruct((bsz, t_len, d // 2), I32),
                   jax.ShapeDtypeStruct((bsz, t_len, TOP_K), I32), jax.ShapeDtypeStruct((bsz, t_len, TOP_K), F32),
                   jax.ShapeDtypeStruct((bsz, t_len, TOP_K), I32), jax.ShapeDtypeStruct((1, N_EXPERTS), I32)),
        grid=(bsz, t_len // tm),
        in_specs=[row(N_BRANCH * D_MODEL), row(GM_WIDTH),
                  pl.BlockSpec((1, RW_HEADS, tm, RW_HEAD_DIM), lambda b, i: (b, 0, i, 0)),
                  row(FOX_WIDTH), row(d), mod, mod, mod,
                  pl.BlockSpec((MIX_WIDTH, d), lambda b, i: (layer, 0)), pl.BlockSpec((d, d), lambda b, i: (layer, 0)),
                  full(w_router.shape), full((1, N_EXPERTS))],
        out_specs=(row(d), row(d // 2), row(TOP_K), row(TOP_K), row(TOP_K), full((1, N_EXPERTS))),
        scratch_shapes=[pltpu.VMEM((1, N_EXPERTS), F32)],
        compiler_params=_cparams(("arbitrary", "arbitrary")),
        name="merge_router",
    )(z, y_gm, y_rw, y_fox, x, gate1, scale2, shift2, w_branch, w_o, w_router, b_router.reshape(1, N_EXPERTS))


def _sc_mesh():
    return plsc.VectorSubcoreMesh(core_axis_name="c", subcore_axis_name="s",
                                  num_cores=SC_CORES, num_subcores=SC_SUBCORES)


def _sc_worker():
    return lax.axis_index("s") * SC_CORES + lax.axis_index("c")


def _sc_scatter_rows(src, idx3, n_out):
    _, d = src.shape
    n_copy, n_grp, _ = idx3.shape
    grp_per_w = n_grp // SC_WORKERS
    assert grp_per_w % 2 == 0

    def body(src_hbm, idx_hbm, out_hbm, idx_v, rows_a, rows_b, sem):
        g0 = _sc_worker() * grp_per_w
        for q in range(n_copy):
            pltpu.sync_copy(idx_hbm.at[q, pl.ds(g0, grp_per_w)], idx_v.at[pl.ds(q * grp_per_w, grp_per_w)])

        @pl.loop(0, grp_per_w, step=2)
        def _(j):
            read_a = pltpu.async_copy(src_hbm.at[pl.ds((g0 + j) * SC_ROWS, SC_ROWS)], rows_a, sem.at[0])
            read_b = pltpu.async_copy(src_hbm.at[pl.ds((g0 + j + 1) * SC_ROWS, SC_ROWS)], rows_b, sem.at[1])
            read_a.wait()
            put_a = [pltpu.async_copy(rows_a, out_hbm.at[idx_v.at[q * grp_per_w + j]], sem.at[2])
                     for q in range(n_copy)]
            read_b.wait()
            put_b = [pltpu.async_copy(rows_b, out_hbm.at[idx_v.at[q * grp_per_w + j + 1]], sem.at[3])
                     for q in range(n_copy)]
            for cp in put_a + put_b:
                cp.wait()

    return pl.kernel(
        body, out_type=jax.ShapeDtypeStruct((n_out, d), src.dtype), mesh=_sc_mesh(),
        scratch_types=[pltpu.VMEM((n_copy * grp_per_w, SC_ROWS), I32), pltpu.VMEM((SC_ROWS, d), src.dtype),
                       pltpu.VMEM((SC_ROWS, d), src.dtype), pltpu.SemaphoreType.DMA((4,))],
        name="sc_dispatch",
    )(src, idx3)


def _sc_gather_rows(table, idx2):
    _, d = table.shape
    n_grp, _ = idx2.shape
    grp_per_w = n_grp // SC_WORKERS
    assert grp_per_w % 2 == 0

    def body(table_hbm, idx_hbm, out_hbm, idx_v, rows_a, rows_b, sem):
        g0 = _sc_worker() * grp_per_w
        pltpu.sync_copy(idx_hbm.at[pl.ds(g0, grp_per_w)], idx_v)

        @pl.loop(0, grp_per_w, step=2)
        def _(j):
            get_a = pltpu.async_copy(table_hbm.at[idx_v.at[j]], rows_a, sem.at[0])
            get_b = pltpu.async_copy(table_hbm.at[idx_v.at[j + 1]], rows_b, sem.at[1])
            get_a.wait()
            put_a = pltpu.async_copy(rows_a, out_hbm.at[pl.ds((g0 + j) * SC_ROWS, SC_ROWS)], sem.at[2])
            get_b.wait()
            put_b = pltpu.async_copy(rows_b, out_hbm.at[pl.ds((g0 + j + 1) * SC_ROWS, SC_ROWS)], sem.at[3])
            put_a.wait()
            put_b.wait()

    return pl.kernel(
        body, out_type=jax.ShapeDtypeStruct((n_grp * SC_ROWS, d), table.dtype), mesh=_sc_mesh(),
        scratch_types=[pltpu.VMEM((grp_per_w, SC_ROWS), I32), pltpu.VMEM((SC_ROWS, d), table.dtype),
                       pltpu.VMEM((SC_ROWS, d), table.dtype), pltpu.SemaphoreType.DMA((4,))],
        name="sc_combine_gather",
    )(table, idx2)


def _ffn_weight_copies(expert, wgu_hbm, wd_hbm, stage_gu, stage_d, sem):
    return (pltpu.make_async_copy(wgu_hbm.at[expert], stage_gu, sem.at[0]),
            pltpu.make_async_copy(wd_hbm.at[expert], stage_d, sem.at[1]))


def _ffn_kernel(be_ref, first_ref, nxt_ref, nv_ref, x_ref, wgu_hbm, wd_hbm, bgu_ref, bd_ref, o_ref,
                stage_gu, stage_d, wgu_b, wd_b, sem):
    step = pl.program_id(0)
    copies = functools.partial(_ffn_weight_copies, wgu_hbm=wgu_hbm, wd_hbm=wd_hbm, stage_gu=stage_gu,
                               stage_d=stage_d, sem=sem)

    @pl.when(step == 0)
    def _():
        for cp in copies(be_ref[0]):
            cp.start()

    def switch_weights(idx):
        @pl.when(first_ref[idx] == 1)
        def _():
            for cp in copies(be_ref[idx]):
                cp.wait()
            wgu_b[...] = stage_gu[...].astype(BF16)
            wd_b[...] = stage_d[...].astype(BF16)

            @pl.when(nxt_ref[idx] >= 0)
            def _():
                for cp in copies(nxt_ref[idx]):
                    cp.start()

    def compute(row0, n_rows, n_valid, expert):
        rows = slice(row0, row0 + n_rows)
        rowid = lax.broadcasted_iota(I32, (n_rows, x_ref.shape[1]), 0)
        xp = jnp.where(rowid < n_valid, x_ref[rows, :], 0)
        x = jnp.concatenate(_unpack_halves(xp), axis=-1).astype(BF16)
        gu = jnp.dot(x, wgu_b[...], preferred_element_type=F32) + bgu_ref[expert]
        g_ = jnp.minimum(gu[:, :D_FF], SWIGLU_LIMIT)
        u_ = jnp.clip(gu[:, D_FF:], -SWIGLU_LIMIT, SWIGLU_LIMIT)
        act = (u_ + 1.0) * (g_ * jax.nn.sigmoid(SWIGLU_ALPHA * g_))
        y = jnp.dot(act.astype(BF16), wd_b[...], preferred_element_type=F32) + bd_ref[expert]
        o_ref[rows, :] = _pack_halves(y)

    def single(idx, row0):
        switch_weights(idx)

        @pl.when(nv_ref[idx] > 0)
        def _():
            compute(row0, MOE_BLOCK, nv_ref[idx], be_ref[idx])

        @pl.when(nv_ref[idx] <= 0)
        def _():
            o_ref[row0:row0 + MOE_BLOCK, :] = jnp.zeros((MOE_BLOCK, o_ref.shape[1]), o_ref.dtype)

    for b in range(FFN_STEP_BLOCKS):
        single(step * FFN_STEP_BLOCKS + b, b * MOE_BLOCK)


def _ffn(block_expert, block_first, block_next, block_valid, xin, w_gate_up, b_gate_up, w_down, b_down):
    n_rows, dp = xin.shape
    d = 2 * dp
    step_rows = FFN_STEP_BLOCKS * MOE_BLOCK
    resident = lambda arr: pl.BlockSpec(arr.shape, lambda i, *_: (0,) * arr.ndim)
    grid_spec = pltpu.PrefetchScalarGridSpec(
        num_scalar_prefetch=4,
        grid=(n_rows // step_rows,),
        in_specs=[pl.BlockSpec((step_rows, dp), lambda i, *_: (i, 0)),
                  pl.BlockSpec(memory_space=pl.ANY), pl.BlockSpec(memory_space=pl.ANY),
                  resident(b_gate_up), resident(b_down)],
        out_specs=pl.BlockSpec((step_rows, dp), lambda i, *_: (i, 0)),
        scratch_shapes=[pltpu.VMEM((d, 2 * D_FF), F32), pltpu.VMEM((D_FF, d), F32),
                        pltpu.VMEM((d, 2 * D_FF), BF16), pltpu.VMEM((D_FF, d), BF16),
                        pltpu.SemaphoreType.DMA((2,))],
    )
    return pl.pallas_call(
        _ffn_kernel,
        out_shape=jax.ShapeDtypeStruct((n_rows, dp), I32),
        grid_spec=grid_spec,
        compiler_params=pltpu.CompilerParams(dimension_semantics=("arbitrary",), vmem_limit_bytes=FFN_VMEM_LIMIT),
        name="expert_ffn",
    )(block_expert, block_first, block_next, block_valid, xin, w_gate_up, w_down, b_gate_up, b_down)


def _combine_kernel(x1_ref, g2_ref, gate_ref, yg_ref, o_ref):
    gate = gate_ref[0]
    y_lo = y_hi = None
    for q in range(TOP_K):
        lo, hi = _unpack_halves(yg_ref[q, 0])
        wq = gate[:, q:q + 1]
        y_lo = wq * lo if y_lo is None else y_lo + wq * lo
        y_hi = wq * hi if y_hi is None else y_hi + wq * hi
    o_ref[0] = x1_ref[0] + g2_ref[0] * jnp.concatenate([y_lo, y_hi], axis=-1)


def _combine(x1, gate2, gate, yg):
    bsz, t_len, d = x1.shape
    tm = min(512, t_len)
    return pl.pallas_call(
        _combine_kernel,
        out_shape=jax.ShapeDtypeStruct((bsz, t_len, d), F32),
        grid=(bsz, t_len // tm),
        in_specs=[pl.BlockSpec((1, tm, d), lambda b, i: (b, i, 0)),
                  pl.BlockSpec((1, 1, d), lambda b, i: (b, 0, 0)),
                  pl.BlockSpec((1, tm, TOP_K), lambda b, i: (b, i, 0)),
                  pl.BlockSpec((TOP_K, 1, tm, d // 2), lambda b, i: (0, b, i, 0))],
        out_specs=pl.BlockSpec((1, tm, d), lambda b, i: (b, i, 0)),
        compiler_params=_cparams(("parallel", "parallel")),
        name="moe_combine",
    )(x1, gate2, gate, yg)


def _moe(x1, gate2, h2, top_idx, gate, rank, counts, w_gate_up, b_gate_up, w_down, b_down, layer):
    bsz, t_len, d = h2.shape
    n_tok = bsz * t_len
    n_assign = n_tok * TOP_K
    n_blocks = -(-n_assign // MOE_BLOCK) + N_EXPERTS
    counts = counts.reshape(N_EXPERTS)
    blocks_e = (counts + MOE_BLOCK - 1) // MOE_BLOCK
    blk_end = jnp.cumsum(blocks_e)
    blk_start = blk_end - blocks_e
    experts = jnp.arange(N_EXPERTS, dtype=I32)
    onehot = top_idx.reshape(n_tok, TOP_K, 1) == experts
    dest = jnp.sum(jnp.where(onehot, blk_start * MOE_BLOCK, 0), axis=-1) + rank.reshape(n_tok, TOP_K)
    dest_t = dest.T.astype(I32)
    blk = jnp.arange(n_blocks, dtype=I32)
    block_expert = jnp.minimum(jnp.sum(blk_end[None, :] <= blk[:, None], axis=1), N_EXPERTS - 1).astype(I32)
    be_hot = block_expert[:, None] == experts
    cnt_b = jnp.sum(jnp.where(be_hot, counts, 0), axis=1)
    start_b = jnp.sum(jnp.where(be_hot, blk_start, 0), axis=1)
    block_valid = jnp.clip(cnt_b - (blk - start_b) * MOE_BLOCK, 0, MOE_BLOCK).astype(I32)
    xin = _sc_scatter_rows(h2.reshape(n_tok, d), dest_t.reshape(TOP_K, n_tok // SC_ROWS, SC_ROWS),
                           n_blocks * MOE_BLOCK)
    block_first = jnp.concatenate([jnp.ones((1,), I32), (block_expert[1:] != block_expert[:-1]).astype(I32)])
    run_start = jnp.where(block_first == 1, blk, n_blocks)
    later_start = lax.cummin(jnp.concatenate([run_start[1:], jnp.full((1,), n_blocks, I32)]), reverse=True)
    next_expert = jnp.concatenate([block_expert, jnp.full((1,), -1 - layer * N_EXPERTS, I32)])[later_start]
    yb = _ffn(block_expert + layer * N_EXPERTS, block_first, next_expert + layer * N_EXPERTS, block_valid, xin,
              w_gate_up, b_gate_up, w_down, b_down)
    yg = _sc_gather_rows(yb, dest_t.reshape(n_assign // SC_ROWS, SC_ROWS))
    return _combine(x1, gate2, gate, yg.reshape(TOP_K, bsz, t_len, d))


def _permute_kernel(w_ref, o_ref):
    o_gm = 0
    o_rw = o_gm + 2 * GM_WIDTH
    o_fox = o_rw + RW_SHIFT_WIDTH
    o_f = o_fox + 3 * FOX_WIDTH
    o_gate = o_f + FOX_HEADS
    w = w_ref[0]
    o_ref[0, :, Z_GATE:Z_FOX] = w[:, o_gate:o_gate + N_BRANCH * D_MODEL].astype(BF16)
    o_ref[0, :, Z_FOX:Z_GM] = w[:, o_fox:o_f].astype(BF16)
    o_ref[0, :, Z_GM:Z_RW] = w[:, o_gm:o_rw].astype(BF16)
    o_ref[0, :, Z_RW:Z_F] = w[:, o_rw:o_fox].astype(BF16)
    tail = jnp.concatenate([w[:, o_f:o_gate], jnp.zeros((w.shape[0], Z_WIDTH - Z_F - FOX_HEADS), F32)], axis=-1)
    o_ref[0, :, Z_F:Z_WIDTH] = tail.astype(BF16)


def _permute_w_in(w_in):
    n_layer, d, w_cols = w_in.shape
    tr = 256
    return pl.pallas_call(
        _permute_kernel,
        out_shape=jax.ShapeDtypeStruct((n_layer, d, Z_WIDTH), BF16),
        grid=(n_layer, d // tr),
        in_specs=[pl.BlockSpec((1, tr, w_cols), lambda l, i: (l, i, 0))],
        out_specs=pl.BlockSpec((1, tr, Z_WIDTH), lambda l, i: (l, i, 0)),
        compiler_params=_cparams(("parallel", "parallel")),
        name="permute_w_in",
    )(w_in)


def _layer(x, mod, w_in_p, gm_v_gain, gm_w_s, gm_b_s, mu_pad, w_lora, rw_w0, rw_a0, rw_k_k, rw_k_a, rw_r_k,
           rw_gn_gain, rw_gn_bias, f_bias_pad, fox_q_gain, fox_k_gain, w_branch, w_o, w_router, b_router,
           w_gate_up, b_gate_up, w_down, b_down, layer):
    shift1, scale1, gate1, shift2, scale2, gate2 = (mod[:, i][:, None, :] for i in range(6))
    z, z_rw = _inproj(x, scale1, shift1, w_in_p, layer)
    y_gm = _gmlp(z, gm_v_gain, gm_w_s, gm_b_s)
    r, lw, k, v, a, b, g = _rwprep(z_rw, mu_pad, w_lora, rw_w0, rw_a0, rw_k_k, rw_k_a)
    y_rw = _rwscan(r, lw, k, v, a, b, g, rw_r_k, rw_gn_gain, rw_gn_bias)
    q, kf, vf = _foxprep(z, z_rw, f_bias_pad, fox_q_gain, fox_k_gain)
    y_fox = _fox(q, kf, vf)
    x1, h2, top_idx, gate, rank, counts = _merge(z, y_gm, y_rw, y_fox, x, gate1, scale2, shift2,
                                                 w_branch, w_o, w_router, b_router, layer)
    return _moe(x1, gate2, h2, top_idx, gate, rank, counts, w_gate_up, b_gate_up, w_down, b_down, layer)


def kernel(x, c, w_ada, b_ada, w_in, gm_v_gain, gm_w_s, gm_b_s, rw_mu, rw_w0, rw_w2, rw_a0, rw_a2, rw_g2, rw_k_k,
           rw_k_a, rw_r_k, rw_gn_gain, rw_gn_bias, fox_f_bias, fox_q_gain, fox_k_gain, w_branch, w_o, w_router,
           b_router, w_gate_up, b_gate_up, w_down, b_down):
    n_layer = w_ada.shape[0]
    bsz = x.shape[0]
    c_pad = jnp.zeros((8, D_MODEL), F32).at[:bsz].set(c)
    mod = _adaln(c_pad, w_ada, b_ada)[:, :bsz].reshape(n_layer, bsz, 6, D_MODEL)
    w_in_p = _permute_w_in(w_in)
    mu_pad = jnp.pad(rw_mu, ((0, 0), (0, RW_BLOCK - RW_SHIFT_WIDTH)))
    w_lora = jnp.zeros((n_layer, RW_LORA, 3 * RW_WIDTH), F32)
    w_lora = w_lora.at[:, 0:RW_DECAY_LORA, 0:RW_WIDTH].set(rw_w2)
    w_lora = w_lora.at[:, RW_DECAY_LORA:RW_DECAY_LORA + RW_ICLR_LORA, RW_WIDTH:2 * RW_WIDTH].set(rw_a2)
    w_lora = w_lora.at[:, RW_DECAY_LORA + RW_ICLR_LORA:, 2 * RW_WIDTH:].set(rw_g2)
    f_bias_pad = jnp.pad(fox_f_bias, ((0, 0), (0, LANES - FOX_HEADS)))
    w_in_p = w_in_p.reshape(n_layer * D_MODEL, Z_WIDTH)
    w_branch_b = w_branch.astype(BF16).reshape(n_layer * MIX_WIDTH, D_MODEL)
    w_o_b = w_o.astype(BF16).reshape(n_layer * D_MODEL, D_MODEL)
    w_gu = w_gate_up.reshape(n_layer * N_EXPERTS, D_MODEL, 2 * D_FF)
    b_gu = b_gate_up.reshape(n_layer * N_EXPERTS, 1, 2 * D_FF)
    w_dn = w_down.reshape(n_layer * N_EXPERTS, D_FF, D_MODEL)
    b_dn = b_down.reshape(n_layer * N_EXPERTS, 1, D_MODEL)
    for l in range(n_layer):
        x = _layer(x, mod[l], w_in_p, gm_v_gain[l], gm_w_s[l], gm_b_s[l], mu_pad[l:l + 1], w_lora[l], rw_w0[l],
                   rw_a0[l], rw_k_k[l], rw_k_a[l], rw_r_k[l], rw_gn_gain[l], rw_gn_bias[l], f_bias_pad[l:l + 1],
                   fox_q_gain[l], fox_k_gain[l], w_branch_b, w_o_b, w_router[l], b_router[l],
                   w_gu, b_gu, w_dn, b_dn, l)
    return x
```

```python
import functools

import jax
import jax.numpy as jnp
import numpy as np
from jax import lax
from jax.experimental import pallas as pl
from jax.experimental.pallas import tpu as pltpu
from jax.experimental.pallas import tpu_sc as plsc

F32 = jnp.float32
BF16 = jnp.bfloat16
I32 = jnp.int32
HIGHEST = lax.Precision.HIGHEST

D_MODEL = 1024
GM_CHUNK = 128
GM_GROUPS = 4
GM_WIDTH = 256
GM_GROUP_DIM = GM_WIDTH // GM_GROUPS
RW_HEADS = 4
RW_HEAD_DIM = 64
RW_WIDTH = RW_HEADS * RW_HEAD_DIM
RW_DECAY_LORA = 32
RW_ICLR_LORA = 32
RW_GATE_LORA = 64
RW_LORA = RW_DECAY_LORA + RW_ICLR_LORA + RW_GATE_LORA
RW_SHIFT_WIDTH = 3 * RW_WIDTH + RW_LORA
RW_GN_EPS = 64e-5
FOX_HEADS = 8
FOX_HEAD_DIM = 64
FOX_WIDTH = FOX_HEADS * FOX_HEAD_DIM
ATTN_SCALE = FOX_HEAD_DIM ** -0.5
MASK_VALUE = -1e30
LOG2E = 1.4426950408889634
N_BRANCH = 3
MIX_WIDTH = GM_WIDTH + RW_WIDTH + FOX_WIDTH
N_EXPERTS = 32
TOP_K = 4
D_FF = D_MODEL
SWIGLU_LIMIT = 7.0
SWIGLU_ALPHA = 1.702
MOE_BLOCK = 256
EPS = 1e-6

Z_GATE = 0
Z_FOX = N_BRANCH * D_MODEL
Z_GM = Z_FOX + 3 * FOX_WIDTH
Z_RW = Z_GM + 2 * GM_WIDTH
RW_BLOCK = 1024
Z_F = Z_RW + RW_SHIFT_WIDTH
Z_WIDTH = Z_RW + RW_BLOCK
LANES = 128
RW_CHUNK = 64
RW_PREP_UNROLL = 4

VMEM_LIMIT = 48 * 1024 * 1024
FFN_VMEM_LIMIT = 56 * 1024 * 1024
FFN_STEP_BLOCKS = 4
SC_CORES = 2
SC_SUBCORES = 16
SC_WORKERS = SC_CORES * SC_SUBCORES
SC_ROWS = 64


def _cparams(sem):
    return pltpu.CompilerParams(dimension_semantics=sem, vmem_limit_bytes=VMEM_LIMIT)


def _mm(a, b):
    return jnp.dot(a.astype(BF16), b.astype(BF16), preferred_element_type=F32)


def _mm_nt(a, b):
    return lax.dot_general(a.astype(BF16), b.astype(BF16), (((1,), (1,)), ((), ())), preferred_element_type=F32)


def _mm_tn(a, b):
    return lax.dot_general(a.astype(BF16), b.astype(BF16), (((0,), (0,)), ((), ())), preferred_element_type=F32)


def _split3(x):
    hi = x.astype(BF16)
    r1 = x - hi.astype(F32)
    mid = r1.astype(BF16)
    lo = (r1 - mid.astype(F32)).astype(BF16)
    return hi, mid, lo


def _tri_cumsum(x, n):
    ri = lax.broadcasted_iota(I32, (n, n), 0)
    ci = lax.broadcasted_iota(I32, (n, n), 1)
    ones = jnp.where(ri >= ci, 1.0, 0.0).astype(BF16)
    hi, mid, lo = _split3(x)
    return (jnp.dot(ones, hi, preferred_element_type=F32) + jnp.dot(ones, mid, preferred_element_type=F32)
            + jnp.dot(ones, lo, preferred_element_type=F32))


def _pack_halves(x):
    w = x.shape[1] // 2
    hi = pltpu.bitcast(x[:, :w].astype(BF16).astype(F32), jnp.uint32)
    lo = pltpu.bitcast(x[:, w:].astype(BF16).astype(F32), jnp.uint32)
    return pltpu.bitcast(hi | (lo >> 16), I32)


def _unpack_halves(p):
    u = pltpu.bitcast(p, jnp.uint32)
    return pltpu.bitcast(u & jnp.uint32(0xFFFF0000), F32), pltpu.bitcast(u << 16, F32)


def _log_sigmoid(x):
    return jnp.minimum(x, 0.0) - jnp.log1p(jnp.exp(-jnp.abs(x)))


def _adaln_kernel(c_ref, w_ref, b_ref, o_ref):
    c = c_ref[...]
    s = c * jax.nn.sigmoid(c)
    o_ref[0] = jnp.dot(s, w_ref[0], preferred_element_type=F32, precision=HIGHEST) + b_ref[0]


def _adaln(c_pad, w_ada, b_ada):
    n_layer, d, w6 = w_ada.shape
    tn = 1536
    return pl.pallas_call(
        _adaln_kernel,
        out_shape=jax.ShapeDtypeStruct((n_layer, c_pad.shape[0], w6), F32),
        grid=(n_layer, w6 // tn),
        in_specs=[pl.BlockSpec(c_pad.shape, lambda l, j: (0, 0)),
                  pl.BlockSpec((1, d, tn), lambda l, j: (l, 0, j)),
                  pl.BlockSpec((1, 1, tn), lambda l, j: (l, 0, j))],
        out_specs=pl.BlockSpec((1, c_pad.shape[0], tn), lambda l, j: (l, 0, j)),
        compiler_params=_cparams(("parallel", "parallel")),
        name="adaln",
    )(c_pad, w_ada, b_ada.reshape(n_layer, 1, w6))


def _inproj_kernel(x_ref, sc_ref, sh_ref, w_ref, zm_ref, zr_ref, xn_ref):
    j = pl.program_id(2)

    @pl.when(j == 0)
    def _():
        x = x_ref[0]
        xn = x * lax.rsqrt(jnp.mean(x * x, axis=-1, keepdims=True) + EPS)
        xn_ref[...] = (xn * (1.0 + sc_ref[0]) + sh_ref[0]).astype(BF16)

    acc = jnp.dot(xn_ref[...], w_ref[...], preferred_element_type=F32)

    @pl.when(j < Z_RW // RW_BLOCK)
    def _():
        zm_ref[0] = acc.astype(BF16)

    @pl.when(j == Z_RW // RW_BLOCK)
    def _():
        zr_ref[0] = acc


def _inproj(x, scale, shift, w, layer):
    bsz, t_len, d = x.shape
    tm = min(1024, t_len)
    tn = RW_BLOCK
    n_main = Z_RW // tn
    return pl.pallas_call(
        _inproj_kernel,
        out_shape=(jax.ShapeDtypeStruct((bsz, t_len, Z_RW), BF16), jax.ShapeDtypeStruct((bsz, t_len, RW_BLOCK), F32)),
        grid=(bsz, t_len // tm, Z_WIDTH // tn),
        in_specs=[pl.BlockSpec((1, tm, d), lambda b, i, j: (b, i, 0)),
                  pl.BlockSpec((1, 1, d), lambda b, i, j: (b, 0, 0)),
                  pl.BlockSpec((1, 1, d), lambda b, i, j: (b, 0, 0)),
                  pl.BlockSpec((d, tn), lambda b, i, j: (layer, j))],
        out_specs=(pl.BlockSpec((1, tm, tn), lambda b, i, j: (b, i, jnp.minimum(j, n_main - 1))),
                   pl.BlockSpec((1, tm, tn), lambda b, i, j: (b, i, 0))),
        scratch_shapes=[pltpu.VMEM((tm, d), BF16)],
        compiler_params=_cparams(("parallel", "parallel", "arbitrary")),
        name="inproj",
    )(x, scale, shift, w)


def _gmlp_kernel(z_ref, gain_ref, ws_ref, bst_ref, o_ref):
    tm = z_ref.shape[1]
    z = z_ref[0].astype(F32)
    u = jax.nn.gelu(z[:, :GM_WIDTH])
    v = jax.nn.gelu(z[:, GM_WIDTH:])
    v = v * lax.rsqrt(jnp.mean(v * v, axis=-1, keepdims=True) + EPS) * gain_ref[...]
    vb = v.astype(BF16)
    grp = lax.broadcasted_iota(I32, (GM_CHUNK, GM_WIDTH), 1) // GM_GROUP_DIM
    ri = lax.broadcasted_iota(I32, (GM_CHUNK, GM_CHUNK), 0)
    ci = lax.broadcasted_iota(I32, (GM_CHUNK, GM_CHUNK), 1)
    causal = ri >= ci
    bias = jnp.zeros((GM_CHUNK, GM_WIDTH), F32)
    ws = []
    for g in range(GM_GROUPS):
        ws.append(jnp.where(causal, ws_ref[g], 0.0).astype(BF16))
        bias = jnp.where(grp == g, bst_ref[:, g:g + 1], bias)
    for c in range(tm // GM_CHUNK):
        rows = slice(c * GM_CHUNK, (c + 1) * GM_CHUNK)
        vc = vb[rows]
        mixed = bias
        for g in range(GM_GROUPS):
            m = jnp.dot(ws[g], vc, preferred_element_type=F32)
            mixed = mixed + jnp.where(grp == g, m, 0.0)
        o_ref[0, rows, :] = (u[rows] * mixed).astype(o_ref.dtype)


def _gmlp(z, gain, w_s, b_s):
    bsz, t_len, _ = z.shape
    tm = min(512, t_len)
    return pl.pallas_call(
        _gmlp_kernel,
        out_shape=jax.ShapeDtypeStruct((bsz, t_len, GM_WIDTH), BF16),
        grid=(bsz, t_len // tm),
        in_specs=[pl.BlockSpec((1, tm, 2 * GM_WIDTH), lambda b, i: (b, i, Z_GM // (2 * GM_WIDTH))),
                  pl.BlockSpec((1, GM_WIDTH), lambda b, i: (0, 0)),
                  pl.BlockSpec((GM_GROUPS, GM_CHUNK, GM_CHUNK), lambda b, i: (0, 0, 0)),
                  pl.BlockSpec((GM_CHUNK, GM_GROUPS), lambda b, i: (0, 0))],
        out_specs=pl.BlockSpec((1, tm, GM_WIDTH), lambda b, i: (b, i, 0)),
        compiler_params=_cparams(("parallel", "parallel")),
        name="gmlp",
    )(z, gain.reshape(1, GM_WIDTH), w_s, b_s.T)


def _rwprep_kernel(z_ref, zp_ref, mu_ref, wl_ref, w0_ref, a0_ref, kk_ref, ka_ref,
                   r_o, lw_o, k_o, v_o, a_o, b_o, g_o):
    tm = z_ref.shape[1]
    z = z_ref[0]
    prev = jnp.where(pl.program_id(1) > 0, zp_ref[0, 7:8, :], 0.0)
    rowid = lax.broadcasted_iota(I32, z.shape, 0)
    zs = jnp.where(rowid == 0, prev, pltpu.roll(z, 1, axis=0))
    zz = z + mu_ref[...] * (zs - z)
    r = zz[:, 0:RW_WIDTH]
    k = zz[:, RW_WIDTH:2 * RW_WIDTH]
    v = zz[:, 2 * RW_WIDTH:3 * RW_WIDTH]
    lo = zz[:, 3 * RW_WIDTH:3 * RW_WIDTH + RW_LORA]
    lane = lax.broadcasted_iota(I32, (tm, RW_LORA), 1)
    act = jnp.where(lane < RW_DECAY_LORA, jnp.tanh(lo),
                    jnp.where(lane < RW_DECAY_LORA + RW_ICLR_LORA, lo, jax.nn.sigmoid(lo)))
    a_hi, a_lo, _ = _split3(act)
    w_hi, w_lo, _ = _split3(wl_ref[...])
    proj = (jnp.dot(a_hi, w_hi, preferred_element_type=F32) + jnp.dot(a_hi, w_lo, preferred_element_type=F32)
            + jnp.dot(a_lo, w_hi, preferred_element_type=F32))
    xw = -(w0_ref[...] + proj[:, 0:RW_WIDTH])
    softplus = jnp.maximum(xw, 0.0) + jnp.log1p(jnp.exp(-jnp.abs(xw)))
    lw = -jnp.exp(-softplus - 0.5)
    a = jax.nn.sigmoid(a0_ref[...] + proj[:, RW_WIDTH:2 * RW_WIDTH])
    g = proj[:, 2 * RW_WIDTH:3 * RW_WIDTH]
    kk = k * kk_ref[...]
    k2 = k * (1.0 + (a - 1.0) * ka_ref[...])
    for h in range(RW_HEADS):
        sl = slice(h * RW_HEAD_DIM, (h + 1) * RW_HEAD_DIM)
        kkh = kk[:, sl]
        nrm = jnp.sqrt(jnp.sum(kkh * kkh, axis=-1, keepdims=True))
        kkh = kkh / jnp.maximum(nrm, 1e-12)
        r_o[0, h] = r[:, sl]
        lw_o[0, h] = lw[:, sl]
        k_o[0, h] = k2[:, sl]
        v_o[0, h] = v[:, sl]
        a_o[0, h] = -kkh
        b_o[0, h] = kkh * a[:, sl]
        g_o[0, h] = g[:, sl]


def _rwprep(z, mu_pad, w_lora, w0, a0, k_k, k_a):
    bsz, t_len, _ = z.shape
    tm = min(512, t_len)
    hm = jax.ShapeDtypeStruct((bsz, RW_HEADS, t_len, RW_HEAD_DIM), F32)
    hm_spec = pl.BlockSpec((1, RW_HEADS, tm, RW_HEAD_DIM), lambda b, i: (b, 0, i, 0))
    vec = lambda n: pl.BlockSpec((1, n), lambda b, i: (0, 0))
    rw_blk = 0
    return pl.pallas_call(
        _rwprep_kernel,
        out_shape=(hm,) * 7,
        grid=(bsz, t_len // tm),
        in_specs=[pl.BlockSpec((1, tm, RW_BLOCK), lambda b, i: (b, i, rw_blk)),
                  pl.BlockSpec((1, 8, RW_BLOCK), lambda b, i: (b, jnp.maximum(i * (tm // 8) - 1, 0), rw_blk)),
                  vec(RW_BLOCK),
                  pl.BlockSpec((RW_LORA, 3 * RW_WIDTH), lambda b, i: (0, 0)),
                  vec(RW_WIDTH), vec(RW_WIDTH), vec(RW_WIDTH), vec(RW_WIDTH)],
        out_specs=(hm_spec,) * 7,
        compiler_params=_cparams(("parallel", "parallel")),
        name="rwprep",
    )(z, z, mu_pad, w_lora, w0.reshape(1, -1), a0.reshape(1, -1), k_k.reshape(1, -1), k_a.reshape(1, -1))


def _rwscan_kernel(r_ref, lw_ref, k_ref, v_ref, a_ref, b_ref, g_ref, rk_ref, gg_ref, gb_ref, o_ref,
                   s_ref, rp_ref, y_ref, gm_ref, h0_ref, we_ref):
    cl = RW_CHUNK
    tb = r_ref.shape[2]
    n_chunk = tb // cl

    @pl.when(pl.program_id(1) == 0)
    def _():
        s_ref[...] = jnp.zeros_like(s_ref)

    n = RW_HEADS * cl
    ri = lax.broadcasted_iota(I32, (n, n), 0)
    ci = lax.broadcasted_iota(I32, (n, n), 1)
    same_head = (ri // cl) == (ci // cl)
    lower = same_head & (ri >= ci)
    strict = same_head & (ri > ci)
    eye = jnp.where(ri == ci, 1.0, 0.0)
    ones_lower = jnp.where(lower, 1.0, 0.0).astype(BF16)

    def prepare(chunks):
        grp = range(len(chunks))
        each = lambda fn: [fn(u) for u in grp]
        rows = [pl.ds(pl.multiple_of(c * cl, cl), cl) for c in chunks]
        stack = lambda ref: each(lambda u: ref[0, :, rows[u], :].reshape(n, RW_HEAD_DIM))
        r, lw, k, v, a, b = (stack(ref) for ref in (r_ref, lw_ref, k_ref, v_ref, a_ref, b_ref))
        hd = RW_HEAD_DIM
        parts = each(lambda u: jnp.concatenate(_split3(lw[u]), axis=-1))
        sums = each(lambda u: jnp.dot(ones_lower, parts[u], preferred_element_type=F32))
        cw = each(lambda u: sums[u][:, :hd] + sums[u][:, hd:2 * hd] + sums[u][:, 2 * hd:])
        w_in = each(lambda u: jnp.exp(cw[u]))
        w_inv = each(lambda u: jnp.exp(-cw[u]))
        rt = each(lambda u: r[u] * w_in[u])
        at = each(lambda u: a[u] * jnp.exp(cw[u] - lw[u]))
        kt = each(lambda u: k[u] * w_inv[u])
        bt = each(lambda u: b[u] * w_inv[u])
        w_end = each(lambda u: w_in[u].reshape(RW_HEADS, cl, RW_HEAD_DIM)[:, cl - 1:cl, :])
        w_end_rows = each(lambda u: jnp.broadcast_to(w_end[u], (RW_HEADS, cl, RW_HEAD_DIM)).reshape(n, RW_HEAD_DIM))
        a_ab = each(lambda u: jnp.where(strict, _mm_nt(at[u], bt[u]), 0.0))
        a_ak = each(lambda u: jnp.where(strict, _mm_nt(at[u], kt[u]), 0.0))
        m_rb = each(lambda u: jnp.where(lower, _mm_nt(rt[u], bt[u]), 0.0))
        m_rk = each(lambda u: jnp.where(lower, _mm_nt(rt[u], kt[u]), 0.0))
        inv = each(lambda u: eye + a_ab[u])
        p = a_ab
        for _ in range(cl.bit_length() - 2):
            p = [_mm(p[u], p[u]) for u in grp]
            inv = [inv[u] + _mm(inv[u], p[u]) for u in grp]
        akv = each(lambda u: _mm(a_ak[u], v[u]))
        apz = each(lambda u: _mm(inv[u], jnp.concatenate([at[u], akv[u]], axis=-1)).astype(BF16))
        mix = each(lambda u: jnp.dot(m_rb[u].astype(BF16), apz[u], preferred_element_type=F32))
        bend = each(lambda u: bt[u] * w_end_rows[u])
        kend = each(lambda u: kt[u] * w_end_rows[u])
        rp = each(lambda u: (rt[u] + mix[u][:, :hd]).astype(BF16))
        y0 = each(lambda u: mix[u][:, hd:] + _mm(m_rk[u], v[u]))
        for u in grp:
            for h in range(RW_HEADS):
                hs = slice(h * cl, (h + 1) * cl)
                both = _mm_tn(apz[u][hs], bend[u][hs])
                rp_ref[h, rows[u], :] = rp[u][hs]
                y_ref[h, rows[u], :] = y0[u][hs]
                gm_ref[h, rows[u], :] = both[:hd].astype(BF16)
                h0_ref[h, rows[u], :] = both[hd:] + _mm_tn(v[u][hs], kend[u][hs])
                we_ref[h, chunks[u]] = w_end[u][h]

    def prepare_step(i, carry):
        prepare([i * RW_PREP_UNROLL + u for u in range(RW_PREP_UNROLL)])
        return carry

    lax.fori_loop(0, n_chunk // RW_PREP_UNROLL, prepare_step, 0)

    def advance(c, carry):
        rows = pl.ds(pl.multiple_of(c * cl, cl), cl)
        for h in range(RW_HEADS):
            s = s_ref[h]
            sb = s.astype(BF16)
            y_ref[h, rows, :] = y_ref[h, rows, :] + lax.dot_general(
                rp_ref[h, rows, :], sb, (((1,), (1,)), ((), ())), preferred_element_type=F32)
            s_ref[h] = (s * we_ref[h, c] + jnp.dot(sb, gm_ref[h, rows, :], preferred_element_type=F32)
                        + h0_ref[h, rows, :])
        return carry

    lax.fori_loop(0, n_chunk, advance, 0)

    for h in range(RW_HEADS):
        y = y_ref[h]
        mu = jnp.mean(y, axis=-1, keepdims=True)
        yc = y - mu
        var = jnp.mean(yc * yc, axis=-1, keepdims=True)
        yn = yc * lax.rsqrt(var + RW_GN_EPS) * gg_ref[h] + gb_ref[h]
        v = v_ref[0, h]
        bonus = jnp.sum(r_ref[0, h] * k_ref[0, h] * rk_ref[h], axis=-1, keepdims=True) * v
        o_ref[0, h] = ((yn + bonus) * g_ref[0, h]).astype(o_ref.dtype)


def _rwscan(r, lw, k, v, a, b, g, r_k, gn_gain, gn_bias):
    bsz, _, t_len, _ = r.shape
    tb = min(512, t_len)
    hm_spec = pl.BlockSpec((1, RW_HEADS, tb, RW_HEAD_DIM), lambda bi, i: (bi, 0, i, 0))
    par = pl.BlockSpec((RW_HEADS, 1, RW_HEAD_DIM), lambda bi, i: (0, 0, 0))
    hshape = (RW_HEADS, 1, RW_HEAD_DIM)
    return pl.pallas_call(
        _rwscan_kernel,
        out_shape=jax.ShapeDtypeStruct((bsz, RW_HEADS, t_len, RW_HEAD_DIM), BF16),
        grid=(bsz, t_len // tb),
        in_specs=[hm_spec] * 7 + [par] * 3,
        out_specs=hm_spec,
        scratch_shapes=[pltpu.VMEM((RW_HEADS, RW_HEAD_DIM, RW_HEAD_DIM), F32),
                        pltpu.VMEM((RW_HEADS, tb, RW_HEAD_DIM), BF16), pltpu.VMEM((RW_HEADS, tb, RW_HEAD_DIM), F32),
                        pltpu.VMEM((RW_HEADS, tb, RW_HEAD_DIM), BF16), pltpu.VMEM((RW_HEADS, tb, RW_HEAD_DIM), F32),
                        pltpu.VMEM((RW_HEADS, tb // RW_CHUNK, 1, RW_HEAD_DIM), F32)],
        compiler_params=_cparams(("parallel", "arbitrary")),
        name="rwscan",
    )(r, lw, k, v, a, b, g, r_k.reshape(hshape), gn_gain.reshape(hshape), gn_bias.reshape(hshape))


FOX_PAIRS = FOX_HEADS // 2
FOX_EXTRA = 3
FOX_ACC_ROWS = FOX_HEAD_DIM + 16
FOX_LOOKAHEAD = 2


def _fox_bias_selector():
    sel = np.zeros((LANES, 2 * FOX_HEADS * LANES), np.float32)
    for h in range(FOX_HEADS):
        base = FOX_HEAD_DIM if h % 2 == 0 else 0
        for p in range(FOX_EXTRA):
            sel[p * FOX_HEADS + h, h * LANES + base + p] = 1.0
            sel[p * FOX_HEADS + h, (FOX_HEADS + h) * LANES + base + FOX_EXTRA + p] = -1.0
    return sel


def _foxprep_kernel(z_ref, f_ref, fb_ref, qg_ref, kg_ref, sel_ref, q_o, k_o, vt_o, carry_ref):
    tm = z_ref.shape[1]

    @pl.when(pl.program_id(1) == 0)
    def _():
        carry_ref[...] = jnp.zeros_like(carry_ref)

    log_f = _log_sigmoid(f_ref[0] + fb_ref[...])
    cum = carry_ref[...] + _tri_cumsum(log_f, tm)
    carry_ref[...] = cum[tm - 1:tm, :]
    lane = lax.broadcasted_iota(I32, (tm, LANES), 1)
    hi, mid, lo = (p.astype(F32) for p in _split3(cum * LOG2E))
    packed = jnp.where(lane < FOX_HEADS, hi,
                       jnp.where(lane < 2 * FOX_HEADS, pltpu.roll(mid, FOX_HEADS, axis=1),
                                 pltpu.roll(lo, 2 * FOX_HEADS, axis=1)))
    packed = jnp.where(lane < FOX_EXTRA * FOX_HEADS, packed, 0.0).astype(BF16)
    extra = jnp.dot(packed, sel_ref[...], preferred_element_type=F32)

    left = lane < FOX_HEAD_DIM
    in_half = lane % FOX_HEAD_DIM
    ones_q = jnp.where((in_half >= FOX_EXTRA) & (in_half < 2 * FOX_EXTRA), 1.0, 0.0)
    ones_k = jnp.where(in_half < FOX_EXTRA, 1.0, 0.0)

    def normed(block, gain):
        sq = block * block
        s_left = jnp.sum(jnp.where(left, sq, 0.0), axis=-1, keepdims=True)
        s_right = jnp.sum(jnp.where(left, 0.0, sq), axis=-1, keepdims=True)
        ms = jnp.where(left, s_left, s_right) * (1.0 / FOX_HEAD_DIM)
        return block * lax.rsqrt(ms + EPS) * gain

    for j in range(FOX_PAIRS):
        qn = normed(z_ref[0, :, j * LANES:(j + 1) * LANES].astype(F32), qg_ref[...] * (ATTN_SCALE * LOG2E))
        kn = normed(z_ref[0, :, FOX_WIDTH + j * LANES:FOX_WIDTH + (j + 1) * LANES].astype(F32), kg_ref[...])
        for par in range(2):
            h = 2 * j + par
            own = left if par == 0 else jnp.logical_not(left)
            q_o[0, h] = jnp.where(own, qn, extra[:, h * LANES:(h + 1) * LANES] + ones_q).astype(BF16)
            k_o[0, h] = jnp.where(own, kn, extra[:, (FOX_HEADS + h) * LANES:(FOX_HEADS + h + 1) * LANES]
                                  + ones_k).astype(BF16)
    ri = lax.broadcasted_iota(I32, (FOX_WIDTH, FOX_WIDTH), 0)
    ci = lax.broadcasted_iota(I32, (FOX_WIDTH, FOX_WIDTH), 1)
    eye = jnp.where(ri == ci, 1.0, 0.0).astype(BF16)
    v = z_ref[0, :, 2 * FOX_WIDTH:3 * FOX_WIDTH].astype(BF16)
    vt_o[0] = lax.dot_general(eye, v, (((1,), (1,)), ((), ())), preferred_element_type=F32).astype(BF16)


def _foxprep(z, z_rw, f_bias_pad, q_gain, k_gain):
    bsz, t_len, _ = z.shape
    tm = min(512, t_len)
    qk = jax.ShapeDtypeStruct((bsz, FOX_HEADS, t_len, LANES), BF16)
    qk_spec = pl.BlockSpec((1, FOX_HEADS, tm, LANES), lambda b, i: (b, 0, i, 0))
    sel = jnp.asarray(_fox_bias_selector(), BF16)
    return pl.pallas_call(
        _foxprep_kernel,
        out_shape=(qk, qk, jax.ShapeDtypeStruct((bsz, FOX_WIDTH, t_len), BF16)),
        grid=(bsz, t_len // tm),
        in_specs=[pl.BlockSpec((1, tm, 3 * FOX_WIDTH), lambda b, i: (b, i, Z_FOX // (3 * FOX_WIDTH))),
                  pl.BlockSpec((1, tm, LANES), lambda b, i: (b, i, (Z_F - Z_RW) // LANES)),
                  pl.BlockSpec((1, LANES), lambda b, i: (0, 0)),
                  pl.BlockSpec((1, LANES), lambda b, i: (0, 0)),
                  pl.BlockSpec((1, LANES), lambda b, i: (0, 0)),
                  pl.BlockSpec(sel.shape, lambda b, i: (0, 0))],
        out_specs=(qk_spec, qk_spec, pl.BlockSpec((1, FOX_WIDTH, tm), lambda b, i: (b, 0, i))),
        scratch_shapes=[pltpu.VMEM((1, LANES), F32)],
        compiler_params=_cparams(("parallel", "arbitrary")),
        name="foxprep",
    )(z, z_rw, f_bias_pad, jnp.tile(q_gain.reshape(1, -1), (1, 2)), jnp.tile(k_gain.reshape(1, -1), (1, 2)), sel)


def _fox_kernel(qi_ref, kj_ref, q_ref, k_ref, vt_ref, o_ref, m_ref, acc_ref):
    i = qi_ref[pl.program_id(1)]
    j = kj_ref[pl.program_id(1)]
    tq = q_ref.shape[2]
    tk = k_ref.shape[2]
    sub = 8

    @pl.when(j == 0)
    def _():
        m_ref[...] = jnp.full_like(m_ref, MASK_VALUE)
        acc_ref[...] = jnp.zeros_like(acc_ref)

    ones_rows = jnp.ones((FOX_ACC_ROWS - FOX_HEAD_DIM, tk), BF16)

    def scores(h):
        return lax.dot_general(k_ref[0, h], q_ref[0, h], (((1,), (1,)), ((), ())), preferred_element_type=F32)

    def update(diagonal):
        if diagonal:
            key = lax.broadcasted_iota(I32, (tk, tq), 0)
            qry = lax.broadcasted_iota(I32, (tk, tq), 1)
            keep = key <= qry
        ahead = [scores(h) for h in range(FOX_LOOKAHEAD)]
        for h in range(FOX_HEADS):
            s = ahead.pop(0)
            if h + FOX_LOOKAHEAD < FOX_HEADS:
                ahead.append(scores(h + FOX_LOOKAHEAD))
            if diagonal:
                s = jnp.where(keep, s, MASK_VALUE)
            s3 = s.reshape(tk // sub, sub, tq)
            m_prev = m_ref[h]
            m_cur = jnp.max(jnp.max(s3, axis=0), axis=0, keepdims=True)
            m_new = jnp.maximum(m_prev, m_cur)
            alpha = jnp.exp2(m_prev - m_new)
            p = jnp.exp2(s3 - m_new[None]).astype(BF16).reshape(tk, tq)
            lhs = jnp.concatenate([vt_ref[0, h * FOX_HEAD_DIM:(h + 1) * FOX_HEAD_DIM, :], ones_rows], axis=0)
            pv = jnp.dot(lhs, p, preferred_element_type=F32)
            acc = acc_ref[h].reshape(FOX_ACC_ROWS // sub, sub, tq) * alpha[None]
            acc_ref[h] = acc.reshape(FOX_ACC_ROWS, tq) + pv
            m_ref[h] = m_new

    @pl.when(j < i)
    def _():
        update(False)

    @pl.when(j == i)
    def _():
        update(True)
        outs = []
        for h in range(FOX_HEADS):
            acc = acc_ref[h]
            outs.append((acc[:FOX_HEAD_DIM] / acc[FOX_HEAD_DIM:FOX_HEAD_DIM + 1]).astype(BF16))
        out_t = jnp.concatenate(outs, axis=0)
        ri = lax.broadcasted_iota(I32, (tq, tq), 0)
        ci = lax.broadcasted_iota(I32, (tq, tq), 1)
        eye = jnp.where(ri == ci, 1.0, 0.0).astype(BF16)
        o_ref[0] = lax.dot_general(eye, out_t, (((1,), (1,)), ((), ())),
                                   preferred_element_type=F32).astype(o_ref.dtype)


def _fox(q, k, vt):
    bsz, _, t_len, _ = q.shape
    tq = min(512, t_len)
    n_blk = t_len // tq
    pairs = [(i, j) for i in range(n_blk) for j in range(i + 1)]
    qi = jnp.asarray([p[0] for p in pairs], I32)
    kj = jnp.asarray([p[1] for p in pairs], I32)
    grid_spec = pltpu.PrefetchScalarGridSpec(
        num_scalar_prefetch=2,
        grid=(bsz, len(pairs)),
        in_specs=[pl.BlockSpec((1, FOX_HEADS, tq, LANES), lambda b, s, qi, kj: (b, 0, qi[s], 0)),
                  pl.BlockSpec((1, FOX_HEADS, tq, LANES), lambda b, s, qi, kj: (b, 0, kj[s], 0)),
                  pl.BlockSpec((1, FOX_WIDTH, tq), lambda b, s, qi, kj: (b, 0, kj[s]))],
        out_specs=pl.BlockSpec((1, tq, FOX_WIDTH), lambda b, s, qi, kj: (b, qi[s], 0)),
        scratch_shapes=[pltpu.VMEM((FOX_HEADS, 8, tq), F32), pltpu.VMEM((FOX_HEADS, FOX_ACC_ROWS, tq), F32)],
    )
    return pl.pallas_call(
        _fox_kernel,
        out_shape=jax.ShapeDtypeStruct((bsz, t_len, FOX_WIDTH), BF16),
        grid_spec=grid_spec,
        compiler_params=_cparams(("parallel", "arbitrary")),
        name="fox",
    )(qi, kj, q, k, vt)


def _merge_kernel(zg_ref, ygm_ref, yrw_ref, yfox_ref, x_ref, g1_ref, sc2_ref, sh2_ref, pb_ref, wo_ref, wr_ref, br_ref,
                  x1_o, h2_o, idx_o, gate_o, rank_o, cnt_o, carry_ref):
    tm = x_ref.shape[1]

    @pl.when((pl.program_id(0) == 0) & (pl.program_id(1) == 0))
    def _():
        carry_ref[...] = jnp.zeros_like(carry_ref)

    sg = 0.5 * jnp.tanh(0.5 * zg_ref[0].astype(F32)) + 0.5
    p_gm = jnp.dot(ygm_ref[0], pb_ref[0:GM_WIDTH, :], preferred_element_type=F32)
    y_rw = jnp.concatenate([yrw_ref[0, h] for h in range(RW_HEADS)], axis=-1)
    p_rw = jnp.dot(y_rw, pb_ref[GM_WIDTH:GM_WIDTH + RW_WIDTH, :], preferred_element_type=F32)
    p_fox = jnp.dot(yfox_ref[0], pb_ref[GM_WIDTH + RW_WIDTH:, :], preferred_element_type=F32)
    merged = sg[:, 0:D_MODEL] * p_gm + sg[:, D_MODEL:2 * D_MODEL] * p_rw + sg[:, 2 * D_MODEL:] * p_fox
    x1 = x_ref[0] + g1_ref[0] * jnp.dot(merged.astype(BF16), wo_ref[...], preferred_element_type=F32)
    x1_o[0] = x1
    h2 = x1 * lax.rsqrt(jnp.mean(x1 * x1, axis=-1, keepdims=True) + EPS) * (1.0 + sc2_ref[0]) + sh2_ref[0]
    h2_o[0] = _pack_halves(h2)

    h_hi, h_lo, _ = _split3(h2)
    w_hi, w_lo, _ = _split3(wr_ref[...])
    logits = (jnp.dot(h_hi, w_hi, preferred_element_type=F32) + jnp.dot(h_hi, w_lo, preferred_element_type=F32)
              + jnp.dot(h_lo, w_hi, preferred_element_type=F32)) + br_ref[...]
    lane = lax.broadcasted_iota(I32, (tm, N_EXPERTS), 1)
    vals, idxs = [], []
    rest = logits
    for _ in range(TOP_K):
        m = jnp.max(rest, axis=-1, keepdims=True)
        am = jnp.min(jnp.where(rest == m, lane, N_EXPERTS), axis=-1, keepdims=True)
        vals.append(m)
        idxs.append(am)
        rest = jnp.where(lane == am, -jnp.inf, rest)
    exps = [jnp.exp(val - vals[0]) for val in vals]
    denom = exps[0] + exps[1] + exps[2] + exps[3]

    onehot = jnp.zeros((tm, N_EXPERTS), F32)
    for am in idxs:
        onehot = onehot + jnp.where(lane == am, 1.0, 0.0)
    ri = lax.broadcasted_iota(I32, (tm, tm), 0)
    ci = lax.broadcasted_iota(I32, (tm, tm), 1)
    before = jnp.where(ri > ci, 1.0, 0.0).astype(BF16)
    seen = carry_ref[...] + jnp.dot(before, onehot.astype(BF16), preferred_element_type=F32)
    lane_k = lax.broadcasted_iota(I32, (tm, TOP_K), 1)
    idx_out = jnp.zeros((tm, TOP_K), I32)
    gate_out = jnp.zeros((tm, TOP_K), F32)
    rank_out = jnp.zeros((tm, TOP_K), I32)
    for kk in range(TOP_K):
        rank = jnp.sum(jnp.where(lane == idxs[kk], seen, 0.0), axis=-1, keepdims=True).astype(I32)
        idx_out = jnp.where(lane_k == kk, idxs[kk], idx_out)
        gate_out = jnp.where(lane_k == kk, exps[kk] / denom, gate_out)
        rank_out = jnp.where(lane_k == kk, rank, rank_out)
    idx_o[0] = idx_out
    gate_o[0] = gate_out
    rank_o[0] = rank_out
    total = carry_ref[...] + jnp.sum(onehot, axis=0, keepdims=True)
    carry_ref[...] = total
    cnt_o[...] = total.astype(I32)


def _merge(z, y_gm, y_rw, y_fox, x, gate1, scale2, shift2, w_branch, w_o, w_router, b_router, layer):
    bsz, t_len, d = x.shape
    tm = min(512, t_len)
    row = lambda w: pl.BlockSpec((1, tm, w), lambda b, i: (b, i, 0))
    mod = pl.BlockSpec((1, 1, d), lambda b, i: (b, 0, 0))
    full = lambda shape: pl.BlockSpec(shape, lambda b, i: (0,) * len(shape))
    return pl.pallas_call(
        _merge_kernel,
        out_shape=(jax.ShapeDtypeStruct((bsz, t_len, d), F32), jax.ShapeDtypeStruct((bsz, t_len, d // 2), I32),
                   jax.ShapeDtypeStruct((bsz, t_len, TOP_K), I32), jax.ShapeDtypeStruct((bsz, t_len, TOP_K), F32),
                   jax.ShapeDtypeStruct((bsz, t_len, TOP_K), I32), jax.ShapeDtypeStruct((1, N_EXPERTS), I32)),
        grid=(bsz, t_len // tm),
        in_specs=[row(N_BRANCH * D_MODEL), row(GM_WIDTH),
                  pl.BlockSpec((1, RW_HEADS, tm, RW_HEAD_DIM), lambda b, i: (b, 0, i, 0)),
                  row(FOX_WIDTH), row(d), mod, mod, mod,
                  pl.BlockSpec((MIX_WIDTH, d), lambda b, i: (layer, 0)), pl.BlockSpec((d, d), lambda b, i: (layer, 0)),
                  full(w_router.shape), full((1, N_EXPERTS))],
        out_specs=(row(d), row(d // 2), row(TOP_K), row(TOP_K), row(TOP_K), full((1, N_EXPERTS))),
        scratch_shapes=[pltpu.VMEM((1, N_EXPERTS), F32)],
        compiler_params=_cparams(("arbitrary", "arbitrary")),
        name="merge_router",
    )(z, y_gm, y_rw, y_fox, x, gate1, scale2, shift2, w_branch, w_o, w_router, b_router.reshape(1, N_EXPERTS))


def _sc_mesh():
    return plsc.VectorSubcoreMesh(core_axis_name="c", subcore_axis_name="s",
                                  num_cores=SC_CORES, num_subcores=SC_SUBCORES)


def _sc_worker():
    return lax.axis_index("s") * SC_CORES + lax.axis_index("c")


def _sc_scatter_rows(src, idx3, n_out):
    _, d = src.shape
    n_copy, n_grp, _ = idx3.shape
    grp_per_w = n_grp // SC_WORKERS
    assert grp_per_w % 2 == 0

    def body(src_hbm, idx_hbm, out_hbm, idx_v, rows_a, rows_b, sem):
        g0 = _sc_worker() * grp_per_w
        for q in range(n_copy):
            pltpu.sync_copy(idx_hbm.at[q, pl.ds(g0, grp_per_w)], idx_v.at[pl.ds(q * grp_per_w, grp_per_w)])

        @pl.loop(0, grp_per_w, step=2)
        def _(j):
            read_a = pltpu.async_copy(src_hbm.at[pl.ds((g0 + j) * SC_ROWS, SC_ROWS)], rows_a, sem.at[0])
            read_b = pltpu.async_copy(src_hbm.at[pl.ds((g0 + j + 1) * SC_ROWS, SC_ROWS)], rows_b, sem.at[1])
            read_a.wait()
            put_a = [pltpu.async_copy(rows_a, out_hbm.at[idx_v.at[q * grp_per_w + j]], sem.at[2])
                     for q in range(n_copy)]
            read_b.wait()
            put_b = [pltpu.async_copy(rows_b, out_hbm.at[idx_v.at[q * grp_per_w + j + 1]], sem.at[3])
                     for q in range(n_copy)]
            for cp in put_a + put_b:
                cp.wait()

    return pl.kernel(
        body, out_type=jax.ShapeDtypeStruct((n_out, d), src.dtype), mesh=_sc_mesh(),
        scratch_types=[pltpu.VMEM((n_copy * grp_per_w, SC_ROWS), I32), pltpu.VMEM((SC_ROWS, d), src.dtype),
                       pltpu.VMEM((SC_ROWS, d), src.dtype), pltpu.SemaphoreType.DMA((4,))],
        name="sc_dispatch",
    )(src, idx3)


def _sc_gather_rows(table, idx2):
    _, d = table.shape
    n_grp, _ = idx2.shape
    grp_per_w = n_grp // SC_WORKERS
    assert grp_per_w % 2 == 0

    def body(table_hbm, idx_hbm, out_hbm, idx_v, rows_a, rows_b, sem):
        g0 = _sc_worker() * grp_per_w
        pltpu.sync_copy(idx_hbm.at[pl.ds(g0, grp_per_w)], idx_v)

        @pl.loop(0, grp_per_w, step=2)
        def _(j):
            get_a = pltpu.async_copy(table_hbm.at[idx_v.at[j]], rows_a, sem.at[0])
            get_b = pltpu.async_copy(table_hbm.at[idx_v.at[j + 1]], rows_b, sem.at[1])
            get_a.wait()
            put_a = pltpu.async_copy(rows_a, out_hbm.at[pl.ds((g0 + j) * SC_ROWS, SC_ROWS)], sem.at[2])
            get_b.wait()
            put_b = pltpu.async_copy(rows_b, out_hbm.at[pl.ds((g0 + j + 1) * SC_ROWS, SC_ROWS)], sem.at[3])
            put_a.wait()
            put_b.wait()

    return pl.kernel(
        body, out_type=jax.ShapeDtypeStruct((n_grp * SC_ROWS, d), table.dtype), mesh=_sc_mesh(),
        scratch_types=[pltpu.VMEM((grp_per_w, SC_ROWS), I32), pltpu.VMEM((SC_ROWS, d), table.dtype),
                       pltpu.VMEM((SC_ROWS, d), table.dtype), pltpu.SemaphoreType.DMA((4,))],
        name="sc_combine_gather",
    )(table, idx2)


def _ffn_weight_copies(expert, wgu_hbm, wd_hbm, stage_gu, stage_d, sem):
    return (pltpu.make_async_copy(wgu_hbm.at[expert], stage_gu, sem.at[0]),
            pltpu.make_async_copy(wd_hbm.at[expert], stage_d, sem.at[1]))


def _ffn_kernel(be_ref, first_ref, nxt_ref, nv_ref, x_ref, wgu_hbm, wd_hbm, bgu_ref, bd_ref, o_ref,
                stage_gu, stage_d, wgu_b, wd_b, sem):
    step = pl.program_id(0)
    copies = functools.partial(_ffn_weight_copies, wgu_hbm=wgu_hbm, wd_hbm=wd_hbm, stage_gu=stage_gu,
                               stage_d=stage_d, sem=sem)

    @pl.when(step == 0)
    def _():
        for cp in copies(be_ref[0]):
            cp.start()

    for b in range(FFN_STEP_BLOCKS):
        idx = step * FFN_STEP_BLOCKS + b
        expert = be_ref[idx]
        n_valid = nv_ref[idx]
        rows = slice(b * MOE_BLOCK, (b + 1) * MOE_BLOCK)

        @pl.when(first_ref[idx] == 1)
        def _():
            for cp in copies(expert):
                cp.wait()
            wgu_b[...] = stage_gu[...].astype(BF16)
            wd_b[...] = stage_d[...].astype(BF16)

            @pl.when(nxt_ref[idx] >= 0)
            def _():
                for cp in copies(nxt_ref[idx]):
                    cp.start()

        @pl.when(n_valid > 0)
        def _():
            rowid = lax.broadcasted_iota(I32, (MOE_BLOCK, x_ref.shape[1]), 0)
            xp = jnp.where(rowid < n_valid, x_ref[rows, :], 0)
            x = jnp.concatenate(_unpack_halves(xp), axis=-1).astype(BF16)
            gu = jnp.dot(x, wgu_b[...], preferred_element_type=F32) + bgu_ref[expert]
            g_ = jnp.minimum(gu[:, :D_FF], SWIGLU_LIMIT)
            u_ = jnp.clip(gu[:, D_FF:], -SWIGLU_LIMIT, SWIGLU_LIMIT)
            act = (u_ + 1.0) * (g_ * jax.nn.sigmoid(SWIGLU_ALPHA * g_))
            y = jnp.dot(act.astype(BF16), wd_b[...], preferred_element_type=F32) + bd_ref[expert]
            o_ref[rows, :] = _pack_halves(y)

        @pl.when(n_valid <= 0)
        def _():
            o_ref[rows, :] = jnp.zeros((MOE_BLOCK, o_ref.shape[1]), o_ref.dtype)


def _ffn(block_expert, block_first, block_next, block_valid, xin, w_gate_up, b_gate_up, w_down, b_down):
    n_rows, dp = xin.shape
    d = 2 * dp
    step_rows = FFN_STEP_BLOCKS * MOE_BLOCK
    resident = lambda arr: pl.BlockSpec(arr.shape, lambda i, *_: (0,) * arr.ndim)
    grid_spec = pltpu.PrefetchScalarGridSpec(
        num_scalar_prefetch=4,
        grid=(n_rows // step_rows,),
        in_specs=[pl.BlockSpec((step_rows, dp), lambda i, *_: (i, 0)),
                  pl.BlockSpec(memory_space=pl.ANY), pl.BlockSpec(memory_space=pl.ANY),
                  resident(b_gate_up), resident(b_down)],
        out_specs=pl.BlockSpec((step_rows, dp), lambda i, *_: (i, 0)),
        scratch_shapes=[pltpu.VMEM((d, 2 * D_FF), F32), pltpu.VMEM((D_FF, d), F32),
                        pltpu.VMEM((d, 2 * D_FF), BF16), pltpu.VMEM((D_FF, d), BF16),
                        pltpu.SemaphoreType.DMA((2,))],
    )
    return pl.pallas_call(
        _ffn_kernel,
        out_shape=jax.ShapeDtypeStruct((n_rows, dp), I32),
        grid_spec=grid_spec,
        compiler_params=pltpu.CompilerParams(dimension_semantics=("arbitrary",), vmem_limit_bytes=FFN_VMEM_LIMIT),
        name="expert_ffn",
    )(block_expert, block_first, block_next, block_valid, xin, w_gate_up, w_down, b_gate_up, b_down)


def _combine_kernel(x1_ref, g2_ref, gate_ref, yg_ref, o_ref):
    gate = gate_ref[0]
    y_lo = y_hi = None
    for q in range(TOP_K):
        lo, hi = _unpack_halves(yg_ref[q, 0])
        wq = gate[:, q:q + 1]
        y_lo = wq * lo if y_lo is None else y_lo + wq * lo
        y_hi = wq * hi if y_hi is None else y_hi + wq * hi
    o_ref[0] = x1_ref[0] + g2_ref[0] * jnp.concatenate([y_lo, y_hi], axis=-1)


def _combine(x1, gate2, gate, yg):
    bsz, t_len, d = x1.shape
    tm = min(512, t_len)
    return pl.pallas_call(
        _combine_kernel,
        out_shape=jax.ShapeDtypeStruct((bsz, t_len, d), F32),
        grid=(bsz, t_len // tm),
        in_specs=[pl.BlockSpec((1, tm, d), lambda b, i: (b, i, 0)),
                  pl.BlockSpec((1, 1, d), lambda b, i: (b, 0, 0)),
                  pl.BlockSpec((1, tm, TOP_K), lambda b, i: (b, i, 0)),
                  pl.BlockSpec((TOP_K, 1, tm, d // 2), lambda b, i: (0, b, i, 0))],
        out_specs=pl.BlockSpec((1, tm, d), lambda b, i: (b, i, 0)),
        compiler_params=_cparams(("parallel", "parallel")),
        name="moe_combine",
    )(x1, gate2, gate, yg)


def _moe(x1, gate2, h2, top_idx, gate, rank, counts, w_gate_up, b_gate_up, w_down, b_down, layer):
    bsz, t_len, d = h2.shape
    n_tok = bsz * t_len
    n_assign = n_tok * TOP_K
    n_blocks = -(-n_assign // MOE_BLOCK) + N_EXPERTS
    counts = counts.reshape(N_EXPERTS)
    blocks_e = (counts + MOE_BLOCK - 1) // MOE_BLOCK
    blk_end = jnp.cumsum(blocks_e)
    blk_start = blk_end - blocks_e
    experts = jnp.arange(N_EXPERTS, dtype=I32)
    onehot = top_idx.reshape(n_tok, TOP_K, 1) == experts
    dest = jnp.sum(jnp.where(onehot, blk_start * MOE_BLOCK, 0), axis=-1) + rank.reshape(n_tok, TOP_K)
    dest_t = dest.T.astype(I32)
    blk = jnp.arange(n_blocks, dtype=I32)
    block_expert = jnp.minimum(jnp.sum(blk_end[None, :] <= blk[:, None], axis=1), N_EXPERTS - 1).astype(I32)
    be_hot = block_expert[:, None] == experts
    cnt_b = jnp.sum(jnp.where(be_hot, counts, 0), axis=1)
    start_b = jnp.sum(jnp.where(be_hot, blk_start, 0), axis=1)
    block_valid = jnp.clip(cnt_b - (blk - start_b) * MOE_BLOCK, 0, MOE_BLOCK).astype(I32)
    xin = _sc_scatter_rows(h2.reshape(n_tok, d), dest_t.reshape(TOP_K, n_tok // SC_ROWS, SC_ROWS),
                           n_blocks * MOE_BLOCK)
    block_first = jnp.concatenate([jnp.ones((1,), I32), (block_expert[1:] != block_expert[:-1]).astype(I32)])
    run_start = jnp.where(block_first == 1, blk, n_blocks)
    later_start = lax.cummin(jnp.concatenate([run_start[1:], jnp.full((1,), n_blocks, I32)]), reverse=True)
    next_expert = jnp.concatenate([block_expert, jnp.full((1,), -1 - layer * N_EXPERTS, I32)])[later_start]
    yb = _ffn(block_expert + layer * N_EXPERTS, block_first, next_expert + layer * N_EXPERTS, block_valid, xin,
              w_gate_up, b_gate_up, w_down, b_down)
    yg = _sc_gather_rows(yb, dest_t.reshape(n_assign // SC_ROWS, SC_ROWS))
    return _combine(x1, gate2, gate, yg.reshape(TOP_K, bsz, t_len, d))


def _permute_kernel(w_ref, o_ref):
    o_gm = 0
    o_rw = o_gm + 2 * GM_WIDTH
    o_fox = o_rw + RW_SHIFT_WIDTH
    o_f = o_fox + 3 * FOX_WIDTH
    o_gate = o_f + FOX_HEADS
    w = w_ref[0]
    o_ref[0, :, Z_GATE:Z_FOX] = w[:, o_gate:o_gate + N_BRANCH * D_MODEL].astype(BF16)
    o_ref[0, :, Z_FOX:Z_GM] = w[:, o_fox:o_f].astype(BF16)
    o_ref[0, :, Z_GM:Z_RW] = w[:, o_gm:o_rw].astype(BF16)
    o_ref[0, :, Z_RW:Z_F] = w[:, o_rw:o_fox].astype(BF16)
    tail = jnp.concatenate([w[:, o_f:o_gate], jnp.zeros((w.shape[0], Z_WIDTH - Z_F - FOX_HEADS), F32)], axis=-1)
    o_ref[0, :, Z_F:Z_WIDTH] = tail.astype(BF16)


def _permute_w_in(w_in):
    n_layer, d, w_cols = w_in.shape
    tr = 256
    return pl.pallas_call(
        _permute_kernel,
        out_shape=jax.ShapeDtypeStruct((n_layer, d, Z_WIDTH), BF16),
        grid=(n_layer, d // tr),
        in_specs=[pl.BlockSpec((1, tr, w_cols), lambda l, i: (l, i, 0))],
        out_specs=pl.BlockSpec((1, tr, Z_WIDTH), lambda l, i: (l, i, 0)),
        compiler_params=_cparams(("parallel", "parallel")),
        name="permute_w_in",
    )(w_in)


def _layer(x, mod, w_in_p, gm_v_gain, gm_w_s, gm_b_s, mu_pad, w_lora, rw_w0, rw_a0, rw_k_k, rw_k_a, rw_r_k,
           rw_gn_gain, rw_gn_bias, f_bias_pad, fox_q_gain, fox_k_gain, w_branch, w_o, w_router, b_router,
           w_gate_up, b_gate_up, w_down, b_down, layer):
    shift1, scale1, gate1, shift2, scale2, gate2 = (mod[:, i][:, None, :] for i in range(6))
    z, z_rw = _inproj(x, scale1, shift1, w_in_p, layer)
    y_gm = _gmlp(z, gm_v_gain, gm_w_s, gm_b_s)
    r, lw, k, v, a, b, g = _rwprep(z_rw, mu_pad, w_lora, rw_w0, rw_a0, rw_k_k, rw_k_a)
    y_rw = _rwscan(r, lw, k, v, a, b, g, rw_r_k, rw_gn_gain, rw_gn_bias)
    q, kf, vf = _foxprep(z, z_rw, f_bias_pad, fox_q_gain, fox_k_gain)
    y_fox = _fox(q, kf, vf)
    x1, h2, top_idx, gate, rank, counts = _merge(z, y_gm, y_rw, y_fox, x, gate1, scale2, shift2,
                                                 w_branch, w_o, w_router, b_router, layer)
    return _moe(x1, gate2, h2, top_idx, gate, rank, counts, w_gate_up, b_gate_up, w_down, b_down, layer)


def kernel(x, c, w_ada, b_ada, w_in, gm_v_gain, gm_w_s, gm_b_s, rw_mu, rw_w0, rw_w2, rw_a0, rw_a2, rw_g2, rw_k_k,
           rw_k_a, rw_r_k, rw_gn_gain, rw_gn_bias, fox_f_bias, fox_q_gain, fox_k_gain, w_branch, w_o, w_router,
           b_router, w_gate_up, b_gate_up, w_down, b_down):
    n_layer = w_ada.shape[0]
    bsz = x.shape[0]
    c_pad = jnp.zeros((8, D_MODEL), F32).at[:bsz].set(c)
    mod = _adaln(c_pad, w_ada, b_ada)[:, :bsz].reshape(n_layer, bsz, 6, D_MODEL)
    w_in_p = _permute_w_in(w_in)
    mu_pad = jnp.pad(rw_mu, ((0, 0), (0, RW_BLOCK - RW_SHIFT_WIDTH)))
    w_lora = jnp.zeros((n_layer, RW_LORA, 3 * RW_WIDTH), F32)
    w_lora = w_lora.at[:, 0:RW_DECAY_LORA, 0:RW_WIDTH].set(rw_w2)
    w_lora = w_lora.at[:, RW_DECAY_LORA:RW_DECAY_LORA + RW_ICLR_LORA, RW_WIDTH:2 * RW_WIDTH].set(rw_a2)
    w_lora = w_lora.at[:, RW_DECAY_LORA + RW_ICLR_LORA:, 2 * RW_WIDTH:].set(rw_g2)
    f_bias_pad = jnp.pad(fox_f_bias, ((0, 0), (0, LANES - FOX_HEADS)))
    w_in_p = w_in_p.reshape(n_layer * D_MODEL, Z_WIDTH)
    w_branch_b = w_branch.astype(BF16).reshape(n_layer * MIX_WIDTH, D_MODEL)
    w_o_b = w_o.astype(BF16).reshape(n_layer * D_MODEL, D_MODEL)
    w_gu = w_gate_up.reshape(n_layer * N_EXPERTS, D_MODEL, 2 * D_FF)
    b_gu = b_gate_up.reshape(n_layer * N_EXPERTS, 1, 2 * D_FF)
    w_dn = w_down.reshape(n_layer * N_EXPERTS, D_FF, D_MODEL)
    b_dn = b_down.reshape(n_layer * N_EXPERTS, 1, D_MODEL)
    for l in range(n_layer):
        x = _layer(x, mod[l], w_in_p, gm_v_gain[l], gm_w_s[l], gm_b_s[l], mu_pad[l:l + 1], w_lora[l], rw_w0[l],
                   rw_a0[l], rw_k_k[l], rw_k_a[l], rw_r_k[l], rw_gn_gain[l], rw_gn_bias[l], f_bias_pad[l:l + 1],
                   fox_q_gain[l], fox_k_gain[l], w_branch_b, w_o_b, w_router[l], b_router[l],
                   w_gu, b_gu, w_dn, b_dn, l)
    return x
```

```python
import functools

import jax
import jax.numpy as jnp
import numpy as np
from jax import lax
from jax.experimental import pallas as pl
from jax.experimental.pallas import tpu as pltpu
from jax.experimental.pallas import tpu_sc as plsc

F32 = jnp.float32
BF16 = jnp.bfloat16
I32 = jnp.int32
HIGHEST = lax.Precision.HIGHEST

D_MODEL = 1024
GM_CHUNK = 128
GM_GROUPS = 4
GM_WIDTH = 256
GM_GROUP_DIM = GM_WIDTH // GM_GROUPS
RW_HEADS = 4
RW_HEAD_DIM = 64
RW_WIDTH = RW_HEADS * RW_HEAD_DIM
RW_DECAY_LORA = 32
RW_ICLR_LORA = 32
RW_GATE_LORA = 64
RW_LORA = RW_DECAY_LORA + RW_ICLR_LORA + RW_GATE_LORA
RW_SHIFT_WIDTH = 3 * RW_WIDTH + RW_LORA
RW_GN_EPS = 64e-5
FOX_HEADS = 8
FOX_HEAD_DIM = 64
FOX_WIDTH = FOX_HEADS * FOX_HEAD_DIM
ATTN_SCALE = FOX_HEAD_DIM ** -0.5
MASK_VALUE = -1e30
LOG2E = 1.4426950408889634
N_BRANCH = 3
MIX_WIDTH = GM_WIDTH + RW_WIDTH + FOX_WIDTH
N_EXPERTS = 32
TOP_K = 4
D_FF = D_MODEL
SWIGLU_LIMIT = 7.0
SWIGLU_ALPHA = 1.702
MOE_BLOCK = 256
EPS = 1e-6

Z_GATE = 0
Z_FOX = N_BRANCH * D_MODEL
Z_GM = Z_FOX + 3 * FOX_WIDTH
Z_RW = Z_GM + 2 * GM_WIDTH
RW_BLOCK = 1024
Z_F = Z_RW + RW_SHIFT_WIDTH
Z_WIDTH = Z_RW + RW_BLOCK
LANES = 128
RW_CHUNK = 64
RW_PREP_UNROLL = 4

VMEM_LIMIT = 48 * 1024 * 1024
FFN_VMEM_LIMIT = 56 * 1024 * 1024
FFN_STEP_BLOCKS = 4
SC_CORES = 2
SC_SUBCORES = 16
SC_WORKERS = SC_CORES * SC_SUBCORES
SC_ROWS = 64


def _cparams(sem):
    return pltpu.CompilerParams(dimension_semantics=sem, vmem_limit_bytes=VMEM_LIMIT)


def _mm(a, b):
    return jnp.dot(a.astype(BF16), b.astype(BF16), preferred_element_type=F32)


def _mm_nt(a, b):
    return lax.dot_general(a.astype(BF16), b.astype(BF16), (((1,), (1,)), ((), ())), preferred_element_type=F32)


def _mm_tn(a, b):
    return lax.dot_general(a.astype(BF16), b.astype(BF16), (((0,), (0,)), ((), ())), preferred_element_type=F32)


def _split3(x):
    hi = x.astype(BF16)
    r1 = x - hi.astype(F32)
    mid = r1.astype(BF16)
    lo = (r1 - mid.astype(F32)).astype(BF16)
    return hi, mid, lo


def _tri_cumsum(x, n):
    ri = lax.broadcasted_iota(I32, (n, n), 0)
    ci = lax.broadcasted_iota(I32, (n, n), 1)
    ones = jnp.where(ri >= ci, 1.0, 0.0).astype(BF16)
    hi, mid, lo = _split3(x)
    return (jnp.dot(ones, hi, preferred_element_type=F32) + jnp.dot(ones, mid, preferred_element_type=F32)
            + jnp.dot(ones, lo, preferred_element_type=F32))


def _pack_halves(x):
    w = x.shape[1] // 2
    hi = pltpu.bitcast(x[:, :w].astype(BF16).astype(F32), jnp.uint32)
    lo = pltpu.bitcast(x[:, w:].astype(BF16).astype(F32), jnp.uint32)
    return pltpu.bitcast(hi | (lo >> 16), I32)


def _unpack_halves(p):
    u = pltpu.bitcast(p, jnp.uint32)
    return pltpu.bitcast(u & jnp.uint32(0xFFFF0000), F32), pltpu.bitcast(u << 16, F32)


def _log_sigmoid(x):
    return jnp.minimum(x, 0.0) - jnp.log1p(jnp.exp(-jnp.abs(x)))


def _adaln_kernel(c_ref, w_ref, b_ref, o_ref):
    c = c_ref[...]
    s = c * jax.nn.sigmoid(c)
    o_ref[0] = jnp.dot(s, w_ref[0], preferred_element_type=F32, precision=HIGHEST) + b_ref[0]


def _adaln(c_pad, w_ada, b_ada):
    n_layer, d, w6 = w_ada.shape
    tn = 1536
    return pl.pallas_call(
        _adaln_kernel,
        out_shape=jax.ShapeDtypeStruct((n_layer, c_pad.shape[0], w6), F32),
        grid=(n_layer, w6 // tn),
        in_specs=[pl.BlockSpec(c_pad.shape, lambda l, j: (0, 0)),
                  pl.BlockSpec((1, d, tn), lambda l, j: (l, 0, j)),
                  pl.BlockSpec((1, 1, tn), lambda l, j: (l, 0, j))],
        out_specs=pl.BlockSpec((1, c_pad.shape[0], tn), lambda l, j: (l, 0, j)),
        compiler_params=_cparams(("parallel", "parallel")),
        name="adaln",
    )(c_pad, w_ada, b_ada.reshape(n_layer, 1, w6))


def _inproj_kernel(x_ref, sc_ref, sh_ref, w_ref, zm_ref, zr_ref, xn_ref):
    j = pl.program_id(2)

    @pl.when(j == 0)
    def _():
        x = x_ref[0]
        xn = x * lax.rsqrt(jnp.mean(x * x, axis=-1, keepdims=True) + EPS)
        xn_ref[...] = (xn * (1.0 + sc_ref[0]) + sh_ref[0]).astype(BF16)

    acc = jnp.dot(xn_ref[...], w_ref[...], preferred_element_type=F32)

    @pl.when(j < Z_RW // RW_BLOCK)
    def _():
        zm_ref[0] = acc.astype(BF16)

    @pl.when(j == Z_RW // RW_BLOCK)
    def _():
        zr_ref[0] = acc


def _inproj(x, scale, shift, w, layer):
    bsz, t_len, d = x.shape
    tm = min(1024, t_len)
    tn = RW_BLOCK
    n_main = Z_RW // tn
    return pl.pallas_call(
        _inproj_kernel,
        out_shape=(jax.ShapeDtypeStruct((bsz, t_len, Z_RW), BF16), jax.ShapeDtypeStruct((bsz, t_len, RW_BLOCK), F32)),
        grid=(bsz, t_len // tm, Z_WIDTH // tn),
        in_specs=[pl.BlockSpec((1, tm, d), lambda b, i, j: (b, i, 0)),
                  pl.BlockSpec((1, 1, d), lambda b, i, j: (b, 0, 0)),
                  pl.BlockSpec((1, 1, d), lambda b, i, j: (b, 0, 0)),
                  pl.BlockSpec((d, tn), lambda b, i, j: (layer, j))],
        out_specs=(pl.BlockSpec((1, tm, tn), lambda b, i, j: (b, i, jnp.minimum(j, n_main - 1))),
                   pl.BlockSpec((1, tm, tn), lambda b, i, j: (b, i, 0))),
        scratch_shapes=[pltpu.VMEM((tm, d), BF16)],
        compiler_params=_cparams(("parallel", "parallel", "arbitrary")),
        name="inproj",
    )(x, scale, shift, w)


def _gmlp_kernel(z_ref, gain_ref, ws_ref, bst_ref, o_ref):
    tm = z_ref.shape[1]
    z = z_ref[0].astype(F32)
    u = jax.nn.gelu(z[:, :GM_WIDTH])
    v = jax.nn.gelu(z[:, GM_WIDTH:])
    v = v * lax.rsqrt(jnp.mean(v * v, axis=-1, keepdims=True) + EPS) * gain_ref[...]
    vb = v.astype(BF16)
    grp = lax.broadcasted_iota(I32, (GM_CHUNK, GM_WIDTH), 1) // GM_GROUP_DIM
    ri = lax.broadcasted_iota(I32, (GM_CHUNK, GM_CHUNK), 0)
    ci = lax.broadcasted_iota(I32, (GM_CHUNK, GM_CHUNK), 1)
    causal = ri >= ci
    bias = jnp.zeros((GM_CHUNK, GM_WIDTH), F32)
    ws = []
    for g in range(GM_GROUPS):
        ws.append(jnp.where(causal, ws_ref[g], 0.0).astype(BF16))
        bias = jnp.where(grp == g, bst_ref[:, g:g + 1], bias)
    for c in range(tm // GM_CHUNK):
        rows = slice(c * GM_CHUNK, (c + 1) * GM_CHUNK)
        vc = vb[rows]
        mixed = bias
        for g in range(GM_GROUPS):
            m = jnp.dot(ws[g], vc, preferred_element_type=F32)
            mixed = mixed + jnp.where(grp == g, m, 0.0)
        o_ref[0, rows, :] = (u[rows] * mixed).astype(o_ref.dtype)


def _gmlp(z, gain, w_s, b_s):
    bsz, t_len, _ = z.shape
    tm = min(1024, t_len)
    return pl.pallas_call(
        _gmlp_kernel,
        out_shape=jax.ShapeDtypeStruct((bsz, t_len, GM_WIDTH), BF16),
        grid=(bsz, t_len // tm),
        in_specs=[pl.BlockSpec((1, tm, 2 * GM_WIDTH), lambda b, i: (b, i, Z_GM // (2 * GM_WIDTH))),
                  pl.BlockSpec((1, GM_WIDTH), lambda b, i: (0, 0)),
                  pl.BlockSpec((GM_GROUPS, GM_CHUNK, GM_CHUNK), lambda b, i: (0, 0, 0)),
                  pl.BlockSpec((GM_CHUNK, GM_GROUPS), lambda b, i: (0, 0))],
        out_specs=pl.BlockSpec((1, tm, GM_WIDTH), lambda b, i: (b, i, 0)),
        compiler_params=_cparams(("parallel", "parallel")),
        name="gmlp",
    )(z, gain.reshape(1, GM_WIDTH), w_s, b_s.T)


def _rwprep_kernel(z_ref, zp_ref, mu_ref, wl_ref, w0_ref, a0_ref, kk_ref, ka_ref,
                   r_o, lw_o, k_o, v_o, a_o, b_o, g_o):
    tm = z_ref.shape[1]
    z = z_ref[0]
    prev = jnp.where(pl.program_id(1) > 0, zp_ref[0, 7:8, :], 0.0)
    rowid = lax.broadcasted_iota(I32, z.shape, 0)
    zs = jnp.where(rowid == 0, prev, pltpu.roll(z, 1, axis=0))
    zz = z + mu_ref[...] * (zs - z)
    r = zz[:, 0:RW_WIDTH]
    k = zz[:, RW_WIDTH:2 * RW_WIDTH]
    v = zz[:, 2 * RW_WIDTH:3 * RW_WIDTH]
    lo = zz[:, 3 * RW_WIDTH:3 * RW_WIDTH + RW_LORA]
    lane = lax.broadcasted_iota(I32, (tm, RW_LORA), 1)
    act = jnp.where(lane < RW_DECAY_LORA, jnp.tanh(lo),
                    jnp.where(lane < RW_DECAY_LORA + RW_ICLR_LORA, lo, jax.nn.sigmoid(lo)))
    a_hi, a_lo, _ = _split3(act)
    w_hi, w_lo, _ = _split3(wl_ref[...])
    proj = (jnp.dot(a_hi, w_hi, preferred_element_type=F32) + jnp.dot(a_hi, w_lo, preferred_element_type=F32)
            + jnp.dot(a_lo, w_hi, preferred_element_type=F32))
    xw = -(w0_ref[...] + proj[:, 0:RW_WIDTH])
    softplus = jnp.maximum(xw, 0.0) + jnp.log1p(jnp.exp(-jnp.abs(xw)))
    lw = -jnp.exp(-softplus - 0.5)
    a = jax.nn.sigmoid(a0_ref[...] + proj[:, RW_WIDTH:2 * RW_WIDTH])
    g = proj[:, 2 * RW_WIDTH:3 * RW_WIDTH]
    kk = k * kk_ref[...]
    k2 = k * (1.0 + (a - 1.0) * ka_ref[...])
    for h in range(RW_HEADS):
        sl = slice(h * RW_HEAD_DIM, (h + 1) * RW_HEAD_DIM)
        kkh = kk[:, sl]
        nrm = jnp.sqrt(jnp.sum(kkh * kkh, axis=-1, keepdims=True))
        kkh = kkh / jnp.maximum(nrm, 1e-12)
        r_o[0, h] = r[:, sl]
        lw_o[0, h] = lw[:, sl]
        k_o[0, h] = k2[:, sl]
        v_o[0, h] = v[:, sl]
        a_o[0, h] = -kkh
        b_o[0, h] = kkh * a[:, sl]
        g_o[0, h] = g[:, sl]


def _rwprep(z, mu_pad, w_lora, w0, a0, k_k, k_a):
    bsz, t_len, _ = z.shape
    tm = min(1024, t_len)
    hm = jax.ShapeDtypeStruct((bsz, RW_HEADS, t_len, RW_HEAD_DIM), F32)
    hm_spec = pl.BlockSpec((1, RW_HEADS, tm, RW_HEAD_DIM), lambda b, i: (b, 0, i, 0))
    vec = lambda n: pl.BlockSpec((1, n), lambda b, i: (0, 0))
    rw_blk = 0
    return pl.pallas_call(
        _rwprep_kernel,
        out_shape=(hm,) * 7,
        grid=(bsz, t_len // tm),
        in_specs=[pl.BlockSpec((1, tm, RW_BLOCK), lambda b, i: (b, i, rw_blk)),
                  pl.BlockSpec((1, 8, RW_BLOCK), lambda b, i: (b, jnp.maximum(i * (tm // 8) - 1, 0), rw_blk)),
                  vec(RW_BLOCK),
                  pl.BlockSpec((RW_LORA, 3 * RW_WIDTH), lambda b, i: (0, 0)),
                  vec(RW_WIDTH), vec(RW_WIDTH), vec(RW_WIDTH), vec(RW_WIDTH)],
        out_specs=(hm_spec,) * 7,
        compiler_params=_cparams(("parallel", "parallel")),
        name="rwprep",
    )(z, z, mu_pad, w_lora, w0.reshape(1, -1), a0.reshape(1, -1), k_k.reshape(1, -1), k_a.reshape(1, -1))


def _rwscan_kernel(r_ref, lw_ref, k_ref, v_ref, a_ref, b_ref, g_ref, rk_ref, gg_ref, gb_ref, o_ref,
                   s_ref, rp_ref, y_ref, gm_ref, h0_ref, we_ref):
    cl = RW_CHUNK
    tb = r_ref.shape[2]
    n_chunk = tb // cl

    @pl.when(pl.program_id(1) == 0)
    def _():
        s_ref[...] = jnp.zeros_like(s_ref)

    n = RW_HEADS * cl
    ri = lax.broadcasted_iota(I32, (n, n), 0)
    ci = lax.broadcasted_iota(I32, (n, n), 1)
    same_head = (ri // cl) == (ci // cl)
    lower = same_head & (ri >= ci)
    strict = same_head & (ri > ci)
    eye = jnp.where(ri == ci, 1.0, 0.0)
    ones_lower = jnp.where(lower, 1.0, 0.0).astype(BF16)

    def prepare(chunks):
        grp = range(len(chunks))
        each = lambda fn: [fn(u) for u in grp]
        rows = [pl.ds(pl.multiple_of(c * cl, cl), cl) for c in chunks]
        stack = lambda ref: each(lambda u: ref[0, :, rows[u], :].reshape(n, RW_HEAD_DIM))
        r, lw, k, v, a, b = (stack(ref) for ref in (r_ref, lw_ref, k_ref, v_ref, a_ref, b_ref))
        hd = RW_HEAD_DIM
        parts = each(lambda u: jnp.concatenate(_split3(lw[u]), axis=-1))
        sums = each(lambda u: jnp.dot(ones_lower, parts[u], preferred_element_type=F32))
        cw = each(lambda u: sums[u][:, :hd] + sums[u][:, hd:2 * hd] + sums[u][:, 2 * hd:])
        w_in = each(lambda u: jnp.exp(cw[u]))
        w_inv = each(lambda u: jnp.exp(-cw[u]))
        rt = each(lambda u: r[u] * w_in[u])
        at = each(lambda u: a[u] * jnp.exp(cw[u] - lw[u]))
        kt = each(lambda u: k[u] * w_inv[u])
        bt = each(lambda u: b[u] * w_inv[u])
        w_end = each(lambda u: w_in[u].reshape(RW_HEADS, cl, RW_HEAD_DIM)[:, cl - 1:cl, :])
        w_end_rows = each(lambda u: jnp.broadcast_to(w_end[u], (RW_HEADS, cl, RW_HEAD_DIM)).reshape(n, RW_HEAD_DIM))
        a_ab = each(lambda u: jnp.where(strict, _mm_nt(at[u], bt[u]), 0.0))
        a_ak = each(lambda u: jnp.where(strict, _mm_nt(at[u], kt[u]), 0.0))
        m_rb = each(lambda u: jnp.where(lower, _mm_nt(rt[u], bt[u]), 0.0))
        m_rk = each(lambda u: jnp.where(lower, _mm_nt(rt[u], kt[u]), 0.0))
        inv = each(lambda u: eye + a_ab[u])
        p = a_ab
        for _ in range(cl.bit_length() - 2):
            p = [_mm(p[u], p[u]) for u in grp]
            inv = [inv[u] + _mm(inv[u], p[u]) for u in grp]
        akv = each(lambda u: _mm(a_ak[u], v[u]))
        apz = each(lambda u: _mm(inv[u], jnp.concatenate([at[u], akv[u]], axis=-1)).astype(BF16))
        mix = each(lambda u: jnp.dot(m_rb[u].astype(BF16), apz[u], preferred_element_type=F32))
        bend = each(lambda u: bt[u] * w_end_rows[u])
        kend = each(lambda u: kt[u] * w_end_rows[u])
        rp = each(lambda u: (rt[u] + mix[u][:, :hd]).astype(BF16))
        y0 = each(lambda u: mix[u][:, hd:] + _mm(m_rk[u], v[u]))
        for u in grp:
            for h in range(RW_HEADS):
                hs = slice(h * cl, (h + 1) * cl)
                both = _mm_tn(apz[u][hs], bend[u][hs])
                rp_ref[h, rows[u], :] = rp[u][hs]
                y_ref[h, rows[u], :] = y0[u][hs]
                gm_ref[h, rows[u], :] = both[:hd].astype(BF16)
                h0_ref[h, rows[u], :] = both[hd:] + _mm_tn(v[u][hs], kend[u][hs])
                we_ref[h, chunks[u]] = w_end[u][h]

    def prepare_step(i, carry):
        prepare([i * RW_PREP_UNROLL + u for u in range(RW_PREP_UNROLL)])
        return carry

    lax.fori_loop(0, n_chunk // RW_PREP_UNROLL, prepare_step, 0)

    def advance(c, carry):
        rows = pl.ds(pl.multiple_of(c * cl, cl), cl)
        for h in range(RW_HEADS):
            s = s_ref[h]
            sb = s.astype(BF16)
            y_ref[h, rows, :] = y_ref[h, rows, :] + lax.dot_general(
                rp_ref[h, rows, :], sb, (((1,), (1,)), ((), ())), preferred_element_type=F32)
            s_ref[h] = (s * we_ref[h, c] + jnp.dot(sb, gm_ref[h, rows, :], preferred_element_type=F32)
                        + h0_ref[h, rows, :])
        return carry

    lax.fori_loop(0, n_chunk, advance, 0)

    for h in range(RW_HEADS):
        y = y_ref[h]
        mu = jnp.mean(y, axis=-1, keepdims=True)
        yc = y - mu
        var = jnp.mean(yc * yc, axis=-1, keepdims=True)
        yn = yc * lax.rsqrt(var + RW_GN_EPS) * gg_ref[h] + gb_ref[h]
        v = v_ref[0, h]
        bonus = jnp.sum(r_ref[0, h] * k_ref[0, h] * rk_ref[h], axis=-1, keepdims=True) * v
        o_ref[0, h] = ((yn + bonus) * g_ref[0, h]).astype(o_ref.dtype)


def _rwscan(r, lw, k, v, a, b, g, r_k, gn_gain, gn_bias):
    bsz, _, t_len, _ = r.shape
    tb = min(512, t_len)
    hm_spec = pl.BlockSpec((1, RW_HEADS, tb, RW_HEAD_DIM), lambda bi, i: (bi, 0, i, 0))
    par = pl.BlockSpec((RW_HEADS, 1, RW_HEAD_DIM), lambda bi, i: (0, 0, 0))
    hshape = (RW_HEADS, 1, RW_HEAD_DIM)
    return pl.pallas_call(
        _rwscan_kernel,
        out_shape=jax.ShapeDtypeStruct((bsz, RW_HEADS, t_len, RW_HEAD_DIM), BF16),
        grid=(bsz, t_len // tb),
        in_specs=[hm_spec] * 7 + [par] * 3,
        out_specs=hm_spec,
        scratch_shapes=[pltpu.VMEM((RW_HEADS, RW_HEAD_DIM, RW_HEAD_DIM), F32),
                        pltpu.VMEM((RW_HEADS, tb, RW_HEAD_DIM), BF16), pltpu.VMEM((RW_HEADS, tb, RW_HEAD_DIM), F32),
                        pltpu.VMEM((RW_HEADS, tb, RW_HEAD_DIM), BF16), pltpu.VMEM((RW_HEADS, tb, RW_HEAD_DIM), F32),
                        pltpu.VMEM((RW_HEADS, tb // RW_CHUNK, 1, RW_HEAD_DIM), F32)],
        compiler_params=_cparams(("parallel", "arbitrary")),
        name="rwscan",
    )(r, lw, k, v, a, b, g, r_k.reshape(hshape), gn_gain.reshape(hshape), gn_bias.reshape(hshape))


FOX_PAIRS = FOX_HEADS // 2
FOX_EXTRA = 3
FOX_ACC_ROWS = FOX_HEAD_DIM + 16
FOX_LOOKAHEAD = 2


def _fox_bias_selector():
    sel = np.zeros((LANES, 2 * FOX_HEADS * LANES), np.float32)
    for h in range(FOX_HEADS):
        base = FOX_HEAD_DIM if h % 2 == 0 else 0
        for p in range(FOX_EXTRA):
            sel[p * FOX_HEADS + h, h * LANES + base + p] = 1.0
            sel[p * FOX_HEADS + h, (FOX_HEADS + h) * LANES + base + FOX_EXTRA + p] = -1.0
    return sel


def _foxprep_kernel(z_ref, f_ref, fb_ref, qg_ref, kg_ref, sel_ref, q_o, k_o, vt_o, carry_ref):
    tm = z_ref.shape[1]

    @pl.when(pl.program_id(1) == 0)
    def _():
        carry_ref[...] = jnp.zeros_like(carry_ref)

    log_f = _log_sigmoid(f_ref[0] + fb_ref[...])
    cum = carry_ref[...] + _tri_cumsum(log_f, tm)
    carry_ref[...] = cum[tm - 1:tm, :]
    lane = lax.broadcasted_iota(I32, (tm, LANES), 1)
    hi, mid, lo = (p.astype(F32) for p in _split3(cum * LOG2E))
    packed = jnp.where(lane < FOX_HEADS, hi,
                       jnp.where(lane < 2 * FOX_HEADS, pltpu.roll(mid, FOX_HEADS, axis=1),
                                 pltpu.roll(lo, 2 * FOX_HEADS, axis=1)))
    packed = jnp.where(lane < FOX_EXTRA * FOX_HEADS, packed, 0.0).astype(BF16)
    extra = jnp.dot(packed, sel_ref[...], preferred_element_type=F32)

    left = lane < FOX_HEAD_DIM
    in_half = lane % FOX_HEAD_DIM
    ones_q = jnp.where((in_half >= FOX_EXTRA) & (in_half < 2 * FOX_EXTRA), 1.0, 0.0)
    ones_k = jnp.where(in_half < FOX_EXTRA, 1.0, 0.0)

    def normed(block, gain):
        sq = block * block
        s_left = jnp.sum(jnp.where(left, sq, 0.0), axis=-1, keepdims=True)
        s_right = jnp.sum(jnp.where(left, 0.0, sq), axis=-1, keepdims=True)
        ms = jnp.where(left, s_left, s_right) * (1.0 / FOX_HEAD_DIM)
        return block * lax.rsqrt(ms + EPS) * gain

    for j in range(FOX_PAIRS):
        qn = normed(z_ref[0, :, j * LANES:(j + 1) * LANES].astype(F32), qg_ref[...] * (ATTN_SCALE * LOG2E))
        kn = normed(z_ref[0, :, FOX_WIDTH + j * LANES:FOX_WIDTH + (j + 1) * LANES].astype(F32), kg_ref[...])
        for par in range(2):
            h = 2 * j + par
            own = left if par == 0 else jnp.logical_not(left)
            q_o[0, h] = jnp.where(own, qn, extra[:, h * LANES:(h + 1) * LANES] + ones_q).astype(BF16)
            k_o[0, h] = jnp.where(own, kn, extra[:, (FOX_HEADS + h) * LANES:(FOX_HEADS + h + 1) * LANES]
                                  + ones_k).astype(BF16)
    ri = lax.broadcasted_iota(I32, (FOX_WIDTH, FOX_WIDTH), 0)
    ci = lax.broadcasted_iota(I32, (FOX_WIDTH, FOX_WIDTH), 1)
    eye = jnp.where(ri == ci, 1.0, 0.0).astype(BF16)
    v = z_ref[0, :, 2 * FOX_WIDTH:3 * FOX_WIDTH].astype(BF16)
    vt_o[0] = lax.dot_general(eye, v, (((1,), (1,)), ((), ())), preferred_element_type=F32).astype(BF16)


def _foxprep(z, z_rw, f_bias_pad, q_gain, k_gain):
    bsz, t_len, _ = z.shape
    tm = min(512, t_len)
    qk = jax.ShapeDtypeStruct((bsz, FOX_HEADS, t_len, LANES), BF16)
    qk_spec = pl.BlockSpec((1, FOX_HEADS, tm, LANES), lambda b, i: (b, 0, i, 0))
    sel = jnp.asarray(_fox_bias_selector(), BF16)
    return pl.pallas_call(
        _foxprep_kernel,
        out_shape=(qk, qk, jax.ShapeDtypeStruct((bsz, FOX_WIDTH, t_len), BF16)),
        grid=(bsz, t_len // tm),
        in_specs=[pl.BlockSpec((1, tm, 3 * FOX_WIDTH), lambda b, i: (b, i, Z_FOX // (3 * FOX_WIDTH))),
                  pl.BlockSpec((1, tm, LANES), lambda b, i: (b, i, (Z_F - Z_RW) // LANES)),
                  pl.BlockSpec((1, LANES), lambda b, i: (0, 0)),
                  pl.BlockSpec((1, LANES), lambda b, i: (0, 0)),
                  pl.BlockSpec((1, LANES), lambda b, i: (0, 0)),
                  pl.BlockSpec(sel.shape, lambda b, i: (0, 0))],
        out_specs=(qk_spec, qk_spec, pl.BlockSpec((1, FOX_WIDTH, tm), lambda b, i: (b, 0, i))),
        scratch_shapes=[pltpu.VMEM((1, LANES), F32)],
        compiler_params=_cparams(("parallel", "arbitrary")),
        name="foxprep",
    )(z, z_rw, f_bias_pad, jnp.tile(q_gain.reshape(1, -1), (1, 2)), jnp.tile(k_gain.reshape(1, -1), (1, 2)), sel)


def _fox_kernel(qi_ref, kj_ref, q_ref, k_ref, vt_ref, o_ref, m_ref, acc_ref):
    i = qi_ref[pl.program_id(1)]
    j = kj_ref[pl.program_id(1)]
    tq = q_ref.shape[2]
    tk = k_ref.shape[2]
    sub = 8

    @pl.when(j == 0)
    def _():
        m_ref[...] = jnp.full_like(m_ref, MASK_VALUE)
        acc_ref[...] = jnp.zeros_like(acc_ref)

    ones_rows = jnp.ones((FOX_ACC_ROWS - FOX_HEAD_DIM, tk), BF16)

    def scores(h):
        return lax.dot_general(k_ref[0, h], q_ref[0, h], (((1,), (1,)), ((), ())), preferred_element_type=F32)

    def update(diagonal):
        if diagonal:
            key = lax.broadcasted_iota(I32, (tk, tq), 0)
            qry = lax.broadcasted_iota(I32, (tk, tq), 1)
            keep = key <= qry
        ahead = [scores(h) for h in range(FOX_LOOKAHEAD)]
        for h in range(FOX_HEADS):
            s = ahead.pop(0)
            if h + FOX_LOOKAHEAD < FOX_HEADS:
                ahead.append(scores(h + FOX_LOOKAHEAD))
            if diagonal:
                s = jnp.where(keep, s, MASK_VALUE)
            s3 = s.reshape(tk // sub, sub, tq)
            m_prev = m_ref[h]
            m_cur = jnp.max(jnp.max(s3, axis=0), axis=0, keepdims=True)
            m_new = jnp.maximum(m_prev, m_cur)
            alpha = jnp.exp2(m_prev - m_new)
            p = jnp.exp2(s3 - m_new[None]).astype(BF16).reshape(tk, tq)
            lhs = jnp.concatenate([vt_ref[0, h * FOX_HEAD_DIM:(h + 1) * FOX_HEAD_DIM, :], ones_rows], axis=0)
            pv = jnp.dot(lhs, p, preferred_element_type=F32)
            acc = acc_ref[h].reshape(FOX_ACC_ROWS // sub, sub, tq) * alpha[None]
            acc_ref[h] = acc.reshape(FOX_ACC_ROWS, tq) + pv
            m_ref[h] = m_new

    @pl.when(j < i)
    def _():
        update(False)

    @pl.when(j == i)
    def _():
        update(True)
        outs = []
        for h in range(FOX_HEADS):
            acc = acc_ref[h]
            outs.append((acc[:FOX_HEAD_DIM] / acc[FOX_HEAD_DIM:FOX_HEAD_DIM + 1]).astype(BF16))
        out_t = jnp.concatenate(outs, axis=0)
        ri = lax.broadcasted_iota(I32, (tq, tq), 0)
        ci = lax.broadcasted_iota(I32, (tq, tq), 1)
        eye = jnp.where(ri == ci, 1.0, 0.0).astype(BF16)
        o_ref[0] = lax.dot_general(eye, out_t, (((1,), (1,)), ((), ())),
                                   preferred_element_type=F32).astype(o_ref.dtype)


def _fox(q, k, vt):
    bsz, _, t_len, _ = q.shape
    tq = min(512, t_len)
    n_blk = t_len // tq
    pairs = [(i, j) for i in range(n_blk) for j in range(i + 1)]
    qi = jnp.asarray([p[0] for p in pairs], I32)
    kj = jnp.asarray([p[1] for p in pairs], I32)
    grid_spec = pltpu.PrefetchScalarGridSpec(
        num_scalar_prefetch=2,
        grid=(bsz, len(pairs)),
        in_specs=[pl.BlockSpec((1, FOX_HEADS, tq, LANES), lambda b, s, qi, kj: (b, 0, qi[s], 0)),
                  pl.BlockSpec((1, FOX_HEADS, tq, LANES), lambda b, s, qi, kj: (b, 0, kj[s], 0)),
                  pl.BlockSpec((1, FOX_WIDTH, tq), lambda b, s, qi, kj: (b, 0, kj[s]))],
        out_specs=pl.BlockSpec((1, tq, FOX_WIDTH), lambda b, s, qi, kj: (b, qi[s], 0)),
        scratch_shapes=[pltpu.VMEM((FOX_HEADS, 8, tq), F32), pltpu.VMEM((FOX_HEADS, FOX_ACC_ROWS, tq), F32)],
    )
    return pl.pallas_call(
        _fox_kernel,
        out_shape=jax.ShapeDtypeStruct((bsz, t_len, FOX_WIDTH), BF16),
        grid_spec=grid_spec,
        compiler_params=_cparams(("parallel", "arbitrary")),
        name="fox",
    )(qi, kj, q, k, vt)


def _merge_kernel(zg_ref, ygm_ref, yrw_ref, yfox_ref, x_ref, g1_ref, sc2_ref, sh2_ref, pb_ref, wo_ref, wr_ref, br_ref,
                  x1_o, h2_o, idx_o, gate_o, rank_o, cnt_o, carry_ref):
    tm = x_ref.shape[1]

    @pl.when((pl.program_id(0) == 0) & (pl.program_id(1) == 0))
    def _():
        carry_ref[...] = jnp.zeros_like(carry_ref)

    sg = 0.5 * jnp.tanh(0.5 * zg_ref[0].astype(F32)) + 0.5
    p_gm = jnp.dot(ygm_ref[0], pb_ref[0:GM_WIDTH, :], preferred_element_type=F32)
    y_rw = jnp.concatenate([yrw_ref[0, h] for h in range(RW_HEADS)], axis=-1)
    p_rw = jnp.dot(y_rw, pb_ref[GM_WIDTH:GM_WIDTH + RW_WIDTH, :], preferred_element_type=F32)
    p_fox = jnp.dot(yfox_ref[0], pb_ref[GM_WIDTH + RW_WIDTH:, :], preferred_element_type=F32)
    merged = sg[:, 0:D_MODEL] * p_gm + sg[:, D_MODEL:2 * D_MODEL] * p_rw + sg[:, 2 * D_MODEL:] * p_fox
    x1 = x_ref[0] + g1_ref[0] * jnp.dot(merged.astype(BF16), wo_ref[...], preferred_element_type=F32)
    x1_o[0] = x1
    h2 = x1 * lax.rsqrt(jnp.mean(x1 * x1, axis=-1, keepdims=True) + EPS) * (1.0 + sc2_ref[0]) + sh2_ref[0]
    h2_o[0] = _pack_halves(h2)

    h_hi, h_lo, _ = _split3(h2)
    w_hi, w_lo, _ = _split3(wr_ref[...])
    logits = (jnp.dot(h_hi, w_hi, preferred_element_type=F32) + jnp.dot(h_hi, w_lo, preferred_element_type=F32)
              + jnp.dot(h_lo, w_hi, preferred_element_type=F32)) + br_ref[...]
    lane = lax.broadcasted_iota(I32, (tm, N_EXPERTS), 1)
    vals, idxs = [], []
    rest = logits
    for _ in range(TOP_K):
        m = jnp.max(rest, axis=-1, keepdims=True)
        am = jnp.min(jnp.where(rest == m, lane, N_EXPERTS), axis=-1, keepdims=True)
        vals.append(m)
        idxs.append(am)
        rest = jnp.where(lane == am, -jnp.inf, rest)
    exps = [jnp.exp(val - vals[0]) for val in vals]
    denom = exps[0] + exps[1] + exps[2] + exps[3]

    onehot = jnp.zeros((tm, N_EXPERTS), F32)
    for am in idxs:
        onehot = onehot + jnp.where(lane == am, 1.0, 0.0)
    ri = lax.broadcasted_iota(I32, (tm, tm), 0)
    ci = lax.broadcasted_iota(I32, (tm, tm), 1)
    before = jnp.where(ri > ci, 1.0, 0.0).astype(BF16)
    seen = carry_ref[...] + jnp.dot(before, onehot.astype(BF16), preferred_element_type=F32)
    lane_k = lax.broadcasted_iota(I32, (tm, TOP_K), 1)
    idx_out = jnp.zeros((tm, TOP_K), I32)
    gate_out = jnp.zeros((tm, TOP_K), F32)
    rank_out = jnp.zeros((tm, TOP_K), I32)
    for kk in range(TOP_K):
        rank = jnp.sum(jnp.where(lane == idxs[kk], seen, 0.0), axis=-1, keepdims=True).astype(I32)
        idx_out = jnp.where(lane_k == kk, idxs[kk], idx_out)
        gate_out = jnp.where(lane_k == kk, exps[kk] / denom, gate_out)
        rank_out = jnp.where(lane_k == kk, rank, rank_out)
    idx_o[0] = idx_out
    gate_o[0] = gate_out
    rank_o[0] = rank_out
    total = carry_ref[...] + jnp.sum(onehot, axis=0, keepdims=True)
    carry_ref[...] = total
    cnt_o[...] = total.astype(I32)


def _merge(z, y_gm, y_rw, y_fox, x, gate1, scale2, shift2, w_branch, w_o, w_router, b_router, layer):
    bsz, t_len, d = x.shape
    tm = min(512, t_len)
    row = lambda w: pl.BlockSpec((1, tm, w), lambda b, i: (b, i, 0))
    mod = pl.BlockSpec((1, 1, d), lambda b, i: (b, 0, 0))
    full = lambda shape: pl.BlockSpec(shape, lambda b, i: (0,) * len(shape))
    return pl.pallas_call(
        _merge_kernel,
        out_shape=(jax.ShapeDtypeStruct((bsz, t_len, d), F32), jax.ShapeDtypeStruct((bsz, t_len, d // 2), I32),
                   jax.ShapeDtypeStruct((bsz, t_len, TOP_K), I32), jax.ShapeDtypeStruct((bsz, t_len, TOP_K), F32),
                   jax.ShapeDtypeStruct((bsz, t_len, TOP_K), I32), jax.ShapeDtypeStruct((1, N_EXPERTS), I32)),
        grid=(bsz, t_len // tm),
        in_specs=[row(N_BRANCH * D_MODEL), row(GM_WIDTH),
                  pl.BlockSpec((1, RW_HEADS, tm, RW_HEAD_DIM), lambda b, i: (b, 0, i, 0)),
                  row(FOX_WIDTH), row(d), mod, mod, mod,
                  pl.BlockSpec((MIX_WIDTH, d), lambda b, i: (layer, 0)), pl.BlockSpec((d, d), lambda b, i: (layer, 0)),
                  full(w_router.shape), full((1, N_EXPERTS))],
        out_specs=(row(d), row(d // 2), row(TOP_K), row(TOP_K), row(TOP_K), full((1, N_EXPERTS))),
        scratch_shapes=[pltpu.VMEM((1, N_EXPERTS), F32)],
        compiler_params=_cparams(("arbitrary", "arbitrary")),
        name="merge_router",
    )(z, y_gm, y_rw, y_fox, x, gate1, scale2, shift2, w_branch, w_o, w_router, b_router.reshape(1, N_EXPERTS))


def _sc_mesh():
    return plsc.VectorSubcoreMesh(core_axis_name="c", subcore_axis_name="s",
                                  num_cores=SC_CORES, num_subcores=SC_SUBCORES)


def _sc_worker():
    return lax.axis_index("s") * SC_CORES + lax.axis_index("c")


def _sc_scatter_rows(src, idx3, n_out):
    _, d = src.shape
    n_copy, n_grp, _ = idx3.shape
    grp_per_w = n_grp // SC_WORKERS
    assert grp_per_w % 2 == 0

    def body(src_hbm, idx_hbm, out_hbm, idx_v, rows_a, rows_b, sem):
        g0 = _sc_worker() * grp_per_w
        for q in range(n_copy):
            pltpu.sync_copy(idx_hbm.at[q, pl.ds(g0, grp_per_w)], idx_v.at[pl.ds(q * grp_per_w, grp_per_w)])

        @pl.loop(0, grp_per_w, step=2)
        def _(j):
            read_a = pltpu.async_copy(src_hbm.at[pl.ds((g0 + j) * SC_ROWS, SC_ROWS)], rows_a, sem.at[0])
            read_b = pltpu.async_copy(src_hbm.at[pl.ds((g0 + j + 1) * SC_ROWS, SC_ROWS)], rows_b, sem.at[1])
            read_a.wait()
            put_a = [pltpu.async_copy(rows_a, out_hbm.at[idx_v.at[q * grp_per_w + j]], sem.at[2])
                     for q in range(n_copy)]
            read_b.wait()
            put_b = [pltpu.async_copy(rows_b, out_hbm.at[idx_v.at[q * grp_per_w + j + 1]], sem.at[3])
                     for q in range(n_copy)]
            for cp in put_a + put_b:
                cp.wait()

    return pl.kernel(
        body, out_type=jax.ShapeDtypeStruct((n_out, d), src.dtype), mesh=_sc_mesh(),
        scratch_types=[pltpu.VMEM((n_copy * grp_per_w, SC_ROWS), I32), pltpu.VMEM((SC_ROWS, d), src.dtype),
                       pltpu.VMEM((SC_ROWS, d), src.dtype), pltpu.SemaphoreType.DMA((4,))],
        name="sc_dispatch",
    )(src, idx3)


def _sc_gather_rows(table, idx2):
    _, d = table.shape
    n_grp, _ = idx2.shape
    grp_per_w = n_grp // SC_WORKERS
    assert grp_per_w % 2 == 0

    def body(table_hbm, idx_hbm, out_hbm, idx_v, rows_a, rows_b, sem):
        g0 = _sc_worker() * grp_per_w
        pltpu.sync_copy(idx_hbm.at[pl.ds(g0, grp_per_w)], idx_v)

        @pl.loop(0, grp_per_w, step=2)
        def _(j):
            get_a = pltpu.async_copy(table_hbm.at[idx_v.at[j]], rows_a, sem.at[0])
            get_b = pltpu.async_copy(table_hbm.at[idx_v.at[j + 1]], rows_b, sem.at[1])
            get_a.wait()
            put_a = pltpu.async_copy(rows_a, out_hbm.at[pl.ds((g0 + j) * SC_ROWS, SC_ROWS)], sem.at[2])
            get_b.wait()
            put_b = pltpu.async_copy(rows_b, out_hbm.at[pl.ds((g0 + j + 1) * SC_ROWS, SC_ROWS)], sem.at[3])
            put_a.wait()
            put_b.wait()

    return pl.kernel(
        body, out_type=jax.ShapeDtypeStruct((n_grp * SC_ROWS, d), table.dtype), mesh=_sc_mesh(),
        scratch_types=[pltpu.VMEM((grp_per_w, SC_ROWS), I32), pltpu.VMEM((SC_ROWS, d), table.dtype),
                       pltpu.VMEM((SC_ROWS, d), table.dtype), pltpu.SemaphoreType.DMA((4,))],
        name="sc_combine_gather",
    )(table, idx2)


def _ffn_weight_copies(expert, wgu_hbm, wd_hbm, stage_gu, stage_d, sem):
    return (pltpu.make_async_copy(wgu_hbm.at[expert], stage_gu, sem.at[0]),
            pltpu.make_async_copy(wd_hbm.at[expert], stage_d, sem.at[1]))


def _ffn_kernel(be_ref, first_ref, nxt_ref, nv_ref, x_ref, wgu_hbm, wd_hbm, bgu_ref, bd_ref, o_ref,
                stage_gu, stage_d, wgu_b, wd_b, sem):
    step = pl.program_id(0)
    copies = functools.partial(_ffn_weight_copies, wgu_hbm=wgu_hbm, wd_hbm=wd_hbm, stage_gu=stage_gu,
                               stage_d=stage_d, sem=sem)

    @pl.when(step == 0)
    def _():
        for cp in copies(be_ref[0]):
            cp.start()

    for b in range(FFN_STEP_BLOCKS):
        idx = step * FFN_STEP_BLOCKS + b
        expert = be_ref[idx]
        n_valid = nv_ref[idx]
        rows = slice(b * MOE_BLOCK, (b + 1) * MOE_BLOCK)

        @pl.when(first_ref[idx] == 1)
        def _():
            for cp in copies(expert):
                cp.wait()
            wgu_b[...] = stage_gu[...].astype(BF16)
            wd_b[...] = stage_d[...].astype(BF16)

            @pl.when(nxt_ref[idx] >= 0)
            def _():
                for cp in copies(nxt_ref[idx]):
                    cp.start()

        @pl.when(n_valid > 0)
        def _():
            rowid = lax.broadcasted_iota(I32, (MOE_BLOCK, x_ref.shape[1]), 0)
            xp = jnp.where(rowid < n_valid, x_ref[rows, :], 0)
            x = jnp.concatenate(_unpack_halves(xp), axis=-1).astype(BF16)
            gu = jnp.dot(x, wgu_b[...], preferred_element_type=F32) + bgu_ref[expert]
            g_ = jnp.minimum(gu[:, :D_FF], SWIGLU_LIMIT)
            u_ = jnp.clip(gu[:, D_FF:], -SWIGLU_LIMIT, SWIGLU_LIMIT)
            act = (u_ + 1.0) * (g_ * jax.nn.sigmoid(SWIGLU_ALPHA * g_))
            y = jnp.dot(act.astype(BF16), wd_b[...], preferred_element_type=F32) + bd_ref[expert]
            o_ref[rows, :] = _pack_halves(y)

        @pl.when(n_valid <= 0)
        def _():
            o_ref[rows, :] = jnp.zeros((MOE_BLOCK, o_ref.shape[1]), o_ref.dtype)


def _ffn(block_expert, block_first, block_next, block_valid, xin, w_gate_up, b_gate_up, w_down, b_down):
    n_rows, dp = xin.shape
    d = 2 * dp
    step_rows = FFN_STEP_BLOCKS * MOE_BLOCK
    resident = lambda arr: pl.BlockSpec(arr.shape, lambda i, *_: (0,) * arr.ndim)
    grid_spec = pltpu.PrefetchScalarGridSpec(
        num_scalar_prefetch=4,
        grid=(n_rows // step_rows,),
        in_specs=[pl.BlockSpec((step_rows, dp), lambda i, *_: (i, 0)),
                  pl.BlockSpec(memory_space=pl.ANY), pl.BlockSpec(memory_space=pl.ANY),
                  resident(b_gate_up), resident(b_down)],
        out_specs=pl.BlockSpec((step_rows, dp), lambda i, *_: (i, 0)),
        scratch_shapes=[pltpu.VMEM((d, 2 * D_FF), F32), pltpu.VMEM((D_FF, d), F32),
                        pltpu.VMEM((d, 2 * D_FF), BF16), pltpu.VMEM((D_FF, d), BF16),
                        pltpu.SemaphoreType.DMA((2,))],
    )
    return pl.pallas_call(
        _ffn_kernel,
        out_shape=jax.ShapeDtypeStruct((n_rows, dp), I32),
        grid_spec=grid_spec,
        compiler_params=pltpu.CompilerParams(dimension_semantics=("arbitrary",), vmem_limit_bytes=FFN_VMEM_LIMIT),
        name="expert_ffn",
    )(block_expert, block_first, block_next, block_valid, xin, w_gate_up, w_down, b_gate_up, b_down)


def _combine_kernel(x1_ref, g2_ref, gate_ref, yg_ref, o_ref):
    gate = gate_ref[0]
    y_lo = y_hi = None
    for q in range(TOP_K):
        lo, hi = _unpack_halves(yg_ref[q, 0])
        wq = gate[:, q:q + 1]
        y_lo = wq * lo if y_lo is None else y_lo + wq * lo
        y_hi = wq * hi if y_hi is None else y_hi + wq * hi
    o_ref[0] = x1_ref[0] + g2_ref[0] * jnp.concatenate([y_lo, y_hi], axis=-1)


def _combine(x1, gate2, gate, yg):
    bsz, t_len, d = x1.shape
    tm = min(1024, t_len)
    return pl.pallas_call(
        _combine_kernel,
        out_shape=jax.ShapeDtypeStruct((bsz, t_len, d), F32),
        grid=(bsz, t_len // tm),
        in_specs=[pl.BlockSpec((1, tm, d), lambda b, i: (b, i, 0)),
                  pl.BlockSpec((1, 1, d), lambda b, i: (b, 0, 0)),
                  pl.BlockSpec((1, tm, TOP_K), lambda b, i: (b, i, 0)),
                  pl.BlockSpec((TOP_K, 1, tm, d // 2), lambda b, i: (0, b, i, 0))],
        out_specs=pl.BlockSpec((1, tm, d), lambda b, i: (b, i, 0)),
        compiler_params=_cparams(("parallel", "parallel")),
        name="moe_combine",
    )(x1, gate2, gate, yg)


def _moe(x1, gate2, h2, top_idx, gate, rank, counts, w_gate_up, b_gate_up, w_down, b_down, layer):
    bsz, t_len, d = h2.shape
    n_tok = bsz * t_len
    n_assign = n_tok * TOP_K
    n_blocks = -(-n_assign // MOE_BLOCK) + N_EXPERTS
    counts = counts.reshape(N_EXPERTS)
    blocks_e = (counts + MOE_BLOCK - 1) // MOE_BLOCK
    blk_end = jnp.cumsum(blocks_e)
    blk_start = blk_end - blocks_e
    experts = jnp.arange(N_EXPERTS, dtype=I32)
    onehot = top_idx.reshape(n_tok, TOP_K, 1) == experts
    dest = jnp.sum(jnp.where(onehot, blk_start * MOE_BLOCK, 0), axis=-1) + rank.reshape(n_tok, TOP_K)
    dest_t = dest.T.astype(I32)
    blk = jnp.arange(n_blocks, dtype=I32)
    block_expert = jnp.minimum(jnp.sum(blk_end[None, :] <= blk[:, None], axis=1), N_EXPERTS - 1).astype(I32)
    be_hot = block_expert[:, None] == experts
    cnt_b = jnp.sum(jnp.where(be_hot, counts, 0), axis=1)
    start_b = jnp.sum(jnp.where(be_hot, blk_start, 0), axis=1)
    block_valid = jnp.clip(cnt_b - (blk - start_b) * MOE_BLOCK, 0, MOE_BLOCK).astype(I32)
    xin = _sc_scatter_rows(h2.reshape(n_tok, d), dest_t.reshape(TOP_K, n_tok // SC_ROWS, SC_ROWS),
                           n_blocks * MOE_BLOCK)
    block_first = jnp.concatenate([jnp.ones((1,), I32), (block_expert[1:] != block_expert[:-1]).astype(I32)])
    run_start = jnp.where(block_first == 1, blk, n_blocks)
    later_start = lax.cummin(jnp.concatenate([run_start[1:], jnp.full((1,), n_blocks, I32)]), reverse=True)
    next_expert = jnp.concatenate([block_expert, jnp.full((1,), -1 - layer * N_EXPERTS, I32)])[later_start]
    yb = _ffn(block_expert + layer * N_EXPERTS, block_first, next_expert + layer * N_EXPERTS, block_valid, xin,
              w_gate_up, b_gate_up, w_down, b_down)
    yg = _sc_gather_rows(yb, dest_t.reshape(n_assign // SC_ROWS, SC_ROWS))
    return _combine(x1, gate2, gate, yg.reshape(TOP_K, bsz, t_len, d))


def _permute_kernel(w_ref, o_ref):
    o_gm = 0
    o_rw = o_gm + 2 * GM_WIDTH
    o_fox = o_rw + RW_SHIFT_WIDTH
    o_f = o_fox + 3 * FOX_WIDTH
    o_gate = o_f + FOX_HEADS
    w = w_ref[0]
    o_ref[0, :, Z_GATE:Z_FOX] = w[:, o_gate:o_gate + N_BRANCH * D_MODEL].astype(BF16)
    o_ref[0, :, Z_FOX:Z_GM] = w[:, o_fox:o_f].astype(BF16)
    o_ref[0, :, Z_GM:Z_RW] = w[:, o_gm:o_rw].astype(BF16)
    o_ref[0, :, Z_RW:Z_F] = w[:, o_rw:o_fox].astype(BF16)
    tail = jnp.concatenate([w[:, o_f:o_gate], jnp.zeros((w.shape[0], Z_WIDTH - Z_F - FOX_HEADS), F32)], axis=-1)
    o_ref[0, :, Z_F:Z_WIDTH] = tail.astype(BF16)


def _permute_w_in(w_in):
    n_layer, d, w_cols = w_in.shape
    tr = 256
    return pl.pallas_call(
        _permute_kernel,
        out_shape=jax.ShapeDtypeStruct((n_layer, d, Z_WIDTH), BF16),
        grid=(n_layer, d // tr),
        in_specs=[pl.BlockSpec((1, tr, w_cols), lambda l, i: (l, i, 0))],
        out_specs=pl.BlockSpec((1, tr, Z_WIDTH), lambda l, i: (l, i, 0)),
        compiler_params=_cparams(("parallel", "parallel")),
        name="permute_w_in",
    )(w_in)


def _layer(x, mod, w_in_p, gm_v_gain, gm_w_s, gm_b_s, mu_pad, w_lora, rw_w0, rw_a0, rw_k_k, rw_k_a, rw_r_k,
           rw_gn_gain, rw_gn_bias, f_bias_pad, fox_q_gain, fox_k_gain, w_branch, w_o, w_router, b_router,
           w_gate_up, b_gate_up, w_down, b_down, layer):
    shift1, scale1, gate1, shift2, scale2, gate2 = (mod[:, i][:, None, :] for i in range(6))
    z, z_rw = _inproj(x, scale1, shift1, w_in_p, layer)
    y_gm = _gmlp(z, gm_v_gain, gm_w_s, gm_b_s)
    r, lw, k, v, a, b, g = _rwprep(z_rw, mu_pad, w_lora, rw_w0, rw_a0, rw_k_k, rw_k_a)
    y_rw = _rwscan(r, lw, k, v, a, b, g, rw_r_k, rw_gn_gain, rw_gn_bias)
    q, kf, vf = _foxprep(z, z_rw, f_bias_pad, fox_q_gain, fox_k_gain)
    y_fox = _fox(q, kf, vf)
    x1, h2, top_idx, gate, rank, counts = _merge(z, y_gm, y_rw, y_fox, x, gate1, scale2, shift2,
                                                 w_branch, w_o, w_router, b_router, layer)
    return _moe(x1, gate2, h2, top_idx, gate, rank, counts, w_gate_up, b_gate_up, w_down, b_down, layer)


def kernel(x, c, w_ada, b_ada, w_in, gm_v_gain, gm_w_s, gm_b_s, rw_mu, rw_w0, rw_w2, rw_a0, rw_a2, rw_g2, rw_k_k,
           rw_k_a, rw_r_k, rw_gn_gain, rw_gn_bias, fox_f_bias, fox_q_gain, fox_k_gain, w_branch, w_o, w_router,
           b_router, w_gate_up, b_gate_up, w_down, b_down):
    n_layer = w_ada.shape[0]
    bsz = x.shape[0]
    c_pad = jnp.zeros((8, D_MODEL), F32).at[:bsz].set(c)
    mod = _adaln(c_pad, w_ada, b_ada)[:, :bsz].reshape(n_layer, bsz, 6, D_MODEL)
    w_in_p = _permute_w_in(w_in)
    mu_pad = jnp.pad(rw_mu, ((0, 0), (0, RW_BLOCK - RW_SHIFT_WIDTH)))
    w_lora = jnp.zeros((n_layer, RW_LORA, 3 * RW_WIDTH), F32)
    w_lora = w_lora.at[:, 0:RW_DECAY_LORA, 0:RW_WIDTH].set(rw_w2)
    w_lora = w_lora.at[:, RW_DECAY_LORA:RW_DECAY_LORA + RW_ICLR_LORA, RW_WIDTH:2 * RW_WIDTH].set(rw_a2)
    w_lora = w_lora.at[:, RW_DECAY_LORA + RW_ICLR_LORA:, 2 * RW_WIDTH:].set(rw_g2)
    f_bias_pad = jnp.pad(fox_f_bias, ((0, 0), (0, LANES - FOX_HEADS)))
    w_in_p = w_in_p.reshape(n_layer * D_MODEL, Z_WIDTH)
    w_branch_b = w_branch.astype(BF16).reshape(n_layer * MIX_WIDTH, D_MODEL)
    w_o_b = w_o.astype(BF16).reshape(n_layer * D_MODEL, D_MODEL)
    w_gu = w_gate_up.reshape(n_layer * N_EXPERTS, D_MODEL, 2 * D_FF)
    b_gu = b_gate_up.reshape(n_layer * N_EXPERTS, 1, 2 * D_FF)
    w_dn = w_down.reshape(n_layer * N_EXPERTS, D_FF, D_MODEL)
    b_dn = b_down.reshape(n_layer * N_EXPERTS, 1, D_MODEL)
    for l in range(n_layer):
        x = _layer(x, mod[l], w_in_p, gm_v_gain[l], gm_w_s[l], gm_b_s[l], mu_pad[l:l + 1], w_lora[l], rw_w0[l],
                   rw_a0[l], rw_k_k[l], rw_k_a[l], rw_r_k[l], rw_gn_gain[l], rw_gn_bias[l], f_bias_pad[l:l + 1],
                   fox_q_gain[l], fox_k_gain[l], w_branch_b, w_o_b, w_router[l], b_router[l],
                   w_gu, b_gu, w_dn, b_dn, l)
    return x
```

```python
import functools

import jax
import jax.numpy as jnp
import numpy as np
from jax import lax
from jax.experimental import pallas as pl
from jax.experimental.pallas import tpu as pltpu
from jax.experimental.pallas import tpu_sc as plsc

F32 = jnp.float32
BF16 = jnp.bfloat16
I32 = jnp.int32
HIGHEST = lax.Precision.HIGHEST

D_MODEL = 1024
GM_CHUNK = 128
GM_GROUPS = 4
GM_WIDTH = 256
GM_GROUP_DIM = GM_WIDTH // GM_GROUPS
RW_HEADS = 4
RW_HEAD_DIM = 64
RW_WIDTH = RW_HEADS * RW_HEAD_DIM
RW_DECAY_LORA = 32
RW_ICLR_LORA = 32
RW_GATE_LORA = 64
RW_LORA = RW_DECAY_LORA + RW_ICLR_LORA + RW_GATE_LORA
RW_SHIFT_WIDTH = 3 * RW_WIDTH + RW_LORA
RW_GN_EPS = 64e-5
FOX_HEADS = 8
FOX_HEAD_DIM = 64
FOX_WIDTH = FOX_HEADS * FOX_HEAD_DIM
ATTN_SCALE = FOX_HEAD_DIM ** -0.5
MASK_VALUE = -1e30
LOG2E = 1.4426950408889634
N_BRANCH = 3
MIX_WIDTH = GM_WIDTH + RW_WIDTH + FOX_WIDTH
N_EXPERTS = 32
TOP_K = 4
D_FF = D_MODEL
SWIGLU_LIMIT = 7.0
SWIGLU_ALPHA = 1.702
MOE_BLOCK = 256
EPS = 1e-6

Z_GATE = 0
Z_FOX = N_BRANCH * D_MODEL
Z_GM = Z_FOX + 3 * FOX_WIDTH
Z_RW = Z_GM + 2 * GM_WIDTH
RW_BLOCK = 1024
Z_F = Z_RW + RW_SHIFT_WIDTH
Z_WIDTH = Z_RW + RW_BLOCK
LANES = 128
RW_CHUNK = 64
RW_PREP_UNROLL = 4

VMEM_LIMIT = 48 * 1024 * 1024
FFN_VMEM_LIMIT = 56 * 1024 * 1024
FFN_STEP_BLOCKS = 4
SC_CORES = 2
SC_SUBCORES = 16
SC_WORKERS = SC_CORES * SC_SUBCORES
SC_ROWS = 64


def _cparams(sem):
    return pltpu.CompilerParams(dimension_semantics=sem, vmem_limit_bytes=VMEM_LIMIT)


def _mm(a, b):
    return jnp.dot(a.astype(BF16), b.astype(BF16), preferred_element_type=F32)


def _mm_nt(a, b):
    return lax.dot_general(a.astype(BF16), b.astype(BF16), (((1,), (1,)), ((), ())), preferred_element_type=F32)


def _mm_tn(a, b):
    return lax.dot_general(a.astype(BF16), b.astype(BF16), (((0,), (0,)), ((), ())), preferred_element_type=F32)


def _split3(x):
    hi = x.astype(BF16)
    r1 = x - hi.astype(F32)
    mid = r1.astype(BF16)
    lo = (r1 - mid.astype(F32)).astype(BF16)
    return hi, mid, lo


def _tri_cumsum(x, n):
    ri = lax.broadcasted_iota(I32, (n, n), 0)
    ci = lax.broadcasted_iota(I32, (n, n), 1)
    ones = jnp.where(ri >= ci, 1.0, 0.0).astype(BF16)
    hi, mid, lo = _split3(x)
    return (jnp.dot(ones, hi, preferred_element_type=F32) + jnp.dot(ones, mid, preferred_element_type=F32)
            + jnp.dot(ones, lo, preferred_element_type=F32))


def _pack_halves(x):
    w = x.shape[1] // 2
    hi = pltpu.bitcast(x[:, :w].astype(BF16).astype(F32), jnp.uint32)
    lo = pltpu.bitcast(x[:, w:].astype(BF16).astype(F32), jnp.uint32)
    return pltpu.bitcast(hi | (lo >> 16), I32)


def _unpack_halves(p):
    u = pltpu.bitcast(p, jnp.uint32)
    return pltpu.bitcast(u & jnp.uint32(0xFFFF0000), F32), pltpu.bitcast(u << 16, F32)


def _log_sigmoid(x):
    return jnp.minimum(x, 0.0) - jnp.log1p(jnp.exp(-jnp.abs(x)))


def _adaln_kernel(c_ref, w_ref, b_ref, o_ref):
    c = c_ref[...]
    s = c * jax.nn.sigmoid(c)
    o_ref[0] = jnp.dot(s, w_ref[0], preferred_element_type=F32, precision=HIGHEST) + b_ref[0]


def _adaln(c_pad, w_ada, b_ada):
    n_layer, d, w6 = w_ada.shape
    tn = 1536
    return pl.pallas_call(
        _adaln_kernel,
        out_shape=jax.ShapeDtypeStruct((n_layer, c_pad.shape[0], w6), F32),
        grid=(n_layer, w6 // tn),
        in_specs=[pl.BlockSpec(c_pad.shape, lambda l, j: (0, 0)),
                  pl.BlockSpec((1, d, tn), lambda l, j: (l, 0, j)),
                  pl.BlockSpec((1, 1, tn), lambda l, j: (l, 0, j))],
        out_specs=pl.BlockSpec((1, c_pad.shape[0], tn), lambda l, j: (l, 0, j)),
        compiler_params=_cparams(("parallel", "parallel")),
        name="adaln",
    )(c_pad, w_ada, b_ada.reshape(n_layer, 1, w6))


def _inproj_kernel(x_ref, sc_ref, sh_ref, w_ref, zm_ref, zr_ref, xn_ref):
    j = pl.program_id(2)

    @pl.when(j == 0)
    def _():
        x = x_ref[0]
        xn = x * lax.rsqrt(jnp.mean(x * x, axis=-1, keepdims=True) + EPS)
        xn_ref[...] = (xn * (1.0 + sc_ref[0]) + sh_ref[0]).astype(BF16)

    acc = jnp.dot(xn_ref[...], w_ref[...], preferred_element_type=F32)

    @pl.when(j < Z_RW // RW_BLOCK)
    def _():
        zm_ref[0] = acc.astype(BF16)

    @pl.when(j == Z_RW // RW_BLOCK)
    def _():
        zr_ref[0] = acc


def _inproj(x, scale, shift, w, layer):
    bsz, t_len, d = x.shape
    tm = min(1024, t_len)
    tn = RW_BLOCK
    n_main = Z_RW // tn
    return pl.pallas_call(
        _inproj_kernel,
        out_shape=(jax.ShapeDtypeStruct((bsz, t_len, Z_RW), BF16), jax.ShapeDtypeStruct((bsz, t_len, RW_BLOCK), F32)),
        grid=(bsz, t_len // tm, Z_WIDTH // tn),
        in_specs=[pl.BlockSpec((1, tm, d), lambda b, i, j: (b, i, 0)),
                  pl.BlockSpec((1, 1, d), lambda b, i, j: (b, 0, 0)),
                  pl.BlockSpec((1, 1, d), lambda b, i, j: (b, 0, 0)),
                  pl.BlockSpec((d, tn), lambda b, i, j: (layer, j))],
        out_specs=(pl.BlockSpec((1, tm, tn), lambda b, i, j: (b, i, jnp.minimum(j, n_main - 1))),
                   pl.BlockSpec((1, tm, tn), lambda b, i, j: (b, i, 0))),
        scratch_shapes=[pltpu.VMEM((tm, d), BF16)],
        compiler_params=_cparams(("parallel", "parallel", "arbitrary")),
        name="inproj",
    )(x, scale, shift, w)


def _gmlp_kernel(z_ref, gain_ref, ws_ref, bst_ref, o_ref):
    tm = z_ref.shape[1]
    z = z_ref[0].astype(F32)
    u = jax.nn.gelu(z[:, :GM_WIDTH])
    v = jax.nn.gelu(z[:, GM_WIDTH:])
    v = v * lax.rsqrt(jnp.mean(v * v, axis=-1, keepdims=True) + EPS) * gain_ref[...]
    vb = v.astype(BF16)
    grp = lax.broadcasted_iota(I32, (GM_CHUNK, GM_WIDTH), 1) // GM_GROUP_DIM
    ri = lax.broadcasted_iota(I32, (GM_CHUNK, GM_CHUNK), 0)
    ci = lax.broadcasted_iota(I32, (GM_CHUNK, GM_CHUNK), 1)
    causal = ri >= ci
    bias = jnp.zeros((GM_CHUNK, GM_WIDTH), F32)
    ws = []
    for g in range(GM_GROUPS):
        ws.append(jnp.where(causal, ws_ref[g], 0.0).astype(BF16))
        bias = jnp.where(grp == g, bst_ref[:, g:g + 1], bias)
    for c in range(tm // GM_CHUNK):
        rows = slice(c * GM_CHUNK, (c + 1) * GM_CHUNK)
        vc = vb[rows]
        mixed = bias
        for g in range(GM_GROUPS):
            m = jnp.dot(ws[g], vc, preferred_element_type=F32)
            mixed = mixed + jnp.where(grp == g, m, 0.0)
        o_ref[0, rows, :] = (u[rows] * mixed).astype(o_ref.dtype)


def _gmlp(z, gain, w_s, b_s):
    bsz, t_len, _ = z.shape
    tm = min(1024, t_len)
    return pl.pallas_call(
        _gmlp_kernel,
        out_shape=jax.ShapeDtypeStruct((bsz, t_len, GM_WIDTH), BF16),
        grid=(bsz, t_len // tm),
        in_specs=[pl.BlockSpec((1, tm, 2 * GM_WIDTH), lambda b, i: (b, i, Z_GM // (2 * GM_WIDTH))),
                  pl.BlockSpec((1, GM_WIDTH), lambda b, i: (0, 0)),
                  pl.BlockSpec((GM_GROUPS, GM_CHUNK, GM_CHUNK), lambda b, i: (0, 0, 0)),
                  pl.BlockSpec((GM_CHUNK, GM_GROUPS), lambda b, i: (0, 0))],
        out_specs=pl.BlockSpec((1, tm, GM_WIDTH), lambda b, i: (b, i, 0)),
        compiler_params=_cparams(("parallel", "parallel")),
        name="gmlp",
    )(z, gain.reshape(1, GM_WIDTH), w_s, b_s.T)


def _rwprep_kernel(z_ref, zp_ref, mu_ref, wl_ref, w0_ref, a0_ref, kk_ref, ka_ref,
                   r_o, lw_o, k_o, v_o, a_o, b_o, g_o):
    tm = z_ref.shape[1]
    z = z_ref[0]
    prev = jnp.where(pl.program_id(1) > 0, zp_ref[0, 7:8, :], 0.0)
    rowid = lax.broadcasted_iota(I32, z.shape, 0)
    zs = jnp.where(rowid == 0, prev, pltpu.roll(z, 1, axis=0))
    zz = z + mu_ref[...] * (zs - z)
    r = zz[:, 0:RW_WIDTH]
    k = zz[:, RW_WIDTH:2 * RW_WIDTH]
    v = zz[:, 2 * RW_WIDTH:3 * RW_WIDTH]
    lo = zz[:, 3 * RW_WIDTH:3 * RW_WIDTH + RW_LORA]
    lane = lax.broadcasted_iota(I32, (tm, RW_LORA), 1)
    act = jnp.where(lane < RW_DECAY_LORA, jnp.tanh(lo),
                    jnp.where(lane < RW_DECAY_LORA + RW_ICLR_LORA, lo, jax.nn.sigmoid(lo)))
    a_hi, a_lo, _ = _split3(act)
    w_hi, w_lo, _ = _split3(wl_ref[...])
    proj = (jnp.dot(a_hi, w_hi, preferred_element_type=F32) + jnp.dot(a_hi, w_lo, preferred_element_type=F32)
            + jnp.dot(a_lo, w_hi, preferred_element_type=F32))
    xw = -(w0_ref[...] + proj[:, 0:RW_WIDTH])
    softplus = jnp.maximum(xw, 0.0) + jnp.log1p(jnp.exp(-jnp.abs(xw)))
    lw = -jnp.exp(-softplus - 0.5)
    a = jax.nn.sigmoid(a0_ref[...] + proj[:, RW_WIDTH:2 * RW_WIDTH])
    g = proj[:, 2 * RW_WIDTH:3 * RW_WIDTH]
    kk = k * kk_ref[...]
    k2 = k * (1.0 + (a - 1.0) * ka_ref[...])
    for h in range(RW_HEADS):
        sl = slice(h * RW_HEAD_DIM, (h + 1) * RW_HEAD_DIM)
        kkh = kk[:, sl]
        nrm = jnp.sqrt(jnp.sum(kkh * kkh, axis=-1, keepdims=True))
        kkh = kkh / jnp.maximum(nrm, 1e-12)
        r_o[0, h] = r[:, sl]
        lw_o[0, h] = lw[:, sl]
        k_o[0, h] = k2[:, sl]
        v_o[0, h] = v[:, sl]
        a_o[0, h] = -kkh
        b_o[0, h] = kkh * a[:, sl]
        g_o[0, h] = g[:, sl]


def _rwprep(z, mu_pad, w_lora, w0, a0, k_k, k_a):
    bsz, t_len, _ = z.shape
    tm = min(1024, t_len)
    hm = jax.ShapeDtypeStruct((bsz, RW_HEADS, t_len, RW_HEAD_DIM), F32)
    hm_spec = pl.BlockSpec((1, RW_HEADS, tm, RW_HEAD_DIM), lambda b, i: (b, 0, i, 0))
    vec = lambda n: pl.BlockSpec((1, n), lambda b, i: (0, 0))
    rw_blk = 0
    return pl.pallas_call(
        _rwprep_kernel,
        out_shape=(hm,) * 7,
        grid=(bsz, t_len // tm),
        in_specs=[pl.BlockSpec((1, tm, RW_BLOCK), lambda b, i: (b, i, rw_blk)),
                  pl.BlockSpec((1, 8, RW_BLOCK), lambda b, i: (b, jnp.maximum(i * (tm // 8) - 1, 0), rw_blk)),
                  vec(RW_BLOCK),
                  pl.BlockSpec((RW_LORA, 3 * RW_WIDTH), lambda b, i: (0, 0)),
                  vec(RW_WIDTH), vec(RW_WIDTH), vec(RW_WIDTH), vec(RW_WIDTH)],
        out_specs=(hm_spec,) * 7,
        compiler_params=_cparams(("parallel", "parallel")),
        name="rwprep",
    )(z, z, mu_pad, w_lora, w0.reshape(1, -1), a0.reshape(1, -1), k_k.reshape(1, -1), k_a.reshape(1, -1))


def _rwscan_kernel(r_ref, lw_ref, k_ref, v_ref, a_ref, b_ref, g_ref, rk_ref, gg_ref, gb_ref, o_ref,
                   s_ref, rp_ref, y_ref, gm_ref, h0_ref, we_ref):
    cl = RW_CHUNK
    tb = r_ref.shape[2]
    n_chunk = tb // cl

    @pl.when(pl.program_id(1) == 0)
    def _():
        s_ref[...] = jnp.zeros_like(s_ref)

    n = RW_HEADS * cl
    ri = lax.broadcasted_iota(I32, (n, n), 0)
    ci = lax.broadcasted_iota(I32, (n, n), 1)
    same_head = (ri // cl) == (ci // cl)
    lower = same_head & (ri >= ci)
    strict = same_head & (ri > ci)
    eye = jnp.where(ri == ci, 1.0, 0.0)
    ones_lower = jnp.where(lower, 1.0, 0.0).astype(BF16)

    def prepare(chunks):
        grp = range(len(chunks))
        each = lambda fn: [fn(u) for u in grp]
        rows = [pl.ds(pl.multiple_of(c * cl, cl), cl) for c in chunks]
        stack = lambda ref: each(lambda u: ref[0, :, rows[u], :].reshape(n, RW_HEAD_DIM))
        r, lw, k, v, a, b = (stack(ref) for ref in (r_ref, lw_ref, k_ref, v_ref, a_ref, b_ref))
        hd = RW_HEAD_DIM
        parts = each(lambda u: jnp.concatenate(_split3(lw[u]), axis=-1))
        sums = each(lambda u: jnp.dot(ones_lower, parts[u], preferred_element_type=F32))
        cw = each(lambda u: sums[u][:, :hd] + sums[u][:, hd:2 * hd] + sums[u][:, 2 * hd:])
        w_in = each(lambda u: jnp.exp(cw[u]))
        w_inv = each(lambda u: jnp.exp(-cw[u]))
        rt = each(lambda u: r[u] * w_in[u])
        at = each(lambda u: a[u] * jnp.exp(cw[u] - lw[u]))
        kt = each(lambda u: k[u] * w_inv[u])
        bt = each(lambda u: b[u] * w_inv[u])
        w_end = each(lambda u: w_in[u].reshape(RW_HEADS, cl, RW_HEAD_DIM)[:, cl - 1:cl, :])
        w_end_rows = each(lambda u: jnp.broadcast_to(w_end[u], (RW_HEADS, cl, RW_HEAD_DIM)).reshape(n, RW_HEAD_DIM))
        a_ab = each(lambda u: jnp.where(strict, _mm_nt(at[u], bt[u]), 0.0))
        a_ak = each(lambda u: jnp.where(strict, _mm_nt(at[u], kt[u]), 0.0))
        m_rb = each(lambda u: jnp.where(lower, _mm_nt(rt[u], bt[u]), 0.0))
        m_rk = each(lambda u: jnp.where(lower, _mm_nt(rt[u], kt[u]), 0.0))
        inv = each(lambda u: eye + a_ab[u])
        p = a_ab
        for _ in range(cl.bit_length() - 2):
            p = [_mm(p[u], p[u]) for u in grp]
            inv = [inv[u] + _mm(inv[u], p[u]) for u in grp]
        akv = each(lambda u: _mm(a_ak[u], v[u]))
        apz = each(lambda u: _mm(inv[u], jnp.concatenate([at[u], akv[u]], axis=-1)).astype(BF16))
        mix = each(lambda u: jnp.dot(m_rb[u].astype(BF16), apz[u], preferred_element_type=F32))
        bend = each(lambda u: bt[u] * w_end_rows[u])
        kend = each(lambda u: kt[u] * w_end_rows[u])
        rp = each(lambda u: (rt[u] + mix[u][:, :hd]).astype(BF16))
        y0 = each(lambda u: mix[u][:, hd:] + _mm(m_rk[u], v[u]))
        for u in grp:
            for h in range(RW_HEADS):
                hs = slice(h * cl, (h + 1) * cl)
                both = _mm_tn(apz[u][hs], bend[u][hs])
                rp_ref[h, rows[u], :] = rp[u][hs]
                y_ref[h, rows[u], :] = y0[u][hs]
                gm_ref[h, rows[u], :] = both[:hd].astype(BF16)
                h0_ref[h, rows[u], :] = both[hd:] + _mm_tn(v[u][hs], kend[u][hs])
                we_ref[h, chunks[u]] = w_end[u][h]

    def prepare_step(i, carry):
        prepare([i * RW_PREP_UNROLL + u for u in range(RW_PREP_UNROLL)])
        return carry

    lax.fori_loop(0, n_chunk // RW_PREP_UNROLL, prepare_step, 0)

    def advance(c, carry):
        rows = pl.ds(pl.multiple_of(c * cl, cl), cl)
        for h in range(RW_HEADS):
            s = s_ref[h]
            sb = s.astype(BF16)
            y_ref[h, rows, :] = y_ref[h, rows, :] + lax.dot_general(
                rp_ref[h, rows, :], sb, (((1,), (1,)), ((), ())), preferred_element_type=F32)
            s_ref[h] = (s * we_ref[h, c] + jnp.dot(sb, gm_ref[h, rows, :], preferred_element_type=F32)
                        + h0_ref[h, rows, :])
        return carry

    lax.fori_loop(0, n_chunk, advance, 0)

    for h in range(RW_HEADS):
        y = y_ref[h]
        mu = jnp.mean(y, axis=-1, keepdims=True)
        yc = y - mu
        var = jnp.mean(yc * yc, axis=-1, keepdims=True)
        yn = yc * lax.rsqrt(var + RW_GN_EPS) * gg_ref[h] + gb_ref[h]
        v = v_ref[0, h]
        bonus = jnp.sum(r_ref[0, h] * k_ref[0, h] * rk_ref[h], axis=-1, keepdims=True) * v
        o_ref[0, h] = ((yn + bonus) * g_ref[0, h]).astype(o_ref.dtype)


def _rwscan(r, lw, k, v, a, b, g, r_k, gn_gain, gn_bias):
    bsz, _, t_len, _ = r.shape
    tb = min(1024, t_len)
    hm_spec = pl.BlockSpec((1, RW_HEADS, tb, RW_HEAD_DIM), lambda bi, i: (bi, 0, i, 0))
    par = pl.BlockSpec((RW_HEADS, 1, RW_HEAD_DIM), lambda bi, i: (0, 0, 0))
    hshape = (RW_HEADS, 1, RW_HEAD_DIM)
    return pl.pallas_call(
        _rwscan_kernel,
        out_shape=jax.ShapeDtypeStruct((bsz, RW_HEADS, t_len, RW_HEAD_DIM), BF16),
        grid=(bsz, t_len // tb),
        in_specs=[hm_spec] * 7 + [par] * 3,
        out_specs=hm_spec,
        scratch_shapes=[pltpu.VMEM((RW_HEADS, RW_HEAD_DIM, RW_HEAD_DIM), F32),
                        pltpu.VMEM((RW_HEADS, tb, RW_HEAD_DIM), BF16), pltpu.VMEM((RW_HEADS, tb, RW_HEAD_DIM), F32),
                        pltpu.VMEM((RW_HEADS, tb, RW_HEAD_DIM), BF16), pltpu.VMEM((RW_HEADS, tb, RW_HEAD_DIM), F32),
                        pltpu.VMEM((RW_HEADS, tb // RW_CHUNK, 1, RW_HEAD_DIM), F32)],
        compiler_params=_cparams(("parallel", "arbitrary")),
        name="rwscan",
    )(r, lw, k, v, a, b, g, r_k.reshape(hshape), gn_gain.reshape(hshape), gn_bias.reshape(hshape))


FOX_PAIRS = FOX_HEADS // 2
FOX_EXTRA = 3
FOX_ACC_ROWS = FOX_HEAD_DIM + 16
FOX_LOOKAHEAD = 2


def _fox_bias_selector():
    sel = np.zeros((LANES, 2 * FOX_HEADS * LANES), np.float32)
    for h in range(FOX_HEADS):
        base = FOX_HEAD_DIM if h % 2 == 0 else 0
        for p in range(FOX_EXTRA):
            sel[p * FOX_HEADS + h, h * LANES + base + p] = 1.0
            sel[p * FOX_HEADS + h, (FOX_HEADS + h) * LANES + base + FOX_EXTRA + p] = -1.0
    return sel


def _foxprep_kernel(z_ref, f_ref, fb_ref, qg_ref, kg_ref, sel_ref, q_o, k_o, vt_o, carry_ref):
    tm = z_ref.shape[1]

    @pl.when(pl.program_id(1) == 0)
    def _():
        carry_ref[...] = jnp.zeros_like(carry_ref)

    log_f = _log_sigmoid(f_ref[0] + fb_ref[...])
    cum = carry_ref[...] + _tri_cumsum(log_f, tm)
    carry_ref[...] = cum[tm - 1:tm, :]
    lane = lax.broadcasted_iota(I32, (tm, LANES), 1)
    hi, mid, lo = (p.astype(F32) for p in _split3(cum * LOG2E))
    packed = jnp.where(lane < FOX_HEADS, hi,
                       jnp.where(lane < 2 * FOX_HEADS, pltpu.roll(mid, FOX_HEADS, axis=1),
                                 pltpu.roll(lo, 2 * FOX_HEADS, axis=1)))
    packed = jnp.where(lane < FOX_EXTRA * FOX_HEADS, packed, 0.0).astype(BF16)
    extra = jnp.dot(packed, sel_ref[...], preferred_element_type=F32)

    left = lane < FOX_HEAD_DIM
    in_half = lane % FOX_HEAD_DIM
    ones_q = jnp.where((in_half >= FOX_EXTRA) & (in_half < 2 * FOX_EXTRA), 1.0, 0.0)
    ones_k = jnp.where(in_half < FOX_EXTRA, 1.0, 0.0)

    def normed(block, gain):
        sq = block * block
        s_left = jnp.sum(jnp.where(left, sq, 0.0), axis=-1, keepdims=True)
        s_right = jnp.sum(jnp.where(left, 0.0, sq), axis=-1, keepdims=True)
        ms = jnp.where(left, s_left, s_right) * (1.0 / FOX_HEAD_DIM)
        return block * lax.rsqrt(ms + EPS) * gain

    for j in range(FOX_PAIRS):
        qn = normed(z_ref[0, :, j * LANES:(j + 1) * LANES].astype(F32), qg_ref[...] * (ATTN_SCALE * LOG2E))
        kn = normed(z_ref[0, :, FOX_WIDTH + j * LANES:FOX_WIDTH + (j + 1) * LANES].astype(F32), kg_ref[...])
        for par in range(2):
            h = 2 * j + par
            own = left if par == 0 else jnp.logical_not(left)
            q_o[0, h] = jnp.where(own, qn, extra[:, h * LANES:(h + 1) * LANES] + ones_q).astype(BF16)
            k_o[0, h] = jnp.where(own, kn, extra[:, (FOX_HEADS + h) * LANES:(FOX_HEADS + h + 1) * LANES]
                                  + ones_k).astype(BF16)
    ri = lax.broadcasted_iota(I32, (FOX_WIDTH, FOX_WIDTH), 0)
    ci = lax.broadcasted_iota(I32, (FOX_WIDTH, FOX_WIDTH), 1)
    eye = jnp.where(ri == ci, 1.0, 0.0).astype(BF16)
    v = z_ref[0, :, 2 * FOX_WIDTH:3 * FOX_WIDTH].astype(BF16)
    vt_o[0] = lax.dot_general(eye, v, (((1,), (1,)), ((), ())), preferred_element_type=F32).astype(BF16)


def _foxprep(z, z_rw, f_bias_pad, q_gain, k_gain):
    bsz, t_len, _ = z.shape
    tm = min(512, t_len)
    qk = jax.ShapeDtypeStruct((bsz, FOX_HEADS, t_len, LANES), BF16)
    qk_spec = pl.BlockSpec((1, FOX_HEADS, tm, LANES), lambda b, i: (b, 0, i, 0))
    sel = jnp.asarray(_fox_bias_selector(), BF16)
    return pl.pallas_call(
        _foxprep_kernel,
        out_shape=(qk, qk, jax.ShapeDtypeStruct((bsz, FOX_WIDTH, t_len), BF16)),
        grid=(bsz, t_len // tm),
        in_specs=[pl.BlockSpec((1, tm, 3 * FOX_WIDTH), lambda b, i: (b, i, Z_FOX // (3 * FOX_WIDTH))),
                  pl.BlockSpec((1, tm, LANES), lambda b, i: (b, i, (Z_F - Z_RW) // LANES)),
                  pl.BlockSpec((1, LANES), lambda b, i: (0, 0)),
                  pl.BlockSpec((1, LANES), lambda b, i: (0, 0)),
                  pl.BlockSpec((1, LANES), lambda b, i: (0, 0)),
                  pl.BlockSpec(sel.shape, lambda b, i: (0, 0))],
        out_specs=(qk_spec, qk_spec, pl.BlockSpec((1, FOX_WIDTH, tm), lambda b, i: (b, 0, i))),
        scratch_shapes=[pltpu.VMEM((1, LANES), F32)],
        compiler_params=_cparams(("parallel", "arbitrary")),
        name="foxprep",
    )(z, z_rw, f_bias_pad, jnp.tile(q_gain.reshape(1, -1), (1, 2)), jnp.tile(k_gain.reshape(1, -1), (1, 2)), sel)


def _fox_kernel(qi_ref, kj_ref, q_ref, k_ref, vt_ref, o_ref, m_ref, acc_ref):
    i = qi_ref[pl.program_id(1)]
    j = kj_ref[pl.program_id(1)]
    tq = q_ref.shape[2]
    tk = k_ref.shape[2]
    sub = 8

    @pl.when(j == 0)
    def _():
        m_ref[...] = jnp.full_like(m_ref, MASK_VALUE)
        acc_ref[...] = jnp.zeros_like(acc_ref)

    ones_rows = jnp.ones((FOX_ACC_ROWS - FOX_HEAD_DIM, tk), BF16)

    def scores(h):
        return lax.dot_general(k_ref[0, h], q_ref[0, h], (((1,), (1,)), ((), ())), preferred_element_type=F32)

    def update(diagonal):
        if diagonal:
            key = lax.broadcasted_iota(I32, (tk, tq), 0)
            qry = lax.broadcasted_iota(I32, (tk, tq), 1)
            keep = key <= qry
        ahead = [scores(h) for h in range(FOX_LOOKAHEAD)]
        for h in range(FOX_HEADS):
            s = ahead.pop(0)
            if h + FOX_LOOKAHEAD < FOX_HEADS:
                ahead.append(scores(h + FOX_LOOKAHEAD))
            if diagonal:
                s = jnp.where(keep, s, MASK_VALUE)
            s3 = s.reshape(tk // sub, sub, tq)
            m_prev = m_ref[h]
            m_cur = jnp.max(jnp.max(s3, axis=0), axis=0, keepdims=True)
            m_new = jnp.maximum(m_prev, m_cur)
            alpha = jnp.exp2(m_prev - m_new)
            p = jnp.exp2(s3 - m_new[None]).astype(BF16).reshape(tk, tq)
            lhs = jnp.concatenate([vt_ref[0, h * FOX_HEAD_DIM:(h + 1) * FOX_HEAD_DIM, :], ones_rows], axis=0)
            pv = jnp.dot(lhs, p, preferred_element_type=F32)
            acc = acc_ref[h].reshape(FOX_ACC_ROWS // sub, sub, tq) * alpha[None]
            acc_ref[h] = acc.reshape(FOX_ACC_ROWS, tq) + pv
            m_ref[h] = m_new

    @pl.when(j < i)
    def _():
        update(False)

    @pl.when(j == i)
    def _():
        update(True)
        outs = []
        for h in range(FOX_HEADS):
            acc = acc_ref[h]
            outs.append((acc[:FOX_HEAD_DIM] / acc[FOX_HEAD_DIM:FOX_HEAD_DIM + 1]).astype(BF16))
        out_t = jnp.concatenate(outs, axis=0)
        ri = lax.broadcasted_iota(I32, (tq, tq), 0)
        ci = lax.broadcasted_iota(I32, (tq, tq), 1)
        eye = jnp.where(ri == ci, 1.0, 0.0).astype(BF16)
        o_ref[0] = lax.dot_general(eye, out_t, (((1,), (1,)), ((), ())),
                                   preferred_element_type=F32).astype(o_ref.dtype)


def _fox(q, k, vt):
    bsz, _, t_len, _ = q.shape
    tq = min(512, t_len)
    n_blk = t_len // tq
    pairs = [(i, j) for i in range(n_blk) for j in range(i + 1)]
    qi = jnp.asarray([p[0] for p in pairs], I32)
    kj = jnp.asarray([p[1] for p in pairs], I32)
    grid_spec = pltpu.PrefetchScalarGridSpec(
        num_scalar_prefetch=2,
        grid=(bsz, len(pairs)),
        in_specs=[pl.BlockSpec((1, FOX_HEADS, tq, LANES), lambda b, s, qi, kj: (b, 0, qi[s], 0)),
                  pl.BlockSpec((1, FOX_HEADS, tq, LANES), lambda b, s, qi, kj: (b, 0, kj[s], 0)),
                  pl.BlockSpec((1, FOX_WIDTH, tq), lambda b, s, qi, kj: (b, 0, kj[s]))],
        out_specs=pl.BlockSpec((1, tq, FOX_WIDTH), lambda b, s, qi, kj: (b, qi[s], 0)),
        scratch_shapes=[pltpu.VMEM((FOX_HEADS, 8, tq), F32), pltpu.VMEM((FOX_HEADS, FOX_ACC_ROWS, tq), F32)],
    )
    return pl.pallas_call(
        _fox_kernel,
        out_shape=jax.ShapeDtypeStruct((bsz, t_len, FOX_WIDTH), BF16),
        grid_spec=grid_spec,
        compiler_params=_cparams(("parallel", "arbitrary")),
        name="fox",
    )(qi, kj, q, k, vt)


def _merge_kernel(zg_ref, ygm_ref, yrw_ref, yfox_ref, x_ref, g1_ref, sc2_ref, sh2_ref, pb_ref, wo_ref, wr_ref, br_ref,
                  x1_o, h2_o, idx_o, gate_o, rank_o, cnt_o, carry_ref):
    tm = x_ref.shape[1]

    @pl.when((pl.program_id(0) == 0) & (pl.program_id(1) == 0))
    def _():
        carry_ref[...] = jnp.zeros_like(carry_ref)

    sg = 0.5 * jnp.tanh(0.5 * zg_ref[0].astype(F32)) + 0.5
    p_gm = jnp.dot(ygm_ref[0], pb_ref[0:GM_WIDTH, :], preferred_element_type=F32)
    y_rw = jnp.concatenate([yrw_ref[0, h] for h in range(RW_HEADS)], axis=-1)
    p_rw = jnp.dot(y_rw, pb_ref[GM_WIDTH:GM_WIDTH + RW_WIDTH, :], preferred_element_type=F32)
    p_fox = jnp.dot(yfox_ref[0], pb_ref[GM_WIDTH + RW_WIDTH:, :], preferred_element_type=F32)
    merged = sg[:, 0:D_MODEL] * p_gm + sg[:, D_MODEL:2 * D_MODEL] * p_rw + sg[:, 2 * D_MODEL:] * p_fox
    x1 = x_ref[0] + g1_ref[0] * jnp.dot(merged.astype(BF16), wo_ref[...], preferred_element_type=F32)
    x1_o[0] = x1
    h2 = x1 * lax.rsqrt(jnp.mean(x1 * x1, axis=-1, keepdims=True) + EPS) * (1.0 + sc2_ref[0]) + sh2_ref[0]
    h2_o[0] = _pack_halves(h2)

    h_hi, h_lo, _ = _split3(h2)
    w_hi, w_lo, _ = _split3(wr_ref[...])
    logits = (jnp.dot(h_hi, w_hi, preferred_element_type=F32) + jnp.dot(h_hi, w_lo, preferred_element_type=F32)
              + jnp.dot(h_lo, w_hi, preferred_element_type=F32)) + br_ref[...]
    lane = lax.broadcasted_iota(I32, (tm, N_EXPERTS), 1)
    vals, idxs = [], []
    rest = logits
    for _ in range(TOP_K):
        m = jnp.max(rest, axis=-1, keepdims=True)
        am = jnp.min(jnp.where(rest == m, lane, N_EXPERTS), axis=-1, keepdims=True)
        vals.append(m)
        idxs.append(am)
        rest = jnp.where(lane == am, -jnp.inf, rest)
    exps = [jnp.exp(val - vals[0]) for val in vals]
    denom = exps[0] + exps[1] + exps[2] + exps[3]

    onehot = jnp.zeros((tm, N_EXPERTS), F32)
    for am in idxs:
        onehot = onehot + jnp.where(lane == am, 1.0, 0.0)
    ri = lax.broadcasted_iota(I32, (tm, tm), 0)
    ci = lax.broadcasted_iota(I32, (tm, tm), 1)
    before = jnp.where(ri > ci, 1.0, 0.0).astype(BF16)
    seen = carry_ref[...] + jnp.dot(before, onehot.astype(BF16), preferred_element_type=F32)
    lane_k = lax.broadcasted_iota(I32, (tm, TOP_K), 1)
    idx_out = jnp.zeros((tm, TOP_K), I32)
    gate_out = jnp.zeros((tm, TOP_K), F32)
    rank_out = jnp.zeros((tm, TOP_K), I32)
    for kk in range(TOP_K):
        rank = jnp.sum(jnp.where(lane == idxs[kk], seen, 0.0), axis=-1, keepdims=True).astype(I32)
        idx_out = jnp.where(lane_k == kk, idxs[kk], idx_out)
        gate_out = jnp.where(lane_k == kk, exps[kk] / denom, gate_out)
        rank_out = jnp.where(lane_k == kk, rank, rank_out)
    idx_o[0] = idx_out
    gate_o[0] = gate_out
    rank_o[0] = rank_out
    total = carry_ref[...] + jnp.sum(onehot, axis=0, keepdims=True)
    carry_ref[...] = total
    cnt_o[...] = total.astype(I32)


def _merge(z, y_gm, y_rw, y_fox, x, gate1, scale2, shift2, w_branch, w_o, w_router, b_router, layer):
    bsz, t_len, d = x.shape
    tm = min(512, t_len)
    row = lambda w: pl.BlockSpec((1, tm, w), lambda b, i: (b, i, 0))
    mod = pl.BlockSpec((1, 1, d), lambda b, i: (b, 0, 0))
    full = lambda shape: pl.BlockSpec(shape, lambda b, i: (0,) * len(shape))
    return pl.pallas_call(
        _merge_kernel,
        out_shape=(jax.ShapeDtypeStruct((bsz, t_len, d), F32), jax.ShapeDtypeStruct((bsz, t_len, d // 2), I32),
                   jax.ShapeDtypeStruct((bsz, t_len, TOP_K), I32), jax.ShapeDtypeStruct((bsz, t_len, TOP_K), F32),
                   jax.ShapeDtypeStruct((bsz, t_len, TOP_K), I32), jax.ShapeDtypeStruct((1, N_EXPERTS), I32)),
        grid=(bsz, t_len // tm),
        in_specs=[row(N_BRANCH * D_MODEL), row(GM_WIDTH),
                  pl.BlockSpec((1, RW_HEADS, tm, RW_HEAD_DIM), lambda b, i: (b, 0, i, 0)),
                  row(FOX_WIDTH), row(d), mod, mod, mod,
                  pl.BlockSpec((MIX_WIDTH, d), lambda b, i: (layer, 0)), pl.BlockSpec((d, d), lambda b, i: (layer, 0)),
                  full(w_router.shape), full((1, N_EXPERTS))],
        out_specs=(row(d), row(d // 2), row(TOP_K), row(TOP_K), row(TOP_K), full((1, N_EXPERTS))),
        scratch_shapes=[pltpu.VMEM((1, N_EXPERTS), F32)],
        compiler_params=_cparams(("arbitrary", "arbitrary")),
        name="merge_router",
    )(z, y_gm, y_rw, y_fox, x, gate1, scale2, shift2, w_branch, w_o, w_router, b_router.reshape(1, N_EXPERTS))


def _sc_mesh():
    return plsc.VectorSubcoreMesh(core_axis_name="c", subcore_axis_name="s",
                                  num_cores=SC_CORES, num_subcores=SC_SUBCORES)


def _sc_worker():
    return lax.axis_index("s") * SC_CORES + lax.axis_index("c")


def _sc_scatter_rows(src, idx3, n_out):
    _, d = src.shape
    n_copy, n_grp, _ = idx3.shape
    grp_per_w = n_grp // SC_WORKERS
    assert grp_per_w % 2 == 0

    def body(src_hbm, idx_hbm, out_hbm, idx_v, rows_a, rows_b, sem):
        g0 = _sc_worker() * grp_per_w
        for q in range(n_copy):
            pltpu.sync_copy(idx_hbm.at[q, pl.ds(g0, grp_per_w)], idx_v.at[pl.ds(q * grp_per_w, grp_per_w)])

        @pl.loop(0, grp_per_w, step=2)
        def _(j):
            read_a = pltpu.async_copy(src_hbm.at[pl.ds((g0 + j) * SC_ROWS, SC_ROWS)], rows_a, sem.at[0])
            read_b = pltpu.async_copy(src_hbm.at[pl.ds((g0 + j + 1) * SC_ROWS, SC_ROWS)], rows_b, sem.at[1])
            read_a.wait()
            put_a = [pltpu.async_copy(rows_a, out_hbm.at[idx_v.at[q * grp_per_w + j]], sem.at[2])
                     for q in range(n_copy)]
            read_b.wait()
            put_b = [pltpu.async_copy(rows_b, out_hbm.at[idx_v.at[q * grp_per_w + j + 1]], sem.at[3])
                     for q in range(n_copy)]
            for cp in put_a + put_b:
                cp.wait()

    return pl.kernel(
        body, out_type=jax.ShapeDtypeStruct((n_out, d), src.dtype), mesh=_sc_mesh(),
        scratch_types=[pltpu.VMEM((n_copy * grp_per_w, SC_ROWS), I32), pltpu.VMEM((SC_ROWS, d), src.dtype),
                       pltpu.VMEM((SC_ROWS, d), src.dtype), pltpu.SemaphoreType.DMA((4,))],
        name="sc_dispatch",
    )(src, idx3)


def _sc_gather_rows(table, idx2):
    _, d = table.shape
    n_grp, _ = idx2.shape
    grp_per_w = n_grp // SC_WORKERS
    assert grp_per_w % 2 == 0

    def body(table_hbm, idx_hbm, out_hbm, idx_v, rows_a, rows_b, sem):
        g0 = _sc_worker() * grp_per_w
        pltpu.sync_copy(idx_hbm.at[pl.ds(g0, grp_per_w)], idx_v)

        @pl.loop(0, grp_per_w, step=2)
        def _(j):
            get_a = pltpu.async_copy(table_hbm.at[idx_v.at[j]], rows_a, sem.at[0])
            get_b = pltpu.async_copy(table_hbm.at[idx_v.at[j + 1]], rows_b, sem.at[1])
            get_a.wait()
            put_a = pltpu.async_copy(rows_a, out_hbm.at[pl.ds((g0 + j) * SC_ROWS, SC_ROWS)], sem.at[2])
            get_b.wait()
            put_b = pltpu.async_copy(rows_b, out_hbm.at[pl.ds((g0 + j + 1) * SC_ROWS, SC_ROWS)], sem.at[3])
            put_a.wait()
            put_b.wait()

    return pl.kernel(
        body, out_type=jax.ShapeDtypeStruct((n_grp * SC_ROWS, d), table.dtype), mesh=_sc_mesh(),
        scratch_types=[pltpu.VMEM((grp_per_w, SC_ROWS), I32), pltpu.VMEM((SC_ROWS, d), table.dtype),
                       pltpu.VMEM((SC_ROWS, d), table.dtype), pltpu.SemaphoreType.DMA((4,))],
        name="sc_combine_gather",
    )(table, idx2)


def _ffn_weight_copies(expert, wgu_hbm, wd_hbm, stage_gu, stage_d, sem):
    return (pltpu.make_async_copy(wgu_hbm.at[expert], stage_gu, sem.at[0]),
            pltpu.make_async_copy(wd_hbm.at[expert], stage_d, sem.at[1]))


def _ffn_kernel(be_ref, first_ref, nxt_ref, nv_ref, x_ref, wgu_hbm, wd_hbm, bgu_ref, bd_ref, o_ref,
                stage_gu, stage_d, wgu_b, wd_b, sem):
    step = pl.program_id(0)
    copies = functools.partial(_ffn_weight_copies, wgu_hbm=wgu_hbm, wd_hbm=wd_hbm, stage_gu=stage_gu,
                               stage_d=stage_d, sem=sem)

    @pl.when(step == 0)
    def _():
        for cp in copies(be_ref[0]):
            cp.start()

    for b in range(FFN_STEP_BLOCKS):
        idx = step * FFN_STEP_BLOCKS + b
        expert = be_ref[idx]
        n_valid = nv_ref[idx]
        rows = slice(b * MOE_BLOCK, (b + 1) * MOE_BLOCK)

        @pl.when(first_ref[idx] == 1)
        def _():
            for cp in copies(expert):
                cp.wait()
            wgu_b[...] = stage_gu[...].astype(BF16)
            wd_b[...] = stage_d[...].astype(BF16)

            @pl.when(nxt_ref[idx] >= 0)
            def _():
                for cp in copies(nxt_ref[idx]):
                    cp.start()

        @pl.when(n_valid > 0)
        def _():
            rowid = lax.broadcasted_iota(I32, (MOE_BLOCK, x_ref.shape[1]), 0)
            xp = jnp.where(rowid < n_valid, x_ref[rows, :], 0)
            x = jnp.concatenate(_unpack_halves(xp), axis=-1).astype(BF16)
            gu = jnp.dot(x, wgu_b[...], preferred_element_type=F32) + bgu_ref[expert]
            g_ = jnp.minimum(gu[:, :D_FF], SWIGLU_LIMIT)
            u_ = jnp.clip(gu[:, D_FF:], -SWIGLU_LIMIT, SWIGLU_LIMIT)
            act = (u_ + 1.0) * (g_ * jax.nn.sigmoid(SWIGLU_ALPHA * g_))
            y = jnp.dot(act.astype(BF16), wd_b[...], preferred_element_type=F32) + bd_ref[expert]
            o_ref[rows, :] = _pack_halves(y)

        @pl.when(n_valid <= 0)
        def _():
            o_ref[rows, :] = jnp.zeros((MOE_BLOCK, o_ref.shape[1]), o_ref.dtype)


def _ffn(block_expert, block_first, block_next, block_valid, xin, w_gate_up, b_gate_up, w_down, b_down):
    n_rows, dp = xin.shape
    d = 2 * dp
    step_rows = FFN_STEP_BLOCKS * MOE_BLOCK
    resident = lambda arr: pl.BlockSpec(arr.shape, lambda i, *_: (0,) * arr.ndim)
    grid_spec = pltpu.PrefetchScalarGridSpec(
        num_scalar_prefetch=4,
        grid=(n_rows // step_rows,),
        in_specs=[pl.BlockSpec((step_rows, dp), lambda i, *_: (i, 0)),
                  pl.BlockSpec(memory_space=pl.ANY), pl.BlockSpec(memory_space=pl.ANY),
                  resident(b_gate_up), resident(b_down)],
        out_specs=pl.BlockSpec((step_rows, dp), lambda i, *_: (i, 0)),
        scratch_shapes=[pltpu.VMEM((d, 2 * D_FF), F32), pltpu.VMEM((D_FF, d), F32),
                        pltpu.VMEM((d, 2 * D_FF), BF16), pltpu.VMEM((D_FF, d), BF16),
                        pltpu.SemaphoreType.DMA((2,))],
    )
    return pl.pallas_call(
        _ffn_kernel,
        out_shape=jax.ShapeDtypeStruct((n_rows, dp), I32),
        grid_spec=grid_spec,
        compiler_params=pltpu.CompilerParams(dimension_semantics=("arbitrary",), vmem_limit_bytes=FFN_VMEM_LIMIT),
        name="expert_ffn",
    )(block_expert, block_first, block_next, block_valid, xin, w_gate_up, w_down, b_gate_up, b_down)


def _combine_kernel(x1_ref, g2_ref, gate_ref, yg_ref, o_ref):
    gate = gate_ref[0]
    y_lo = y_hi = None
    for q in range(TOP_K):
        lo, hi = _unpack_halves(yg_ref[q, 0])
        wq = gate[:, q:q + 1]
        y_lo = wq * lo if y_lo is None else y_lo + wq * lo
        y_hi = wq * hi if y_hi is None else y_hi + wq * hi
    o_ref[0] = x1_ref[0] + g2_ref[0] * jnp.concatenate([y_lo, y_hi], axis=-1)


def _combine(x1, gate2, gate, yg):
    bsz, t_len, d = x1.shape
    tm = min(1024, t_len)
    return pl.pallas_call(
        _combine_kernel,
        out_shape=jax.ShapeDtypeStruct((bsz, t_len, d), F32),
        grid=(bsz, t_len // tm),
        in_specs=[pl.BlockSpec((1, tm, d), lambda b, i: (b, i, 0)),
                  pl.BlockSpec((1, 1, d), lambda b, i: (b, 0, 0)),
                  pl.BlockSpec((1, tm, TOP_K), lambda b, i: (b, i, 0)),
                  pl.BlockSpec((TOP_K, 1, tm, d // 2), lambda b, i: (0, b, i, 0))],
        out_specs=pl.BlockSpec((1, tm, d), lambda b, i: (b, i, 0)),
        compiler_params=_cparams(("parallel", "parallel")),
        name="moe_combine",
    )(x1, gate2, gate, yg)


def _moe(x1, gate2, h2, top_idx, gate, rank, counts, w_gate_up, b_gate_up, w_down, b_down, layer):
    bsz, t_len, d = h2.shape
    n_tok = bsz * t_len
    n_assign = n_tok * TOP_K
    n_blocks = -(-n_assign // MOE_BLOCK) + N_EXPERTS
    counts = counts.reshape(N_EXPERTS)
    blocks_e = (counts + MOE_BLOCK - 1) // MOE_BLOCK
    blk_end = jnp.cumsum(blocks_e)
    blk_start = blk_end - blocks_e
    experts = jnp.arange(N_EXPERTS, dtype=I32)
    onehot = top_idx.reshape(n_tok, TOP_K, 1) == experts
    dest = jnp.sum(jnp.where(onehot, blk_start * MOE_BLOCK, 0), axis=-1) + rank.reshape(n_tok, TOP_K)
    dest_t = dest.T.astype(I32)
    blk = jnp.arange(n_blocks, dtype=I32)
    block_expert = jnp.minimum(jnp.sum(blk_end[None, :] <= blk[:, None], axis=1), N_EXPERTS - 1).astype(I32)
    be_hot = block_expert[:, None] == experts
    cnt_b = jnp.sum(jnp.where(be_hot, counts, 0), axis=1)
    start_b = jnp.sum(jnp.where(be_hot, blk_start, 0), axis=1)
    block_valid = jnp.clip(cnt_b - (blk - start_b) * MOE_BLOCK, 0, MOE_BLOCK).astype(I32)
    xin = _sc_scatter_rows(h2.reshape(n_tok, d), dest_t.reshape(TOP_K, n_tok // SC_ROWS, SC_ROWS),
                           n_blocks * MOE_BLOCK)
    block_first = jnp.concatenate([jnp.ones((1,), I32), (block_expert[1:] != block_expert[:-1]).astype(I32)])
    run_start = jnp.where(block_first == 1, blk, n_blocks)
    later_start = lax.cummin(jnp.concatenate([run_start[1:], jnp.full((1,), n_blocks, I32)]), reverse=True)
    next_expert = jnp.concatenate([block_expert, jnp.full((1,), -1 - layer * N_EXPERTS, I32)])[later_start]
    yb = _ffn(block_expert + layer * N_EXPERTS, block_first, next_expert + layer * N_EXPERTS, block_valid, xin,
              w_gate_up, b_gate_up, w_down, b_down)
    yg = _sc_gather_rows(yb, dest_t.reshape(n_assign // SC_ROWS, SC_ROWS))
    return _combine(x1, gate2, gate, yg.reshape(TOP_K, bsz, t_len, d))


def _permute_kernel(w_ref, o_ref):
    o_gm = 0
    o_rw = o_gm + 2 * GM_WIDTH
    o_fox = o_rw + RW_SHIFT_WIDTH
    o_f = o_fox + 3 * FOX_WIDTH
    o_gate = o_f + FOX_HEADS
    w = w_ref[0]
    o_ref[0, :, Z_GATE:Z_FOX] = w[:, o_gate:o_gate + N_BRANCH * D_MODEL].astype(BF16)
    o_ref[0, :, Z_FOX:Z_GM] = w[:, o_fox:o_f].astype(BF16)
    o_ref[0, :, Z_GM:Z_RW] = w[:, o_gm:o_rw].astype(BF16)
    o_ref[0, :, Z_RW:Z_F] = w[:, o_rw:o_fox].astype(BF16)
    tail = jnp.concatenate([w[:, o_f:o_gate], jnp.zeros((w.shape[0], Z_WIDTH - Z_F - FOX_HEADS), F32)], axis=-1)
    o_ref[0, :, Z_F:Z_WIDTH] = tail.astype(BF16)


def _permute_w_in(w_in):
    n_layer, d, w_cols = w_in.shape
    tr = 256
    return pl.pallas_call(
        _permute_kernel,
        out_shape=jax.ShapeDtypeStruct((n_layer, d, Z_WIDTH), BF16),
        grid=(n_layer, d // tr),
        in_specs=[pl.BlockSpec((1, tr, w_cols), lambda l, i: (l, i, 0))],
        out_specs=pl.BlockSpec((1, tr, Z_WIDTH), lambda l, i: (l, i, 0)),
        compiler_params=_cparams(("parallel", "parallel")),
        name="permute_w_in",
    )(w_in)


def _layer(x, mod, w_in_p, gm_v_gain, gm_w_s, gm_b_s, mu_pad, w_lora, rw_w0, rw_a0, rw_k_k, rw_k_a, rw_r_k,
           rw_gn_gain, rw_gn_bias, f_bias_pad, fox_q_gain, fox_k_gain, w_branch, w_o, w_router, b_router,
           w_gate_up, b_gate_up, w_down, b_down, layer):
    shift1, scale1, gate1, shift2, scale2, gate2 = (mod[:, i][:, None, :] for i in range(6))
    z, z_rw = _inproj(x, scale1, shift1, w_in_p, layer)
    y_gm = _gmlp(z, gm_v_gain, gm_w_s, gm_b_s)
    r, lw, k, v, a, b, g = _rwprep(z_rw, mu_pad, w_lora, rw_w0, rw_a0, rw_k_k, rw_k_a)
    y_rw = _rwscan(r, lw, k, v, a, b, g, rw_r_k, rw_gn_gain, rw_gn_bias)
    q, kf, vf = _foxprep(z, z_rw, f_bias_pad, fox_q_gain, fox_k_gain)
    y_fox = _fox(q, kf, vf)
    x1, h2, top_idx, gate, rank, counts = _merge(z, y_gm, y_rw, y_fox, x, gate1, scale2, shift2,
                                                 w_branch, w_o, w_router, b_router, layer)
    return _moe(x1, gate2, h2, top_idx, gate, rank, counts, w_gate_up, b_gate_up, w_down, b_down, layer)


def kernel(x, c, w_ada, b_ada, w_in, gm_v_gain, gm_w_s, gm_b_s, rw_mu, rw_w0, rw_w2, rw_a0, rw_a2, rw_g2, rw_k_k,
           rw_k_a, rw_r_k, rw_gn_gain, rw_gn_bias, fox_f_bias, fox_q_gain, fox_k_gain, w_branch, w_o, w_router,
           b_router, w_gate_up, b_gate_up, w_down, b_down):
    n_layer = w_ada.shape[0]
    bsz = x.shape[0]
    c_pad = jnp.zeros((8, D_MODEL), F32).at[:bsz].set(c)
    mod = _adaln(c_pad, w_ada, b_ada)[:, :bsz].reshape(n_layer, bsz, 6, D_MODEL)
    w_in_p = _permute_w_in(w_in)
    mu_pad = jnp.pad(rw_mu, ((0, 0), (0, RW_BLOCK - RW_SHIFT_WIDTH)))
    w_lora = jnp.zeros((n_layer, RW_LORA, 3 * RW_WIDTH), F32)
    w_lora = w_lora.at[:, 0:RW_DECAY_LORA, 0:RW_WIDTH].set(rw_w2)
    w_lora = w_lora.at[:, RW_DECAY_LORA:RW_DECAY_LORA + RW_ICLR_LORA, RW_WIDTH:2 * RW_WIDTH].set(rw_a2)
    w_lora = w_lora.at[:, RW_DECAY_LORA + RW_ICLR_LORA:, 2 * RW_WIDTH:].set(rw_g2)
    f_bias_pad = jnp.pad(fox_f_bias, ((0, 0), (0, LANES - FOX_HEADS)))
    w_in_p = w_in_p.reshape(n_layer * D_MODEL, Z_WIDTH)
    w_branch_b = w_branch.astype(BF16).reshape(n_layer * MIX_WIDTH, D_MODEL)
    w_o_b = w_o.astype(BF16).reshape(n_layer * D_MODEL, D_MODEL)
    w_gu = w_gate_up.reshape(n_layer * N_EXPERTS, D_MODEL, 2 * D_FF)
    b_gu = b_gate_up.reshape(n_layer * N_EXPERTS, 1, 2 * D_FF)
    w_dn = w_down.reshape(n_layer * N_EXPERTS, D_FF, D_MODEL)
    b_dn = b_down.reshape(n_layer * N_EXPERTS, 1, D_MODEL)
    for l in range(n_layer):
        x = _layer(x, mod[l], w_in_p, gm_v_gain[l], gm_w_s[l], gm_b_s[l], mu_pad[l:l + 1], w_lora[l], rw_w0[l],
                   rw_a0[l], rw_k_k[l], rw_k_a[l], rw_r_k[l], rw_gn_gain[l], rw_gn_bias[l], f_bias_pad[l:l + 1],
                   fox_q_gain[l], fox_k_gain[l], w_branch_b, w_o_b, w_router[l], b_router[l],
                   w_gu, b_gu, w_dn, b_dn, l)
    return x
```

```python
import functools

import jax
import jax.numpy as jnp
import numpy as np
from jax import lax
from jax.experimental import pallas as pl
from jax.experimental.pallas import tpu as pltpu
from jax.experimental.pallas import tpu_sc as plsc

F32 = jnp.float32
BF16 = jnp.bfloat16
I32 = jnp.int32
HIGHEST = lax.Precision.HIGHEST

D_MODEL = 1024
GM_CHUNK = 128
GM_GROUPS = 4
GM_WIDTH = 256
GM_GROUP_DIM = GM_WIDTH // GM_GROUPS
RW_HEADS = 4
RW_HEAD_DIM = 64
RW_WIDTH = RW_HEADS * RW_HEAD_DIM
RW_DECAY_LORA = 32
RW_ICLR_LORA = 32
RW_GATE_LORA = 64
RW_LORA = RW_DECAY_LORA + RW_ICLR_LORA + RW_GATE_LORA
RW_SHIFT_WIDTH = 3 * RW_WIDTH + RW_LORA
RW_GN_EPS = 64e-5
FOX_HEADS = 8
FOX_HEAD_DIM = 64
FOX_WIDTH = FOX_HEADS * FOX_HEAD_DIM
ATTN_SCALE = FOX_HEAD_DIM ** -0.5
MASK_VALUE = -1e30
LOG2E = 1.4426950408889634
N_BRANCH = 3
MIX_WIDTH = GM_WIDTH + RW_WIDTH + FOX_WIDTH
N_EXPERTS = 32
TOP_K = 4
D_FF = D_MODEL
SWIGLU_LIMIT = 7.0
SWIGLU_ALPHA = 1.702
MOE_BLOCK = 256
EPS = 1e-6

Z_GATE = 0
Z_FOX = N_BRANCH * D_MODEL
Z_GM = Z_FOX + 3 * FOX_WIDTH
Z_RW = Z_GM + 2 * GM_WIDTH
RW_BLOCK = 1024
Z_F = Z_RW + RW_SHIFT_WIDTH
Z_WIDTH = Z_RW + RW_BLOCK
LANES = 128
RW_CHUNK = 64
RW_PREP_UNROLL = 4

VMEM_LIMIT = 48 * 1024 * 1024
FFN_VMEM_LIMIT = 56 * 1024 * 1024
FFN_STEP_BLOCKS = 4
SC_CORES = 2
SC_SUBCORES = 16
SC_WORKERS = SC_CORES * SC_SUBCORES
SC_ROWS = 64


def _cparams(sem):
    return pltpu.CompilerParams(dimension_semantics=sem, vmem_limit_bytes=VMEM_LIMIT)


def _mm(a, b):
    return jnp.dot(a.astype(BF16), b.astype(BF16), preferred_element_type=F32)


def _mm_nt(a, b):
    return lax.dot_general(a.astype(BF16), b.astype(BF16), (((1,), (1,)), ((), ())), preferred_element_type=F32)


def _mm_tn(a, b):
    return lax.dot_general(a.astype(BF16), b.astype(BF16), (((0,), (0,)), ((), ())), preferred_element_type=F32)


def _split3(x):
    hi = x.astype(BF16)
    r1 = x - hi.astype(F32)
    mid = r1.astype(BF16)
    lo = (r1 - mid.astype(F32)).astype(BF16)
    return hi, mid, lo


def _tri_cumsum(x, n):
    ri = lax.broadcasted_iota(I32, (n, n), 0)
    ci = lax.broadcasted_iota(I32, (n, n), 1)
    ones = jnp.where(ri >= ci, 1.0, 0.0).astype(BF16)
    hi, mid, lo = _split3(x)
    return (jnp.dot(ones, hi, preferred_element_type=F32) + jnp.dot(ones, mid, preferred_element_type=F32)
            + jnp.dot(ones, lo, preferred_element_type=F32))


def _pack_halves(x):
    w = x.shape[1] // 2
    hi = pltpu.bitcast(x[:, :w].astype(BF16).astype(F32), jnp.uint32)
    lo = pltpu.bitcast(x[:, w:].astype(BF16).astype(F32), jnp.uint32)
    return pltpu.bitcast(hi | (lo >> 16), I32)


def _unpack_halves(p):
    u = pltpu.bitcast(p, jnp.uint32)
    return pltpu.bitcast(u & jnp.uint32(0xFFFF0000), F32), pltpu.bitcast(u << 16, F32)


def _log_sigmoid(x):
    return jnp.minimum(x, 0.0) - jnp.log1p(jnp.exp(-jnp.abs(x)))


def _adaln_kernel(c_ref, w_ref, b_ref, o_ref):
    c = c_ref[...]
    s = c * jax.nn.sigmoid(c)
    o_ref[0] = jnp.dot(s, w_ref[0], preferred_element_type=F32, precision=HIGHEST) + b_ref[0]


def _adaln(c_pad, w_ada, b_ada):
    n_layer, d, w6 = w_ada.shape
    tn = 1536
    return pl.pallas_call(
        _adaln_kernel,
        out_shape=jax.ShapeDtypeStruct((n_layer, c_pad.shape[0], w6), F32),
        grid=(n_layer, w6 // tn),
        in_specs=[pl.BlockSpec(c_pad.shape, lambda l, j: (0, 0)),
                  pl.BlockSpec((1, d, tn), lambda l, j: (l, 0, j)),
                  pl.BlockSpec((1, 1, tn), lambda l, j: (l, 0, j))],
        out_specs=pl.BlockSpec((1, c_pad.shape[0], tn), lambda l, j: (l, 0, j)),
        compiler_params=_cparams(("parallel", "parallel")),
        name="adaln",
    )(c_pad, w_ada, b_ada.reshape(n_layer, 1, w6))


def _inproj_kernel(x_ref, sc_ref, sh_ref, w_ref, zm_ref, zr_ref, xn_ref):
    j = pl.program_id(2)

    @pl.when(j == 0)
    def _():
        x = x_ref[0]
        xn = x * lax.rsqrt(jnp.mean(x * x, axis=-1, keepdims=True) + EPS)
        xn_ref[...] = (xn * (1.0 + sc_ref[0]) + sh_ref[0]).astype(BF16)

    acc = jnp.dot(xn_ref[...], w_ref[...], preferred_element_type=F32)

    @pl.when(j < Z_RW // RW_BLOCK)
    def _():
        zm_ref[0] = acc.astype(BF16)

    @pl.when(j == Z_RW // RW_BLOCK)
    def _():
        zr_ref[0] = acc


def _inproj(x, scale, shift, w, layer):
    bsz, t_len, d = x.shape
    tm = min(1024, t_len)
    tn = RW_BLOCK
    n_main = Z_RW // tn
    return pl.pallas_call(
        _inproj_kernel,
        out_shape=(jax.ShapeDtypeStruct((bsz, t_len, Z_RW), BF16), jax.ShapeDtypeStruct((bsz, t_len, RW_BLOCK), F32)),
        grid=(bsz, t_len // tm, Z_WIDTH // tn),
        in_specs=[pl.BlockSpec((1, tm, d), lambda b, i, j: (b, i, 0)),
                  pl.BlockSpec((1, 1, d), lambda b, i, j: (b, 0, 0)),
                  pl.BlockSpec((1, 1, d), lambda b, i, j: (b, 0, 0)),
                  pl.BlockSpec((d, tn), lambda b, i, j: (layer, j))],
        out_specs=(pl.BlockSpec((1, tm, tn), lambda b, i, j: (b, i, jnp.minimum(j, n_main - 1))),
                   pl.BlockSpec((1, tm, tn), lambda b, i, j: (b, i, 0))),
        scratch_shapes=[pltpu.VMEM((tm, d), BF16)],
        compiler_params=_cparams(("parallel", "parallel", "arbitrary")),
        name="inproj",
    )(x, scale, shift, w)


def _gmlp_kernel(z_ref, gain_ref, ws_ref, bst_ref, o_ref):
    tm = z_ref.shape[1]
    z = z_ref[0].astype(F32)
    u = jax.nn.gelu(z[:, :GM_WIDTH])
    v = jax.nn.gelu(z[:, GM_WIDTH:])
    v = v * lax.rsqrt(jnp.mean(v * v, axis=-1, keepdims=True) + EPS) * gain_ref[...]
    vb = v.astype(BF16)
    grp = lax.broadcasted_iota(I32, (GM_CHUNK, GM_WIDTH), 1) // GM_GROUP_DIM
    ri = lax.broadcasted_iota(I32, (GM_CHUNK, GM_CHUNK), 0)
    ci = lax.broadcasted_iota(I32, (GM_CHUNK, GM_CHUNK), 1)
    causal = ri >= ci
    bias = jnp.zeros((GM_CHUNK, GM_WIDTH), F32)
    ws = []
    for g in range(GM_GROUPS):
        ws.append(jnp.where(causal, ws_ref[g], 0.0).astype(BF16))
        bias = jnp.where(grp == g, bst_ref[:, g:g + 1], bias)
    for c in range(tm // GM_CHUNK):
        rows = slice(c * GM_CHUNK, (c + 1) * GM_CHUNK)
        vc = vb[rows]
        mixed = bias
        for g in range(GM_GROUPS):
            m = jnp.dot(ws[g], vc, preferred_element_type=F32)
            mixed = mixed + jnp.where(grp == g, m, 0.0)
        o_ref[0, rows, :] = (u[rows] * mixed).astype(o_ref.dtype)


def _gmlp(z, gain, w_s, b_s):
    bsz, t_len, _ = z.shape
    tm = min(1024, t_len)
    return pl.pallas_call(
        _gmlp_kernel,
        out_shape=jax.ShapeDtypeStruct((bsz, t_len, GM_WIDTH), BF16),
        grid=(bsz, t_len // tm),
        in_specs=[pl.BlockSpec((1, tm, 2 * GM_WIDTH), lambda b, i: (b, i, Z_GM // (2 * GM_WIDTH))),
                  pl.BlockSpec((1, GM_WIDTH), lambda b, i: (0, 0)),
                  pl.BlockSpec((GM_GROUPS, GM_CHUNK, GM_CHUNK), lambda b, i: (0, 0, 0)),
                  pl.BlockSpec((GM_CHUNK, GM_GROUPS), lambda b, i: (0, 0))],
        out_specs=pl.BlockSpec((1, tm, GM_WIDTH), lambda b, i: (b, i, 0)),
        compiler_params=_cparams(("parallel", "parallel")),
        name="gmlp",
    )(z, gain.reshape(1, GM_WIDTH), w_s, b_s.T)


def _rwprep_kernel(z_ref, zp_ref, mu_ref, wl_ref, w0_ref, a0_ref, kk_ref, ka_ref,
                   r_o, lw_o, k_o, v_o, a_o, b_o, g_o):
    tm = z_ref.shape[1]
    z = z_ref[0]
    prev = jnp.where(pl.program_id(1) > 0, zp_ref[0, 7:8, :], 0.0)
    rowid = lax.broadcasted_iota(I32, z.shape, 0)
    zs = jnp.where(rowid == 0, prev, pltpu.roll(z, 1, axis=0))
    zz = z + mu_ref[...] * (zs - z)
    r = zz[:, 0:RW_WIDTH]
    k = zz[:, RW_WIDTH:2 * RW_WIDTH]
    v = zz[:, 2 * RW_WIDTH:3 * RW_WIDTH]
    lo = zz[:, 3 * RW_WIDTH:3 * RW_WIDTH + RW_LORA]
    lane = lax.broadcasted_iota(I32, (tm, RW_LORA), 1)
    act = jnp.where(lane < RW_DECAY_LORA, jnp.tanh(lo),
                    jnp.where(lane < RW_DECAY_LORA + RW_ICLR_LORA, lo, jax.nn.sigmoid(lo)))
    a_hi, a_lo, _ = _split3(act)
    w_hi, w_lo, _ = _split3(wl_ref[...])
    proj = (jnp.dot(a_hi, w_hi, preferred_element_type=F32) + jnp.dot(a_hi, w_lo, preferred_element_type=F32)
            + jnp.dot(a_lo, w_hi, preferred_element_type=F32))
    xw = -(w0_ref[...] + proj[:, 0:RW_WIDTH])
    softplus = jnp.maximum(xw, 0.0) + jnp.log1p(jnp.exp(-jnp.abs(xw)))
    lw = -jnp.exp(-softplus - 0.5)
    a = jax.nn.sigmoid(a0_ref[...] + proj[:, RW_WIDTH:2 * RW_WIDTH])
    g = proj[:, 2 * RW_WIDTH:3 * RW_WIDTH]
    kk = k * kk_ref[...]
    k2 = k * (1.0 + (a - 1.0) * ka_ref[...])
    for h in range(RW_HEADS):
        sl = slice(h * RW_HEAD_DIM, (h + 1) * RW_HEAD_DIM)
        kkh = kk[:, sl]
        nrm = jnp.sqrt(jnp.sum(kkh * kkh, axis=-1, keepdims=True))
        kkh = kkh / jnp.maximum(nrm, 1e-12)
        r_o[0, h] = r[:, sl]
        lw_o[0, h] = lw[:, sl]
        k_o[0, h] = k2[:, sl]
        v_o[0, h] = v[:, sl]
        a_o[0, h] = -kkh
        b_o[0, h] = kkh * a[:, sl]
        g_o[0, h] = g[:, sl]


def _rwprep(z, mu_pad, w_lora, w0, a0, k_k, k_a):
    bsz, t_len, _ = z.shape
    tm = min(1024, t_len)
    hm = jax.ShapeDtypeStruct((bsz, RW_HEADS, t_len, RW_HEAD_DIM), F32)
    hm_spec = pl.BlockSpec((1, RW_HEADS, tm, RW_HEAD_DIM), lambda b, i: (b, 0, i, 0))
    vec = lambda n: pl.BlockSpec((1, n), lambda b, i: (0, 0))
    rw_blk = 0
    return pl.pallas_call(
        _rwprep_kernel,
        out_shape=(hm,) * 7,
        grid=(bsz, t_len // tm),
        in_specs=[pl.BlockSpec((1, tm, RW_BLOCK), lambda b, i: (b, i, rw_blk)),
                  pl.BlockSpec((1, 8, RW_BLOCK), lambda b, i: (b, jnp.maximum(i * (tm // 8) - 1, 0), rw_blk)),
                  vec(RW_BLOCK),
                  pl.BlockSpec((RW_LORA, 3 * RW_WIDTH), lambda b, i: (0, 0)),
                  vec(RW_WIDTH), vec(RW_WIDTH), vec(RW_WIDTH), vec(RW_WIDTH)],
        out_specs=(hm_spec,) * 7,
        compiler_params=_cparams(("parallel", "parallel")),
        name="rwprep",
    )(z, z, mu_pad, w_lora, w0.reshape(1, -1), a0.reshape(1, -1), k_k.reshape(1, -1), k_a.reshape(1, -1))


def _rwscan_kernel(r_ref, lw_ref, k_ref, v_ref, a_ref, b_ref, g_ref, rk_ref, gg_ref, gb_ref, o_ref,
                   s_ref, rp_ref, y_ref, gm_ref, h0_ref, we_ref):
    cl = RW_CHUNK
    tb = r_ref.shape[2]
    n_chunk = tb // cl

    @pl.when(pl.program_id(1) == 0)
    def _():
        s_ref[...] = jnp.zeros_like(s_ref)

    n = RW_HEADS * cl
    ri = lax.broadcasted_iota(I32, (n, n), 0)
    ci = lax.broadcasted_iota(I32, (n, n), 1)
    same_head = (ri // cl) == (ci // cl)
    lower = same_head & (ri >= ci)
    strict = same_head & (ri > ci)
    eye = jnp.where(ri == ci, 1.0, 0.0)
    ones_lower = jnp.where(lower, 1.0, 0.0).astype(BF16)

    def prepare(chunks):
        grp = range(len(chunks))
        each = lambda fn: [fn(u) for u in grp]
        rows = [pl.ds(pl.multiple_of(c * cl, cl), cl) for c in chunks]
        stack = lambda ref: each(lambda u: ref[0, :, rows[u], :].reshape(n, RW_HEAD_DIM))
        r, lw, k, v, a, b = (stack(ref) for ref in (r_ref, lw_ref, k_ref, v_ref, a_ref, b_ref))
        hd = RW_HEAD_DIM
        parts = each(lambda u: jnp.concatenate(_split3(lw[u]), axis=-1))
        sums = each(lambda u: jnp.dot(ones_lower, parts[u], preferred_element_type=F32))
        cw = each(lambda u: sums[u][:, :hd] + sums[u][:, hd:2 * hd] + sums[u][:, 2 * hd:])
        w_in = each(lambda u: jnp.exp(cw[u]))
        w_inv = each(lambda u: jnp.exp(-cw[u]))
        rt = each(lambda u: r[u] * w_in[u])
        at = each(lambda u: a[u] * jnp.exp(cw[u] - lw[u]))
        kt = each(lambda u: k[u] * w_inv[u])
        bt = each(lambda u: b[u] * w_inv[u])
        w_end = each(lambda u: w_in[u].reshape(RW_HEADS, cl, RW_HEAD_DIM)[:, cl - 1:cl, :])
        w_end_rows = each(lambda u: jnp.broadcast_to(w_end[u], (RW_HEADS, cl, RW_HEAD_DIM)).reshape(n, RW_HEAD_DIM))
        a_ab = each(lambda u: jnp.where(strict, _mm_nt(at[u], bt[u]), 0.0))
        a_ak = each(lambda u: jnp.where(strict, _mm_nt(at[u], kt[u]), 0.0))
        m_rb = each(lambda u: jnp.where(lower, _mm_nt(rt[u], bt[u]), 0.0))
        m_rk = each(lambda u: jnp.where(lower, _mm_nt(rt[u], kt[u]), 0.0))
        inv = each(lambda u: eye + a_ab[u])
        p = a_ab
        for _ in range(cl.bit_length() - 2):
            p = [_mm(p[u], p[u]) for u in grp]
            inv = [inv[u] + _mm(inv[u], p[u]) for u in grp]
        akv = each(lambda u: _mm(a_ak[u], v[u]))
        apz = each(lambda u: _mm(inv[u], jnp.concatenate([at[u], akv[u]], axis=-1)).astype(BF16))
        mix = each(lambda u: jnp.dot(m_rb[u].astype(BF16), apz[u], preferred_element_type=F32))
        bend = each(lambda u: bt[u] * w_end_rows[u])
        kend = each(lambda u: kt[u] * w_end_rows[u])
        rp = each(lambda u: (rt[u] + mix[u][:, :hd]).astype(BF16))
        y0 = each(lambda u: mix[u][:, hd:] + _mm(m_rk[u], v[u]))
        for u in grp:
            for h in range(RW_HEADS):
                hs = slice(h * cl, (h + 1) * cl)
                both = _mm_tn(apz[u][hs], bend[u][hs])
                rp_ref[h, rows[u], :] = rp[u][hs]
                y_ref[h, rows[u], :] = y0[u][hs]
                gm_ref[h, rows[u], :] = both[:hd].astype(BF16)
                h0_ref[h, rows[u], :] = both[hd:] + _mm_tn(v[u][hs], kend[u][hs])
                we_ref[h, chunks[u]] = w_end[u][h]

    def prepare_step(i, carry):
        prepare([i * RW_PREP_UNROLL + u for u in range(RW_PREP_UNROLL)])
        return carry

    lax.fori_loop(0, n_chunk // RW_PREP_UNROLL, prepare_step, 0)

    def advance(c, carry):
        rows = pl.ds(pl.multiple_of(c * cl, cl), cl)
        for h in range(RW_HEADS):
            s = s_ref[h]
            sb = s.astype(BF16)
            y_ref[h, rows, :] = y_ref[h, rows, :] + lax.dot_general(
                rp_ref[h, rows, :], sb, (((1,), (1,)), ((), ())), preferred_element_type=F32)
            s_ref[h] = (s * we_ref[h, c] + jnp.dot(sb, gm_ref[h, rows, :], preferred_element_type=F32)
                        + h0_ref[h, rows, :])
        return carry

    lax.fori_loop(0, n_chunk, advance, 0)

    for h in range(RW_HEADS):
        y = y_ref[h]
        mu = jnp.mean(y, axis=-1, keepdims=True)
        yc = y - mu
        var = jnp.mean(yc * yc, axis=-1, keepdims=True)
        yn = yc * lax.rsqrt(var + RW_GN_EPS) * gg_ref[h] + gb_ref[h]
        v = v_ref[0, h]
        bonus = jnp.sum(r_ref[0, h] * k_ref[0, h] * rk_ref[h], axis=-1, keepdims=True) * v
        o_ref[0, h] = ((yn + bonus) * g_ref[0, h]).astype(o_ref.dtype)


def _rwscan(r, lw, k, v, a, b, g, r_k, gn_gain, gn_bias):
    bsz, _, t_len, _ = r.shape
    tb = min(512, t_len)
    hm_spec = pl.BlockSpec((1, RW_HEADS, tb, RW_HEAD_DIM), lambda bi, i: (bi, 0, i, 0))
    par = pl.BlockSpec((RW_HEADS, 1, RW_HEAD_DIM), lambda bi, i: (0, 0, 0))
    hshape = (RW_HEADS, 1, RW_HEAD_DIM)
    return pl.pallas_call(
        _rwscan_kernel,
        out_shape=jax.ShapeDtypeStruct((bsz, RW_HEADS, t_len, RW_HEAD_DIM), BF16),
        grid=(bsz, t_len // tb),
        in_specs=[hm_spec] * 7 + [par] * 3,
        out_specs=hm_spec,
        scratch_shapes=[pltpu.VMEM((RW_HEADS, RW_HEAD_DIM, RW_HEAD_DIM), F32),
                        pltpu.VMEM((RW_HEADS, tb, RW_HEAD_DIM), BF16), pltpu.VMEM((RW_HEADS, tb, RW_HEAD_DIM), F32),
                        pltpu.VMEM((RW_HEADS, tb, RW_HEAD_DIM), BF16), pltpu.VMEM((RW_HEADS, tb, RW_HEAD_DIM), F32),
                        pltpu.VMEM((RW_HEADS, tb // RW_CHUNK, 1, RW_HEAD_DIM), F32)],
        compiler_params=_cparams(("parallel", "arbitrary")),
        name="rwscan",
    )(r, lw, k, v, a, b, g, r_k.reshape(hshape), gn_gain.reshape(hshape), gn_bias.reshape(hshape))


FOX_PAIRS = FOX_HEADS // 2
FOX_EXTRA = 3
FOX_ACC_ROWS = FOX_HEAD_DIM + 16
FOX_Q_SPLIT = 2
FOX_LOOKAHEAD = 4


def _fox_bias_selector():
    sel = np.zeros((LANES, 2 * FOX_HEADS * LANES), np.float32)
    for h in range(FOX_HEADS):
        base = FOX_HEAD_DIM if h % 2 == 0 else 0
        for p in range(FOX_EXTRA):
            sel[p * FOX_HEADS + h, h * LANES + base + p] = 1.0
            sel[p * FOX_HEADS + h, (FOX_HEADS + h) * LANES + base + FOX_EXTRA + p] = -1.0
    return sel


def _foxprep_kernel(z_ref, f_ref, fb_ref, qg_ref, kg_ref, sel_ref, q_o, k_o, vt_o, carry_ref):
    tm = z_ref.shape[1]

    @pl.when(pl.program_id(1) == 0)
    def _():
        carry_ref[...] = jnp.zeros_like(carry_ref)

    log_f = _log_sigmoid(f_ref[0] + fb_ref[...])
    cum = carry_ref[...] + _tri_cumsum(log_f, tm)
    carry_ref[...] = cum[tm - 1:tm, :]
    lane = lax.broadcasted_iota(I32, (tm, LANES), 1)
    hi, mid, lo = (p.astype(F32) for p in _split3(cum * LOG2E))
    packed = jnp.where(lane < FOX_HEADS, hi,
                       jnp.where(lane < 2 * FOX_HEADS, pltpu.roll(mid, FOX_HEADS, axis=1),
                                 pltpu.roll(lo, 2 * FOX_HEADS, axis=1)))
    packed = jnp.where(lane < FOX_EXTRA * FOX_HEADS, packed, 0.0).astype(BF16)
    extra = jnp.dot(packed, sel_ref[...], preferred_element_type=F32)

    left = lane < FOX_HEAD_DIM
    in_half = lane % FOX_HEAD_DIM
    ones_q = jnp.where((in_half >= FOX_EXTRA) & (in_half < 2 * FOX_EXTRA), 1.0, 0.0)
    ones_k = jnp.where(in_half < FOX_EXTRA, 1.0, 0.0)

    def normed(block, gain):
        sq = block * block
        s_left = jnp.sum(jnp.where(left, sq, 0.0), axis=-1, keepdims=True)
        s_right = jnp.sum(jnp.where(left, 0.0, sq), axis=-1, keepdims=True)
        ms = jnp.where(left, s_left, s_right) * (1.0 / FOX_HEAD_DIM)
        return block * lax.rsqrt(ms + EPS) * gain

    for j in range(FOX_PAIRS):
        qn = normed(z_ref[0, :, j * LANES:(j + 1) * LANES].astype(F32), qg_ref[...] * (ATTN_SCALE * LOG2E))
        kn = normed(z_ref[0, :, FOX_WIDTH + j * LANES:FOX_WIDTH + (j + 1) * LANES].astype(F32), kg_ref[...])
        for par in range(2):
            h = 2 * j + par
            own = left if par == 0 else jnp.logical_not(left)
            q_o[0, h] = jnp.where(own, qn, extra[:, h * LANES:(h + 1) * LANES] + ones_q).astype(BF16)
            k_o[0, h] = jnp.where(own, kn, extra[:, (FOX_HEADS + h) * LANES:(FOX_HEADS + h + 1) * LANES]
                                  + ones_k).astype(BF16)
    ri = lax.broadcasted_iota(I32, (FOX_WIDTH, FOX_WIDTH), 0)
    ci = lax.broadcasted_iota(I32, (FOX_WIDTH, FOX_WIDTH), 1)
    eye = jnp.where(ri == ci, 1.0, 0.0).astype(BF16)
    v = z_ref[0, :, 2 * FOX_WIDTH:3 * FOX_WIDTH].astype(BF16)
    vt_o[0] = lax.dot_general(eye, v, (((1,), (1,)), ((), ())), preferred_element_type=F32).astype(BF16)


def _foxprep(z, z_rw, f_bias_pad, q_gain, k_gain):
    bsz, t_len, _ = z.shape
    tm = min(512, t_len)
    qk = jax.ShapeDtypeStruct((bsz, FOX_HEADS, t_len, LANES), BF16)
    qk_spec = pl.BlockSpec((1, FOX_HEADS, tm, LANES), lambda b, i: (b, 0, i, 0))
    sel = jnp.asarray(_fox_bias_selector(), BF16)
    return pl.pallas_call(
        _foxprep_kernel,
        out_shape=(qk, qk, jax.ShapeDtypeStruct((bsz, FOX_WIDTH, t_len), BF16)),
        grid=(bsz, t_len // tm),
        in_specs=[pl.BlockSpec((1, tm, 3 * FOX_WIDTH), lambda b, i: (b, i, Z_FOX // (3 * FOX_WIDTH))),
                  pl.BlockSpec((1, tm, LANES), lambda b, i: (b, i, (Z_F - Z_RW) // LANES)),
                  pl.BlockSpec((1, LANES), lambda b, i: (0, 0)),
                  pl.BlockSpec((1, LANES), lambda b, i: (0, 0)),
                  pl.BlockSpec((1, LANES), lambda b, i: (0, 0)),
                  pl.BlockSpec(sel.shape, lambda b, i: (0, 0))],
        out_specs=(qk_spec, qk_spec, pl.BlockSpec((1, FOX_WIDTH, tm), lambda b, i: (b, 0, i))),
        scratch_shapes=[pltpu.VMEM((1, LANES), F32)],
        compiler_params=_cparams(("parallel", "arbitrary")),
        name="foxprep",
    )(z, z_rw, f_bias_pad, jnp.tile(q_gain.reshape(1, -1), (1, 2)), jnp.tile(k_gain.reshape(1, -1), (1, 2)), sel)


def _fox_kernel(qi_ref, kj_ref, q_ref, k_ref, vt_ref, o_ref, m_ref, acc_ref):
    i = qi_ref[pl.program_id(1)]
    j = kj_ref[pl.program_id(1)]
    tq = q_ref.shape[2]
    tk = k_ref.shape[2]
    sub = 8

    @pl.when(j == 0)
    def _():
        m_ref[...] = jnp.full_like(m_ref, MASK_VALUE)
        acc_ref[...] = jnp.zeros_like(acc_ref)

    ones_rows = jnp.ones((FOX_ACC_ROWS - FOX_HEAD_DIM, tk), BF16)

    tw = tq // FOX_Q_SPLIT
    units = [(h, c) for h in range(FOX_HEADS) for c in range(FOX_Q_SPLIT)]

    def scores(unit):
        h, c = unit
        return lax.dot_general(k_ref[0, h], q_ref[0, h, c * tw:(c + 1) * tw, :], (((1,), (1,)), ((), ())),
                               preferred_element_type=F32)

    def update(diagonal):
        if diagonal:
            key = lax.broadcasted_iota(I32, (tk, tw), 0)
            qry = lax.broadcasted_iota(I32, (tk, tw), 1)
        ahead = [scores(u) for u in units[:FOX_LOOKAHEAD]]
        for n, (h, c) in enumerate(units):
            cols = slice(c * tw, (c + 1) * tw)
            s = ahead.pop(0)
            if n + FOX_LOOKAHEAD < len(units):
                ahead.append(scores(units[n + FOX_LOOKAHEAD]))
            if diagonal:
                s = jnp.where(key <= qry + c * tw, s, MASK_VALUE)
            s3 = s.reshape(tk // sub, sub, tw)
            m_prev = m_ref[h, :, cols]
            m_cur = jnp.max(jnp.max(s3, axis=0), axis=0, keepdims=True)
            m_new = jnp.maximum(m_prev, m_cur)
            alpha = jnp.exp2(m_prev - m_new)
            p = jnp.exp2(s3 - m_new[None]).astype(BF16).reshape(tk, tw)
            lhs = jnp.concatenate([vt_ref[0, h * FOX_HEAD_DIM:(h + 1) * FOX_HEAD_DIM, :], ones_rows], axis=0)
            pv = jnp.dot(lhs, p, preferred_element_type=F32)
            acc = acc_ref[h, :, cols].reshape(FOX_ACC_ROWS // sub, sub, tw) * alpha[None]
            acc_ref[h, :, cols] = acc.reshape(FOX_ACC_ROWS, tw) + pv
            m_ref[h, :, cols] = m_new

    @pl.when(j < i)
    def _():
        update(False)

    @pl.when(j == i)
    def _():
        update(True)
        outs = []
        for h in range(FOX_HEADS):
            acc = acc_ref[h]
            outs.append((acc[:FOX_HEAD_DIM] / acc[FOX_HEAD_DIM:FOX_HEAD_DIM + 1]).astype(BF16))
        out_t = jnp.concatenate(outs, axis=0)
        ri = lax.broadcasted_iota(I32, (tq, tq), 0)
        ci = lax.broadcasted_iota(I32, (tq, tq), 1)
        eye = jnp.where(ri == ci, 1.0, 0.0).astype(BF16)
        o_ref[0] = lax.dot_general(eye, out_t, (((1,), (1,)), ((), ())),
                                   preferred_element_type=F32).astype(o_ref.dtype)


def _fox(q, k, vt):
    bsz, _, t_len, _ = q.shape
    tq = min(512, t_len)
    n_blk = t_len // tq
    pairs = [(i, j) for i in range(n_blk) for j in range(i + 1)]
    qi = jnp.asarray([p[0] for p in pairs], I32)
    kj = jnp.asarray([p[1] for p in pairs], I32)
    grid_spec = pltpu.PrefetchScalarGridSpec(
        num_scalar_prefetch=2,
        grid=(bsz, len(pairs)),
        in_specs=[pl.BlockSpec((1, FOX_HEADS, tq, LANES), lambda b, s, qi, kj: (b, 0, qi[s], 0)),
                  pl.BlockSpec((1, FOX_HEADS, tq, LANES), lambda b, s, qi, kj: (b, 0, kj[s], 0)),
                  pl.BlockSpec((1, FOX_WIDTH, tq), lambda b, s, qi, kj: (b, 0, kj[s]))],
        out_specs=pl.BlockSpec((1, tq, FOX_WIDTH), lambda b, s, qi, kj: (b, qi[s], 0)),
        scratch_shapes=[pltpu.VMEM((FOX_HEADS, 8, tq), F32), pltpu.VMEM((FOX_HEADS, FOX_ACC_ROWS, tq), F32)],
    )
    return pl.pallas_call(
        _fox_kernel,
        out_shape=jax.ShapeDtypeStruct((bsz, t_len, FOX_WIDTH), BF16),
        grid_spec=grid_spec,
        compiler_params=_cparams(("parallel", "arbitrary")),
        name="fox",
    )(qi, kj, q, k, vt)


def _merge_kernel(zg_ref, ygm_ref, yrw_ref, yfox_ref, x_ref, g1_ref, sc2_ref, sh2_ref, pb_ref, wo_ref, wr_ref, br_ref,
                  x1_o, h2_o, idx_o, gate_o, rank_o, cnt_o, carry_ref):
    tm = x_ref.shape[1]

    @pl.when((pl.program_id(0) == 0) & (pl.program_id(1) == 0))
    def _():
        carry_ref[...] = jnp.zeros_like(carry_ref)

    sg = 0.5 * jnp.tanh(0.5 * zg_ref[0].astype(F32)) + 0.5
    p_gm = jnp.dot(ygm_ref[0], pb_ref[0:GM_WIDTH, :], preferred_element_type=F32)
    y_rw = jnp.concatenate([yrw_ref[0, h] for h in range(RW_HEADS)], axis=-1)
    p_rw = jnp.dot(y_rw, pb_ref[GM_WIDTH:GM_WIDTH + RW_WIDTH, :], preferred_element_type=F32)
    p_fox = jnp.dot(yfox_ref[0], pb_ref[GM_WIDTH + RW_WIDTH:, :], preferred_element_type=F32)
    merged = sg[:, 0:D_MODEL] * p_gm + sg[:, D_MODEL:2 * D_MODEL] * p_rw + sg[:, 2 * D_MODEL:] * p_fox
    x1 = x_ref[0] + g1_ref[0] * jnp.dot(merged.astype(BF16), wo_ref[...], preferred_element_type=F32)
    x1_o[0] = x1
    h2 = x1 * lax.rsqrt(jnp.mean(x1 * x1, axis=-1, keepdims=True) + EPS) * (1.0 + sc2_ref[0]) + sh2_ref[0]
    h2_o[0] = _pack_halves(h2)

    h_hi, h_lo, _ = _split3(h2)
    w_hi, w_lo, _ = _split3(wr_ref[...])
    logits = (jnp.dot(h_hi, w_hi, preferred_element_type=F32) + jnp.dot(h_hi, w_lo, preferred_element_type=F32)
              + jnp.dot(h_lo, w_hi, preferred_element_type=F32)) + br_ref[...]
    lane = lax.broadcasted_iota(I32, (tm, N_EXPERTS), 1)
    vals, idxs = [], []
    rest = logits
    for _ in range(TOP_K):
        m = jnp.max(rest, axis=-1, keepdims=True)
        am = jnp.min(jnp.where(rest == m, lane, N_EXPERTS), axis=-1, keepdims=True)
        vals.append(m)
        idxs.append(am)
        rest = jnp.where(lane == am, -jnp.inf, rest)
    exps = [jnp.exp(val - vals[0]) for val in vals]
    denom = exps[0] + exps[1] + exps[2] + exps[3]

    onehot = jnp.zeros((tm, N_EXPERTS), F32)
    for am in idxs:
        onehot = onehot + jnp.where(lane == am, 1.0, 0.0)
    ri = lax.broadcasted_iota(I32, (tm, tm), 0)
    ci = lax.broadcasted_iota(I32, (tm, tm), 1)
    before = jnp.where(ri > ci, 1.0, 0.0).astype(BF16)
    seen = carry_ref[...] + jnp.dot(before, onehot.astype(BF16), preferred_element_type=F32)
    lane_k = lax.broadcasted_iota(I32, (tm, TOP_K), 1)
    idx_out = jnp.zeros((tm, TOP_K), I32)
    gate_out = jnp.zeros((tm, TOP_K), F32)
    rank_out = jnp.zeros((tm, TOP_K), I32)
    for kk in range(TOP_K):
        rank = jnp.sum(jnp.where(lane == idxs[kk], seen, 0.0), axis=-1, keepdims=True).astype(I32)
        idx_out = jnp.where(lane_k == kk, idxs[kk], idx_out)
        gate_out = jnp.where(lane_k == kk, exps[kk] / denom, gate_out)
        rank_out = jnp.where(lane_k == kk, rank, rank_out)
    idx_o[0] = idx_out
    gate_o[0] = gate_out
    rank_o[0] = rank_out
    total = carry_ref[...] + jnp.sum(onehot, axis=0, keepdims=True)
    carry_ref[...] = total
    cnt_o[...] = total.astype(I32)


def _merge(z, y_gm, y_rw, y_fox, x, gate1, scale2, shift2, w_branch, w_o, w_router, b_router, layer):
    bsz, t_len, d = x.shape
    tm = min(512, t_len)
    row = lambda w: pl.BlockSpec((1, tm, w), lambda b, i: (b, i, 0))
    mod = pl.BlockSpec((1, 1, d), lambda b, i: (b, 0, 0))
    full = lambda shape: pl.BlockSpec(shape, lambda b, i: (0,) * len(shape))
    return pl.pallas_call(
        _merge_kernel,
        out_shape=(jax.ShapeDtypeStruct((bsz, t_len, d), F32), jax.ShapeDtypeStruct((bsz, t_len, d // 2), I32),
                   jax.ShapeDtypeStruct((bsz, t_len, TOP_K), I32), jax.ShapeDtypeStruct((bsz, t_len, TOP_K), F32),
                   jax.ShapeDtypeStruct((bsz, t_len, TOP_K), I32), jax.ShapeDtypeStruct((1, N_EXPERTS), I32)),
        grid=(bsz, t_len // tm),
        in_specs=[row(N_BRANCH * D_MODEL), row(GM_WIDTH),
                  pl.BlockSpec((1, RW_HEADS, tm, RW_HEAD_DIM), lambda b, i: (b, 0, i, 0)),
                  row(FOX_WIDTH), row(d), mod, mod, mod,
                  pl.BlockSpec((MIX_WIDTH, d), lambda b, i: (layer, 0)), pl.BlockSpec((d, d), lambda b, i: (layer, 0)),
                  full(w_router.shape), full((1, N_EXPERTS))],
        out_specs=(row(d), row(d // 2), row(TOP_K), row(TOP_K), row(TOP_K), full((1, N_EXPERTS))),
        scratch_shapes=[pltpu.VMEM((1, N_EXPERTS), F32)],
        compiler_params=_cparams(("arbitrary", "arbitrary")),
        name="merge_router",
    )(z, y_gm, y_rw, y_fox, x, gate1, scale2, shift2, w_branch, w_o, w_router, b_router.reshape(1, N_EXPERTS))


def _sc_mesh():
    return plsc.VectorSubcoreMesh(core_axis_name="c", subcore_axis_name="s",
                                  num_cores=SC_CORES, num_subcores=SC_SUBCORES)


def _sc_worker():
    return lax.axis_index("s") * SC_CORES + lax.axis_index("c")


def _sc_scatter_rows(src, idx3, n_out):
    _, d = src.shape
    n_copy, n_grp, _ = idx3.shape
    grp_per_w = n_grp // SC_WORKERS
    assert grp_per_w % 2 == 0

    def body(src_hbm, idx_hbm, out_hbm, idx_v, rows_a, rows_b, sem):
        g0 = _sc_worker() * grp_per_w
        for q in range(n_copy):
            pltpu.sync_copy(idx_hbm.at[q, pl.ds(g0, grp_per_w)], idx_v.at[pl.ds(q * grp_per_w, grp_per_w)])

        @pl.loop(0, grp_per_w, step=2)
        def _(j):
            read_a = pltpu.async_copy(src_hbm.at[pl.ds((g0 + j) * SC_ROWS, SC_ROWS)], rows_a, sem.at[0])
            read_b = pltpu.async_copy(src_hbm.at[pl.ds((g0 + j + 1) * SC_ROWS, SC_ROWS)], rows_b, sem.at[1])
            read_a.wait()
            put_a = [pltpu.async_copy(rows_a, out_hbm.at[idx_v.at[q * grp_per_w + j]], sem.at[2])
                     for q in range(n_copy)]
            read_b.wait()
            put_b = [pltpu.async_copy(rows_b, out_hbm.at[idx_v.at[q * grp_per_w + j + 1]], sem.at[3])
                     for q in range(n_copy)]
            for cp in put_a + put_b:
                cp.wait()

    return pl.kernel(
        body, out_type=jax.ShapeDtypeStruct((n_out, d), src.dtype), mesh=_sc_mesh(),
        scratch_types=[pltpu.VMEM((n_copy * grp_per_w, SC_ROWS), I32), pltpu.VMEM((SC_ROWS, d), src.dtype),
                       pltpu.VMEM((SC_ROWS, d), src.dtype), pltpu.SemaphoreType.DMA((4,))],
        name="sc_dispatch",
    )(src, idx3)


def _sc_gather_rows(table, idx2):
    _, d = table.shape
    n_grp, _ = idx2.shape
    grp_per_w = n_grp // SC_WORKERS
    assert grp_per_w % 2 == 0

    def body(table_hbm, idx_hbm, out_hbm, idx_v, rows_a, rows_b, sem):
        g0 = _sc_worker() * grp_per_w
        pltpu.sync_copy(idx_hbm.at[pl.ds(g0, grp_per_w)], idx_v)

        @pl.loop(0, grp_per_w, step=2)
        def _(j):
            get_a = pltpu.async_copy(table_hbm.at[idx_v.at[j]], rows_a, sem.at[0])
            get_b = pltpu.async_copy(table_hbm.at[idx_v.at[j + 1]], rows_b, sem.at[1])
            get_a.wait()
            put_a = pltpu.async_copy(rows_a, out_hbm.at[pl.ds((g0 + j) * SC_ROWS, SC_ROWS)], sem.at[2])
            get_b.wait()
            put_b = pltpu.async_copy(rows_b, out_hbm.at[pl.ds((g0 + j + 1) * SC_ROWS, SC_ROWS)], sem.at[3])
            put_a.wait()
            put_b.wait()

    return pl.kernel(
        body, out_type=jax.ShapeDtypeStruct((n_grp * SC_ROWS, d), table.dtype), mesh=_sc_mesh(),
        scratch_types=[pltpu.VMEM((grp_per_w, SC_ROWS), I32), pltpu.VMEM((SC_ROWS, d), table.dtype),
                       pltpu.VMEM((SC_ROWS, d), table.dtype), pltpu.SemaphoreType.DMA((4,))],
        name="sc_combine_gather",
    )(table, idx2)


def _ffn_weight_copies(expert, wgu_hbm, wd_hbm, stage_gu, stage_d, sem):
    return (pltpu.make_async_copy(wgu_hbm.at[expert], stage_gu, sem.at[0]),
            pltpu.make_async_copy(wd_hbm.at[expert], stage_d, sem.at[1]))


def _ffn_kernel(be_ref, first_ref, nxt_ref, nv_ref, x_ref, wgu_hbm, wd_hbm, bgu_ref, bd_ref, o_ref,
                stage_gu, stage_d, wgu_b, wd_b, sem):
    step = pl.program_id(0)
    copies = functools.partial(_ffn_weight_copies, wgu_hbm=wgu_hbm, wd_hbm=wd_hbm, stage_gu=stage_gu,
                               stage_d=stage_d, sem=sem)

    @pl.when(step == 0)
    def _():
        for cp in copies(be_ref[0]):
            cp.start()

    for b in range(FFN_STEP_BLOCKS):
        idx = step * FFN_STEP_BLOCKS + b
        expert = be_ref[idx]
        n_valid = nv_ref[idx]
        rows = slice(b * MOE_BLOCK, (b + 1) * MOE_BLOCK)

        @pl.when(first_ref[idx] == 1)
        def _():
            for cp in copies(expert):
                cp.wait()
            wgu_b[...] = stage_gu[...].astype(BF16)
            wd_b[...] = stage_d[...].astype(BF16)

            @pl.when(nxt_ref[idx] >= 0)
            def _():
                for cp in copies(nxt_ref[idx]):
                    cp.start()

        @pl.when(n_valid > 0)
        def _():
            rowid = lax.broadcasted_iota(I32, (MOE_BLOCK, x_ref.shape[1]), 0)
            xp = jnp.where(rowid < n_valid, x_ref[rows, :], 0)
            x = jnp.concatenate(_unpack_halves(xp), axis=-1).astype(BF16)
            gu = jnp.dot(x, wgu_b[...], preferred_element_type=F32) + bgu_ref[expert]
            g_ = jnp.minimum(gu[:, :D_FF], SWIGLU_LIMIT)
            u_ = jnp.clip(gu[:, D_FF:], -SWIGLU_LIMIT, SWIGLU_LIMIT)
            act = (u_ + 1.0) * (g_ * jax.nn.sigmoid(SWIGLU_ALPHA * g_))
            y = jnp.dot(act.astype(BF16), wd_b[...], preferred_element_type=F32) + bd_ref[expert]
            o_ref[rows, :] = _pack_halves(y)

        @pl.when(n_valid <= 0)
        def _():
            o_ref[rows, :] = jnp.zeros((MOE_BLOCK, o_ref.shape[1]), o_ref.dtype)


def _ffn(block_expert, block_first, block_next, block_valid, xin, w_gate_up, b_gate_up, w_down, b_down):
    n_rows, dp = xin.shape
    d = 2 * dp
    step_rows = FFN_STEP_BLOCKS * MOE_BLOCK
    resident = lambda arr: pl.BlockSpec(arr.shape, lambda i, *_: (0,) * arr.ndim)
    grid_spec = pltpu.PrefetchScalarGridSpec(
        num_scalar_prefetch=4,
        grid=(n_rows // step_rows,),
        in_specs=[pl.BlockSpec((step_rows, dp), lambda i, *_: (i, 0)),
                  pl.BlockSpec(memory_space=pl.ANY), pl.BlockSpec(memory_space=pl.ANY),
                  resident(b_gate_up), resident(b_down)],
        out_specs=pl.BlockSpec((step_rows, dp), lambda i, *_: (i, 0)),
        scratch_shapes=[pltpu.VMEM((d, 2 * D_FF), F32), pltpu.VMEM((D_FF, d), F32),
                        pltpu.VMEM((d, 2 * D_FF), BF16), pltpu.VMEM((D_FF, d), BF16),
                        pltpu.SemaphoreType.DMA((2,))],
    )
    return pl.pallas_call(
        _ffn_kernel,
        out_shape=jax.ShapeDtypeStruct((n_rows, dp), I32),
        grid_spec=grid_spec,
        compiler_params=pltpu.CompilerParams(dimension_semantics=("arbitrary",), vmem_limit_bytes=FFN_VMEM_LIMIT),
        name="expert_ffn",
    )(block_expert, block_first, block_next, block_valid, xin, w_gate_up, w_down, b_gate_up, b_down)


def _combine_kernel(x1_ref, g2_ref, gate_ref, yg_ref, o_ref):
    gate = gate_ref[0]
    y_lo = y_hi = None
    for q in range(TOP_K):
        lo, hi = _unpack_halves(yg_ref[q, 0])
        wq = gate[:, q:q + 1]
        y_lo = wq * lo if y_lo is None else y_lo + wq * lo
        y_hi = wq * hi if y_hi is None else y_hi + wq * hi
    o_ref[0] = x1_ref[0] + g2_ref[0] * jnp.concatenate([y_lo, y_hi], axis=-1)


def _combine(x1, gate2, gate, yg):
    bsz, t_len, d = x1.shape
    tm = min(1024, t_len)
    return pl.pallas_call(
        _combine_kernel,
        out_shape=jax.ShapeDtypeStruct((bsz, t_len, d), F32),
        grid=(bsz, t_len // tm),
        in_specs=[pl.BlockSpec((1, tm, d), lambda b, i: (b, i, 0)),
                  pl.BlockSpec((1, 1, d), lambda b, i: (b, 0, 0)),
                  pl.BlockSpec((1, tm, TOP_K), lambda b, i: (b, i, 0)),
                  pl.BlockSpec((TOP_K, 1, tm, d // 2), lambda b, i: (0, b, i, 0))],
        out_specs=pl.BlockSpec((1, tm, d), lambda b, i: (b, i, 0)),
        compiler_params=_cparams(("parallel", "parallel")),
        name="moe_combine",
    )(x1, gate2, gate, yg)


def _moe(x1, gate2, h2, top_idx, gate, rank, counts, w_gate_up, b_gate_up, w_down, b_down, layer):
    bsz, t_len, d = h2.shape
    n_tok = bsz * t_len
    n_assign = n_tok * TOP_K
    n_blocks = -(-n_assign // MOE_BLOCK) + N_EXPERTS
    counts = counts.reshape(N_EXPERTS)
    blocks_e = (counts + MOE_BLOCK - 1) // MOE_BLOCK
    blk_end = jnp.cumsum(blocks_e)
    blk_start = blk_end - blocks_e
    experts = jnp.arange(N_EXPERTS, dtype=I32)
    onehot = top_idx.reshape(n_tok, TOP_K, 1) == experts
    dest = jnp.sum(jnp.where(onehot, blk_start * MOE_BLOCK, 0), axis=-1) + rank.reshape(n_tok, TOP_K)
    dest_t = dest.T.astype(I32)
    blk = jnp.arange(n_blocks, dtype=I32)
    block_expert = jnp.minimum(jnp.sum(blk_end[None, :] <= blk[:, None], axis=1), N_EXPERTS - 1).astype(I32)
    be_hot = block_expert[:, None] == experts
    cnt_b = jnp.sum(jnp.where(be_hot, counts, 0), axis=1)
    start_b = jnp.sum(jnp.where(be_hot, blk_start, 0), axis=1)
    block_valid = jnp.clip(cnt_b - (blk - start_b) * MOE_BLOCK, 0, MOE_BLOCK).astype(I32)
    xin = _sc_scatter_rows(h2.reshape(n_tok, d), dest_t.reshape(TOP_K, n_tok // SC_ROWS, SC_ROWS),
                           n_blocks * MOE_BLOCK)
    block_first = jnp.concatenate([jnp.ones((1,), I32), (block_expert[1:] != block_expert[:-1]).astype(I32)])
    run_start = jnp.where(block_first == 1, blk, n_blocks)
    later_start = lax.cummin(jnp.concatenate([run_start[1:], jnp.full((1,), n_blocks, I32)]), reverse=True)
    next_expert = jnp.concatenate([block_expert, jnp.full((1,), -1 - layer * N_EXPERTS, I32)])[later_start]
    yb = _ffn(block_expert + layer * N_EXPERTS, block_first, next_expert + layer * N_EXPERTS, block_valid, xin,
              w_gate_up, b_gate_up, w_down, b_down)
    yg = _sc_gather_rows(yb, dest_t.reshape(n_assign // SC_ROWS, SC_ROWS))
    return _combine(x1, gate2, gate, yg.reshape(TOP_K, bsz, t_len, d))


def _permute_kernel(w_ref, o_ref):
    o_gm = 0
    o_rw = o_gm + 2 * GM_WIDTH
    o_fox = o_rw + RW_SHIFT_WIDTH
    o_f = o_fox + 3 * FOX_WIDTH
    o_gate = o_f + FOX_HEADS
    w = w_ref[0]
    o_ref[0, :, Z_GATE:Z_FOX] = w[:, o_gate:o_gate + N_BRANCH * D_MODEL].astype(BF16)
    o_ref[0, :, Z_FOX:Z_GM] = w[:, o_fox:o_f].astype(BF16)
    o_ref[0, :, Z_GM:Z_RW] = w[:, o_gm:o_rw].astype(BF16)
    o_ref[0, :, Z_RW:Z_F] = w[:, o_rw:o_fox].astype(BF16)
    tail = jnp.concatenate([w[:, o_f:o_gate], jnp.zeros((w.shape[0], Z_WIDTH - Z_F - FOX_HEADS), F32)], axis=-1)
    o_ref[0, :, Z_F:Z_WIDTH] = tail.astype(BF16)


def _permute_w_in(w_in):
    n_layer, d, w_cols = w_in.shape
    tr = 256
    return pl.pallas_call(
        _permute_kernel,
        out_shape=jax.ShapeDtypeStruct((n_layer, d, Z_WIDTH), BF16),
        grid=(n_layer, d // tr),
        in_specs=[pl.BlockSpec((1, tr, w_cols), lambda l, i: (l, i, 0))],
        out_specs=pl.BlockSpec((1, tr, Z_WIDTH), lambda l, i: (l, i, 0)),
        compiler_params=_cparams(("parallel", "parallel")),
        name="permute_w_in",
    )(w_in)


def _layer(x, mod, w_in_p, gm_v_gain, gm_w_s, gm_b_s, mu_pad, w_lora, rw_w0, rw_a0, rw_k_k, rw_k_a, rw_r_k,
           rw_gn_gain, rw_gn_bias, f_bias_pad, fox_q_gain, fox_k_gain, w_branch, w_o, w_router, b_router,
           w_gate_up, b_gate_up, w_down, b_down, layer):
    shift1, scale1, gate1, shift2, scale2, gate2 = (mod[:, i][:, None, :] for i in range(6))
    z, z_rw = _inproj(x, scale1, shift1, w_in_p, layer)
    y_gm = _gmlp(z, gm_v_gain, gm_w_s, gm_b_s)
    r, lw, k, v, a, b, g = _rwprep(z_rw, mu_pad, w_lora, rw_w0, rw_a0, rw_k_k, rw_k_a)
    y_rw = _rwscan(r, lw, k, v, a, b, g, rw_r_k, rw_gn_gain, rw_gn_bias)
    q, kf, vf = _foxprep(z, z_rw, f_bias_pad, fox_q_gain, fox_k_gain)
    y_fox = _fox(q, kf, vf)
    x1, h2, top_idx, gate, rank, counts = _merge(z, y_gm, y_rw, y_fox, x, gate1, scale2, shift2,
                                                 w_branch, w_o, w_router, b_router, layer)
    return _moe(x1, gate2, h2, top_idx, gate, rank, counts, w_gate_up, b_gate_up, w_down, b_down, layer)


def kernel(x, c, w_ada, b_ada, w_in, gm_v_gain, gm_w_s, gm_b_s, rw_mu, rw_w0, rw_w2, rw_a0, rw_a2, rw_g2, rw_k_k,
           rw_k_a, rw_r_k, rw_gn_gain, rw_gn_bias, fox_f_bias, fox_q_gain, fox_k_gain, w_branch, w_o, w_router,
           b_router, w_gate_up, b_gate_up, w_down, b_down):
    n_layer = w_ada.shape[0]
    bsz = x.shape[0]
    c_pad = jnp.zeros((8, D_MODEL), F32).at[:bsz].set(c)
    mod = _adaln(c_pad, w_ada, b_ada)[:, :bsz].reshape(n_layer, bsz, 6, D_MODEL)
    w_in_p = _permute_w_in(w_in)
    mu_pad = jnp.pad(rw_mu, ((0, 0), (0, RW_BLOCK - RW_SHIFT_WIDTH)))
    w_lora = jnp.zeros((n_layer, RW_LORA, 3 * RW_WIDTH), F32)
    w_lora = w_lora.at[:, 0:RW_DECAY_LORA, 0:RW_WIDTH].set(rw_w2)
    w_lora = w_lora.at[:, RW_DECAY_LORA:RW_DECAY_LORA + RW_ICLR_LORA, RW_WIDTH:2 * RW_WIDTH].set(rw_a2)
    w_lora = w_lora.at[:, RW_DECAY_LORA + RW_ICLR_LORA:, 2 * RW_WIDTH:].set(rw_g2)
    f_bias_pad = jnp.pad(fox_f_bias, ((0, 0), (0, LANES - FOX_HEADS)))
    w_in_p = w_in_p.reshape(n_layer * D_MODEL, Z_WIDTH)
    w_branch_b = w_branch.astype(BF16).reshape(n_layer * MIX_WIDTH, D_MODEL)
    w_o_b = w_o.astype(BF16).reshape(n_layer * D_MODEL, D_MODEL)
    w_gu = w_gate_up.reshape(n_layer * N_EXPERTS, D_MODEL, 2 * D_FF)
    b_gu = b_gate_up.reshape(n_layer * N_EXPERTS, 1, 2 * D_FF)
    w_dn = w_down.reshape(n_layer * N_EXPERTS, D_FF, D_MODEL)
    b_dn = b_down.reshape(n_layer * N_EXPERTS, 1, D_MODEL)
    for l in range(n_layer):
        x = _layer(x, mod[l], w_in_p, gm_v_gain[l], gm_w_s[l], gm_b_s[l], mu_pad[l:l + 1], w_lora[l], rw_w0[l],
                   rw_a0[l], rw_k_k[l], rw_k_a[l], rw_r_k[l], rw_gn_gain[l], rw_gn_bias[l], f_bias_pad[l:l + 1],
                   fox_q_gain[l], fox_k_gain[l], w_branch_b, w_o_b, w_router[l], b_router[l],
                   w_gu, b_gu, w_dn, b_dn, l)
    return x
```
